```python
import math
import jax, jax.numpy as jnp
from jax import lax
import numpy as np

D_MODEL = 1024
BATCH = 8
SEQ = 4096
DEPTH = 1

MEM_LEN = 256
HEAD_DIM = 64
SSD_HEADS = 16
SSD_WIDTH = SSD_HEADS * HEAD_DIM
SSD_GROUPS = 2
SSD_STATE = 128
SSD_CONV = 4
CHUNK = 128
XBC_WIDTH = SSD_WIDTH + 2 * SSD_GROUPS * SSD_STATE
CF_GROUPS = 16
CF_WIDTH = CF_GROUPS * HEAD_DIM
CF_CONV = 31
MIX_WIDTH = SSD_WIDTH + CF_WIDTH
Z_END = SSD_WIDTH
XBC_END = Z_END + XBC_WIDTH
DT_END = XBC_END + SSD_HEADS
IN_WIDTH = DT_END + 2 * CF_WIDTH
X_HEADS = 4
X_HEAD_DIM = D_MODEL // X_HEADS
D_FF = int(math.ceil(8 * D_MODEL / 3 / 256) * 256)
EPS = 1e-6

kernel_name = "hybrid_ssd_conformer_xattn_block"


def rmsnorm(x, g):
    xf = x.astype(jnp.float32)
    y = xf * lax.rsqrt(jnp.mean(xf * xf, axis=-1, keepdims=True) + EPS)
    return (y * g.astype(jnp.float32)).astype(x.dtype)


def layernorm(x, g, b):
    xf = x.astype(jnp.float32)
    mu = jnp.mean(xf, axis=-1, keepdims=True)
    var = jnp.mean(jnp.square(xf - mu), axis=-1, keepdims=True)
    y = (xf - mu) * lax.rsqrt(var + EPS)
    return (y * g.astype(jnp.float32) + b.astype(jnp.float32)).astype(x.dtype)


def causal_depthwise_conv(x, w, b):
    K, C = w.shape
    y = lax.conv_general_dilated(
        x, w[:, None, :].astype(x.dtype), window_strides=(1,), padding=[(K - 1, 0)],
        dimension_numbers=("NWC", "WIO", "NWC"), feature_group_count=C)
    return y + b.astype(x.dtype)


def ssd_chunked(xh, dt, A, Bg, Cg):
    Bsz, S, H, P = xh.shape
    G, N = Bg.shape[-2:]
    R = H // G
    nc = S // CHUNK
    f32 = jnp.float32
    x = xh.astype(f32).reshape(Bsz, nc, CHUNK, G, R, P)
    dtc = dt.astype(f32).reshape(Bsz, nc, CHUNK, G, R)
    Bc = Bg.astype(f32).reshape(Bsz, nc, CHUNK, G, N)
    Cc = Cg.astype(f32).reshape(Bsz, nc, CHUNK, G, N)
    dA = dtc * A.astype(f32).reshape(G, R)
    Acs = jnp.cumsum(dA, axis=2)
    seg = Acs[:, :, :, None] - Acs[:, :, None]
    causal = jnp.tril(jnp.ones((CHUNK, CHUNK), dtype=bool))[:, :, None, None]
    decay = jnp.exp(jnp.where(causal, seg, -jnp.inf))
    CB = jnp.einsum("bclgn,bcsgn->bclsg", Cc, Bc)
    scores = CB[..., None] * decay * dtc[:, :, None]
    y_diag = jnp.einsum("bclsgr,bcsgrp->bclgrp", scores, x)
    decay_to_end = jnp.exp(Acs[:, :, -1:] - Acs)
    states = jnp.einsum("bclgn,bclgr,bclgrp->bcgrpn", Bc, decay_to_end * dtc, x)
    chunk_decay = jnp.exp(Acs[:, :, -1])

    def step(carry, inp):
        st, dec = inp
        new = carry * dec[..., None, None] + st
        return new, carry

    init = jnp.zeros((Bsz, G, R, P, N), f32)
    _, prev = lax.scan(step, init, (jnp.moveaxis(states, 1, 0), jnp.moveaxis(chunk_decay, 1, 0)))
    prev = jnp.moveaxis(prev, 0, 1)
    y_off = jnp.einsum("bclgn,bcgrpn,bclgr->bclgrp", Cc, prev, jnp.exp(Acs))
    return (y_diag + y_off).reshape(Bsz, S, H, P)


def _fwd_setup_inputs(seed: int = 0) -> dict:
    key = jax.random.key(seed)
    ks = jax.random.split(key, 26)
    f32 = jnp.float32

    def nrm(k, shape, scale):
        return jax.random.normal(k, shape, f32) * scale

    def gain(k, shape):
        return 1.0 + 0.05 * jax.random.normal(k, shape, f32)

    dt0 = jnp.exp(jax.random.uniform(ks[5], (DEPTH, SSD_HEADS), f32, math.log(1e-3), math.log(1e-1)))
    return {
        "x": nrm(ks[0], (BATCH, SEQ, D_MODEL), 1.0),
        "mem": nrm(ks[1], (BATCH, MEM_LEN, D_MODEL), 1.0),
        "norm_mix_g": gain(ks[2], (DEPTH, D_MODEL)),
        "w_in": nrm(ks[3], (DEPTH, D_MODEL, IN_WIDTH), D_MODEL ** -0.5),
        "ssd_conv_w": nrm(ks[4], (DEPTH, SSD_CONV, XBC_WIDTH), SSD_CONV ** -0.5),
        "ssd_conv_b": nrm(ks[6], (DEPTH, XBC_WIDTH), 0.02),
        "ssd_dt_bias": dt0 + jnp.log(-jnp.expm1(-dt0)),
        "ssd_A_log": jnp.log(jax.random.uniform(ks[7], (DEPTH, SSD_HEADS), f32, 1.0, 16.0)),
        "ssd_D": gain(ks[8], (DEPTH, SSD_HEADS)),
        "ssd_norm_g": gain(ks[9], (DEPTH, SSD_WIDTH)),
        "cf_conv_w": nrm(ks[10], (DEPTH, CF_CONV, CF_WIDTH), CF_CONV ** -0.5),
        "cf_conv_b": nrm(ks[11], (DEPTH, CF_WIDTH), 0.02),
        "cf_ln_g": gain(ks[12], (DEPTH, CF_WIDTH)),
        "cf_ln_b": nrm(ks[13], (DEPTH, CF_WIDTH), 0.02),
        "w_out": nrm(ks[14], (DEPTH, MIX_WIDTH, D_MODEL), MIX_WIDTH ** -0.5),
        "norm_xattn_g": gain(ks[15], (DEPTH, D_MODEL)),
        "norm_mem_g": gain(ks[16], (DEPTH, D_MODEL)),
        "w_q": nrm(ks[17], (DEPTH, D_MODEL, D_MODEL), D_MODEL ** -0.5),
        "w_kv": nrm(ks[18], (DEPTH, D_MODEL, 2 * D_MODEL), D_MODEL ** -0.5),
        "w_o": nrm(ks[19], (DEPTH, D_MODEL, D_MODEL), D_MODEL ** -0.5),
        "norm_ffn_g": gain(ks[20], (DEPTH, D_MODEL)),
        "w_gate": nrm(ks[21], (DEPTH, D_MODEL, D_FF), D_MODEL ** -0.5),
        "w_up": nrm(ks[22], (DEPTH, D_MODEL, D_FF), D_MODEL ** -0.5),
        "w_down": nrm(ks[23], (DEPTH, D_FF, D_MODEL), D_FF ** -0.5),
        "norm_final_g": gain(ks[24], (D_MODEL,)),
    }


def _fwd_reference(x, mem, norm_mix_g, w_in, ssd_conv_w, ssd_conv_b, ssd_dt_bias, ssd_A_log, ssd_D,
              ssd_norm_g, cf_conv_w, cf_conv_b, cf_ln_g, cf_ln_b, w_out, norm_xattn_g, norm_mem_g,
              w_q, w_kv, w_o, norm_ffn_g, w_gate, w_up, w_down, norm_final_g):
    Bsz, S, _ = x.shape
    M = mem.shape[1]
    for i in range(DEPTH):
        h = rmsnorm(x, norm_mix_g[i])
        proj = h @ w_in[i]
        z, xbc, dt_raw, glu = jnp.split(proj, [Z_END, XBC_END, DT_END], axis=-1)

        xbc = jax.nn.silu(causal_depthwise_conv(xbc, ssd_conv_w[i], ssd_conv_b[i]))
        xs, Bm, Cm = jnp.split(xbc, [SSD_WIDTH, SSD_WIDTH + SSD_GROUPS * SSD_STATE], axis=-1)
        dt = jax.nn.softplus(dt_raw.astype(jnp.float32) + ssd_dt_bias[i].astype(jnp.float32))
        A = -jnp.exp(ssd_A_log[i].astype(jnp.float32))
        xh = xs.reshape(Bsz, S, SSD_HEADS, HEAD_DIM)
        y = ssd_chunked(xh, dt, A,
                        Bm.reshape(Bsz, S, SSD_GROUPS, SSD_STATE),
                        Cm.reshape(Bsz, S, SSD_GROUPS, SSD_STATE))
        y = y + ssd_D[i].astype(jnp.float32)[:, None] * xh.astype(jnp.float32)
        y = y.reshape(Bsz, S, SSD_WIDTH) * jax.nn.silu(z.astype(jnp.float32))
        y = rmsnorm(y.reshape(Bsz, S, SSD_GROUPS, SSD_WIDTH // SSD_GROUPS),
                    ssd_norm_g[i].reshape(SSD_GROUPS, SSD_WIDTH // SSD_GROUPS))
        y = y.reshape(Bsz, S, SSD_WIDTH).astype(x.dtype)

        a, g = jnp.split(glu, 2, axis=-1)
        u = a * jax.nn.sigmoid(g)
        u = causal_depthwise_conv(u, cf_conv_w[i], cf_conv_b[i])
        u = jax.nn.silu(layernorm(u, cf_ln_g[i], cf_ln_b[i]))

        x = x + jnp.concatenate([y, u], axis=-1) @ w_out[i]

        q = (rmsnorm(x, norm_xattn_g[i]) @ w_q[i]).reshape(Bsz, S, X_HEADS, X_HEAD_DIM)
        kv = rmsnorm(mem, norm_mem_g[i]) @ w_kv[i]
        k, v = jnp.split(kv, 2, axis=-1)
        k = k.reshape(Bsz, M, X_HEADS, X_HEAD_DIM)
        v = v.reshape(Bsz, M, X_HEADS, X_HEAD_DIM)
        s = jnp.einsum("bshd,bmhd->bhsm", q.astype(jnp.float32), k.astype(jnp.float32))
        p = jax.nn.softmax(s * (X_HEAD_DIM ** -0.5), axis=-1).astype(v.dtype)
        o = jnp.einsum("bhsm,bmhd->bshd", p, v).reshape(Bsz, S, D_MODEL)
        x = x + o @ w_o[i]

        hf = rmsnorm(x, norm_ffn_g[i])
        x = x + (jax.nn.silu(hf @ w_gate[i]) * (hf @ w_up[i])) @ w_down[i]
    return rmsnorm(x, norm_final_g)


import jax as _jax
import jax.numpy as _jnp

TWIN_FORMAT = 'train_step'
FWD_PARAMS = ['x', 'mem', 'norm_mix_g', 'w_in', 'ssd_conv_w', 'ssd_conv_b', 'ssd_dt_bias', 'ssd_A_log', 'ssd_D', 'ssd_norm_g', 'cf_conv_w', 'cf_conv_b', 'cf_ln_g', 'cf_ln_b', 'w_out', 'norm_xattn_g', 'norm_mem_g', 'w_q', 'w_kv', 'w_o', 'norm_ffn_g', 'w_gate', 'w_up', 'w_down', 'norm_final_g']
TWIN_WEIGHTS = ['norm_mix_g', 'w_in', 'ssd_conv_w', 'ssd_conv_b', 'ssd_dt_bias', 'ssd_A_log', 'ssd_D', 'ssd_norm_g', 'cf_conv_w', 'cf_conv_b', 'cf_ln_g', 'cf_ln_b', 'w_out', 'norm_xattn_g', 'norm_mem_g', 'w_q', 'w_kv', 'w_o', 'norm_ffn_g', 'w_gate', 'w_up', 'w_down', 'norm_final_g']
TWIN_DIFF_INPUT = 'x'
TWIN_INPUTS = ['x', 'mem', 'norm_mix_g', 'w_in', 'ssd_conv_w', 'ssd_conv_b', 'ssd_dt_bias', 'ssd_A_log', 'ssd_D', 'ssd_norm_g', 'cf_conv_w', 'cf_conv_b', 'cf_ln_g', 'cf_ln_b', 'w_out', 'norm_xattn_g', 'norm_mem_g', 'w_q', 'w_kv', 'w_o', 'norm_ffn_g', 'w_gate', 'w_up', 'w_down', 'norm_final_g', 'loss_target', 'm_norm_mix_g', 'm_w_in', 'm_ssd_conv_w', 'm_ssd_conv_b', 'm_ssd_dt_bias', 'm_ssd_A_log', 'm_ssd_D', 'm_ssd_norm_g', 'm_cf_conv_w', 'm_cf_conv_b', 'm_cf_ln_g', 'm_cf_ln_b', 'm_w_out', 'm_norm_xattn_g', 'm_norm_mem_g', 'm_w_q', 'm_w_kv', 'm_w_o', 'm_norm_ffn_g', 'm_w_gate', 'm_w_up', 'm_w_down', 'm_norm_final_g', 'v_norm_mix_g', 'v_w_in', 'v_ssd_conv_w', 'v_ssd_conv_b', 'v_ssd_dt_bias', 'v_ssd_A_log', 'v_ssd_D', 'v_ssd_norm_g', 'v_cf_conv_w', 'v_cf_conv_b', 'v_cf_ln_g', 'v_cf_ln_b', 'v_w_out', 'v_norm_xattn_g', 'v_norm_mem_g', 'v_w_q', 'v_w_kv', 'v_w_o', 'v_norm_ffn_g', 'v_w_gate', 'v_w_up', 'v_w_down', 'v_norm_final_g']
TWIN_OUTPUTS = ['loss', 'grad_x', 'grad_norm_mix_g', 'grad_w_in', 'grad_ssd_conv_w', 'grad_ssd_conv_b', 'grad_ssd_dt_bias', 'grad_ssd_A_log', 'grad_ssd_D', 'grad_ssd_norm_g', 'grad_cf_conv_w', 'grad_cf_conv_b', 'grad_cf_ln_g', 'grad_cf_ln_b', 'grad_w_out', 'grad_norm_xattn_g', 'grad_norm_mem_g', 'grad_w_q', 'grad_w_kv', 'grad_w_o', 'grad_norm_ffn_g', 'grad_w_gate', 'grad_w_up', 'grad_w_down', 'grad_norm_final_g', 'delta_norm_mix_g', 'delta_w_in', 'delta_ssd_conv_w', 'delta_ssd_conv_b', 'delta_ssd_dt_bias', 'delta_ssd_A_log', 'delta_ssd_D', 'delta_ssd_norm_g', 'delta_cf_conv_w', 'delta_cf_conv_b', 'delta_cf_ln_g', 'delta_cf_ln_b', 'delta_w_out', 'delta_norm_xattn_g', 'delta_norm_mem_g', 'delta_w_q', 'delta_w_kv', 'delta_w_o', 'delta_norm_ffn_g', 'delta_w_gate', 'delta_w_up', 'delta_w_down', 'delta_norm_final_g', 'new_m_norm_mix_g', 'new_m_w_in', 'new_m_ssd_conv_w', 'new_m_ssd_conv_b', 'new_m_ssd_dt_bias', 'new_m_ssd_A_log', 'new_m_ssd_D', 'new_m_ssd_norm_g', 'new_m_cf_conv_w', 'new_m_cf_conv_b', 'new_m_cf_ln_g', 'new_m_cf_ln_b', 'new_m_w_out', 'new_m_norm_xattn_g', 'new_m_norm_mem_g', 'new_m_w_q', 'new_m_w_kv', 'new_m_w_o', 'new_m_norm_ffn_g', 'new_m_w_gate', 'new_m_w_up', 'new_m_w_down', 'new_m_norm_final_g', 'new_v_norm_mix_g', 'new_v_w_in', 'new_v_ssd_conv_w', 'new_v_ssd_conv_b', 'new_v_ssd_dt_bias', 'new_v_ssd_A_log', 'new_v_ssd_D', 'new_v_ssd_norm_g', 'new_v_cf_conv_w', 'new_v_cf_conv_b', 'new_v_cf_ln_g', 'new_v_cf_ln_b', 'new_v_w_out', 'new_v_norm_xattn_g', 'new_v_norm_mem_g', 'new_v_w_q', 'new_v_w_kv', 'new_v_w_o', 'new_v_norm_ffn_g', 'new_v_w_gate', 'new_v_w_up', 'new_v_w_down', 'new_v_norm_final_g']
TWIN_LEAF_KINDS = {'loss': 'loss', 'grad_x': 'grad_x', 'grad_norm_mix_g': 'grad_w', 'grad_w_in': 'grad_w', 'grad_ssd_conv_w': 'grad_w', 'grad_ssd_conv_b': 'grad_w', 'grad_ssd_dt_bias': 'grad_w', 'grad_ssd_A_log': 'grad_w', 'grad_ssd_D': 'grad_w', 'grad_ssd_norm_g': 'grad_w', 'grad_cf_conv_w': 'grad_w', 'grad_cf_conv_b': 'grad_w', 'grad_cf_ln_g': 'grad_w', 'grad_cf_ln_b': 'grad_w', 'grad_w_out': 'grad_w', 'grad_norm_xattn_g': 'grad_w', 'grad_norm_mem_g': 'grad_w', 'grad_w_q': 'grad_w', 'grad_w_kv': 'grad_w', 'grad_w_o': 'grad_w', 'grad_norm_ffn_g': 'grad_w', 'grad_w_gate': 'grad_w', 'grad_w_up': 'grad_w', 'grad_w_down': 'grad_w', 'grad_norm_final_g': 'grad_w', 'delta_norm_mix_g': 'delta_w', 'delta_w_in': 'delta_w', 'delta_ssd_conv_w': 'delta_w', 'delta_ssd_conv_b': 'delta_w', 'delta_ssd_dt_bias': 'delta_w', 'delta_ssd_A_log': 'delta_w', 'delta_ssd_D': 'delta_w', 'delta_ssd_norm_g': 'delta_w', 'delta_cf_conv_w': 'delta_w', 'delta_cf_conv_b': 'delta_w', 'delta_cf_ln_g': 'delta_w', 'delta_cf_ln_b': 'delta_w', 'delta_w_out': 'delta_w', 'delta_norm_xattn_g': 'delta_w', 'delta_norm_mem_g': 'delta_w', 'delta_w_q': 'delta_w', 'delta_w_kv': 'delta_w', 'delta_w_o': 'delta_w', 'delta_norm_ffn_g': 'delta_w', 'delta_w_gate': 'delta_w', 'delta_w_up': 'delta_w', 'delta_w_down': 'delta_w', 'delta_norm_final_g': 'delta_w', 'new_m_norm_mix_g': 'new_m', 'new_m_w_in': 'new_m', 'new_m_ssd_conv_w': 'new_m', 'new_m_ssd_conv_b': 'new_m', 'new_m_ssd_dt_bias': 'new_m', 'new_m_ssd_A_log': 'new_m', 'new_m_ssd_D': 'new_m', 'new_m_ssd_norm_g': 'new_m', 'new_m_cf_conv_w': 'new_m', 'new_m_cf_conv_b': 'new_m', 'new_m_cf_ln_g': 'new_m', 'new_m_cf_ln_b': 'new_m', 'new_m_w_out': 'new_m', 'new_m_norm_xattn_g': 'new_m', 'new_m_norm_mem_g': 'new_m', 'new_m_w_q': 'new_m', 'new_m_w_kv': 'new_m', 'new_m_w_o': 'new_m', 'new_m_norm_ffn_g': 'new_m', 'new_m_w_gate': 'new_m', 'new_m_w_up': 'new_m', 'new_m_w_down': 'new_m', 'new_m_norm_final_g': 'new_m', 'new_v_norm_mix_g': 'new_v', 'new_v_w_in': 'new_v', 'new_v_ssd_conv_w': 'new_v', 'new_v_ssd_conv_b': 'new_v', 'new_v_ssd_dt_bias': 'new_v', 'new_v_ssd_A_log': 'new_v', 'new_v_ssd_D': 'new_v', 'new_v_ssd_norm_g': 'new_v', 'new_v_cf_conv_w': 'new_v', 'new_v_cf_conv_b': 'new_v', 'new_v_cf_ln_g': 'new_v', 'new_v_cf_ln_b': 'new_v', 'new_v_w_out': 'new_v', 'new_v_norm_xattn_g': 'new_v', 'new_v_norm_mem_g': 'new_v', 'new_v_w_q': 'new_v', 'new_v_w_kv': 'new_v', 'new_v_w_o': 'new_v', 'new_v_norm_ffn_g': 'new_v', 'new_v_w_gate': 'new_v', 'new_v_w_up': 'new_v', 'new_v_w_down': 'new_v', 'new_v_norm_final_g': 'new_v'}


def _forward(args):
    return _fwd_reference(*[args[k] for k in FWD_PARAMS])


def _output_shape():
    def fwd():
        inp = _fwd_setup_inputs(0)
        return _fwd_reference(*[inp[k] for k in FWD_PARAMS])
    out = _jax.eval_shape(fwd)
    return out.shape, out.dtype

N_MICROBATCH = 1
ADAM_LR = 0.001
ADAM_B1 = 0.9
ADAM_B2 = 0.999
ADAM_EPS = 1e-08
ADAM_WD = 0.01
ADAM_STEP = 10
PER_EXAMPLE_BATCH_AXIS = {'x': 0, 'mem': 0, 'loss_target': 0}
SHARED_INPUTS = []
_WEIGHT_DTYPES = {'norm_mix_g': _jnp.float32, 'w_in': _jnp.float32, 'ssd_conv_w': _jnp.float32, 'ssd_conv_b': _jnp.float32, 'ssd_dt_bias': _jnp.float32, 'ssd_A_log': _jnp.float32, 'ssd_D': _jnp.float32, 'ssd_norm_g': _jnp.float32, 'cf_conv_w': _jnp.float32, 'cf_conv_b': _jnp.float32, 'cf_ln_g': _jnp.float32, 'cf_ln_b': _jnp.float32, 'w_out': _jnp.float32, 'norm_xattn_g': _jnp.float32, 'norm_mem_g': _jnp.float32, 'w_q': _jnp.float32, 'w_kv': _jnp.float32, 'w_o': _jnp.float32, 'norm_ffn_g': _jnp.float32, 'w_gate': _jnp.float32, 'w_up': _jnp.float32, 'w_down': _jnp.float32, 'norm_final_g': _jnp.float32}
MOMENT_SCALE = {'norm_mix_g': 1.779580e-01, 'w_in': 8.313208e-02, 'ssd_conv_w': 9.607603e-02, 'ssd_conv_b': 1.384912e-01, 'ssd_dt_bias': 3.286704e-01, 'ssd_A_log': 4.241261e-01, 'ssd_D': 6.103214e-01, 'ssd_norm_g': 1.154638e-01, 'cf_conv_w': 6.665390e-02, 'cf_conv_b': 2.056171e-01, 'cf_ln_g': 1.066698e-01, 'cf_ln_b': 1.166025e-01, 'w_out': 1.352088e-01, 'norm_xattn_g': 1.602013e-02, 'norm_mem_g': 2.410397e-02, 'w_q': 1.589443e-02, 'w_kv': 1.635196e-02, 'w_o': 1.695956e-02, 'norm_ffn_g': 1.093348e-01, 'w_gate': 4.677319e-02, 'w_up': 4.582935e-02, 'w_down': 7.641748e-02, 'norm_final_g': 3.208578e+01}


def _to_microbatches(a, axis):
    t = _jnp.moveaxis(a, axis, 0)
    t = t.reshape((N_MICROBATCH, t.shape[0] // N_MICROBATCH) + t.shape[1:])
    return _jnp.moveaxis(t, 1, axis + 1)


def setup_inputs(seed: int = 0) -> dict:
    inp = _fwd_setup_inputs(seed)
    key = _jax.random.fold_in(_jax.random.key(seed), 7919)
    shape, _ = _output_shape()
    out = dict(inp)
    out["loss_target"] = _jax.random.normal(_jax.random.fold_in(key, 0), shape, _jnp.float32)
    for i, name in enumerate(TWIN_WEIGHTS):
        w = inp[name].astype(_jnp.float32)
        if MOMENT_SCALE is None:
            s = _jnp.sqrt(_jnp.mean(_jnp.square(w)) + 1e-30)
        else:
            s = MOMENT_SCALE[name]
        km, kv = _jax.random.split(_jax.random.fold_in(key, i + 1))
        out[name] = w
        out["m_" + name] = s * _jax.random.normal(km, w.shape, _jnp.float32)
        out["v_" + name] = (s * s) * _jax.random.uniform(kv, w.shape, _jnp.float32, 0.5, 1.5)
    if N_MICROBATCH > 1:
        for name, axis in PER_EXAMPLE_BATCH_AXIS.items():
            out[name] = _to_microbatches(out[name], axis)
    return {'x': out['x'], 'mem': out['mem'], 'norm_mix_g': out['norm_mix_g'], 'w_in': out['w_in'], 'ssd_conv_w': out['ssd_conv_w'], 'ssd_conv_b': out['ssd_conv_b'], 'ssd_dt_bias': out['ssd_dt_bias'], 'ssd_A_log': out['ssd_A_log'], 'ssd_D': out['ssd_D'], 'ssd_norm_g': out['ssd_norm_g'], 'cf_conv_w': out['cf_conv_w'], 'cf_conv_b': out['cf_conv_b'], 'cf_ln_g': out['cf_ln_g'], 'cf_ln_b': out['cf_ln_b'], 'w_out': out['w_out'], 'norm_xattn_g': out['norm_xattn_g'], 'norm_mem_g': out['norm_mem_g'], 'w_q': out['w_q'], 'w_kv': out['w_kv'], 'w_o': out['w_o'], 'norm_ffn_g': out['norm_ffn_g'], 'w_gate': out['w_gate'], 'w_up': out['w_up'], 'w_down': out['w_down'], 'norm_final_g': out['norm_final_g'], 'loss_target': out['loss_target'], 'm_norm_mix_g': out['m_norm_mix_g'], 'm_w_in': out['m_w_in'], 'm_ssd_conv_w': out['m_ssd_conv_w'], 'm_ssd_conv_b': out['m_ssd_conv_b'], 'm_ssd_dt_bias': out['m_ssd_dt_bias'], 'm_ssd_A_log': out['m_ssd_A_log'], 'm_ssd_D': out['m_ssd_D'], 'm_ssd_norm_g': out['m_ssd_norm_g'], 'm_cf_conv_w': out['m_cf_conv_w'], 'm_cf_conv_b': out['m_cf_conv_b'], 'm_cf_ln_g': out['m_cf_ln_g'], 'm_cf_ln_b': out['m_cf_ln_b'], 'm_w_out': out['m_w_out'], 'm_norm_xattn_g': out['m_norm_xattn_g'], 'm_norm_mem_g': out['m_norm_mem_g'], 'm_w_q': out['m_w_q'], 'm_w_kv': out['m_w_kv'], 'm_w_o': out['m_w_o'], 'm_norm_ffn_g': out['m_norm_ffn_g'], 'm_w_gate': out['m_w_gate'], 'm_w_up': out['m_w_up'], 'm_w_down': out['m_w_down'], 'm_norm_final_g': out['m_norm_final_g'], 'v_norm_mix_g': out['v_norm_mix_g'], 'v_w_in': out['v_w_in'], 'v_ssd_conv_w': out['v_ssd_conv_w'], 'v_ssd_conv_b': out['v_ssd_conv_b'], 'v_ssd_dt_bias': out['v_ssd_dt_bias'], 'v_ssd_A_log': out['v_ssd_A_log'], 'v_ssd_D': out['v_ssd_D'], 'v_ssd_norm_g': out['v_ssd_norm_g'], 'v_cf_conv_w': out['v_cf_conv_w'], 'v_cf_conv_b': out['v_cf_conv_b'], 'v_cf_ln_g': out['v_cf_ln_g'], 'v_cf_ln_b': out['v_cf_ln_b'], 'v_w_out': out['v_w_out'], 'v_norm_xattn_g': out['v_norm_xattn_g'], 'v_norm_mem_g': out['v_norm_mem_g'], 'v_w_q': out['v_w_q'], 'v_w_kv': out['v_w_kv'], 'v_w_o': out['v_w_o'], 'v_norm_ffn_g': out['v_norm_ffn_g'], 'v_w_gate': out['v_w_gate'], 'v_w_up': out['v_w_up'], 'v_w_down': out['v_w_down'], 'v_norm_final_g': out['v_norm_final_g']}


def _loss(weights, diff, rest, loss_target):
    with _jax.named_scope("forward"):
        args = {**rest, TWIN_DIFF_INPUT: diff, **{k: w.astype(_WEIGHT_DTYPES[k]) for k, w in weights.items()}}
        y = _forward(args)
    with _jax.named_scope("loss_head"):
        err = _jnp.square(y.astype(_jnp.float32) - loss_target)
        return 0.5 * _jnp.sum(_jnp.mean(err, axis=-1)) if err.ndim else 0.5 * err


def _adamw(w, g, m, v):
    m = ADAM_B1 * m + (1.0 - ADAM_B1) * g
    v = ADAM_B2 * v + (1.0 - ADAM_B2) * _jnp.square(g)
    m_hat = m / (1.0 - ADAM_B1 ** ADAM_STEP)
    v_hat = v / (1.0 - ADAM_B2 ** ADAM_STEP)
    delta = -ADAM_LR * (m_hat / (_jnp.sqrt(v_hat) + ADAM_EPS) + ADAM_WD * w)
    return delta, m, v


def reference(x, mem, norm_mix_g, w_in, ssd_conv_w, ssd_conv_b, ssd_dt_bias, ssd_A_log, ssd_D, ssd_norm_g, cf_conv_w, cf_conv_b, cf_ln_g, cf_ln_b, w_out, norm_xattn_g, norm_mem_g, w_q, w_kv, w_o, norm_ffn_g, w_gate, w_up, w_down, norm_final_g, loss_target, m_norm_mix_g, m_w_in, m_ssd_conv_w, m_ssd_conv_b, m_ssd_dt_bias, m_ssd_A_log, m_ssd_D, m_ssd_norm_g, m_cf_conv_w, m_cf_conv_b, m_cf_ln_g, m_cf_ln_b, m_w_out, m_norm_xattn_g, m_norm_mem_g, m_w_q, m_w_kv, m_w_o, m_norm_ffn_g, m_w_gate, m_w_up, m_w_down, m_norm_final_g, v_norm_mix_g, v_w_in, v_ssd_conv_w, v_ssd_conv_b, v_ssd_dt_bias, v_ssd_A_log, v_ssd_D, v_ssd_norm_g, v_cf_conv_w, v_cf_conv_b, v_cf_ln_g, v_cf_ln_b, v_w_out, v_norm_xattn_g, v_norm_mem_g, v_w_q, v_w_kv, v_w_o, v_norm_ffn_g, v_w_gate, v_w_up, v_w_down, v_norm_final_g):
    given = dict(x=x, mem=mem, norm_mix_g=norm_mix_g, w_in=w_in, ssd_conv_w=ssd_conv_w, ssd_conv_b=ssd_conv_b, ssd_dt_bias=ssd_dt_bias, ssd_A_log=ssd_A_log, ssd_D=ssd_D, ssd_norm_g=ssd_norm_g, cf_conv_w=cf_conv_w, cf_conv_b=cf_conv_b, cf_ln_g=cf_ln_g, cf_ln_b=cf_ln_b, w_out=w_out, norm_xattn_g=norm_xattn_g, norm_mem_g=norm_mem_g, w_q=w_q, w_kv=w_kv, w_o=w_o, norm_ffn_g=norm_ffn_g, w_gate=w_gate, w_up=w_up, w_down=w_down, norm_final_g=norm_final_g, loss_target=loss_target, m_norm_mix_g=m_norm_mix_g, m_w_in=m_w_in, m_ssd_conv_w=m_ssd_conv_w, m_ssd_conv_b=m_ssd_conv_b, m_ssd_dt_bias=m_ssd_dt_bias, m_ssd_A_log=m_ssd_A_log, m_ssd_D=m_ssd_D, m_ssd_norm_g=m_ssd_norm_g, m_cf_conv_w=m_cf_conv_w, m_cf_conv_b=m_cf_conv_b, m_cf_ln_g=m_cf_ln_g, m_cf_ln_b=m_cf_ln_b, m_w_out=m_w_out, m_norm_xattn_g=m_norm_xattn_g, m_norm_mem_g=m_norm_mem_g, m_w_q=m_w_q, m_w_kv=m_w_kv, m_w_o=m_w_o, m_norm_ffn_g=m_norm_ffn_g, m_w_gate=m_w_gate, m_w_up=m_w_up, m_w_down=m_w_down, m_norm_final_g=m_norm_final_g, v_norm_mix_g=v_norm_mix_g, v_w_in=v_w_in, v_ssd_conv_w=v_ssd_conv_w, v_ssd_conv_b=v_ssd_conv_b, v_ssd_dt_bias=v_ssd_dt_bias, v_ssd_A_log=v_ssd_A_log, v_ssd_D=v_ssd_D, v_ssd_norm_g=v_ssd_norm_g, v_cf_conv_w=v_cf_conv_w, v_cf_conv_b=v_cf_conv_b, v_cf_ln_g=v_cf_ln_g, v_cf_ln_b=v_cf_ln_b, v_w_out=v_w_out, v_norm_xattn_g=v_norm_xattn_g, v_norm_mem_g=v_norm_mem_g, v_w_q=v_w_q, v_w_kv=v_w_kv, v_w_o=v_w_o, v_norm_ffn_g=v_norm_ffn_g, v_w_gate=v_w_gate, v_w_up=v_w_up, v_w_down=v_w_down, v_norm_final_g=v_norm_final_g)
    weights = {n: given[n] for n in TWIN_WEIGHTS}
    shared = {n: given[n] for n in SHARED_INPUTS}
    per_example = {n: given[n] for n in ['x', 'mem']}
    grad_fn = _jax.value_and_grad(_loss, argnums=(0, 1))

    def one_microbatch(ex, loss_target):
        ex = dict(ex)
        diff = ex.pop(TWIN_DIFF_INPUT)
        return grad_fn(weights, diff, {**shared, **ex}, loss_target)

    if N_MICROBATCH == 1:
        loss, (grad_w, grad_x) = one_microbatch(per_example, given["loss_target"])
    else:
        def body(carry, xs):
            loss_sum, grad_sum = carry
            l_k, (gw_k, gx_k) = one_microbatch(xs[0], xs[1])
            with _jax.named_scope("update"):
                return (loss_sum + l_k, _jax.tree.map(_jnp.add, grad_sum, gw_k)), gx_k

        init = (_jnp.zeros((), _jnp.float32), _jax.tree.map(_jnp.zeros_like, weights))
        (loss, grad_w), grad_x = _jax.lax.scan(body, init, (per_example, given["loss_target"]))
    with _jax.named_scope("update"):
        delta_w, new_m, new_v = {}, {}, {}
        for n in TWIN_WEIGHTS:
            delta_w[n], new_m[n], new_v[n] = _adamw(weights[n], grad_w[n], given["m_" + n], given["v_" + n])
    return (loss, grad_x, *[grad_w[n] for n in TWIN_WEIGHTS], *[delta_w[n] for n in TWIN_WEIGHTS],
            *[new_m[n] for n in TWIN_WEIGHTS], *[new_v[n] for n in TWIN_WEIGHTS])
```

```python
import functools
import math

import jax
import jax.numpy as jnp
from jax import lax
from jax.experimental import pallas as pl
from jax.experimental.pallas import tpu as pltpu

F32 = jnp.float32
BF16 = jnp.bfloat16
_MXU = BF16
_HI = lax.Precision.HIGHEST

D = 1024
MEM = 256
NH, HP, NG, NS = 16, 64, 2, 128
GW = NH * HP // NG
CH = 128
XBC = NH * HP + 2 * NG * NS
KS, KC = 4, 31
XH, XD = 4, 256
DFF = 2816
FB = 256
EPS = 1e-6
COL_Z, COL_A, COL_G, COL_XBC, MAINW = 0, 1024, 2048, 3072, 4608
VMEM_LIMIT = 56 * 2 ** 20

ADAM_LR, ADAM_B1, ADAM_B2, ADAM_EPS, ADAM_WD, ADAM_STEP = 0.001, 0.9, 0.999, 1e-08, 0.01, 10

SDS = jax.ShapeDtypeStruct
MESHID = pl.DeviceIdType.MESH


def _cp(*sem):
    return pltpu.CompilerParams(dimension_semantics=sem, vmem_limit_bytes=VMEM_LIMIT)


def _tile(n, cap, unit=128):
    if n <= cap:
        return n
    best = None
    for t in range(unit, cap + 1, unit):
        if n % t == 0:
            best = t
    assert best is not None, (n, cap)
    return best


def _sigmoid(x):
    return 1.0 / (1.0 + jnp.exp(-x))


def _silu(x):
    return x * _sigmoid(x)


def _dsilu(x):
    s = _sigmoid(x)
    return s * (1.0 + x * (1.0 - s))


def _softplus(x):
    return jnp.maximum(x, 0.0) + jnp.log(1.0 + jnp.exp(-jnp.abs(x)))


def _dot(a, b, dims=None, hi=False):
    dn = {None: (((1,), (0,)), ((), ())), "nt": (((1,), (1,)), ((), ())), "tn": (((0,), (0,)), ((), ()))}[dims]
    if hi:
        return lax.dot_general(a.astype(F32), b.astype(F32), dn, preferred_element_type=F32, precision=_HI)
    return lax.dot_general(a.astype(_MXU), b.astype(_MXU), dn, preferred_element_type=F32)


def _mm_nn(a, b, name, add=None, out_dtype=F32, tm_cap=1024, tn_cap=1408):
    M, K = a.shape
    _, N = b.shape
    tm, tn = _tile(M, tm_cap, 8), _tile(N, tn_cap)

    def body(a_ref, b_ref, *rest):
        o_ref = rest[-1]
        acc = _dot(a_ref[...], b_ref[...])
        if add is not None:
            acc = acc + rest[0][...]
        o_ref[...] = acc.astype(o_ref.dtype)

    in_specs = [pl.BlockSpec((tm, K), lambda j, i: (i, 0)), pl.BlockSpec((K, tn), lambda j, i: (0, j))]
    args = [a, b]
    if add is not None:
        in_specs.append(pl.BlockSpec((tm, tn), lambda j, i: (i, j)))
        args.append(add)
    return pl.pallas_call(
        body, name=name, grid=(N // tn, M // tm), in_specs=in_specs,
        out_specs=pl.BlockSpec((tm, tn), lambda j, i: (i, j)), out_shape=SDS((M, N), out_dtype),
        compiler_params=_cp("parallel", "parallel"))(*args)


def _mm_nt(a, b, name, add=None, out_dtype=F32, tm_cap=512, tk_cap=1024):
    M, N = a.shape
    K, _ = b.shape
    tm, tk = _tile(M, tm_cap, 8), _tile(K, tk_cap)

    def body(a_ref, b_ref, *rest):
        o_ref = rest[-1]
        acc = _dot(a_ref[...], b_ref[...], "nt")
        if add is not None:
            acc = acc + rest[0][...]
        o_ref[...] = acc.astype(o_ref.dtype)

    in_specs = [pl.BlockSpec((tm, N), lambda j, i: (i, 0)), pl.BlockSpec((tk, N), lambda j, i: (j, 0))]
    args = [a, b]
    if add is not None:
        in_specs.append(pl.BlockSpec((tm, tk), lambda j, i: (i, j)))
        args.append(add)
    return pl.pallas_call(
        body, name=name, grid=(K // tk, M // tm), in_specs=in_specs,
        out_specs=pl.BlockSpec((tm, tk), lambda j, i: (i, j)), out_shape=SDS((M, K), out_dtype),
        compiler_params=_cp("parallel", "parallel"))(*args)


def _mm_tn(a, b, name, tm_cap=1024, tk_cap=512, tn_cap=1408):
    M, K = a.shape
    _, N = b.shape
    tm, tk, tn = _tile(M, tm_cap, 8), _tile(K, tk_cap), _tile(N, tn_cap)

    def body(a_ref, b_ref, o_ref):
        @pl.when(pl.program_id(2) == 0)
        def _():
            o_ref[...] = jnp.zeros_like(o_ref)

        o_ref[...] += _dot(a_ref[...], b_ref[...], "tn")

    return pl.pallas_call(
        body, name=name, grid=(K // tk, N // tn, M // tm),
        in_specs=[pl.BlockSpec((tm, tk), lambda k, n, m: (m, k)), pl.BlockSpec((tm, tn), lambda k, n, m: (m, n))],
        out_specs=pl.BlockSpec((tk, tn), lambda k, n, m: (k, n)), out_shape=SDS((K, N), F32),
        compiler_params=_cp("parallel", "parallel", "arbitrary"))(a, b)


def _rms_fwd(x, g, name, tb_cap=512):
    S, Dm = x.shape
    tb = _tile(S, tb_cap, 8)

    def body(x_ref, g_ref, o_ref):
        xv = x_ref[...]
        r = lax.rsqrt(jnp.mean(xv * xv, axis=-1, keepdims=True) + EPS)
        o_ref[...] = (xv * r * g_ref[...]).astype(o_ref.dtype)

    return pl.pallas_call(
        body, name=name, grid=(S // tb,),
        in_specs=[pl.BlockSpec((tb, Dm), lambda i: (i, 0)), pl.BlockSpec((1, Dm), lambda i: (0, 0))],
        out_specs=pl.BlockSpec((tb, Dm), lambda i: (i, 0)), out_shape=SDS((S, Dm), _MXU),
        compiler_params=_cp("parallel"))(x, g)


def _rms_bwd(x, g, dh, dres, name, tb_cap=512):
    S, Dm = x.shape
    tb = _tile(S, tb_cap, 8)
    need_dx = dres is not None

    def body(x_ref, g_ref, dh_ref, *rest):
        dg_ref = rest[-1]
        xv = x_ref[...]
        r = lax.rsqrt(jnp.mean(xv * xv, axis=-1, keepdims=True) + EPS)
        xh = xv * r
        dy = dh_ref[...].astype(F32)

        @pl.when(pl.program_id(0) == 0)
        def _():
            dg_ref[...] = jnp.zeros_like(dg_ref)

        dg_ref[...] += jnp.sum(dy * xh, axis=0, keepdims=True)
        if need_dx:
            gdy = dy * g_ref[...]
            dx = r * (gdy - xh * jnp.mean(xh * gdy, axis=-1, keepdims=True))
            rest[1][...] = rest[0][...] + dx

    row = pl.BlockSpec((tb, Dm), lambda i: (i, 0))
    vec = pl.BlockSpec((1, Dm), lambda i: (0, 0))
    if need_dx:
        return pl.pallas_call(
            body, name=name, grid=(S // tb,), in_specs=[row, vec, row, row], out_specs=[row, vec],
            out_shape=[SDS((S, Dm), F32), SDS((1, Dm), F32)], compiler_params=_cp("arbitrary"))(x, g, dh, dres)
    return pl.pallas_call(
        body, name=name, grid=(S // tb,), in_specs=[row, vec, row], out_specs=vec,
        out_shape=SDS((1, Dm), F32), compiler_params=_cp("arbitrary"))(x, g, dh)


def _final_loss(x, g, tgt, name="final_loss", tb_cap=512):
    S, Dm = x.shape
    tb = _tile(S, tb_cap, 8)

    def body(x_ref, g_ref, t_ref, loss_ref, dx_ref, dg_ref):
        xv = x_ref[...]
        gv = g_ref[...]
        r = lax.rsqrt(jnp.mean(xv * xv, axis=-1, keepdims=True) + EPS)
        xh = xv * r
        e = xh * gv - t_ref[...]

        @pl.when(pl.program_id(0) == 0)
        def _():
            loss_ref[...] = jnp.zeros_like(loss_ref)
            dg_ref[...] = jnp.zeros_like(dg_ref)

        loss_ref[...] += 0.5 * jnp.sum(jnp.mean(e * e, axis=-1, keepdims=True))
        dy = e * (1.0 / Dm)
        dg_ref[...] += jnp.sum(dy * xh, axis=0, keepdims=True)
        gdy = dy * gv
        dx_ref[...] = r * (gdy - xh * jnp.mean(xh * gdy, axis=-1, keepdims=True))

    row = pl.BlockSpec((tb, Dm), lambda i: (i, 0))
    vec = pl.BlockSpec((1, Dm), lambda i: (0, 0))
    return pl.pallas_call(
        body, name=name, grid=(S // tb,), in_specs=[row, vec, row],
        out_specs=[pl.BlockSpec((1, 128), lambda i: (0, 0)), row, vec],
        out_shape=[SDS((1, 128), F32), SDS((S, Dm), F32), SDS((1, Dm), F32)],
        compiler_params=_cp("arbitrary"))(x, g, tgt)


SSD_HALO = 8
CF_HALO = 32
CONV_CB = 512


def _ssd_conv_fwd(proj, w, b, name="ssd_conv_fwd", tb_cap=512):
    S = proj.shape[0]
    tb = _tile(S, tb_cap, 8)
    nb = S // tb
    c0 = COL_XBC // CONV_CB

    def body(x_ref, w_ref, b_ref, o_ref, ext):
        @pl.when(pl.program_id(1) == 0)
        def _():
            ext[pl.ds(0, SSD_HALO), :] = jnp.zeros((SSD_HALO, CONV_CB), F32)

        ext[pl.ds(SSD_HALO, tb), :] = x_ref[...]
        acc = jnp.zeros((tb, CONV_CB), F32) + b_ref[...]
        for k in range(KS):
            acc = acc + ext[pl.ds(SSD_HALO - (KS - 1) + k, tb), :] * w_ref[k:k + 1, :]
        o_ref[...] = acc
        ext[pl.ds(0, SSD_HALO), :] = ext[pl.ds(tb, SSD_HALO), :]

    return pl.pallas_call(
        body, name=name, grid=(XBC // CONV_CB, nb),
        in_specs=[pl.BlockSpec((tb, CONV_CB), lambda j, i: (i, c0 + j)),
                  pl.BlockSpec((8, CONV_CB), lambda j, i: (0, j)),
                  pl.BlockSpec((1, CONV_CB), lambda j, i: (0, j))],
        out_specs=pl.BlockSpec((tb, CONV_CB), lambda j, i: (i, j)), out_shape=SDS((S, XBC), F32),
        scratch_shapes=[pltpu.VMEM((SSD_HALO + tb, CONV_CB), F32)],
        compiler_params=_cp("parallel", "arbitrary"))(proj, w, b)


def _ssd_conv_bwd(dxbc, proj, w, name="ssd_conv_bwd", tb_cap=512):
    S = proj.shape[0]
    tb = _tile(S, tb_cap, 8)
    nb = S // tb
    c0 = COL_XBC // CONV_CB

    def body(dy_ref, x_ref, w_ref, dx_ref, dw_ref, db_ref, ext):
        @pl.when(pl.program_id(1) == 0)
        def _():
            ext[pl.ds(tb, SSD_HALO), :] = jnp.zeros((SSD_HALO, CONV_CB), F32)
            dw_ref[...] = jnp.zeros_like(dw_ref)
            db_ref[...] = jnp.zeros_like(db_ref)

        dy = dy_ref[...]
        ext[pl.ds(0, tb), :] = dy
        xv = x_ref[...]
        acc = jnp.zeros((tb, CONV_CB), F32)
        for k in range(KS):
            sh = ext[pl.ds(KS - 1 - k, tb), :]
            acc = acc + sh * w_ref[k:k + 1, :]
            dw_ref[k:k + 1, :] += jnp.sum(xv * sh, axis=0, keepdims=True)
        db_ref[...] += jnp.sum(dy, axis=0, keepdims=True)
        dx_ref[...] = acc.astype(dx_ref.dtype)
        ext[pl.ds(tb, SSD_HALO), :] = ext[pl.ds(0, SSD_HALO), :]

    return pl.pallas_call(
        body, name=name, grid=(XBC // CONV_CB, nb),
        in_specs=[pl.BlockSpec((tb, CONV_CB), lambda j, i: (nb - 1 - i, j)),
                  pl.BlockSpec((tb, CONV_CB), lambda j, i: (nb - 1 - i, c0 + j)),
                  pl.BlockSpec((8, CONV_CB), lambda j, i: (0, j))],
        out_specs=[pl.BlockSpec((tb, CONV_CB), lambda j, i: (nb - 1 - i, j)),
                   pl.BlockSpec((8, CONV_CB), lambda j, i: (0, j)),
                   pl.BlockSpec((1, CONV_CB), lambda j, i: (0, j))],
        out_shape=[SDS((S, XBC), _MXU), SDS((8, XBC), F32), SDS((1, XBC), F32)],
        scratch_shapes=[pltpu.VMEM((tb + SSD_HALO, CONV_CB), F32)],
        compiler_params=_cp("parallel", "arbitrary"))(dxbc, proj, w)


def _head_consts():
    e = (lax.broadcasted_iota(jnp.int32, (128, NH * HP), 1) // HP == lax.broadcasted_iota(jnp.int32, (128, NH * HP), 0)).astype(F32)
    et = (lax.broadcasted_iota(jnp.int32, (NH * HP, 128), 0) // HP == lax.broadcasted_iota(jnp.int32, (NH * HP, 128), 1)).astype(F32)
    r = lax.broadcasted_iota(jnp.int32, (CH, CH), 0)
    c = lax.broadcasted_iota(jnp.int32, (CH, CH), 1)
    return e, et, (c <= r), (r <= c)


def _ssd_common(xbc_c, dtr, dtb, alog, e, tril, triu):
    xbc = _silu(xbc_c)
    xs = xbc[:, :NH * HP]
    dt = _softplus(dtr + dtb)
    A = -jnp.exp(alog)
    a = dt * A
    cs = _dot(tril.astype(F32), a, hi=True)
    csT = _dot(a, triu.astype(F32), "tn", hi=True)
    csL = cs[CH - 1:CH, :]
    wdec = jnp.exp(csL - cs) * dt
    dtE = _dot(dt, e, hi=True)
    ecsE = _dot(jnp.exp(cs), e, hi=True)
    wE = _dot(wdec, e, hi=True)
    eL = jnp.exp(csL)
    return xbc, xs, dt, A, cs, csT, csL, wdec, dtE, ecsE, wE, eL


def _ssd_fwd(proj, xbc_c, dtr, dtb, alog, dpar, norm_g, name="ssd_fwd"):
    S = proj.shape[0]
    nc = S // CH

    def body(z_ref, x_ref, dtr_ref, dtb_ref, alog_ref, d_ref, ng_ref, y_ref, yn_ref, hp_ref, hst):
        @pl.when(pl.program_id(0) == 0)
        def _():
            hst[...] = jnp.zeros_like(hst)

        e, et, tril, triu = _head_consts()
        xbc, xs, dt, A, cs, csT, csL, wdec, dtE, ecsE, wE, eL = _ssd_common(
            x_ref[...], dtr_ref[...], dtb_ref[...], alog_ref[...], e, tril, triu)
        hp_ref[0] = hst[...]
        xd = xs * dtE
        xw = xs * wE
        dE = _dot(jnp.broadcast_to(d_ref[...], (8, 128)), e, hi=True)[0:1, :]
        eLcol = jnp.sum(et * eL, axis=1, keepdims=True)
        for g in range(NG):
            Bg = xbc[:, NH * HP + g * NS: NH * HP + (g + 1) * NS]
            Cg = xbc[:, NH * HP + NG * NS + g * NS: NH * HP + NG * NS + (g + 1) * NS]
            gs = slice(g * GW, (g + 1) * GW)
            G = _dot(Cg, Bg, "nt")
            hg = hst[gs, :]
            yoff = ecsE[:, gs] * _dot(Cg, hg, "nt")
            hst[gs, :] = eLcol[gs, :] * hg + _dot(xw[:, gs], Bg, "tn")
            for hh in range(NH // NG):
                h = g * (NH // NG) + hh
                hs = slice(h * HP, (h + 1) * HP)
                m = jnp.where(tril, jnp.exp(jnp.where(tril, cs[:, h:h + 1] - csT[h:h + 1, :], 0.0)), 0.0)
                yd = _dot(G * m, xd[:, hs])
                y_ref[:, hs] = yd + yoff[:, hh * HP:(hh + 1) * HP] + dE[:, hs] * xs[:, hs]
        y = y_ref[...]
        yz = y * _silu(z_ref[...])
        for g in range(NG):
            gs = slice(g * GW, (g + 1) * GW)
            yg = yz[:, gs]
            r = lax.rsqrt(jnp.mean(yg * yg, axis=-1, keepdims=True) + EPS)
            yn_ref[:, gs] = (yg * r * ng_ref[:, gs]).astype(yn_ref.dtype)

    vec = pl.BlockSpec((1, 128), lambda c: (0, 0))
    return pl.pallas_call(
        body, name=name, grid=(nc,),
        in_specs=[pl.BlockSpec((CH, D), lambda c: (c, COL_Z // D)),
                  pl.BlockSpec((CH, XBC), lambda c: (c, 0)),
                  pl.BlockSpec((CH, 128), lambda c: (c, 0)), vec, vec, vec,
                  pl.BlockSpec((1, D), lambda c: (0, 0))],
        out_specs=[pl.BlockSpec((CH, D), lambda c: (c, 0)), pl.BlockSpec((CH, D), lambda c: (c, 0)),
                   pl.BlockSpec((1, NH * HP, NS), lambda c: (c, 0, 0))],
        out_shape=[SDS((S, D), F32), SDS((S, D), _MXU), SDS((nc, NH * HP, NS), F32)],
        scratch_shapes=[pltpu.VMEM((NH * HP, NS), F32)],
        compiler_params=_cp("arbitrary"))(proj, xbc_c, dtr, dtb, alog, dpar, norm_g)


def _ssd_bwd(dmix, y, proj, xbc_c, dtr, hprev, dtb, alog, dpar, norm_g, name="ssd_bwd"):
    S = proj.shape[0]
    nc = S // CH
    rev = lambda c: nc - 1 - c

    def body(dyn_ref, y_ref, z_ref, x_ref, dtr_ref, hp_ref, dtb_ref, alog_ref, d_ref, ng_ref,
             dz_ref, dx_ref, ddtr_ref, gdtb_ref, galog_ref, gd_ref, gng_ref, dh, dxd):
        @pl.when(pl.program_id(0) == 0)
        def _():
            dh[...] = jnp.zeros_like(dh)
            gdtb_ref[...] = jnp.zeros_like(gdtb_ref)
            galog_ref[...] = jnp.zeros_like(galog_ref)
            gd_ref[...] = jnp.zeros_like(gd_ref)
            gng_ref[...] = jnp.zeros_like(gng_ref)

        e, et, tril, triu = _head_consts()
        xbc_c = x_ref[...]
        dtr = dtr_ref[...]
        dtb = dtb_ref[...]
        xbc, xs, dt, A, cs, csT, csL, wdec, dtE, ecsE, wE, eL = _ssd_common(
            xbc_c, dtr, dtb, alog_ref[...], e, tril, triu)
        xd = xs * dtE
        xw = xs * wE
        dE = _dot(jnp.broadcast_to(d_ref[...], (8, 128)), e, hi=True)[0:1, :]
        eLcol = jnp.sum(et * eL, axis=1, keepdims=True)

        yv = y_ref[...]
        zv = z_ref[...]
        sz = _silu(zv)
        yz = yv * sz
        dyn = dyn_ref[...]
        dyz_parts = []
        for g in range(NG):
            gs = slice(g * GW, (g + 1) * GW)
            yg = yz[:, gs]
            r = lax.rsqrt(jnp.mean(yg * yg, axis=-1, keepdims=True) + EPS)
            yh = yg * r
            dn = dyn[:, gs]
            gng_ref[:, gs] += jnp.sum(dn * yh, axis=0, keepdims=True)
            gdn = dn * ng_ref[:, gs]
            dyz_parts.append(r * (gdn - yh * jnp.mean(yh * gdn, axis=-1, keepdims=True)))
        dyz = jnp.concatenate(dyz_parts, axis=1)
        dy = dyz * sz
        dz_ref[...] = (dyz * yv * _dsilu(zv)).astype(dz_ref.dtype)

        gd_ref[...] += jnp.sum(_dot(dy * xs, et, hi=True), axis=0, keepdims=True)
        dxs = dE * dy
        dzo = ecsE * dy
        dcs = jnp.zeros((CH, 128), F32)
        dcsL = jnp.zeros((1, 128), F32)
        ddt = jnp.zeros((CH, 128), F32)
        dB_parts, dC_parts, yoff_parts, dxw_parts = [], [], [], []
        for g in range(NG):
            Bg = xbc[:, NH * HP + g * NS: NH * HP + (g + 1) * NS]
            Cg = xbc[:, NH * HP + NG * NS + g * NS: NH * HP + NG * NS + (g + 1) * NS]
            gs = slice(g * GW, (g + 1) * GW)
            hg = hp_ref[0, gs, :]
            dhn = dh[gs, :]
            G = _dot(Cg, Bg, "nt")
            yoff_parts.append(ecsE[:, gs] * _dot(Cg, hg, "nt"))
            dC = _dot(dzo[:, gs], hg)
            dhp = _dot(dzo[:, gs], Cg, "tn") + eLcol[gs, :] * dhn
            t1 = jnp.sum(dhn * hg, axis=1, keepdims=True) * eLcol[gs, :]
            dcsL = dcsL + jnp.sum(et[gs, :] * t1, axis=0, keepdims=True)
            dxw_parts.append(_dot(Bg, dhn, "nt"))
            dB = _dot(xw[:, gs], dhn)
            dgsum = jnp.zeros((CH, CH), F32)
            for hh in range(NH // NG):
                h = g * (NH // NG) + hh
                hs = slice(h * HP, (h + 1) * HP)
                m = jnp.where(tril, jnp.exp(jnp.where(tril, cs[:, h:h + 1] - csT[h:h + 1, :], 0.0)), 0.0)
                sc = G * m
                dyh = dy[:, hs]
                dxd[:, hs] = _dot(sc, dyh, "tn")
                dsc = _dot(dyh, xd[:, hs], "nt")
                q = dsc * sc
                oh = (lax.broadcasted_iota(jnp.int32, (CH, 128), 1) == h).astype(F32)
                dcs = dcs + _dot(q, oh, hi=True) - _dot(q, oh, "tn", hi=True)
                dgsum = dgsum + dsc * m
            dC_parts.append(dC + _dot(dgsum, Bg))
            dB_parts.append(dB + _dot(dgsum, Cg, "tn"))
            dh[gs, :] = dhp
        yoff = jnp.concatenate(yoff_parts, axis=1)
        dxw = jnp.concatenate(dxw_parts, axis=1)
        dxdv = dxd[...]
        dcs = dcs + _dot(dy * yoff, et, hi=True)
        dxs = dxs + wE * dxw + dtE * dxdv
        dw = _dot(dxw * xs, et, hi=True)
        ddt = ddt + dw * jnp.exp(csL - cs) + _dot(dxdv * xs, et, hi=True)
        dcs = dcs - dw * wdec
        dcsL = dcsL + jnp.sum(dw * wdec, axis=0, keepdims=True)
        last = lax.broadcasted_iota(jnp.int32, (CH, 128), 0) == CH - 1
        dcs = dcs + jnp.where(last, dcsL, 0.0)
        da = _dot(triu.astype(F32), dcs, hi=True)
        ddt = ddt + da * A
        galog_ref[...] += jnp.sum(da * dt, axis=0, keepdims=True) * A
        valid = lax.broadcasted_iota(jnp.int32, (CH, 128), 1) < NH
        ddtr = jnp.where(valid, ddt * _sigmoid(dtr + dtb), 0.0)
        gdtb_ref[...] += jnp.sum(ddtr, axis=0, keepdims=True)
        ddtr_ref[...] = ddtr.astype(ddtr_ref.dtype)
        dxbc = jnp.concatenate([dxs] + dB_parts + dC_parts, axis=1)
        dx_ref[...] = dxbc * _dsilu(xbc_c)

    vec = pl.BlockSpec((1, 128), lambda c: (0, 0))
    vecd = pl.BlockSpec((1, D), lambda c: (0, 0))
    row = lambda w, j=0: pl.BlockSpec((CH, w), lambda c: (rev(c), j))
    return pl.pallas_call(
        body, name=name, grid=(nc,),
        in_specs=[row(D), row(D), row(D, COL_Z // D), row(XBC), row(128),
                  pl.BlockSpec((1, NH * HP, NS), lambda c: (rev(c), 0, 0)), vec, vec, vec, vecd],
        out_specs=[row(D), row(XBC), row(128), vec, vec, vec, vecd],
        out_shape=[SDS((S, D), _MXU), SDS((S, XBC), F32), SDS((S, 128), _MXU),
                   SDS((1, 128), F32), SDS((1, 128), F32), SDS((1, 128), F32), SDS((1, D), F32)],
        scratch_shapes=[pltpu.VMEM((NH * HP, NS), F32), pltpu.VMEM((CH, NH * HP), F32)],
        compiler_params=_cp("arbitrary"))(dmix, y, proj, xbc_c, dtr, hprev, dtb, alog, dpar, norm_g)


def _cf_fwd(proj, w, b, lg, lb, name="cf_fwd", tb_cap=256):
    S = proj.shape[0]
    tb = _tile(S, tb_cap, 8)

    def body(a_ref, g_ref, w_ref, b_ref, lg_ref, lb_ref, u1_ref, u_ref, ext):
        @pl.when(pl.program_id(0) == 0)
        def _():
            ext[pl.ds(0, CF_HALO), :] = jnp.zeros((CF_HALO, D), F32)

        ext[pl.ds(CF_HALO, tb), :] = a_ref[...] * _sigmoid(g_ref[...])
        acc = jnp.zeros((tb, D), F32) + b_ref[...]
        for k in range(KC):
            acc = acc + ext[pl.ds(CF_HALO - (KC - 1) + k, tb), :] * w_ref[k:k + 1, :]
        u1_ref[...] = acc
        mu = jnp.mean(acc, axis=-1, keepdims=True)
        xc = acc - mu
        r = lax.rsqrt(jnp.mean(xc * xc, axis=-1, keepdims=True) + EPS)
        u_ref[...] = _silu(xc * r * lg_ref[...] + lb_ref[...]).astype(u_ref.dtype)
        ext[pl.ds(0, CF_HALO), :] = ext[pl.ds(tb, CF_HALO), :]

    vec = pl.BlockSpec((1, D), lambda i: (0, 0))
    return pl.pallas_call(
        body, name=name, grid=(S // tb,),
        in_specs=[pl.BlockSpec((tb, D), lambda i: (i, COL_A // D)), pl.BlockSpec((tb, D), lambda i: (i, COL_G // D)),
                  pl.BlockSpec((32, D), lambda i: (0, 0)), vec, vec, vec],
        out_specs=[pl.BlockSpec((tb, D), lambda i: (i, 0)), pl.BlockSpec((tb, D), lambda i: (i, 0))],
        out_shape=[SDS((S, D), F32), SDS((S, D), _MXU)],
        scratch_shapes=[pltpu.VMEM((CF_HALO + tb, D), F32)],
        compiler_params=_cp("arbitrary"))(proj, proj, w, b, lg, lb)


def _cf_bwd(dmix, u1, proj, w, lg, lb, name="cf_bwd", tb_cap=256):
    S = proj.shape[0]
    tb = _tile(S, tb_cap, 8)
    nb = S // tb
    rev = lambda i: nb - 1 - i

    def body(du_ref, u1_ref, a_ref, g_ref, w_ref, lg_ref, lb_ref,
             da_ref, dg_ref, dw_ref, db_ref, dlg_ref, dlb_ref, ext):
        @pl.when(pl.program_id(0) == 0)
        def _():
            ext[pl.ds(tb, CF_HALO), :] = jnp.zeros((CF_HALO, D), F32)
            dw_ref[...] = jnp.zeros_like(dw_ref)
            db_ref[...] = jnp.zeros_like(db_ref)
            dlg_ref[...] = jnp.zeros_like(dlg_ref)
            dlb_ref[...] = jnp.zeros_like(dlb_ref)

        u1 = u1_ref[...]
        mu = jnp.mean(u1, axis=-1, keepdims=True)
        xc = u1 - mu
        r = lax.rsqrt(jnp.mean(xc * xc, axis=-1, keepdims=True) + EPS)
        xh = xc * r
        lgv = lg_ref[...]
        du2 = du_ref[...] * _dsilu(xh * lgv + lb_ref[...])
        dlg_ref[...] += jnp.sum(du2 * xh, axis=0, keepdims=True)
        dlb_ref[...] += jnp.sum(du2, axis=0, keepdims=True)
        gd = du2 * lgv
        du1 = r * (gd - jnp.mean(gd, axis=-1, keepdims=True) - xh * jnp.mean(gd * xh, axis=-1, keepdims=True))
        db_ref[...] += jnp.sum(du1, axis=0, keepdims=True)
        ext[pl.ds(0, tb), :] = du1
        av = a_ref[...]
        sg = _sigmoid(g_ref[...])
        u0 = av * sg
        acc = jnp.zeros((tb, D), F32)
        for k in range(KC):
            sh = ext[pl.ds(KC - 1 - k, tb), :]
            acc = acc + sh * w_ref[k:k + 1, :]
            dw_ref[k:k + 1, :] += jnp.sum(u0 * sh, axis=0, keepdims=True)
        da_ref[...] = (acc * sg).astype(da_ref.dtype)
        dg_ref[...] = (acc * av * sg * (1.0 - sg)).astype(dg_ref.dtype)
        ext[pl.ds(tb, CF_HALO), :] = ext[pl.ds(0, CF_HALO), :]

    vec = pl.BlockSpec((1, D), lambda i: (0, 0))
    wsp = pl.BlockSpec((32, D), lambda i: (0, 0))
    row = lambda j=0: pl.BlockSpec((tb, D), lambda i: (rev(i), j))
    return pl.pallas_call(
        body, name=name, grid=(nb,),
        in_specs=[row(1), row(), row(COL_A // D), row(COL_G // D), wsp, vec, vec],
        out_specs=[row(), row(), wsp, vec, vec, vec],
        out_shape=[SDS((S, D), _MXU), SDS((S, D), _MXU), SDS((32, D), F32),
                   SDS((1, D), F32), SDS((1, D), F32), SDS((1, D), F32)],
        scratch_shapes=[pltpu.VMEM((tb + CF_HALO, D), F32)],
        compiler_params=_cp("arbitrary"))(dmix, u1, proj, proj, w, lg, lb)


def _attn_fwd(q, kv, name="attn_fwd", tq_cap=512):
    S = q.shape[0]
    tq = _tile(S, tq_cap, 8)
    scale = XD ** -0.5

    def body(q_ref, kv_ref, o_ref):
        for h in range(XH):
            hs = slice(h * XD, (h + 1) * XD)
            s = _dot(q_ref[:, hs], kv_ref[:, hs], "nt") * scale
            s = s - jnp.max(s, axis=-1, keepdims=True)
            p = jnp.exp(s)
            p = p / jnp.sum(p, axis=-1, keepdims=True)
            o_ref[:, hs] = _dot(p, kv_ref[:, D + h * XD: D + (h + 1) * XD]).astype(o_ref.dtype)

    return pl.pallas_call(
        body, name=name, grid=(S // tq,),
        in_specs=[pl.BlockSpec((tq, D), lambda i: (i, 0)), pl.BlockSpec((MEM, 2 * D), lambda i: (0, 0))],
        out_specs=pl.BlockSpec((tq, D), lambda i: (i, 0)), out_shape=SDS((S, D), _MXU),
        compiler_params=_cp("parallel"))(q, kv)


def _attn_bwd(do, q, kv, name="attn_bwd", tq_cap=512):
    S = q.shape[0]
    tq = _tile(S, tq_cap, 8)
    scale = XD ** -0.5

    def body(do_ref, q_ref, kv_ref, dq_ref, dkv_ref):
        @pl.when(pl.program_id(0) == 0)
        def _():
            dkv_ref[...] = jnp.zeros_like(dkv_ref)

        for h in range(XH):
            hs = slice(h * XD, (h + 1) * XD)
            vs = slice(D + h * XD, D + (h + 1) * XD)
            qh = q_ref[:, hs]
            kh = kv_ref[:, hs]
            s = _dot(qh, kh, "nt") * scale
            s = s - jnp.max(s, axis=-1, keepdims=True)
            p = jnp.exp(s)
            p = p / jnp.sum(p, axis=-1, keepdims=True)
            doh = do_ref[:, hs]
            dp = _dot(doh, kv_ref[:, vs], "nt")
            ds = p * (dp - jnp.sum(dp * p, axis=-1, keepdims=True)) * scale
            dq_ref[:, hs] = _dot(ds, kh).astype(dq_ref.dtype)
            dkv_ref[:, hs] += _dot(ds, qh, "tn")
            dkv_ref[:, vs] += _dot(p, doh, "tn")

    return pl.pallas_call(
        body, name=name, grid=(S // tq,),
        in_specs=[pl.BlockSpec((tq, D), lambda i: (i, 0)), pl.BlockSpec((tq, D), lambda i: (i, 0)),
                  pl.BlockSpec((MEM, 2 * D), lambda i: (0, 0))],
        out_specs=[pl.BlockSpec((tq, D), lambda i: (i, 0)), pl.BlockSpec((MEM, 2 * D), lambda i: (0, 0))],
        out_shape=[SDS((S, D), _MXU), SDS((MEM, 2 * D), F32)],
        compiler_params=_cp("arbitrary"))(do, q, kv)


def _ffn_act(gu, name="ffn_act", tb_cap=512):
    S = gu.shape[0]
    tb = _tile(S, tb_cap, 8)

    def body(gu_ref, o_ref):
        o_ref[...] = (_silu(gu_ref[:, :FB]) * gu_ref[:, FB:]).astype(o_ref.dtype)

    return pl.pallas_call(
        body, name=name, grid=(S // tb, DFF // FB),
        in_specs=[pl.BlockSpec((tb, 2 * FB), lambda i, j: (i, j))],
        out_specs=pl.BlockSpec((tb, FB), lambda i, j: (i, j)), out_shape=SDS((S, DFF), _MXU),
        compiler_params=_cp("parallel", "parallel"))(gu)


def _ffn_act_bwd(dact, gu, name="ffn_act_bwd", tb_cap=512):
    S = gu.shape[0]
    tb = _tile(S, tb_cap, 8)

    def body(d_ref, gu_ref, o_ref):
        gt = gu_ref[:, :FB]
        d = d_ref[...]
        o_ref[:, :FB] = (d * gu_ref[:, FB:] * _dsilu(gt)).astype(o_ref.dtype)
        o_ref[:, FB:] = (d * _silu(gt)).astype(o_ref.dtype)

    return pl.pallas_call(
        body, name=name, grid=(S // tb, DFF // FB),
        in_specs=[pl.BlockSpec((tb, FB), lambda i, j: (i, j)), pl.BlockSpec((tb, 2 * FB), lambda i, j: (i, j))],
        out_specs=pl.BlockSpec((tb, 2 * FB), lambda i, j: (i, j)), out_shape=SDS((S, 2 * DFF), _MXU),
        compiler_params=_cp("parallel", "parallel"))(dact, gu)


def _local_step(x, mem, tgt, W, P):
    h = _rms_fwd(x, P["g_mix"], "rms_mix")
    proj = _mm_nn(h, W["main"], "in_proj", tn_cap=1152)
    dtr = _mm_nn(h, W["dt"], "in_proj_dt")
    xbc_c = _ssd_conv_fwd(proj, P["conv4_w"], P["conv4_b"])
    y, yn, hprev = _ssd_fwd(proj, xbc_c, dtr, P["dtb"], P["alog"], P["dpar"], P["ssd_norm_g"])
    u1, u = _cf_fwd(proj, P["cf_w"], P["cf_b"], P["ln_g"], P["ln_b"])
    mix = jnp.concatenate([yn, u], axis=1)
    x1 = _mm_nn(mix, W["out"], "out_proj", add=x)
    hq = _rms_fwd(x1, P["g_xattn"], "rms_xattn")
    q = _mm_nn(hq, W["q"], "q_proj")
    mn = _rms_fwd(mem, P["g_mem"], "rms_mem")
    kv = _mm_nn(mn, W["kv"], "kv_proj")
    o = _attn_fwd(q, kv)
    x2 = _mm_nn(o, W["o"], "o_proj", add=x1)
    hf = _rms_fwd(x2, P["g_ffn"], "rms_ffn")
    gu = _mm_nn(hf, W["gu"], "ffn_in")
    act = _ffn_act(gu)
    x3 = _mm_nn(act, W["down"], "ffn_out", add=x2)
    loss, dx3, g_final = _final_loss(x3, P["g_final"], tgt)
    GW, GP = {}, {"g_final": g_final}
    dx3b = dx3.astype(_MXU)
    dact = _mm_nt(dx3b, W["down"], "ffn_out_dx", tk_cap=1408)
    GW["down"] = _mm_tn(act, dx3b, "ffn_out_dw", tk_cap=1408, tn_cap=1024)
    dgu = _ffn_act_bwd(dact, gu)
    dhf = _mm_nt(dgu, W["gu"], "ffn_in_dx", tk_cap=512)
    GW["gu"] = _mm_tn(hf, dgu, "ffn_in_dw")
    dx2, GP["g_ffn"] = _rms_bwd(x2, P["g_ffn"], dhf, dx3, "rms_ffn_bwd")
    dx2b = dx2.astype(_MXU)
    do = _mm_nt(dx2b, W["o"], "o_proj_dx")
    GW["o"] = _mm_tn(o, dx2b, "o_proj_dw")
    dq, dkv = _attn_bwd(do, q, kv)
    dhq = _mm_nt(dq, W["q"], "q_proj_dx")
    GW["q"] = _mm_tn(hq, dq, "q_proj_dw")
    dkvb = dkv.astype(_MXU)
    GW["kv"] = _mm_tn(mn, dkvb, "kv_proj_dw", tm_cap=256)
    dmn = _mm_nt(dkvb, W["kv"], "kv_proj_dx")
    GP["g_mem"] = _rms_bwd(mem, P["g_mem"], dmn, None, "rms_mem_bwd")
    dx1, GP["g_xattn"] = _rms_bwd(x1, P["g_xattn"], dhq, dx2, "rms_xattn_bwd")
    dx1b = dx1.astype(_MXU)
    dmix = _mm_nt(dx1b, W["out"], "out_proj_dx")
    GW["out"] = _mm_tn(mix, dx1b, "out_proj_dw", tn_cap=1024)
    da, dg, GP["cf_w"], GP["cf_b"], GP["ln_g"], GP["ln_b"] = _cf_bwd(dmix, u1, proj, P["cf_w"], P["ln_g"], P["ln_b"])
    dz, dxbc_c, ddtr, GP["dtb"], GP["alog"], GP["dpar"], GP["ssd_norm_g"] = _ssd_bwd(
        dmix, y, proj, xbc_c, dtr, hprev, P["dtb"], P["alog"], P["dpar"], P["ssd_norm_g"])
    dxbc, GP["conv4_w"], GP["conv4_b"] = _ssd_conv_bwd(dxbc_c, proj, P["conv4_w"])
    dproj = jnp.concatenate([dz, da, dg, dxbc], axis=1)
    dh = _mm_nt(ddtr, W["dt"], "in_proj_dt_dx")
    dh = _mm_nt(dproj, W["main"], "in_proj_dx", add=dh, tk_cap=512)
    GW["main"] = _mm_tn(h, dproj, "in_proj_dw", tn_cap=1152)
    GW["dt"] = _mm_tn(h, ddtr, "in_proj_dt_dw")
    grad_x, GP["g_mix"] = _rms_bwd(x, P["g_mix"], dh, dx1, "rms_mix_bwd")
    return loss, grad_x, GW, GP


Z_END, XBC_END, DT_END = NH * HP, NH * HP + XBC, NH * HP + XBC + NH
NFB = DFF // FB


def _pad_to(a, rows=None, cols=None):
    r = 0 if rows is None else rows - a.shape[0]
    c = 0 if cols is None else cols - a.shape[1]
    return jnp.pad(a, ((0, r), (0, c)))


def _pack_params(p):
    w_in = p["w_in"]
    main = jnp.concatenate([w_in[:, :Z_END], w_in[:, DT_END:], w_in[:, Z_END:XBC_END]], axis=1)
    gu = jnp.stack([p["w_gate"].reshape(D, NFB, FB), p["w_up"].reshape(D, NFB, FB)], axis=2).reshape(D, 2 * DFF)
    W = {"main": main, "dt": _pad_to(w_in[:, XBC_END:DT_END], cols=128), "out": p["w_out"], "q": p["w_q"],
         "kv": p["w_kv"], "o": p["w_o"], "gu": gu, "down": p["w_down"]}
    W = {k: v.astype(_MXU) for k, v in W.items()}
    P = {"g_mix": p["norm_mix_g"], "g_xattn": p["norm_xattn_g"], "g_mem": p["norm_mem_g"], "g_ffn": p["norm_ffn_g"],
         "g_final": p["norm_final_g"].reshape(1, D), "ssd_norm_g": p["ssd_norm_g"], "cf_b": p["cf_conv_b"],
         "ln_g": p["cf_ln_g"], "ln_b": p["cf_ln_b"], "conv4_w": _pad_to(p["ssd_conv_w"], rows=8),
         "conv4_b": p["ssd_conv_b"], "dtb": _pad_to(p["ssd_dt_bias"], cols=128),
         "alog": _pad_to(p["ssd_A_log"], cols=128), "dpar": _pad_to(p["ssd_D"], cols=128),
         "cf_w": _pad_to(p["cf_conv_w"], rows=32)}
    return W, P


def _unpack_grads(GW, GP):
    m = GW["main"]
    g = {"w_in": jnp.concatenate([m[:, COL_Z:COL_Z + D], m[:, COL_XBC:], GW["dt"][:, :NH],
                                  m[:, COL_A:COL_A + D], m[:, COL_G:COL_G + D]], axis=1)}
    gu = GW["gu"].reshape(D, NFB, 2, FB)
    g["w_gate"] = gu[:, :, 0, :].reshape(D, DFF)
    g["w_up"] = gu[:, :, 1, :].reshape(D, DFF)
    for k, n in (("out", "w_out"), ("q", "w_q"), ("kv", "w_kv"), ("o", "w_o"), ("down", "w_down")):
        g[n] = GW[k]
    for k, n in (("g_mix", "norm_mix_g"), ("g_xattn", "norm_xattn_g"), ("g_mem", "norm_mem_g"), ("g_ffn", "norm_ffn_g"),
                 ("ssd_norm_g", "ssd_norm_g"), ("cf_b", "cf_conv_b"), ("ln_g", "cf_ln_g"), ("ln_b", "cf_ln_b"),
                 ("conv4_b", "ssd_conv_b")):
        g[n] = GP[k]
    g["norm_final_g"] = GP["g_final"].reshape(D)
    g["ssd_conv_w"] = GP["conv4_w"][:KS]
    g["cf_conv_w"] = GP["cf_w"][:KC]
    g["ssd_dt_bias"] = GP["dtb"][:, :NH]
    g["ssd_A_log"] = GP["alog"][:, :NH]
    g["ssd_D"] = GP["dpar"][:, :NH]
    return g


HBM_SPEC = pl.BlockSpec(memory_space=pl.ANY)
COMM_PARAMS = pltpu.CompilerParams(vmem_limit_bytes=VMEM_LIMIT)


def _chip_peers(x, y):
    return [(1 - x, y), (x, 1 - y), (1 - x, 1 - y)]


def _remote(src, dst, send_sem, recv_sem, dev):
    return pltpu.make_async_remote_copy(src_ref=src, dst_ref=dst, send_sem=send_sem, recv_sem=recv_sem,
                                        device_id=dev, device_id_type=MESHID)


def _allgather_chips(src, name):
    R = src.shape[0]

    def body(src_ref, out_ref, send_sems, recv_sems, local_sem):
        x, y, c = lax.axis_index("x"), lax.axis_index("y"), lax.axis_index("c")
        me = 2 * x + y
        mine = pltpu.make_async_copy(src_ref, out_ref.at[me], local_sem)
        mine.start()
        peers = _chip_peers(x, y)
        sends = [_remote(src_ref, out_ref.at[me], send_sems.at[k], recv_sems.at[k], (px, py, c))
                 for k, (px, py) in enumerate(peers)]
        for cp in sends:
            cp.start()
        for k, (px, py) in enumerate(peers):
            _remote(src_ref, out_ref.at[2 * px + py], send_sems.at[k], recv_sems.at[k], (px, py, c)).wait_recv()
        for cp in sends:
            cp.wait_send()
        mine.wait()

    return pl.pallas_call(
        body, name=name, in_specs=[HBM_SPEC], out_specs=HBM_SPEC, out_shape=SDS((4, R, 128), src.dtype),
        scratch_shapes=[pltpu.SemaphoreType.DMA((3,)), pltpu.SemaphoreType.DMA((3,)), pltpu.SemaphoreType.DMA],
        compiler_params=COMM_PARAMS)(src)


def _pair_split(g, name="rs_pair_send"):
    _, _, R, _ = g.shape

    def body(g_ref, mine_ref, theirs_ref, send_sem, recv_sem, local_sem):
        x, y, c = lax.axis_index("x"), lax.axis_index("y"), lax.axis_index("c")
        keep = pltpu.make_async_copy(g_ref.at[c], mine_ref, local_sem)
        keep.start()
        cp = _remote(g_ref.at[1 - c], theirs_ref, send_sem, recv_sem, (x, y, 1 - c))
        cp.start()
        cp.wait_recv()
        cp.wait_send()
        keep.wait()

    return pl.pallas_call(
        body, name=name, in_specs=[HBM_SPEC], out_specs=[HBM_SPEC, HBM_SPEC],
        out_shape=[SDS((4, R, 128), g.dtype), SDS((4, R, 128), g.dtype)],
        scratch_shapes=[pltpu.SemaphoreType.DMA, pltpu.SemaphoreType.DMA, pltpu.SemaphoreType.DMA],
        compiler_params=COMM_PARAMS)(g)


def _scatter_chips(p, name="rs_chip_send"):
    _, R, _ = p.shape

    def body(p_ref, out_ref, send_sems, recv_sems, local_sem):
        x, y, c = lax.axis_index("x"), lax.axis_index("y"), lax.axis_index("c")
        me = 2 * x + y
        mine = pltpu.make_async_copy(p_ref.at[me], out_ref.at[me], local_sem)
        mine.start()
        peers = _chip_peers(x, y)
        sends = [_remote(p_ref.at[2 * px + py], out_ref.at[me], send_sems.at[k], recv_sems.at[k], (px, py, c))
                 for k, (px, py) in enumerate(peers)]
        for cp in sends:
            cp.start()
        for k, (px, py) in enumerate(peers):
            _remote(p_ref.at[me], out_ref.at[2 * px + py], send_sems.at[k], recv_sems.at[k], (px, py, c)).wait_recv()
        for cp in sends:
            cp.wait_send()
        mine.wait()

    return pl.pallas_call(
        body, name=name, in_specs=[HBM_SPEC], out_specs=HBM_SPEC, out_shape=SDS((4, R, 128), p.dtype),
        scratch_shapes=[pltpu.SemaphoreType.DMA((3,)), pltpu.SemaphoreType.DMA((3,)), pltpu.SemaphoreType.DMA],
        compiler_params=COMM_PARAMS)(p)


def _pair_join(p, name="rs_pair_join"):
    R = p.shape[0]

    def body(p_ref, out_ref, send_sem, recv_sem, local_sem):
        x, y, c = lax.axis_index("x"), lax.axis_index("y"), lax.axis_index("c")
        keep = pltpu.make_async_copy(p_ref, out_ref.at[c], local_sem)
        keep.start()
        cp = _remote(p_ref, out_ref.at[c], send_sem, recv_sem, (x, y, 1 - c))
        cp.start()
        _remote(p_ref, out_ref.at[1 - c], send_sem, recv_sem, (x, y, 1 - c)).wait_recv()
        cp.wait_send()
        keep.wait()

    return pl.pallas_call(
        body, name=name, in_specs=[HBM_SPEC], out_specs=HBM_SPEC, out_shape=SDS((2, R, 128), p.dtype),
        scratch_shapes=[pltpu.SemaphoreType.DMA, pltpu.SemaphoreType.DMA, pltpu.SemaphoreType.DMA],
        compiler_params=COMM_PARAMS)(p)


def _allreduce_small(v, name="allreduce_small"):
    R = v.shape[0]

    def body(v_ref, out_ref, buf, send_sems, recv_sems):
        x, y, c = lax.axis_index("x"), lax.axis_index("y"), lax.axis_index("c")
        me = 4 * x + 2 * y + c
        peers = [(x, y, 1 - c)] + [(px, py, pc) for px, py in _chip_peers(x, y) for pc in (c, 1 - c)]
        sends = [_remote(v_ref, buf.at[me], send_sems.at[k], recv_sems.at[k], dev) for k, dev in enumerate(peers)]
        for cp in sends:
            cp.start()
        buf[me] = v_ref[...]
        for k, (px, py, pc) in enumerate(peers):
            _remote(v_ref, buf.at[4 * px + 2 * py + pc], send_sems.at[k], recv_sems.at[k], (px, py, pc)).wait_recv()
        for cp in sends:
            cp.wait_send()
        acc = buf[0]
        for i in range(1, 8):
            acc = acc + buf[i]
        out_ref[...] = acc

    return pl.pallas_call(
        body, name=name, in_specs=[pl.BlockSpec(memory_space=pltpu.VMEM)],
        out_specs=pl.BlockSpec(memory_space=pltpu.VMEM), out_shape=SDS((R, 128), F32),
        scratch_shapes=[pltpu.VMEM((8, R, 128), F32), pltpu.SemaphoreType.DMA((7,)), pltpu.SemaphoreType.DMA((7,))],
        compiler_params=COMM_PARAMS)(v)


def _pair_sum(a, b, name="rs_pair_sum", tr=2432):
    _, R, _ = a.shape
    tr = _tile(R, tr, 16)

    def body(a_ref, b_ref, o_ref):
        o_ref[...] = (a_ref[...] + b_ref[...]).astype(o_ref.dtype)

    blk = pl.BlockSpec((1, tr, 128), lambda j, i: (j, i, 0))
    return pl.pallas_call(body, name=name, grid=(4, R // tr), in_specs=[blk, blk], out_specs=blk,
                          out_shape=SDS((4, R, 128), BF16), compiler_params=_cp("parallel", "parallel"))(a, b)


def _chip_sum(r, name="rs_chip_sum", tr=2432):
    _, R, _ = r.shape
    tr = _tile(R, tr, 16)

    def body(r_ref, o_ref):
        acc = r_ref[0].astype(F32)
        for i in range(1, 4):
            acc = acc + r_ref[i].astype(F32)
        o_ref[...] = acc

    return pl.pallas_call(body, name=name, grid=(R // tr,), in_specs=[pl.BlockSpec((4, tr, 128), lambda i: (0, i, 0))],
                          out_specs=pl.BlockSpec((tr, 128), lambda i: (i, 0)), out_shape=SDS((R, 128), F32),
                          compiler_params=_cp("parallel"))(r)


def _adamw(w, g, m, v, name):
    R, C = w.shape
    tr = R if R * C <= 2 ** 17 else _tile(R, max(8, (2 ** 17 // C) // 8 * 8), 8)
    bc1 = 1.0 - ADAM_B1 ** ADAM_STEP
    bc2 = 1.0 - ADAM_B2 ** ADAM_STEP

    def body(w_ref, g_ref, m_ref, v_ref, d_ref, mo_ref, vo_ref):
        gv = g_ref[...]
        mn = ADAM_B1 * m_ref[...] + (1.0 - ADAM_B1) * gv
        vn = ADAM_B2 * v_ref[...] + (1.0 - ADAM_B2) * (gv * gv)
        mo_ref[...] = mn
        vo_ref[...] = vn
        d_ref[...] = -ADAM_LR * ((mn / bc1) / (jnp.sqrt(vn / bc2) + ADAM_EPS) + ADAM_WD * w_ref[...])

    blk = pl.BlockSpec((tr, C), lambda i: (i, 0))
    return pl.pallas_call(body, name=name, grid=(R // tr,), in_specs=[blk] * 4, out_specs=[blk] * 3,
                          out_shape=[SDS((R, C), F32)] * 3, compiler_params=_cp("parallel"))(w, g, m, v)


WEIGHT_NAMES = ["norm_mix_g", "w_in", "ssd_conv_w", "ssd_conv_b", "ssd_dt_bias", "ssd_A_log", "ssd_D", "ssd_norm_g",
                "cf_conv_w", "cf_conv_b", "cf_ln_g", "cf_ln_b", "w_out", "norm_xattn_g", "norm_mem_g", "w_q", "w_kv",
                "w_o", "norm_ffn_g", "w_gate", "w_up", "w_down", "norm_final_g"]
BIG = [("w_in", True), ("w_out", False), ("w_q", False), ("w_kv", True), ("w_o", False), ("w_gate", True),
       ("w_up", True), ("w_down", False)]
SMALL = [("g_mix", (1, D)), ("g_xattn", (1, D)), ("g_mem", (1, D)), ("g_ffn", (1, D)), ("g_final", (1, D)),
         ("ssd_norm_g", (1, D)), ("cf_b", (1, D)), ("ln_g", (1, D)), ("ln_b", (1, D)), ("conv4_w", (8, XBC)),
         ("conv4_b", (1, XBC)), ("dtb", (1, 128)), ("alog", (1, 128)), ("dpar", (1, 128)), ("cf_w", (32, D)),
         ("loss", (1, 128))]
SMALL_REF = [("norm_mix_g", "g_mix"), ("norm_xattn_g", "g_xattn"), ("norm_mem_g", "g_mem"), ("norm_ffn_g", "g_ffn"),
             ("norm_final_g", "g_final"), ("ssd_norm_g", "ssd_norm_g"), ("cf_conv_b", "cf_b"), ("cf_ln_g", "ln_g"),
             ("cf_ln_b", "ln_b"), ("ssd_conv_b", "conv4_b"), ("ssd_dt_bias", "dtb"), ("ssd_A_log", "alog"),
             ("ssd_D", "dpar")]
RS_ROWS = 19456


def _pack_small(d):
    flat = jnp.concatenate([d[k].reshape(-1) for k, _ in SMALL])
    return flat.reshape(-1, 128)


def _unpack_small(a):
    flat = a.reshape(-1)
    out, off = {}, 0
    for k, shp in SMALL:
        n = shp[0] * shp[1]
        out[k] = flat[off:off + n].reshape(shp)
        off += n
    return out


def _small_side(get):
    d = {k: jnp.zeros(shp, F32) for k, shp in SMALL}
    for ref_name, k in SMALL_REF:
        a = get(ref_name).reshape(1, -1)
        d[k] = _pad_to(a, cols=128) if a.shape[1] < 128 else a
    return d


def kernel(x, mem, norm_mix_g, w_in, ssd_conv_w, ssd_conv_b, ssd_dt_bias, ssd_A_log, ssd_D, ssd_norm_g, cf_conv_w, cf_conv_b, cf_ln_g, cf_ln_b, w_out, norm_xattn_g, norm_mem_g, w_q, w_kv, w_o, norm_ffn_g, w_gate, w_up, w_down, norm_final_g, loss_target, m_norm_mix_g, m_w_in, m_ssd_conv_w, m_ssd_conv_b, m_ssd_dt_bias, m_ssd_A_log, m_ssd_D, m_ssd_norm_g, m_cf_conv_w, m_cf_conv_b, m_cf_ln_g, m_cf_ln_b, m_w_out, m_norm_xattn_g, m_norm_mem_g, m_w_q, m_w_kv, m_w_o, m_norm_ffn_g, m_w_gate, m_w_up, m_w_down, m_norm_final_g, v_norm_mix_g, v_w_in, v_ssd_conv_w, v_ssd_conv_b, v_ssd_dt_bias, v_ssd_A_log, v_ssd_D, v_ssd_norm_g, v_cf_conv_w, v_cf_conv_b, v_cf_ln_g, v_cf_ln_b, v_w_out, v_norm_xattn_g, v_norm_mem_g, v_w_q, v_w_kv, v_w_o, v_norm_ffn_g, v_w_gate, v_w_up, v_w_down, v_norm_final_g):
    env = dict(locals())
    wts = {n: env[n] for n in WEIGHT_NAMES}
    mom = {n: env["m_" + n] for n in WEIGHT_NAMES}
    var = {n: env["v_" + n] for n in WEIGHT_NAMES}
    chip = 2 * lax.axis_index("x") + lax.axis_index("y")

    shards = {n: wts[n][0] for n, _ in BIG}
    flat = jnp.concatenate([shards[n].astype(BF16).reshape(-1) for n, _ in BIG]).reshape(-1, 128)
    gathered = _allgather_chips(flat, "allgather_weights").reshape(4, -1)
    full, off = {}, 0
    for n, by_cols in BIG:
        r, c = shards[n].shape
        piece = gathered[:, off:off + r * c].reshape(4, r, c)
        full[n] = piece.transpose(1, 0, 2).reshape(r, 4 * c) if by_cols else piece.reshape(4 * r, c)
        off += r * c
    cw4, cw31 = ssd_conv_w[0], cf_conv_w[0]
    n4, n31 = cw4.size, cw31.size
    cflat = jnp.concatenate([cw4.reshape(-1), cw31.reshape(-1), jnp.zeros((-(n4 + n31)) % 1024, F32)]).reshape(-1, 128)
    cg = _allgather_chips(cflat, "allgather_conv").reshape(4, -1)
    full["ssd_conv_w"] = cg[:, :n4].reshape(4, KS, -1).transpose(1, 0, 2).reshape(KS, XBC)
    full["cf_conv_w"] = cg[:, n4:n4 + n31].reshape(4, KC, -1).transpose(1, 0, 2).reshape(KC, D)
    for n in WEIGHT_NAMES:
        if n not in full:
            full[n] = wts[n]
    W, P = _pack_params(full)

    loss, grad_x, GW, GP = _local_step(x[0], mem[0], loss_target[0], W, P)
    g = _unpack_grads(GW, GP)

    parts = []
    for n, by_cols in BIG:
        a = g[n]
        if by_cols:
            r, c4 = a.shape
            parts.append(a.reshape(2, r // 2, 4, c4 // 4).transpose(0, 2, 1, 3).reshape(2, 4, -1))
        else:
            r4, c = a.shape
            parts.append(a.reshape(4, 2, r4 // 8, c).transpose(1, 0, 2, 3).reshape(2, 4, -1))
    used = sum(p.shape[-1] for p in parts)
    parts.append(jnp.zeros((2, 4, RS_ROWS * 128 - used), F32))
    gflat = jnp.concatenate(parts, axis=-1).reshape(2, 4, RS_ROWS, 128)
    mine, theirs = _pair_split(gflat)
    pieces = _scatter_chips(_pair_sum(mine, theirs))
    reduced = _pair_join(_chip_sum(pieces)).reshape(2, -1)
    gshard, off = {}, 0
    for n, _ in BIG:
        r, c = shards[n].shape
        h = r * c // 2
        gshard[n] = jnp.concatenate([reduced[0, off:off + h], reduced[1, off:off + h]]).reshape(r, c)
        off += h

    small = dict(GP)
    small["loss"] = loss
    red = _unpack_small(_allreduce_small(_pack_small(small)))
    gshard["ssd_conv_w"] = lax.dynamic_slice_in_dim(red["conv4_w"][:KS], chip * cw4.shape[1], cw4.shape[1], axis=1)
    gshard["cf_conv_w"] = lax.dynamic_slice_in_dim(red["cf_w"][:KC], chip * cw31.shape[1], cw31.shape[1], axis=1)

    grads, delta, new_m, new_v = {}, {}, {}, {}
    for n in [b for b, _ in BIG] + ["ssd_conv_w", "cf_conv_w"]:
        d_, m_, v_ = _adamw(wts[n][0], gshard[n], mom[n][0], var[n][0], "adamw_" + n)
        grads[n], delta[n], new_m[n], new_v[n] = gshard[n][None], d_[None], m_[None], v_[None]
    sd, sm, sv = _adamw(_pack_small(_small_side(lambda n: wts[n])), _pack_small(red),
                        _pack_small(_small_side(lambda n: mom[n])), _pack_small(_small_side(lambda n: var[n])),
                        "adamw_small")
    sd, sm, sv = _unpack_small(sd), _unpack_small(sm), _unpack_small(sv)
    for ref_name, k in SMALL_REF:
        shp = wts[ref_name].shape
        for dst, src in ((grads, red), (delta, sd), (new_m, sm), (new_v, sv)):
            dst[ref_name] = src[k][:, :shp[-1]].reshape(shp)

    return (red["loss"][0, 0], grad_x[None], *[grads[n] for n in WEIGHT_NAMES], *[delta[n] for n in WEIGHT_NAMES],
            *[new_m[n] for n in WEIGHT_NAMES], *[new_v[n] for n in WEIGHT_NAMES])
```

```python
import functools
import math

import jax
import jax.numpy as jnp
from jax import lax
from jax.experimental import pallas as pl
from jax.experimental.pallas import tpu as pltpu

F32 = jnp.float32
BF16 = jnp.bfloat16
_MXU = BF16
_HI = lax.Precision.HIGHEST

D = 1024
MEM = 256
NH, HP, NG, NS = 16, 64, 2, 128
GW = NH * HP // NG
CH = 128
XBC = NH * HP + 2 * NG * NS
KS, KC = 4, 31
XH, XD = 4, 256
DFF = 2816
FB = 256
EPS = 1e-6
COL_Z, COL_A, COL_G, COL_XBC, MAINW = 0, 1024, 2048, 3072, 4608
VMEM_LIMIT = 56 * 2 ** 20

ADAM_LR, ADAM_B1, ADAM_B2, ADAM_EPS, ADAM_WD, ADAM_STEP = 0.001, 0.9, 0.999, 1e-08, 0.01, 10

SDS = jax.ShapeDtypeStruct
MESHID = pl.DeviceIdType.MESH


def _cp(*sem):
    return pltpu.CompilerParams(dimension_semantics=sem, vmem_limit_bytes=VMEM_LIMIT)


def _tile(n, cap, unit=128):
    if n <= cap:
        return n
    best = None
    for t in range(unit, cap + 1, unit):
        if n % t == 0:
            best = t
    assert best is not None, (n, cap)
    return best


def _sigmoid(x):
    return 1.0 / (1.0 + jnp.exp(-x))


def _silu(x):
    return x * _sigmoid(x)


def _dsilu(x):
    s = _sigmoid(x)
    return s * (1.0 + x * (1.0 - s))


def _softplus(x):
    return jnp.maximum(x, 0.0) + jnp.log(1.0 + jnp.exp(-jnp.abs(x)))


def _dot(a, b, dims=None, hi=False):
    dn = {None: (((1,), (0,)), ((), ())), "nt": (((1,), (1,)), ((), ())), "tn": (((0,), (0,)), ((), ()))}[dims]
    if hi:
        return lax.dot_general(a.astype(F32), b.astype(F32), dn, preferred_element_type=F32, precision=_HI)
    return lax.dot_general(a.astype(_MXU), b.astype(_MXU), dn, preferred_element_type=F32)


def _mm_nn(a, b, name, add=None, out_dtype=F32, tm_cap=1024, tn_cap=1408):
    M, K = a.shape
    _, N = b.shape
    tm, tn = _tile(M, tm_cap, 8), _tile(N, tn_cap)

    def body(a_ref, b_ref, *rest):
        o_ref = rest[-1]
        acc = _dot(a_ref[...], b_ref[...])
        if add is not None:
            acc = acc + rest[0][...]
        o_ref[...] = acc.astype(o_ref.dtype)

    in_specs = [pl.BlockSpec((tm, K), lambda j, i: (i, 0)), pl.BlockSpec((K, tn), lambda j, i: (0, j))]
    args = [a, b]
    if add is not None:
        in_specs.append(pl.BlockSpec((tm, tn), lambda j, i: (i, j)))
        args.append(add)
    return pl.pallas_call(
        body, name=name, grid=(N // tn, M // tm), in_specs=in_specs,
        out_specs=pl.BlockSpec((tm, tn), lambda j, i: (i, j)), out_shape=SDS((M, N), out_dtype),
        compiler_params=_cp("parallel", "parallel"))(*args)


def _mm_nt(a, b, name, add=None, out_dtype=F32, tm_cap=512, tk_cap=1024):
    M, N = a.shape
    K, _ = b.shape
    tm, tk = _tile(M, tm_cap, 8), _tile(K, tk_cap)

    def body(a_ref, b_ref, *rest):
        o_ref = rest[-1]
        acc = _dot(a_ref[...], b_ref[...], "nt")
        if add is not None:
            acc = acc + rest[0][...]
        o_ref[...] = acc.astype(o_ref.dtype)

    in_specs = [pl.BlockSpec((tm, N), lambda j, i: (i, 0)), pl.BlockSpec((tk, N), lambda j, i: (j, 0))]
    args = [a, b]
    if add is not None:
        in_specs.append(pl.BlockSpec((tm, tk), lambda j, i: (i, j)))
        args.append(add)
    return pl.pallas_call(
        body, name=name, grid=(K // tk, M // tm), in_specs=in_specs,
        out_specs=pl.BlockSpec((tm, tk), lambda j, i: (i, j)), out_shape=SDS((M, K), out_dtype),
        compiler_params=_cp("parallel", "parallel"))(*args)


def _mm_tn(a, b, name, tm_cap=1024, tk_cap=512, tn_cap=1408):
    M, K = a.shape
    _, N = b.shape
    tm, tk, tn = _tile(M, tm_cap, 8), _tile(K, tk_cap), _tile(N, tn_cap)

    def body(a_ref, b_ref, o_ref):
        @pl.when(pl.program_id(2) == 0)
        def _():
            o_ref[...] = jnp.zeros_like(o_ref)

        o_ref[...] += _dot(a_ref[...], b_ref[...], "tn")

    return pl.pallas_call(
        body, name=name, grid=(K // tk, N // tn, M // tm),
        in_specs=[pl.BlockSpec((tm, tk), lambda k, n, m: (m, k)), pl.BlockSpec((tm, tn), lambda k, n, m: (m, n))],
        out_specs=pl.BlockSpec((tk, tn), lambda k, n, m: (k, n)), out_shape=SDS((K, N), F32),
        compiler_params=_cp("parallel", "parallel", "arbitrary"))(a, b)


def _rms_fwd(x, g, name, tb_cap=512):
    S, Dm = x.shape
    tb = _tile(S, tb_cap, 8)

    def body(x_ref, g_ref, o_ref):
        xv = x_ref[...]
        r = lax.rsqrt(jnp.mean(xv * xv, axis=-1, keepdims=True) + EPS)
        o_ref[...] = (xv * r * g_ref[...]).astype(o_ref.dtype)

    return pl.pallas_call(
        body, name=name, grid=(S // tb,),
        in_specs=[pl.BlockSpec((tb, Dm), lambda i: (i, 0)), pl.BlockSpec((1, Dm), lambda i: (0, 0))],
        out_specs=pl.BlockSpec((tb, Dm), lambda i: (i, 0)), out_shape=SDS((S, Dm), _MXU),
        compiler_params=_cp("parallel"))(x, g)


def _rms_bwd(x, g, dh, dres, name, tb_cap=512, low=True):
    S, Dm = x.shape
    tb = _tile(S, tb_cap, 8)
    need_dx = dres is not None

    def body(x_ref, g_ref, dh_ref, *rest):
        dg_ref = rest[-1]
        xv = x_ref[...]
        r = lax.rsqrt(jnp.mean(xv * xv, axis=-1, keepdims=True) + EPS)
        xh = xv * r
        dy = dh_ref[...].astype(F32)

        @pl.when(pl.program_id(0) == 0)
        def _():
            dg_ref[...] = jnp.zeros_like(dg_ref)

        dg_ref[...] += jnp.sum(dy * xh, axis=0, keepdims=True)
        if need_dx:
            gdy = dy * g_ref[...]
            dx = r * (gdy - xh * jnp.mean(xh * gdy, axis=-1, keepdims=True))
            tot = rest[0][...] + dx
            rest[1][...] = tot
            if low:
                rest[2][...] = tot.astype(rest[2].dtype)

    row = pl.BlockSpec((tb, Dm), lambda i: (i, 0))
    vec = pl.BlockSpec((1, Dm), lambda i: (0, 0))
    if need_dx:
        outs = [SDS((S, Dm), F32)] + ([SDS((S, Dm), _MXU)] if low else [])
        return pl.pallas_call(
            body, name=name, grid=(S // tb,), in_specs=[row, vec, row, row], out_specs=[row] * len(outs) + [vec],
            out_shape=outs + [SDS((1, Dm), F32)], compiler_params=_cp("arbitrary"))(x, g, dh, dres)
    return pl.pallas_call(
        body, name=name, grid=(S // tb,), in_specs=[row, vec, row], out_specs=vec,
        out_shape=SDS((1, Dm), F32), compiler_params=_cp("arbitrary"))(x, g, dh)


def _final_loss(x, g, tgt, name="final_loss", tb_cap=512):
    S, Dm = x.shape
    tb = _tile(S, tb_cap, 8)

    def body(x_ref, g_ref, t_ref, loss_ref, dx_ref, dxl_ref, dg_ref):
        xv = x_ref[...]
        gv = g_ref[...]
        r = lax.rsqrt(jnp.mean(xv * xv, axis=-1, keepdims=True) + EPS)
        xh = xv * r
        e = xh * gv - t_ref[...]

        @pl.when(pl.program_id(0) == 0)
        def _():
            loss_ref[...] = jnp.zeros_like(loss_ref)
            dg_ref[...] = jnp.zeros_like(dg_ref)

        loss_ref[...] += 0.5 * jnp.sum(jnp.mean(e * e, axis=-1, keepdims=True))
        dy = e * (1.0 / Dm)
        dg_ref[...] += jnp.sum(dy * xh, axis=0, keepdims=True)
        gdy = dy * gv
        dx = r * (gdy - xh * jnp.mean(xh * gdy, axis=-1, keepdims=True))
        dx_ref[...] = dx
        dxl_ref[...] = dx.astype(dxl_ref.dtype)

    row = pl.BlockSpec((tb, Dm), lambda i: (i, 0))
    vec = pl.BlockSpec((1, Dm), lambda i: (0, 0))
    return pl.pallas_call(
        body, name=name, grid=(S // tb,), in_specs=[row, vec, row],
        out_specs=[pl.BlockSpec((1, 128), lambda i: (0, 0)), row, row, vec],
        out_shape=[SDS((1, 128), F32), SDS((S, Dm), F32), SDS((S, Dm), _MXU), SDS((1, Dm), F32)],
        compiler_params=_cp("arbitrary"))(x, g, tgt)


SSD_HALO = 8
CF_HALO = 32
CONV_CB = 512


def _ssd_conv_fwd(proj, w, b, name="ssd_conv_fwd", tb_cap=512):
    S = proj.shape[0]
    tb = _tile(S, tb_cap, 8)
    nb = S // tb
    c0 = COL_XBC // CONV_CB

    def body(x_ref, w_ref, b_ref, o_ref, ext):
        @pl.when(pl.program_id(1) == 0)
        def _():
            ext[pl.ds(0, SSD_HALO), :] = jnp.zeros((SSD_HALO, CONV_CB), F32)

        ext[pl.ds(SSD_HALO, tb), :] = x_ref[...]
        acc = jnp.zeros((tb, CONV_CB), F32) + b_ref[...]
        for k in range(KS):
            acc = acc + ext[pl.ds(SSD_HALO - (KS - 1) + k, tb), :] * w_ref[k:k + 1, :]
        o_ref[...] = acc
        ext[pl.ds(0, SSD_HALO), :] = ext[pl.ds(tb, SSD_HALO), :]

    return pl.pallas_call(
        body, name=name, grid=(XBC // CONV_CB, nb),
        in_specs=[pl.BlockSpec((tb, CONV_CB), lambda j, i: (i, c0 + j)),
                  pl.BlockSpec((KS, CONV_CB), lambda j, i: (0, j)),
                  pl.BlockSpec((1, CONV_CB), lambda j, i: (0, j))],
        out_specs=pl.BlockSpec((tb, CONV_CB), lambda j, i: (i, j)), out_shape=SDS((S, XBC), F32),
        scratch_shapes=[pltpu.VMEM((SSD_HALO + tb, CONV_CB), F32)],
        compiler_params=_cp("parallel", "arbitrary"))(proj, w, b)


def _ssd_conv_bwd(dxbc, proj, w, name="ssd_conv_bwd", tb_cap=512):
    S = proj.shape[0]
    tb = _tile(S, tb_cap, 8)
    nb = S // tb
    c0 = COL_XBC // CONV_CB

    def body(dy_ref, x_ref, w_ref, dx_ref, dw_ref, db_ref, ext):
        @pl.when(pl.program_id(1) == 0)
        def _():
            ext[pl.ds(tb, SSD_HALO), :] = jnp.zeros((SSD_HALO, CONV_CB), F32)
            dw_ref[...] = jnp.zeros_like(dw_ref)
            db_ref[...] = jnp.zeros_like(db_ref)

        dy = dy_ref[...]
        ext[pl.ds(0, tb), :] = dy
        xv = x_ref[...]
        acc = jnp.zeros((tb, CONV_CB), F32)
        for k in range(KS):
            sh = ext[pl.ds(KS - 1 - k, tb), :]
            acc = acc + sh * w_ref[k:k + 1, :]
            dw_ref[k:k + 1, :] += jnp.sum(xv * sh, axis=0, keepdims=True)
        db_ref[...] += jnp.sum(dy, axis=0, keepdims=True)
        dx_ref[...] = acc.astype(dx_ref.dtype)
        ext[pl.ds(tb, SSD_HALO), :] = ext[pl.ds(0, SSD_HALO), :]

    return pl.pallas_call(
        body, name=name, grid=(XBC // CONV_CB, nb),
        in_specs=[pl.BlockSpec((tb, CONV_CB), lambda j, i: (nb - 1 - i, j)),
                  pl.BlockSpec((tb, CONV_CB), lambda j, i: (nb - 1 - i, c0 + j)),
                  pl.BlockSpec((KS, CONV_CB), lambda j, i: (0, j))],
        out_specs=[pl.BlockSpec((tb, CONV_CB), lambda j, i: (nb - 1 - i, j)),
                   pl.BlockSpec((KS, CONV_CB), lambda j, i: (0, j)),
                   pl.BlockSpec((1, CONV_CB), lambda j, i: (0, j))],
        out_shape=[SDS((S, XBC), _MXU), SDS((KS, XBC), F32), SDS((1, XBC), F32)],
        scratch_shapes=[pltpu.VMEM((tb + SSD_HALO, CONV_CB), F32)],
        compiler_params=_cp("parallel", "arbitrary"))(dxbc, proj, w)


def _head_consts():
    e = (lax.broadcasted_iota(jnp.int32, (128, NH * HP), 1) // HP == lax.broadcasted_iota(jnp.int32, (128, NH * HP), 0)).astype(F32)
    et = (lax.broadcasted_iota(jnp.int32, (NH * HP, 128), 0) // HP == lax.broadcasted_iota(jnp.int32, (NH * HP, 128), 1)).astype(F32)
    r = lax.broadcasted_iota(jnp.int32, (CH, CH), 0)
    c = lax.broadcasted_iota(jnp.int32, (CH, CH), 1)
    return e, et, (c <= r), (r <= c)


def _ssd_common(xbc_c, dtr, dtb, alog, e, tril, triu):
    xbc = _silu(xbc_c)
    xs = xbc[:, :NH * HP]
    dt = _softplus(dtr + dtb)
    A = -jnp.exp(alog)
    a = dt * A
    cs = _dot(tril.astype(F32), a, hi=True)
    csT = _dot(a, triu.astype(F32), "tn", hi=True)
    csL = cs[CH - 1:CH, :]
    wdec = jnp.exp(csL - cs) * dt
    dtE = _dot(dt, e, hi=True)
    ecsE = _dot(jnp.exp(cs), e, hi=True)
    wE = _dot(wdec, e, hi=True)
    eL = jnp.exp(csL)
    return xbc, xs, dt, A, cs, csT, csL, wdec, dtE, ecsE, wE, eL


def _ssd_fwd(proj, xbc_c, dtr, sc, norm_g, name="ssd_fwd"):
    S = proj.shape[0]
    nc = S // CH

    def body(z_ref, x_ref, dtr_ref, sc_ref, ng_ref, y_ref, yn_ref, hp_ref, hst):
        @pl.when(pl.program_id(0) == 0)
        def _():
            hst[...] = jnp.zeros_like(hst)

        e, et, tril, triu = _head_consts()
        xbc, xs, dt, A, cs, csT, csL, wdec, dtE, ecsE, wE, eL = _ssd_common(
            x_ref[...], dtr_ref[...], sc_ref[0:1, :], sc_ref[1:2, :], e, tril, triu)
        hp_ref[0] = hst[...]
        xd = xs * dtE
        xw = xs * wE
        dE = _dot(jnp.broadcast_to(sc_ref[2:3, :], (8, 128)), e, hi=True)[0:1, :]
        eLcol = jnp.sum(et * eL, axis=1, keepdims=True)
        for g in range(NG):
            Bg = xbc[:, NH * HP + g * NS: NH * HP + (g + 1) * NS]
            Cg = xbc[:, NH * HP + NG * NS + g * NS: NH * HP + NG * NS + (g + 1) * NS]
            gs = slice(g * GW, (g + 1) * GW)
            G = _dot(Cg, Bg, "nt")
            hg = hst[gs, :]
            yoff = ecsE[:, gs] * _dot(Cg, hg, "nt")
            hst[gs, :] = eLcol[gs, :] * hg + _dot(xw[:, gs], Bg, "tn")
            for hh in range(NH // NG):
                h = g * (NH // NG) + hh
                hs = slice(h * HP, (h + 1) * HP)
                m = jnp.where(tril, jnp.exp(jnp.where(tril, cs[:, h:h + 1] - csT[h:h + 1, :], 0.0)), 0.0)
                yd = _dot(G * m, xd[:, hs])
                y_ref[:, hs] = yd + yoff[:, hh * HP:(hh + 1) * HP] + dE[:, hs] * xs[:, hs]
        y = y_ref[...]
        yz = y * _silu(z_ref[...])
        for g in range(NG):
            gs = slice(g * GW, (g + 1) * GW)
            yg = yz[:, gs]
            r = lax.rsqrt(jnp.mean(yg * yg, axis=-1, keepdims=True) + EPS)
            yn_ref[:, gs] = (yg * r * ng_ref[:, gs]).astype(yn_ref.dtype)

    return pl.pallas_call(
        body, name=name, grid=(nc,),
        in_specs=[pl.BlockSpec((CH, D), lambda c: (c, COL_Z // D)),
                  pl.BlockSpec((CH, XBC), lambda c: (c, 0)),
                  pl.BlockSpec((CH, 128), lambda c: (c, 0)),
                  pl.BlockSpec((8, 128), lambda c: (0, 0)),
                  pl.BlockSpec((1, D), lambda c: (0, 0))],
        out_specs=[pl.BlockSpec((CH, D), lambda c: (c, 0)), pl.BlockSpec((CH, D), lambda c: (c, 0)),
                   pl.BlockSpec((1, NH * HP, NS), lambda c: (c, 0, 0))],
        out_shape=[SDS((S, D), F32), SDS((S, D), _MXU), SDS((nc, NH * HP, NS), F32)],
        scratch_shapes=[pltpu.VMEM((NH * HP, NS), F32)],
        compiler_params=_cp("arbitrary"))(proj, xbc_c, dtr, sc, norm_g)


def _ssd_bwd(dmix, y, proj, xbc_c, dtr, hprev, sc, norm_g, name="ssd_bwd"):
    S = proj.shape[0]
    nc = S // CH
    rev = lambda c: nc - 1 - c

    def body(dyn_ref, y_ref, z_ref, x_ref, dtr_ref, hp_ref, sc_ref, ng_ref,
             dz_ref, dx_ref, ddtr_ref, gsc_ref, gng_ref, dh, dxd):
        @pl.when(pl.program_id(0) == 0)
        def _():
            dh[...] = jnp.zeros_like(dh)
            gsc_ref[...] = jnp.zeros_like(gsc_ref)
            gng_ref[...] = jnp.zeros_like(gng_ref)

        e, et, tril, triu = _head_consts()
        xbc_c = x_ref[...]
        dtr = dtr_ref[...]
        dtb = sc_ref[0:1, :]
        xbc, xs, dt, A, cs, csT, csL, wdec, dtE, ecsE, wE, eL = _ssd_common(
            xbc_c, dtr, dtb, sc_ref[1:2, :], e, tril, triu)
        xd = xs * dtE
        xw = xs * wE
        dE = _dot(jnp.broadcast_to(sc_ref[2:3, :], (8, 128)), e, hi=True)[0:1, :]
        eLcol = jnp.sum(et * eL, axis=1, keepdims=True)

        yv = y_ref[...]
        zv = z_ref[...]
        sz = _silu(zv)
        yz = yv * sz
        dyn = dyn_ref[...]
        dyz_parts = []
        for g in range(NG):
            gs = slice(g * GW, (g + 1) * GW)
            yg = yz[:, gs]
            r = lax.rsqrt(jnp.mean(yg * yg, axis=-1, keepdims=True) + EPS)
            yh = yg * r
            dn = dyn[:, gs]
            gng_ref[:, gs] += jnp.sum(dn * yh, axis=0, keepdims=True)
            gdn = dn * ng_ref[:, gs]
            dyz_parts.append(r * (gdn - yh * jnp.mean(yh * gdn, axis=-1, keepdims=True)))
        dyz = jnp.concatenate(dyz_parts, axis=1)
        dy = dyz * sz
        dz_ref[...] = (dyz * yv * _dsilu(zv)).astype(dz_ref.dtype)

        gsc_ref[2:3, :] += jnp.sum(_dot(dy * xs, et, hi=True), axis=0, keepdims=True)
        dxs = dE * dy
        dzo = ecsE * dy
        dcs = jnp.zeros((CH, 128), F32)
        dcsL = jnp.zeros((1, 128), F32)
        ddt = jnp.zeros((CH, 128), F32)
        dB_parts, dC_parts, yoff_parts, dxw_parts = [], [], [], []
        for g in range(NG):
            Bg = xbc[:, NH * HP + g * NS: NH * HP + (g + 1) * NS]
            Cg = xbc[:, NH * HP + NG * NS + g * NS: NH * HP + NG * NS + (g + 1) * NS]
            gs = slice(g * GW, (g + 1) * GW)
            hg = hp_ref[0, gs, :]
            dhn = dh[gs, :]
            G = _dot(Cg, Bg, "nt")
            yoff_parts.append(ecsE[:, gs] * _dot(Cg, hg, "nt"))
            dC = _dot(dzo[:, gs], hg)
            dhp = _dot(dzo[:, gs], Cg, "tn") + eLcol[gs, :] * dhn
            t1 = jnp.sum(dhn * hg, axis=1, keepdims=True) * eLcol[gs, :]
            dcsL = dcsL + jnp.sum(et[gs, :] * t1, axis=0, keepdims=True)
            dxw_parts.append(_dot(Bg, dhn, "nt"))
            dB = _dot(xw[:, gs], dhn)
            dgsum = jnp.zeros((CH, CH), F32)
            for hh in range(NH // NG):
                h = g * (NH // NG) + hh
                hs = slice(h * HP, (h + 1) * HP)
                m = jnp.where(tril, jnp.exp(jnp.where(tril, cs[:, h:h + 1] - csT[h:h + 1, :], 0.0)), 0.0)
                sc = G * m
                dyh = dy[:, hs]
                dxd[:, hs] = _dot(sc, dyh, "tn")
                dsc = _dot(dyh, xd[:, hs], "nt")
                q = dsc * sc
                oh = (lax.broadcasted_iota(jnp.int32, (CH, 128), 1) == h).astype(F32)
                dcs = dcs + _dot(q, oh, hi=True) - _dot(q, oh, "tn", hi=True)
                dgsum = dgsum + dsc * m
            dC_parts.append(dC + _dot(dgsum, Bg))
            dB_parts.append(dB + _dot(dgsum, Cg, "tn"))
            dh[gs, :] = dhp
        yoff = jnp.concatenate(yoff_parts, axis=1)
        dxw = jnp.concatenate(dxw_parts, axis=1)
        dxdv = dxd[...]
        dcs = dcs + _dot(dy * yoff, et, hi=True)
        dxs = dxs + wE * dxw + dtE * dxdv
        dw = _dot(dxw * xs, et, hi=True)
        ddt = ddt + dw * jnp.exp(csL - cs) + _dot(dxdv * xs, et, hi=True)
        dcs = dcs - dw * wdec
        dcsL = dcsL + jnp.sum(dw * wdec, axis=0, keepdims=True)
        last = lax.broadcasted_iota(jnp.int32, (CH, 128), 0) == CH - 1
        dcs = dcs + jnp.where(last, dcsL, 0.0)
        da = _dot(triu.astype(F32), dcs, hi=True)
        ddt = ddt + da * A
        gsc_ref[1:2, :] += jnp.sum(da * dt, axis=0, keepdims=True) * A
        valid = lax.broadcasted_iota(jnp.int32, (CH, 128), 1) < NH
        ddtr = jnp.where(valid, ddt * _sigmoid(dtr + dtb), 0.0)
        gsc_ref[0:1, :] += jnp.sum(ddtr, axis=0, keepdims=True)
        ddtr_ref[...] = ddtr.astype(ddtr_ref.dtype)
        dxbc = jnp.concatenate([dxs] + dB_parts + dC_parts, axis=1)
        dx_ref[...] = dxbc * _dsilu(xbc_c)

    vec = pl.BlockSpec((8, 128), lambda c: (0, 0))
    vecd = pl.BlockSpec((1, D), lambda c: (0, 0))
    row = lambda w, j=0: pl.BlockSpec((CH, w), lambda c: (rev(c), j))
    return pl.pallas_call(
        body, name=name, grid=(nc,),
        in_specs=[row(D), row(D), row(D, COL_Z // D), row(XBC), row(128),
                  pl.BlockSpec((1, NH * HP, NS), lambda c: (rev(c), 0, 0)), vec, vecd],
        out_specs=[row(D), row(XBC), row(128), vec, vecd],
        out_shape=[SDS((S, D), _MXU), SDS((S, XBC), F32), SDS((S, 128), _MXU), SDS((8, 128), F32), SDS((1, D), F32)],
        scratch_shapes=[pltpu.VMEM((NH * HP, NS), F32), pltpu.VMEM((CH, NH * HP), F32)],
        compiler_params=_cp("arbitrary"))(dmix, y, proj, xbc_c, dtr, hprev, sc, norm_g)


def _cf_fwd(proj, w, b, lg, lb, name="cf_fwd", tb_cap=256):
    S = proj.shape[0]
    tb = _tile(S, tb_cap, 8)

    def body(a_ref, g_ref, w_ref, b_ref, lg_ref, lb_ref, u1_ref, u_ref, ext):
        @pl.when(pl.program_id(0) == 0)
        def _():
            ext[pl.ds(0, CF_HALO), :] = jnp.zeros((CF_HALO, D), F32)

        ext[pl.ds(CF_HALO, tb), :] = a_ref[...] * _sigmoid(g_ref[...])
        acc = jnp.zeros((tb, D), F32) + b_ref[...]
        for k in range(KC):
            acc = acc + ext[pl.ds(CF_HALO - (KC - 1) + k, tb), :] * w_ref[k:k + 1, :]
        u1_ref[...] = acc
        mu = jnp.mean(acc, axis=-1, keepdims=True)
        xc = acc - mu
        r = lax.rsqrt(jnp.mean(xc * xc, axis=-1, keepdims=True) + EPS)
        u_ref[...] = _silu(xc * r * lg_ref[...] + lb_ref[...]).astype(u_ref.dtype)
        ext[pl.ds(0, CF_HALO), :] = ext[pl.ds(tb, CF_HALO), :]

    vec = pl.BlockSpec((1, D), lambda i: (0, 0))
    return pl.pallas_call(
        body, name=name, grid=(S // tb,),
        in_specs=[pl.BlockSpec((tb, D), lambda i: (i, COL_A // D)), pl.BlockSpec((tb, D), lambda i: (i, COL_G // D)),
                  pl.BlockSpec((KC, D), lambda i: (0, 0)), vec, vec, vec],
        out_specs=[pl.BlockSpec((tb, D), lambda i: (i, 0)), pl.BlockSpec((tb, D), lambda i: (i, 0))],
        out_shape=[SDS((S, D), F32), SDS((S, D), _MXU)],
        scratch_shapes=[pltpu.VMEM((CF_HALO + tb, D), F32)],
        compiler_params=_cp("arbitrary"))(proj, proj, w, b, lg, lb)


def _cf_bwd(dmix, u1, proj, w, lg, lb, name="cf_bwd", tb_cap=256):
    S = proj.shape[0]
    tb = _tile(S, tb_cap, 8)
    nb = S // tb
    rev = lambda i: nb - 1 - i

    def body(du_ref, u1_ref, a_ref, g_ref, w_ref, lg_ref, lb_ref,
             da_ref, dg_ref, dw_ref, db_ref, dlg_ref, dlb_ref, ext):
        @pl.when(pl.program_id(0) == 0)
        def _():
            ext[pl.ds(tb, CF_HALO), :] = jnp.zeros((CF_HALO, D), F32)
            dw_ref[...] = jnp.zeros_like(dw_ref)
            db_ref[...] = jnp.zeros_like(db_ref)
            dlg_ref[...] = jnp.zeros_like(dlg_ref)
            dlb_ref[...] = jnp.zeros_like(dlb_ref)

        u1 = u1_ref[...]
        mu = jnp.mean(u1, axis=-1, keepdims=True)
        xc = u1 - mu
        r = lax.rsqrt(jnp.mean(xc * xc, axis=-1, keepdims=True) + EPS)
        xh = xc * r
        lgv = lg_ref[...]
        du2 = du_ref[...] * _dsilu(xh * lgv + lb_ref[...])
        dlg_ref[...] += jnp.sum(du2 * xh, axis=0, keepdims=True)
        dlb_ref[...] += jnp.sum(du2, axis=0, keepdims=True)
        gd = du2 * lgv
        du1 = r * (gd - jnp.mean(gd, axis=-1, keepdims=True) - xh * jnp.mean(gd * xh, axis=-1, keepdims=True))
        db_ref[...] += jnp.sum(du1, axis=0, keepdims=True)
        ext[pl.ds(0, tb), :] = du1
        av = a_ref[...]
        sg = _sigmoid(g_ref[...])
        u0 = av * sg
        acc = jnp.zeros((tb, D), F32)
        for k in range(KC):
            sh = ext[pl.ds(KC - 1 - k, tb), :]
            acc = acc + sh * w_ref[k:k + 1, :]
            dw_ref[k:k + 1, :] += jnp.sum(u0 * sh, axis=0, keepdims=True)
        da_ref[...] = (acc * sg).astype(da_ref.dtype)
        dg_ref[...] = (acc * av * sg * (1.0 - sg)).astype(dg_ref.dtype)
        ext[pl.ds(tb, CF_HALO), :] = ext[pl.ds(0, CF_HALO), :]

    vec = pl.BlockSpec((1, D), lambda i: (0, 0))
    wsp = pl.BlockSpec((KC, D), lambda i: (0, 0))
    row = lambda j=0: pl.BlockSpec((tb, D), lambda i: (rev(i), j))
    return pl.pallas_call(
        body, name=name, grid=(nb,),
        in_specs=[row(1), row(), row(COL_A // D), row(COL_G // D), wsp, vec, vec],
        out_specs=[row(), row(), wsp, vec, vec, vec],
        out_shape=[SDS((S, D), _MXU), SDS((S, D), _MXU), SDS((KC, D), F32),
                   SDS((1, D), F32), SDS((1, D), F32), SDS((1, D), F32)],
        scratch_shapes=[pltpu.VMEM((tb + CF_HALO, D), F32)],
        compiler_params=_cp("arbitrary"))(dmix, u1, proj, proj, w, lg, lb)


def _attn_fwd(q, kv, name="attn_fwd", tq_cap=512):
    S = q.shape[0]
    tq = _tile(S, tq_cap, 8)
    scale = XD ** -0.5

    def body(q_ref, kv_ref, o_ref):
        for h in range(XH):
            hs = slice(h * XD, (h + 1) * XD)
            s = _dot(q_ref[:, hs], kv_ref[:, hs], "nt") * scale
            s = s - jnp.max(s, axis=-1, keepdims=True)
            p = jnp.exp(s)
            p = p / jnp.sum(p, axis=-1, keepdims=True)
            o_ref[:, hs] = _dot(p, kv_ref[:, D + h * XD: D + (h + 1) * XD]).astype(o_ref.dtype)

    return pl.pallas_call(
        body, name=name, grid=(S // tq,),
        in_specs=[pl.BlockSpec((tq, D), lambda i: (i, 0)), pl.BlockSpec((MEM, 2 * D), lambda i: (0, 0))],
        out_specs=pl.BlockSpec((tq, D), lambda i: (i, 0)), out_shape=SDS((S, D), _MXU),
        compiler_params=_cp("parallel"))(q, kv)


def _attn_bwd(do, q, kv, name="attn_bwd", tq_cap=512):
    S = q.shape[0]
    tq = _tile(S, tq_cap, 8)
    scale = XD ** -0.5

    def body(do_ref, q_ref, kv_ref, dq_ref, dkv_ref):
        @pl.when(pl.program_id(0) == 0)
        def _():
            dkv_ref[...] = jnp.zeros_like(dkv_ref)

        for h in range(XH):
            hs = slice(h * XD, (h + 1) * XD)
            vs = slice(D + h * XD, D + (h + 1) * XD)
            qh = q_ref[:, hs]
            kh = kv_ref[:, hs]
            s = _dot(qh, kh, "nt") * scale
            s = s - jnp.max(s, axis=-1, keepdims=True)
            p = jnp.exp(s)
            p = p / jnp.sum(p, axis=-1, keepdims=True)
            doh = do_ref[:, hs]
            dp = _dot(doh, kv_ref[:, vs], "nt")
            ds = p * (dp - jnp.sum(dp * p, axis=-1, keepdims=True)) * scale
            dq_ref[:, hs] = _dot(ds, kh).astype(dq_ref.dtype)
            dkv_ref[:, hs] += _dot(ds, qh, "tn")
            dkv_ref[:, vs] += _dot(p, doh, "tn")

    return pl.pallas_call(
        body, name=name, grid=(S // tq,),
        in_specs=[pl.BlockSpec((tq, D), lambda i: (i, 0)), pl.BlockSpec((tq, D), lambda i: (i, 0)),
                  pl.BlockSpec((MEM, 2 * D), lambda i: (0, 0))],
        out_specs=[pl.BlockSpec((tq, D), lambda i: (i, 0)), pl.BlockSpec((MEM, 2 * D), lambda i: (0, 0))],
        out_shape=[SDS((S, D), _MXU), SDS((MEM, 2 * D), F32)],
        compiler_params=_cp("arbitrary"))(do, q, kv)


def _ffn_act(gu, name="ffn_act", tb_cap=512):
    S = gu.shape[0]
    tb = _tile(S, tb_cap, 8)

    def body(gu_ref, o_ref):
        o_ref[...] = (_silu(gu_ref[:, :FB]) * gu_ref[:, FB:]).astype(o_ref.dtype)

    return pl.pallas_call(
        body, name=name, grid=(S // tb, DFF // FB),
        in_specs=[pl.BlockSpec((tb, 2 * FB), lambda i, j: (i, j))],
        out_specs=pl.BlockSpec((tb, FB), lambda i, j: (i, j)), out_shape=SDS((S, DFF), _MXU),
        compiler_params=_cp("parallel", "parallel"))(gu)


def _ffn_act_bwd(dact, gu, name="ffn_act_bwd", tb_cap=512):
    S = gu.shape[0]
    tb = _tile(S, tb_cap, 8)

    def body(d_ref, gu_ref, o_ref):
        gt = gu_ref[:, :FB]
        d = d_ref[...]
        o_ref[:, :FB] = (d * gu_ref[:, FB:] * _dsilu(gt)).astype(o_ref.dtype)
        o_ref[:, FB:] = (d * _silu(gt)).astype(o_ref.dtype)

    return pl.pallas_call(
        body, name=name, grid=(S // tb, DFF // FB),
        in_specs=[pl.BlockSpec((tb, FB), lambda i, j: (i, j)), pl.BlockSpec((tb, 2 * FB), lambda i, j: (i, j))],
        out_specs=pl.BlockSpec((tb, 2 * FB), lambda i, j: (i, j)), out_shape=SDS((S, 2 * DFF), _MXU),
        compiler_params=_cp("parallel", "parallel"))(dact, gu)


def _local_step(x, mem, tgt, W, P):
    h = _rms_fwd(x, P["g_mix"], "rms_mix")
    proj = _mm_nn(h, W["main"], "in_proj", tn_cap=1152)
    dtr = _mm_nn(h, W["dt"], "in_proj_dt")
    xbc_c = _ssd_conv_fwd(proj, P["conv4_w"], P["conv4_b"])
    y, yn, hprev = _ssd_fwd(proj, xbc_c, dtr, P["sc"], P["ssd_norm_g"])
    u1, u = _cf_fwd(proj, P["cf_w"], P["cf_b"], P["ln_g"], P["ln_b"])
    mix = jnp.concatenate([yn, u], axis=1)
    x1 = _mm_nn(mix, W["out"], "out_proj", add=x)
    hq = _rms_fwd(x1, P["g_xattn"], "rms_xattn")
    q = _mm_nn(hq, W["q"], "q_proj")
    mn = _rms_fwd(mem, P["g_mem"], "rms_mem")
    kv = _mm_nn(mn, W["kv"], "kv_proj")
    o = _attn_fwd(q, kv)
    x2 = _mm_nn(o, W["o"], "o_proj", add=x1)
    hf = _rms_fwd(x2, P["g_ffn"], "rms_ffn")
    gu = _mm_nn(hf, W["gu"], "ffn_in")
    act = _ffn_act(gu)
    x3 = _mm_nn(act, W["down"], "ffn_out", add=x2)
    loss, dx3, dx3b, g_final = _final_loss(x3, P["g_final"], tgt)
    GW, GP = {}, {"g_final": g_final}
    dact = _mm_nt(dx3b, W["down"], "ffn_out_dx", tk_cap=1408)
    GW["down"] = _mm_tn(act, dx3b, "ffn_out_dw", tk_cap=1408, tn_cap=1024)
    dgu = _ffn_act_bwd(dact, gu)
    dhf = _mm_nt(dgu, W["gu"], "ffn_in_dx", tk_cap=512)
    GW["gu"] = _mm_tn(hf, dgu, "ffn_in_dw")
    dx2, dx2b, GP["g_ffn"] = _rms_bwd(x2, P["g_ffn"], dhf, dx3, "rms_ffn_bwd")
    do = _mm_nt(dx2b, W["o"], "o_proj_dx")
    GW["o"] = _mm_tn(o, dx2b, "o_proj_dw")
    dq, dkv = _attn_bwd(do, q, kv)
    dhq = _mm_nt(dq, W["q"], "q_proj_dx")
    GW["q"] = _mm_tn(hq, dq, "q_proj_dw")
    dkvb = dkv.astype(_MXU)
    GW["kv"] = _mm_tn(mn, dkvb, "kv_proj_dw", tm_cap=256)
    dmn = _mm_nt(dkvb, W["kv"], "kv_proj_dx")
    GP["g_mem"] = _rms_bwd(mem, P["g_mem"], dmn, None, "rms_mem_bwd")
    dx1, dx1b, GP["g_xattn"] = _rms_bwd(x1, P["g_xattn"], dhq, dx2, "rms_xattn_bwd")
    dmix = _mm_nt(dx1b, W["out"], "out_proj_dx")
    GW["out"] = _mm_tn(mix, dx1b, "out_proj_dw", tn_cap=1024)
    da, dg, GP["cf_w"], GP["cf_b"], GP["ln_g"], GP["ln_b"] = _cf_bwd(dmix, u1, proj, P["cf_w"], P["ln_g"], P["ln_b"])
    dz, dxbc_c, ddtr, GP["sc"], GP["ssd_norm_g"] = _ssd_bwd(dmix, y, proj, xbc_c, dtr, hprev, P["sc"], P["ssd_norm_g"])
    dxbc, GP["conv4_w"], GP["conv4_b"] = _ssd_conv_bwd(dxbc_c, proj, P["conv4_w"])
    dproj = jnp.concatenate([dz, da, dg, dxbc], axis=1)
    dh = _mm_nt(ddtr, W["dt"], "in_proj_dt_dx")
    dh = _mm_nt(dproj, W["main"], "in_proj_dx", add=dh, tk_cap=512)
    GW["main"] = _mm_tn(h, dproj, "in_proj_dw", tn_cap=1152)
    GW["dt"] = _mm_tn(h, ddtr, "in_proj_dt_dw")
    grad_x, GP["g_mix"] = _rms_bwd(x, P["g_mix"], dh, dx1, "rms_mix_bwd", low=False)
    return loss, grad_x, GW, GP


Z_END, XBC_END, DT_END = NH * HP, NH * HP + XBC, NH * HP + XBC + NH
NFB = DFF // FB


def _pad_to(a, rows=None, cols=None):
    r = 0 if rows is None else rows - a.shape[0]
    c = 0 if cols is None else cols - a.shape[1]
    return jnp.pad(a, ((0, r), (0, c)))


IN_W = DT_END + 2 * D
W_IN_SEGS = [(0, Z_END, "main", COL_Z), (Z_END, XBC_END, "main", COL_XBC), (XBC_END, DT_END, "dt", 0),
             (DT_END, DT_END + D, "main", COL_A), (DT_END + D, IN_W, "main", COL_G)]
BIG = [("w_in", True), ("w_out", False), ("w_q", False), ("w_kv", True), ("w_o", False), ("w_gate", True),
       ("w_up", True), ("w_down", False)]


def _ref_cols(pieces, a, b):
    cw = IN_W // 4
    out = []
    for j in range(4):
        lo, hi = max(a, j * cw), min(b, (j + 1) * cw)
        if lo < hi:
            out.append(pieces[j][:, lo - j * cw:hi - j * cw])
    return out


def _cat_cols(pieces):
    return jnp.concatenate([pieces[j] for j in range(4)], axis=1)


def _pack_weights(pc):
    w_in = pc["w_in"]
    main = jnp.concatenate(_ref_cols(w_in, 0, Z_END) + _ref_cols(w_in, DT_END, IN_W) + _ref_cols(w_in, Z_END, XBC_END), axis=1)
    dt = _pad_to(jnp.concatenate(_ref_cols(w_in, XBC_END, DT_END), axis=1), cols=128)
    gate, up = _cat_cols(pc["w_gate"]), _cat_cols(pc["w_up"])
    gu = jnp.stack([gate.reshape(D, NFB, FB), up.reshape(D, NFB, FB)], axis=2).reshape(D, 2 * DFF)
    rows = lambda n: pc[n].reshape(-1, pc[n].shape[-1])
    return {"main": main, "dt": dt, "out": rows("w_out"), "q": rows("w_q"), "kv": _cat_cols(pc["w_kv"]),
            "o": rows("w_o"), "gu": gu, "down": rows("w_down")}


def _shard_grads(GW):
    cw = IN_W // 4
    pieces = []
    for j in range(4):
        parts = []
        for a, b, src, col in W_IN_SEGS:
            lo, hi = max(a, j * cw), min(b, (j + 1) * cw)
            if lo < hi:
                parts.append(GW[src][:, col + lo - a:col + hi - a])
        pieces.append(jnp.concatenate(parts, axis=1))
    g = {"w_in": jnp.stack(pieces)}
    gu = GW["gu"].reshape(D, NFB, 2, FB)
    for k, n in ((0, "w_gate"), (1, "w_up")):
        full = gu[:, :, k, :].reshape(D, DFF)
        g[n] = jnp.stack([full[:, j * (DFF // 4):(j + 1) * (DFF // 4)] for j in range(4)])
    kv = GW["kv"]
    g["w_kv"] = jnp.stack([kv[:, j * (D // 2):(j + 1) * (D // 2)] for j in range(4)])
    for k, n in (("out", "w_out"), ("q", "w_q"), ("o", "w_o"), ("down", "w_down")):
        g[n] = GW[k].reshape(4, GW[k].shape[0] // 4, GW[k].shape[1])
    return g


def _stack_sc(dt_bias, a_log, d):
    return _pad_to(jnp.concatenate([dt_bias, a_log, d], axis=0), rows=8, cols=128)


HBM_SPEC = pl.BlockSpec(memory_space=pl.ANY)
COMM_PARAMS = pltpu.CompilerParams(vmem_limit_bytes=VMEM_LIMIT)


def _chip_peers(x, y):
    return [(1 - x, y), (x, 1 - y), (1 - x, 1 - y)]


def _remote(src, dst, send_sem, recv_sem, dev):
    return pltpu.make_async_remote_copy(src_ref=src, dst_ref=dst, send_sem=send_sem, recv_sem=recv_sem,
                                        device_id=dev, device_id_type=MESHID)


def _dma_sems(*counts):
    return [pltpu.SemaphoreType.DMA((n,)) for n in counts]


def _allgather_list(arrs, name):
    n = len(arrs)

    def body(*refs):
        srcs, outs = refs[:n], refs[n:2 * n]
        send_sems, recv_sems, local_sems = refs[2 * n:]
        x, y, c = lax.axis_index("x"), lax.axis_index("y"), lax.axis_index("c")
        me = 2 * x + y
        peers = _chip_peers(x, y)
        local = [pltpu.make_async_copy(srcs[i], outs[i].at[me], local_sems.at[i]) for i in range(n)]
        sends = [_remote(srcs[i], outs[i].at[me], send_sems.at[3 * i + k], recv_sems.at[3 * i + k], (px, py, c))
                 for i in range(n) for k, (px, py) in enumerate(peers)]
        for cp in local + sends:
            cp.start()
        for i in range(n):
            for k, (px, py) in enumerate(peers):
                _remote(srcs[i], outs[i].at[2 * px + py], send_sems.at[3 * i + k], recv_sems.at[3 * i + k],
                        (px, py, c)).wait_recv()
        for cp in sends:
            cp.wait_send()
        for cp in local:
            cp.wait()

    return pl.pallas_call(
        body, name=name, in_specs=[HBM_SPEC] * n, out_specs=[HBM_SPEC] * n,
        out_shape=[SDS((4,) + a.shape, a.dtype) for a in arrs],
        scratch_shapes=_dma_sems(3 * n, 3 * n, n), compiler_params=COMM_PARAMS)(*arrs)


def _pair_split_list(gs, name="rs_pair_send"):
    n = len(gs)

    def body(*refs):
        srcs, outs = refs[:n], refs[n:2 * n]
        send_sems, recv_sems = refs[2 * n:]
        x, y, c = lax.axis_index("x"), lax.axis_index("y"), lax.axis_index("c")
        sib = (x, y, 1 - c)
        sends = [_remote(srcs[i].at[j, 1 - c], outs[i].at[j], send_sems.at[4 * i + j], recv_sems.at[4 * i + j], sib)
                 for i in range(n) for j in range(4)]
        for cp in sends:
            cp.start()
        for cp in sends:
            cp.wait_recv()
        for cp in sends:
            cp.wait_send()

    return pl.pallas_call(
        body, name=name, in_specs=[HBM_SPEC] * n, out_specs=[HBM_SPEC] * n,
        out_shape=[SDS((4,) + g.shape[2:], g.dtype) for g in gs],
        scratch_shapes=_dma_sems(4 * n, 4 * n), compiler_params=COMM_PARAMS)(*gs)


def _scatter_list(ps, name="rs_chip_send"):
    n = len(ps)

    def body(*refs):
        srcs, outs = refs[:n], refs[n:2 * n]
        send_sems, recv_sems, local_sems = refs[2 * n:]
        x, y, c = lax.axis_index("x"), lax.axis_index("y"), lax.axis_index("c")
        me = 2 * x + y
        peers = _chip_peers(x, y)
        local = [pltpu.make_async_copy(srcs[i].at[me], outs[i].at[me], local_sems.at[i]) for i in range(n)]
        sends = [_remote(srcs[i].at[2 * px + py], outs[i].at[me], send_sems.at[3 * i + k], recv_sems.at[3 * i + k],
                         (px, py, c)) for i in range(n) for k, (px, py) in enumerate(peers)]
        for cp in local + sends:
            cp.start()
        for i in range(n):
            for k, (px, py) in enumerate(peers):
                _remote(srcs[i].at[me], outs[i].at[2 * px + py], send_sems.at[3 * i + k], recv_sems.at[3 * i + k],
                        (px, py, c)).wait_recv()
        for cp in sends:
            cp.wait_send()
        for cp in local:
            cp.wait()

    return pl.pallas_call(
        body, name=name, in_specs=[HBM_SPEC] * n, out_specs=[HBM_SPEC] * n,
        out_shape=[SDS(p.shape, p.dtype) for p in ps],
        scratch_shapes=_dma_sems(3 * n, 3 * n, n), compiler_params=COMM_PARAMS)(*ps)


JOIN_SPLIT = 4


def _pair_join_list(ps, name="rs_pair_join"):
    n = len(ps)

    def body(*refs):
        srcs, outs = refs[:n], refs[n:2 * n]
        send_sems, recv_sems, local_sems = refs[2 * n:]
        x, y, c = lax.axis_index("x"), lax.axis_index("y"), lax.axis_index("c")
        sib = (x, y, 1 - c)
        local = [pltpu.make_async_copy(srcs[i], outs[i].at[c], local_sems.at[i]) for i in range(n)]
        sends, recvs = [], []
        for i in range(n):
            rc = ps[i].shape[0] // JOIN_SPLIT
            for q in range(JOIN_SPLIT):
                k = JOIN_SPLIT * i + q
                rows = pl.ds(q * rc, rc)
                sends.append(_remote(srcs[i].at[rows], outs[i].at[c, rows], send_sems.at[k], recv_sems.at[k], sib))
                recvs.append(_remote(srcs[i].at[rows], outs[i].at[1 - c, rows], send_sems.at[k], recv_sems.at[k], sib))
        for cp in local + sends:
            cp.start()
        for cp in recvs:
            cp.wait_recv()
        for cp in sends:
            cp.wait_send()
        for cp in local:
            cp.wait()

    return pl.pallas_call(
        body, name=name, in_specs=[HBM_SPEC] * n, out_specs=[HBM_SPEC] * n,
        out_shape=[SDS((2,) + p.shape, p.dtype) for p in ps],
        scratch_shapes=_dma_sems(JOIN_SPLIT * n, JOIN_SPLIT * n, n), compiler_params=COMM_PARAMS)(*ps)


def _pair_sum(g, theirs, core, name):
    _, _, r, c = g.shape

    def body(core_ref, g_ref, t_ref, o_ref):
        o_ref[...] = (g_ref[...] + t_ref[...]).astype(o_ref.dtype)

    spec = pltpu.PrefetchScalarGridSpec(
        num_scalar_prefetch=1, grid=(4,),
        in_specs=[pl.BlockSpec((None, None, r, c), lambda j, core_ref: (j, core_ref[0], 0, 0)),
                  pl.BlockSpec((None, r, c), lambda j, core_ref: (j, 0, 0))],
        out_specs=pl.BlockSpec((None, r, c), lambda j, core_ref: (j, 0, 0)))
    return pl.pallas_call(body, name=name, grid_spec=spec, out_shape=SDS((4, r, c), BF16),
                          compiler_params=_cp("parallel"))(core, g, theirs)


def _chip_sum(p, name):
    _, r, c = p.shape
    tr = r // 2

    def body(p_ref, o_ref):
        acc = p_ref[0].astype(F32)
        for i in range(1, 4):
            acc = acc + p_ref[i].astype(F32)
        o_ref[...] = acc

    return pl.pallas_call(body, name=name, grid=(r // tr,), in_specs=[pl.BlockSpec((4, tr, c), lambda i: (0, i, 0))],
                          out_specs=pl.BlockSpec((tr, c), lambda i: (i, 0)), out_shape=SDS((r, c), F32),
                          compiler_params=_cp("parallel"))(p)


def _adam_math(w, g, m, v):
    bc1 = 1.0 - ADAM_B1 ** ADAM_STEP
    bc2 = 1.0 - ADAM_B2 ** ADAM_STEP
    mn = ADAM_B1 * m + (1.0 - ADAM_B1) * g
    vn = ADAM_B2 * v + (1.0 - ADAM_B2) * (g * g)
    return -ADAM_LR * ((mn / bc1) / (jnp.sqrt(vn / bc2) + ADAM_EPS) + ADAM_WD * w), mn, vn


PACK_COLS = XBC
PACK = {"g_mix": (0, 1, D), "g_xattn": (1, 1, D), "g_mem": (2, 1, D), "g_ffn": (3, 1, D), "g_final": (4, 1, D),
        "ssd_norm_g": (5, 1, D), "cf_b": (6, 1, D), "ln_g": (7, 1, D), "ln_b": (8, 1, D), "conv4_b": (9, 1, XBC),
        "conv4_w": (10, KS, XBC), "sc": (16, 8, 128), "cf_w": (24, KC, D), "loss": (55, 1, 128)}
PACK_ROWS = 56
SMALL_ADAM = ["g_mix", "g_xattn", "g_mem", "g_ffn", "g_final", "ssd_norm_g", "cf_b", "ln_g", "ln_b", "conv4_b", "sc"]


def _small_allreduce_adamw(grads, wts, mom, var, name="allreduce_small"):
    gk = list(PACK)
    ng, na = len(gk), len(SMALL_ADAM)

    def body(*refs):
        g_in = refs[:ng]
        w_in, m_in, v_in = (refs[ng + i * na: ng + (i + 1) * na] for i in range(3))
        o = refs[ng + 3 * na:]
        g_out = o[:ng]
        d_out, m_out, v_out = (o[ng + i * na: ng + (i + 1) * na] for i in range(3))
        pack, buf, acc, send_sems, recv_sems = o[ng + 3 * na:]
        x, y, c = lax.axis_index("x"), lax.axis_index("y"), lax.axis_index("c")
        me = 4 * x + 2 * y + c
        pack[...] = jnp.zeros_like(pack)
        for i, k in enumerate(gk):
            r0, nr, nc = PACK[k]
            pack[r0:r0 + nr, 0:nc] = g_in[i][...]
        peers = [(x, y, 1 - c)] + [(px, py, pc) for px, py in _chip_peers(x, y) for pc in (c, 1 - c)]
        sends = [_remote(pack, buf.at[me], send_sems.at[k], recv_sems.at[k], dev) for k, dev in enumerate(peers)]
        for cp in sends:
            cp.start()
        buf[me] = pack[...]
        for k, (px, py, pc) in enumerate(peers):
            _remote(pack, buf.at[4 * px + 2 * py + pc], send_sems.at[k], recv_sems.at[k], (px, py, pc)).wait_recv()
        for cp in sends:
            cp.wait_send()
        tot = buf[0]
        for i in range(1, 8):
            tot = tot + buf[i]
        acc[...] = tot
        for i, k in enumerate(gk):
            r0, nr, nc = PACK[k]
            g_out[i][...] = acc[r0:r0 + nr, 0:nc]
        for i, k in enumerate(SMALL_ADAM):
            r0, nr, nc = PACK[k]
            d_out[i][...], m_out[i][...], v_out[i][...] = _adam_math(
                w_in[i][...], acc[r0:r0 + nr, 0:nc], m_in[i][...], v_in[i][...])

    args = [grads[k] for k in gk] + [d[k] for d in (wts, mom, var) for k in SMALL_ADAM]
    shp = lambda k: SDS((PACK[k][1], PACK[k][2]), F32)
    vm = pl.BlockSpec(memory_space=pltpu.VMEM)
    outs = pl.pallas_call(
        body, name=name, in_specs=[vm] * len(args), out_specs=[vm] * (ng + 3 * na),
        out_shape=[shp(k) for k in gk] + [shp(k) for _ in range(3) for k in SMALL_ADAM],
        scratch_shapes=[pltpu.VMEM((PACK_ROWS, PACK_COLS), F32), pltpu.VMEM((8, PACK_ROWS, PACK_COLS), F32),
                        pltpu.VMEM((PACK_ROWS, PACK_COLS), F32)] + _dma_sems(7, 7),
        compiler_params=COMM_PARAMS)(*args)
    red = dict(zip(gk, outs[:ng]))
    parts = [dict(zip(SMALL_ADAM, outs[ng + i * na: ng + (i + 1) * na])) for i in range(3)]
    return red, parts[0], parts[1], parts[2]


def _adamw_cols(w, gfull, m, v, chip, name):
    R, C = w.shape

    def body(chip_ref, w_ref, g_ref, m_ref, v_ref, go_ref, d_ref, mo_ref, vo_ref):
        go_ref[...] = g_ref[...]
        d_ref[...], mo_ref[...], vo_ref[...] = _adam_math(w_ref[...], g_ref[...], m_ref[...], v_ref[...])

    blk = pl.BlockSpec((R, C), lambda i, chip_ref: (0, 0))
    spec = pltpu.PrefetchScalarGridSpec(
        num_scalar_prefetch=1, grid=(1,),
        in_specs=[blk, pl.BlockSpec((R, C), lambda i, chip_ref: (0, chip_ref[0])), blk, blk], out_specs=[blk] * 4)
    return pl.pallas_call(body, name=name, grid_spec=spec, out_shape=[SDS((R, C), F32)] * 4,
                          compiler_params=_cp("arbitrary"))(chip, w, gfull, m, v)


def _adamw(w, g, m, v, name):
    R, C = w.shape
    tr = R if R * C <= 2 ** 17 else _tile(R, max(8, (2 ** 17 // C) // 8 * 8), 8)

    def body(w_ref, g_ref, m_ref, v_ref, d_ref, mo_ref, vo_ref):
        d_ref[...], mo_ref[...], vo_ref[...] = _adam_math(w_ref[...], g_ref[...], m_ref[...], v_ref[...])

    blk = pl.BlockSpec((tr, C), lambda i: (i, 0))
    return pl.pallas_call(body, name=name, grid=(R // tr,), in_specs=[blk] * 4, out_specs=[blk] * 3,
                          out_shape=[SDS((R, C), F32)] * 3, compiler_params=_cp("parallel"))(w, g, m, v)


WEIGHT_NAMES = ["norm_mix_g", "w_in", "ssd_conv_w", "ssd_conv_b", "ssd_dt_bias", "ssd_A_log", "ssd_D", "ssd_norm_g",
                "cf_conv_w", "cf_conv_b", "cf_ln_g", "cf_ln_b", "w_out", "norm_xattn_g", "norm_mem_g", "w_q", "w_kv",
                "w_o", "norm_ffn_g", "w_gate", "w_up", "w_down", "norm_final_g"]
VEC_REF = [("norm_mix_g", "g_mix"), ("norm_xattn_g", "g_xattn"), ("norm_mem_g", "g_mem"), ("norm_ffn_g", "g_ffn"),
           ("norm_final_g", "g_final"), ("ssd_norm_g", "ssd_norm_g"), ("cf_conv_b", "cf_b"), ("cf_ln_g", "ln_g"),
           ("cf_ln_b", "ln_b"), ("ssd_conv_b", "conv4_b")]
SC_REF = ["ssd_dt_bias", "ssd_A_log", "ssd_D"]


def _small_side(get):
    d = {k: get(ref_name).reshape(1, -1) for ref_name, k in VEC_REF}
    d["sc"] = _stack_sc(*[get(n) for n in SC_REF])
    return d


def kernel(x, mem, norm_mix_g, w_in, ssd_conv_w, ssd_conv_b, ssd_dt_bias, ssd_A_log, ssd_D, ssd_norm_g, cf_conv_w, cf_conv_b, cf_ln_g, cf_ln_b, w_out, norm_xattn_g, norm_mem_g, w_q, w_kv, w_o, norm_ffn_g, w_gate, w_up, w_down, norm_final_g, loss_target, m_norm_mix_g, m_w_in, m_ssd_conv_w, m_ssd_conv_b, m_ssd_dt_bias, m_ssd_A_log, m_ssd_D, m_ssd_norm_g, m_cf_conv_w, m_cf_conv_b, m_cf_ln_g, m_cf_ln_b, m_w_out, m_norm_xattn_g, m_norm_mem_g, m_w_q, m_w_kv, m_w_o, m_norm_ffn_g, m_w_gate, m_w_up, m_w_down, m_norm_final_g, v_norm_mix_g, v_w_in, v_ssd_conv_w, v_ssd_conv_b, v_ssd_dt_bias, v_ssd_A_log, v_ssd_D, v_ssd_norm_g, v_cf_conv_w, v_cf_conv_b, v_cf_ln_g, v_cf_ln_b, v_w_out, v_norm_xattn_g, v_norm_mem_g, v_w_q, v_w_kv, v_w_o, v_norm_ffn_g, v_w_gate, v_w_up, v_w_down, v_norm_final_g):
    env = dict(locals())
    wts = {n: env[n] for n in WEIGHT_NAMES}
    mom = {n: env["m_" + n] for n in WEIGHT_NAMES}
    var = {n: env["v_" + n] for n in WEIGHT_NAMES}
    chip = (2 * lax.axis_index("x") + lax.axis_index("y")).astype(jnp.int32).reshape(1)
    core = lax.axis_index("c").astype(jnp.int32).reshape(1)
    big = [n for n, _ in BIG]

    gathered = _allgather_list([wts[n][0].astype(BF16) for n in big] + [ssd_conv_w[0], cf_conv_w[0]], "allgather_weights")
    W = _pack_weights(dict(zip(big, gathered)))
    P = _small_side(lambda n: wts[n])
    P["conv4_w"], P["cf_w"] = _cat_cols(gathered[len(big)]), _cat_cols(gathered[len(big) + 1])

    loss, grad_x, GW, GP = _local_step(x[0], mem[0], loss_target[0], W, P)

    gs = _shard_grads(GW)
    halves = [gs[n].reshape(4, 2, gs[n].shape[1] // 2, gs[n].shape[2]) for n in big]
    theirs = _pair_split_list(halves)
    pieces = _scatter_list([_pair_sum(h, t, core, "rs_pair_sum_" + n) for h, t, n in zip(halves, theirs, big)])
    joined = _pair_join_list([_chip_sum(p, "rs_chip_sum_" + n) for p, n in zip(pieces, big)])
    gshard = {n: j.reshape(wts[n].shape[1:]) for n, j in zip(big, joined)}

    small = dict(GP)
    small["loss"] = loss
    red, sd, sm, sv = _small_allreduce_adamw(small, {k: P[k] for k in SMALL_ADAM}, _small_side(lambda n: mom[n]),
                                             _small_side(lambda n: var[n]))
    grads, delta, new_m, new_v = {}, {}, {}, {}
    for ref_name, k in VEC_REF:
        shp = wts[ref_name].shape
        for dst, src in ((grads, red), (delta, sd), (new_m, sm), (new_v, sv)):
            dst[ref_name] = src[k].reshape(shp)
    for row, ref_name in enumerate(SC_REF):
        for dst, src in ((grads, red), (delta, sd), (new_m, sm), (new_v, sv)):
            dst[ref_name] = src["sc"][row:row + 1, :NH]

    for n, k in (("ssd_conv_w", "conv4_w"), ("cf_conv_w", "cf_w")):
        g_, d_, m_, v_ = _adamw_cols(wts[n][0], red[k], mom[n][0], var[n][0], chip, "adamw_" + n)
        grads[n], delta[n], new_m[n], new_v[n] = g_[None], d_[None], m_[None], v_[None]
    for n in big:
        d_, m_, v_ = _adamw(wts[n][0], gshard[n], mom[n][0], var[n][0], "adamw_" + n)
        grads[n], delta[n], new_m[n], new_v[n] = gshard[n][None], d_[None], m_[None], v_[None]

    return (red["loss"][0, 0], grad_x[None], *[grads[n] for n in WEIGHT_NAMES], *[delta[n] for n in WEIGHT_NAMES],
            *[new_m[n] for n in WEIGHT_NAMES], *[new_v[n] for n in WEIGHT_NAMES])
```

```python
import functools
import math

import jax
import jax.numpy as jnp
from jax import lax
from jax.experimental import pallas as pl
from jax.experimental.pallas import tpu as pltpu

F32 = jnp.float32
BF16 = jnp.bfloat16
_MXU = BF16
_HI = lax.Precision.HIGHEST

D = 1024
MEM = 256
NH, HP, NG, NS = 16, 64, 2, 128
GW = NH * HP // NG
CH = 128
XBC = NH * HP + 2 * NG * NS
KS, KC = 4, 31
XH, XD = 4, 256
DFF = 2816
FB = 256
EPS = 1e-6
COL_Z, COL_A, COL_G, COL_XBC, MAINW = 0, 1024, 2048, 3072, 4608
VMEM_LIMIT = 56 * 2 ** 20

ADAM_LR, ADAM_B1, ADAM_B2, ADAM_EPS, ADAM_WD, ADAM_STEP = 0.001, 0.9, 0.999, 1e-08, 0.01, 10

SDS = jax.ShapeDtypeStruct
MESHID = pl.DeviceIdType.MESH


def _cp(*sem):
    return pltpu.CompilerParams(dimension_semantics=sem, vmem_limit_bytes=VMEM_LIMIT)


def _tile(n, cap, unit=128):
    if n <= cap:
        return n
    best = None
    for t in range(unit, cap + 1, unit):
        if n % t == 0:
            best = t
    assert best is not None, (n, cap)
    return best


def _sigmoid(x):
    return 1.0 / (1.0 + jnp.exp(-x))


def _silu(x):
    return x * _sigmoid(x)


def _dsilu(x):
    s = _sigmoid(x)
    return s * (1.0 + x * (1.0 - s))


def _softplus(x):
    return jnp.maximum(x, 0.0) + jnp.log(1.0 + jnp.exp(-jnp.abs(x)))


def _dot(a, b, dims=None, hi=False):
    dn = {None: (((1,), (0,)), ((), ())), "nt": (((1,), (1,)), ((), ())), "tn": (((0,), (0,)), ((), ()))}[dims]
    if hi:
        return lax.dot_general(a.astype(F32), b.astype(F32), dn, preferred_element_type=F32, precision=_HI)
    return lax.dot_general(a.astype(_MXU), b.astype(_MXU), dn, preferred_element_type=F32)


def _mm_nn(a, b, name, add=None, out_dtype=F32, tm_cap=1024, tn_cap=1408):
    M, K = a.shape
    _, N = b.shape
    tm, tn = _tile(M, tm_cap, 8), _tile(N, tn_cap)

    def body(a_ref, b_ref, *rest):
        o_ref = rest[-1]
        acc = _dot(a_ref[...], b_ref[...])
        if add is not None:
            acc = acc + rest[0][...]
        o_ref[...] = acc.astype(o_ref.dtype)

    in_specs = [pl.BlockSpec((tm, K), lambda j, i: (i, 0)), pl.BlockSpec((K, tn), lambda j, i: (0, j))]
    args = [a, b]
    if add is not None:
        in_specs.append(pl.BlockSpec((tm, tn), lambda j, i: (i, j)))
        args.append(add)
    return pl.pallas_call(
        body, name=name, grid=(N // tn, M // tm), in_specs=in_specs,
        out_specs=pl.BlockSpec((tm, tn), lambda j, i: (i, j)), out_shape=SDS((M, N), out_dtype),
        compiler_params=_cp("parallel", "parallel"))(*args)


def _mm_nt(a, b, name, add=None, out_dtype=F32, tm_cap=512, tk_cap=1024, b_col=0):
    M, N = a.shape
    K = b.shape[0]
    tm, tk = _tile(M, tm_cap, 8), _tile(K, tk_cap)

    def body(a_ref, b_ref, *rest):
        o_ref = rest[-1]
        acc = _dot(a_ref[...], b_ref[...], "nt")
        if add is not None:
            acc = acc + rest[0][...]
        o_ref[...] = acc.astype(o_ref.dtype)

    in_specs = [pl.BlockSpec((tm, N), lambda j, i: (i, 0)), pl.BlockSpec((tk, N), lambda j, i: (j, b_col))]
    args = [a, b]
    if add is not None:
        in_specs.append(pl.BlockSpec((tm, tk), lambda j, i: (i, j)))
        args.append(add)
    return pl.pallas_call(
        body, name=name, grid=(K // tk, M // tm), in_specs=in_specs,
        out_specs=pl.BlockSpec((tm, tk), lambda j, i: (i, j)), out_shape=SDS((M, K), out_dtype),
        compiler_params=_cp("parallel", "parallel"))(*args)


def _mm_tn(a, b, name, tm_cap=1024, tk_cap=512, tn_cap=1408):
    M, K = a.shape
    _, N = b.shape
    tm, tk, tn = _tile(M, tm_cap, 8), _tile(K, tk_cap), _tile(N, tn_cap)

    def body(a_ref, b_ref, o_ref):
        @pl.when(pl.program_id(2) == 0)
        def _():
            o_ref[...] = jnp.zeros_like(o_ref)

        o_ref[...] += _dot(a_ref[...], b_ref[...], "tn")

    return pl.pallas_call(
        body, name=name, grid=(K // tk, N // tn, M // tm),
        in_specs=[pl.BlockSpec((tm, tk), lambda k, n, m: (m, k)), pl.BlockSpec((tm, tn), lambda k, n, m: (m, n))],
        out_specs=pl.BlockSpec((tk, tn), lambda k, n, m: (k, n)), out_shape=SDS((K, N), F32),
        compiler_params=_cp("parallel", "parallel", "arbitrary"))(a, b)


def _rms_fwd(x, g, name, tb_cap=512):
    S, Dm = x.shape
    tb = _tile(S, tb_cap, 8)

    def body(x_ref, g_ref, o_ref):
        xv = x_ref[...]
        r = lax.rsqrt(jnp.mean(xv * xv, axis=-1, keepdims=True) + EPS)
        o_ref[...] = (xv * r * g_ref[...]).astype(o_ref.dtype)

    return pl.pallas_call(
        body, name=name, grid=(S // tb,),
        in_specs=[pl.BlockSpec((tb, Dm), lambda i: (i, 0)), pl.BlockSpec((1, Dm), lambda i: (0, 0))],
        out_specs=pl.BlockSpec((tb, Dm), lambda i: (i, 0)), out_shape=SDS((S, Dm), _MXU),
        compiler_params=_cp("parallel"))(x, g)


def _rms_bwd(x, g, dh, dres, name, tb_cap=512, low=True):
    S, Dm = x.shape
    tb = _tile(S, tb_cap, 8)
    need_dx = dres is not None

    def body(x_ref, g_ref, dh_ref, *rest):
        dg_ref = rest[-1]
        xv = x_ref[...]
        r = lax.rsqrt(jnp.mean(xv * xv, axis=-1, keepdims=True) + EPS)
        xh = xv * r
        dy = dh_ref[...].astype(F32)

        @pl.when(pl.program_id(0) == 0)
        def _():
            dg_ref[...] = jnp.zeros_like(dg_ref)

        dg_ref[...] += jnp.sum(dy * xh, axis=0, keepdims=True)
        if need_dx:
            gdy = dy * g_ref[...]
            dx = r * (gdy - xh * jnp.mean(xh * gdy, axis=-1, keepdims=True))
            tot = rest[0][...] + dx
            rest[1][...] = tot
            if low:
                rest[2][...] = tot.astype(rest[2].dtype)

    row = pl.BlockSpec((tb, Dm), lambda i: (i, 0))
    vec = pl.BlockSpec((1, Dm), lambda i: (0, 0))
    if need_dx:
        outs = [SDS((S, Dm), F32)] + ([SDS((S, Dm), _MXU)] if low else [])
        return pl.pallas_call(
            body, name=name, grid=(S // tb,), in_specs=[row, vec, row, row], out_specs=[row] * len(outs) + [vec],
            out_shape=outs + [SDS((1, Dm), F32)], compiler_params=_cp("arbitrary"))(x, g, dh, dres)
    return pl.pallas_call(
        body, name=name, grid=(S // tb,), in_specs=[row, vec, row], out_specs=vec,
        out_shape=SDS((1, Dm), F32), compiler_params=_cp("arbitrary"))(x, g, dh)


def _final_loss(x, g, tgt, name="final_loss", tb_cap=512):
    S, Dm = x.shape
    tb = _tile(S, tb_cap, 8)

    def body(x_ref, g_ref, t_ref, loss_ref, dx_ref, dxl_ref, dg_ref):
        xv = x_ref[...]
        gv = g_ref[...]
        r = lax.rsqrt(jnp.mean(xv * xv, axis=-1, keepdims=True) + EPS)
        xh = xv * r
        e = xh * gv - t_ref[...]

        @pl.when(pl.program_id(0) == 0)
        def _():
            loss_ref[...] = jnp.zeros_like(loss_ref)
            dg_ref[...] = jnp.zeros_like(dg_ref)

        loss_ref[...] += 0.5 * jnp.sum(jnp.mean(e * e, axis=-1, keepdims=True))
        dy = e * (1.0 / Dm)
        dg_ref[...] += jnp.sum(dy * xh, axis=0, keepdims=True)
        gdy = dy * gv
        dx = r * (gdy - xh * jnp.mean(xh * gdy, axis=-1, keepdims=True))
        dx_ref[...] = dx
        dxl_ref[...] = dx.astype(dxl_ref.dtype)

    row = pl.BlockSpec((tb, Dm), lambda i: (i, 0))
    vec = pl.BlockSpec((1, Dm), lambda i: (0, 0))
    return pl.pallas_call(
        body, name=name, grid=(S // tb,), in_specs=[row, vec, row],
        out_specs=[pl.BlockSpec((1, 128), lambda i: (0, 0)), row, row, vec],
        out_shape=[SDS((1, 128), F32), SDS((S, Dm), F32), SDS((S, Dm), _MXU), SDS((1, Dm), F32)],
        compiler_params=_cp("arbitrary"))(x, g, tgt)


SSD_HALO = 8
CF_HALO = 32
CONV_CB = 512


def _ssd_conv_fwd(proj, w, b, name="ssd_conv_fwd", tb_cap=512):
    S = proj.shape[0]
    tb = _tile(S, tb_cap, 8)
    nb = S // tb
    c0 = COL_XBC // CONV_CB

    def body(x_ref, w_ref, b_ref, o_ref, ext):
        @pl.when(pl.program_id(1) == 0)
        def _():
            ext[pl.ds(0, SSD_HALO), :] = jnp.zeros((SSD_HALO, CONV_CB), F32)

        ext[pl.ds(SSD_HALO, tb), :] = x_ref[...]
        acc = jnp.zeros((tb, CONV_CB), F32) + b_ref[...]
        for k in range(KS):
            acc = acc + ext[pl.ds(SSD_HALO - (KS - 1) + k, tb), :] * w_ref[k:k + 1, :]
        o_ref[...] = acc
        ext[pl.ds(0, SSD_HALO), :] = ext[pl.ds(tb, SSD_HALO), :]

    return pl.pallas_call(
        body, name=name, grid=(XBC // CONV_CB, nb),
        in_specs=[pl.BlockSpec((tb, CONV_CB), lambda j, i: (i, c0 + j)),
                  pl.BlockSpec((KS, CONV_CB), lambda j, i: (0, j)),
                  pl.BlockSpec((1, CONV_CB), lambda j, i: (0, j))],
        out_specs=pl.BlockSpec((tb, CONV_CB), lambda j, i: (i, j)), out_shape=SDS((S, XBC), F32),
        scratch_shapes=[pltpu.VMEM((SSD_HALO + tb, CONV_CB), F32)],
        compiler_params=_cp("parallel", "arbitrary"))(proj, w, b)


def _ssd_conv_bwd(dxbc, proj, w, name="ssd_conv_bwd", tb_cap=512):
    S = proj.shape[0]
    tb = _tile(S, tb_cap, 8)
    nb = S // tb
    c0 = COL_XBC // CONV_CB

    def body(dy_ref, x_ref, w_ref, dx_ref, dw_ref, db_ref, ext):
        @pl.when(pl.program_id(1) == 0)
        def _():
            ext[pl.ds(tb, SSD_HALO), :] = jnp.zeros((SSD_HALO, CONV_CB), F32)
            dw_ref[...] = jnp.zeros_like(dw_ref)
            db_ref[...] = jnp.zeros_like(db_ref)

        dy = dy_ref[...]
        ext[pl.ds(0, tb), :] = dy
        xv = x_ref[...]
        acc = jnp.zeros((tb, CONV_CB), F32)
        for k in range(KS):
            sh = ext[pl.ds(KS - 1 - k, tb), :]
            acc = acc + sh * w_ref[k:k + 1, :]
            dw_ref[k:k + 1, :] += jnp.sum(xv * sh, axis=0, keepdims=True)
        db_ref[...] += jnp.sum(dy, axis=0, keepdims=True)
        dx_ref[...] = acc.astype(dx_ref.dtype)
        ext[pl.ds(tb, SSD_HALO), :] = ext[pl.ds(0, SSD_HALO), :]

    return pl.pallas_call(
        body, name=name, grid=(XBC // CONV_CB, nb),
        in_specs=[pl.BlockSpec((tb, CONV_CB), lambda j, i: (nb - 1 - i, j)),
                  pl.BlockSpec((tb, CONV_CB), lambda j, i: (nb - 1 - i, c0 + j)),
                  pl.BlockSpec((KS, CONV_CB), lambda j, i: (0, j))],
        out_specs=[pl.BlockSpec((tb, CONV_CB), lambda j, i: (nb - 1 - i, j)),
                   pl.BlockSpec((KS, CONV_CB), lambda j, i: (0, j)),
                   pl.BlockSpec((1, CONV_CB), lambda j, i: (0, j))],
        out_shape=[SDS((S, XBC), _MXU), SDS((KS, XBC), F32), SDS((1, XBC), F32)],
        scratch_shapes=[pltpu.VMEM((tb + SSD_HALO, CONV_CB), F32)],
        compiler_params=_cp("parallel", "arbitrary"))(dxbc, proj, w)


def _head_consts():
    e = (lax.broadcasted_iota(jnp.int32, (128, NH * HP), 1) // HP == lax.broadcasted_iota(jnp.int32, (128, NH * HP), 0)).astype(F32)
    et = (lax.broadcasted_iota(jnp.int32, (NH * HP, 128), 0) // HP == lax.broadcasted_iota(jnp.int32, (NH * HP, 128), 1)).astype(F32)
    r = lax.broadcasted_iota(jnp.int32, (CH, CH), 0)
    c = lax.broadcasted_iota(jnp.int32, (CH, CH), 1)
    return e, et, (c <= r), (r <= c)


def _ssd_common(xbc_c, dtr, dtb, alog, e, tril, triu):
    xbc = _silu(xbc_c)
    xs = xbc[:, :NH * HP]
    dt = _softplus(dtr + dtb)
    A = -jnp.exp(alog)
    a = dt * A
    cs = _dot(tril.astype(F32), a, hi=True)
    csT = _dot(a, triu.astype(F32), "tn", hi=True)
    csL = cs[CH - 1:CH, :]
    wdec = jnp.exp(csL - cs) * dt
    dtE = _dot(dt, e, hi=True)
    ecsE = _dot(jnp.exp(cs), e, hi=True)
    wE = _dot(wdec, e, hi=True)
    eL = jnp.exp(csL)
    return xbc, xs, dt, A, cs, csT, csL, wdec, dtE, ecsE, wE, eL


def _ssd_fwd(proj, xbc_c, dtr, sc, norm_g, name="ssd_fwd"):
    S = proj.shape[0]
    nc = S // CH

    def body(z_ref, x_ref, dtr_ref, sc_ref, ng_ref, y_ref, yn_ref, hp_ref, hst):
        @pl.when(pl.program_id(0) == 0)
        def _():
            hst[...] = jnp.zeros_like(hst)

        e, et, tril, triu = _head_consts()
        xbc, xs, dt, A, cs, csT, csL, wdec, dtE, ecsE, wE, eL = _ssd_common(
            x_ref[...], dtr_ref[...], sc_ref[0:1, :], sc_ref[1:2, :], e, tril, triu)
        hp_ref[0] = hst[...]
        xd = xs * dtE
        xw = xs * wE
        dE = _dot(jnp.broadcast_to(sc_ref[2:3, :], (8, 128)), e, hi=True)[0:1, :]
        eLcol = jnp.sum(et * eL, axis=1, keepdims=True)
        for g in range(NG):
            Bg = xbc[:, NH * HP + g * NS: NH * HP + (g + 1) * NS]
            Cg = xbc[:, NH * HP + NG * NS + g * NS: NH * HP + NG * NS + (g + 1) * NS]
            gs = slice(g * GW, (g + 1) * GW)
            G = _dot(Cg, Bg, "nt")
            hg = hst[gs, :]
            yoff = ecsE[:, gs] * _dot(Cg, hg, "nt")
            hst[gs, :] = eLcol[gs, :] * hg + _dot(xw[:, gs], Bg, "tn")
            for hh in range(NH // NG):
                h = g * (NH // NG) + hh
                hs = slice(h * HP, (h + 1) * HP)
                m = jnp.where(tril, jnp.exp(jnp.where(tril, cs[:, h:h + 1] - csT[h:h + 1, :], 0.0)), 0.0)
                yd = _dot(G * m, xd[:, hs])
                y_ref[:, hs] = yd + yoff[:, hh * HP:(hh + 1) * HP] + dE[:, hs] * xs[:, hs]
        y = y_ref[...]
        yz = y * _silu(z_ref[...])
        for g in range(NG):
            gs = slice(g * GW, (g + 1) * GW)
            yg = yz[:, gs]
            r = lax.rsqrt(jnp.mean(yg * yg, axis=-1, keepdims=True) + EPS)
            yn_ref[:, gs] = (yg * r * ng_ref[:, gs]).astype(yn_ref.dtype)

    return pl.pallas_call(
        body, name=name, grid=(nc,),
        in_specs=[pl.BlockSpec((CH, D), lambda c: (c, COL_Z // D)),
                  pl.BlockSpec((CH, XBC), lambda c: (c, 0)),
                  pl.BlockSpec((CH, 128), lambda c: (c, 0)),
                  pl.BlockSpec((8, 128), lambda c: (0, 0)),
                  pl.BlockSpec((1, D), lambda c: (0, 0))],
        out_specs=[pl.BlockSpec((CH, D), lambda c: (c, 0)), pl.BlockSpec((CH, D), lambda c: (c, 0)),
                   pl.BlockSpec((1, NH * HP, NS), lambda c: (c, 0, 0))],
        out_shape=[SDS((S, D), F32), SDS((S, D), _MXU), SDS((nc, NH * HP, NS), F32)],
        scratch_shapes=[pltpu.VMEM((NH * HP, NS), F32)],
        compiler_params=_cp("arbitrary"))(proj, xbc_c, dtr, sc, norm_g)


def _ssd_bwd(dmix, y, proj, xbc_c, dtr, hprev, sc, norm_g, name="ssd_bwd"):
    S = proj.shape[0]
    nc = S // CH
    rev = lambda c: nc - 1 - c

    def body(dyn_ref, y_ref, z_ref, x_ref, dtr_ref, hp_ref, sc_ref, ng_ref,
             dz_ref, dx_ref, ddtr_ref, gsc_ref, gng_ref, dh, dxd):
        @pl.when(pl.program_id(0) == 0)
        def _():
            dh[...] = jnp.zeros_like(dh)
            gsc_ref[...] = jnp.zeros_like(gsc_ref)
            gng_ref[...] = jnp.zeros_like(gng_ref)

        e, et, tril, triu = _head_consts()
        xbc_c = x_ref[...]
        dtr = dtr_ref[...]
        dtb = sc_ref[0:1, :]
        xbc, xs, dt, A, cs, csT, csL, wdec, dtE, ecsE, wE, eL = _ssd_common(
            xbc_c, dtr, dtb, sc_ref[1:2, :], e, tril, triu)
        xd = xs * dtE
        xw = xs * wE
        dE = _dot(jnp.broadcast_to(sc_ref[2:3, :], (8, 128)), e, hi=True)[0:1, :]
        eLcol = jnp.sum(et * eL, axis=1, keepdims=True)

        yv = y_ref[...]
        zv = z_ref[...]
        sz = _silu(zv)
        yz = yv * sz
        dyn = dyn_ref[...]
        dyz_parts = []
        for g in range(NG):
            gs = slice(g * GW, (g + 1) * GW)
            yg = yz[:, gs]
            r = lax.rsqrt(jnp.mean(yg * yg, axis=-1, keepdims=True) + EPS)
            yh = yg * r
            dn = dyn[:, gs]
            gng_ref[:, gs] += jnp.sum(dn * yh, axis=0, keepdims=True)
            gdn = dn * ng_ref[:, gs]
            dyz_parts.append(r * (gdn - yh * jnp.mean(yh * gdn, axis=-1, keepdims=True)))
        dyz = jnp.concatenate(dyz_parts, axis=1)
        dy = dyz * sz
        dz_ref[...] = (dyz * yv * _dsilu(zv)).astype(dz_ref.dtype)

        gsc_ref[2:3, :] += jnp.sum(_dot(dy * xs, et, hi=True), axis=0, keepdims=True)
        dxs = dE * dy
        dzo = ecsE * dy
        dcs = jnp.zeros((CH, 128), F32)
        dcsL = jnp.zeros((1, 128), F32)
        ddt = jnp.zeros((CH, 128), F32)
        dB_parts, dC_parts, yoff_parts, dxw_parts = [], [], [], []
        for g in range(NG):
            Bg = xbc[:, NH * HP + g * NS: NH * HP + (g + 1) * NS]
            Cg = xbc[:, NH * HP + NG * NS + g * NS: NH * HP + NG * NS + (g + 1) * NS]
            gs = slice(g * GW, (g + 1) * GW)
            hg = hp_ref[0, gs, :]
            dhn = dh[gs, :]
            G = _dot(Cg, Bg, "nt")
            yoff_parts.append(ecsE[:, gs] * _dot(Cg, hg, "nt"))
            dC = _dot(dzo[:, gs], hg)
            dhp = _dot(dzo[:, gs], Cg, "tn") + eLcol[gs, :] * dhn
            t1 = jnp.sum(dhn * hg, axis=1, keepdims=True) * eLcol[gs, :]
            dcsL = dcsL + jnp.sum(et[gs, :] * t1, axis=0, keepdims=True)
            dxw_parts.append(_dot(Bg, dhn, "nt"))
            dB = _dot(xw[:, gs], dhn)
            dgsum = jnp.zeros((CH, CH), F32)
            for hh in range(NH // NG):
                h = g * (NH // NG) + hh
                hs = slice(h * HP, (h + 1) * HP)
                m = jnp.where(tril, jnp.exp(jnp.where(tril, cs[:, h:h + 1] - csT[h:h + 1, :], 0.0)), 0.0)
                sc = G * m
                dyh = dy[:, hs]
                dxd[:, hs] = _dot(sc, dyh, "tn")
                dsc = _dot(dyh, xd[:, hs], "nt")
                q = dsc * sc
                oh = (lax.broadcasted_iota(jnp.int32, (CH, 128), 1) == h).astype(F32)
                dcs = dcs + _dot(q, oh, hi=True) - _dot(q, oh, "tn", hi=True)
                dgsum = dgsum + dsc * m
            dC_parts.append(dC + _dot(dgsum, Bg))
            dB_parts.append(dB + _dot(dgsum, Cg, "tn"))
            dh[gs, :] = dhp
        yoff = jnp.concatenate(yoff_parts, axis=1)
        dxw = jnp.concatenate(dxw_parts, axis=1)
        dxdv = dxd[...]
        dcs = dcs + _dot(dy * yoff, et, hi=True)
        dxs = dxs + wE * dxw + dtE * dxdv
        dw = _dot(dxw * xs, et, hi=True)
        ddt = ddt + dw * jnp.exp(csL - cs) + _dot(dxdv * xs, et, hi=True)
        dcs = dcs - dw * wdec
        dcsL = dcsL + jnp.sum(dw * wdec, axis=0, keepdims=True)
        last = lax.broadcasted_iota(jnp.int32, (CH, 128), 0) == CH - 1
        dcs = dcs + jnp.where(last, dcsL, 0.0)
        da = _dot(triu.astype(F32), dcs, hi=True)
        ddt = ddt + da * A
        gsc_ref[1:2, :] += jnp.sum(da * dt, axis=0, keepdims=True) * A
        valid = lax.broadcasted_iota(jnp.int32, (CH, 128), 1) < NH
        ddtr = jnp.where(valid, ddt * _sigmoid(dtr + dtb), 0.0)
        gsc_ref[0:1, :] += jnp.sum(ddtr, axis=0, keepdims=True)
        ddtr_ref[...] = ddtr.astype(ddtr_ref.dtype)
        dxbc = jnp.concatenate([dxs] + dB_parts + dC_parts, axis=1)
        dx_ref[...] = dxbc * _dsilu(xbc_c)

    vec = pl.BlockSpec((8, 128), lambda c: (0, 0))
    vecd = pl.BlockSpec((1, D), lambda c: (0, 0))
    row = lambda w, j=0: pl.BlockSpec((CH, w), lambda c: (rev(c), j))
    return pl.pallas_call(
        body, name=name, grid=(nc,),
        in_specs=[row(D), row(D), row(D, COL_Z // D), row(XBC), row(128),
                  pl.BlockSpec((1, NH * HP, NS), lambda c: (rev(c), 0, 0)), vec, vecd],
        out_specs=[row(D), row(XBC), row(128), vec, vecd],
        out_shape=[SDS((S, D), _MXU), SDS((S, XBC), F32), SDS((S, 128), _MXU), SDS((8, 128), F32), SDS((1, D), F32)],
        scratch_shapes=[pltpu.VMEM((NH * HP, NS), F32), pltpu.VMEM((CH, NH * HP), F32)],
        compiler_params=_cp("arbitrary"))(dmix, y, proj, xbc_c, dtr, hprev, sc, norm_g)


def _cf_fwd(proj, w, b, lg, lb, name="cf_fwd", tb_cap=256):
    S = proj.shape[0]
    tb = _tile(S, tb_cap, 8)

    def body(a_ref, g_ref, w_ref, b_ref, lg_ref, lb_ref, u1_ref, u_ref, ext):
        @pl.when(pl.program_id(0) == 0)
        def _():
            ext[pl.ds(0, CF_HALO), :] = jnp.zeros((CF_HALO, D), F32)

        ext[pl.ds(CF_HALO, tb), :] = a_ref[...] * _sigmoid(g_ref[...])
        acc = jnp.zeros((tb, D), F32) + b_ref[...]
        for k in range(KC):
            acc = acc + ext[pl.ds(CF_HALO - (KC - 1) + k, tb), :] * w_ref[k:k + 1, :]
        u1_ref[...] = acc
        mu = jnp.mean(acc, axis=-1, keepdims=True)
        xc = acc - mu
        r = lax.rsqrt(jnp.mean(xc * xc, axis=-1, keepdims=True) + EPS)
        u_ref[...] = _silu(xc * r * lg_ref[...] + lb_ref[...]).astype(u_ref.dtype)
        ext[pl.ds(0, CF_HALO), :] = ext[pl.ds(tb, CF_HALO), :]

    vec = pl.BlockSpec((1, D), lambda i: (0, 0))
    return pl.pallas_call(
        body, name=name, grid=(S // tb,),
        in_specs=[pl.BlockSpec((tb, D), lambda i: (i, COL_A // D)), pl.BlockSpec((tb, D), lambda i: (i, COL_G // D)),
                  pl.BlockSpec((KC, D), lambda i: (0, 0)), vec, vec, vec],
        out_specs=[pl.BlockSpec((tb, D), lambda i: (i, 0)), pl.BlockSpec((tb, D), lambda i: (i, 0))],
        out_shape=[SDS((S, D), F32), SDS((S, D), _MXU)],
        scratch_shapes=[pltpu.VMEM((CF_HALO + tb, D), F32)],
        compiler_params=_cp("arbitrary"))(proj, proj, w, b, lg, lb)


def _cf_bwd(dmix, u1, proj, w, lg, lb, name="cf_bwd", tb_cap=256):
    S = proj.shape[0]
    tb = _tile(S, tb_cap, 8)
    nb = S // tb
    rev = lambda i: nb - 1 - i

    def body(du_ref, u1_ref, a_ref, g_ref, w_ref, lg_ref, lb_ref,
             da_ref, dg_ref, dw_ref, db_ref, dlg_ref, dlb_ref, ext):
        @pl.when(pl.program_id(0) == 0)
        def _():
            ext[pl.ds(tb, CF_HALO), :] = jnp.zeros((CF_HALO, D), F32)
            dw_ref[...] = jnp.zeros_like(dw_ref)
            db_ref[...] = jnp.zeros_like(db_ref)
            dlg_ref[...] = jnp.zeros_like(dlg_ref)
            dlb_ref[...] = jnp.zeros_like(dlb_ref)

        u1 = u1_ref[...]
        mu = jnp.mean(u1, axis=-1, keepdims=True)
        xc = u1 - mu
        r = lax.rsqrt(jnp.mean(xc * xc, axis=-1, keepdims=True) + EPS)
        xh = xc * r
        lgv = lg_ref[...]
        du2 = du_ref[...] * _dsilu(xh * lgv + lb_ref[...])
        dlg_ref[...] += jnp.sum(du2 * xh, axis=0, keepdims=True)
        dlb_ref[...] += jnp.sum(du2, axis=0, keepdims=True)
        gd = du2 * lgv
        du1 = r * (gd - jnp.mean(gd, axis=-1, keepdims=True) - xh * jnp.mean(gd * xh, axis=-1, keepdims=True))
        db_ref[...] += jnp.sum(du1, axis=0, keepdims=True)
        ext[pl.ds(0, tb), :] = du1
        av = a_ref[...]
        sg = _sigmoid(g_ref[...])
        u0 = av * sg
        acc = jnp.zeros((tb, D), F32)
        for k in range(KC):
            sh = ext[pl.ds(KC - 1 - k, tb), :]
            acc = acc + sh * w_ref[k:k + 1, :]
            dw_ref[k:k + 1, :] += jnp.sum(u0 * sh, axis=0, keepdims=True)
        da_ref[...] = (acc * sg).astype(da_ref.dtype)
        dg_ref[...] = (acc * av * sg * (1.0 - sg)).astype(dg_ref.dtype)
        ext[pl.ds(tb, CF_HALO), :] = ext[pl.ds(0, CF_HALO), :]

    vec = pl.BlockSpec((1, D), lambda i: (0, 0))
    wsp = pl.BlockSpec((KC, D), lambda i: (0, 0))
    row = lambda j=0: pl.BlockSpec((tb, D), lambda i: (rev(i), j))
    return pl.pallas_call(
        body, name=name, grid=(nb,),
        in_specs=[row(1), row(), row(COL_A // D), row(COL_G // D), wsp, vec, vec],
        out_specs=[row(), row(), wsp, vec, vec, vec],
        out_shape=[SDS((S, D), _MXU), SDS((S, D), _MXU), SDS((KC, D), F32),
                   SDS((1, D), F32), SDS((1, D), F32), SDS((1, D), F32)],
        scratch_shapes=[pltpu.VMEM((tb + CF_HALO, D), F32)],
        compiler_params=_cp("arbitrary"))(dmix, u1, proj, proj, w, lg, lb)


def _attn_fwd(q, kv, name="attn_fwd", tq_cap=512):
    S = q.shape[0]
    tq = _tile(S, tq_cap, 8)
    scale = XD ** -0.5

    def body(q_ref, kv_ref, o_ref):
        for h in range(XH):
            hs = slice(h * XD, (h + 1) * XD)
            s = _dot(q_ref[:, hs], kv_ref[:, hs], "nt") * scale
            s = s - jnp.max(s, axis=-1, keepdims=True)
            p = jnp.exp(s)
            p = p / jnp.sum(p, axis=-1, keepdims=True)
            o_ref[:, hs] = _dot(p, kv_ref[:, D + h * XD: D + (h + 1) * XD]).astype(o_ref.dtype)

    return pl.pallas_call(
        body, name=name, grid=(S // tq,),
        in_specs=[pl.BlockSpec((tq, D), lambda i: (i, 0)), pl.BlockSpec((MEM, 2 * D), lambda i: (0, 0))],
        out_specs=pl.BlockSpec((tq, D), lambda i: (i, 0)), out_shape=SDS((S, D), _MXU),
        compiler_params=_cp("parallel"))(q, kv)


def _attn_bwd(do, q, kv, name="attn_bwd", tq_cap=512):
    S = q.shape[0]
    tq = _tile(S, tq_cap, 8)
    scale = XD ** -0.5

    def body(do_ref, q_ref, kv_ref, dq_ref, dkv_ref):
        @pl.when(pl.program_id(0) == 0)
        def _():
            dkv_ref[...] = jnp.zeros_like(dkv_ref)

        for h in range(XH):
            hs = slice(h * XD, (h + 1) * XD)
            vs = slice(D + h * XD, D + (h + 1) * XD)
            qh = q_ref[:, hs]
            kh = kv_ref[:, hs]
            s = _dot(qh, kh, "nt") * scale
            s = s - jnp.max(s, axis=-1, keepdims=True)
            p = jnp.exp(s)
            p = p / jnp.sum(p, axis=-1, keepdims=True)
            doh = do_ref[:, hs]
            dp = _dot(doh, kv_ref[:, vs], "nt")
            ds = p * (dp - jnp.sum(dp * p, axis=-1, keepdims=True)) * scale
            dq_ref[:, hs] = _dot(ds, kh).astype(dq_ref.dtype)
            dkv_ref[:, hs] += _dot(ds, qh, "tn")
            dkv_ref[:, vs] += _dot(p, doh, "tn")

    return pl.pallas_call(
        body, name=name, grid=(S // tq,),
        in_specs=[pl.BlockSpec((tq, D), lambda i: (i, 0)), pl.BlockSpec((tq, D), lambda i: (i, 0)),
                  pl.BlockSpec((MEM, 2 * D), lambda i: (0, 0))],
        out_specs=[pl.BlockSpec((tq, D), lambda i: (i, 0)), pl.BlockSpec((MEM, 2 * D), lambda i: (0, 0))],
        out_shape=[SDS((S, D), _MXU), SDS((MEM, 2 * D), F32)],
        compiler_params=_cp("arbitrary"))(do, q, kv)


def _ffn_act(gu, name="ffn_act", tb_cap=512):
    S = gu.shape[0]
    tb = _tile(S, tb_cap, 8)

    def body(g_ref, u_ref, o_ref):
        o_ref[...] = (_silu(g_ref[...]) * u_ref[...]).astype(o_ref.dtype)

    return pl.pallas_call(
        body, name=name, grid=(S // tb, NFB),
        in_specs=[pl.BlockSpec((tb, FB), lambda i, j: (i, j)), pl.BlockSpec((tb, FB), lambda i, j: (i, NFB + j))],
        out_specs=pl.BlockSpec((tb, FB), lambda i, j: (i, j)), out_shape=SDS((S, DFF), _MXU),
        compiler_params=_cp("parallel", "parallel"))(gu, gu)


def _ffn_act_bwd(dact, gu, name="ffn_act_bwd", tb_cap=512):
    S = gu.shape[0]
    tb = _tile(S, tb_cap, 8)

    def body(d_ref, g_ref, u_ref, dg_ref, du_ref):
        gt = g_ref[...]
        d = d_ref[...]
        dg_ref[...] = (d * u_ref[...] * _dsilu(gt)).astype(dg_ref.dtype)
        du_ref[...] = (d * _silu(gt)).astype(du_ref.dtype)

    blk = pl.BlockSpec((tb, FB), lambda i, j: (i, j))
    return pl.pallas_call(
        body, name=name, grid=(S // tb, NFB),
        in_specs=[blk, blk, pl.BlockSpec((tb, FB), lambda i, j: (i, NFB + j))],
        out_specs=[blk, blk], out_shape=[SDS((S, DFF), _MXU), SDS((S, DFF), _MXU)],
        compiler_params=_cp("parallel", "parallel"))(dact, gu, gu)


def _local_step(x, mem, tgt, W, P):
    h = _rms_fwd(x, P["g_mix"], "rms_mix")
    proj = _mm_nn(h, W["main"], "in_proj", tn_cap=1152)
    dtr = _mm_nn(h, W["dt"], "in_proj_dt")
    xbc_c = _ssd_conv_fwd(proj, P["conv4_w"], P["conv4_b"])
    y, yn, hprev = _ssd_fwd(proj, xbc_c, dtr, P["sc"], P["ssd_norm_g"])
    u1, u = _cf_fwd(proj, P["cf_w"], P["cf_b"], P["ln_g"], P["ln_b"])
    mix = jnp.concatenate([yn, u], axis=1)
    x1 = _mm_nn(mix, W["out"], "out_proj", add=x)
    hq = _rms_fwd(x1, P["g_xattn"], "rms_xattn")
    q = _mm_nn(hq, W["q"], "q_proj")
    mn = _rms_fwd(mem, P["g_mem"], "rms_mem")
    kv = _mm_nn(mn, W["kv"], "kv_proj")
    o = _attn_fwd(q, kv)
    x2 = _mm_nn(o, W["o"], "o_proj", add=x1)
    hf = _rms_fwd(x2, P["g_ffn"], "rms_ffn")
    gu = _mm_nn(hf, W["gu"], "ffn_in")
    act = _ffn_act(gu)
    x3 = _mm_nn(act, W["down"], "ffn_out", add=x2)
    loss, dx3, dx3b, g_final = _final_loss(x3, P["g_final"], tgt)
    GW, GP = {}, {"g_final": g_final}
    dact = _mm_nt(dx3b, W["down"], "ffn_out_dx", tk_cap=1408)
    GW["down"] = _mm_tn(act, dx3b, "ffn_out_dw", tk_cap=1408, tn_cap=1024)
    dgt, dup = _ffn_act_bwd(dact, gu)
    dhf = _mm_nt(dgt, W["gu"], "ffn_gate_dx", b_col=0)
    dhf = _mm_nt(dup, W["gu"], "ffn_up_dx", b_col=1, add=dhf)
    GW["gate"] = _mm_tn(hf, dgt, "ffn_gate_dw")
    GW["up"] = _mm_tn(hf, dup, "ffn_up_dw")
    dx2, dx2b, GP["g_ffn"] = _rms_bwd(x2, P["g_ffn"], dhf, dx3, "rms_ffn_bwd")
    do = _mm_nt(dx2b, W["o"], "o_proj_dx")
    GW["o"] = _mm_tn(o, dx2b, "o_proj_dw")
    dq, dkv = _attn_bwd(do, q, kv)
    dhq = _mm_nt(dq, W["q"], "q_proj_dx")
    GW["q"] = _mm_tn(hq, dq, "q_proj_dw")
    dkvb = dkv.astype(_MXU)
    GW["kv"] = _mm_tn(mn, dkvb, "kv_proj_dw", tm_cap=256)
    dmn = _mm_nt(dkvb, W["kv"], "kv_proj_dx")
    GP["g_mem"] = _rms_bwd(mem, P["g_mem"], dmn, None, "rms_mem_bwd")
    dx1, dx1b, GP["g_xattn"] = _rms_bwd(x1, P["g_xattn"], dhq, dx2, "rms_xattn_bwd")
    dmix = _mm_nt(dx1b, W["out"], "out_proj_dx")
    GW["out"] = _mm_tn(mix, dx1b, "out_proj_dw", tn_cap=1024)
    da, dg, GP["cf_w"], GP["cf_b"], GP["ln_g"], GP["ln_b"] = _cf_bwd(dmix, u1, proj, P["cf_w"], P["ln_g"], P["ln_b"])
    dz, dxbc_c, ddtr, GP["sc"], GP["ssd_norm_g"] = _ssd_bwd(dmix, y, proj, xbc_c, dtr, hprev, P["sc"], P["ssd_norm_g"])
    dxbc, GP["conv4_w"], GP["conv4_b"] = _ssd_conv_bwd(dxbc_c, proj, P["conv4_w"])
    dproj = jnp.concatenate([dz, da, dg, dxbc], axis=1)
    dh = _mm_nt(ddtr, W["dt"], "in_proj_dt_dx")
    dh = _mm_nt(dproj, W["main"], "in_proj_dx", add=dh, tk_cap=512)
    GW["main"] = _mm_tn(h, dproj, "in_proj_dw", tn_cap=1152)
    GW["dt"] = _mm_tn(h, ddtr, "in_proj_dt_dw")
    grad_x, GP["g_mix"] = _rms_bwd(x, P["g_mix"], dh, dx1, "rms_mix_bwd", low=False)
    return loss, grad_x, GW, GP


Z_END, XBC_END, DT_END = NH * HP, NH * HP + XBC, NH * HP + XBC + NH
NFB = DFF // FB


def _pad_to(a, rows=None, cols=None):
    r = 0 if rows is None else rows - a.shape[0]
    c = 0 if cols is None else cols - a.shape[1]
    return jnp.pad(a, ((0, r), (0, c)))


IN_W = DT_END + 2 * D
W_IN_SEGS = [(0, Z_END, "main", COL_Z), (Z_END, XBC_END, "main", COL_XBC), (XBC_END, DT_END, "dt", 0),
             (DT_END, DT_END + D, "main", COL_A), (DT_END + D, IN_W, "main", COL_G)]
BIG = [("w_in", True), ("w_out", False), ("w_q", False), ("w_kv", True), ("w_o", False), ("w_gate", True),
       ("w_up", True), ("w_down", False)]


def _ref_cols(pieces, a, b):
    cw = IN_W // 4
    out = []
    for j in range(4):
        lo, hi = max(a, j * cw), min(b, (j + 1) * cw)
        if lo < hi:
            out.append(pieces[j][:, lo - j * cw:hi - j * cw])
    return out


def _cat_cols(pieces):
    return jnp.concatenate([pieces[j] for j in range(4)], axis=1)


def _pack_weights(pc):
    w_in = pc["w_in"]
    main = jnp.concatenate(_ref_cols(w_in, 0, Z_END) + _ref_cols(w_in, DT_END, IN_W) + _ref_cols(w_in, Z_END, XBC_END), axis=1)
    dt = _pad_to(jnp.concatenate(_ref_cols(w_in, XBC_END, DT_END), axis=1), cols=128)
    gu = jnp.concatenate([pc["w_gate"][j] for j in range(4)] + [pc["w_up"][j] for j in range(4)], axis=1)
    rows = lambda n: pc[n].reshape(-1, pc[n].shape[-1])
    return {"main": main, "dt": dt, "out": rows("w_out"), "q": rows("w_q"), "kv": _cat_cols(pc["w_kv"]),
            "o": rows("w_o"), "gu": gu, "down": rows("w_down")}


def _shard_grads(GW):
    cw = IN_W // 4
    pieces = []
    for j in range(4):
        parts = []
        for a, b, src, col in W_IN_SEGS:
            lo, hi = max(a, j * cw), min(b, (j + 1) * cw)
            if lo < hi:
                parts.append(GW[src][:, col + lo - a:col + hi - a])
        pieces.append(jnp.concatenate(parts, axis=1))
    g = {"w_in": jnp.stack(pieces)}
    for k, n in (("gate", "w_gate"), ("up", "w_up"), ("kv", "w_kv")):
        cw = GW[k].shape[1] // 4
        g[n] = jnp.stack([GW[k][:, j * cw:(j + 1) * cw] for j in range(4)])
    for k, n in (("out", "w_out"), ("q", "w_q"), ("o", "w_o"), ("down", "w_down")):
        g[n] = GW[k].reshape(4, GW[k].shape[0] // 4, GW[k].shape[1])
    return g


def _stack_sc(dt_bias, a_log, d):
    return _pad_to(jnp.concatenate([dt_bias, a_log, d], axis=0), rows=8, cols=128)


HBM_SPEC = pl.BlockSpec(memory_space=pl.ANY)
COMM_PARAMS = pltpu.CompilerParams(vmem_limit_bytes=VMEM_LIMIT)


def _chip_peers(x, y):
    return [(1 - x, y), (x, 1 - y), (1 - x, 1 - y)]


def _remote(src, dst, send_sem, recv_sem, dev):
    return pltpu.make_async_remote_copy(src_ref=src, dst_ref=dst, send_sem=send_sem, recv_sem=recv_sem,
                                        device_id=dev, device_id_type=MESHID)


def _dma_sems(*counts):
    return [pltpu.SemaphoreType.DMA((n,)) for n in counts]


def _allgather_list(arrs, name):
    n = len(arrs)
    halved = [a.shape[0] % 16 == 0 for a in arrs]
    oshape = [(4, 2, a.shape[0] // 2, a.shape[1]) if h else (4, 1) + a.shape for a, h in zip(arrs, halved)]

    def body(*refs):
        srcs, outs = refs[:n], refs[n:2 * n]
        ici_send, ici_recv, own_send, own_recv, fwd_send, fwd_recv = refs[2 * n:]
        x, y, c = lax.axis_index("x"), lax.axis_index("y"), lax.axis_index("c")
        me = 2 * x + y
        sib = (x, y, 1 - c)
        peers = _chip_peers(x, y)

        def half(i, h):
            r = arrs[i].shape[0] // 2
            if not halved[i]:
                return srcs[i]
            return srcs[i].at[pl.ds(h * r if isinstance(h, int) else pl.multiple_of(h * r, 8), r)]

        ici, own, fwd = [], [], []
        for i in range(n):
            mine_h = c if halved[i] else 0
            for k, (px, py) in enumerate(peers):
                s = 3 * i + k
                ici.append(_remote(half(i, c), outs[i].at[me, mine_h], ici_send.at[s], ici_recv.at[s], (px, py, c)))
            for h in range(2 if halved[i] else 1):
                s = 2 * i + h
                own.append(_remote(half(i, h), outs[i].at[me, h], own_send.at[s], own_recv.at[s], sib))
        for cp in ici + own:
            cp.start()
        for i in range(n):
            if not halved[i]:
                continue
            for k, (px, py) in enumerate(peers):
                s = 3 * i + k
                got = outs[i].at[2 * px + py, c]
                _remote(half(i, c), got, ici_send.at[s], ici_recv.at[s], (px, py, c)).wait_recv()
                f = _remote(got, got, fwd_send.at[s], fwd_recv.at[s], sib)
                f.start()
                fwd.append(f)
        for i in range(n):
            for k, (px, py) in enumerate(peers):
                s = 3 * i + k
                if halved[i]:
                    _remote(half(i, c), outs[i].at[2 * px + py, 1 - c], fwd_send.at[s], fwd_recv.at[s], sib).wait_recv()
                else:
                    _remote(srcs[i], outs[i].at[2 * px + py, 0], ici_send.at[s], ici_recv.at[s], (px, py, c)).wait_recv()
            for h in range(2 if halved[i] else 1):
                s = 2 * i + h
                _remote(half(i, h), outs[i].at[me, h], own_send.at[s], own_recv.at[s], sib).wait_recv()
        for cp in ici + own + fwd:
            cp.wait_send()

    outs = pl.pallas_call(
        body, name=name, in_specs=[HBM_SPEC] * n, out_specs=[HBM_SPEC] * n,
        out_shape=[SDS(s, a.dtype) for s, a in zip(oshape, arrs)],
        scratch_shapes=_dma_sems(3 * n, 3 * n, 2 * n, 2 * n, 3 * n, 3 * n), compiler_params=COMM_PARAMS)(*arrs)
    return [o.reshape((4,) + a.shape) for o, a in zip(outs, arrs)]


def _pair_split_list(gs, name="rs_pair_send"):
    n = len(gs)

    def body(*refs):
        srcs, outs = refs[:n], refs[n:2 * n]
        send_sems, recv_sems = refs[2 * n:]
        x, y, c = lax.axis_index("x"), lax.axis_index("y"), lax.axis_index("c")
        sib = (x, y, 1 - c)
        sends = [_remote(srcs[i].at[j, 1 - c], outs[i].at[j], send_sems.at[4 * i + j], recv_sems.at[4 * i + j], sib)
                 for i in range(n) for j in range(4)]
        for cp in sends:
            cp.start()
        for cp in sends:
            cp.wait_recv()
        for cp in sends:
            cp.wait_send()

    return pl.pallas_call(
        body, name=name, in_specs=[HBM_SPEC] * n, out_specs=[HBM_SPEC] * n,
        out_shape=[SDS((4,) + g.shape[2:], g.dtype) for g in gs],
        scratch_shapes=_dma_sems(4 * n, 4 * n), compiler_params=COMM_PARAMS)(*gs)


def _scatter_list(ps, name="rs_chip_send"):
    n = len(ps)

    def body(*refs):
        srcs, outs = refs[:n], refs[n:2 * n]
        send_sems, recv_sems = refs[2 * n:]
        x, y, c = lax.axis_index("x"), lax.axis_index("y"), lax.axis_index("c")
        me = 2 * x + y
        peers = _chip_peers(x, y)
        sends = [_remote(srcs[i].at[2 * px + py], outs[i].at[me], send_sems.at[3 * i + k], recv_sems.at[3 * i + k],
                         (px, py, c)) for i in range(n) for k, (px, py) in enumerate(peers)]
        for cp in sends:
            cp.start()
        for i in range(n):
            for k, (px, py) in enumerate(peers):
                _remote(srcs[i].at[me], outs[i].at[2 * px + py], send_sems.at[3 * i + k], recv_sems.at[3 * i + k],
                        (px, py, c)).wait_recv()
        for cp in sends:
            cp.wait_send()

    return pl.pallas_call(
        body, name=name, in_specs=[HBM_SPEC] * n, out_specs=[HBM_SPEC] * n,
        out_shape=[SDS(p.shape, p.dtype) for p in ps],
        scratch_shapes=_dma_sems(3 * n, 3 * n), compiler_params=COMM_PARAMS)(*ps)


JOIN_SPLIT = 4


def _pair_join_list(bufs, name="rs_pair_join"):
    n = len(bufs)

    def body(*refs):
        outs = refs[n:2 * n]
        send_sems, recv_sems = refs[2 * n:]
        x, y, c = lax.axis_index("x"), lax.axis_index("y"), lax.axis_index("c")
        sib = (x, y, 1 - c)
        sends, recvs = [], []
        for i in range(n):
            rc = bufs[i].shape[1] // JOIN_SPLIT
            for q in range(JOIN_SPLIT):
                k = JOIN_SPLIT * i + q
                rows = pl.ds(q * rc, rc)
                sends.append(_remote(outs[i].at[c, rows], outs[i].at[c, rows], send_sems.at[k], recv_sems.at[k], sib))
                recvs.append(_remote(outs[i].at[c, rows], outs[i].at[1 - c, rows], send_sems.at[k], recv_sems.at[k], sib))
        for cp in sends:
            cp.start()
        for cp in recvs:
            cp.wait_recv()
        for cp in sends:
            cp.wait_send()

    return pl.pallas_call(
        body, name=name, in_specs=[HBM_SPEC] * n, out_specs=[HBM_SPEC] * n,
        out_shape=[SDS(b.shape, b.dtype) for b in bufs], input_output_aliases={i: i for i in range(n)},
        scratch_shapes=_dma_sems(JOIN_SPLIT * n, JOIN_SPLIT * n), compiler_params=COMM_PARAMS)(*bufs)


def _pair_sum(g, theirs, core, name):
    _, _, r, c = g.shape

    def body(core_ref, g_ref, t_ref, o_ref):
        o_ref[...] = (g_ref[...] + t_ref[...]).astype(o_ref.dtype)

    spec = pltpu.PrefetchScalarGridSpec(
        num_scalar_prefetch=1, grid=(4,),
        in_specs=[pl.BlockSpec((None, None, r, c), lambda j, core_ref: (j, core_ref[0], 0, 0)),
                  pl.BlockSpec((None, r, c), lambda j, core_ref: (j, 0, 0))],
        out_specs=pl.BlockSpec((None, r, c), lambda j, core_ref: (j, 0, 0)))
    return pl.pallas_call(body, name=name, grid_spec=spec, out_shape=SDS((4, r, c), BF16),
                          compiler_params=_cp("parallel"))(core, g, theirs)


def _chip_sum(own, got, where, name):
    _, r, c = own.shape
    tr = r // 2

    def body(w_ref, a_ref, b1_ref, b2_ref, b3_ref, o_ref):
        o_ref[...] = ((a_ref[...].astype(F32) + b1_ref[...].astype(F32)) + b2_ref[...].astype(F32)) + b3_ref[...].astype(F32)

    piece = lambda k: pl.BlockSpec((None, tr, c), lambda i, w_ref: ((w_ref[0] + k) % 4, i, 0))
    spec = pltpu.PrefetchScalarGridSpec(
        num_scalar_prefetch=1, grid=(r // tr,), in_specs=[piece(0), piece(1), piece(2), piece(3)],
        out_specs=pl.BlockSpec((None, tr, c), lambda i, w_ref: (w_ref[1], i, 0)))
    return pl.pallas_call(body, name=name, grid_spec=spec, out_shape=SDS((2, r, c), F32),
                          compiler_params=_cp("parallel"))(where, own, got, got, got)


def _adam_math(w, g, m, v):
    bc1 = 1.0 - ADAM_B1 ** ADAM_STEP
    bc2 = 1.0 - ADAM_B2 ** ADAM_STEP
    mn = ADAM_B1 * m + (1.0 - ADAM_B1) * g
    vn = ADAM_B2 * v + (1.0 - ADAM_B2) * (g * g)
    return -ADAM_LR * ((mn / bc1) / (jnp.sqrt(vn / bc2) + ADAM_EPS) + ADAM_WD * w), mn, vn


PACK_COLS = XBC
PACK = {"g_mix": (0, 1, D), "g_xattn": (1, 1, D), "g_mem": (2, 1, D), "g_ffn": (3, 1, D), "g_final": (4, 1, D),
        "ssd_norm_g": (5, 1, D), "cf_b": (6, 1, D), "ln_g": (7, 1, D), "ln_b": (8, 1, D), "conv4_b": (9, 1, XBC),
        "conv4_w": (10, KS, XBC), "sc": (16, 8, 128), "cf_w": (24, KC, D), "loss": (55, 1, 128)}
PACK_ROWS = 56
SMALL_ADAM = ["g_mix", "g_xattn", "g_mem", "g_ffn", "g_final", "ssd_norm_g", "cf_b", "ln_g", "ln_b", "conv4_b", "sc"]


def _small_allreduce_adamw(grads, wts, mom, var, name="allreduce_small"):
    gk = list(PACK)
    ng, na = len(gk), len(SMALL_ADAM)

    def body(*refs):
        g_in = refs[:ng]
        w_in, m_in, v_in = (refs[ng + i * na: ng + (i + 1) * na] for i in range(3))
        o = refs[ng + 3 * na:]
        g_out = o[:ng]
        d_out, m_out, v_out = (o[ng + i * na: ng + (i + 1) * na] for i in range(3))
        pack, buf, acc, send_sems, recv_sems = o[ng + 3 * na:]
        x, y, c = lax.axis_index("x"), lax.axis_index("y"), lax.axis_index("c")
        me = 4 * x + 2 * y + c
        pack[...] = jnp.zeros_like(pack)
        for i, k in enumerate(gk):
            r0, nr, nc = PACK[k]
            pack[r0:r0 + nr, 0:nc] = g_in[i][...]
        peers = [(x, y, 1 - c)] + [(px, py, pc) for px, py in _chip_peers(x, y) for pc in (c, 1 - c)]
        sends = [_remote(pack, buf.at[me], send_sems.at[k], recv_sems.at[k], dev) for k, dev in enumerate(peers)]
        for cp in sends:
            cp.start()
        buf[me] = pack[...]
        for k, (px, py, pc) in enumerate(peers):
            _remote(pack, buf.at[4 * px + 2 * py + pc], send_sems.at[k], recv_sems.at[k], (px, py, pc)).wait_recv()
        for cp in sends:
            cp.wait_send()
        tot = buf[0]
        for i in range(1, 8):
            tot = tot + buf[i]
        acc[...] = tot
        for i, k in enumerate(gk):
            r0, nr, nc = PACK[k]
            g_out[i][...] = acc[r0:r0 + nr, 0:nc]
        for i, k in enumerate(SMALL_ADAM):
            r0, nr, nc = PACK[k]
            d_out[i][...], m_out[i][...], v_out[i][...] = _adam_math(
                w_in[i][...], acc[r0:r0 + nr, 0:nc], m_in[i][...], v_in[i][...])

    args = [grads[k] for k in gk] + [d[k] for d in (wts, mom, var) for k in SMALL_ADAM]
    shp = lambda k: SDS((PACK[k][1], PACK[k][2]), F32)
    vm = pl.BlockSpec(memory_space=pltpu.VMEM)
    outs = pl.pallas_call(
        body, name=name, in_specs=[vm] * len(args), out_specs=[vm] * (ng + 3 * na),
        out_shape=[shp(k) for k in gk] + [shp(k) for _ in range(3) for k in SMALL_ADAM],
        scratch_shapes=[pltpu.VMEM((PACK_ROWS, PACK_COLS), F32), pltpu.VMEM((8, PACK_ROWS, PACK_COLS), F32),
                        pltpu.VMEM((PACK_ROWS, PACK_COLS), F32)] + _dma_sems(7, 7),
        compiler_params=COMM_PARAMS)(*args)
    red = dict(zip(gk, outs[:ng]))
    parts = [dict(zip(SMALL_ADAM, outs[ng + i * na: ng + (i + 1) * na])) for i in range(3)]
    return red, parts[0], parts[1], parts[2]


def _adamw_cols(w, gfull, m, v, chip, name):
    _, R, C = w.shape

    def body(w_idx, w_ref, g_ref, m_ref, v_ref, go_ref, d_ref, mo_ref, vo_ref):
        go_ref[...] = g_ref[...]
        d_ref[...], mo_ref[...], vo_ref[...] = _adam_math(w_ref[...], g_ref[...], m_ref[...], v_ref[...])

    blk = pl.BlockSpec((None, R, C), lambda i, w_idx: (0, 0, 0))
    spec = pltpu.PrefetchScalarGridSpec(
        num_scalar_prefetch=1, grid=(1,),
        in_specs=[blk, pl.BlockSpec((R, C), lambda i, w_idx: (0, w_idx[0])), blk, blk], out_specs=[blk] * 4)
    return pl.pallas_call(body, name=name, grid_spec=spec, out_shape=[SDS((1, R, C), F32)] * 4,
                          compiler_params=_cp("arbitrary"))(chip, w, gfull, m, v)


def _adamw(w, g, m, v, name):
    _, R, C = w.shape
    half = R // 2
    tr = _tile(half, max(8, (2 ** 17 // C) // 8 * 8), 8)
    nh = half // tr

    def body(w_ref, g_ref, m_ref, v_ref, go_ref, d_ref, mo_ref, vo_ref):
        go_ref[...] = g_ref[...]
        d_ref[...], mo_ref[...], vo_ref[...] = _adam_math(w_ref[...], g_ref[...], m_ref[...], v_ref[...])

    blk = pl.BlockSpec((None, tr, C), lambda i: (0, i, 0))
    gblk = pl.BlockSpec((None, tr, C), lambda i: (i // nh, i % nh, 0))
    return pl.pallas_call(body, name=name, grid=(R // tr,), in_specs=[blk, gblk, blk, blk], out_specs=[blk] * 4,
                          out_shape=[SDS((1, R, C), F32)] * 4, compiler_params=_cp("parallel"))(w, g, m, v)


WEIGHT_NAMES = ["norm_mix_g", "w_in", "ssd_conv_w", "ssd_conv_b", "ssd_dt_bias", "ssd_A_log", "ssd_D", "ssd_norm_g",
                "cf_conv_w", "cf_conv_b", "cf_ln_g", "cf_ln_b", "w_out", "norm_xattn_g", "norm_mem_g", "w_q", "w_kv",
                "w_o", "norm_ffn_g", "w_gate", "w_up", "w_down", "norm_final_g"]
VEC_REF = [("norm_mix_g", "g_mix"), ("norm_xattn_g", "g_xattn"), ("norm_mem_g", "g_mem"), ("norm_ffn_g", "g_ffn"),
           ("norm_final_g", "g_final"), ("ssd_norm_g", "ssd_norm_g"), ("cf_conv_b", "cf_b"), ("cf_ln_g", "ln_g"),
           ("cf_ln_b", "ln_b"), ("ssd_conv_b", "conv4_b")]
SC_REF = ["ssd_dt_bias", "ssd_A_log", "ssd_D"]


def _small_side(get):
    d = {k: get(ref_name).reshape(1, -1) for ref_name, k in VEC_REF}
    d["sc"] = _stack_sc(*[get(n) for n in SC_REF])
    return d


def kernel(x, mem, norm_mix_g, w_in, ssd_conv_w, ssd_conv_b, ssd_dt_bias, ssd_A_log, ssd_D, ssd_norm_g, cf_conv_w, cf_conv_b, cf_ln_g, cf_ln_b, w_out, norm_xattn_g, norm_mem_g, w_q, w_kv, w_o, norm_ffn_g, w_gate, w_up, w_down, norm_final_g, loss_target, m_norm_mix_g, m_w_in, m_ssd_conv_w, m_ssd_conv_b, m_ssd_dt_bias, m_ssd_A_log, m_ssd_D, m_ssd_norm_g, m_cf_conv_w, m_cf_conv_b, m_cf_ln_g, m_cf_ln_b, m_w_out, m_norm_xattn_g, m_norm_mem_g, m_w_q, m_w_kv, m_w_o, m_norm_ffn_g, m_w_gate, m_w_up, m_w_down, m_norm_final_g, v_norm_mix_g, v_w_in, v_ssd_conv_w, v_ssd_conv_b, v_ssd_dt_bias, v_ssd_A_log, v_ssd_D, v_ssd_norm_g, v_cf_conv_w, v_cf_conv_b, v_cf_ln_g, v_cf_ln_b, v_w_out, v_norm_xattn_g, v_norm_mem_g, v_w_q, v_w_kv, v_w_o, v_norm_ffn_g, v_w_gate, v_w_up, v_w_down, v_norm_final_g):
    env = dict(locals())
    wts = {n: env[n] for n in WEIGHT_NAMES}
    mom = {n: env["m_" + n] for n in WEIGHT_NAMES}
    var = {n: env["v_" + n] for n in WEIGHT_NAMES}
    chip = (2 * lax.axis_index("x") + lax.axis_index("y")).astype(jnp.int32).reshape(1)
    core = lax.axis_index("c").astype(jnp.int32).reshape(1)
    where = jnp.concatenate([chip, core])
    big = [n for n, _ in BIG]

    gathered = _allgather_list([wts[n][0].astype(BF16) for n in big] + [ssd_conv_w[0], cf_conv_w[0]], "allgather_weights")
    W = _pack_weights(dict(zip(big, gathered)))
    P = _small_side(lambda n: wts[n])
    P["conv4_w"], P["cf_w"] = _cat_cols(gathered[len(big)]), _cat_cols(gathered[len(big) + 1])

    loss, grad_x, GW, GP = _local_step(x[0], mem[0], loss_target[0], W, P)

    gs = _shard_grads(GW)
    halves = [gs[n].reshape(4, 2, gs[n].shape[1] // 2, gs[n].shape[2]) for n in big]
    theirs = _pair_split_list(halves)
    pair = [_pair_sum(h, t, core, "rs_pair_sum_" + n) for h, t, n in zip(halves, theirs, big)]
    got = _scatter_list(pair)
    joined = _pair_join_list([_chip_sum(p, q, where, "rs_chip_sum_" + n) for p, q, n in zip(pair, got, big)])
    gshard = dict(zip(big, joined))

    small = dict(GP)
    small["loss"] = loss
    red, sd, sm, sv = _small_allreduce_adamw(small, {k: P[k] for k in SMALL_ADAM}, _small_side(lambda n: mom[n]),
                                             _small_side(lambda n: var[n]))
    grads, delta, new_m, new_v = {}, {}, {}, {}
    for ref_name, k in VEC_REF:
        shp = wts[ref_name].shape
        for dst, src in ((grads, red), (delta, sd), (new_m, sm), (new_v, sv)):
            dst[ref_name] = src[k].reshape(shp)
    for row, ref_name in enumerate(SC_REF):
        for dst, src in ((grads, red), (delta, sd), (new_m, sm), (new_v, sv)):
            dst[ref_name] = src["sc"][row:row + 1, :NH]

    for n, k in (("ssd_conv_w", "conv4_w"), ("cf_conv_w", "cf_w")):
        grads[n], delta[n], new_m[n], new_v[n] = _adamw_cols(wts[n], red[k], mom[n], var[n], chip, "adamw_" + n)
    for n in big:
        grads[n], delta[n], new_m[n], new_v[n] = _adamw(wts[n], gshard[n], mom[n], var[n], "adamw_" + n)

    return (red["loss"][0, 0], grad_x[None], *[grads[n] for n in WEIGHT_NAMES], *[delta[n] for n in WEIGHT_NAMES],
            *[new_m[n] for n in WEIGHT_NAMES], *[new_v[n] for n in WEIGHT_NAMES])
```

```python
import functools
import math

import jax
import jax.numpy as jnp
from jax import lax
from jax.experimental import pallas as pl
from jax.experimental.pallas import tpu as pltpu

F32 = jnp.float32
BF16 = jnp.bfloat16
_MXU = BF16

D = 1024
MEM = 256
NH, HP, NG, NS = 16, 64, 2, 128
GW = NH * HP // NG
CH = 128
XBC = NH * HP + 2 * NG * NS
KS, KC = 4, 31
XH, XD = 4, 256
DFF = 2816
FB = 256
EPS = 1e-6
COL_Z, COL_A, COL_G, COL_XBC, MAINW = 0, 1024, 2048, 3072, 4608
VMEM_LIMIT = 56 * 2 ** 20

ADAM_LR, ADAM_B1, ADAM_B2, ADAM_EPS, ADAM_WD, ADAM_STEP = 0.001, 0.9, 0.999, 1e-08, 0.01, 10

SDS = jax.ShapeDtypeStruct
MESHID = pl.DeviceIdType.MESH


def _cp(*sem):
    return pltpu.CompilerParams(dimension_semantics=sem, vmem_limit_bytes=VMEM_LIMIT)


def _tile(n, cap, unit=128):
    if n <= cap:
        return n
    best = None
    for t in range(unit, cap + 1, unit):
        if n % t == 0:
            best = t
    assert best is not None, (n, cap)
    return best


def _sigmoid(x):
    return 1.0 / (1.0 + jnp.exp(-x))


def _silu(x):
    return x * _sigmoid(x)


def _dsilu(x):
    s = _sigmoid(x)
    return s * (1.0 + x * (1.0 - s))


def _softplus(x):
    return jnp.maximum(x, 0.0) + jnp.log(1.0 + jnp.exp(-jnp.abs(x)))


def _split_bf16(x, passes):
    parts, r = [], x.astype(F32)
    for _ in range(passes):
        p = r.astype(BF16)
        parts.append(p)
        r = r - p.astype(F32)
    return parts


def _dot(a, b, dims=None, exact=None, passes=2):
    dn = {None: (((1,), (0,)), ((), ())), "nt": (((1,), (1,)), ((), ())), "tn": (((0,), (0,)), ((), ()))}[dims]
    if exact is None:
        return lax.dot_general(a.astype(_MXU), b.astype(_MXU), dn, preferred_element_type=F32)
    if exact == "a":
        terms = [(a.astype(BF16), p) for p in _split_bf16(b, passes)]
    else:
        terms = [(p, b.astype(BF16)) for p in _split_bf16(a, passes)]
    out = None
    for lhs, rhs in terms:
        d = lax.dot_general(lhs, rhs, dn, preferred_element_type=F32)
        out = d if out is None else out + d
    return out


def _mm_nn(a, b, name, add=None, out_dtype=F32, tm_cap=1024, tn_cap=1408):
    M, K = a.shape
    _, N = b.shape
    tm, tn = _tile(M, tm_cap, 8), _tile(N, tn_cap)

    def body(a_ref, b_ref, *rest):
        o_ref = rest[-1]
        acc = _dot(a_ref[...], b_ref[...])
        if add is not None:
            acc = acc + rest[0][...]
        o_ref[...] = acc.astype(o_ref.dtype)

    in_specs = [pl.BlockSpec((tm, K), lambda j, i: (i, 0)), pl.BlockSpec((K, tn), lambda j, i: (0, j))]
    args = [a, b]
    if add is not None:
        in_specs.append(pl.BlockSpec((tm, tn), lambda j, i: (i, j)))
        args.append(add)
    return pl.pallas_call(
        body, name=name, grid=(N // tn, M // tm), in_specs=in_specs,
        out_specs=pl.BlockSpec((tm, tn), lambda j, i: (i, j)), out_shape=SDS((M, N), out_dtype),
        compiler_params=_cp("parallel", "parallel"))(*args)


def _mm_nt(a, b, name, add=None, out_dtype=F32, tm_cap=512, tk_cap=1024, b_col=0):
    M, N = a.shape
    K = b.shape[0]
    tm, tk = _tile(M, tm_cap, 8), _tile(K, tk_cap)

    def body(a_ref, b_ref, *rest):
        o_ref = rest[-1]
        acc = _dot(a_ref[...], b_ref[...], "nt")
        if add is not None:
            acc = acc + rest[0][...]
        o_ref[...] = acc.astype(o_ref.dtype)

    in_specs = [pl.BlockSpec((tm, N), lambda j, i: (i, 0)), pl.BlockSpec((tk, N), lambda j, i: (j, b_col))]
    args = [a, b]
    if add is not None:
        in_specs.append(pl.BlockSpec((tm, tk), lambda j, i: (i, j)))
        args.append(add)
    return pl.pallas_call(
        body, name=name, grid=(K // tk, M // tm), in_specs=in_specs,
        out_specs=pl.BlockSpec((tm, tk), lambda j, i: (i, j)), out_shape=SDS((M, K), out_dtype),
        compiler_params=_cp("parallel", "parallel"))(*args)


def _mm_tn(a, b, name, tm_cap=1024, tk_cap=512, tn_cap=1408):
    M, K = a.shape
    _, N = b.shape
    tm, tk, tn = _tile(M, tm_cap, 8), _tile(K, tk_cap), _tile(N, tn_cap)

    def body(a_ref, b_ref, o_ref):
        @pl.when(pl.program_id(2) == 0)
        def _():
            o_ref[...] = jnp.zeros_like(o_ref)

        o_ref[...] += _dot(a_ref[...], b_ref[...], "tn")

    return pl.pallas_call(
        body, name=name, grid=(K // tk, N // tn, M // tm),
        in_specs=[pl.BlockSpec((tm, tk), lambda k, n, m: (m, k)), pl.BlockSpec((tm, tn), lambda k, n, m: (m, n))],
        out_specs=pl.BlockSpec((tk, tn), lambda k, n, m: (k, n)), out_shape=SDS((K, N), F32),
        compiler_params=_cp("parallel", "parallel", "arbitrary"))(a, b)


def _rms_fwd(x, g, name, tb_cap=512):
    S, Dm = x.shape
    tb = _tile(S, tb_cap, 8)

    def body(x_ref, g_ref, o_ref):
        xv = x_ref[...]
        r = lax.rsqrt(jnp.mean(xv * xv, axis=-1, keepdims=True) + EPS)
        o_ref[...] = (xv * r * g_ref[...]).astype(o_ref.dtype)

    return pl.pallas_call(
        body, name=name, grid=(S // tb,),
        in_specs=[pl.BlockSpec((tb, Dm), lambda i: (i, 0)), pl.BlockSpec((1, Dm), lambda i: (0, 0))],
        out_specs=pl.BlockSpec((tb, Dm), lambda i: (i, 0)), out_shape=SDS((S, Dm), _MXU),
        compiler_params=_cp("parallel"))(x, g)


def _rms_bwd(x, g, dh, dres, name, tb_cap=512, low=True):
    S, Dm = x.shape
    tb = _tile(S, tb_cap, 8)
    need_dx = dres is not None

    def body(x_ref, g_ref, dh_ref, *rest):
        dg_ref = rest[-1]
        xv = x_ref[...]
        r = lax.rsqrt(jnp.mean(xv * xv, axis=-1, keepdims=True) + EPS)
        xh = xv * r
        dy = dh_ref[...].astype(F32)

        @pl.when(pl.program_id(0) == 0)
        def _():
            dg_ref[...] = jnp.zeros_like(dg_ref)

        dg_ref[...] += jnp.sum(dy * xh, axis=0, keepdims=True)
        if need_dx:
            gdy = dy * g_ref[...]
            dx = r * (gdy - xh * jnp.mean(xh * gdy, axis=-1, keepdims=True))
            tot = rest[0][...] + dx
            rest[1][...] = tot
            if low:
                rest[2][...] = tot.astype(rest[2].dtype)

    row = pl.BlockSpec((tb, Dm), lambda i: (i, 0))
    vec = pl.BlockSpec((1, Dm), lambda i: (0, 0))
    if need_dx:
        outs = [SDS((S, Dm), F32)] + ([SDS((S, Dm), _MXU)] if low else [])
        return pl.pallas_call(
            body, name=name, grid=(S // tb,), in_specs=[row, vec, row, row], out_specs=[row] * len(outs) + [vec],
            out_shape=outs + [SDS((1, Dm), F32)], compiler_params=_cp("arbitrary"))(x, g, dh, dres)
    return pl.pallas_call(
        body, name=name, grid=(S // tb,), in_specs=[row, vec, row], out_specs=vec,
        out_shape=SDS((1, Dm), F32), compiler_params=_cp("arbitrary"))(x, g, dh)


def _final_loss(x, g, tgt, name="final_loss", tb_cap=512):
    S, Dm = x.shape
    tb = _tile(S, tb_cap, 8)

    def body(x_ref, g_ref, t_ref, loss_ref, dx_ref, dxl_ref, dg_ref):
        xv = x_ref[...]
        gv = g_ref[...]
        r = lax.rsqrt(jnp.mean(xv * xv, axis=-1, keepdims=True) + EPS)
        xh = xv * r
        e = xh * gv - t_ref[...]

        @pl.when(pl.program_id(0) == 0)
        def _():
            loss_ref[...] = jnp.zeros_like(loss_ref)
            dg_ref[...] = jnp.zeros_like(dg_ref)

        loss_ref[...] += 0.5 * jnp.sum(jnp.mean(e * e, axis=-1, keepdims=True))
        dy = e * (1.0 / Dm)
        dg_ref[...] += jnp.sum(dy * xh, axis=0, keepdims=True)
        gdy = dy * gv
        dx = r * (gdy - xh * jnp.mean(xh * gdy, axis=-1, keepdims=True))
        dx_ref[...] = dx
        dxl_ref[...] = dx.astype(dxl_ref.dtype)

    row = pl.BlockSpec((tb, Dm), lambda i: (i, 0))
    vec = pl.BlockSpec((1, Dm), lambda i: (0, 0))
    return pl.pallas_call(
        body, name=name, grid=(S // tb,), in_specs=[row, vec, row],
        out_specs=[pl.BlockSpec((1, 128), lambda i: (0, 0)), row, row, vec],
        out_shape=[SDS((1, 128), F32), SDS((S, Dm), F32), SDS((S, Dm), _MXU), SDS((1, Dm), F32)],
        compiler_params=_cp("arbitrary"))(x, g, tgt)


SSD_HALO = 8
CF_HALO = 32
CONV_CB = 512


def _ssd_conv_fwd(proj, w, b, name="ssd_conv_fwd", tb_cap=512):
    S = proj.shape[0]
    tb = _tile(S, tb_cap, 8)
    nb = S // tb
    c0 = COL_XBC // CONV_CB

    def body(x_ref, w_ref, b_ref, o_ref, ext):
        @pl.when(pl.program_id(1) == 0)
        def _():
            ext[pl.ds(0, SSD_HALO), :] = jnp.zeros((SSD_HALO, CONV_CB), F32)

        ext[pl.ds(SSD_HALO, tb), :] = x_ref[...]
        acc = jnp.zeros((tb, CONV_CB), F32) + b_ref[...]
        for k in range(KS):
            acc = acc + ext[pl.ds(SSD_HALO - (KS - 1) + k, tb), :] * w_ref[k:k + 1, :]
        o_ref[...] = acc
        ext[pl.ds(0, SSD_HALO), :] = ext[pl.ds(tb, SSD_HALO), :]

    return pl.pallas_call(
        body, name=name, grid=(XBC // CONV_CB, nb),
        in_specs=[pl.BlockSpec((tb, CONV_CB), lambda j, i: (i, c0 + j)),
                  pl.BlockSpec((KS, CONV_CB), lambda j, i: (0, j)),
                  pl.BlockSpec((1, CONV_CB), lambda j, i: (0, j))],
        out_specs=pl.BlockSpec((tb, CONV_CB), lambda j, i: (i, j)), out_shape=SDS((S, XBC), F32),
        scratch_shapes=[pltpu.VMEM((SSD_HALO + tb, CONV_CB), F32)],
        compiler_params=_cp("parallel", "arbitrary"))(proj, w, b)


def _ssd_conv_bwd(dxbc, proj, w, name="ssd_conv_bwd", tb_cap=512):
    S = proj.shape[0]
    tb = _tile(S, tb_cap, 8)
    nb = S // tb
    c0 = COL_XBC // CONV_CB

    def body(dy_ref, x_ref, w_ref, dx_ref, dw_ref, db_ref, ext):
        @pl.when(pl.program_id(1) == 0)
        def _():
            ext[pl.ds(tb, SSD_HALO), :] = jnp.zeros((SSD_HALO, CONV_CB), F32)
            dw_ref[...] = jnp.zeros_like(dw_ref)
            db_ref[...] = jnp.zeros_like(db_ref)

        dy = dy_ref[...]
        ext[pl.ds(0, tb), :] = dy
        xv = x_ref[...]
        acc = jnp.zeros((tb, CONV_CB), F32)
        for k in range(KS):
            sh = ext[pl.ds(KS - 1 - k, tb), :]
            acc = acc + sh * w_ref[k:k + 1, :]
            dw_ref[k:k + 1, :] += jnp.sum(xv * sh, axis=0, keepdims=True)
        db_ref[...] += jnp.sum(dy, axis=0, keepdims=True)
        dx_ref[...] = acc.astype(dx_ref.dtype)
        ext[pl.ds(tb, SSD_HALO), :] = ext[pl.ds(0, SSD_HALO), :]

    return pl.pallas_call(
        body, name=name, grid=(XBC // CONV_CB, nb),
        in_specs=[pl.BlockSpec((tb, CONV_CB), lambda j, i: (nb - 1 - i, j)),
                  pl.BlockSpec((tb, CONV_CB), lambda j, i: (nb - 1 - i, c0 + j)),
                  pl.BlockSpec((KS, CONV_CB), lambda j, i: (0, j))],
        out_specs=[pl.BlockSpec((tb, CONV_CB), lambda j, i: (nb - 1 - i, j)),
                   pl.BlockSpec((KS, CONV_CB), lambda j, i: (0, j)),
                   pl.BlockSpec((1, CONV_CB), lambda j, i: (0, j))],
        out_shape=[SDS((S, XBC), _MXU), SDS((KS, XBC), F32), SDS((1, XBC), F32)],
        scratch_shapes=[pltpu.VMEM((tb + SSD_HALO, CONV_CB), F32)],
        compiler_params=_cp("parallel", "arbitrary"))(dxbc, proj, w)


def _head_consts():
    e = (lax.broadcasted_iota(jnp.int32, (128, NH * HP), 1) // HP == lax.broadcasted_iota(jnp.int32, (128, NH * HP), 0)).astype(F32)
    et = (lax.broadcasted_iota(jnp.int32, (NH * HP, 128), 0) // HP == lax.broadcasted_iota(jnp.int32, (NH * HP, 128), 1)).astype(F32)
    r = lax.broadcasted_iota(jnp.int32, (CH, CH), 0)
    c = lax.broadcasted_iota(jnp.int32, (CH, CH), 1)
    return e, et, (c <= r), (r <= c)


def _ssd_common(xbc_c, dtr, dtb, alog, e, tril, triu):
    xbc = _silu(xbc_c)
    xs = xbc[:, :NH * HP]
    dt = _softplus(dtr + dtb)
    A = -jnp.exp(alog)
    a = dt * A
    cs = _dot(tril, a, exact="a", passes=3)
    csT = _dot(a, triu, "tn", exact="b", passes=3)
    csL = cs[CH - 1:CH, :]
    wdec = jnp.exp(csL - cs) * dt
    dtE = _dot(dt, e, exact="b")
    ecsE = _dot(jnp.exp(cs), e, exact="b")
    wE = _dot(wdec, e, exact="b")
    eL = jnp.exp(csL)
    return xbc, xs, dt, A, cs, csT, csL, wdec, dtE, ecsE, wE, eL


def _ssd_fwd(proj, xbc_c, dtr, sc, norm_g, name="ssd_fwd"):
    S = proj.shape[0]
    nc = S // CH

    def body(z_ref, x_ref, dtr_ref, sc_ref, ng_ref, y_ref, yn_ref, hp_ref, hst):
        @pl.when(pl.program_id(0) == 0)
        def _():
            hst[...] = jnp.zeros_like(hst)

        e, et, tril, triu = _head_consts()
        xbc, xs, dt, A, cs, csT, csL, wdec, dtE, ecsE, wE, eL = _ssd_common(
            x_ref[...], dtr_ref[...], sc_ref[0:1, :], sc_ref[1:2, :], e, tril, triu)
        hp_ref[0] = hst[...]
        xd = xs * dtE
        xw = xs * wE
        dE = _dot(jnp.broadcast_to(sc_ref[2:3, :], (8, 128)), e, exact="b", passes=3)[0:1, :]
        eLcol = jnp.sum(et * eL, axis=1, keepdims=True)
        for g in range(NG):
            Bg = xbc[:, NH * HP + g * NS: NH * HP + (g + 1) * NS]
            Cg = xbc[:, NH * HP + NG * NS + g * NS: NH * HP + NG * NS + (g + 1) * NS]
            gs = slice(g * GW, (g + 1) * GW)
            G = _dot(Cg, Bg, "nt")
            hg = hst[gs, :]
            yoff = ecsE[:, gs] * _dot(Cg, hg, "nt")
            hst[gs, :] = eLcol[gs, :] * hg + _dot(xw[:, gs], Bg, "tn")
            for hh in range(NH // NG):
                h = g * (NH // NG) + hh
                hs = slice(h * HP, (h + 1) * HP)
                m = jnp.where(tril, jnp.exp(jnp.where(tril, cs[:, h:h + 1] - csT[h:h + 1, :], 0.0)), 0.0)
                yd = _dot(G * m, xd[:, hs])
                y_ref[:, hs] = yd + yoff[:, hh * HP:(hh + 1) * HP] + dE[:, hs] * xs[:, hs]
        y = y_ref[...]
        yz = y * _silu(z_ref[...])
        for g in range(NG):
            gs = slice(g * GW, (g + 1) * GW)
            yg = yz[:, gs]
            r = lax.rsqrt(jnp.mean(yg * yg, axis=-1, keepdims=True) + EPS)
            yn_ref[:, gs] = (yg * r * ng_ref[:, gs]).astype(yn_ref.dtype)

    return pl.pallas_call(
        body, name=name, grid=(nc,),
        in_specs=[pl.BlockSpec((CH, D), lambda c: (c, COL_Z // D)),
                  pl.BlockSpec((CH, XBC), lambda c: (c, 0)),
                  pl.BlockSpec((CH, 128), lambda c: (c, 0)),
                  pl.BlockSpec((8, 128), lambda c: (0, 0)),
                  pl.BlockSpec((1, D), lambda c: (0, 0))],
        out_specs=[pl.BlockSpec((CH, D), lambda c: (c, 0)), pl.BlockSpec((CH, D), lambda c: (c, 0)),
                   pl.BlockSpec((1, NH * HP, NS), lambda c: (c, 0, 0))],
        out_shape=[SDS((S, D), F32), SDS((S, D), _MXU), SDS((nc, NH * HP, NS), F32)],
        scratch_shapes=[pltpu.VMEM((NH * HP, NS), F32)],
        compiler_params=_cp("arbitrary"))(proj, xbc_c, dtr, sc, norm_g)


def _ssd_bwd(dmix, y, proj, xbc_c, dtr, hprev, sc, norm_g, name="ssd_bwd"):
    S = proj.shape[0]
    nc = S // CH
    rev = lambda c: nc - 1 - c

    def body(dyn_ref, y_ref, z_ref, x_ref, dtr_ref, hp_ref, sc_ref, ng_ref,
             dz_ref, dx_ref, ddtr_ref, gsc_ref, gng_ref, dh, dxd):
        @pl.when(pl.program_id(0) == 0)
        def _():
            dh[...] = jnp.zeros_like(dh)
            gsc_ref[...] = jnp.zeros_like(gsc_ref)
            gng_ref[...] = jnp.zeros_like(gng_ref)

        e, et, tril, triu = _head_consts()
        xbc_c = x_ref[...]
        dtr = dtr_ref[...]
        dtb = sc_ref[0:1, :]
        xbc, xs, dt, A, cs, csT, csL, wdec, dtE, ecsE, wE, eL = _ssd_common(
            xbc_c, dtr, dtb, sc_ref[1:2, :], e, tril, triu)
        xd = xs * dtE
        xw = xs * wE
        dE = _dot(jnp.broadcast_to(sc_ref[2:3, :], (8, 128)), e, exact="b", passes=3)[0:1, :]
        eLcol = jnp.sum(et * eL, axis=1, keepdims=True)

        yv = y_ref[...]
        zv = z_ref[...]
        sz = _silu(zv)
        yz = yv * sz
        dyn = dyn_ref[...]
        dyz_parts = []
        for g in range(NG):
            gs = slice(g * GW, (g + 1) * GW)
            yg = yz[:, gs]
            r = lax.rsqrt(jnp.mean(yg * yg, axis=-1, keepdims=True) + EPS)
            yh = yg * r
            dn = dyn[:, gs]
            gng_ref[:, gs] += jnp.sum(dn * yh, axis=0, keepdims=True)
            gdn = dn * ng_ref[:, gs]
            dyz_parts.append(r * (gdn - yh * jnp.mean(yh * gdn, axis=-1, keepdims=True)))
        dyz = jnp.concatenate(dyz_parts, axis=1)
        dy = dyz * sz
        dz_ref[...] = (dyz * yv * _dsilu(zv)).astype(dz_ref.dtype)

        gsc_ref[2:3, :] += jnp.sum(_dot(dy * xs, et, exact="b", passes=3), axis=0, keepdims=True)
        dxs = dE * dy
        dzo = ecsE * dy
        dcs = jnp.zeros((CH, 128), F32)
        dcsL = jnp.zeros((1, 128), F32)
        ddt = jnp.zeros((CH, 128), F32)
        dB_parts, dC_parts, yoff_parts, dxw_parts = [], [], [], []
        for g in range(NG):
            Bg = xbc[:, NH * HP + g * NS: NH * HP + (g + 1) * NS]
            Cg = xbc[:, NH * HP + NG * NS + g * NS: NH * HP + NG * NS + (g + 1) * NS]
            gs = slice(g * GW, (g + 1) * GW)
            hg = hp_ref[0, gs, :]
            dhn = dh[gs, :]
            G = _dot(Cg, Bg, "nt")
            yoff_parts.append(ecsE[:, gs] * _dot(Cg, hg, "nt"))
            dC = _dot(dzo[:, gs], hg)
            dhp = _dot(dzo[:, gs], Cg, "tn") + eLcol[gs, :] * dhn
            t1 = jnp.sum(dhn * hg, axis=1, keepdims=True) * eLcol[gs, :]
            dcsL = dcsL + jnp.sum(et[gs, :] * t1, axis=0, keepdims=True)
            dxw_parts.append(_dot(Bg, dhn, "nt"))
            dB = _dot(xw[:, gs], dhn)
            dgsum = jnp.zeros((CH, CH), F32)
            for hh in range(NH // NG):
                h = g * (NH // NG) + hh
                hs = slice(h * HP, (h + 1) * HP)
                m = jnp.where(tril, jnp.exp(jnp.where(tril, cs[:, h:h + 1] - csT[h:h + 1, :], 0.0)), 0.0)
                sc = G * m
                dyh = dy[:, hs]
                dxd[:, hs] = _dot(sc, dyh, "tn")
                dsc = _dot(dyh, xd[:, hs], "nt")
                q = dsc * sc
                oh = (lax.broadcasted_iota(jnp.int32, (CH, 128), 1) == h).astype(F32)
                dcs = dcs + _dot(q, oh, exact="b") - _dot(q, oh, "tn", exact="b")
                dgsum = dgsum + dsc * m
            dC_parts.append(dC + _dot(dgsum, Bg))
            dB_parts.append(dB + _dot(dgsum, Cg, "tn"))
            dh[gs, :] = dhp
        yoff = jnp.concatenate(yoff_parts, axis=1)
        dxw = jnp.concatenate(dxw_parts, axis=1)
        dxdv = dxd[...]
        dcs = dcs + _dot(dy * yoff, et, exact="b")
        dxs = dxs + wE * dxw + dtE * dxdv
        dw = _dot(dxw * xs, et, exact="b")
        ddt = ddt + dw * jnp.exp(csL - cs) + _dot(dxdv * xs, et, exact="b")
        dcs = dcs - dw * wdec
        dcsL = dcsL + jnp.sum(dw * wdec, axis=0, keepdims=True)
        last = lax.broadcasted_iota(jnp.int32, (CH, 128), 0) == CH - 1
        dcs = dcs + jnp.where(last, dcsL, 0.0)
        da = _dot(triu, dcs, exact="a", passes=3)
        ddt = ddt + da * A
        gsc_ref[1:2, :] += jnp.sum(da * dt, axis=0, keepdims=True) * A
        valid = lax.broadcasted_iota(jnp.int32, (CH, 128), 1) < NH
        ddtr = jnp.where(valid, ddt * _sigmoid(dtr + dtb), 0.0)
        gsc_ref[0:1, :] += jnp.sum(ddtr, axis=0, keepdims=True)
        ddtr_ref[...] = ddtr.astype(ddtr_ref.dtype)
        dxbc = jnp.concatenate([dxs] + dB_parts + dC_parts, axis=1)
        dx_ref[...] = dxbc * _dsilu(xbc_c)

    vec = pl.BlockSpec((8, 128), lambda c: (0, 0))
    vecd = pl.BlockSpec((1, D), lambda c: (0, 0))
    row = lambda w, j=0: pl.BlockSpec((CH, w), lambda c: (rev(c), j))
    return pl.pallas_call(
        body, name=name, grid=(nc,),
        in_specs=[row(D), row(D), row(D, COL_Z // D), row(XBC), row(128),
                  pl.BlockSpec((1, NH * HP, NS), lambda c: (rev(c), 0, 0)), vec, vecd],
        out_specs=[row(D), row(XBC), row(128), vec, vecd],
        out_shape=[SDS((S, D), _MXU), SDS((S, XBC), F32), SDS((S, 128), _MXU), SDS((8, 128), F32), SDS((1, D), F32)],
        scratch_shapes=[pltpu.VMEM((NH * HP, NS), F32), pltpu.VMEM((CH, NH * HP), F32)],
        compiler_params=_cp("arbitrary"))(dmix, y, proj, xbc_c, dtr, hprev, sc, norm_g)


CONV_RT = 32


def _fill_phases(ext, ph, rows):
    for s in range(1, 8):
        ph[s - 1, pl.ds(0, rows), :] = ext[pl.ds(s, rows), :]


def _window(ext, ph, off, r0, ls):
    s = off % 8
    src = ext if s == 0 else ph.at[s - 1]
    return src[pl.ds(pl.multiple_of(off - s + r0, 8), CONV_RT), ls]


def _cf_fwd(proj, w, b, lg, lb, name="cf_fwd", tb_cap=256):
    S = proj.shape[0]
    tb = _tile(S, tb_cap, 8)

    def body(a_ref, g_ref, w_ref, b_ref, lg_ref, lb_ref, u1_ref, u_ref, ext, ph):
        @pl.when(pl.program_id(0) == 0)
        def _():
            ext[pl.ds(0, CF_HALO), :] = jnp.zeros((CF_HALO, D), F32)

        ext[pl.ds(CF_HALO, tb), :] = a_ref[...] * _sigmoid(g_ref[...])
        _fill_phases(ext, ph, tb + CF_HALO - 8)

        def tile(i, carry):
            r0 = pl.multiple_of(i * CONV_RT, CONV_RT)
            for l in range(D // 128):
                ls = pl.ds(l * 128, 128)
                acc = jnp.broadcast_to(b_ref[:, ls], (CONV_RT, 128))
                for k in range(KC):
                    acc = acc + _window(ext, ph, CF_HALO - (KC - 1) + k, r0, ls) * w_ref[k:k + 1, ls]
                u1_ref[pl.ds(r0, CONV_RT), ls] = acc
            return carry

        lax.fori_loop(0, tb // CONV_RT, tile, 0)
        acc = u1_ref[...]
        mu = jnp.mean(acc, axis=-1, keepdims=True)
        xc = acc - mu
        r = lax.rsqrt(jnp.mean(xc * xc, axis=-1, keepdims=True) + EPS)
        u_ref[...] = _silu(xc * r * lg_ref[...] + lb_ref[...]).astype(u_ref.dtype)
        ext[pl.ds(0, CF_HALO), :] = ext[pl.ds(tb, CF_HALO), :]

    vec = pl.BlockSpec((1, D), lambda i: (0, 0))
    return pl.pallas_call(
        body, name=name, grid=(S // tb,),
        in_specs=[pl.BlockSpec((tb, D), lambda i: (i, COL_A // D)), pl.BlockSpec((tb, D), lambda i: (i, COL_G // D)),
                  pl.BlockSpec((KC, D), lambda i: (0, 0)), vec, vec, vec],
        out_specs=[pl.BlockSpec((tb, D), lambda i: (i, 0)), pl.BlockSpec((tb, D), lambda i: (i, 0))],
        out_shape=[SDS((S, D), F32), SDS((S, D), _MXU)],
        scratch_shapes=[pltpu.VMEM((CF_HALO + tb, D), F32), pltpu.VMEM((7, tb + CF_HALO - 8, D), F32)],
        compiler_params=_cp("arbitrary"))(proj, proj, w, b, lg, lb)


def _cf_bwd(dmix, u1, proj, w, lg, lb, name="cf_bwd", tb_cap=256):
    S = proj.shape[0]
    tb = _tile(S, tb_cap, 8)
    nb = S // tb
    rev = lambda i: nb - 1 - i

    def body(du_ref, u1_ref, a_ref, g_ref, w_ref, lg_ref, lb_ref,
             da_ref, dg_ref, dw_ref, db_ref, dlg_ref, dlb_ref, ext, ph, u0s):
        @pl.when(pl.program_id(0) == 0)
        def _():
            ext[pl.ds(tb, CF_HALO), :] = jnp.zeros((CF_HALO, D), F32)
            dw_ref[...] = jnp.zeros_like(dw_ref)
            db_ref[...] = jnp.zeros_like(db_ref)
            dlg_ref[...] = jnp.zeros_like(dlg_ref)
            dlb_ref[...] = jnp.zeros_like(dlb_ref)

        u1 = u1_ref[...]
        mu = jnp.mean(u1, axis=-1, keepdims=True)
        xc = u1 - mu
        r = lax.rsqrt(jnp.mean(xc * xc, axis=-1, keepdims=True) + EPS)
        xh = xc * r
        lgv = lg_ref[...]
        du2 = du_ref[...] * _dsilu(xh * lgv + lb_ref[...])
        dlg_ref[...] += jnp.sum(du2 * xh, axis=0, keepdims=True)
        dlb_ref[...] += jnp.sum(du2, axis=0, keepdims=True)
        gd = du2 * lgv
        du1 = r * (gd - jnp.mean(gd, axis=-1, keepdims=True) - xh * jnp.mean(gd * xh, axis=-1, keepdims=True))
        db_ref[...] += jnp.sum(du1, axis=0, keepdims=True)
        ext[pl.ds(0, tb), :] = du1
        u0s[...] = a_ref[...] * _sigmoid(g_ref[...])
        _fill_phases(ext, ph, tb + CF_HALO - 8)

        def dx_tile(i, carry):
            r0 = pl.multiple_of(i * CONV_RT, CONV_RT)
            rows = pl.ds(r0, CONV_RT)
            for l in range(D // 128):
                ls = pl.ds(l * 128, 128)
                acc = jnp.zeros((CONV_RT, 128), F32)
                for k in range(KC):
                    acc = acc + _window(ext, ph, KC - 1 - k, r0, ls) * w_ref[k:k + 1, ls]
                sg = _sigmoid(g_ref[rows, ls])
                da_ref[rows, ls] = (acc * sg).astype(da_ref.dtype)
                dg_ref[rows, ls] = (acc * a_ref[rows, ls] * sg * (1.0 - sg)).astype(dg_ref.dtype)
            return carry

        lax.fori_loop(0, tb // CONV_RT, dx_tile, 0)
        for l in range(D // 128):
            ls = pl.ds(l * 128, 128)

            def dw_tile(i, accs, ls=ls):
                r0 = pl.multiple_of(i * CONV_RT, CONV_RT)
                u0t = u0s[pl.ds(r0, CONV_RT), ls]
                out = []
                for k in range(KC):
                    p = u0t * _window(ext, ph, KC - 1 - k, r0, ls)
                    out.append(accs[k] + ((p[0:8] + p[8:16]) + (p[16:24] + p[24:32])))
                return tuple(out)

            accs = lax.fori_loop(0, tb // CONV_RT, dw_tile, tuple(jnp.zeros((8, 128), F32) for _ in range(KC)))
            for k in range(KC):
                dw_ref[k:k + 1, ls] += jnp.sum(accs[k], axis=0, keepdims=True)
        ext[pl.ds(tb, CF_HALO), :] = ext[pl.ds(0, CF_HALO), :]

    vec = pl.BlockSpec((1, D), lambda i: (0, 0))
    wsp = pl.BlockSpec((KC, D), lambda i: (0, 0))
    row = lambda j=0: pl.BlockSpec((tb, D), lambda i: (rev(i), j))
    return pl.pallas_call(
        body, name=name, grid=(nb,),
        in_specs=[row(1), row(), row(COL_A // D), row(COL_G // D), wsp, vec, vec],
        out_specs=[row(), row(), wsp, vec, vec, vec],
        out_shape=[SDS((S, D), _MXU), SDS((S, D), _MXU), SDS((KC, D), F32),
                   SDS((1, D), F32), SDS((1, D), F32), SDS((1, D), F32)],
        scratch_shapes=[pltpu.VMEM((tb + CF_HALO, D), F32), pltpu.VMEM((7, tb + CF_HALO - 8, D), F32),
                        pltpu.VMEM((tb, D), F32)],
        compiler_params=_cp("arbitrary"))(dmix, u1, proj, proj, w, lg, lb)


def _attn_fwd(q, kv, name="attn_fwd", tq_cap=512):
    S = q.shape[0]
    tq = _tile(S, tq_cap, 8)
    scale = XD ** -0.5

    def body(q_ref, kv_ref, o_ref):
        for h in range(XH):
            hs = slice(h * XD, (h + 1) * XD)
            s = _dot(q_ref[:, hs], kv_ref[:, hs], "nt") * scale
            s = s - jnp.max(s, axis=-1, keepdims=True)
            p = jnp.exp(s)
            p = p / jnp.sum(p, axis=-1, keepdims=True)
            o_ref[:, hs] = _dot(p, kv_ref[:, D + h * XD: D + (h + 1) * XD]).astype(o_ref.dtype)

    return pl.pallas_call(
        body, name=name, grid=(S // tq,),
        in_specs=[pl.BlockSpec((tq, D), lambda i: (i, 0)), pl.BlockSpec((MEM, 2 * D), lambda i: (0, 0))],
        out_specs=pl.BlockSpec((tq, D), lambda i: (i, 0)), out_shape=SDS((S, D), _MXU),
        compiler_params=_cp("parallel"))(q, kv)


def _attn_bwd(do, q, kv, name="attn_bwd", tq_cap=512):
    S = q.shape[0]
    tq = _tile(S, tq_cap, 8)
    scale = XD ** -0.5

    def body(do_ref, q_ref, kv_ref, dq_ref, dkv_ref):
        @pl.when(pl.program_id(0) == 0)
        def _():
            dkv_ref[...] = jnp.zeros_like(dkv_ref)

        for h in range(XH):
            hs = slice(h * XD, (h + 1) * XD)
            vs = slice(D + h * XD, D + (h + 1) * XD)
            qh = q_ref[:, hs]
            kh = kv_ref[:, hs]
            s = _dot(qh, kh, "nt") * scale
            s = s - jnp.max(s, axis=-1, keepdims=True)
            p = jnp.exp(s)
            p = p / jnp.sum(p, axis=-1, keepdims=True)
            doh = do_ref[:, hs]
            dp = _dot(doh, kv_ref[:, vs], "nt")
            ds = p * (dp - jnp.sum(dp * p, axis=-1, keepdims=True)) * scale
            dq_ref[:, hs] = _dot(ds, kh).astype(dq_ref.dtype)
            dkv_ref[:, hs] += _dot(ds, qh, "tn")
            dkv_ref[:, vs] += _dot(p, doh, "tn")

    return pl.pallas_call(
        body, name=name, grid=(S // tq,),
        in_specs=[pl.BlockSpec((tq, D), lambda i: (i, 0)), pl.BlockSpec((tq, D), lambda i: (i, 0)),
                  pl.BlockSpec((MEM, 2 * D), lambda i: (0, 0))],
        out_specs=[pl.BlockSpec((tq, D), lambda i: (i, 0)), pl.BlockSpec((MEM, 2 * D), lambda i: (0, 0))],
        out_shape=[SDS((S, D), _MXU), SDS((MEM, 2 * D), F32)],
        compiler_params=_cp("arbitrary"))(do, q, kv)


def _ffn_act(gu, name="ffn_act", tb_cap=128):
    S = gu.shape[0]
    tb = _tile(S, tb_cap, 8)

    def body(g_ref, u_ref, o_ref):
        o_ref[...] = (_silu(g_ref[...]) * u_ref[...]).astype(o_ref.dtype)

    return pl.pallas_call(
        body, name=name, grid=(S // tb,),
        in_specs=[pl.BlockSpec((tb, DFF), lambda i: (i, 0)), pl.BlockSpec((tb, DFF), lambda i: (i, 1))],
        out_specs=pl.BlockSpec((tb, DFF), lambda i: (i, 0)), out_shape=SDS((S, DFF), _MXU),
        compiler_params=_cp("parallel"))(gu, gu)


def _ffn_act_bwd(dact, gu, name="ffn_act_bwd", tb_cap=128):
    S = gu.shape[0]
    tb = _tile(S, tb_cap, 8)

    def body(d_ref, g_ref, u_ref, dg_ref, du_ref):
        gt = g_ref[...]
        d = d_ref[...]
        s = _sigmoid(gt)
        dg_ref[...] = (d * u_ref[...] * (s * (1.0 + gt * (1.0 - s)))).astype(dg_ref.dtype)
        du_ref[...] = (d * gt * s).astype(du_ref.dtype)

    blk = pl.BlockSpec((tb, DFF), lambda i: (i, 0))
    return pl.pallas_call(
        body, name=name, grid=(S // tb,),
        in_specs=[blk, blk, pl.BlockSpec((tb, DFF), lambda i: (i, 1))],
        out_specs=[blk, blk], out_shape=[SDS((S, DFF), _MXU), SDS((S, DFF), _MXU)],
        compiler_params=_cp("parallel"))(dact, gu, gu)


def _local_step(x, mem, tgt, W, P):
    h = _rms_fwd(x, P["g_mix"], "rms_mix")
    proj = _mm_nn(h, W["main"], "in_proj", tn_cap=1152)
    dtr = _mm_nn(h, W["dt"], "in_proj_dt")
    xbc_c = _ssd_conv_fwd(proj, P["conv4_w"], P["conv4_b"])
    y, yn, hprev = _ssd_fwd(proj, xbc_c, dtr, P["sc"], P["ssd_norm_g"])
    u1, u = _cf_fwd(proj, P["cf_w"], P["cf_b"], P["ln_g"], P["ln_b"])
    mix = jnp.concatenate([yn, u], axis=1)
    x1 = _mm_nn(mix, W["out"], "out_proj", add=x)
    hq = _rms_fwd(x1, P["g_xattn"], "rms_xattn")
    q = _mm_nn(hq, W["q"], "q_proj")
    mn = _rms_fwd(mem, P["g_mem"], "rms_mem")
    kv = _mm_nn(mn, W["kv"], "kv_proj")
    o = _attn_fwd(q, kv)
    x2 = _mm_nn(o, W["o"], "o_proj", add=x1)
    hf = _rms_fwd(x2, P["g_ffn"], "rms_ffn")
    gu = _mm_nn(hf, W["gu"], "ffn_in")
    act = _ffn_act(gu)
    x3 = _mm_nn(act, W["down"], "ffn_out", add=x2)
    loss, dx3, dx3b, g_final = _final_loss(x3, P["g_final"], tgt)
    GW, GP = {}, {"g_final": g_final}
    dact = _mm_nt(dx3b, W["down"], "ffn_out_dx", tk_cap=1408)
    GW["down"] = _mm_tn(act, dx3b, "ffn_out_dw", tk_cap=1408, tn_cap=1024)
    dgt, dup = _ffn_act_bwd(dact, gu)
    dhf = _mm_nt(dgt, W["gu"], "ffn_gate_dx", b_col=0)
    dhf = _mm_nt(dup, W["gu"], "ffn_up_dx", b_col=1, add=dhf)
    GW["gate"] = _mm_tn(hf, dgt, "ffn_gate_dw")
    GW["up"] = _mm_tn(hf, dup, "ffn_up_dw")
    dx2, dx2b, GP["g_ffn"] = _rms_bwd(x2, P["g_ffn"], dhf, dx3, "rms_ffn_bwd")
    do = _mm_nt(dx2b, W["o"], "o_proj_dx")
    GW["o"] = _mm_tn(o, dx2b, "o_proj_dw")
    dq, dkv = _attn_bwd(do, q, kv)
    dhq = _mm_nt(dq, W["q"], "q_proj_dx")
    GW["q"] = _mm_tn(hq, dq, "q_proj_dw")
    dkvb = dkv.astype(_MXU)
    GW["kv"] = _mm_tn(mn, dkvb, "kv_proj_dw", tm_cap=256)
    dmn = _mm_nt(dkvb, W["kv"], "kv_proj_dx")
    GP["g_mem"] = _rms_bwd(mem, P["g_mem"], dmn, None, "rms_mem_bwd")
    dx1, dx1b, GP["g_xattn"] = _rms_bwd(x1, P["g_xattn"], dhq, dx2, "rms_xattn_bwd")
    dmix = _mm_nt(dx1b, W["out"], "out_proj_dx")
    GW["out"] = _mm_tn(mix, dx1b, "out_proj_dw", tn_cap=1024)
    da, dg, GP["cf_w"], GP["cf_b"], GP["ln_g"], GP["ln_b"] = _cf_bwd(dmix, u1, proj, P["cf_w"], P["ln_g"], P["ln_b"])
    dz, dxbc_c, ddtr, GP["sc"], GP["ssd_norm_g"] = _ssd_bwd(dmix, y, proj, xbc_c, dtr, hprev, P["sc"], P["ssd_norm_g"])
    dxbc, GP["conv4_w"], GP["conv4_b"] = _ssd_conv_bwd(dxbc_c, proj, P["conv4_w"])
    dproj = jnp.concatenate([dz, da, dg, dxbc], axis=1)
    dh = _mm_nt(ddtr, W["dt"], "in_proj_dt_dx")
    dh = _mm_nt(dproj, W["main"], "in_proj_dx", add=dh, tk_cap=512)
    GW["main"] = _mm_tn(h, dproj, "in_proj_dw", tn_cap=1152)
    GW["dt"] = _mm_tn(h, ddtr, "in_proj_dt_dw")
    grad_x, GP["g_mix"] = _rms_bwd(x, P["g_mix"], dh, dx1, "rms_mix_bwd", low=False)
    return loss, grad_x, GW, GP


Z_END, XBC_END, DT_END = NH * HP, NH * HP + XBC, NH * HP + XBC + NH
NFB = DFF // FB


def _pad_to(a, rows=None, cols=None):
    r = 0 if rows is None else rows - a.shape[0]
    c = 0 if cols is None else cols - a.shape[1]
    return jnp.pad(a, ((0, r), (0, c)))


IN_W = DT_END + 2 * D
W_IN_SEGS = [(0, Z_END, "main", COL_Z), (Z_END, XBC_END, "main", COL_XBC), (XBC_END, DT_END, "dt", 0),
             (DT_END, DT_END + D, "main", COL_A), (DT_END + D, IN_W, "main", COL_G)]
BIG = [("w_in", True), ("w_out", False), ("w_q", False), ("w_kv", True), ("w_o", False), ("w_gate", True),
       ("w_up", True), ("w_down", False)]


def _ref_cols(pieces, a, b):
    cw = IN_W // 4
    out = []
    for j in range(4):
        lo, hi = max(a, j * cw), min(b, (j + 1) * cw)
        if lo < hi:
            out.append(pieces[j][:, lo - j * cw:hi - j * cw])
    return out


def _cat_cols(pieces):
    return jnp.concatenate([pieces[j] for j in range(4)], axis=1)


def _pack_weights(pc):
    w_in = pc["w_in"]
    main = jnp.concatenate(_ref_cols(w_in, 0, Z_END) + _ref_cols(w_in, DT_END, IN_W) + _ref_cols(w_in, Z_END, XBC_END), axis=1)
    dt = _pad_to(jnp.concatenate(_ref_cols(w_in, XBC_END, DT_END), axis=1), cols=128)
    gu = jnp.concatenate([pc["w_gate"][j] for j in range(4)] + [pc["w_up"][j] for j in range(4)], axis=1)
    rows = lambda n: pc[n].reshape(-1, pc[n].shape[-1])
    return {"main": main, "dt": dt, "out": rows("w_out"), "q": rows("w_q"), "kv": _cat_cols(pc["w_kv"]),
            "o": rows("w_o"), "gu": gu, "down": rows("w_down")}


def _shard_grads(GW):
    cw = IN_W // 4
    pieces = []
    for j in range(4):
        parts = []
        for a, b, src, col in W_IN_SEGS:
            lo, hi = max(a, j * cw), min(b, (j + 1) * cw)
            if lo < hi:
                parts.append(GW[src][:, col + lo - a:col + hi - a])
        pieces.append(jnp.concatenate(parts, axis=1))
    g = {"w_in": jnp.stack(pieces)}
    for k, n in (("gate", "w_gate"), ("up", "w_up"), ("kv", "w_kv")):
        cw = GW[k].shape[1] // 4
        g[n] = jnp.stack([GW[k][:, j * cw:(j + 1) * cw] for j in range(4)])
    for k, n in (("out", "w_out"), ("q", "w_q"), ("o", "w_o"), ("down", "w_down")):
        g[n] = GW[k].reshape(4, GW[k].shape[0] // 4, GW[k].shape[1])
    return g


def _stack_sc(dt_bias, a_log, d):
    return _pad_to(jnp.concatenate([dt_bias, a_log, d], axis=0), rows=8, cols=128)


HBM_SPEC = pl.BlockSpec(memory_space=pl.ANY)
COMM_PARAMS = pltpu.CompilerParams(vmem_limit_bytes=VMEM_LIMIT)


def _chip_peers(x, y):
    return [(1 - x, y), (x, 1 - y), (1 - x, 1 - y)]


def _remote(src, dst, send_sem, recv_sem, dev):
    return pltpu.make_async_remote_copy(src_ref=src, dst_ref=dst, send_sem=send_sem, recv_sem=recv_sem,
                                        device_id=dev, device_id_type=MESHID)


def _dma_sems(*counts):
    return [pltpu.SemaphoreType.DMA((n,)) for n in counts]


def _allgather_list(arrs, name):
    n = len(arrs)
    halved = [a.shape[0] % 16 == 0 for a in arrs]
    oshape = [(4, 2, a.shape[0] // 2, a.shape[1]) if h else (4, 1) + a.shape for a, h in zip(arrs, halved)]

    def body(*refs):
        srcs, outs = refs[:n], refs[n:2 * n]
        ici_send, ici_recv, own_send, own_recv, fwd_send, fwd_recv = refs[2 * n:]
        x, y, c = lax.axis_index("x"), lax.axis_index("y"), lax.axis_index("c")
        me = 2 * x + y
        sib = (x, y, 1 - c)
        peers = _chip_peers(x, y)

        def half(i, h):
            r = arrs[i].shape[0] // 2
            if not halved[i]:
                return srcs[i]
            return srcs[i].at[pl.ds(h * r if isinstance(h, int) else pl.multiple_of(h * r, 8), r)]

        ici, own, fwd = [], [], []
        for i in range(n):
            mine_h = c if halved[i] else 0
            for k, (px, py) in enumerate(peers):
                s = 3 * i + k
                ici.append(_remote(half(i, c), outs[i].at[me, mine_h], ici_send.at[s], ici_recv.at[s], (px, py, c)))
            for h in range(2 if halved[i] else 1):
                s = 2 * i + h
                own.append(_remote(half(i, h), outs[i].at[me, h], own_send.at[s], own_recv.at[s], sib))
        for cp in ici + own:
            cp.start()
        for i in range(n):
            if not halved[i]:
                continue
            for k, (px, py) in enumerate(peers):
                s = 3 * i + k
                got = outs[i].at[2 * px + py, c]
                _remote(half(i, c), got, ici_send.at[s], ici_recv.at[s], (px, py, c)).wait_recv()
                f = _remote(got, got, fwd_send.at[s], fwd_recv.at[s], sib)
                f.start()
                fwd.append(f)
        for i in range(n):
            for k, (px, py) in enumerate(peers):
                s = 3 * i + k
                if halved[i]:
                    _remote(half(i, c), outs[i].at[2 * px + py, 1 - c], fwd_send.at[s], fwd_recv.at[s], sib).wait_recv()
                else:
                    _remote(srcs[i], outs[i].at[2 * px + py, 0], ici_send.at[s], ici_recv.at[s], (px, py, c)).wait_recv()
            for h in range(2 if halved[i] else 1):
                s = 2 * i + h
                _remote(half(i, h), outs[i].at[me, h], own_send.at[s], own_recv.at[s], sib).wait_recv()
        for cp in ici + own + fwd:
            cp.wait_send()

    outs = pl.pallas_call(
        body, name=name, in_specs=[HBM_SPEC] * n, out_specs=[HBM_SPEC] * n,
        out_shape=[SDS(s, a.dtype) for s, a in zip(oshape, arrs)],
        scratch_shapes=_dma_sems(3 * n, 3 * n, 2 * n, 2 * n, 3 * n, 3 * n), compiler_params=COMM_PARAMS)(*arrs)
    return [o.reshape((4,) + a.shape) for o, a in zip(outs, arrs)]


def _pair_split_list(gs, name="rs_pair_send"):
    n = len(gs)

    def body(*refs):
        srcs, outs = refs[:n], refs[n:2 * n]
        send_sems, recv_sems = refs[2 * n:]
        x, y, c = lax.axis_index("x"), lax.axis_index("y"), lax.axis_index("c")
        sib = (x, y, 1 - c)
        sends = [_remote(srcs[i].at[j, 1 - c], outs[i].at[j], send_sems.at[4 * i + j], recv_sems.at[4 * i + j], sib)
                 for i in range(n) for j in range(4)]
        for cp in sends:
            cp.start()
        for cp in sends:
            cp.wait_recv()
        for cp in sends:
            cp.wait_send()

    return pl.pallas_call(
        body, name=name, in_specs=[HBM_SPEC] * n, out_specs=[HBM_SPEC] * n,
        out_shape=[SDS((4,) + g.shape[2:], g.dtype) for g in gs],
        scratch_shapes=_dma_sems(4 * n, 4 * n), compiler_params=COMM_PARAMS)(*gs)


def _scatter_list(ps, name="rs_chip_send"):
    n = len(ps)

    def body(*refs):
        srcs, outs = refs[:n], refs[n:2 * n]
        send_sems, recv_sems = refs[2 * n:]
        x, y, c = lax.axis_index("x"), lax.axis_index("y"), lax.axis_index("c")
        me = 2 * x + y
        peers = _chip_peers(x, y)
        sends = [_remote(srcs[i].at[2 * px + py], outs[i].at[me], send_sems.at[3 * i + k], recv_sems.at[3 * i + k],
                         (px, py, c)) for i in range(n) for k, (px, py) in enumerate(peers)]
        for cp in sends:
            cp.start()
        for i in range(n):
            for k, (px, py) in enumerate(peers):
                _remote(srcs[i].at[me], outs[i].at[2 * px + py], send_sems.at[3 * i + k], recv_sems.at[3 * i + k],
                        (px, py, c)).wait_recv()
        for cp in sends:
            cp.wait_send()

    return pl.pallas_call(
        body, name=name, in_specs=[HBM_SPEC] * n, out_specs=[HBM_SPEC] * n,
        out_shape=[SDS(p.shape, p.dtype) for p in ps],
        scratch_shapes=_dma_sems(3 * n, 3 * n), compiler_params=COMM_PARAMS)(*ps)


JOIN_SPLIT = 4


def _pair_join_list(bufs, name="rs_pair_join"):
    n = len(bufs)

    def body(*refs):
        outs = refs[n:2 * n]
        send_sems, recv_sems = refs[2 * n:]
        x, y, c = lax.axis_index("x"), lax.axis_index("y"), lax.axis_index("c")
        sib = (x, y, 1 - c)
        sends, recvs = [], []
        for i in range(n):
            rc = bufs[i].shape[1] // JOIN_SPLIT
            for q in range(JOIN_SPLIT):
                k = JOIN_SPLIT * i + q
                rows = pl.ds(q * rc, rc)
                sends.append(_remote(outs[i].at[c, rows], outs[i].at[c, rows], send_sems.at[k], recv_sems.at[k], sib))
                recvs.append(_remote(outs[i].at[c, rows], outs[i].at[1 - c, rows], send_sems.at[k], recv_sems.at[k], sib))
        for cp in sends:
            cp.start()
        for cp in recvs:
            cp.wait_recv()
        for cp in sends:
            cp.wait_send()

    return pl.pallas_call(
        body, name=name, in_specs=[HBM_SPEC] * n, out_specs=[HBM_SPEC] * n,
        out_shape=[SDS(b.shape, b.dtype) for b in bufs], input_output_aliases={i: i for i in range(n)},
        scratch_shapes=_dma_sems(JOIN_SPLIT * n, JOIN_SPLIT * n), compiler_params=COMM_PARAMS)(*bufs)


def _pair_sum(g, theirs, core, name):
    _, _, r, c = g.shape

    def body(core_ref, g_ref, t_ref, o_ref):
        o_ref[...] = (g_ref[...] + t_ref[...]).astype(o_ref.dtype)

    spec = pltpu.PrefetchScalarGridSpec(
        num_scalar_prefetch=1, grid=(4,),
        in_specs=[pl.BlockSpec((None, None, r, c), lambda j, core_ref: (j, core_ref[0], 0, 0)),
                  pl.BlockSpec((None, r, c), lambda j, core_ref: (j, 0, 0))],
        out_specs=pl.BlockSpec((None, r, c), lambda j, core_ref: (j, 0, 0)))
    return pl.pallas_call(body, name=name, grid_spec=spec, out_shape=SDS((4, r, c), BF16),
                          compiler_params=_cp("parallel"))(core, g, theirs)


def _chip_sum(own, got, where, name):
    _, r, c = own.shape
    tr = r // 2

    def body(w_ref, a_ref, b1_ref, b2_ref, b3_ref, o_ref):
        o_ref[...] = ((a_ref[...].astype(F32) + b1_ref[...].astype(F32)) + b2_ref[...].astype(F32)) + b3_ref[...].astype(F32)

    piece = lambda k: pl.BlockSpec((None, tr, c), lambda i, w_ref: ((w_ref[0] + k) % 4, i, 0))
    spec = pltpu.PrefetchScalarGridSpec(
        num_scalar_prefetch=1, grid=(r // tr,), in_specs=[piece(0), piece(1), piece(2), piece(3)],
        out_specs=pl.BlockSpec((None, tr, c), lambda i, w_ref: (w_ref[1], i, 0)))
    return pl.pallas_call(body, name=name, grid_spec=spec, out_shape=SDS((2, r, c), F32),
                          compiler_params=_cp("parallel"))(where, own, got, got, got)


def _adam_math(w, g, m, v):
    bc1 = 1.0 - ADAM_B1 ** ADAM_STEP
    bc2 = 1.0 - ADAM_B2 ** ADAM_STEP
    mn = ADAM_B1 * m + (1.0 - ADAM_B1) * g
    vn = ADAM_B2 * v + (1.0 - ADAM_B2) * (g * g)
    return -ADAM_LR * ((mn / bc1) / (jnp.sqrt(vn / bc2) + ADAM_EPS) + ADAM_WD * w), mn, vn


PACK_COLS = XBC
PACK = {"g_mix": (0, 1, D), "g_xattn": (1, 1, D), "g_mem": (2, 1, D), "g_ffn": (3, 1, D), "g_final": (4, 1, D),
        "ssd_norm_g": (5, 1, D), "cf_b": (6, 1, D), "ln_g": (7, 1, D), "ln_b": (8, 1, D), "conv4_b": (9, 1, XBC),
        "conv4_w": (10, KS, XBC), "sc": (16, 8, 128), "cf_w": (24, KC, D), "loss": (55, 1, 128)}
PACK_ROWS = 56
SMALL_ADAM = ["g_mix", "g_xattn", "g_mem", "g_ffn", "g_final", "ssd_norm_g", "cf_b", "ln_g", "ln_b", "conv4_b", "sc"]


def _small_allreduce_adamw(grads, wts, mom, var, name="allreduce_small"):
    gk = list(PACK)
    ng, na = len(gk), len(SMALL_ADAM)

    def body(*refs):
        g_in = refs[:ng]
        w_in, m_in, v_in = (refs[ng + i * na: ng + (i + 1) * na] for i in range(3))
        o = refs[ng + 3 * na:]
        g_out = o[:ng]
        d_out, m_out, v_out = (o[ng + i * na: ng + (i + 1) * na] for i in range(3))
        pack, buf, acc, send_sems, recv_sems = o[ng + 3 * na:]
        x, y, c = lax.axis_index("x"), lax.axis_index("y"), lax.axis_index("c")
        me = 4 * x + 2 * y + c
        pack[...] = jnp.zeros_like(pack)
        for i, k in enumerate(gk):
            r0, nr, nc = PACK[k]
            pack[r0:r0 + nr, 0:nc] = g_in[i][...]
        peers = [(x, y, 1 - c)] + [(px, py, pc) for px, py in _chip_peers(x, y) for pc in (c, 1 - c)]
        sends = [_remote(pack, buf.at[me], send_sems.at[k], recv_sems.at[k], dev) for k, dev in enumerate(peers)]
        for cp in sends:
            cp.start()
        buf[me] = pack[...]
        for k, (px, py, pc) in enumerate(peers):
            _remote(pack, buf.at[4 * px + 2 * py + pc], send_sems.at[k], recv_sems.at[k], (px, py, pc)).wait_recv()
        for cp in sends:
            cp.wait_send()
        tot = buf[0]
        for i in range(1, 8):
            tot = tot + buf[i]
        acc[...] = tot
        for i, k in enumerate(gk):
            r0, nr, nc = PACK[k]
            g_out[i][...] = acc[r0:r0 + nr, 0:nc]
        for i, k in enumerate(SMALL_ADAM):
            r0, nr, nc = PACK[k]
            d_out[i][...], m_out[i][...], v_out[i][...] = _adam_math(
                w_in[i][...], acc[r0:r0 + nr, 0:nc], m_in[i][...], v_in[i][...])

    args = [grads[k] for k in gk] + [d[k] for d in (wts, mom, var) for k in SMALL_ADAM]
    shp = lambda k: SDS((PACK[k][1], PACK[k][2]), F32)
    vm = pl.BlockSpec(memory_space=pltpu.VMEM)
    outs = pl.pallas_call(
        body, name=name, in_specs=[vm] * len(args), out_specs=[vm] * (ng + 3 * na),
        out_shape=[shp(k) for k in gk] + [shp(k) for _ in range(3) for k in SMALL_ADAM],
        scratch_shapes=[pltpu.VMEM((PACK_ROWS, PACK_COLS), F32), pltpu.VMEM((8, PACK_ROWS, PACK_COLS), F32),
                        pltpu.VMEM((PACK_ROWS, PACK_COLS), F32)] + _dma_sems(7, 7),
        compiler_params=COMM_PARAMS)(*args)
    red = dict(zip(gk, outs[:ng]))
    parts = [dict(zip(SMALL_ADAM, outs[ng + i * na: ng + (i + 1) * na])) for i in range(3)]
    return red, parts[0], parts[1], parts[2]


def _adamw_cols(w, gfull, m, v, chip, name):
    _, R, C = w.shape

    def body(w_idx, w_ref, g_ref, m_ref, v_ref, go_ref, d_ref, mo_ref, vo_ref):
        go_ref[...] = g_ref[...]
        d_ref[...], mo_ref[...], vo_ref[...] = _adam_math(w_ref[...], g_ref[...], m_ref[...], v_ref[...])

    blk = pl.BlockSpec((None, R, C), lambda i, w_idx: (0, 0, 0))
    spec = pltpu.PrefetchScalarGridSpec(
        num_scalar_prefetch=1, grid=(1,),
        in_specs=[blk, pl.BlockSpec((R, C), lambda i, w_idx: (0, w_idx[0])), blk, blk], out_specs=[blk] * 4)
    return pl.pallas_call(body, name=name, grid_spec=spec, out_shape=[SDS((1, R, C), F32)] * 4,
                          compiler_params=_cp("arbitrary"))(chip, w, gfull, m, v)


def _adamw(w, g, m, v, name):
    _, R, C = w.shape
    half = R // 2
    tr = _tile(half, max(8, (2 ** 17 // C) // 8 * 8), 8)
    nh = half // tr

    def body(w_ref, g_ref, m_ref, v_ref, go_ref, d_ref, mo_ref, vo_ref):
        go_ref[...] = g_ref[...]
        d_ref[...], mo_ref[...], vo_ref[...] = _adam_math(w_ref[...], g_ref[...], m_ref[...], v_ref[...])

    blk = pl.BlockSpec((None, tr, C), lambda i: (0, i, 0))
    gblk = pl.BlockSpec((None, tr, C), lambda i: (i // nh, i % nh, 0))
    return pl.pallas_call(body, name=name, grid=(R // tr,), in_specs=[blk, gblk, blk, blk], out_specs=[blk] * 4,
                          out_shape=[SDS((1, R, C), F32)] * 4, compiler_params=_cp("parallel"))(w, g, m, v)


WEIGHT_NAMES = ["norm_mix_g", "w_in", "ssd_conv_w", "ssd_conv_b", "ssd_dt_bias", "ssd_A_log", "ssd_D", "ssd_norm_g",
                "cf_conv_w", "cf_conv_b", "cf_ln_g", "cf_ln_b", "w_out", "norm_xattn_g", "norm_mem_g", "w_q", "w_kv",
                "w_o", "norm_ffn_g", "w_gate", "w_up", "w_down", "norm_final_g"]
VEC_REF = [("norm_mix_g", "g_mix"), ("norm_xattn_g", "g_xattn"), ("norm_mem_g", "g_mem"), ("norm_ffn_g", "g_ffn"),
           ("norm_final_g", "g_final"), ("ssd_norm_g", "ssd_norm_g"), ("cf_conv_b", "cf_b"), ("cf_ln_g", "ln_g"),
           ("cf_ln_b", "ln_b"), ("ssd_conv_b", "conv4_b")]
SC_REF = ["ssd_dt_bias", "ssd_A_log", "ssd_D"]


def _small_side(get):
    d = {k: get(ref_name).reshape(1, -1) for ref_name, k in VEC_REF}
    d["sc"] = _stack_sc(*[get(n) for n in SC_REF])
    return d


def kernel(x, mem, norm_mix_g, w_in, ssd_conv_w, ssd_conv_b, ssd_dt_bias, ssd_A_log, ssd_D, ssd_norm_g, cf_conv_w, cf_conv_b, cf_ln_g, cf_ln_b, w_out, norm_xattn_g, norm_mem_g, w_q, w_kv, w_o, norm_ffn_g, w_gate, w_up, w_down, norm_final_g, loss_target, m_norm_mix_g, m_w_in, m_ssd_conv_w, m_ssd_conv_b, m_ssd_dt_bias, m_ssd_A_log, m_ssd_D, m_ssd_norm_g, m_cf_conv_w, m_cf_conv_b, m_cf_ln_g, m_cf_ln_b, m_w_out, m_norm_xattn_g, m_norm_mem_g, m_w_q, m_w_kv, m_w_o, m_norm_ffn_g, m_w_gate, m_w_up, m_w_down, m_norm_final_g, v_norm_mix_g, v_w_in, v_ssd_conv_w, v_ssd_conv_b, v_ssd_dt_bias, v_ssd_A_log, v_ssd_D, v_ssd_norm_g, v_cf_conv_w, v_cf_conv_b, v_cf_ln_g, v_cf_ln_b, v_w_out, v_norm_xattn_g, v_norm_mem_g, v_w_q, v_w_kv, v_w_o, v_norm_ffn_g, v_w_gate, v_w_up, v_w_down, v_norm_final_g):
    env = dict(locals())
    wts = {n: env[n] for n in WEIGHT_NAMES}
    mom = {n: env["m_" + n] for n in WEIGHT_NAMES}
    var = {n: env["v_" + n] for n in WEIGHT_NAMES}
    chip = (2 * lax.axis_index("x") + lax.axis_index("y")).astype(jnp.int32).reshape(1)
    core = lax.axis_index("c").astype(jnp.int32).reshape(1)
    where = jnp.concatenate([chip, core])
    big = [n for n, _ in BIG]

    gathered = _allgather_list([wts[n][0].astype(BF16) for n in big] + [ssd_conv_w[0], cf_conv_w[0]], "allgather_weights")
    W = _pack_weights(dict(zip(big, gathered)))
    P = _small_side(lambda n: wts[n])
    P["conv4_w"], P["cf_w"] = _cat_cols(gathered[len(big)]), _cat_cols(gathered[len(big) + 1])

    loss, grad_x, GW, GP = _local_step(x[0], mem[0], loss_target[0], W, P)

    gs = _shard_grads(GW)
    halves = [gs[n].reshape(4, 2, gs[n].shape[1] // 2, gs[n].shape[2]) for n in big]
    theirs = _pair_split_list(halves)
    pair = [_pair_sum(h, t, core, "rs_pair_sum_" + n) for h, t, n in zip(halves, theirs, big)]
    got = _scatter_list(pair)
    joined = _pair_join_list([_chip_sum(p, q, where, "rs_chip_sum_" + n) for p, q, n in zip(pair, got, big)])
    gshard = dict(zip(big, joined))

    small = dict(GP)
    small["loss"] = loss
    red, sd, sm, sv = _small_allreduce_adamw(small, {k: P[k] for k in SMALL_ADAM}, _small_side(lambda n: mom[n]),
                                             _small_side(lambda n: var[n]))
    grads, delta, new_m, new_v = {}, {}, {}, {}
    for ref_name, k in VEC_REF:
        shp = wts[ref_name].shape
        for dst, src in ((grads, red), (delta, sd), (new_m, sm), (new_v, sv)):
            dst[ref_name] = src[k].reshape(shp)
    for row, ref_name in enumerate(SC_REF):
        for dst, src in ((grads, red), (delta, sd), (new_m, sm), (new_v, sv)):
            dst[ref_name] = src["sc"][row:row + 1, :NH]

    for n, k in (("ssd_conv_w", "conv4_w"), ("cf_conv_w", "cf_w")):
        grads[n], delta[n], new_m[n], new_v[n] = _adamw_cols(wts[n], red[k], mom[n], var[n], chip, "adamw_" + n)
    for n in big:
        grads[n], delta[n], new_m[n], new_v[n] = _adamw(wts[n], gshard[n], mom[n], var[n], "adamw_" + n)

    return (red["loss"][0, 0], grad_x[None], *[grads[n] for n in WEIGHT_NAMES], *[delta[n] for n in WEIGHT_NAMES],
            *[new_m[n] for n in WEIGHT_NAMES], *[new_v[n] for n in WEIGHT_NAMES])
```

```python
import functools
import math

import jax
import jax.numpy as jnp
from jax import lax
from jax.experimental import pallas as pl
from jax.experimental.pallas import tpu as pltpu

F32 = jnp.float32
BF16 = jnp.bfloat16
_MXU = BF16

D = 1024
MEM = 256
NH, HP, NG, NS = 16, 64, 2, 128
GW = NH * HP // NG
CH = 128
XBC = NH * HP + 2 * NG * NS
KS, KC = 4, 31
XH, XD = 4, 256
DFF = 2816
FB = 256
EPS = 1e-6
COL_Z, COL_A, COL_G, COL_XBC, MAINW = 0, 1024, 2048, 3072, 4608
VMEM_LIMIT = 56 * 2 ** 20

ADAM_LR, ADAM_B1, ADAM_B2, ADAM_EPS, ADAM_WD, ADAM_STEP = 0.001, 0.9, 0.999, 1e-08, 0.01, 10

SDS = jax.ShapeDtypeStruct
MESHID = pl.DeviceIdType.MESH


def _cp(*sem):
    return pltpu.CompilerParams(dimension_semantics=sem, vmem_limit_bytes=VMEM_LIMIT)


def _tile(n, cap, unit=128):
    if n <= cap:
        return n
    best = None
    for t in range(unit, cap + 1, unit):
        if n % t == 0:
            best = t
    assert best is not None, (n, cap)
    return best


def _sigmoid(x):
    return 1.0 / (1.0 + jnp.exp(-x))


def _silu(x):
    return x * _sigmoid(x)


def _dsilu(x):
    s = _sigmoid(x)
    return s * (1.0 + x * (1.0 - s))


def _softplus(x):
    return jnp.maximum(x, 0.0) + jnp.log(1.0 + jnp.exp(-jnp.abs(x)))


def _split_bf16(x, passes):
    parts, r = [], x.astype(F32)
    for _ in range(passes):
        p = r.astype(BF16)
        parts.append(p)
        r = r - p.astype(F32)
    return parts


def _dot(a, b, dims=None, exact=None, passes=2):
    dn = {None: (((1,), (0,)), ((), ())), "nt": (((1,), (1,)), ((), ())), "tn": (((0,), (0,)), ((), ()))}[dims]
    if exact is None:
        return lax.dot_general(a.astype(_MXU), b.astype(_MXU), dn, preferred_element_type=F32)
    if exact == "a":
        terms = [(a.astype(BF16), p) for p in _split_bf16(b, passes)]
    else:
        terms = [(p, b.astype(BF16)) for p in _split_bf16(a, passes)]
    out = None
    for lhs, rhs in terms:
        d = lax.dot_general(lhs, rhs, dn, preferred_element_type=F32)
        out = d if out is None else out + d
    return out


def _mm_nn(a, b, name, add=None, out_dtype=F32, tm_cap=1024, tn_cap=1408):
    M, K = a.shape
    _, N = b.shape
    tm, tn = _tile(M, tm_cap, 8), _tile(N, tn_cap)

    def body(a_ref, b_ref, *rest):
        o_ref = rest[-1]
        acc = _dot(a_ref[...], b_ref[...])
        if add is not None:
            acc = acc + rest[0][...]
        o_ref[...] = acc.astype(o_ref.dtype)

    in_specs = [pl.BlockSpec((tm, K), lambda j, i: (i, 0)), pl.BlockSpec((K, tn), lambda j, i: (0, j))]
    args = [a, b]
    if add is not None:
        in_specs.append(pl.BlockSpec((tm, tn), lambda j, i: (i, j)))
        args.append(add)
    return pl.pallas_call(
        body, name=name, grid=(N // tn, M // tm), in_specs=in_specs,
        out_specs=pl.BlockSpec((tm, tn), lambda j, i: (i, j)), out_shape=SDS((M, N), out_dtype),
        compiler_params=_cp("parallel", "parallel"))(*args)


def _mm_nt(a, b, name, add=None, out_dtype=F32, tm_cap=512, tk_cap=1024, b_col=0):
    M, N = a.shape
    K = b.shape[0]
    tm, tk = _tile(M, tm_cap, 8), _tile(K, tk_cap)

    def body(a_ref, b_ref, *rest):
        o_ref = rest[-1]
        acc = _dot(a_ref[...], b_ref[...], "nt")
        if add is not None:
            acc = acc + rest[0][...]
        o_ref[...] = acc.astype(o_ref.dtype)

    in_specs = [pl.BlockSpec((tm, N), lambda j, i: (i, 0)), pl.BlockSpec((tk, N), lambda j, i: (j, b_col))]
    args = [a, b]
    if add is not None:
        in_specs.append(pl.BlockSpec((tm, tk), lambda j, i: (i, j)))
        args.append(add)
    return pl.pallas_call(
        body, name=name, grid=(K // tk, M // tm), in_specs=in_specs,
        out_specs=pl.BlockSpec((tm, tk), lambda j, i: (i, j)), out_shape=SDS((M, K), out_dtype),
        compiler_params=_cp("parallel", "parallel"))(*args)


def _mm_tn(a, b, name, tm_cap=1024, tk_cap=512, tn_cap=1408):
    M, K = a.shape
    _, N = b.shape
    tm, tk, tn = _tile(M, tm_cap, 8), _tile(K, tk_cap), _tile(N, tn_cap)

    def body(a_ref, b_ref, o_ref):
        @pl.when(pl.program_id(2) == 0)
        def _():
            o_ref[...] = jnp.zeros_like(o_ref)

        o_ref[...] += _dot(a_ref[...], b_ref[...], "tn")

    return pl.pallas_call(
        body, name=name, grid=(K // tk, N // tn, M // tm),
        in_specs=[pl.BlockSpec((tm, tk), lambda k, n, m: (m, k)), pl.BlockSpec((tm, tn), lambda k, n, m: (m, n))],
        out_specs=pl.BlockSpec((tk, tn), lambda k, n, m: (k, n)), out_shape=SDS((K, N), F32),
        compiler_params=_cp("parallel", "parallel", "arbitrary"))(a, b)


def _rms_fwd(x, g, name, tb_cap=512):
    S, Dm = x.shape
    tb = _tile(S, tb_cap, 8)

    def body(x_ref, g_ref, o_ref):
        xv = x_ref[...]
        r = lax.rsqrt(jnp.mean(xv * xv, axis=-1, keepdims=True) + EPS)
        o_ref[...] = (xv * r * g_ref[...]).astype(o_ref.dtype)

    return pl.pallas_call(
        body, name=name, grid=(S // tb,),
        in_specs=[pl.BlockSpec((tb, Dm), lambda i: (i, 0)), pl.BlockSpec((1, Dm), lambda i: (0, 0))],
        out_specs=pl.BlockSpec((tb, Dm), lambda i: (i, 0)), out_shape=SDS((S, Dm), _MXU),
        compiler_params=_cp("parallel"))(x, g)


def _rms_bwd(x, g, dh, dres, name, tb_cap=512, low=True):
    S, Dm = x.shape
    tb = _tile(S, tb_cap, 8)
    need_dx = dres is not None

    def body(x_ref, g_ref, dh_ref, *rest):
        dg_ref = rest[-1]
        xv = x_ref[...]
        r = lax.rsqrt(jnp.mean(xv * xv, axis=-1, keepdims=True) + EPS)
        xh = xv * r
        dy = dh_ref[...].astype(F32)

        @pl.when(pl.program_id(0) == 0)
        def _():
            dg_ref[...] = jnp.zeros_like(dg_ref)

        dg_ref[...] += jnp.sum(dy * xh, axis=0, keepdims=True)
        if need_dx:
            gdy = dy * g_ref[...]
            dx = r * (gdy - xh * jnp.mean(xh * gdy, axis=-1, keepdims=True))
            tot = rest[0][...] + dx
            rest[1][...] = tot
            if low:
                rest[2][...] = tot.astype(rest[2].dtype)

    row = pl.BlockSpec((tb, Dm), lambda i: (i, 0))
    vec = pl.BlockSpec((1, Dm), lambda i: (0, 0))
    if need_dx:
        outs = [SDS((S, Dm), F32)] + ([SDS((S, Dm), _MXU)] if low else [])
        return pl.pallas_call(
            body, name=name, grid=(S // tb,), in_specs=[row, vec, row, row], out_specs=[row] * len(outs) + [vec],
            out_shape=outs + [SDS((1, Dm), F32)], compiler_params=_cp("arbitrary"))(x, g, dh, dres)
    return pl.pallas_call(
        body, name=name, grid=(S // tb,), in_specs=[row, vec, row], out_specs=vec,
        out_shape=SDS((1, Dm), F32), compiler_params=_cp("arbitrary"))(x, g, dh)


def _final_loss(x, g, tgt, name="final_loss", tb_cap=512):
    S, Dm = x.shape
    tb = _tile(S, tb_cap, 8)

    def body(x_ref, g_ref, t_ref, loss_ref, dx_ref, dxl_ref, dg_ref):
        xv = x_ref[...]
        gv = g_ref[...]
        r = lax.rsqrt(jnp.mean(xv * xv, axis=-1, keepdims=True) + EPS)
        xh = xv * r
        e = xh * gv - t_ref[...]

        @pl.when(pl.program_id(0) == 0)
        def _():
            loss_ref[...] = jnp.zeros_like(loss_ref)
            dg_ref[...] = jnp.zeros_like(dg_ref)

        loss_ref[...] += 0.5 * jnp.sum(jnp.mean(e * e, axis=-1, keepdims=True))
        dy = e * (1.0 / Dm)
        dg_ref[...] += jnp.sum(dy * xh, axis=0, keepdims=True)
        gdy = dy * gv
        dx = r * (gdy - xh * jnp.mean(xh * gdy, axis=-1, keepdims=True))
        dx_ref[...] = dx
        dxl_ref[...] = dx.astype(dxl_ref.dtype)

    row = pl.BlockSpec((tb, Dm), lambda i: (i, 0))
    vec = pl.BlockSpec((1, Dm), lambda i: (0, 0))
    return pl.pallas_call(
        body, name=name, grid=(S // tb,), in_specs=[row, vec, row],
        out_specs=[pl.BlockSpec((1, 128), lambda i: (0, 0)), row, row, vec],
        out_shape=[SDS((1, 128), F32), SDS((S, Dm), F32), SDS((S, Dm), _MXU), SDS((1, Dm), F32)],
        compiler_params=_cp("arbitrary"))(x, g, tgt)


SSD_HALO = 8
CF_HALO = 32
CONV_CB = 512


def _ssd_conv_fwd(proj, w, b, name="ssd_conv_fwd", tb_cap=512):
    S = proj.shape[0]
    tb = _tile(S, tb_cap, 8)
    nb = S // tb
    c0 = COL_XBC // CONV_CB

    def body(x_ref, w_ref, b_ref, o_ref, ext):
        @pl.when(pl.program_id(1) == 0)
        def _():
            ext[pl.ds(0, SSD_HALO), :] = jnp.zeros((SSD_HALO, CONV_CB), F32)

        ext[pl.ds(SSD_HALO, tb), :] = x_ref[...]
        acc = jnp.zeros((tb, CONV_CB), F32) + b_ref[...]
        for k in range(KS):
            acc = acc + ext[pl.ds(SSD_HALO - (KS - 1) + k, tb), :] * w_ref[k:k + 1, :]
        o_ref[...] = acc
        ext[pl.ds(0, SSD_HALO), :] = ext[pl.ds(tb, SSD_HALO), :]

    return pl.pallas_call(
        body, name=name, grid=(XBC // CONV_CB, nb),
        in_specs=[pl.BlockSpec((tb, CONV_CB), lambda j, i: (i, c0 + j)),
                  pl.BlockSpec((KS, CONV_CB), lambda j, i: (0, j)),
                  pl.BlockSpec((1, CONV_CB), lambda j, i: (0, j))],
        out_specs=pl.BlockSpec((tb, CONV_CB), lambda j, i: (i, j)), out_shape=SDS((S, XBC), F32),
        scratch_shapes=[pltpu.VMEM((SSD_HALO + tb, CONV_CB), F32)],
        compiler_params=_cp("parallel", "arbitrary"))(proj, w, b)


def _ssd_conv_bwd(dxbc, proj, w, name="ssd_conv_bwd", tb_cap=512):
    S = proj.shape[0]
    tb = _tile(S, tb_cap, 8)
    nb = S // tb
    c0 = COL_XBC // CONV_CB

    def body(dy_ref, x_ref, w_ref, dx_ref, dw_ref, db_ref, ext):
        @pl.when(pl.program_id(1) == 0)
        def _():
            ext[pl.ds(tb, SSD_HALO), :] = jnp.zeros((SSD_HALO, CONV_CB), F32)
            dw_ref[...] = jnp.zeros_like(dw_ref)
            db_ref[...] = jnp.zeros_like(db_ref)

        dy = dy_ref[...]
        ext[pl.ds(0, tb), :] = dy
        xv = x_ref[...]
        acc = jnp.zeros((tb, CONV_CB), F32)
        for k in range(KS):
            sh = ext[pl.ds(KS - 1 - k, tb), :]
            acc = acc + sh * w_ref[k:k + 1, :]
            dw_ref[k:k + 1, :] += jnp.sum(xv * sh, axis=0, keepdims=True)
        db_ref[...] += jnp.sum(dy, axis=0, keepdims=True)
        dx_ref[...] = acc.astype(dx_ref.dtype)
        ext[pl.ds(tb, SSD_HALO), :] = ext[pl.ds(0, SSD_HALO), :]

    return pl.pallas_call(
        body, name=name, grid=(XBC // CONV_CB, nb),
        in_specs=[pl.BlockSpec((tb, CONV_CB), lambda j, i: (nb - 1 - i, j)),
                  pl.BlockSpec((tb, CONV_CB), lambda j, i: (nb - 1 - i, c0 + j)),
                  pl.BlockSpec((KS, CONV_CB), lambda j, i: (0, j))],
        out_specs=[pl.BlockSpec((tb, CONV_CB), lambda j, i: (nb - 1 - i, j)),
                   pl.BlockSpec((KS, CONV_CB), lambda j, i: (0, j)),
                   pl.BlockSpec((1, CONV_CB), lambda j, i: (0, j))],
        out_shape=[SDS((S, XBC), _MXU), SDS((KS, XBC), F32), SDS((1, XBC), F32)],
        scratch_shapes=[pltpu.VMEM((tb + SSD_HALO, CONV_CB), F32)],
        compiler_params=_cp("parallel", "arbitrary"))(dxbc, proj, w)


HBM_SPEC = pl.BlockSpec(memory_space=pl.ANY)


def _chip_peers(x, y):
    return [(1 - x, y), (x, 1 - y), (1 - x, 1 - y)]


def _remote(src, dst, send_sem, recv_sem, dev):
    return pltpu.make_async_remote_copy(src_ref=src, dst_ref=dst, send_sem=send_sem, recv_sem=recv_sem,
                                        device_id=dev, device_id_type=MESHID)


def _scatter_copies(srcs, outs, send_sems, recv_sems):
    x, y, c = lax.axis_index("x"), lax.axis_index("y"), lax.axis_index("c")
    me = 2 * x + y
    sends, recvs = [], []
    for i, (s, o) in enumerate(zip(srcs, outs)):
        for k, (px, py) in enumerate(_chip_peers(x, y)):
            j = 3 * i + k
            sends.append(_remote(s.at[2 * px + py], o.at[me], send_sems.at[j], recv_sems.at[j], (px, py, c)))
            recvs.append(_remote(s.at[me], o.at[2 * px + py], send_sems.at[j], recv_sems.at[j], (px, py, c)))
    return sends, recvs


class _RidingExchange:
    def __init__(self, srcs, outs, sems, steps):
        self.srcs, self.outs, self.sems, self.steps = srcs, outs, sems, steps

    @staticmethod
    def scratch(n):
        return [pltpu.SemaphoreType.DMA((3 * n,)), pltpu.SemaphoreType.DMA((3 * n,))] if n else []

    def start(self):
        if self.srcs:
            @pl.when(pl.program_id(0) == 0)
            def _():
                for cp in _scatter_copies(self.srcs, self.outs, *self.sems)[0]:
                    cp.start()

    def finish(self):
        if self.srcs:
            @pl.when(pl.program_id(0) == self.steps - 1)
            def _():
                sends, recvs = _scatter_copies(self.srcs, self.outs, *self.sems)
                for cp in recvs:
                    cp.wait_recv()
                for cp in sends:
                    cp.wait_send()


def _head_consts():
    e = (lax.broadcasted_iota(jnp.int32, (128, NH * HP), 1) // HP == lax.broadcasted_iota(jnp.int32, (128, NH * HP), 0)).astype(F32)
    et = (lax.broadcasted_iota(jnp.int32, (NH * HP, 128), 0) // HP == lax.broadcasted_iota(jnp.int32, (NH * HP, 128), 1)).astype(F32)
    r = lax.broadcasted_iota(jnp.int32, (CH, CH), 0)
    c = lax.broadcasted_iota(jnp.int32, (CH, CH), 1)
    return e, et, (c <= r), (r <= c)


def _ssd_common(xbc_c, dtr, dtb, alog, e, tril, triu):
    xbc = _silu(xbc_c)
    xs = xbc[:, :NH * HP]
    dt = _softplus(dtr + dtb)
    A = -jnp.exp(alog)
    a = dt * A
    cs = _dot(tril, a, exact="a", passes=3)
    csT = _dot(a, triu, "tn", exact="b", passes=3)
    csL = cs[CH - 1:CH, :]
    wdec = jnp.exp(csL - cs) * dt
    dtE = _dot(dt, e, exact="b")
    ecsE = _dot(jnp.exp(cs), e, exact="b")
    wE = _dot(wdec, e, exact="b")
    eL = jnp.exp(csL)
    return xbc, xs, dt, A, cs, csT, csL, wdec, dtE, ecsE, wE, eL


def _ssd_fwd(proj, xbc_c, dtr, sc, norm_g, name="ssd_fwd"):
    S = proj.shape[0]
    nc = S // CH

    def body(z_ref, x_ref, dtr_ref, sc_ref, ng_ref, y_ref, yn_ref, hp_ref, hst):
        @pl.when(pl.program_id(0) == 0)
        def _():
            hst[...] = jnp.zeros_like(hst)

        e, et, tril, triu = _head_consts()
        xbc, xs, dt, A, cs, csT, csL, wdec, dtE, ecsE, wE, eL = _ssd_common(
            x_ref[...], dtr_ref[...], sc_ref[0:1, :], sc_ref[1:2, :], e, tril, triu)
        hp_ref[0] = hst[...]
        xd = xs * dtE
        xw = xs * wE
        dE = _dot(jnp.broadcast_to(sc_ref[2:3, :], (8, 128)), e, exact="b", passes=3)[0:1, :]
        eLcol = jnp.sum(et * eL, axis=1, keepdims=True)
        for g in range(NG):
            Bg = xbc[:, NH * HP + g * NS: NH * HP + (g + 1) * NS]
            Cg = xbc[:, NH * HP + NG * NS + g * NS: NH * HP + NG * NS + (g + 1) * NS]
            gs = slice(g * GW, (g + 1) * GW)
            G = _dot(Cg, Bg, "nt")
            hg = hst[gs, :]
            yoff = ecsE[:, gs] * _dot(Cg, hg, "nt")
            hst[gs, :] = eLcol[gs, :] * hg + _dot(xw[:, gs], Bg, "tn")
            for hh in range(NH // NG):
                h = g * (NH // NG) + hh
                hs = slice(h * HP, (h + 1) * HP)
                m = jnp.where(tril, jnp.exp(jnp.where(tril, cs[:, h:h + 1] - csT[h:h + 1, :], 0.0)), 0.0)
                yd = _dot(G * m, xd[:, hs])
                y_ref[:, hs] = yd + yoff[:, hh * HP:(hh + 1) * HP] + dE[:, hs] * xs[:, hs]
        y = y_ref[...]
        yz = y * _silu(z_ref[...])
        for g in range(NG):
            gs = slice(g * GW, (g + 1) * GW)
            yg = yz[:, gs]
            r = lax.rsqrt(jnp.mean(yg * yg, axis=-1, keepdims=True) + EPS)
            yn_ref[:, gs] = (yg * r * ng_ref[:, gs]).astype(yn_ref.dtype)

    return pl.pallas_call(
        body, name=name, grid=(nc,),
        in_specs=[pl.BlockSpec((CH, D), lambda c: (c, COL_Z // D)),
                  pl.BlockSpec((CH, XBC), lambda c: (c, 0)),
                  pl.BlockSpec((CH, 128), lambda c: (c, 0)),
                  pl.BlockSpec((8, 128), lambda c: (0, 0)),
                  pl.BlockSpec((1, D), lambda c: (0, 0))],
        out_specs=[pl.BlockSpec((CH, D), lambda c: (c, 0)), pl.BlockSpec((CH, D), lambda c: (c, 0)),
                   pl.BlockSpec((1, NH * HP, NS), lambda c: (c, 0, 0))],
        out_shape=[SDS((S, D), F32), SDS((S, D), _MXU), SDS((nc, NH * HP, NS), F32)],
        scratch_shapes=[pltpu.VMEM((NH * HP, NS), F32)],
        compiler_params=_cp("arbitrary"))(proj, xbc_c, dtr, sc, norm_g)


def _ssd_bwd(dmix, y, proj, xbc_c, dtr, hprev, sc, norm_g, comm=(), name="ssd_bwd"):
    S = proj.shape[0]
    nc = S // CH
    nco = len(comm)
    rev = lambda c: nc - 1 - c

    def body(*refs):
        dyn_ref, y_ref, z_ref, x_ref, dtr_ref, hp_ref, sc_ref, ng_ref = refs[:8]
        dz_ref, dx_ref, ddtr_ref, gsc_ref, gng_ref = refs[8 + nco:13 + nco]
        dh, dxd = refs[13 + 2 * nco:15 + 2 * nco]
        exchange = _RidingExchange(refs[8:8 + nco], refs[13 + nco:13 + 2 * nco], refs[15 + 2 * nco:], nc)
        exchange.start()

        @pl.when(pl.program_id(0) == 0)
        def _():
            dh[...] = jnp.zeros_like(dh)
            gsc_ref[...] = jnp.zeros_like(gsc_ref)
            gng_ref[...] = jnp.zeros_like(gng_ref)

        e, et, tril, triu = _head_consts()
        xbc_c = x_ref[...]
        dtr = dtr_ref[...]
        dtb = sc_ref[0:1, :]
        xbc, xs, dt, A, cs, csT, csL, wdec, dtE, ecsE, wE, eL = _ssd_common(
            xbc_c, dtr, dtb, sc_ref[1:2, :], e, tril, triu)
        xd = xs * dtE
        xw = xs * wE
        dE = _dot(jnp.broadcast_to(sc_ref[2:3, :], (8, 128)), e, exact="b", passes=3)[0:1, :]
        eLcol = jnp.sum(et * eL, axis=1, keepdims=True)

        yv = y_ref[...]
        zv = z_ref[...]
        sz = _silu(zv)
        yz = yv * sz
        dyn = dyn_ref[...]
        dyz_parts = []
        for g in range(NG):
            gs = slice(g * GW, (g + 1) * GW)
            yg = yz[:, gs]
            r = lax.rsqrt(jnp.mean(yg * yg, axis=-1, keepdims=True) + EPS)
            yh = yg * r
            dn = dyn[:, gs]
            gng_ref[:, gs] += jnp.sum(dn * yh, axis=0, keepdims=True)
            gdn = dn * ng_ref[:, gs]
            dyz_parts.append(r * (gdn - yh * jnp.mean(yh * gdn, axis=-1, keepdims=True)))
        dyz = jnp.concatenate(dyz_parts, axis=1)
        dy = dyz * sz
        dz_ref[...] = (dyz * yv * _dsilu(zv)).astype(dz_ref.dtype)

        gsc_ref[2:3, :] += jnp.sum(_dot(dy * xs, et, exact="b", passes=3), axis=0, keepdims=True)
        dxs = dE * dy
        dzo = ecsE * dy
        dcs = jnp.zeros((CH, 128), F32)
        dcsL = jnp.zeros((1, 128), F32)
        ddt = jnp.zeros((CH, 128), F32)
        dB_parts, dC_parts, yoff_parts, dxw_parts = [], [], [], []
        for g in range(NG):
            Bg = xbc[:, NH * HP + g * NS: NH * HP + (g + 1) * NS]
            Cg = xbc[:, NH * HP + NG * NS + g * NS: NH * HP + NG * NS + (g + 1) * NS]
            gs = slice(g * GW, (g + 1) * GW)
            hg = hp_ref[0, gs, :]
            dhn = dh[gs, :]
            G = _dot(Cg, Bg, "nt")
            yoff_parts.append(ecsE[:, gs] * _dot(Cg, hg, "nt"))
            dC = _dot(dzo[:, gs], hg)
            dhp = _dot(dzo[:, gs], Cg, "tn") + eLcol[gs, :] * dhn
            t1 = jnp.sum(dhn * hg, axis=1, keepdims=True) * eLcol[gs, :]
            dcsL = dcsL + jnp.sum(et[gs, :] * t1, axis=0, keepdims=True)
            dxw_parts.append(_dot(Bg, dhn, "nt"))
            dB = _dot(xw[:, gs], dhn)
            dgsum = jnp.zeros((CH, CH), F32)
            for hh in range(NH // NG):
                h = g * (NH // NG) + hh
                hs = slice(h * HP, (h + 1) * HP)
                m = jnp.where(tril, jnp.exp(jnp.where(tril, cs[:, h:h + 1] - csT[h:h + 1, :], 0.0)), 0.0)
                sc = G * m
                dyh = dy[:, hs]
                dxd[:, hs] = _dot(sc, dyh, "tn")
                dsc = _dot(dyh, xd[:, hs], "nt")
                q = dsc * sc
                oh = (lax.broadcasted_iota(jnp.int32, (CH, 128), 1) == h).astype(F32)
                dcs = dcs + _dot(q, oh, exact="b") - _dot(q, oh, "tn", exact="b")
                dgsum = dgsum + dsc * m
            dC_parts.append(dC + _dot(dgsum, Bg))
            dB_parts.append(dB + _dot(dgsum, Cg, "tn"))
            dh[gs, :] = dhp
        yoff = jnp.concatenate(yoff_parts, axis=1)
        dxw = jnp.concatenate(dxw_parts, axis=1)
        dxdv = dxd[...]
        dcs = dcs + _dot(dy * yoff, et, exact="b")
        dxs = dxs + wE * dxw + dtE * dxdv
        dw = _dot(dxw * xs, et, exact="b")
        ddt = ddt + dw * jnp.exp(csL - cs) + _dot(dxdv * xs, et, exact="b")
        dcs = dcs - dw * wdec
        dcsL = dcsL + jnp.sum(dw * wdec, axis=0, keepdims=True)
        last = lax.broadcasted_iota(jnp.int32, (CH, 128), 0) == CH - 1
        dcs = dcs + jnp.where(last, dcsL, 0.0)
        da = _dot(triu, dcs, exact="a", passes=3)
        ddt = ddt + da * A
        gsc_ref[1:2, :] += jnp.sum(da * dt, axis=0, keepdims=True) * A
        valid = lax.broadcasted_iota(jnp.int32, (CH, 128), 1) < NH
        ddtr = jnp.where(valid, ddt * _sigmoid(dtr + dtb), 0.0)
        gsc_ref[0:1, :] += jnp.sum(ddtr, axis=0, keepdims=True)
        ddtr_ref[...] = ddtr.astype(ddtr_ref.dtype)
        dxbc = jnp.concatenate([dxs] + dB_parts + dC_parts, axis=1)
        dx_ref[...] = dxbc * _dsilu(xbc_c)
        exchange.finish()

    vec = pl.BlockSpec((8, 128), lambda c: (0, 0))
    vecd = pl.BlockSpec((1, D), lambda c: (0, 0))
    row = lambda w, j=0: pl.BlockSpec((CH, w), lambda c: (rev(c), j))
    outs = pl.pallas_call(
        body, name=name, grid=(nc,),
        in_specs=[row(D), row(D), row(D, COL_Z // D), row(XBC), row(128),
                  pl.BlockSpec((1, NH * HP, NS), lambda c: (rev(c), 0, 0)), vec, vecd] + [HBM_SPEC] * nco,
        out_specs=[row(D), row(XBC), row(128), vec, vecd] + [HBM_SPEC] * nco,
        out_shape=[SDS((S, D), _MXU), SDS((S, XBC), F32), SDS((S, 128), _MXU), SDS((8, 128), F32), SDS((1, D), F32)]
        + [SDS(p.shape, p.dtype) for p in comm],
        scratch_shapes=[pltpu.VMEM((NH * HP, NS), F32), pltpu.VMEM((CH, NH * HP), F32)] + _RidingExchange.scratch(nco),
        compiler_params=_cp("arbitrary"))(dmix, y, proj, xbc_c, dtr, hprev, sc, norm_g, *comm)
    return outs[:5], outs[5:]


CONV_RT = 32


def _fill_phases(ext, ph, rows):
    for s in range(1, 8):
        ph[s - 1, pl.ds(0, rows), :] = ext[pl.ds(s, rows), :]


def _window(ext, ph, off, r0, ls):
    s = off % 8
    src = ext if s == 0 else ph.at[s - 1]
    return src[pl.ds(pl.multiple_of(off - s + r0, 8), CONV_RT), ls]


def _cf_fwd(proj, w, b, lg, lb, name="cf_fwd", tb_cap=256):
    S = proj.shape[0]
    tb = _tile(S, tb_cap, 8)

    def body(a_ref, g_ref, w_ref, b_ref, lg_ref, lb_ref, u1_ref, u_ref, ext, ph):
        @pl.when(pl.program_id(0) == 0)
        def _():
            ext[pl.ds(0, CF_HALO), :] = jnp.zeros((CF_HALO, D), F32)

        ext[pl.ds(CF_HALO, tb), :] = a_ref[...] * _sigmoid(g_ref[...])
        _fill_phases(ext, ph, tb + CF_HALO - 8)

        def tile(i, carry):
            r0 = pl.multiple_of(i * CONV_RT, CONV_RT)
            for l in range(D // 128):
                ls = pl.ds(l * 128, 128)
                acc = jnp.broadcast_to(b_ref[:, ls], (CONV_RT, 128))
                for k in range(KC):
                    acc = acc + _window(ext, ph, CF_HALO - (KC - 1) + k, r0, ls) * w_ref[k:k + 1, ls]
                u1_ref[pl.ds(r0, CONV_RT), ls] = acc
            return carry

        lax.fori_loop(0, tb // CONV_RT, tile, 0)
        acc = u1_ref[...]
        mu = jnp.mean(acc, axis=-1, keepdims=True)
        xc = acc - mu
        r = lax.rsqrt(jnp.mean(xc * xc, axis=-1, keepdims=True) + EPS)
        u_ref[...] = _silu(xc * r * lg_ref[...] + lb_ref[...]).astype(u_ref.dtype)
        ext[pl.ds(0, CF_HALO), :] = ext[pl.ds(tb, CF_HALO), :]

    vec = pl.BlockSpec((1, D), lambda i: (0, 0))
    return pl.pallas_call(
        body, name=name, grid=(S // tb,),
        in_specs=[pl.BlockSpec((tb, D), lambda i: (i, COL_A // D)), pl.BlockSpec((tb, D), lambda i: (i, COL_G // D)),
                  pl.BlockSpec((KC, D), lambda i: (0, 0)), vec, vec, vec],
        out_specs=[pl.BlockSpec((tb, D), lambda i: (i, 0)), pl.BlockSpec((tb, D), lambda i: (i, 0))],
        out_shape=[SDS((S, D), F32), SDS((S, D), _MXU)],
        scratch_shapes=[pltpu.VMEM((CF_HALO + tb, D), F32), pltpu.VMEM((7, tb + CF_HALO - 8, D), F32)],
        compiler_params=_cp("arbitrary"))(proj, proj, w, b, lg, lb)


def _cf_bwd(dmix, u1, proj, w, lg, lb, comm=(), name="cf_bwd", tb_cap=256):
    S = proj.shape[0]
    tb = _tile(S, tb_cap, 8)
    nb = S // tb
    nco = len(comm)
    rev = lambda i: nb - 1 - i

    def body(*refs):
        du_ref, u1_ref, a_ref, g_ref, w_ref, lg_ref, lb_ref = refs[:7]
        da_ref, dg_ref, dw_ref, db_ref, dlg_ref, dlb_ref = refs[7 + nco:13 + nco]
        ext, ph, u0s = refs[13 + 2 * nco:16 + 2 * nco]
        exchange = _RidingExchange(refs[7:7 + nco], refs[13 + nco:13 + 2 * nco], refs[16 + 2 * nco:], nb)
        exchange.start()

        @pl.when(pl.program_id(0) == 0)
        def _():
            ext[pl.ds(tb, CF_HALO), :] = jnp.zeros((CF_HALO, D), F32)
            dw_ref[...] = jnp.zeros_like(dw_ref)
            db_ref[...] = jnp.zeros_like(db_ref)
            dlg_ref[...] = jnp.zeros_like(dlg_ref)
            dlb_ref[...] = jnp.zeros_like(dlb_ref)

        u1 = u1_ref[...]
        mu = jnp.mean(u1, axis=-1, keepdims=True)
        xc = u1 - mu
        r = lax.rsqrt(jnp.mean(xc * xc, axis=-1, keepdims=True) + EPS)
        xh = xc * r
        lgv = lg_ref[...]
        du2 = du_ref[...] * _dsilu(xh * lgv + lb_ref[...])
        dlg_ref[...] += jnp.sum(du2 * xh, axis=0, keepdims=True)
        dlb_ref[...] += jnp.sum(du2, axis=0, keepdims=True)
        gd = du2 * lgv
        du1 = r * (gd - jnp.mean(gd, axis=-1, keepdims=True) - xh * jnp.mean(gd * xh, axis=-1, keepdims=True))
        db_ref[...] += jnp.sum(du1, axis=0, keepdims=True)
        ext[pl.ds(0, tb), :] = du1
        u0s[...] = a_ref[...] * _sigmoid(g_ref[...])
        _fill_phases(ext, ph, tb + CF_HALO - 8)

        def dx_tile(i, carry):
            r0 = pl.multiple_of(i * CONV_RT, CONV_RT)
            rows = pl.ds(r0, CONV_RT)
            for l in range(D // 128):
                ls = pl.ds(l * 128, 128)
                acc = jnp.zeros((CONV_RT, 128), F32)
                for k in range(KC):
                    acc = acc + _window(ext, ph, KC - 1 - k, r0, ls) * w_ref[k:k + 1, ls]
                sg = _sigmoid(g_ref[rows, ls])
                da_ref[rows, ls] = (acc * sg).astype(da_ref.dtype)
                dg_ref[rows, ls] = (acc * a_ref[rows, ls] * sg * (1.0 - sg)).astype(dg_ref.dtype)
            return carry

        lax.fori_loop(0, tb // CONV_RT, dx_tile, 0)
        for l in range(D // 128):
            ls = pl.ds(l * 128, 128)

            def dw_tile(i, accs, ls=ls):
                r0 = pl.multiple_of(i * CONV_RT, CONV_RT)
                u0t = u0s[pl.ds(r0, CONV_RT), ls]
                out = []
                for k in range(KC):
                    p = u0t * _window(ext, ph, KC - 1 - k, r0, ls)
                    out.append(accs[k] + ((p[0:8] + p[8:16]) + (p[16:24] + p[24:32])))
                return tuple(out)

            accs = lax.fori_loop(0, tb // CONV_RT, dw_tile, tuple(jnp.zeros((8, 128), F32) for _ in range(KC)))
            for k in range(KC):
                dw_ref[k:k + 1, ls] += jnp.sum(accs[k], axis=0, keepdims=True)
        ext[pl.ds(tb, CF_HALO), :] = ext[pl.ds(0, CF_HALO), :]
        exchange.finish()

    vec = pl.BlockSpec((1, D), lambda i: (0, 0))
    wsp = pl.BlockSpec((KC, D), lambda i: (0, 0))
    row = lambda j=0: pl.BlockSpec((tb, D), lambda i: (rev(i), j))
    outs = pl.pallas_call(
        body, name=name, grid=(nb,),
        in_specs=[row(1), row(), row(COL_A // D), row(COL_G // D), wsp, vec, vec] + [HBM_SPEC] * nco,
        out_specs=[row(), row(), wsp, vec, vec, vec] + [HBM_SPEC] * nco,
        out_shape=[SDS((S, D), _MXU), SDS((S, D), _MXU), SDS((KC, D), F32),
                   SDS((1, D), F32), SDS((1, D), F32), SDS((1, D), F32)] + [SDS(p.shape, p.dtype) for p in comm],
        scratch_shapes=[pltpu.VMEM((tb + CF_HALO, D), F32), pltpu.VMEM((7, tb + CF_HALO - 8, D), F32),
                        pltpu.VMEM((tb, D), F32)] + _RidingExchange.scratch(nco),
        compiler_params=_cp("arbitrary"))(dmix, u1, proj, proj, w, lg, lb, *comm)
    return outs[:6], outs[6:]


def _attn_fwd(q, kv, name="attn_fwd", tq_cap=512):
    S = q.shape[0]
    tq = _tile(S, tq_cap, 8)
    scale = XD ** -0.5

    def body(q_ref, kv_ref, o_ref):
        for h in range(XH):
            hs = slice(h * XD, (h + 1) * XD)
            s = _dot(q_ref[:, hs], kv_ref[:, hs], "nt") * scale
            s = s - jnp.max(s, axis=-1, keepdims=True)
            p = jnp.exp(s)
            p = p / jnp.sum(p, axis=-1, keepdims=True)
            o_ref[:, hs] = _dot(p, kv_ref[:, D + h * XD: D + (h + 1) * XD]).astype(o_ref.dtype)

    return pl.pallas_call(
        body, name=name, grid=(S // tq,),
        in_specs=[pl.BlockSpec((tq, D), lambda i: (i, 0)), pl.BlockSpec((MEM, 2 * D), lambda i: (0, 0))],
        out_specs=pl.BlockSpec((tq, D), lambda i: (i, 0)), out_shape=SDS((S, D), _MXU),
        compiler_params=_cp("parallel"))(q, kv)


def _attn_bwd(do, q, kv, name="attn_bwd", tq_cap=512):
    S = q.shape[0]
    tq = _tile(S, tq_cap, 8)
    scale = XD ** -0.5

    def body(do_ref, q_ref, kv_ref, dq_ref, dkv_ref):
        @pl.when(pl.program_id(0) == 0)
        def _():
            dkv_ref[...] = jnp.zeros_like(dkv_ref)

        for h in range(XH):
            hs = slice(h * XD, (h + 1) * XD)
            vs = slice(D + h * XD, D + (h + 1) * XD)
            qh = q_ref[:, hs]
            kh = kv_ref[:, hs]
            s = _dot(qh, kh, "nt") * scale
            s = s - jnp.max(s, axis=-1, keepdims=True)
            p = jnp.exp(s)
            p = p / jnp.sum(p, axis=-1, keepdims=True)
            doh = do_ref[:, hs]
            dp = _dot(doh, kv_ref[:, vs], "nt")
            ds = p * (dp - jnp.sum(dp * p, axis=-1, keepdims=True)) * scale
            dq_ref[:, hs] = _dot(ds, kh).astype(dq_ref.dtype)
            dkv_ref[:, hs] += _dot(ds, qh, "tn")
            dkv_ref[:, vs] += _dot(p, doh, "tn")

    return pl.pallas_call(
        body, name=name, grid=(S // tq,),
        in_specs=[pl.BlockSpec((tq, D), lambda i: (i, 0)), pl.BlockSpec((tq, D), lambda i: (i, 0)),
                  pl.BlockSpec((MEM, 2 * D), lambda i: (0, 0))],
        out_specs=[pl.BlockSpec((tq, D), lambda i: (i, 0)), pl.BlockSpec((MEM, 2 * D), lambda i: (0, 0))],
        out_shape=[SDS((S, D), _MXU), SDS((MEM, 2 * D), F32)],
        compiler_params=_cp("arbitrary"))(do, q, kv)


def _ffn_act(gu, name="ffn_act", tb_cap=128):
    S = gu.shape[0]
    tb = _tile(S, tb_cap, 8)

    def body(g_ref, u_ref, o_ref):
        o_ref[...] = (_silu(g_ref[...]) * u_ref[...]).astype(o_ref.dtype)

    return pl.pallas_call(
        body, name=name, grid=(S // tb,),
        in_specs=[pl.BlockSpec((tb, DFF), lambda i: (i, 0)), pl.BlockSpec((tb, DFF), lambda i: (i, 1))],
        out_specs=pl.BlockSpec((tb, DFF), lambda i: (i, 0)), out_shape=SDS((S, DFF), _MXU),
        compiler_params=_cp("parallel"))(gu, gu)


def _ffn_act_bwd(dact, gu, name="ffn_act_bwd", tb_cap=128):
    S = gu.shape[0]
    tb = _tile(S, tb_cap, 8)

    def body(d_ref, g_ref, u_ref, dg_ref, du_ref):
        gt = g_ref[...]
        d = d_ref[...]
        s = _sigmoid(gt)
        dg_ref[...] = (d * u_ref[...] * (s * (1.0 + gt * (1.0 - s)))).astype(dg_ref.dtype)
        du_ref[...] = (d * gt * s).astype(du_ref.dtype)

    blk = pl.BlockSpec((tb, DFF), lambda i: (i, 0))
    return pl.pallas_call(
        body, name=name, grid=(S // tb,),
        in_specs=[blk, blk, pl.BlockSpec((tb, DFF), lambda i: (i, 1))],
        out_specs=[blk, blk], out_shape=[SDS((S, DFF), _MXU), SDS((S, DFF), _MXU)],
        compiler_params=_cp("parallel"))(dact, gu, gu)


def _local_step(x, mem, tgt, W, P, core=None):
    pair, got = {}, {}

    def rs_pair(group):
        if core is None:
            return []
        ps = _rs_pair(group, GW, core)
        pair.update(zip(group, ps))
        return ps
    h = _rms_fwd(x, P["g_mix"], "rms_mix")
    proj = _mm_nn(h, W["main"], "in_proj", tn_cap=1152)
    dtr = _mm_nn(h, W["dt"], "in_proj_dt")
    xbc_c = _ssd_conv_fwd(proj, P["conv4_w"], P["conv4_b"])
    y, yn, hprev = _ssd_fwd(proj, xbc_c, dtr, P["sc"], P["ssd_norm_g"])
    u1, u = _cf_fwd(proj, P["cf_w"], P["cf_b"], P["ln_g"], P["ln_b"])
    mix = jnp.concatenate([yn, u], axis=1)
    x1 = _mm_nn(mix, W["out"], "out_proj", add=x)
    hq = _rms_fwd(x1, P["g_xattn"], "rms_xattn")
    q = _mm_nn(hq, W["q"], "q_proj")
    mn = _rms_fwd(mem, P["g_mem"], "rms_mem")
    kv = _mm_nn(mn, W["kv"], "kv_proj")
    o = _attn_fwd(q, kv)
    x2 = _mm_nn(o, W["o"], "o_proj", add=x1)
    hf = _rms_fwd(x2, P["g_ffn"], "rms_ffn")
    gu = _mm_nn(hf, W["gu"], "ffn_in")
    act = _ffn_act(gu)
    x3 = _mm_nn(act, W["down"], "ffn_out", add=x2)
    loss, dx3, dx3b, g_final = _final_loss(x3, P["g_final"], tgt)
    GW, GP = {}, {"g_final": g_final}
    dact = _mm_nt(dx3b, W["down"], "ffn_out_dx", tk_cap=1408)
    GW["down"] = _mm_tn(act, dx3b, "ffn_out_dw", tk_cap=1408, tn_cap=1024)
    dgt, dup = _ffn_act_bwd(dact, gu)
    dhf = _mm_nt(dgt, W["gu"], "ffn_gate_dx", b_col=0)
    dhf = _mm_nt(dup, W["gu"], "ffn_up_dx", b_col=1, add=dhf)
    GW["gate"] = _mm_tn(hf, dgt, "ffn_gate_dw")
    GW["up"] = _mm_tn(hf, dup, "ffn_up_dw")
    ffn_pieces = rs_pair(RS_GROUPS[0])
    dx2, dx2b, GP["g_ffn"] = _rms_bwd(x2, P["g_ffn"], dhf, dx3, "rms_ffn_bwd")
    do = _mm_nt(dx2b, W["o"], "o_proj_dx")
    GW["o"] = _mm_tn(o, dx2b, "o_proj_dw")
    dq, dkv = _attn_bwd(do, q, kv)
    dhq = _mm_nt(dq, W["q"], "q_proj_dx")
    GW["q"] = _mm_tn(hq, dq, "q_proj_dw")
    dkvb = dkv.astype(_MXU)
    GW["kv"] = _mm_tn(mn, dkvb, "kv_proj_dw", tm_cap=256)
    dmn = _mm_nt(dkvb, W["kv"], "kv_proj_dx")
    GP["g_mem"] = _rms_bwd(mem, P["g_mem"], dmn, None, "rms_mem_bwd")
    dx1, dx1b, GP["g_xattn"] = _rms_bwd(x1, P["g_xattn"], dhq, dx2, "rms_xattn_bwd")
    dmix = _mm_nt(dx1b, W["out"], "out_proj_dx")
    GW["out"] = _mm_tn(mix, dx1b, "out_proj_dw", tn_cap=1024)
    attn_pieces = rs_pair(RS_GROUPS[1])
    (da, dg, GP["cf_w"], GP["cf_b"], GP["ln_g"], GP["ln_b"]), came = _cf_bwd(
        dmix, u1, proj, P["cf_w"], P["ln_g"], P["ln_b"], comm=ffn_pieces)
    got.update(zip(RS_GROUPS[0], came))
    (dz, dxbc_c, ddtr, GP["sc"], GP["ssd_norm_g"]), came = _ssd_bwd(
        dmix, y, proj, xbc_c, dtr, hprev, P["sc"], P["ssd_norm_g"], comm=attn_pieces)
    got.update(zip(RS_GROUPS[1], came))
    dxbc, GP["conv4_w"], GP["conv4_b"] = _ssd_conv_bwd(dxbc_c, proj, P["conv4_w"])
    dproj = jnp.concatenate([dz, da, dg, dxbc], axis=1)
    dh = _mm_nt(ddtr, W["dt"], "in_proj_dt_dx")
    dh = _mm_nt(dproj, W["main"], "in_proj_dx", add=dh, tk_cap=512)
    GW["main"] = _mm_tn(h, dproj, "in_proj_dw", tn_cap=1152)
    GW["dt"] = _mm_tn(h, ddtr, "in_proj_dt_dw")
    in_pieces = rs_pair(RS_GROUPS[2])
    if in_pieces:
        got.update(zip(RS_GROUPS[2], _scatter_list(in_pieces)))
    grad_x, GP["g_mix"] = _rms_bwd(x, P["g_mix"], dh, dx1, "rms_mix_bwd", low=False)
    if core is None:
        return loss, grad_x, GW, GP
    return loss, grad_x, GW, GP, pair, got


Z_END, XBC_END, DT_END = NH * HP, NH * HP + XBC, NH * HP + XBC + NH
NFB = DFF // FB


def _pad_to(a, rows=None, cols=None):
    r = 0 if rows is None else rows - a.shape[0]
    c = 0 if cols is None else cols - a.shape[1]
    return jnp.pad(a, ((0, r), (0, c)))


IN_W = DT_END + 2 * D
W_IN_SEGS = [(0, Z_END, "main", COL_Z), (Z_END, XBC_END, "main", COL_XBC), (XBC_END, DT_END, "dt", 0),
             (DT_END, DT_END + D, "main", COL_A), (DT_END + D, IN_W, "main", COL_G)]
BIG = [("w_in", True), ("w_out", False), ("w_q", False), ("w_kv", True), ("w_o", False), ("w_gate", True),
       ("w_up", True), ("w_down", False)]


def _ref_cols(pieces, a, b):
    cw = IN_W // 4
    out = []
    for j in range(4):
        lo, hi = max(a, j * cw), min(b, (j + 1) * cw)
        if lo < hi:
            out.append(pieces[j][:, lo - j * cw:hi - j * cw])
    return out


def _cat_cols(pieces):
    return jnp.concatenate([pieces[j] for j in range(4)], axis=1)


def _pack_weights(pc):
    w_in = pc["w_in"]
    main = jnp.concatenate(_ref_cols(w_in, 0, Z_END) + _ref_cols(w_in, DT_END, IN_W) + _ref_cols(w_in, Z_END, XBC_END), axis=1)
    dt = _pad_to(jnp.concatenate(_ref_cols(w_in, XBC_END, DT_END), axis=1), cols=128)
    gu = jnp.concatenate([pc["w_gate"][j] for j in range(4)] + [pc["w_up"][j] for j in range(4)], axis=1)
    rows = lambda n: pc[n].reshape(-1, pc[n].shape[-1])
    return {"main": main, "dt": dt, "out": rows("w_out"), "q": rows("w_q"), "kv": _cat_cols(pc["w_kv"]),
            "o": rows("w_o"), "gu": gu, "down": rows("w_down")}


GW_KEY = {"w_gate": "gate", "w_up": "up", "w_kv": "kv", "w_out": "out", "w_q": "q", "w_o": "o", "w_down": "down"}
RS_GROUPS = (("w_down", "w_gate", "w_up"), ("w_out", "w_q", "w_kv", "w_o"), ("w_in",))


def _shard_grad(name, GW):
    if name == "w_in":
        cw = IN_W // 4
        pieces = []
        for j in range(4):
            parts = []
            for a, b, src, col in W_IN_SEGS:
                lo, hi = max(a, j * cw), min(b, (j + 1) * cw)
                if lo < hi:
                    parts.append(GW[src][:, col + lo - a:col + hi - a])
            pieces.append(jnp.concatenate(parts, axis=1))
        return jnp.stack(pieces)
    g = GW[GW_KEY[name]]
    if dict(BIG)[name]:
        cw = g.shape[1] // 4
        return jnp.stack([g[:, j * cw:(j + 1) * cw] for j in range(4)])
    return g.reshape(4, g.shape[0] // 4, g.shape[1])


def _rs_pair(names, GW, core):
    gs = [_shard_grad(n, GW) for n in names]
    halves = [g.reshape(4, 2, g.shape[1] // 2, g.shape[2]) for g in gs]
    theirs = _pair_split_list(halves, "rs_pair_send_" + names[0])
    return [_pair_sum(h, t, core, "rs_pair_sum_" + n) for h, t, n in zip(halves, theirs, names)]


def _stack_sc(dt_bias, a_log, d):
    return _pad_to(jnp.concatenate([dt_bias, a_log, d], axis=0), rows=8, cols=128)


COMM_PARAMS = pltpu.CompilerParams(vmem_limit_bytes=VMEM_LIMIT)


def _dma_sems(*counts):
    return [pltpu.SemaphoreType.DMA((n,)) for n in counts]


def _allgather_list(arrs, name):
    n = len(arrs)
    halved = [a.shape[0] % 16 == 0 for a in arrs]
    oshape = [(4, 2, a.shape[0] // 2, a.shape[1]) if h else (4, 1) + a.shape for a, h in zip(arrs, halved)]

    def body(*refs):
        srcs, outs = refs[:n], refs[n:2 * n]
        ici_send, ici_recv, own_send, own_recv, fwd_send, fwd_recv = refs[2 * n:]
        x, y, c = lax.axis_index("x"), lax.axis_index("y"), lax.axis_index("c")
        me = 2 * x + y
        sib = (x, y, 1 - c)
        peers = _chip_peers(x, y)

        def half(i, h):
            r = arrs[i].shape[0] // 2
            if not halved[i]:
                return srcs[i]
            return srcs[i].at[pl.ds(h * r if isinstance(h, int) else pl.multiple_of(h * r, 8), r)]

        ici, own, fwd = [], [], []
        for i in range(n):
            mine_h = c if halved[i] else 0
            for k, (px, py) in enumerate(peers):
                s = 3 * i + k
                ici.append(_remote(half(i, c), outs[i].at[me, mine_h], ici_send.at[s], ici_recv.at[s], (px, py, c)))
            for h in range(2 if halved[i] else 1):
                s = 2 * i + h
                own.append(_remote(half(i, h), outs[i].at[me, h], own_send.at[s], own_recv.at[s], sib))
        for cp in ici + own:
            cp.start()
        for i in range(n):
            if not halved[i]:
                continue
            for k, (px, py) in enumerate(peers):
                s = 3 * i + k
                got = outs[i].at[2 * px + py, c]
                _remote(half(i, c), got, ici_send.at[s], ici_recv.at[s], (px, py, c)).wait_recv()
                f = _remote(got, got, fwd_send.at[s], fwd_recv.at[s], sib)
                f.start()
                fwd.append(f)
        for i in range(n):
            for k, (px, py) in enumerate(peers):
                s = 3 * i + k
                if halved[i]:
                    _remote(half(i, c), outs[i].at[2 * px + py, 1 - c], fwd_send.at[s], fwd_recv.at[s], sib).wait_recv()
                else:
                    _remote(srcs[i], outs[i].at[2 * px + py, 0], ici_send.at[s], ici_recv.at[s], (px, py, c)).wait_recv()
            for h in range(2 if halved[i] else 1):
                s = 2 * i + h
                _remote(half(i, h), outs[i].at[me, h], own_send.at[s], own_recv.at[s], sib).wait_recv()
        for cp in ici + own + fwd:
            cp.wait_send()

    outs = pl.pallas_call(
        body, name=name, in_specs=[HBM_SPEC] * n, out_specs=[HBM_SPEC] * n,
        out_shape=[SDS(s, a.dtype) for s, a in zip(oshape, arrs)],
        scratch_shapes=_dma_sems(3 * n, 3 * n, 2 * n, 2 * n, 3 * n, 3 * n), compiler_params=COMM_PARAMS)(*arrs)
    return [o.reshape((4,) + a.shape) for o, a in zip(outs, arrs)]


def _pair_split_list(gs, name):
    n = len(gs)

    def body(*refs):
        srcs, outs = refs[:n], refs[n:2 * n]
        send_sems, recv_sems = refs[2 * n:]
        x, y, c = lax.axis_index("x"), lax.axis_index("y"), lax.axis_index("c")
        sib = (x, y, 1 - c)
        sends = [_remote(srcs[i].at[j, 1 - c], outs[i].at[j], send_sems.at[4 * i + j], recv_sems.at[4 * i + j], sib)
                 for i in range(n) for j in range(4)]
        for cp in sends:
            cp.start()
        for cp in sends:
            cp.wait_recv()
        for cp in sends:
            cp.wait_send()

    return pl.pallas_call(
        body, name=name, in_specs=[HBM_SPEC] * n, out_specs=[HBM_SPEC] * n,
        out_shape=[SDS((4,) + g.shape[2:], g.dtype) for g in gs],
        scratch_shapes=_dma_sems(4 * n, 4 * n), compiler_params=COMM_PARAMS)(*gs)


def _scatter_list(ps, name="rs_chip_send"):
    n = len(ps)

    def body(*refs):
        sends, recvs = _scatter_copies(refs[:n], refs[n:2 * n], *refs[2 * n:])
        for cp in sends:
            cp.start()
        for cp in recvs:
            cp.wait_recv()
        for cp in sends:
            cp.wait_send()

    return pl.pallas_call(
        body, name=name, in_specs=[HBM_SPEC] * n, out_specs=[HBM_SPEC] * n,
        out_shape=[SDS(p.shape, p.dtype) for p in ps],
        scratch_shapes=_dma_sems(3 * n, 3 * n), compiler_params=COMM_PARAMS)(*ps)


JOIN_SPLIT = 4


def _pair_join_list(bufs, name="rs_pair_join"):
    n = len(bufs)

    def body(*refs):
        outs = refs[n:2 * n]
        send_sems, recv_sems = refs[2 * n:]
        x, y, c = lax.axis_index("x"), lax.axis_index("y"), lax.axis_index("c")
        sib = (x, y, 1 - c)
        sends, recvs = [], []
        for i in range(n):
            rc = bufs[i].shape[1] // JOIN_SPLIT
            for q in range(JOIN_SPLIT):
                k = JOIN_SPLIT * i + q
                rows = pl.ds(q * rc, rc)
                sends.append(_remote(outs[i].at[c, rows], outs[i].at[c, rows], send_sems.at[k], recv_sems.at[k], sib))
                recvs.append(_remote(outs[i].at[c, rows], outs[i].at[1 - c, rows], send_sems.at[k], recv_sems.at[k], sib))
        for cp in sends:
            cp.start()
        for cp in recvs:
            cp.wait_recv()
        for cp in sends:
            cp.wait_send()

    return pl.pallas_call(
        body, name=name, in_specs=[HBM_SPEC] * n, out_specs=[HBM_SPEC] * n,
        out_shape=[SDS(b.shape, b.dtype) for b in bufs], input_output_aliases={i: i for i in range(n)},
        scratch_shapes=_dma_sems(JOIN_SPLIT * n, JOIN_SPLIT * n), compiler_params=COMM_PARAMS)(*bufs)


def _pair_sum(g, theirs, core, name):
    _, _, r, c = g.shape

    def body(core_ref, g_ref, t_ref, o_ref):
        o_ref[...] = (g_ref[...] + t_ref[...]).astype(o_ref.dtype)

    spec = pltpu.PrefetchScalarGridSpec(
        num_scalar_prefetch=1, grid=(4,),
        in_specs=[pl.BlockSpec((None, None, r, c), lambda j, core_ref: (j, core_ref[0], 0, 0)),
                  pl.BlockSpec((None, r, c), lambda j, core_ref: (j, 0, 0))],
        out_specs=pl.BlockSpec((None, r, c), lambda j, core_ref: (j, 0, 0)))
    return pl.pallas_call(body, name=name, grid_spec=spec, out_shape=SDS((4, r, c), BF16),
                          compiler_params=_cp("parallel"))(core, g, theirs)


def _chip_sum(own, got, where, name):
    _, r, c = own.shape
    tr = r // 2

    def body(w_ref, a_ref, b1_ref, b2_ref, b3_ref, o_ref):
        o_ref[...] = ((a_ref[...].astype(F32) + b1_ref[...].astype(F32)) + b2_ref[...].astype(F32)) + b3_ref[...].astype(F32)

    piece = lambda k: pl.BlockSpec((None, tr, c), lambda i, w_ref: ((w_ref[0] + k) % 4, i, 0))
    spec = pltpu.PrefetchScalarGridSpec(
        num_scalar_prefetch=1, grid=(r // tr,), in_specs=[piece(0), piece(1), piece(2), piece(3)],
        out_specs=pl.BlockSpec((None, tr, c), lambda i, w_ref: (w_ref[1], i, 0)))
    return pl.pallas_call(body, name=name, grid_spec=spec, out_shape=SDS((2, r, c), F32),
                          compiler_params=_cp("parallel"))(where, own, got, got, got)


def _adam_math(w, g, m, v):
    bc1 = 1.0 - ADAM_B1 ** ADAM_STEP
    bc2 = 1.0 - ADAM_B2 ** ADAM_STEP
    mn = ADAM_B1 * m + (1.0 - ADAM_B1) * g
    vn = ADAM_B2 * v + (1.0 - ADAM_B2) * (g * g)
    return -ADAM_LR * ((mn / bc1) / (jnp.sqrt(vn / bc2) + ADAM_EPS) + ADAM_WD * w), mn, vn


PACK_COLS = XBC
PACK = {"g_mix": (0, 1, D), "g_xattn": (1, 1, D), "g_mem": (2, 1, D), "g_ffn": (3, 1, D), "g_final": (4, 1, D),
        "ssd_norm_g": (5, 1, D), "cf_b": (6, 1, D), "ln_g": (7, 1, D), "ln_b": (8, 1, D), "conv4_b": (9, 1, XBC),
        "conv4_w": (10, KS, XBC), "sc": (16, 8, 128), "cf_w": (24, KC, D), "loss": (55, 1, 128)}
PACK_ROWS = 56
SMALL_ADAM = ["g_mix", "g_xattn", "g_mem", "g_ffn", "g_final", "ssd_norm_g", "cf_b", "ln_g", "ln_b", "conv4_b", "sc"]


def _small_allreduce_adamw(grads, wts, mom, var, name="allreduce_small"):
    gk = list(PACK)
    ng, na = len(gk), len(SMALL_ADAM)

    def body(*refs):
        g_in = refs[:ng]
        w_in, m_in, v_in = (refs[ng + i * na: ng + (i + 1) * na] for i in range(3))
        o = refs[ng + 3 * na:]
        g_out = o[:ng]
        d_out, m_out, v_out = (o[ng + i * na: ng + (i + 1) * na] for i in range(3))
        pack, buf, acc, send_sems, recv_sems = o[ng + 3 * na:]
        x, y, c = lax.axis_index("x"), lax.axis_index("y"), lax.axis_index("c")
        me = 4 * x + 2 * y + c
        pack[...] = jnp.zeros_like(pack)
        for i, k in enumerate(gk):
            r0, nr, nc = PACK[k]
            pack[r0:r0 + nr, 0:nc] = g_in[i][...]
        peers = [(x, y, 1 - c)] + [(px, py, pc) for px, py in _chip_peers(x, y) for pc in (c, 1 - c)]
        sends = [_remote(pack, buf.at[me], send_sems.at[k], recv_sems.at[k], dev) for k, dev in enumerate(peers)]
        for cp in sends:
            cp.start()
        buf[me] = pack[...]
        for k, (px, py, pc) in enumerate(peers):
            _remote(pack, buf.at[4 * px + 2 * py + pc], send_sems.at[k], recv_sems.at[k], (px, py, pc)).wait_recv()
        for cp in sends:
            cp.wait_send()
        tot = buf[0]
        for i in range(1, 8):
            tot = tot + buf[i]
        acc[...] = tot
        for i, k in enumerate(gk):
            r0, nr, nc = PACK[k]
            g_out[i][...] = acc[r0:r0 + nr, 0:nc]
        for i, k in enumerate(SMALL_ADAM):
            r0, nr, nc = PACK[k]
            d_out[i][...], m_out[i][...], v_out[i][...] = _adam_math(
                w_in[i][...], acc[r0:r0 + nr, 0:nc], m_in[i][...], v_in[i][...])

    args = [grads[k] for k in gk] + [d[k] for d in (wts, mom, var) for k in SMALL_ADAM]
    shp = lambda k: SDS((PACK[k][1], PACK[k][2]), F32)
    vm = pl.BlockSpec(memory_space=pltpu.VMEM)
    outs = pl.pallas_call(
        body, name=name, in_specs=[vm] * len(args), out_specs=[vm] * (ng + 3 * na),
        out_shape=[shp(k) for k in gk] + [shp(k) for _ in range(3) for k in SMALL_ADAM],
        scratch_shapes=[pltpu.VMEM((PACK_ROWS, PACK_COLS), F32), pltpu.VMEM((8, PACK_ROWS, PACK_COLS), F32),
                        pltpu.VMEM((PACK_ROWS, PACK_COLS), F32)] + _dma_sems(7, 7),
        compiler_params=COMM_PARAMS)(*args)
    red = dict(zip(gk, outs[:ng]))
    parts = [dict(zip(SMALL_ADAM, outs[ng + i * na: ng + (i + 1) * na])) for i in range(3)]
    return red, parts[0], parts[1], parts[2]


def _adamw_cols(w, gfull, m, v, chip, name):
    _, R, C = w.shape

    def body(w_idx, w_ref, g_ref, m_ref, v_ref, go_ref, d_ref, mo_ref, vo_ref):
        go_ref[...] = g_ref[...]
        d_ref[...], mo_ref[...], vo_ref[...] = _adam_math(w_ref[...], g_ref[...], m_ref[...], v_ref[...])

    blk = pl.BlockSpec((None, R, C), lambda i, w_idx: (0, 0, 0))
    spec = pltpu.PrefetchScalarGridSpec(
        num_scalar_prefetch=1, grid=(1,),
        in_specs=[blk, pl.BlockSpec((R, C), lambda i, w_idx: (0, w_idx[0])), blk, blk], out_specs=[blk] * 4)
    return pl.pallas_call(body, name=name, grid_spec=spec, out_shape=[SDS((1, R, C), F32)] * 4,
                          compiler_params=_cp("arbitrary"))(chip, w, gfull, m, v)


def _adamw(w, g, m, v, name):
    _, R, C = w.shape
    half = R // 2
    tr = _tile(half, max(8, (2 ** 17 // C) // 8 * 8), 8)
    nh = half // tr

    def body(w_ref, g_ref, m_ref, v_ref, go_ref, d_ref, mo_ref, vo_ref):
        go_ref[...] = g_ref[...]
        d_ref[...], mo_ref[...], vo_ref[...] = _adam_math(w_ref[...], g_ref[...], m_ref[...], v_ref[...])

    blk = pl.BlockSpec((None, tr, C), lambda i: (0, i, 0))
    gblk = pl.BlockSpec((None, tr, C), lambda i: (i // nh, i % nh, 0))
    return pl.pallas_call(body, name=name, grid=(R // tr,), in_specs=[blk, gblk, blk, blk], out_specs=[blk] * 4,
                          out_shape=[SDS((1, R, C), F32)] * 4, compiler_params=_cp("parallel"))(w, g, m, v)


WEIGHT_NAMES = ["norm_mix_g", "w_in", "ssd_conv_w", "ssd_conv_b", "ssd_dt_bias", "ssd_A_log", "ssd_D", "ssd_norm_g",
                "cf_conv_w", "cf_conv_b", "cf_ln_g", "cf_ln_b", "w_out", "norm_xattn_g", "norm_mem_g", "w_q", "w_kv",
                "w_o", "norm_ffn_g", "w_gate", "w_up", "w_down", "norm_final_g"]
VEC_REF = [("norm_mix_g", "g_mix"), ("norm_xattn_g", "g_xattn"), ("norm_mem_g", "g_mem"), ("norm_ffn_g", "g_ffn"),
           ("norm_final_g", "g_final"), ("ssd_norm_g", "ssd_norm_g"), ("cf_conv_b", "cf_b"), ("cf_ln_g", "ln_g"),
           ("cf_ln_b", "ln_b"), ("ssd_conv_b", "conv4_b")]
SC_REF = ["ssd_dt_bias", "ssd_A_log", "ssd_D"]


def _small_side(get):
    d = {k: get(ref_name).reshape(1, -1) for ref_name, k in VEC_REF}
    d["sc"] = _stack_sc(*[get(n) for n in SC_REF])
    return d


def kernel(x, mem, norm_mix_g, w_in, ssd_conv_w, ssd_conv_b, ssd_dt_bias, ssd_A_log, ssd_D, ssd_norm_g, cf_conv_w, cf_conv_b, cf_ln_g, cf_ln_b, w_out, norm_xattn_g, norm_mem_g, w_q, w_kv, w_o, norm_ffn_g, w_gate, w_up, w_down, norm_final_g, loss_target, m_norm_mix_g, m_w_in, m_ssd_conv_w, m_ssd_conv_b, m_ssd_dt_bias, m_ssd_A_log, m_ssd_D, m_ssd_norm_g, m_cf_conv_w, m_cf_conv_b, m_cf_ln_g, m_cf_ln_b, m_w_out, m_norm_xattn_g, m_norm_mem_g, m_w_q, m_w_kv, m_w_o, m_norm_ffn_g, m_w_gate, m_w_up, m_w_down, m_norm_final_g, v_norm_mix_g, v_w_in, v_ssd_conv_w, v_ssd_conv_b, v_ssd_dt_bias, v_ssd_A_log, v_ssd_D, v_ssd_norm_g, v_cf_conv_w, v_cf_conv_b, v_cf_ln_g, v_cf_ln_b, v_w_out, v_norm_xattn_g, v_norm_mem_g, v_w_q, v_w_kv, v_w_o, v_norm_ffn_g, v_w_gate, v_w_up, v_w_down, v_norm_final_g):
    env = dict(locals())
    wts = {n: env[n] for n in WEIGHT_NAMES}
    mom = {n: env["m_" + n] for n in WEIGHT_NAMES}
    var = {n: env["v_" + n] for n in WEIGHT_NAMES}
    chip = (2 * lax.axis_index("x") + lax.axis_index("y")).astype(jnp.int32).reshape(1)
    core = lax.axis_index("c").astype(jnp.int32).reshape(1)
    where = jnp.concatenate([chip, core])
    big = [n for n, _ in BIG]

    gathered = _allgather_list([wts[n][0].astype(BF16) for n in big] + [ssd_conv_w[0], cf_conv_w[0]], "allgather_weights")
    W = _pack_weights(dict(zip(big, gathered)))
    P = _small_side(lambda n: wts[n])
    P["conv4_w"], P["cf_w"] = _cat_cols(gathered[len(big)]), _cat_cols(gathered[len(big) + 1])

    loss, grad_x, GW, GP, pair, got = _local_step(x[0], mem[0], loss_target[0], W, P, core)
    joined = _pair_join_list([_chip_sum(pair[n], got[n], where, "rs_chip_sum_" + n) for n in big])
    gshard = dict(zip(big, joined))

    small = dict(GP)
    small["loss"] = loss
    red, sd, sm, sv = _small_allreduce_adamw(small, {k: P[k] for k in SMALL_ADAM}, _small_side(lambda n: mom[n]),
                                             _small_side(lambda n: var[n]))
    grads, delta, new_m, new_v = {}, {}, {}, {}
    for ref_name, k in VEC_REF:
        shp = wts[ref_name].shape
        for dst, src in ((grads, red), (delta, sd), (new_m, sm), (new_v, sv)):
            dst[ref_name] = src[k].reshape(shp)
    for row, ref_name in enumerate(SC_REF):
        for dst, src in ((grads, red), (delta, sd), (new_m, sm), (new_v, sv)):
            dst[ref_name] = src["sc"][row:row + 1, :NH]

    for n, k in (("ssd_conv_w", "conv4_w"), ("cf_conv_w", "cf_w")):
        grads[n], delta[n], new_m[n], new_v[n] = _adamw_cols(wts[n], red[k], mom[n], var[n], chip, "adamw_" + n)
    for n in big:
        grads[n], delta[n], new_m[n], new_v[n] = _adamw(wts[n], gshard[n], mom[n], var[n], "adamw_" + n)

    return (red["loss"][0, 0], grad_x[None], *[grads[n] for n in WEIGHT_NAMES], *[delta[n] for n in WEIGHT_NAMES],
            *[new_m[n] for n in WEIGHT_NAMES], *[new_v[n] for n in WEIGHT_NAMES])
```

```python
import functools
import math

import jax
import jax.numpy as jnp
from jax import lax
from jax.experimental import pallas as pl
from jax.experimental.pallas import tpu as pltpu

F32 = jnp.float32
BF16 = jnp.bfloat16
_MXU = BF16

D = 1024
MEM = 256
NH, HP, NG, NS = 16, 64, 2, 128
GW = NH * HP // NG
CH = 128
XBC = NH * HP + 2 * NG * NS
KS, KC = 4, 31
XH, XD = 4, 256
DFF = 2816
FB = 256
EPS = 1e-6
COL_Z, COL_A, COL_G, COL_XBC, MAINW = 0, 1024, 2048, 3072, 4608
VMEM_LIMIT = 56 * 2 ** 20

ADAM_LR, ADAM_B1, ADAM_B2, ADAM_EPS, ADAM_WD, ADAM_STEP = 0.001, 0.9, 0.999, 1e-08, 0.01, 10

SDS = jax.ShapeDtypeStruct
MESHID = pl.DeviceIdType.MESH


def _cp(*sem):
    return pltpu.CompilerParams(dimension_semantics=sem, vmem_limit_bytes=VMEM_LIMIT)


def _tile(n, cap, unit=128):
    if n <= cap:
        return n
    best = None
    for t in range(unit, cap + 1, unit):
        if n % t == 0:
            best = t
    assert best is not None, (n, cap)
    return best


def _sigmoid(x):
    return 1.0 / (1.0 + jnp.exp(-x))


def _silu(x):
    return x * _sigmoid(x)


def _dsilu(x):
    s = _sigmoid(x)
    return s * (1.0 + x * (1.0 - s))


def _softplus(x):
    return jnp.maximum(x, 0.0) + jnp.log(1.0 + jnp.exp(-jnp.abs(x)))


def _split_bf16(x, passes):
    parts, r = [], x.astype(F32)
    for _ in range(passes):
        p = r.astype(BF16)
        parts.append(p)
        r = r - p.astype(F32)
    return parts


def _dot(a, b, dims=None, exact=None, passes=2):
    dn = {None: (((1,), (0,)), ((), ())), "nt": (((1,), (1,)), ((), ())), "tn": (((0,), (0,)), ((), ()))}[dims]
    if exact is None:
        return lax.dot_general(a.astype(_MXU), b.astype(_MXU), dn, preferred_element_type=F32)
    if exact == "a":
        terms = [(a.astype(BF16), p) for p in _split_bf16(b, passes)]
    else:
        terms = [(p, b.astype(BF16)) for p in _split_bf16(a, passes)]
    out = None
    for lhs, rhs in terms:
        d = lax.dot_general(lhs, rhs, dn, preferred_element_type=F32)
        out = d if out is None else out + d
    return out


def _mm_nn(a, b, name, add=None, out_dtype=F32, tm_cap=1024, tn_cap=1408):
    M, K = a.shape
    _, N = b.shape
    tm, tn = _tile(M, tm_cap, 8), _tile(N, tn_cap)

    def body(a_ref, b_ref, *rest):
        o_ref = rest[-1]
        acc = _dot(a_ref[...], b_ref[...])
        if add is not None:
            acc = acc + rest[0][...]
        o_ref[...] = acc.astype(o_ref.dtype)

    in_specs = [pl.BlockSpec((tm, K), lambda j, i: (i, 0)), pl.BlockSpec((K, tn), lambda j, i: (0, j))]
    args = [a, b]
    if add is not None:
        in_specs.append(pl.BlockSpec((tm, tn), lambda j, i: (i, j)))
        args.append(add)
    return pl.pallas_call(
        body, name=name, grid=(N // tn, M // tm), in_specs=in_specs,
        out_specs=pl.BlockSpec((tm, tn), lambda j, i: (i, j)), out_shape=SDS((M, N), out_dtype),
        compiler_params=_cp("parallel", "parallel"))(*args)


def _mm_nt(a, b, name, add=None, out_dtype=F32, tm_cap=512, tk_cap=1024, b_col=0):
    M, N = a.shape
    K = b.shape[0]
    tm, tk = _tile(M, tm_cap, 8), _tile(K, tk_cap)

    def body(a_ref, b_ref, *rest):
        o_ref = rest[-1]
        acc = _dot(a_ref[...], b_ref[...], "nt")
        if add is not None:
            acc = acc + rest[0][...]
        o_ref[...] = acc.astype(o_ref.dtype)

    in_specs = [pl.BlockSpec((tm, N), lambda j, i: (i, 0)), pl.BlockSpec((tk, N), lambda j, i: (j, b_col))]
    args = [a, b]
    if add is not None:
        in_specs.append(pl.BlockSpec((tm, tk), lambda j, i: (i, j)))
        args.append(add)
    return pl.pallas_call(
        body, name=name, grid=(K // tk, M // tm), in_specs=in_specs,
        out_specs=pl.BlockSpec((tm, tk), lambda j, i: (i, j)), out_shape=SDS((M, K), out_dtype),
        compiler_params=_cp("parallel", "parallel"))(*args)


def _mm_tn(a, b, name, tm_cap=1024, tk_cap=512, tn_cap=1408):
    M, K = a.shape
    _, N = b.shape
    tm, tk, tn = _tile(M, tm_cap, 8), _tile(K, tk_cap), _tile(N, tn_cap)

    def body(a_ref, b_ref, o_ref):
        @pl.when(pl.program_id(2) == 0)
        def _():
            o_ref[...] = jnp.zeros_like(o_ref)

        o_ref[...] += _dot(a_ref[...], b_ref[...], "tn")

    return pl.pallas_call(
        body, name=name, grid=(K // tk, N // tn, M // tm),
        in_specs=[pl.BlockSpec((tm, tk), lambda k, n, m: (m, k)), pl.BlockSpec((tm, tn), lambda k, n, m: (m, n))],
        out_specs=pl.BlockSpec((tk, tn), lambda k, n, m: (k, n)), out_shape=SDS((K, N), F32),
        compiler_params=_cp("parallel", "parallel", "arbitrary"))(a, b)


def _rms_fwd(x, g, name, tb_cap=512):
    S, Dm = x.shape
    tb = _tile(S, tb_cap, 8)

    def body(x_ref, g_ref, o_ref):
        xv = x_ref[...]
        r = lax.rsqrt(jnp.mean(xv * xv, axis=-1, keepdims=True) + EPS)
        o_ref[...] = (xv * r * g_ref[...]).astype(o_ref.dtype)

    return pl.pallas_call(
        body, name=name, grid=(S // tb,),
        in_specs=[pl.BlockSpec((tb, Dm), lambda i: (i, 0)), pl.BlockSpec((1, Dm), lambda i: (0, 0))],
        out_specs=pl.BlockSpec((tb, Dm), lambda i: (i, 0)), out_shape=SDS((S, Dm), _MXU),
        compiler_params=_cp("parallel"))(x, g)


def _rms_bwd(x, g, dh, dres, name, tb_cap=512, low=True):
    S, Dm = x.shape
    tb = _tile(S, tb_cap, 8)
    need_dx = dres is not None

    def body(x_ref, g_ref, dh_ref, *rest):
        dg_ref = rest[-1]
        xv = x_ref[...]
        r = lax.rsqrt(jnp.mean(xv * xv, axis=-1, keepdims=True) + EPS)
        xh = xv * r
        dy = dh_ref[...].astype(F32)

        @pl.when(pl.program_id(0) == 0)
        def _():
            dg_ref[...] = jnp.zeros_like(dg_ref)

        dg_ref[...] += jnp.sum(dy * xh, axis=0, keepdims=True)
        if need_dx:
            gdy = dy * g_ref[...]
            dx = r * (gdy - xh * jnp.mean(xh * gdy, axis=-1, keepdims=True))
            tot = rest[0][...] + dx
            rest[1][...] = tot
            if low:
                rest[2][...] = tot.astype(rest[2].dtype)

    row = pl.BlockSpec((tb, Dm), lambda i: (i, 0))
    vec = pl.BlockSpec((1, Dm), lambda i: (0, 0))
    if need_dx:
        outs = [SDS((S, Dm), F32)] + ([SDS((S, Dm), _MXU)] if low else [])
        return pl.pallas_call(
            body, name=name, grid=(S // tb,), in_specs=[row, vec, row, row], out_specs=[row] * len(outs) + [vec],
            out_shape=outs + [SDS((1, Dm), F32)], compiler_params=_cp("arbitrary"))(x, g, dh, dres)
    return pl.pallas_call(
        body, name=name, grid=(S // tb,), in_specs=[row, vec, row], out_specs=vec,
        out_shape=SDS((1, Dm), F32), compiler_params=_cp("arbitrary"))(x, g, dh)


def _final_loss(x, g, tgt, name="final_loss", tb_cap=512):
    S, Dm = x.shape
    tb = _tile(S, tb_cap, 8)

    def body(x_ref, g_ref, t_ref, loss_ref, dx_ref, dxl_ref, dg_ref):
        xv = x_ref[...]
        gv = g_ref[...]
        r = lax.rsqrt(jnp.mean(xv * xv, axis=-1, keepdims=True) + EPS)
        xh = xv * r
        e = xh * gv - t_ref[...]

        @pl.when(pl.program_id(0) == 0)
        def _():
            loss_ref[...] = jnp.zeros_like(loss_ref)
            dg_ref[...] = jnp.zeros_like(dg_ref)

        loss_ref[...] += 0.5 * jnp.sum(jnp.mean(e * e, axis=-1, keepdims=True))
        dy = e * (1.0 / Dm)
        dg_ref[...] += jnp.sum(dy * xh, axis=0, keepdims=True)
        gdy = dy * gv
        dx = r * (gdy - xh * jnp.mean(xh * gdy, axis=-1, keepdims=True))
        dx_ref[...] = dx
        dxl_ref[...] = dx.astype(dxl_ref.dtype)

    row = pl.BlockSpec((tb, Dm), lambda i: (i, 0))
    vec = pl.BlockSpec((1, Dm), lambda i: (0, 0))
    return pl.pallas_call(
        body, name=name, grid=(S // tb,), in_specs=[row, vec, row],
        out_specs=[pl.BlockSpec((1, 128), lambda i: (0, 0)), row, row, vec],
        out_shape=[SDS((1, 128), F32), SDS((S, Dm), F32), SDS((S, Dm), _MXU), SDS((1, Dm), F32)],
        compiler_params=_cp("arbitrary"))(x, g, tgt)


SSD_HALO = 8
CF_HALO = 32
CONV_CB = 512


def _ssd_conv_fwd(proj, w, b, name="ssd_conv_fwd", tb_cap=512):
    S = proj.shape[0]
    tb = _tile(S, tb_cap, 8)
    nb = S // tb
    c0 = COL_XBC // CONV_CB

    def body(x_ref, w_ref, b_ref, o_ref, ext):
        @pl.when(pl.program_id(1) == 0)
        def _():
            ext[pl.ds(0, SSD_HALO), :] = jnp.zeros((SSD_HALO, CONV_CB), F32)

        ext[pl.ds(SSD_HALO, tb), :] = x_ref[...]
        acc = jnp.zeros((tb, CONV_CB), F32) + b_ref[...]
        for k in range(KS):
            acc = acc + ext[pl.ds(SSD_HALO - (KS - 1) + k, tb), :] * w_ref[k:k + 1, :]
        o_ref[...] = acc
        ext[pl.ds(0, SSD_HALO), :] = ext[pl.ds(tb, SSD_HALO), :]

    return pl.pallas_call(
        body, name=name, grid=(XBC // CONV_CB, nb),
        in_specs=[pl.BlockSpec((tb, CONV_CB), lambda j, i: (i, c0 + j)),
                  pl.BlockSpec((KS, CONV_CB), lambda j, i: (0, j)),
                  pl.BlockSpec((1, CONV_CB), lambda j, i: (0, j))],
        out_specs=pl.BlockSpec((tb, CONV_CB), lambda j, i: (i, j)), out_shape=SDS((S, XBC), F32),
        scratch_shapes=[pltpu.VMEM((SSD_HALO + tb, CONV_CB), F32)],
        compiler_params=_cp("parallel", "arbitrary"))(proj, w, b)


def _ssd_conv_bwd(dxbc, proj, w, name="ssd_conv_bwd", tb_cap=512):
    S = proj.shape[0]
    tb = _tile(S, tb_cap, 8)
    nb = S // tb
    c0 = COL_XBC // CONV_CB

    def body(dy_ref, x_ref, w_ref, dx_ref, dw_ref, db_ref, ext):
        @pl.when(pl.program_id(1) == 0)
        def _():
            ext[pl.ds(tb, SSD_HALO), :] = jnp.zeros((SSD_HALO, CONV_CB), F32)
            dw_ref[...] = jnp.zeros_like(dw_ref)
            db_ref[...] = jnp.zeros_like(db_ref)

        dy = dy_ref[...]
        ext[pl.ds(0, tb), :] = dy
        xv = x_ref[...]
        acc = jnp.zeros((tb, CONV_CB), F32)
        for k in range(KS):
            sh = ext[pl.ds(KS - 1 - k, tb), :]
            acc = acc + sh * w_ref[k:k + 1, :]
            dw_ref[k:k + 1, :] += jnp.sum(xv * sh, axis=0, keepdims=True)
        db_ref[...] += jnp.sum(dy, axis=0, keepdims=True)
        dx_ref[...] = acc.astype(dx_ref.dtype)
        ext[pl.ds(tb, SSD_HALO), :] = ext[pl.ds(0, SSD_HALO), :]

    return pl.pallas_call(
        body, name=name, grid=(XBC // CONV_CB, nb),
        in_specs=[pl.BlockSpec((tb, CONV_CB), lambda j, i: (nb - 1 - i, j)),
                  pl.BlockSpec((tb, CONV_CB), lambda j, i: (nb - 1 - i, c0 + j)),
                  pl.BlockSpec((KS, CONV_CB), lambda j, i: (0, j))],
        out_specs=[pl.BlockSpec((tb, CONV_CB), lambda j, i: (nb - 1 - i, j)),
                   pl.BlockSpec((KS, CONV_CB), lambda j, i: (0, j)),
                   pl.BlockSpec((1, CONV_CB), lambda j, i: (0, j))],
        out_shape=[SDS((S, XBC), _MXU), SDS((KS, XBC), F32), SDS((1, XBC), F32)],
        scratch_shapes=[pltpu.VMEM((tb + SSD_HALO, CONV_CB), F32)],
        compiler_params=_cp("parallel", "arbitrary"))(dxbc, proj, w)


HBM_SPEC = pl.BlockSpec(memory_space=pl.ANY)


def _chip_peers(x, y):
    return [(1 - x, y), (x, 1 - y), (1 - x, 1 - y)]


def _remote(src, dst, send_sem, recv_sem, dev):
    return pltpu.make_async_remote_copy(src_ref=src, dst_ref=dst, send_sem=send_sem, recv_sem=recv_sem,
                                        device_id=dev, device_id_type=MESHID)


def _scatter_copies(srcs, outs, send_sems, recv_sems):
    x, y, c = lax.axis_index("x"), lax.axis_index("y"), lax.axis_index("c")
    me = 2 * x + y
    sends, recvs = [], []
    for i, (s, o) in enumerate(zip(srcs, outs)):
        for k, (px, py) in enumerate(_chip_peers(x, y)):
            j = 3 * i + k
            sends.append(_remote(s.at[2 * px + py], o.at[me], send_sems.at[j], recv_sems.at[j], (px, py, c)))
            recvs.append(_remote(s.at[me], o.at[2 * px + py], send_sems.at[j], recv_sems.at[j], (px, py, c)))
    return sends, recvs


class _RidingExchange:
    def __init__(self, srcs, outs, sems, steps):
        self.srcs, self.outs, self.sems, self.steps = srcs, outs, sems, steps

    @staticmethod
    def scratch(n):
        return [pltpu.SemaphoreType.DMA((3 * n,)), pltpu.SemaphoreType.DMA((3 * n,))] if n else []

    def copies(self):
        return _scatter_copies(self.srcs, self.outs, *self.sems)

    def start(self):
        if self.srcs:
            @pl.when(pl.program_id(0) == 0)
            def _():
                for cp in self.copies()[0]:
                    cp.start()

    def finish(self):
        if self.srcs:
            @pl.when(pl.program_id(0) == self.steps - 1)
            def _():
                sends, recvs = self.copies()
                for cp in recvs:
                    cp.wait_recv()
                for cp in sends:
                    cp.wait_send()


def _rows_half(ref, rows, h):
    r = rows // 2
    return ref.at[pl.ds(h * r if isinstance(h, int) else pl.multiple_of(h * r, 8), r)]


def _gather_copies(srcs, outs, rows, send_sems, recv_sems):
    x, y, c = lax.axis_index("x"), lax.axis_index("y"), lax.axis_index("c")
    me = 2 * x + y
    sends, recvs = [], []
    for i, (s, o) in enumerate(zip(srcs, outs)):
        mine = _rows_half(s, rows[i], c)
        for k, (px, py) in enumerate(_chip_peers(x, y)):
            j = 3 * i + k
            sends.append(_remote(mine, o.at[me, c], send_sems.at[j], recv_sems.at[j], (px, py, c)))
            recvs.append(_remote(mine, o.at[2 * px + py, c], send_sems.at[j], recv_sems.at[j], (px, py, c)))
    return sends, recvs


def _gather_shapes(shards):
    return [SDS((4, 2, a.shape[0] // 2, a.shape[1]), a.dtype) for a in shards]


class _RidingGather(_RidingExchange):
    def __init__(self, srcs, outs, sems, steps, rows):
        super().__init__(srcs, outs, sems, steps)
        self.rows = rows

    def copies(self):
        return _gather_copies(self.srcs, self.outs, self.rows, *self.sems)


def _head_consts():
    e = (lax.broadcasted_iota(jnp.int32, (128, NH * HP), 1) // HP == lax.broadcasted_iota(jnp.int32, (128, NH * HP), 0)).astype(F32)
    et = (lax.broadcasted_iota(jnp.int32, (NH * HP, 128), 0) // HP == lax.broadcasted_iota(jnp.int32, (NH * HP, 128), 1)).astype(F32)
    r = lax.broadcasted_iota(jnp.int32, (CH, CH), 0)
    c = lax.broadcasted_iota(jnp.int32, (CH, CH), 1)
    return e, et, (c <= r), (r <= c)


def _ssd_common(xbc_c, dtr, dtb, alog, e, tril, triu):
    xbc = _silu(xbc_c)
    xs = xbc[:, :NH * HP]
    dt = _softplus(dtr + dtb)
    A = -jnp.exp(alog)
    a = dt * A
    cs = _dot(tril, a, exact="a", passes=3)
    csT = _dot(a, triu, "tn", exact="b", passes=3)
    csL = cs[CH - 1:CH, :]
    wdec = jnp.exp(csL - cs) * dt
    dtE = _dot(dt, e, exact="b")
    ecsE = _dot(jnp.exp(cs), e, exact="b")
    wE = _dot(wdec, e, exact="b")
    eL = jnp.exp(csL)
    return xbc, xs, dt, A, cs, csT, csL, wdec, dtE, ecsE, wE, eL


def _ssd_fwd(proj, xbc_c, dtr, sc, norm_g, comm=(), name="ssd_fwd"):
    S = proj.shape[0]
    nc = S // CH
    nco = len(comm)

    def body(*refs):
        z_ref, x_ref, dtr_ref, sc_ref, ng_ref = refs[:5]
        y_ref, yn_ref, hp_ref = refs[5 + nco:8 + nco]
        hst = refs[8 + 2 * nco]
        gather = _RidingGather(refs[5:5 + nco], refs[8 + nco:8 + 2 * nco], refs[9 + 2 * nco:], nc,
                               [a.shape[0] for a in comm])
        gather.start()

        @pl.when(pl.program_id(0) == 0)
        def _():
            hst[...] = jnp.zeros_like(hst)

        e, et, tril, triu = _head_consts()
        xbc, xs, dt, A, cs, csT, csL, wdec, dtE, ecsE, wE, eL = _ssd_common(
            x_ref[...], dtr_ref[...], sc_ref[0:1, :], sc_ref[1:2, :], e, tril, triu)
        hp_ref[0] = hst[...]
        xd = xs * dtE
        xw = xs * wE
        dE = _dot(jnp.broadcast_to(sc_ref[2:3, :], (8, 128)), e, exact="b", passes=3)[0:1, :]
        eLcol = jnp.sum(et * eL, axis=1, keepdims=True)
        for g in range(NG):
            Bg = xbc[:, NH * HP + g * NS: NH * HP + (g + 1) * NS]
            Cg = xbc[:, NH * HP + NG * NS + g * NS: NH * HP + NG * NS + (g + 1) * NS]
            gs = slice(g * GW, (g + 1) * GW)
            G = _dot(Cg, Bg, "nt")
            hg = hst[gs, :]
            yoff = ecsE[:, gs] * _dot(Cg, hg, "nt")
            hst[gs, :] = eLcol[gs, :] * hg + _dot(xw[:, gs], Bg, "tn")
            for hh in range(NH // NG):
                h = g * (NH // NG) + hh
                hs = slice(h * HP, (h + 1) * HP)
                m = jnp.where(tril, jnp.exp(jnp.where(tril, cs[:, h:h + 1] - csT[h:h + 1, :], 0.0)), 0.0)
                yd = _dot(G * m, xd[:, hs])
                y_ref[:, hs] = yd + yoff[:, hh * HP:(hh + 1) * HP] + dE[:, hs] * xs[:, hs]
        y = y_ref[...]
        yz = y * _silu(z_ref[...])
        for g in range(NG):
            gs = slice(g * GW, (g + 1) * GW)
            yg = yz[:, gs]
            r = lax.rsqrt(jnp.mean(yg * yg, axis=-1, keepdims=True) + EPS)
            yn_ref[:, gs] = (yg * r * ng_ref[:, gs]).astype(yn_ref.dtype)
        gather.finish()

    outs = pl.pallas_call(
        body, name=name, grid=(nc,),
        in_specs=[pl.BlockSpec((CH, D), lambda c: (c, COL_Z // D)),
                  pl.BlockSpec((CH, XBC), lambda c: (c, 0)),
                  pl.BlockSpec((CH, 128), lambda c: (c, 0)),
                  pl.BlockSpec((8, 128), lambda c: (0, 0)),
                  pl.BlockSpec((1, D), lambda c: (0, 0))] + [HBM_SPEC] * nco,
        out_specs=[pl.BlockSpec((CH, D), lambda c: (c, 0)), pl.BlockSpec((CH, D), lambda c: (c, 0)),
                   pl.BlockSpec((1, NH * HP, NS), lambda c: (c, 0, 0))] + [HBM_SPEC] * nco,
        out_shape=[SDS((S, D), F32), SDS((S, D), _MXU), SDS((nc, NH * HP, NS), F32)] + _gather_shapes(comm),
        scratch_shapes=[pltpu.VMEM((NH * HP, NS), F32)] + _RidingExchange.scratch(nco),
        compiler_params=_cp("arbitrary"))(proj, xbc_c, dtr, sc, norm_g, *comm)
    return outs[:3], outs[3:]


def _ssd_bwd(dmix, y, proj, xbc_c, dtr, hprev, sc, norm_g, comm=(), name="ssd_bwd"):
    S = proj.shape[0]
    nc = S // CH
    nco = len(comm)
    rev = lambda c: nc - 1 - c

    def body(*refs):
        dyn_ref, y_ref, z_ref, x_ref, dtr_ref, hp_ref, sc_ref, ng_ref = refs[:8]
        dz_ref, dx_ref, ddtr_ref, gsc_ref, gng_ref = refs[8 + nco:13 + nco]
        dh, dxd = refs[13 + 2 * nco:15 + 2 * nco]
        exchange = _RidingExchange(refs[8:8 + nco], refs[13 + nco:13 + 2 * nco], refs[15 + 2 * nco:], nc)
        exchange.start()

        @pl.when(pl.program_id(0) == 0)
        def _():
            dh[...] = jnp.zeros_like(dh)
            gsc_ref[...] = jnp.zeros_like(gsc_ref)
            gng_ref[...] = jnp.zeros_like(gng_ref)

        e, et, tril, triu = _head_consts()
        xbc_c = x_ref[...]
        dtr = dtr_ref[...]
        dtb = sc_ref[0:1, :]
        xbc, xs, dt, A, cs, csT, csL, wdec, dtE, ecsE, wE, eL = _ssd_common(
            xbc_c, dtr, dtb, sc_ref[1:2, :], e, tril, triu)
        xd = xs * dtE
        xw = xs * wE
        dE = _dot(jnp.broadcast_to(sc_ref[2:3, :], (8, 128)), e, exact="b", passes=3)[0:1, :]
        eLcol = jnp.sum(et * eL, axis=1, keepdims=True)

        yv = y_ref[...]
        zv = z_ref[...]
        sz = _silu(zv)
        yz = yv * sz
        dyn = dyn_ref[...]
        dyz_parts = []
        for g in range(NG):
            gs = slice(g * GW, (g + 1) * GW)
            yg = yz[:, gs]
            r = lax.rsqrt(jnp.mean(yg * yg, axis=-1, keepdims=True) + EPS)
            yh = yg * r
            dn = dyn[:, gs]
            gng_ref[:, gs] += jnp.sum(dn * yh, axis=0, keepdims=True)
            gdn = dn * ng_ref[:, gs]
            dyz_parts.append(r * (gdn - yh * jnp.mean(yh * gdn, axis=-1, keepdims=True)))
        dyz = jnp.concatenate(dyz_parts, axis=1)
        dy = dyz * sz
        dz_ref[...] = (dyz * yv * _dsilu(zv)).astype(dz_ref.dtype)

        gsc_ref[2:3, :] += jnp.sum(_dot(dy * xs, et, exact="b", passes=3), axis=0, keepdims=True)
        dxs = dE * dy
        dzo = ecsE * dy
        dcs = jnp.zeros((CH, 128), F32)
        dcsL = jnp.zeros((1, 128), F32)
        ddt = jnp.zeros((CH, 128), F32)
        dB_parts, dC_parts, yoff_parts, dxw_parts = [], [], [], []
        for g in range(NG):
            Bg = xbc[:, NH * HP + g * NS: NH * HP + (g + 1) * NS]
            Cg = xbc[:, NH * HP + NG * NS + g * NS: NH * HP + NG * NS + (g + 1) * NS]
            gs = slice(g * GW, (g + 1) * GW)
            hg = hp_ref[0, gs, :]
            dhn = dh[gs, :]
            G = _dot(Cg, Bg, "nt")
            yoff_parts.append(ecsE[:, gs] * _dot(Cg, hg, "nt"))
            dC = _dot(dzo[:, gs], hg)
            dhp = _dot(dzo[:, gs], Cg, "tn") + eLcol[gs, :] * dhn
            t1 = jnp.sum(dhn * hg, axis=1, keepdims=True) * eLcol[gs, :]
            dcsL = dcsL + jnp.sum(et[gs, :] * t1, axis=0, keepdims=True)
            dxw_parts.append(_dot(Bg, dhn, "nt"))
            dB = _dot(xw[:, gs], dhn)
            dgsum = jnp.zeros((CH, CH), F32)
            for hh in range(NH // NG):
                h = g * (NH // NG) + hh
                hs = slice(h * HP, (h + 1) * HP)
                m = jnp.where(tril, jnp.exp(jnp.where(tril, cs[:, h:h + 1] - csT[h:h + 1, :], 0.0)), 0.0)
                sc = G * m
                dyh = dy[:, hs]
                dxd[:, hs] = _dot(sc, dyh, "tn")
                dsc = _dot(dyh, xd[:, hs], "nt")
                q = dsc * sc
                oh = (lax.broadcasted_iota(jnp.int32, (CH, 128), 1) == h).astype(F32)
                dcs = dcs + _dot(q, oh, exact="b") - _dot(q, oh, "tn", exact="b")
                dgsum = dgsum + dsc * m
            dC_parts.append(dC + _dot(dgsum, Bg))
            dB_parts.append(dB + _dot(dgsum, Cg, "tn"))
            dh[gs, :] = dhp
        yoff = jnp.concatenate(yoff_parts, axis=1)
        dxw = jnp.concatenate(dxw_parts, axis=1)
        dxdv = dxd[...]
        dcs = dcs + _dot(dy * yoff, et, exact="b")
        dxs = dxs + wE * dxw + dtE * dxdv
        dw = _dot(dxw * xs, et, exact="b")
        ddt = ddt + dw * jnp.exp(csL - cs) + _dot(dxdv * xs, et, exact="b")
        dcs = dcs - dw * wdec
        dcsL = dcsL + jnp.sum(dw * wdec, axis=0, keepdims=True)
        last = lax.broadcasted_iota(jnp.int32, (CH, 128), 0) == CH - 1
        dcs = dcs + jnp.where(last, dcsL, 0.0)
        da = _dot(triu, dcs, exact="a", passes=3)
        ddt = ddt + da * A
        gsc_ref[1:2, :] += jnp.sum(da * dt, axis=0, keepdims=True) * A
        valid = lax.broadcasted_iota(jnp.int32, (CH, 128), 1) < NH
        ddtr = jnp.where(valid, ddt * _sigmoid(dtr + dtb), 0.0)
        gsc_ref[0:1, :] += jnp.sum(ddtr, axis=0, keepdims=True)
        ddtr_ref[...] = ddtr.astype(ddtr_ref.dtype)
        dxbc = jnp.concatenate([dxs] + dB_parts + dC_parts, axis=1)
        dx_ref[...] = dxbc * _dsilu(xbc_c)
        exchange.finish()

    vec = pl.BlockSpec((8, 128), lambda c: (0, 0))
    vecd = pl.BlockSpec((1, D), lambda c: (0, 0))
    row = lambda w, j=0: pl.BlockSpec((CH, w), lambda c: (rev(c), j))
    outs = pl.pallas_call(
        body, name=name, grid=(nc,),
        in_specs=[row(D), row(D), row(D, COL_Z // D), row(XBC), row(128),
                  pl.BlockSpec((1, NH * HP, NS), lambda c: (rev(c), 0, 0)), vec, vecd] + [HBM_SPEC] * nco,
        out_specs=[row(D), row(XBC), row(128), vec, vecd] + [HBM_SPEC] * nco,
        out_shape=[SDS((S, D), _MXU), SDS((S, XBC), F32), SDS((S, 128), _MXU), SDS((8, 128), F32), SDS((1, D), F32)]
        + [SDS(p.shape, p.dtype) for p in comm],
        scratch_shapes=[pltpu.VMEM((NH * HP, NS), F32), pltpu.VMEM((CH, NH * HP), F32)] + _RidingExchange.scratch(nco),
        compiler_params=_cp("arbitrary"))(dmix, y, proj, xbc_c, dtr, hprev, sc, norm_g, *comm)
    return outs[:5], outs[5:]


CONV_RT = 32


def _fill_phases(ext, ph, rows):
    for s in range(1, 8):
        ph[s - 1, pl.ds(0, rows), :] = ext[pl.ds(s, rows), :]


def _window(ext, ph, off, r0, ls):
    s = off % 8
    src = ext if s == 0 else ph.at[s - 1]
    return src[pl.ds(pl.multiple_of(off - s + r0, 8), CONV_RT), ls]


def _cf_fwd(proj, w, b, lg, lb, comm=(), name="cf_fwd", tb_cap=256):
    S = proj.shape[0]
    tb = _tile(S, tb_cap, 8)
    nb = S // tb
    nco = len(comm)

    def body(*refs):
        a_ref, g_ref, w_ref, b_ref, lg_ref, lb_ref = refs[:6]
        u1_ref, u_ref = refs[6 + nco:8 + nco]
        ext, ph = refs[8 + 2 * nco:10 + 2 * nco]
        gather = _RidingGather(refs[6:6 + nco], refs[8 + nco:8 + 2 * nco], refs[10 + 2 * nco:], nb,
                               [a.shape[0] for a in comm])
        gather.start()

        @pl.when(pl.program_id(0) == 0)
        def _():
            ext[pl.ds(0, CF_HALO), :] = jnp.zeros((CF_HALO, D), F32)

        ext[pl.ds(CF_HALO, tb), :] = a_ref[...] * _sigmoid(g_ref[...])
        _fill_phases(ext, ph, tb + CF_HALO - 8)

        def tile(i, carry):
            r0 = pl.multiple_of(i * CONV_RT, CONV_RT)
            for l in range(D // 128):
                ls = pl.ds(l * 128, 128)
                acc = jnp.broadcast_to(b_ref[:, ls], (CONV_RT, 128))
                for k in range(KC):
                    acc = acc + _window(ext, ph, CF_HALO - (KC - 1) + k, r0, ls) * w_ref[k:k + 1, ls]
                u1_ref[pl.ds(r0, CONV_RT), ls] = acc
            return carry

        lax.fori_loop(0, tb // CONV_RT, tile, 0)
        acc = u1_ref[...]
        mu = jnp.mean(acc, axis=-1, keepdims=True)
        xc = acc - mu
        r = lax.rsqrt(jnp.mean(xc * xc, axis=-1, keepdims=True) + EPS)
        u_ref[...] = _silu(xc * r * lg_ref[...] + lb_ref[...]).astype(u_ref.dtype)
        ext[pl.ds(0, CF_HALO), :] = ext[pl.ds(tb, CF_HALO), :]
        gather.finish()

    vec = pl.BlockSpec((1, D), lambda i: (0, 0))
    outs = pl.pallas_call(
        body, name=name, grid=(nb,),
        in_specs=[pl.BlockSpec((tb, D), lambda i: (i, COL_A // D)), pl.BlockSpec((tb, D), lambda i: (i, COL_G // D)),
                  pl.BlockSpec((KC, D), lambda i: (0, 0)), vec, vec, vec] + [HBM_SPEC] * nco,
        out_specs=[pl.BlockSpec((tb, D), lambda i: (i, 0)), pl.BlockSpec((tb, D), lambda i: (i, 0))] + [HBM_SPEC] * nco,
        out_shape=[SDS((S, D), F32), SDS((S, D), _MXU)] + _gather_shapes(comm),
        scratch_shapes=[pltpu.VMEM((CF_HALO + tb, D), F32), pltpu.VMEM((7, tb + CF_HALO - 8, D), F32)]
        + _RidingExchange.scratch(nco),
        compiler_params=_cp("arbitrary"))(proj, proj, w, b, lg, lb, *comm)
    return outs[:2], outs[2:]


def _cf_bwd(dmix, u1, proj, w, lg, lb, comm=(), name="cf_bwd", tb_cap=256):
    S = proj.shape[0]
    tb = _tile(S, tb_cap, 8)
    nb = S // tb
    nco = len(comm)
    rev = lambda i: nb - 1 - i

    def body(*refs):
        du_ref, u1_ref, a_ref, g_ref, w_ref, lg_ref, lb_ref = refs[:7]
        da_ref, dg_ref, dw_ref, db_ref, dlg_ref, dlb_ref = refs[7 + nco:13 + nco]
        ext, ph, u0s = refs[13 + 2 * nco:16 + 2 * nco]
        exchange = _RidingExchange(refs[7:7 + nco], refs[13 + nco:13 + 2 * nco], refs[16 + 2 * nco:], nb)
        exchange.start()

        @pl.when(pl.program_id(0) == 0)
        def _():
            ext[pl.ds(tb, CF_HALO), :] = jnp.zeros((CF_HALO, D), F32)
            dw_ref[...] = jnp.zeros_like(dw_ref)
            db_ref[...] = jnp.zeros_like(db_ref)
            dlg_ref[...] = jnp.zeros_like(dlg_ref)
            dlb_ref[...] = jnp.zeros_like(dlb_ref)

        u1 = u1_ref[...]
        mu = jnp.mean(u1, axis=-1, keepdims=True)
        xc = u1 - mu
        r = lax.rsqrt(jnp.mean(xc * xc, axis=-1, keepdims=True) + EPS)
        xh = xc * r
        lgv = lg_ref[...]
        du2 = du_ref[...] * _dsilu(xh * lgv + lb_ref[...])
        dlg_ref[...] += jnp.sum(du2 * xh, axis=0, keepdims=True)
        dlb_ref[...] += jnp.sum(du2, axis=0, keepdims=True)
        gd = du2 * lgv
        du1 = r * (gd - jnp.mean(gd, axis=-1, keepdims=True) - xh * jnp.mean(gd * xh, axis=-1, keepdims=True))
        db_ref[...] += jnp.sum(du1, axis=0, keepdims=True)
        ext[pl.ds(0, tb), :] = du1
        u0s[...] = a_ref[...] * _sigmoid(g_ref[...])
        _fill_phases(ext, ph, tb + CF_HALO - 8)

        def dx_tile(i, carry):
            r0 = pl.multiple_of(i * CONV_RT, CONV_RT)
            rows = pl.ds(r0, CONV_RT)
            for l in range(D // 128):
                ls = pl.ds(l * 128, 128)
                acc = jnp.zeros((CONV_RT, 128), F32)
                for k in range(KC):
                    acc = acc + _window(ext, ph, KC - 1 - k, r0, ls) * w_ref[k:k + 1, ls]
                sg = _sigmoid(g_ref[rows, ls])
                da_ref[rows, ls] = (acc * sg).astype(da_ref.dtype)
                dg_ref[rows, ls] = (acc * a_ref[rows, ls] * sg * (1.0 - sg)).astype(dg_ref.dtype)
            return carry

        lax.fori_loop(0, tb // CONV_RT, dx_tile, 0)
        for l in range(D // 128):
            ls = pl.ds(l * 128, 128)

            def dw_tile(i, accs, ls=ls):
                r0 = pl.multiple_of(i * CONV_RT, CONV_RT)
                u0t = u0s[pl.ds(r0, CONV_RT), ls]
                out = []
                for k in range(KC):
                    p = u0t * _window(ext, ph, KC - 1 - k, r0, ls)
                    out.append(accs[k] + ((p[0:8] + p[8:16]) + (p[16:24] + p[24:32])))
                return tuple(out)

            accs = lax.fori_loop(0, tb // CONV_RT, dw_tile, tuple(jnp.zeros((8, 128), F32) for _ in range(KC)))
            for k in range(KC):
                dw_ref[k:k + 1, ls] += jnp.sum(accs[k], axis=0, keepdims=True)
        ext[pl.ds(tb, CF_HALO), :] = ext[pl.ds(0, CF_HALO), :]
        exchange.finish()

    vec = pl.BlockSpec((1, D), lambda i: (0, 0))
    wsp = pl.BlockSpec((KC, D), lambda i: (0, 0))
    row = lambda j=0: pl.BlockSpec((tb, D), lambda i: (rev(i), j))
    outs = pl.pallas_call(
        body, name=name, grid=(nb,),
        in_specs=[row(1), row(), row(COL_A // D), row(COL_G // D), wsp, vec, vec] + [HBM_SPEC] * nco,
        out_specs=[row(), row(), wsp, vec, vec, vec] + [HBM_SPEC] * nco,
        out_shape=[SDS((S, D), _MXU), SDS((S, D), _MXU), SDS((KC, D), F32),
                   SDS((1, D), F32), SDS((1, D), F32), SDS((1, D), F32)] + [SDS(p.shape, p.dtype) for p in comm],
        scratch_shapes=[pltpu.VMEM((tb + CF_HALO, D), F32), pltpu.VMEM((7, tb + CF_HALO - 8, D), F32),
                        pltpu.VMEM((tb, D), F32)] + _RidingExchange.scratch(nco),
        compiler_params=_cp("arbitrary"))(dmix, u1, proj, proj, w, lg, lb, *comm)
    return outs[:6], outs[6:]


def _attn_fwd(q, kv, name="attn_fwd", tq_cap=512):
    S = q.shape[0]
    tq = _tile(S, tq_cap, 8)
    scale = XD ** -0.5

    def body(q_ref, kv_ref, o_ref):
        for h in range(XH):
            hs = slice(h * XD, (h + 1) * XD)
            s = _dot(q_ref[:, hs], kv_ref[:, hs], "nt") * scale
            s = s - jnp.max(s, axis=-1, keepdims=True)
            p = jnp.exp(s)
            p = p / jnp.sum(p, axis=-1, keepdims=True)
            o_ref[:, hs] = _dot(p, kv_ref[:, D + h * XD: D + (h + 1) * XD]).astype(o_ref.dtype)

    return pl.pallas_call(
        body, name=name, grid=(S // tq,),
        in_specs=[pl.BlockSpec((tq, D), lambda i: (i, 0)), pl.BlockSpec((MEM, 2 * D), lambda i: (0, 0))],
        out_specs=pl.BlockSpec((tq, D), lambda i: (i, 0)), out_shape=SDS((S, D), _MXU),
        compiler_params=_cp("parallel"))(q, kv)


def _attn_bwd(do, q, kv, name="attn_bwd", tq_cap=512):
    S = q.shape[0]
    tq = _tile(S, tq_cap, 8)
    scale = XD ** -0.5

    def body(do_ref, q_ref, kv_ref, dq_ref, dkv_ref):
        @pl.when(pl.program_id(0) == 0)
        def _():
            dkv_ref[...] = jnp.zeros_like(dkv_ref)

        for h in range(XH):
            hs = slice(h * XD, (h + 1) * XD)
            vs = slice(D + h * XD, D + (h + 1) * XD)
            qh = q_ref[:, hs]
            kh = kv_ref[:, hs]
            s = _dot(qh, kh, "nt") * scale
            s = s - jnp.max(s, axis=-1, keepdims=True)
            p = jnp.exp(s)
            p = p / jnp.sum(p, axis=-1, keepdims=True)
            doh = do_ref[:, hs]
            dp = _dot(doh, kv_ref[:, vs], "nt")
            ds = p * (dp - jnp.sum(dp * p, axis=-1, keepdims=True)) * scale
            dq_ref[:, hs] = _dot(ds, kh).astype(dq_ref.dtype)
            dkv_ref[:, hs] += _dot(ds, qh, "tn")
            dkv_ref[:, vs] += _dot(p, doh, "tn")

    return pl.pallas_call(
        body, name=name, grid=(S // tq,),
        in_specs=[pl.BlockSpec((tq, D), lambda i: (i, 0)), pl.BlockSpec((tq, D), lambda i: (i, 0)),
                  pl.BlockSpec((MEM, 2 * D), lambda i: (0, 0))],
        out_specs=[pl.BlockSpec((tq, D), lambda i: (i, 0)), pl.BlockSpec((MEM, 2 * D), lambda i: (0, 0))],
        out_shape=[SDS((S, D), _MXU), SDS((MEM, 2 * D), F32)],
        compiler_params=_cp("arbitrary"))(do, q, kv)


def _ffn_act(gu, name="ffn_act", tb_cap=128):
    S = gu.shape[0]
    tb = _tile(S, tb_cap, 8)

    def body(g_ref, u_ref, o_ref):
        o_ref[...] = (_silu(g_ref[...]) * u_ref[...]).astype(o_ref.dtype)

    return pl.pallas_call(
        body, name=name, grid=(S // tb,),
        in_specs=[pl.BlockSpec((tb, DFF), lambda i: (i, 0)), pl.BlockSpec((tb, DFF), lambda i: (i, 1))],
        out_specs=pl.BlockSpec((tb, DFF), lambda i: (i, 0)), out_shape=SDS((S, DFF), _MXU),
        compiler_params=_cp("parallel"))(gu, gu)


def _ffn_act_bwd(dact, gu, name="ffn_act_bwd", tb_cap=128):
    S = gu.shape[0]
    tb = _tile(S, tb_cap, 8)

    def body(d_ref, g_ref, u_ref, dg_ref, du_ref):
        gt = g_ref[...]
        d = d_ref[...]
        s = _sigmoid(gt)
        dg_ref[...] = (d * u_ref[...] * (s * (1.0 + gt * (1.0 - s)))).astype(dg_ref.dtype)
        du_ref[...] = (d * gt * s).astype(du_ref.dtype)

    blk = pl.BlockSpec((tb, DFF), lambda i: (i, 0))
    return pl.pallas_call(
        body, name=name, grid=(S // tb,),
        in_specs=[blk, blk, pl.BlockSpec((tb, DFF), lambda i: (i, 1))],
        out_specs=[blk, blk], out_shape=[SDS((S, DFF), _MXU), SDS((S, DFF), _MXU)],
        compiler_params=_cp("parallel"))(dact, gu, gu)


AG_RIDE = (("w_out", "w_q", "w_kv", "w_o"), ("w_gate", "w_up", "w_down"))


def _local_step(x, mem, tgt, W, P, core=None, late=None):
    pair, got = {}, {}
    ride = [[late[n] for n in grp] if late is not None else [] for grp in AG_RIDE]

    def rs_pair(group):
        if core is None:
            return []
        ps = _rs_pair(group, GW, core)
        pair.update(zip(group, ps))
        return ps
    h = _rms_fwd(x, P["g_mix"], "rms_mix")
    proj = _mm_nn(h, W["main"], "in_proj", tn_cap=1152)
    dtr = _mm_nn(h, W["dt"], "in_proj_dt")
    xbc_c = _ssd_conv_fwd(proj, P["conv4_w"], P["conv4_b"])
    (y, yn, hprev), bufs0 = _ssd_fwd(proj, xbc_c, dtr, P["sc"], P["ssd_norm_g"], comm=ride[0])
    (u1, u), bufs1 = _cf_fwd(proj, P["cf_w"], P["cf_b"], P["ln_g"], P["ln_b"], comm=ride[1])
    if late is not None:
        names = AG_RIDE[0] + AG_RIDE[1]
        W = dict(W, **_pack_late(dict(zip(names, _gather_finish_list(ride[0] + ride[1], list(bufs0) + list(bufs1))))))
    mix = jnp.concatenate([yn, u], axis=1)
    x1 = _mm_nn(mix, W["out"], "out_proj", add=x)
    hq = _rms_fwd(x1, P["g_xattn"], "rms_xattn")
    q = _mm_nn(hq, W["q"], "q_proj")
    mn = _rms_fwd(mem, P["g_mem"], "rms_mem")
    kv = _mm_nn(mn, W["kv"], "kv_proj")
    o = _attn_fwd(q, kv)
    x2 = _mm_nn(o, W["o"], "o_proj", add=x1)
    hf = _rms_fwd(x2, P["g_ffn"], "rms_ffn")
    gu = _mm_nn(hf, W["gu"], "ffn_in")
    act = _ffn_act(gu)
    x3 = _mm_nn(act, W["down"], "ffn_out", add=x2)
    loss, dx3, dx3b, g_final = _final_loss(x3, P["g_final"], tgt)
    GW, GP = {}, {"g_final": g_final}
    dact = _mm_nt(dx3b, W["down"], "ffn_out_dx", tk_cap=1408)
    GW["down"] = _mm_tn(act, dx3b, "ffn_out_dw", tk_cap=1408, tn_cap=1024)
    dgt, dup = _ffn_act_bwd(dact, gu)
    dhf = _mm_nt(dgt, W["gu"], "ffn_gate_dx", b_col=0)
    dhf = _mm_nt(dup, W["gu"], "ffn_up_dx", b_col=1, add=dhf)
    GW["gate"] = _mm_tn(hf, dgt, "ffn_gate_dw")
    GW["up"] = _mm_tn(hf, dup, "ffn_up_dw")
    ffn_pieces = rs_pair(RS_GROUPS[0])
    dx2, dx2b, GP["g_ffn"] = _rms_bwd(x2, P["g_ffn"], dhf, dx3, "rms_ffn_bwd")
    do = _mm_nt(dx2b, W["o"], "o_proj_dx")
    GW["o"] = _mm_tn(o, dx2b, "o_proj_dw")
    dq, dkv = _attn_bwd(do, q, kv)
    dhq = _mm_nt(dq, W["q"], "q_proj_dx")
    GW["q"] = _mm_tn(hq, dq, "q_proj_dw")
    dkvb = dkv.astype(_MXU)
    GW["kv"] = _mm_tn(mn, dkvb, "kv_proj_dw", tm_cap=256)
    dmn = _mm_nt(dkvb, W["kv"], "kv_proj_dx")
    GP["g_mem"] = _rms_bwd(mem, P["g_mem"], dmn, None, "rms_mem_bwd")
    dx1, dx1b, GP["g_xattn"] = _rms_bwd(x1, P["g_xattn"], dhq, dx2, "rms_xattn_bwd")
    dmix = _mm_nt(dx1b, W["out"], "out_proj_dx")
    GW["out"] = _mm_tn(mix, dx1b, "out_proj_dw", tn_cap=1024)
    attn_pieces = rs_pair(RS_GROUPS[1])
    (da, dg, GP["cf_w"], GP["cf_b"], GP["ln_g"], GP["ln_b"]), came = _cf_bwd(
        dmix, u1, proj, P["cf_w"], P["ln_g"], P["ln_b"], comm=ffn_pieces)
    got.update(zip(RS_GROUPS[0], came))
    (dz, dxbc_c, ddtr, GP["sc"], GP["ssd_norm_g"]), came = _ssd_bwd(
        dmix, y, proj, xbc_c, dtr, hprev, P["sc"], P["ssd_norm_g"], comm=attn_pieces)
    got.update(zip(RS_GROUPS[1], came))
    dxbc, GP["conv4_w"], GP["conv4_b"] = _ssd_conv_bwd(dxbc_c, proj, P["conv4_w"])
    dproj = jnp.concatenate([dz, da, dg, dxbc], axis=1)
    dh = _mm_nt(ddtr, W["dt"], "in_proj_dt_dx")
    dh = _mm_nt(dproj, W["main"], "in_proj_dx", add=dh, tk_cap=512)
    GW["main"] = _mm_tn(h, dproj, "in_proj_dw", tn_cap=1152)
    GW["dt"] = _mm_tn(h, ddtr, "in_proj_dt_dw")
    in_pieces = rs_pair(RS_GROUPS[2])
    if in_pieces:
        got.update(zip(RS_GROUPS[2], _scatter_list(in_pieces)))
    grad_x, GP["g_mix"] = _rms_bwd(x, P["g_mix"], dh, dx1, "rms_mix_bwd", low=False)
    if core is None:
        return loss, grad_x, GW, GP
    return loss, grad_x, GW, GP, pair, got


Z_END, XBC_END, DT_END = NH * HP, NH * HP + XBC, NH * HP + XBC + NH
NFB = DFF // FB


def _pad_to(a, rows=None, cols=None):
    r = 0 if rows is None else rows - a.shape[0]
    c = 0 if cols is None else cols - a.shape[1]
    return jnp.pad(a, ((0, r), (0, c)))


IN_W = DT_END + 2 * D
W_IN_SEGS = [(0, Z_END, "main", COL_Z), (Z_END, XBC_END, "main", COL_XBC), (XBC_END, DT_END, "dt", 0),
             (DT_END, DT_END + D, "main", COL_A), (DT_END + D, IN_W, "main", COL_G)]
BIG = [("w_in", True), ("w_out", False), ("w_q", False), ("w_kv", True), ("w_o", False), ("w_gate", True),
       ("w_up", True), ("w_down", False)]


def _ref_cols(pieces, a, b):
    cw = IN_W // 4
    out = []
    for j in range(4):
        lo, hi = max(a, j * cw), min(b, (j + 1) * cw)
        if lo < hi:
            out.append(pieces[j][:, lo - j * cw:hi - j * cw])
    return out


def _cat_cols(pieces):
    return jnp.concatenate([pieces[j] for j in range(4)], axis=1)


def _pack_in(w_in):
    main = jnp.concatenate(_ref_cols(w_in, 0, Z_END) + _ref_cols(w_in, DT_END, IN_W) + _ref_cols(w_in, Z_END, XBC_END), axis=1)
    return {"main": main, "dt": _pad_to(jnp.concatenate(_ref_cols(w_in, XBC_END, DT_END), axis=1), cols=128)}


def _pack_late(pc):
    gu = jnp.concatenate([pc["w_gate"][j] for j in range(4)] + [pc["w_up"][j] for j in range(4)], axis=1)
    rows = lambda n: pc[n].reshape(-1, pc[n].shape[-1])
    return {"out": rows("w_out"), "q": rows("w_q"), "kv": _cat_cols(pc["w_kv"]), "o": rows("w_o"), "gu": gu,
            "down": rows("w_down")}


GW_KEY = {"w_gate": "gate", "w_up": "up", "w_kv": "kv", "w_out": "out", "w_q": "q", "w_o": "o", "w_down": "down"}
RS_GROUPS = (("w_down", "w_gate", "w_up"), ("w_out", "w_q", "w_kv", "w_o"), ("w_in",))


def _shard_grad(name, GW):
    if name == "w_in":
        cw = IN_W // 4
        pieces = []
        for j in range(4):
            parts = []
            for a, b, src, col in W_IN_SEGS:
                lo, hi = max(a, j * cw), min(b, (j + 1) * cw)
                if lo < hi:
                    parts.append(GW[src][:, col + lo - a:col + hi - a])
            pieces.append(jnp.concatenate(parts, axis=1))
        return jnp.stack(pieces)
    g = GW[GW_KEY[name]]
    if dict(BIG)[name]:
        cw = g.shape[1] // 4
        return jnp.stack([g[:, j * cw:(j + 1) * cw] for j in range(4)])
    return g.reshape(4, g.shape[0] // 4, g.shape[1])


def _rs_pair(names, GW, core):
    gs = [_shard_grad(n, GW) for n in names]
    halves = [g.reshape(4, 2, g.shape[1] // 2, g.shape[2]) for g in gs]
    theirs = _pair_split_list(halves, "rs_pair_send_" + names[0])
    return [_pair_sum(h, t, core, "rs_pair_sum_" + n) for h, t, n in zip(halves, theirs, names)]


def _stack_sc(dt_bias, a_log, d):
    return _pad_to(jnp.concatenate([dt_bias, a_log, d], axis=0), rows=8, cols=128)


COMM_PARAMS = pltpu.CompilerParams(vmem_limit_bytes=VMEM_LIMIT)


def _dma_sems(*counts):
    return [pltpu.SemaphoreType.DMA((n,)) for n in counts]


def _allgather_list(arrs, name):
    n = len(arrs)
    halved = [a.shape[0] % 16 == 0 for a in arrs]
    oshape = [(4, 2, a.shape[0] // 2, a.shape[1]) if h else (4, 1) + a.shape for a, h in zip(arrs, halved)]

    def body(*refs):
        srcs, outs = refs[:n], refs[n:2 * n]
        ici_send, ici_recv, own_send, own_recv, fwd_send, fwd_recv = refs[2 * n:]
        x, y, c = lax.axis_index("x"), lax.axis_index("y"), lax.axis_index("c")
        me = 2 * x + y
        sib = (x, y, 1 - c)
        peers = _chip_peers(x, y)

        def half(i, h):
            r = arrs[i].shape[0] // 2
            if not halved[i]:
                return srcs[i]
            return srcs[i].at[pl.ds(h * r if isinstance(h, int) else pl.multiple_of(h * r, 8), r)]

        ici, own, fwd = [], [], []
        for i in range(n):
            mine_h = c if halved[i] else 0
            for k, (px, py) in enumerate(peers):
                s = 3 * i + k
                ici.append(_remote(half(i, c), outs[i].at[me, mine_h], ici_send.at[s], ici_recv.at[s], (px, py, c)))
            for h in range(2 if halved[i] else 1):
                s = 2 * i + h
                own.append(_remote(half(i, h), outs[i].at[me, h], own_send.at[s], own_recv.at[s], sib))
        for cp in ici + own:
            cp.start()
        for i in range(n):
            if not halved[i]:
                continue
            for k, (px, py) in enumerate(peers):
                s = 3 * i + k
                got = outs[i].at[2 * px + py, c]
                _remote(half(i, c), got, ici_send.at[s], ici_recv.at[s], (px, py, c)).wait_recv()
                f = _remote(got, got, fwd_send.at[s], fwd_recv.at[s], sib)
                f.start()
                fwd.append(f)
        for i in range(n):
            for k, (px, py) in enumerate(peers):
                s = 3 * i + k
                if halved[i]:
                    _remote(half(i, c), outs[i].at[2 * px + py, 1 - c], fwd_send.at[s], fwd_recv.at[s], sib).wait_recv()
                else:
                    _remote(srcs[i], outs[i].at[2 * px + py, 0], ici_send.at[s], ici_recv.at[s], (px, py, c)).wait_recv()
            for h in range(2 if halved[i] else 1):
                s = 2 * i + h
                _remote(half(i, h), outs[i].at[me, h], own_send.at[s], own_recv.at[s], sib).wait_recv()
        for cp in ici + own + fwd:
            cp.wait_send()

    outs = pl.pallas_call(
        body, name=name, in_specs=[HBM_SPEC] * n, out_specs=[HBM_SPEC] * n,
        out_shape=[SDS(s, a.dtype) for s, a in zip(oshape, arrs)],
        scratch_shapes=_dma_sems(3 * n, 3 * n, 2 * n, 2 * n, 3 * n, 3 * n), compiler_params=COMM_PARAMS)(*arrs)
    return [o.reshape((4,) + a.shape) for o, a in zip(outs, arrs)]


def _pair_split_list(gs, name):
    n = len(gs)

    def body(*refs):
        srcs, outs = refs[:n], refs[n:2 * n]
        send_sems, recv_sems = refs[2 * n:]
        x, y, c = lax.axis_index("x"), lax.axis_index("y"), lax.axis_index("c")
        sib = (x, y, 1 - c)
        sends = [_remote(srcs[i].at[j, 1 - c], outs[i].at[j], send_sems.at[4 * i + j], recv_sems.at[4 * i + j], sib)
                 for i in range(n) for j in range(4)]
        for cp in sends:
            cp.start()
        for cp in sends:
            cp.wait_recv()
        for cp in sends:
            cp.wait_send()

    return pl.pallas_call(
        body, name=name, in_specs=[HBM_SPEC] * n, out_specs=[HBM_SPEC] * n,
        out_shape=[SDS((4,) + g.shape[2:], g.dtype) for g in gs],
        scratch_shapes=_dma_sems(4 * n, 4 * n), compiler_params=COMM_PARAMS)(*gs)


def _scatter_list(ps, name="rs_chip_send"):
    n = len(ps)

    def body(*refs):
        sends, recvs = _scatter_copies(refs[:n], refs[n:2 * n], *refs[2 * n:])
        for cp in sends:
            cp.start()
        for cp in recvs:
            cp.wait_recv()
        for cp in sends:
            cp.wait_send()

    return pl.pallas_call(
        body, name=name, in_specs=[HBM_SPEC] * n, out_specs=[HBM_SPEC] * n,
        out_shape=[SDS(p.shape, p.dtype) for p in ps],
        scratch_shapes=_dma_sems(3 * n, 3 * n), compiler_params=COMM_PARAMS)(*ps)


def _gather_finish_list(shards, bufs, name="allgather_finish"):
    n = len(shards)

    def body(*refs):
        srcs, outs = refs[:n], refs[2 * n:3 * n]
        own_send, own_recv, fwd_send, fwd_recv = refs[3 * n:]
        x, y, c = lax.axis_index("x"), lax.axis_index("y"), lax.axis_index("c")
        me = 2 * x + y
        sib = (x, y, 1 - c)
        sends, recvs = [], []
        for i in range(n):
            for h in range(2):
                own = _remote(_rows_half(srcs[i], shards[i].shape[0], h), outs[i].at[me, h],
                              own_send.at[2 * i + h], own_recv.at[2 * i + h], sib)
                sends.append(own)
                recvs.append(own)
            for k, (px, py) in enumerate(_chip_peers(x, y)):
                got, s = outs[i].at[2 * px + py, c], 3 * i + k
                sends.append(_remote(got, got, fwd_send.at[s], fwd_recv.at[s], sib))
                recvs.append(_remote(got, outs[i].at[2 * px + py, 1 - c], fwd_send.at[s], fwd_recv.at[s], sib))
        for cp in sends:
            cp.start()
        for cp in recvs:
            cp.wait_recv()
        for cp in sends:
            cp.wait_send()

    outs = pl.pallas_call(
        body, name=name, in_specs=[HBM_SPEC] * (2 * n), out_specs=[HBM_SPEC] * n,
        out_shape=[SDS(b.shape, b.dtype) for b in bufs], input_output_aliases={n + i: i for i in range(n)},
        scratch_shapes=_dma_sems(2 * n, 2 * n, 3 * n, 3 * n), compiler_params=COMM_PARAMS)(*shards, *bufs)
    return [o.reshape((4,) + a.shape) for o, a in zip(outs, shards)]


JOIN_SPLIT = 4


def _pair_join_list(bufs, name="rs_pair_join"):
    n = len(bufs)

    def body(*refs):
        outs = refs[n:2 * n]
        send_sems, recv_sems = refs[2 * n:]
        x, y, c = lax.axis_index("x"), lax.axis_index("y"), lax.axis_index("c")
        sib = (x, y, 1 - c)
        sends, recvs = [], []
        for i in range(n):
            rc = bufs[i].shape[1] // JOIN_SPLIT
            for q in range(JOIN_SPLIT):
                k = JOIN_SPLIT * i + q
                rows = pl.ds(q * rc, rc)
                sends.append(_remote(outs[i].at[c, rows], outs[i].at[c, rows], send_sems.at[k], recv_sems.at[k], sib))
                recvs.append(_remote(outs[i].at[c, rows], outs[i].at[1 - c, rows], send_sems.at[k], recv_sems.at[k], sib))
        for cp in sends:
            cp.start()
        for cp in recvs:
            cp.wait_recv()
        for cp in sends:
            cp.wait_send()

    return pl.pallas_call(
        body, name=name, in_specs=[HBM_SPEC] * n, out_specs=[HBM_SPEC] * n,
        out_shape=[SDS(b.shape, b.dtype) for b in bufs], input_output_aliases={i: i for i in range(n)},
        scratch_shapes=_dma_sems(JOIN_SPLIT * n, JOIN_SPLIT * n), compiler_params=COMM_PARAMS)(*bufs)


def _pair_sum(g, theirs, core, name):
    _, _, r, c = g.shape

    def body(core_ref, g_ref, t_ref, o_ref):
        o_ref[...] = (g_ref[...] + t_ref[...]).astype(o_ref.dtype)

    spec = pltpu.PrefetchScalarGridSpec(
        num_scalar_prefetch=1, grid=(4,),
        in_specs=[pl.BlockSpec((None, None, r, c), lambda j, core_ref: (j, core_ref[0], 0, 0)),
                  pl.BlockSpec((None, r, c), lambda j, core_ref: (j, 0, 0))],
        out_specs=pl.BlockSpec((None, r, c), lambda j, core_ref: (j, 0, 0)))
    return pl.pallas_call(body, name=name, grid_spec=spec, out_shape=SDS((4, r, c), BF16),
                          compiler_params=_cp("parallel"))(core, g, theirs)


def _chip_sum(own, got, where, name):
    _, r, c = own.shape
    tr = r // 2

    def body(w_ref, a_ref, b1_ref, b2_ref, b3_ref, o_ref):
        o_ref[...] = ((a_ref[...].astype(F32) + b1_ref[...].astype(F32)) + b2_ref[...].astype(F32)) + b3_ref[...].astype(F32)

    piece = lambda k: pl.BlockSpec((None, tr, c), lambda i, w_ref: ((w_ref[0] + k) % 4, i, 0))
    spec = pltpu.PrefetchScalarGridSpec(
        num_scalar_prefetch=1, grid=(r // tr,), in_specs=[piece(0), piece(1), piece(2), piece(3)],
        out_specs=pl.BlockSpec((None, tr, c), lambda i, w_ref: (w_ref[1], i, 0)))
    return pl.pallas_call(body, name=name, grid_spec=spec, out_shape=SDS((2, r, c), F32),
                          compiler_params=_cp("parallel"))(where, own, got, got, got)


def _adam_math(w, g, m, v):
    bc1 = 1.0 - ADAM_B1 ** ADAM_STEP
    bc2 = 1.0 - ADAM_B2 ** ADAM_STEP
    mn = ADAM_B1 * m + (1.0 - ADAM_B1) * g
    vn = ADAM_B2 * v + (1.0 - ADAM_B2) * (g * g)
    return -ADAM_LR * ((mn / bc1) / (jnp.sqrt(vn / bc2) + ADAM_EPS) + ADAM_WD * w), mn, vn


PACK_COLS = XBC
PACK = {"g_mix": (0, 1, D), "g_xattn": (1, 1, D), "g_mem": (2, 1, D), "g_ffn": (3, 1, D), "g_final": (4, 1, D),
        "ssd_norm_g": (5, 1, D), "cf_b": (6, 1, D), "ln_g": (7, 1, D), "ln_b": (8, 1, D), "conv4_b": (9, 1, XBC),
        "conv4_w": (10, KS, XBC), "sc": (16, 8, 128), "cf_w": (24, KC, D), "loss": (55, 1, 128)}
PACK_ROWS = 56
SMALL_ADAM = ["g_mix", "g_xattn", "g_mem", "g_ffn", "g_final", "ssd_norm_g", "cf_b", "ln_g", "ln_b", "conv4_b", "sc"]


def _small_allreduce_adamw(grads, wts, mom, var, name="allreduce_small"):
    gk = list(PACK)
    ng, na = len(gk), len(SMALL_ADAM)

    def body(*refs):
        g_in = refs[:ng]
        w_in, m_in, v_in = (refs[ng + i * na: ng + (i + 1) * na] for i in range(3))
        o = refs[ng + 3 * na:]
        g_out = o[:ng]
        d_out, m_out, v_out = (o[ng + i * na: ng + (i + 1) * na] for i in range(3))
        pack, buf, acc, send_sems, recv_sems = o[ng + 3 * na:]
        x, y, c = lax.axis_index("x"), lax.axis_index("y"), lax.axis_index("c")
        me = 4 * x + 2 * y + c
        pack[...] = jnp.zeros_like(pack)
        for i, k in enumerate(gk):
            r0, nr, nc = PACK[k]
            pack[r0:r0 + nr, 0:nc] = g_in[i][...]
        peers = [(x, y, 1 - c)] + [(px, py, pc) for px, py in _chip_peers(x, y) for pc in (c, 1 - c)]
        sends = [_remote(pack, buf.at[me], send_sems.at[k], recv_sems.at[k], dev) for k, dev in enumerate(peers)]
        for cp in sends:
            cp.start()
        buf[me] = pack[...]
        for k, (px, py, pc) in enumerate(peers):
            _remote(pack, buf.at[4 * px + 2 * py + pc], send_sems.at[k], recv_sems.at[k], (px, py, pc)).wait_recv()
        for cp in sends:
            cp.wait_send()
        tot = buf[0]
        for i in range(1, 8):
            tot = tot + buf[i]
        acc[...] = tot
        for i, k in enumerate(gk):
            r0, nr, nc = PACK[k]
            g_out[i][...] = acc[r0:r0 + nr, 0:nc]
        for i, k in enumerate(SMALL_ADAM):
            r0, nr, nc = PACK[k]
            d_out[i][...], m_out[i][...], v_out[i][...] = _adam_math(
                w_in[i][...], acc[r0:r0 + nr, 0:nc], m_in[i][...], v_in[i][...])

    args = [grads[k] for k in gk] + [d[k] for d in (wts, mom, var) for k in SMALL_ADAM]
    shp = lambda k: SDS((PACK[k][1], PACK[k][2]), F32)
    vm = pl.BlockSpec(memory_space=pltpu.VMEM)
    outs = pl.pallas_call(
        body, name=name, in_specs=[vm] * len(args), out_specs=[vm] * (ng + 3 * na),
        out_shape=[shp(k) for k in gk] + [shp(k) for _ in range(3) for k in SMALL_ADAM],
        scratch_shapes=[pltpu.VMEM((PACK_ROWS, PACK_COLS), F32), pltpu.VMEM((8, PACK_ROWS, PACK_COLS), F32),
                        pltpu.VMEM((PACK_ROWS, PACK_COLS), F32)] + _dma_sems(7, 7),
        compiler_params=COMM_PARAMS)(*args)
    red = dict(zip(gk, outs[:ng]))
    parts = [dict(zip(SMALL_ADAM, outs[ng + i * na: ng + (i + 1) * na])) for i in range(3)]
    return red, parts[0], parts[1], parts[2]


def _adamw_cols(w, gfull, m, v, chip, name):
    _, R, C = w.shape

    def body(w_idx, w_ref, g_ref, m_ref, v_ref, go_ref, d_ref, mo_ref, vo_ref):
        go_ref[...] = g_ref[...]
        d_ref[...], mo_ref[...], vo_ref[...] = _adam_math(w_ref[...], g_ref[...], m_ref[...], v_ref[...])

    blk = pl.BlockSpec((None, R, C), lambda i, w_idx: (0, 0, 0))
    spec = pltpu.PrefetchScalarGridSpec(
        num_scalar_prefetch=1, grid=(1,),
        in_specs=[blk, pl.BlockSpec((R, C), lambda i, w_idx: (0, w_idx[0])), blk, blk], out_specs=[blk] * 4)
    return pl.pallas_call(body, name=name, grid_spec=spec, out_shape=[SDS((1, R, C), F32)] * 4,
                          compiler_params=_cp("arbitrary"))(chip, w, gfull, m, v)


def _adamw(w, g, m, v, name):
    _, R, C = w.shape
    half = R // 2
    tr = _tile(half, max(8, (2 ** 17 // C) // 8 * 8), 8)
    nh = half // tr

    def body(w_ref, g_ref, m_ref, v_ref, go_ref, d_ref, mo_ref, vo_ref):
        go_ref[...] = g_ref[...]
        d_ref[...], mo_ref[...], vo_ref[...] = _adam_math(w_ref[...], g_ref[...], m_ref[...], v_ref[...])

    blk = pl.BlockSpec((None, tr, C), lambda i: (0, i, 0))
    gblk = pl.BlockSpec((None, tr, C), lambda i: (i // nh, i % nh, 0))
    return pl.pallas_call(body, name=name, grid=(R // tr,), in_specs=[blk, gblk, blk, blk], out_specs=[blk] * 4,
                          out_shape=[SDS((1, R, C), F32)] * 4, compiler_params=_cp("parallel"))(w, g, m, v)


WEIGHT_NAMES = ["norm_mix_g", "w_in", "ssd_conv_w", "ssd_conv_b", "ssd_dt_bias", "ssd_A_log", "ssd_D", "ssd_norm_g",
                "cf_conv_w", "cf_conv_b", "cf_ln_g", "cf_ln_b", "w_out", "norm_xattn_g", "norm_mem_g", "w_q", "w_kv",
                "w_o", "norm_ffn_g", "w_gate", "w_up", "w_down", "norm_final_g"]
VEC_REF = [("norm_mix_g", "g_mix"), ("norm_xattn_g", "g_xattn"), ("norm_mem_g", "g_mem"), ("norm_ffn_g", "g_ffn"),
           ("norm_final_g", "g_final"), ("ssd_norm_g", "ssd_norm_g"), ("cf_conv_b", "cf_b"), ("cf_ln_g", "ln_g"),
           ("cf_ln_b", "ln_b"), ("ssd_conv_b", "conv4_b")]
SC_REF = ["ssd_dt_bias", "ssd_A_log", "ssd_D"]


def _small_side(get):
    d = {k: get(ref_name).reshape(1, -1) for ref_name, k in VEC_REF}
    d["sc"] = _stack_sc(*[get(n) for n in SC_REF])
    return d


def kernel(x, mem, norm_mix_g, w_in, ssd_conv_w, ssd_conv_b, ssd_dt_bias, ssd_A_log, ssd_D, ssd_norm_g, cf_conv_w, cf_conv_b, cf_ln_g, cf_ln_b, w_out, norm_xattn_g, norm_mem_g, w_q, w_kv, w_o, norm_ffn_g, w_gate, w_up, w_down, norm_final_g, loss_target, m_norm_mix_g, m_w_in, m_ssd_conv_w, m_ssd_conv_b, m_ssd_dt_bias, m_ssd_A_log, m_ssd_D, m_ssd_norm_g, m_cf_conv_w, m_cf_conv_b, m_cf_ln_g, m_cf_ln_b, m_w_out, m_norm_xattn_g, m_norm_mem_g, m_w_q, m_w_kv, m_w_o, m_norm_ffn_g, m_w_gate, m_w_up, m_w_down, m_norm_final_g, v_norm_mix_g, v_w_in, v_ssd_conv_w, v_ssd_conv_b, v_ssd_dt_bias, v_ssd_A_log, v_ssd_D, v_ssd_norm_g, v_cf_conv_w, v_cf_conv_b, v_cf_ln_g, v_cf_ln_b, v_w_out, v_norm_xattn_g, v_norm_mem_g, v_w_q, v_w_kv, v_w_o, v_norm_ffn_g, v_w_gate, v_w_up, v_w_down, v_norm_final_g):
    env = dict(locals())
    wts = {n: env[n] for n in WEIGHT_NAMES}
    mom = {n: env["m_" + n] for n in WEIGHT_NAMES}
    var = {n: env["v_" + n] for n in WEIGHT_NAMES}
    chip = (2 * lax.axis_index("x") + lax.axis_index("y")).astype(jnp.int32).reshape(1)
    core = lax.axis_index("c").astype(jnp.int32).reshape(1)
    where = jnp.concatenate([chip, core])
    big = [n for n, _ in BIG]

    w_in_g, conv4_g, cf_g = _allgather_list([w_in[0].astype(BF16), ssd_conv_w[0], cf_conv_w[0]], "allgather_first")
    W = _pack_in(w_in_g)
    P = _small_side(lambda n: wts[n])
    P["conv4_w"], P["cf_w"] = _cat_cols(conv4_g), _cat_cols(cf_g)
    late = {n: wts[n][0].astype(BF16) for grp in AG_RIDE for n in grp}

    loss, grad_x, GW, GP, pair, got = _local_step(x[0], mem[0], loss_target[0], W, P, core, late)
    joined = _pair_join_list([_chip_sum(pair[n], got[n], where, "rs_chip_sum_" + n) for n in big])
    gshard = dict(zip(big, joined))

    small = dict(GP)
    small["loss"] = loss
    red, sd, sm, sv = _small_allreduce_adamw(small, {k: P[k] for k in SMALL_ADAM}, _small_side(lambda n: mom[n]),
                                             _small_side(lambda n: var[n]))
    grads, delta, new_m, new_v = {}, {}, {}, {}
    for ref_name, k in VEC_REF:
        shp = wts[ref_name].shape
        for dst, src in ((grads, red), (delta, sd), (new_m, sm), (new_v, sv)):
            dst[ref_name] = src[k].reshape(shp)
    for row, ref_name in enumerate(SC_REF):
        for dst, src in ((grads, red), (delta, sd), (new_m, sm), (new_v, sv)):
            dst[ref_name] = src["sc"][row:row + 1, :NH]

    for n, k in (("ssd_conv_w", "conv4_w"), ("cf_conv_w", "cf_w")):
        grads[n], delta[n], new_m[n], new_v[n] = _adamw_cols(wts[n], red[k], mom[n], var[n], chip, "adamw_" + n)
    for n in big:
        grads[n], delta[n], new_m[n], new_v[n] = _adamw(wts[n], gshard[n], mom[n], var[n], "adamw_" + n)

    return (red["loss"][0, 0], grad_x[None], *[grads[n] for n in WEIGHT_NAMES], *[delta[n] for n in WEIGHT_NAMES],
            *[new_m[n] for n in WEIGHT_NAMES], *[new_v[n] for n in WEIGHT_NAMES])
```

```python
import functools
import math

import jax
import jax.numpy as jnp
from jax import lax
from jax.experimental import pallas as pl
from jax.experimental.pallas import tpu as pltpu

F32 = jnp.float32
BF16 = jnp.bfloat16
_MXU = BF16

D = 1024
MEM = 256
NH, HP, NG, NS = 16, 64, 2, 128
GW = NH * HP // NG
CH = 128
XBC = NH * HP + 2 * NG * NS
KS, KC = 4, 31
XH, XD = 4, 256
DFF = 2816
EPS = 1e-6
COL_Z, COL_A, COL_G, COL_XBC, MAINW = 0, 1024, 2048, 3072, 4608
VMEM_LIMIT = 56 * 2 ** 20

ADAM_LR, ADAM_B1, ADAM_B2, ADAM_EPS, ADAM_WD, ADAM_STEP = 0.001, 0.9, 0.999, 1e-08, 0.01, 10

SDS = jax.ShapeDtypeStruct
MESHID = pl.DeviceIdType.MESH


def _cp(*sem):
    return pltpu.CompilerParams(dimension_semantics=sem, vmem_limit_bytes=VMEM_LIMIT)


def _tile(n, cap, unit=128):
    if n <= cap:
        return n
    best = None
    for t in range(unit, cap + 1, unit):
        if n % t == 0:
            best = t
    assert best is not None, (n, cap)
    return best


def _sigmoid(x):
    return 1.0 / (1.0 + jnp.exp(-x))


def _silu(x):
    return x * _sigmoid(x)


def _dsilu(x):
    s = _sigmoid(x)
    return s * (1.0 + x * (1.0 - s))


def _softplus(x):
    return jnp.maximum(x, 0.0) + jnp.log(1.0 + jnp.exp(-jnp.abs(x)))


def _split_bf16(x, passes):
    parts, r = [], x.astype(F32)
    for _ in range(passes):
        p = r.astype(BF16)
        parts.append(p)
        r = r - p.astype(F32)
    return parts


def _dot(a, b, dims=None, exact=None, passes=2):
    dn = {None: (((1,), (0,)), ((), ())), "nt": (((1,), (1,)), ((), ())), "tn": (((0,), (0,)), ((), ()))}[dims]
    if exact is None:
        return lax.dot_general(a.astype(_MXU), b.astype(_MXU), dn, preferred_element_type=F32)
    if exact == "a":
        terms = [(a.astype(BF16), p) for p in _split_bf16(b, passes)]
    else:
        terms = [(p, b.astype(BF16)) for p in _split_bf16(a, passes)]
    out = None
    for lhs, rhs in terms:
        d = lax.dot_general(lhs, rhs, dn, preferred_element_type=F32)
        out = d if out is None else out + d
    return out


def _mm_nn(a, b, name, add=None, out_dtype=F32, tm_cap=1024, tn_cap=1408):
    M, K = a.shape
    _, N = b.shape
    tm, tn = _tile(M, tm_cap, 8), _tile(N, tn_cap)

    def body(a_ref, b_ref, *rest):
        o_ref = rest[-1]
        acc = _dot(a_ref[...], b_ref[...])
        if add is not None:
            acc = acc + rest[0][...]
        o_ref[...] = acc.astype(o_ref.dtype)

    in_specs = [pl.BlockSpec((tm, K), lambda j, i: (i, 0)), pl.BlockSpec((K, tn), lambda j, i: (0, j))]
    args = [a, b]
    if add is not None:
        in_specs.append(pl.BlockSpec((tm, tn), lambda j, i: (i, j)))
        args.append(add)
    return pl.pallas_call(
        body, name=name, grid=(N // tn, M // tm), in_specs=in_specs,
        out_specs=pl.BlockSpec((tm, tn), lambda j, i: (i, j)), out_shape=SDS((M, N), out_dtype),
        compiler_params=_cp("parallel", "parallel"))(*args)


def _mm_nt(a, b, name, add=None, out_dtype=F32, tm_cap=512, tk_cap=1024, comm=()):
    M, N = a.shape
    K = b.shape[0]
    tm, tk = _tile(M, tm_cap, 8), _tile(K, tk_cap)
    nin, nco = 2 + (add is not None), len(comm)
    grid = (K // tk, M // tm)

    def body(*refs):
        a_ref, b_ref, o_ref = refs[0], refs[1], refs[nin + nco]
        exchange = _RidingExchange(refs[nin:nin + nco], refs[nin + nco + 1:nin + 2 * nco + 1],
                                   refs[nin + 2 * nco + 1:], grid)
        exchange.start()
        acc = _dot(a_ref[...], b_ref[...], "nt")
        if add is not None:
            acc = acc + refs[2][...]
        o_ref[...] = acc.astype(o_ref.dtype)
        exchange.finish()

    in_specs = [pl.BlockSpec((tm, N), lambda j, i: (i, 0)), pl.BlockSpec((tk, N), lambda j, i: (j, 0))]
    args = [a, b]
    if add is not None:
        in_specs.append(pl.BlockSpec((tm, tk), lambda j, i: (i, j)))
        args.append(add)
    order = ("arbitrary", "arbitrary") if comm else ("parallel", "parallel")
    outs = pl.pallas_call(
        body, name=name, grid=grid, in_specs=in_specs + [HBM_SPEC] * nco,
        out_specs=[pl.BlockSpec((tm, tk), lambda j, i: (i, j))] + [HBM_SPEC] * nco,
        out_shape=[SDS((M, K), out_dtype)] + [SDS(p.shape, p.dtype) for p in comm],
        scratch_shapes=_RidingExchange.scratch(nco), compiler_params=_cp(*order))(*args, *comm)
    return (outs[0], outs[1:]) if comm else outs[0]


def _mm_tn(a, b, name, tm_cap=1024, tk_cap=512, tn_cap=1408):
    M, K = a.shape
    _, N = b.shape
    tm, tk, tn = _tile(M, tm_cap, 8), _tile(K, tk_cap), _tile(N, tn_cap)

    def body(a_ref, b_ref, o_ref):
        @pl.when(pl.program_id(2) == 0)
        def _():
            o_ref[...] = jnp.zeros_like(o_ref)

        o_ref[...] += _dot(a_ref[...], b_ref[...], "tn")

    return pl.pallas_call(
        body, name=name, grid=(K // tk, N // tn, M // tm),
        in_specs=[pl.BlockSpec((tm, tk), lambda k, n, m: (m, k)), pl.BlockSpec((tm, tn), lambda k, n, m: (m, n))],
        out_specs=pl.BlockSpec((tk, tn), lambda k, n, m: (k, n)), out_shape=SDS((K, N), F32),
        compiler_params=_cp("parallel", "parallel", "arbitrary"))(a, b)


def _rms_fwd(x, g, name, tb_cap=512):
    S, Dm = x.shape
    tb = _tile(S, tb_cap, 8)

    def body(x_ref, g_ref, o_ref):
        xv = x_ref[...]
        r = lax.rsqrt(jnp.mean(xv * xv, axis=-1, keepdims=True) + EPS)
        o_ref[...] = (xv * r * g_ref[...]).astype(o_ref.dtype)

    return pl.pallas_call(
        body, name=name, grid=(S // tb,),
        in_specs=[pl.BlockSpec((tb, Dm), lambda i: (i, 0)), pl.BlockSpec((1, Dm), lambda i: (0, 0))],
        out_specs=pl.BlockSpec((tb, Dm), lambda i: (i, 0)), out_shape=SDS((S, Dm), _MXU),
        compiler_params=_cp("parallel"))(x, g)


def _rms_bwd(x, g, dh, dres, name, tb_cap=512, low=True):
    S, Dm = x.shape
    tb = _tile(S, tb_cap, 8)
    need_dx = dres is not None

    def body(x_ref, g_ref, dh_ref, *rest):
        dg_ref = rest[-1]
        xv = x_ref[...]
        r = lax.rsqrt(jnp.mean(xv * xv, axis=-1, keepdims=True) + EPS)
        xh = xv * r
        dy = dh_ref[...].astype(F32)

        @pl.when(pl.program_id(0) == 0)
        def _():
            dg_ref[...] = jnp.zeros_like(dg_ref)

        dg_ref[...] += jnp.sum(dy * xh, axis=0, keepdims=True)
        if need_dx:
            gdy = dy * g_ref[...]
            dx = r * (gdy - xh * jnp.mean(xh * gdy, axis=-1, keepdims=True))
            tot = rest[0][...] + dx
            rest[1][...] = tot
            if low:
                rest[2][...] = tot.astype(rest[2].dtype)

    row = pl.BlockSpec((tb, Dm), lambda i: (i, 0))
    vec = pl.BlockSpec((1, Dm), lambda i: (0, 0))
    if need_dx:
        outs = [SDS((S, Dm), F32)] + ([SDS((S, Dm), _MXU)] if low else [])
        return pl.pallas_call(
            body, name=name, grid=(S // tb,), in_specs=[row, vec, row, row], out_specs=[row] * len(outs) + [vec],
            out_shape=outs + [SDS((1, Dm), F32)], compiler_params=_cp("arbitrary"))(x, g, dh, dres)
    return pl.pallas_call(
        body, name=name, grid=(S // tb,), in_specs=[row, vec, row], out_specs=vec,
        out_shape=SDS((1, Dm), F32), compiler_params=_cp("arbitrary"))(x, g, dh)


def _final_loss(x, g, tgt, name="final_loss", tb_cap=512):
    S, Dm = x.shape
    tb = _tile(S, tb_cap, 8)

    def body(x_ref, g_ref, t_ref, loss_ref, dx_ref, dxl_ref, dg_ref):
        xv = x_ref[...]
        gv = g_ref[...]
        r = lax.rsqrt(jnp.mean(xv * xv, axis=-1, keepdims=True) + EPS)
        xh = xv * r
        e = xh * gv - t_ref[...]

        @pl.when(pl.program_id(0) == 0)
        def _():
            loss_ref[...] = jnp.zeros_like(loss_ref)
            dg_ref[...] = jnp.zeros_like(dg_ref)

        loss_ref[...] += 0.5 * jnp.sum(jnp.mean(e * e, axis=-1, keepdims=True))
        dy = e * (1.0 / Dm)
        dg_ref[...] += jnp.sum(dy * xh, axis=0, keepdims=True)
        gdy = dy * gv
        dx = r * (gdy - xh * jnp.mean(xh * gdy, axis=-1, keepdims=True))
        dx_ref[...] = dx
        dxl_ref[...] = dx.astype(dxl_ref.dtype)

    row = pl.BlockSpec((tb, Dm), lambda i: (i, 0))
    vec = pl.BlockSpec((1, Dm), lambda i: (0, 0))
    return pl.pallas_call(
        body, name=name, grid=(S // tb,), in_specs=[row, vec, row],
        out_specs=[pl.BlockSpec((1, 128), lambda i: (0, 0)), row, row, vec],
        out_shape=[SDS((1, 128), F32), SDS((S, Dm), F32), SDS((S, Dm), _MXU), SDS((1, Dm), F32)],
        compiler_params=_cp("arbitrary"))(x, g, tgt)


SSD_HALO = 8
CF_HALO = 32
CONV_CB = 512


def _ssd_conv_fwd(proj, w, b, name="ssd_conv_fwd", tb_cap=512):
    S = proj.shape[0]
    tb = _tile(S, tb_cap, 8)
    nb = S // tb
    c0 = COL_XBC // CONV_CB

    def body(x_ref, w_ref, b_ref, o_ref, ext):
        @pl.when(pl.program_id(1) == 0)
        def _():
            ext[pl.ds(0, SSD_HALO), :] = jnp.zeros((SSD_HALO, CONV_CB), F32)

        ext[pl.ds(SSD_HALO, tb), :] = x_ref[...]
        acc = jnp.zeros((tb, CONV_CB), F32) + b_ref[...]
        for k in range(KS):
            acc = acc + ext[pl.ds(SSD_HALO - (KS - 1) + k, tb), :] * w_ref[k:k + 1, :]
        o_ref[...] = acc
        ext[pl.ds(0, SSD_HALO), :] = ext[pl.ds(tb, SSD_HALO), :]

    return pl.pallas_call(
        body, name=name, grid=(XBC // CONV_CB, nb),
        in_specs=[pl.BlockSpec((tb, CONV_CB), lambda j, i: (i, c0 + j)),
                  pl.BlockSpec((KS, CONV_CB), lambda j, i: (0, j)),
                  pl.BlockSpec((1, CONV_CB), lambda j, i: (0, j))],
        out_specs=pl.BlockSpec((tb, CONV_CB), lambda j, i: (i, j)), out_shape=SDS((S, XBC), F32),
        scratch_shapes=[pltpu.VMEM((SSD_HALO + tb, CONV_CB), F32)],
        compiler_params=_cp("parallel", "arbitrary"))(proj, w, b)


def _ssd_conv_bwd(dxbc, proj, w, name="ssd_conv_bwd", tb_cap=512):
    S = proj.shape[0]
    tb = _tile(S, tb_cap, 8)
    nb = S // tb
    c0 = COL_XBC // CONV_CB

    def body(dy_ref, x_ref, w_ref, dx_ref, dw_ref, db_ref, ext):
        @pl.when(pl.program_id(1) == 0)
        def _():
            ext[pl.ds(tb, SSD_HALO), :] = jnp.zeros((SSD_HALO, CONV_CB), F32)
            dw_ref[...] = jnp.zeros_like(dw_ref)
            db_ref[...] = jnp.zeros_like(db_ref)

        dy = dy_ref[...]
        ext[pl.ds(0, tb), :] = dy
        xv = x_ref[...]
        acc = jnp.zeros((tb, CONV_CB), F32)
        for k in range(KS):
            sh = ext[pl.ds(KS - 1 - k, tb), :]
            acc = acc + sh * w_ref[k:k + 1, :]
            dw_ref[k:k + 1, :] += jnp.sum(xv * sh, axis=0, keepdims=True)
        db_ref[...] += jnp.sum(dy, axis=0, keepdims=True)
        dx_ref[...] = acc.astype(dx_ref.dtype)
        ext[pl.ds(tb, SSD_HALO), :] = ext[pl.ds(0, SSD_HALO), :]

    return pl.pallas_call(
        body, name=name, grid=(XBC // CONV_CB, nb),
        in_specs=[pl.BlockSpec((tb, CONV_CB), lambda j, i: (nb - 1 - i, j)),
                  pl.BlockSpec((tb, CONV_CB), lambda j, i: (nb - 1 - i, c0 + j)),
                  pl.BlockSpec((KS, CONV_CB), lambda j, i: (0, j))],
        out_specs=[pl.BlockSpec((tb, CONV_CB), lambda j, i: (nb - 1 - i, j)),
                   pl.BlockSpec((KS, CONV_CB), lambda j, i: (0, j)),
                   pl.BlockSpec((1, CONV_CB), lambda j, i: (0, j))],
        out_shape=[SDS((S, XBC), _MXU), SDS((KS, XBC), F32), SDS((1, XBC), F32)],
        scratch_shapes=[pltpu.VMEM((tb + SSD_HALO, CONV_CB), F32)],
        compiler_params=_cp("parallel", "arbitrary"))(dxbc, proj, w)


HBM_SPEC = pl.BlockSpec(memory_space=pl.ANY)


def _chip_peers(x, y):
    return [(1 - x, y), (x, 1 - y), (1 - x, 1 - y)]


def _remote(src, dst, send_sem, recv_sem, dev):
    return pltpu.make_async_remote_copy(src_ref=src, dst_ref=dst, send_sem=send_sem, recv_sem=recv_sem,
                                        device_id=dev, device_id_type=MESHID)


def _scatter_copies(srcs, outs, send_sems, recv_sems):
    x, y, c = lax.axis_index("x"), lax.axis_index("y"), lax.axis_index("c")
    me = 2 * x + y
    sends, recvs = [], []
    for i, (s, o) in enumerate(zip(srcs, outs)):
        for k, (px, py) in enumerate(_chip_peers(x, y)):
            j = 3 * i + k
            sends.append(_remote(s.at[2 * px + py], o.at[me], send_sems.at[j], recv_sems.at[j], (px, py, c)))
            recvs.append(_remote(s.at[me], o.at[2 * px + py], send_sems.at[j], recv_sems.at[j], (px, py, c)))
    return sends, recvs


class _RidingExchange:
    def __init__(self, srcs, outs, sems, steps):
        self.srcs, self.outs, self.sems, self.steps = srcs, outs, sems, steps

    @staticmethod
    def scratch(n):
        return [pltpu.SemaphoreType.DMA((3 * n,)), pltpu.SemaphoreType.DMA((3 * n,))] if n else []

    def copies(self):
        return _scatter_copies(self.srcs, self.outs, *self.sems)

    def _at(self, last):
        dims = self.steps if isinstance(self.steps, tuple) else (self.steps,)
        hit = None
        for ax, n in enumerate(dims):
            here = pl.program_id(ax) == (n - 1 if last else 0)
            hit = here if hit is None else jnp.logical_and(hit, here)
        return hit

    def start(self):
        if self.srcs:
            @pl.when(self._at(last=False))
            def _():
                for cp in self.copies()[0]:
                    cp.start()

    def finish(self):
        if self.srcs:
            @pl.when(self._at(last=True))
            def _():
                sends, recvs = self.copies()
                for cp in recvs:
                    cp.wait_recv()
                for cp in sends:
                    cp.wait_send()


def _rows_half(ref, rows, h):
    r = rows // 2
    return ref.at[pl.ds(h * r if isinstance(h, int) else pl.multiple_of(h * r, 8), r)]


def _gather_copies(srcs, outs, rows, send_sems, recv_sems):
    x, y, c = lax.axis_index("x"), lax.axis_index("y"), lax.axis_index("c")
    me = 2 * x + y
    sends, recvs = [], []
    for i, (s, o) in enumerate(zip(srcs, outs)):
        mine = _rows_half(s, rows[i], c)
        for k, (px, py) in enumerate(_chip_peers(x, y)):
            j = 3 * i + k
            sends.append(_remote(mine, o.at[me, c], send_sems.at[j], recv_sems.at[j], (px, py, c)))
            recvs.append(_remote(mine, o.at[2 * px + py, c], send_sems.at[j], recv_sems.at[j], (px, py, c)))
    return sends, recvs


def _gather_shapes(shards):
    return [SDS((4, 2, a.shape[0] // 2, a.shape[1]), a.dtype) for a in shards]


class _RidingGather(_RidingExchange):
    def __init__(self, srcs, outs, sems, steps, rows):
        super().__init__(srcs, outs, sems, steps)
        self.rows = rows

    def copies(self):
        return _gather_copies(self.srcs, self.outs, self.rows, *self.sems)


def _head_consts():
    e = (lax.broadcasted_iota(jnp.int32, (128, NH * HP), 1) // HP == lax.broadcasted_iota(jnp.int32, (128, NH * HP), 0)).astype(F32)
    et = (lax.broadcasted_iota(jnp.int32, (NH * HP, 128), 0) // HP == lax.broadcasted_iota(jnp.int32, (NH * HP, 128), 1)).astype(F32)
    r = lax.broadcasted_iota(jnp.int32, (CH, CH), 0)
    c = lax.broadcasted_iota(jnp.int32, (CH, CH), 1)
    return e, et, (c <= r), (r <= c)


def _ssd_common(xbc_c, dtr, dtb, alog, e, tril, triu):
    xbc = _silu(xbc_c)
    xs = xbc[:, :NH * HP]
    dt = _softplus(dtr + dtb)
    A = -jnp.exp(alog)
    a = dt * A
    cs = _dot(tril, a, exact="a", passes=3)
    csT = _dot(a, triu, "tn", exact="b", passes=3)
    csL = cs[CH - 1:CH, :]
    wdec = jnp.exp(csL - cs) * dt
    dtE = _dot(dt, e, exact="b")
    ecsE = _dot(jnp.exp(cs), e, exact="b")
    wE = _dot(wdec, e, exact="b")
    eL = jnp.exp(csL)
    return xbc, xs, dt, A, cs, csT, csL, wdec, dtE, ecsE, wE, eL


def _ssd_fwd(proj, xbc_c, dtr, sc, norm_g, comm=(), name="ssd_fwd"):
    S = proj.shape[0]
    nc = S // CH
    nco = len(comm)

    def body(*refs):
        z_ref, x_ref, dtr_ref, sc_ref, ng_ref = refs[:5]
        y_ref, yn_ref, hp_ref = refs[5 + nco:8 + nco]
        hst = refs[8 + 2 * nco]
        gather = _RidingGather(refs[5:5 + nco], refs[8 + nco:8 + 2 * nco], refs[9 + 2 * nco:], nc,
                               [a.shape[0] for a in comm])
        gather.start()

        @pl.when(pl.program_id(0) == 0)
        def _():
            hst[...] = jnp.zeros_like(hst)

        e, et, tril, triu = _head_consts()
        xbc, xs, dt, A, cs, csT, csL, wdec, dtE, ecsE, wE, eL = _ssd_common(
            x_ref[...], dtr_ref[...], sc_ref[0:1, :], sc_ref[1:2, :], e, tril, triu)
        hp_ref[0] = hst[...]
        xd = xs * dtE
        xw = xs * wE
        dE = _dot(jnp.broadcast_to(sc_ref[2:3, :], (8, 128)), e, exact="b", passes=3)[0:1, :]
        eLcol = jnp.sum(et * eL, axis=1, keepdims=True)
        for g in range(NG):
            Bg = xbc[:, NH * HP + g * NS: NH * HP + (g + 1) * NS]
            Cg = xbc[:, NH * HP + NG * NS + g * NS: NH * HP + NG * NS + (g + 1) * NS]
            gs = slice(g * GW, (g + 1) * GW)
            G = _dot(Cg, Bg, "nt")
            hg = hst[gs, :]
            yoff = ecsE[:, gs] * _dot(Cg, hg, "nt")
            hst[gs, :] = eLcol[gs, :] * hg + _dot(xw[:, gs], Bg, "tn")
            for hh in range(NH // NG):
                h = g * (NH // NG) + hh
                hs = slice(h * HP, (h + 1) * HP)
                m = jnp.where(tril, jnp.exp(jnp.where(tril, cs[:, h:h + 1] - csT[h:h + 1, :], 0.0)), 0.0)
                yd = _dot(G * m, xd[:, hs])
                y_ref[:, hs] = yd + yoff[:, hh * HP:(hh + 1) * HP] + dE[:, hs] * xs[:, hs]
        y = y_ref[...]
        yz = y * _silu(z_ref[...])
        for g in range(NG):
            gs = slice(g * GW, (g + 1) * GW)
            yg = yz[:, gs]
            r = lax.rsqrt(jnp.mean(yg * yg, axis=-1, keepdims=True) + EPS)
            yn_ref[:, gs] = (yg * r * ng_ref[:, gs]).astype(yn_ref.dtype)
        gather.finish()

    outs = pl.pallas_call(
        body, name=name, grid=(nc,),
        in_specs=[pl.BlockSpec((CH, D), lambda c: (c, COL_Z // D)),
                  pl.BlockSpec((CH, XBC), lambda c: (c, 0)),
                  pl.BlockSpec((CH, 128), lambda c: (c, 0)),
                  pl.BlockSpec((8, 128), lambda c: (0, 0)),
                  pl.BlockSpec((1, D), lambda c: (0, 0))] + [HBM_SPEC] * nco,
        out_specs=[pl.BlockSpec((CH, D), lambda c: (c, 0)), pl.BlockSpec((CH, D), lambda c: (c, 0)),
                   pl.BlockSpec((1, NH * HP, NS), lambda c: (c, 0, 0))] + [HBM_SPEC] * nco,
        out_shape=[SDS((S, D), F32), SDS((S, D), _MXU), SDS((nc, NH * HP, NS), F32)] + _gather_shapes(comm),
        scratch_shapes=[pltpu.VMEM((NH * HP, NS), F32)] + _RidingExchange.scratch(nco),
        compiler_params=_cp("arbitrary"))(proj, xbc_c, dtr, sc, norm_g, *comm)
    return outs[:3], outs[3:]


def _ssd_bwd(dmix, y, proj, xbc_c, dtr, hprev, sc, norm_g, comm=(), name="ssd_bwd"):
    S = proj.shape[0]
    nc = S // CH
    nco = len(comm)
    rev = lambda c: nc - 1 - c

    def body(*refs):
        dyn_ref, y_ref, z_ref, x_ref, dtr_ref, hp_ref, sc_ref, ng_ref = refs[:8]
        dz_ref, dx_ref, ddtr_ref, gsc_ref, gng_ref = refs[8 + nco:13 + nco]
        dh, dxd = refs[13 + 2 * nco:15 + 2 * nco]
        exchange = _RidingExchange(refs[8:8 + nco], refs[13 + nco:13 + 2 * nco], refs[15 + 2 * nco:], nc)
        exchange.start()

        @pl.when(pl.program_id(0) == 0)
        def _():
            dh[...] = jnp.zeros_like(dh)
            gsc_ref[...] = jnp.zeros_like(gsc_ref)
            gng_ref[...] = jnp.zeros_like(gng_ref)

        e, et, tril, triu = _head_consts()
        xbc_c = x_ref[...]
        dtr = dtr_ref[...]
        dtb = sc_ref[0:1, :]
        xbc, xs, dt, A, cs, csT, csL, wdec, dtE, ecsE, wE, eL = _ssd_common(
            xbc_c, dtr, dtb, sc_ref[1:2, :], e, tril, triu)
        xd = xs * dtE
        xw = xs * wE
        dE = _dot(jnp.broadcast_to(sc_ref[2:3, :], (8, 128)), e, exact="b", passes=3)[0:1, :]
        eLcol = jnp.sum(et * eL, axis=1, keepdims=True)

        yv = y_ref[...]
        zv = z_ref[...]
        sz = _silu(zv)
        yz = yv * sz
        dyn = dyn_ref[...]
        dyz_parts = []
        for g in range(NG):
            gs = slice(g * GW, (g + 1) * GW)
            yg = yz[:, gs]
            r = lax.rsqrt(jnp.mean(yg * yg, axis=-1, keepdims=True) + EPS)
            yh = yg * r
            dn = dyn[:, gs]
            gng_ref[:, gs] += jnp.sum(dn * yh, axis=0, keepdims=True)
            gdn = dn * ng_ref[:, gs]
            dyz_parts.append(r * (gdn - yh * jnp.mean(yh * gdn, axis=-1, keepdims=True)))
        dyz = jnp.concatenate(dyz_parts, axis=1)
        dy = dyz * sz
        dz_ref[...] = (dyz * yv * _dsilu(zv)).astype(dz_ref.dtype)

        gsc_ref[2:3, :] += jnp.sum(_dot(dy * xs, et, exact="b", passes=3), axis=0, keepdims=True)
        dxs = dE * dy
        dzo = ecsE * dy
        dcs = jnp.zeros((CH, 128), F32)
        dcsL = jnp.zeros((1, 128), F32)
        ddt = jnp.zeros((CH, 128), F32)
        dB_parts, dC_parts, yoff_parts, dxw_parts = [], [], [], []
        for g in range(NG):
            Bg = xbc[:, NH * HP + g * NS: NH * HP + (g + 1) * NS]
            Cg = xbc[:, NH * HP + NG * NS + g * NS: NH * HP + NG * NS + (g + 1) * NS]
            gs = slice(g * GW, (g + 1) * GW)
            hg = hp_ref[0, gs, :]
            dhn = dh[gs, :]
            G = _dot(Cg, Bg, "nt")
            yoff_parts.append(ecsE[:, gs] * _dot(Cg, hg, "nt"))
            dC = _dot(dzo[:, gs], hg)
            dhp = _dot(dzo[:, gs], Cg, "tn") + eLcol[gs, :] * dhn
            t1 = jnp.sum(dhn * hg, axis=1, keepdims=True) * eLcol[gs, :]
            dcsL = dcsL + jnp.sum(et[gs, :] * t1, axis=0, keepdims=True)
            dxw_parts.append(_dot(Bg, dhn, "nt"))
            dB = _dot(xw[:, gs], dhn)
            dgsum = jnp.zeros((CH, CH), F32)
            for hh in range(NH // NG):
                h = g * (NH // NG) + hh
                hs = slice(h * HP, (h + 1) * HP)
                m = jnp.where(tril, jnp.exp(jnp.where(tril, cs[:, h:h + 1] - csT[h:h + 1, :], 0.0)), 0.0)
                sc = G * m
                dyh = dy[:, hs]
                dxd[:, hs] = _dot(sc, dyh, "tn")
                dsc = _dot(dyh, xd[:, hs], "nt")
                q = dsc * sc
                oh = (lax.broadcasted_iota(jnp.int32, (CH, 128), 1) == h).astype(F32)
                dcs = dcs + _dot(q, oh, exact="b") - _dot(q, oh, "tn", exact="b")
                dgsum = dgsum + dsc * m
            dC_parts.append(dC + _dot(dgsum, Bg))
            dB_parts.append(dB + _dot(dgsum, Cg, "tn"))
            dh[gs, :] = dhp
        yoff = jnp.concatenate(yoff_parts, axis=1)
        dxw = jnp.concatenate(dxw_parts, axis=1)
        dxdv = dxd[...]
        dcs = dcs + _dot(dy * yoff, et, exact="b")
        dxs = dxs + wE * dxw + dtE * dxdv
        dw = _dot(dxw * xs, et, exact="b")
        ddt = ddt + dw * jnp.exp(csL - cs) + _dot(dxdv * xs, et, exact="b")
        dcs = dcs - dw * wdec
        dcsL = dcsL + jnp.sum(dw * wdec, axis=0, keepdims=True)
        last = lax.broadcasted_iota(jnp.int32, (CH, 128), 0) == CH - 1
        dcs = dcs + jnp.where(last, dcsL, 0.0)
        da = _dot(triu, dcs, exact="a", passes=3)
        ddt = ddt + da * A
        gsc_ref[1:2, :] += jnp.sum(da * dt, axis=0, keepdims=True) * A
        valid = lax.broadcasted_iota(jnp.int32, (CH, 128), 1) < NH
        ddtr = jnp.where(valid, ddt * _sigmoid(dtr + dtb), 0.0)
        gsc_ref[0:1, :] += jnp.sum(ddtr, axis=0, keepdims=True)
        ddtr_ref[...] = ddtr.astype(ddtr_ref.dtype)
        dxbc = jnp.concatenate([dxs] + dB_parts + dC_parts, axis=1)
        dx_ref[...] = dxbc * _dsilu(xbc_c)
        exchange.finish()

    vec = pl.BlockSpec((8, 128), lambda c: (0, 0))
    vecd = pl.BlockSpec((1, D), lambda c: (0, 0))
    row = lambda w, j=0: pl.BlockSpec((CH, w), lambda c: (rev(c), j))
    outs = pl.pallas_call(
        body, name=name, grid=(nc,),
        in_specs=[row(D), row(D), row(D, COL_Z // D), row(XBC), row(128),
                  pl.BlockSpec((1, NH * HP, NS), lambda c: (rev(c), 0, 0)), vec, vecd] + [HBM_SPEC] * nco,
        out_specs=[row(D), row(XBC), row(128), vec, vecd] + [HBM_SPEC] * nco,
        out_shape=[SDS((S, D), _MXU), SDS((S, XBC), F32), SDS((S, 128), _MXU), SDS((8, 128), F32), SDS((1, D), F32)]
        + [SDS(p.shape, p.dtype) for p in comm],
        scratch_shapes=[pltpu.VMEM((NH * HP, NS), F32), pltpu.VMEM((CH, NH * HP), F32)] + _RidingExchange.scratch(nco),
        compiler_params=_cp("arbitrary"))(dmix, y, proj, xbc_c, dtr, hprev, sc, norm_g, *comm)
    return outs[:5], outs[5:]


CONV_RT = 32


def _fill_phases(ext, ph, rows):
    for s in range(1, 8):
        ph[s - 1, pl.ds(0, rows), :] = ext[pl.ds(s, rows), :]


def _window(ext, ph, off, r0, ls):
    s = off % 8
    src = ext if s == 0 else ph.at[s - 1]
    return src[pl.ds(pl.multiple_of(off - s + r0, 8), CONV_RT), ls]


def _cf_fwd(proj, w, b, lg, lb, comm=(), name="cf_fwd", tb_cap=256):
    S = proj.shape[0]
    tb = _tile(S, tb_cap, 8)
    nb = S // tb
    nco = len(comm)

    def body(*refs):
        a_ref, g_ref, w_ref, b_ref, lg_ref, lb_ref = refs[:6]
        u1_ref, u_ref = refs[6 + nco:8 + nco]
        ext, ph = refs[8 + 2 * nco:10 + 2 * nco]
        gather = _RidingGather(refs[6:6 + nco], refs[8 + nco:8 + 2 * nco], refs[10 + 2 * nco:], nb,
                               [a.shape[0] for a in comm])
        gather.start()

        @pl.when(pl.program_id(0) == 0)
        def _():
            ext[pl.ds(0, CF_HALO), :] = jnp.zeros((CF_HALO, D), F32)

        ext[pl.ds(CF_HALO, tb), :] = a_ref[...] * _sigmoid(g_ref[...])
        _fill_phases(ext, ph, tb + CF_HALO - 8)

        def tile(i, carry):
            r0 = pl.multiple_of(i * CONV_RT, CONV_RT)
            for l in range(D // 128):
                ls = pl.ds(l * 128, 128)
                acc = jnp.broadcast_to(b_ref[:, ls], (CONV_RT, 128))
                for k in range(KC):
                    acc = acc + _window(ext, ph, CF_HALO - (KC - 1) + k, r0, ls) * w_ref[k:k + 1, ls]
                u1_ref[pl.ds(r0, CONV_RT), ls] = acc
            return carry

        lax.fori_loop(0, tb // CONV_RT, tile, 0)
        acc = u1_ref[...]
        mu = jnp.mean(acc, axis=-1, keepdims=True)
        xc = acc - mu
        r = lax.rsqrt(jnp.mean(xc * xc, axis=-1, keepdims=True) + EPS)
        u_ref[...] = _silu(xc * r * lg_ref[...] + lb_ref[...]).astype(u_ref.dtype)
        ext[pl.ds(0, CF_HALO), :] = ext[pl.ds(tb, CF_HALO), :]
        gather.finish()

    vec = pl.BlockSpec((1, D), lambda i: (0, 0))
    outs = pl.pallas_call(
        body, name=name, grid=(nb,),
        in_specs=[pl.BlockSpec((tb, D), lambda i: (i, COL_A // D)), pl.BlockSpec((tb, D), lambda i: (i, COL_G // D)),
                  pl.BlockSpec((KC, D), lambda i: (0, 0)), vec, vec, vec] + [HBM_SPEC] * nco,
        out_specs=[pl.BlockSpec((tb, D), lambda i: (i, 0)), pl.BlockSpec((tb, D), lambda i: (i, 0))] + [HBM_SPEC] * nco,
        out_shape=[SDS((S, D), F32), SDS((S, D), _MXU)] + _gather_shapes(comm),
        scratch_shapes=[pltpu.VMEM((CF_HALO + tb, D), F32), pltpu.VMEM((7, tb + CF_HALO - 8, D), F32)]
        + _RidingExchange.scratch(nco),
        compiler_params=_cp("arbitrary"))(proj, proj, w, b, lg, lb, *comm)
    return outs[:2], outs[2:]


def _cf_bwd(dmix, u1, proj, w, lg, lb, comm=(), name="cf_bwd", tb_cap=256):
    S = proj.shape[0]
    tb = _tile(S, tb_cap, 8)
    nb = S // tb
    nco = len(comm)
    rev = lambda i: nb - 1 - i

    def body(*refs):
        du_ref, u1_ref, a_ref, g_ref, w_ref, lg_ref, lb_ref = refs[:7]
        da_ref, dg_ref, dw_ref, db_ref, dlg_ref, dlb_ref = refs[7 + nco:13 + nco]
        ext, ph, u0s = refs[13 + 2 * nco:16 + 2 * nco]
        exchange = _RidingExchange(refs[7:7 + nco], refs[13 + nco:13 + 2 * nco], refs[16 + 2 * nco:], nb)
        exchange.start()

        @pl.when(pl.program_id(0) == 0)
        def _():
            ext[pl.ds(tb, CF_HALO), :] = jnp.zeros((CF_HALO, D), F32)
            dw_ref[...] = jnp.zeros_like(dw_ref)
            db_ref[...] = jnp.zeros_like(db_ref)
            dlg_ref[...] = jnp.zeros_like(dlg_ref)
            dlb_ref[...] = jnp.zeros_like(dlb_ref)

        u1 = u1_ref[...]
        mu = jnp.mean(u1, axis=-1, keepdims=True)
        xc = u1 - mu
        r = lax.rsqrt(jnp.mean(xc * xc, axis=-1, keepdims=True) + EPS)
        xh = xc * r
        lgv = lg_ref[...]
        du2 = du_ref[...] * _dsilu(xh * lgv + lb_ref[...])
        dlg_ref[...] += jnp.sum(du2 * xh, axis=0, keepdims=True)
        dlb_ref[...] += jnp.sum(du2, axis=0, keepdims=True)
        gd = du2 * lgv
        du1 = r * (gd - jnp.mean(gd, axis=-1, keepdims=True) - xh * jnp.mean(gd * xh, axis=-1, keepdims=True))
        db_ref[...] += jnp.sum(du1, axis=0, keepdims=True)
        ext[pl.ds(0, tb), :] = du1
        u0s[...] = a_ref[...] * _sigmoid(g_ref[...])
        _fill_phases(ext, ph, tb + CF_HALO - 8)

        def dx_tile(i, carry):
            r0 = pl.multiple_of(i * CONV_RT, CONV_RT)
            rows = pl.ds(r0, CONV_RT)
            for l in range(D // 128):
                ls = pl.ds(l * 128, 128)
                acc = jnp.zeros((CONV_RT, 128), F32)
                for k in range(KC):
                    acc = acc + _window(ext, ph, KC - 1 - k, r0, ls) * w_ref[k:k + 1, ls]
                sg = _sigmoid(g_ref[rows, ls])
                da_ref[rows, ls] = (acc * sg).astype(da_ref.dtype)
                dg_ref[rows, ls] = (acc * a_ref[rows, ls] * sg * (1.0 - sg)).astype(dg_ref.dtype)
            return carry

        lax.fori_loop(0, tb // CONV_RT, dx_tile, 0)
        for l in range(D // 128):
            ls = pl.ds(l * 128, 128)

            def dw_tile(i, accs, ls=ls):
                r0 = pl.multiple_of(i * CONV_RT, CONV_RT)
                u0t = u0s[pl.ds(r0, CONV_RT), ls]
                out = []
                for k in range(KC):
                    p = u0t * _window(ext, ph, KC - 1 - k, r0, ls)
                    out.append(accs[k] + ((p[0:8] + p[8:16]) + (p[16:24] + p[24:32])))
                return tuple(out)

            accs = lax.fori_loop(0, tb // CONV_RT, dw_tile, tuple(jnp.zeros((8, 128), F32) for _ in range(KC)))
            for k in range(KC):
                dw_ref[k:k + 1, ls] += jnp.sum(accs[k], axis=0, keepdims=True)
        ext[pl.ds(tb, CF_HALO), :] = ext[pl.ds(0, CF_HALO), :]
        exchange.finish()

    vec = pl.BlockSpec((1, D), lambda i: (0, 0))
    wsp = pl.BlockSpec((KC, D), lambda i: (0, 0))
    row = lambda j=0: pl.BlockSpec((tb, D), lambda i: (rev(i), j))
    outs = pl.pallas_call(
        body, name=name, grid=(nb,),
        in_specs=[row(1), row(), row(COL_A // D), row(COL_G // D), wsp, vec, vec] + [HBM_SPEC] * nco,
        out_specs=[row(), row(), wsp, vec, vec, vec] + [HBM_SPEC] * nco,
        out_shape=[SDS((S, D), _MXU), SDS((S, D), _MXU), SDS((KC, D), F32),
                   SDS((1, D), F32), SDS((1, D), F32), SDS((1, D), F32)] + [SDS(p.shape, p.dtype) for p in comm],
        scratch_shapes=[pltpu.VMEM((tb + CF_HALO, D), F32), pltpu.VMEM((7, tb + CF_HALO - 8, D), F32),
                        pltpu.VMEM((tb, D), F32)] + _RidingExchange.scratch(nco),
        compiler_params=_cp("arbitrary"))(dmix, u1, proj, proj, w, lg, lb, *comm)
    return outs[:6], outs[6:]


def _attn_fwd(q, kv, name="attn_fwd", tq_cap=512):
    S = q.shape[0]
    tq = _tile(S, tq_cap, 8)
    scale = XD ** -0.5

    def body(q_ref, kv_ref, o_ref):
        for h in range(XH):
            hs = slice(h * XD, (h + 1) * XD)
            s = _dot(q_ref[:, hs], kv_ref[:, hs], "nt") * scale
            s = s - jnp.max(s, axis=-1, keepdims=True)
            p = jnp.exp(s)
            p = p / jnp.sum(p, axis=-1, keepdims=True)
            o_ref[:, hs] = _dot(p, kv_ref[:, D + h * XD: D + (h + 1) * XD]).astype(o_ref.dtype)

    return pl.pallas_call(
        body, name=name, grid=(S // tq,),
        in_specs=[pl.BlockSpec((tq, D), lambda i: (i, 0)), pl.BlockSpec((MEM, 2 * D), lambda i: (0, 0))],
        out_specs=pl.BlockSpec((tq, D), lambda i: (i, 0)), out_shape=SDS((S, D), _MXU),
        compiler_params=_cp("parallel"))(q, kv)


def _attn_bwd(do, q, kv, name="attn_bwd", tq_cap=512):
    S = q.shape[0]
    tq = _tile(S, tq_cap, 8)
    scale = XD ** -0.5

    def body(do_ref, q_ref, kv_ref, dq_ref, dkv_ref):
        @pl.when(pl.program_id(0) == 0)
        def _():
            dkv_ref[...] = jnp.zeros_like(dkv_ref)

        for h in range(XH):
            hs = slice(h * XD, (h + 1) * XD)
            vs = slice(D + h * XD, D + (h + 1) * XD)
            qh = q_ref[:, hs]
            kh = kv_ref[:, hs]
            s = _dot(qh, kh, "nt") * scale
            s = s - jnp.max(s, axis=-1, keepdims=True)
            p = jnp.exp(s)
            p = p / jnp.sum(p, axis=-1, keepdims=True)
            doh = do_ref[:, hs]
            dp = _dot(doh, kv_ref[:, vs], "nt")
            ds = p * (dp - jnp.sum(dp * p, axis=-1, keepdims=True)) * scale
            dq_ref[:, hs] = _dot(ds, kh).astype(dq_ref.dtype)
            dkv_ref[:, hs] += _dot(ds, qh, "tn")
            dkv_ref[:, vs] += _dot(p, doh, "tn")

    return pl.pallas_call(
        body, name=name, grid=(S // tq,),
        in_specs=[pl.BlockSpec((tq, D), lambda i: (i, 0)), pl.BlockSpec((tq, D), lambda i: (i, 0)),
                  pl.BlockSpec((MEM, 2 * D), lambda i: (0, 0))],
        out_specs=[pl.BlockSpec((tq, D), lambda i: (i, 0)), pl.BlockSpec((MEM, 2 * D), lambda i: (0, 0))],
        out_shape=[SDS((S, D), _MXU), SDS((MEM, 2 * D), F32)],
        compiler_params=_cp("arbitrary"))(do, q, kv)


def _ffn_act(gt, up, name="ffn_act", tb_cap=128):
    S = gt.shape[0]
    tb = _tile(S, tb_cap, 8)

    def body(g_ref, u_ref, o_ref):
        o_ref[...] = (_silu(g_ref[...]) * u_ref[...]).astype(o_ref.dtype)

    blk = pl.BlockSpec((tb, DFF), lambda i: (i, 0))
    return pl.pallas_call(body, name=name, grid=(S // tb,), in_specs=[blk, blk], out_specs=blk,
                          out_shape=SDS((S, DFF), _MXU), compiler_params=_cp("parallel"))(gt, up)


def _ffn_act_bwd(dact, gt, up, name="ffn_act_bwd", tb_cap=128):
    S = gt.shape[0]
    tb = _tile(S, tb_cap, 8)

    def body(d_ref, g_ref, u_ref, dg_ref, du_ref):
        gt = g_ref[...]
        d = d_ref[...]
        s = _sigmoid(gt)
        dg_ref[...] = (d * u_ref[...] * (s * (1.0 + gt * (1.0 - s)))).astype(dg_ref.dtype)
        du_ref[...] = (d * gt * s).astype(du_ref.dtype)

    blk = pl.BlockSpec((tb, DFF), lambda i: (i, 0))
    return pl.pallas_call(
        body, name=name, grid=(S // tb,), in_specs=[blk, blk, blk], out_specs=[blk, blk],
        out_shape=[SDS((S, DFF), _MXU), SDS((S, DFF), _MXU)], compiler_params=_cp("parallel"))(dact, gt, up)


AG_RIDE = (("w_out", "w_q", "w_kv", "w_o"), ("w_gate", "w_up", "w_down"))


def _local_step(x, mem, tgt, W, P, core=None, late=None):
    pair, got = {}, {}
    ride = [[late[n] for n in grp] if late is not None else [] for grp in AG_RIDE]

    def rs_pair(group):
        if core is None:
            return []
        ps = _rs_pair(group, GW, core)
        pair.update(zip(group, ps))
        return ps
    h = _rms_fwd(x, P["g_mix"], "rms_mix")
    proj = _mm_nn(h, W["main"], "in_proj", tn_cap=1152)
    dtr = _mm_nn(h, W["dt"], "in_proj_dt")
    xbc_c = _ssd_conv_fwd(proj, P["conv4_w"], P["conv4_b"])
    (y, yn, hprev), bufs0 = _ssd_fwd(proj, xbc_c, dtr, P["sc"], P["ssd_norm_g"], comm=ride[0])
    (u1, u), bufs1 = _cf_fwd(proj, P["cf_w"], P["cf_b"], P["ln_g"], P["ln_b"], comm=ride[1])
    if late is not None:
        names = AG_RIDE[0] + AG_RIDE[1]
        W = dict(W, **_pack_late(dict(zip(names, _gather_finish_list(ride[0] + ride[1], list(bufs0) + list(bufs1))))))
    mix = jnp.concatenate([yn, u], axis=1)
    x1 = _mm_nn(mix, W["out"], "out_proj", add=x)
    hq = _rms_fwd(x1, P["g_xattn"], "rms_xattn")
    q = _mm_nn(hq, W["q"], "q_proj")
    mn = _rms_fwd(mem, P["g_mem"], "rms_mem")
    kv = _mm_nn(mn, W["kv"], "kv_proj")
    o = _attn_fwd(q, kv)
    x2 = _mm_nn(o, W["o"], "o_proj", add=x1)
    hf = _rms_fwd(x2, P["g_ffn"], "rms_ffn")
    gt = _mm_nt(hf, W["gate_t"], "ffn_gate", tk_cap=1408)
    up = _mm_nt(hf, W["up_t"], "ffn_up", tk_cap=1408)
    act = _ffn_act(gt, up)
    x3 = _mm_nn(act, W["down"], "ffn_out", add=x2)
    loss, dx3, dx3b, g_final = _final_loss(x3, P["g_final"], tgt)
    GW, GP = {}, {"g_final": g_final}
    dact = _mm_nt(dx3b, W["down"], "ffn_out_dx", tk_cap=1408)
    GW["down"] = _mm_tn(act, dx3b, "ffn_out_dw", tk_cap=1408, tn_cap=1024)
    dgt, dup = _ffn_act_bwd(dact, gt, up)
    dhf = _mm_nn(dgt, W["gate_t"], "ffn_gate_dx", tm_cap=512)
    dhf = _mm_nn(dup, W["up_t"], "ffn_up_dx", add=dhf, tm_cap=512)
    GW["gate_t"] = _mm_tn(dgt, hf, "ffn_gate_dw", tk_cap=1408, tn_cap=1024)
    GW["up_t"] = _mm_tn(dup, hf, "ffn_up_dw", tk_cap=1408, tn_cap=1024)
    ffn_pieces = rs_pair(RS_GROUPS[0])
    dx2, dx2b, GP["g_ffn"] = _rms_bwd(x2, P["g_ffn"], dhf, dx3, "rms_ffn_bwd")
    do = _mm_nt(dx2b, W["o"], "o_proj_dx")
    GW["o"] = _mm_tn(o, dx2b, "o_proj_dw")
    dq, dkv = _attn_bwd(do, q, kv)
    dhq = _mm_nt(dq, W["q"], "q_proj_dx")
    GW["q"] = _mm_tn(hq, dq, "q_proj_dw")
    dkvb = dkv.astype(_MXU)
    GW["kv"] = _mm_tn(mn, dkvb, "kv_proj_dw", tm_cap=256)
    dmn = _mm_nt(dkvb, W["kv"], "kv_proj_dx")
    GP["g_mem"] = _rms_bwd(mem, P["g_mem"], dmn, None, "rms_mem_bwd")
    dx1, dx1b, GP["g_xattn"] = _rms_bwd(x1, P["g_xattn"], dhq, dx2, "rms_xattn_bwd")
    dmix = _mm_nt(dx1b, W["out"], "out_proj_dx")
    GW["out"] = _mm_tn(mix, dx1b, "out_proj_dw", tn_cap=1024)
    attn_pieces = rs_pair(RS_GROUPS[1])
    (da, dg, GP["cf_w"], GP["cf_b"], GP["ln_g"], GP["ln_b"]), came = _cf_bwd(
        dmix, u1, proj, P["cf_w"], P["ln_g"], P["ln_b"], comm=ffn_pieces)
    got.update(zip(RS_GROUPS[0], came))
    (dz, dxbc_c, ddtr, GP["sc"], GP["ssd_norm_g"]), came = _ssd_bwd(
        dmix, y, proj, xbc_c, dtr, hprev, P["sc"], P["ssd_norm_g"], comm=attn_pieces)
    got.update(zip(RS_GROUPS[1], came))
    dxbc, GP["conv4_w"], GP["conv4_b"] = _ssd_conv_bwd(dxbc_c, proj, P["conv4_w"])
    dproj = jnp.concatenate([dz, da, dg, dxbc], axis=1)
    GW["main"] = _mm_tn(h, dproj, "in_proj_dw", tn_cap=1152)
    GW["dt"] = _mm_tn(h, ddtr, "in_proj_dt_dw")
    in_pieces = rs_pair(RS_GROUPS[2])
    dh = _mm_nt(ddtr, W["dt"], "in_proj_dt_dx")
    if in_pieces:
        dh, came = _mm_nt(dproj, W["main"], "in_proj_dx", add=dh, tk_cap=512, comm=in_pieces)
        got.update(zip(RS_GROUPS[2], came))
    else:
        dh = _mm_nt(dproj, W["main"], "in_proj_dx", add=dh, tk_cap=512)
    grad_x, GP["g_mix"] = _rms_bwd(x, P["g_mix"], dh, dx1, "rms_mix_bwd", low=False)
    if core is None:
        return loss, grad_x, GW, GP
    return loss, grad_x, GW, GP, pair, got


Z_END, XBC_END, DT_END = NH * HP, NH * HP + XBC, NH * HP + XBC + NH


def _pad_to(a, rows=None, cols=None):
    r = 0 if rows is None else rows - a.shape[0]
    c = 0 if cols is None else cols - a.shape[1]
    return jnp.pad(a, ((0, r), (0, c)))


IN_W = DT_END + 2 * D
W_IN_SEGS = [(0, Z_END, "main", COL_Z), (Z_END, XBC_END, "main", COL_XBC), (XBC_END, DT_END, "dt", 0),
             (DT_END, DT_END + D, "main", COL_A), (DT_END + D, IN_W, "main", COL_G)]
BIG = [("w_in", True), ("w_out", False), ("w_q", False), ("w_kv", True), ("w_o", False), ("w_gate", False),
       ("w_up", False), ("w_down", False)]
TRANSPOSED = ("w_gate", "w_up")


def _ref_cols(pieces, a, b):
    cw = IN_W // 4
    out = []
    for j in range(4):
        lo, hi = max(a, j * cw), min(b, (j + 1) * cw)
        if lo < hi:
            out.append(pieces[j][:, lo - j * cw:hi - j * cw])
    return out


def _cat_cols(pieces):
    return jnp.concatenate([pieces[j] for j in range(4)], axis=1)


def _pack_in(w_in):
    main = jnp.concatenate(_ref_cols(w_in, 0, Z_END) + _ref_cols(w_in, DT_END, IN_W) + _ref_cols(w_in, Z_END, XBC_END), axis=1)
    return {"main": main, "dt": _pad_to(jnp.concatenate(_ref_cols(w_in, XBC_END, DT_END), axis=1), cols=128)}


def _pack_late(pc):
    rows = lambda n: pc[n].reshape(-1, pc[n].shape[-1])
    return {"out": rows("w_out"), "q": rows("w_q"), "kv": _cat_cols(pc["w_kv"]), "o": rows("w_o"),
            "gate_t": rows("w_gate"), "up_t": rows("w_up"), "down": rows("w_down")}


GW_KEY = {"w_gate": "gate_t", "w_up": "up_t", "w_kv": "kv", "w_out": "out", "w_q": "q", "w_o": "o", "w_down": "down"}
RS_GROUPS = (("w_down", "w_gate", "w_up"), ("w_out", "w_q", "w_kv", "w_o"), ("w_in",))


def _shard_grad(name, GW):
    if name == "w_in":
        cw = IN_W // 4
        pieces = []
        for j in range(4):
            parts = []
            for a, b, src, col in W_IN_SEGS:
                lo, hi = max(a, j * cw), min(b, (j + 1) * cw)
                if lo < hi:
                    parts.append(GW[src][:, col + lo - a:col + hi - a])
            pieces.append(jnp.concatenate(parts, axis=1))
        return jnp.stack(pieces)
    g = GW[GW_KEY[name]]
    if dict(BIG)[name]:
        cw = g.shape[1] // 4
        return jnp.stack([g[:, j * cw:(j + 1) * cw] for j in range(4)])
    return g.reshape(4, g.shape[0] // 4, g.shape[1])


def _rs_pair(names, GW, core):
    gs = [_shard_grad(n, GW) for n in names]
    halves = [g.reshape(4, 2, g.shape[1] // 2, g.shape[2]) for g in gs]
    theirs = _pair_split_list(halves, "rs_pair_send_" + names[0])
    return [_pair_sum(h, t, core, "rs_pair_sum_" + n) for h, t, n in zip(halves, theirs, names)]


def _stack_sc(dt_bias, a_log, d):
    return _pad_to(jnp.concatenate([dt_bias, a_log, d], axis=0), rows=8, cols=128)


COMM_PARAMS = pltpu.CompilerParams(vmem_limit_bytes=VMEM_LIMIT)


def _dma_sems(*counts):
    return [pltpu.SemaphoreType.DMA((n,)) for n in counts]


def _allgather_list(arrs, name):
    n = len(arrs)
    halved = [a.shape[0] % 16 == 0 for a in arrs]
    oshape = [(4, 2, a.shape[0] // 2, a.shape[1]) if h else (4, 1) + a.shape for a, h in zip(arrs, halved)]

    def body(*refs):
        srcs, outs = refs[:n], refs[n:2 * n]
        ici_send, ici_recv, own_send, own_recv, fwd_send, fwd_recv = refs[2 * n:]
        x, y, c = lax.axis_index("x"), lax.axis_index("y"), lax.axis_index("c")
        me = 2 * x + y
        sib = (x, y, 1 - c)
        peers = _chip_peers(x, y)

        def half(i, h):
            r = arrs[i].shape[0] // 2
            if not halved[i]:
                return srcs[i]
            return srcs[i].at[pl.ds(h * r if isinstance(h, int) else pl.multiple_of(h * r, 8), r)]

        ici, own, fwd = [], [], []
        for i in range(n):
            mine_h = c if halved[i] else 0
            for k, (px, py) in enumerate(peers):
                s = 3 * i + k
                ici.append(_remote(half(i, c), outs[i].at[me, mine_h], ici_send.at[s], ici_recv.at[s], (px, py, c)))
            for h in range(2 if halved[i] else 1):
                s = 2 * i + h
                own.append(_remote(half(i, h), outs[i].at[me, h], own_send.at[s], own_recv.at[s], sib))
        for cp in ici + own:
            cp.start()
        for i in range(n):
            if not halved[i]:
                continue
            for k, (px, py) in enumerate(peers):
                s = 3 * i + k
                got = outs[i].at[2 * px + py, c]
                _remote(half(i, c), got, ici_send.at[s], ici_recv.at[s], (px, py, c)).wait_recv()
                f = _remote(got, got, fwd_send.at[s], fwd_recv.at[s], sib)
                f.start()
                fwd.append(f)
        for i in range(n):
            for k, (px, py) in enumerate(peers):
                s = 3 * i + k
                if halved[i]:
                    _remote(half(i, c), outs[i].at[2 * px + py, 1 - c], fwd_send.at[s], fwd_recv.at[s], sib).wait_recv()
                else:
                    _remote(srcs[i], outs[i].at[2 * px + py, 0], ici_send.at[s], ici_recv.at[s], (px, py, c)).wait_recv()
            for h in range(2 if halved[i] else 1):
                s = 2 * i + h
                _remote(half(i, h), outs[i].at[me, h], own_send.at[s], own_recv.at[s], sib).wait_recv()
        for cp in ici + own + fwd:
            cp.wait_send()

    outs = pl.pallas_call(
        body, name=name, in_specs=[HBM_SPEC] * n, out_specs=[HBM_SPEC] * n,
        out_shape=[SDS(s, a.dtype) for s, a in zip(oshape, arrs)],
        scratch_shapes=_dma_sems(3 * n, 3 * n, 2 * n, 2 * n, 3 * n, 3 * n), compiler_params=COMM_PARAMS)(*arrs)
    return [o.reshape((4,) + a.shape) for o, a in zip(outs, arrs)]


def _pair_split_list(gs, name):
    n = len(gs)

    def body(*refs):
        srcs, outs = refs[:n], refs[n:2 * n]
        send_sems, recv_sems = refs[2 * n:]
        x, y, c = lax.axis_index("x"), lax.axis_index("y"), lax.axis_index("c")
        sib = (x, y, 1 - c)
        sends = [_remote(srcs[i].at[j, 1 - c], outs[i].at[j], send_sems.at[4 * i + j], recv_sems.at[4 * i + j], sib)
                 for i in range(n) for j in range(4)]
        for cp in sends:
            cp.start()
        for cp in sends:
            cp.wait_recv()
        for cp in sends:
            cp.wait_send()

    return pl.pallas_call(
        body, name=name, in_specs=[HBM_SPEC] * n, out_specs=[HBM_SPEC] * n,
        out_shape=[SDS((4,) + g.shape[2:], g.dtype) for g in gs],
        scratch_shapes=_dma_sems(4 * n, 4 * n), compiler_params=COMM_PARAMS)(*gs)


def _gather_finish_list(shards, bufs, name="allgather_finish"):
    n = len(shards)

    def body(*refs):
        srcs, outs = refs[:n], refs[2 * n:3 * n]
        own_send, own_recv, fwd_send, fwd_recv = refs[3 * n:]
        x, y, c = lax.axis_index("x"), lax.axis_index("y"), lax.axis_index("c")
        me = 2 * x + y
        sib = (x, y, 1 - c)
        sends, recvs = [], []
        for i in range(n):
            for h in range(2):
                own = _remote(_rows_half(srcs[i], shards[i].shape[0], h), outs[i].at[me, h],
                              own_send.at[2 * i + h], own_recv.at[2 * i + h], sib)
                sends.append(own)
                recvs.append(own)
            for k, (px, py) in enumerate(_chip_peers(x, y)):
                got, s = outs[i].at[2 * px + py, c], 3 * i + k
                sends.append(_remote(got, got, fwd_send.at[s], fwd_recv.at[s], sib))
                recvs.append(_remote(got, outs[i].at[2 * px + py, 1 - c], fwd_send.at[s], fwd_recv.at[s], sib))
        for cp in sends:
            cp.start()
        for cp in recvs:
            cp.wait_recv()
        for cp in sends:
            cp.wait_send()

    outs = pl.pallas_call(
        body, name=name, in_specs=[HBM_SPEC] * (2 * n), out_specs=[HBM_SPEC] * n,
        out_shape=[SDS(b.shape, b.dtype) for b in bufs], input_output_aliases={n + i: i for i in range(n)},
        scratch_shapes=_dma_sems(2 * n, 2 * n, 3 * n, 3 * n), compiler_params=COMM_PARAMS)(*shards, *bufs)
    return [o.reshape((4,) + a.shape) for o, a in zip(outs, shards)]


JOIN_SPLIT = 4


def _pair_join_list(bufs, name="rs_pair_join"):
    n = len(bufs)

    def body(*refs):
        outs = refs[n:2 * n]
        send_sems, recv_sems = refs[2 * n:]
        x, y, c = lax.axis_index("x"), lax.axis_index("y"), lax.axis_index("c")
        sib = (x, y, 1 - c)
        sends, recvs = [], []
        for i in range(n):
            rc = bufs[i].shape[1] // JOIN_SPLIT
            for q in range(JOIN_SPLIT):
                k = JOIN_SPLIT * i + q
                rows = pl.ds(q * rc, rc)
                sends.append(_remote(outs[i].at[c, rows], outs[i].at[c, rows], send_sems.at[k], recv_sems.at[k], sib))
                recvs.append(_remote(outs[i].at[c, rows], outs[i].at[1 - c, rows], send_sems.at[k], recv_sems.at[k], sib))
        for cp in sends:
            cp.start()
        for cp in recvs:
            cp.wait_recv()
        for cp in sends:
            cp.wait_send()

    return pl.pallas_call(
        body, name=name, in_specs=[HBM_SPEC] * n, out_specs=[HBM_SPEC] * n,
        out_shape=[SDS(b.shape, b.dtype) for b in bufs], input_output_aliases={i: i for i in range(n)},
        scratch_shapes=_dma_sems(JOIN_SPLIT * n, JOIN_SPLIT * n), compiler_params=COMM_PARAMS)(*bufs)


def _pair_sum(g, theirs, core, name):
    _, _, r, c = g.shape

    def body(core_ref, g_ref, t_ref, o_ref):
        o_ref[...] = (g_ref[...] + t_ref[...]).astype(o_ref.dtype)

    spec = pltpu.PrefetchScalarGridSpec(
        num_scalar_prefetch=1, grid=(4,),
        in_specs=[pl.BlockSpec((None, None, r, c), lambda j, core_ref: (j, core_ref[0], 0, 0)),
                  pl.BlockSpec((None, r, c), lambda j, core_ref: (j, 0, 0))],
        out_specs=pl.BlockSpec((None, r, c), lambda j, core_ref: (j, 0, 0)))
    return pl.pallas_call(body, name=name, grid_spec=spec, out_shape=SDS((4, r, c), BF16),
                          compiler_params=_cp("parallel"))(core, g, theirs)


def _chip_sum(own, got, where, name):
    _, r, c = own.shape
    tr = r // 2

    def body(w_ref, a_ref, b1_ref, b2_ref, b3_ref, o_ref):
        o_ref[...] = ((a_ref[...].astype(F32) + b1_ref[...].astype(F32)) + b2_ref[...].astype(F32)) + b3_ref[...].astype(F32)

    piece = lambda k: pl.BlockSpec((None, tr, c), lambda i, w_ref: ((w_ref[0] + k) % 4, i, 0))
    spec = pltpu.PrefetchScalarGridSpec(
        num_scalar_prefetch=1, grid=(r // tr,), in_specs=[piece(0), piece(1), piece(2), piece(3)],
        out_specs=pl.BlockSpec((None, tr, c), lambda i, w_ref: (w_ref[1], i, 0)))
    return pl.pallas_call(body, name=name, grid_spec=spec, out_shape=SDS((2, r, c), F32),
                          compiler_params=_cp("parallel"))(where, own, got, got, got)


def _adam_math(w, g, m, v):
    bc1 = 1.0 - ADAM_B1 ** ADAM_STEP
    bc2 = 1.0 - ADAM_B2 ** ADAM_STEP
    mn = ADAM_B1 * m + (1.0 - ADAM_B1) * g
    vn = ADAM_B2 * v + (1.0 - ADAM_B2) * (g * g)
    return -ADAM_LR * ((mn / bc1) / (jnp.sqrt(vn / bc2) + ADAM_EPS) + ADAM_WD * w), mn, vn


PACK_COLS = XBC
PACK = {"g_mix": (0, 1, D), "g_xattn": (1, 1, D), "g_mem": (2, 1, D), "g_ffn": (3, 1, D), "g_final": (4, 1, D),
        "ssd_norm_g": (5, 1, D), "cf_b": (6, 1, D), "ln_g": (7, 1, D), "ln_b": (8, 1, D), "conv4_b": (9, 1, XBC),
        "conv4_w": (10, KS, XBC), "sc": (16, 8, 128), "cf_w": (24, KC, D), "loss": (55, 1, 128)}
PACK_ROWS = 56
SMALL_ADAM = ["g_mix", "g_xattn", "g_mem", "g_ffn", "g_final", "ssd_norm_g", "cf_b", "ln_g", "ln_b", "conv4_b", "sc"]


def _small_allreduce_adamw(grads, wts, mom, var, name="allreduce_small"):
    gk = list(PACK)
    ng, na = len(gk), len(SMALL_ADAM)

    def body(*refs):
        g_in = refs[:ng]
        w_in, m_in, v_in = (refs[ng + i * na: ng + (i + 1) * na] for i in range(3))
        o = refs[ng + 3 * na:]
        g_out = o[:ng]
        d_out, m_out, v_out = (o[ng + i * na: ng + (i + 1) * na] for i in range(3))
        pack, buf, acc, send_sems, recv_sems = o[ng + 3 * na:]
        x, y, c = lax.axis_index("x"), lax.axis_index("y"), lax.axis_index("c")
        me = 4 * x + 2 * y + c
        pack[...] = jnp.zeros_like(pack)
        for i, k in enumerate(gk):
            r0, nr, nc = PACK[k]
            pack[r0:r0 + nr, 0:nc] = g_in[i][...]
        peers = [(x, y, 1 - c)] + [(px, py, pc) for px, py in _chip_peers(x, y) for pc in (c, 1 - c)]
        sends = [_remote(pack, buf.at[me], send_sems.at[k], recv_sems.at[k], dev) for k, dev in enumerate(peers)]
        for cp in sends:
            cp.start()
        buf[me] = pack[...]
        for k, (px, py, pc) in enumerate(peers):
            _remote(pack, buf.at[4 * px + 2 * py + pc], send_sems.at[k], recv_sems.at[k], (px, py, pc)).wait_recv()
        for cp in sends:
            cp.wait_send()
        tot = buf[0]
        for i in range(1, 8):
            tot = tot + buf[i]
        acc[...] = tot
        for i, k in enumerate(gk):
            r0, nr, nc = PACK[k]
            g_out[i][...] = acc[r0:r0 + nr, 0:nc]
        for i, k in enumerate(SMALL_ADAM):
            r0, nr, nc = PACK[k]
            d_out[i][...], m_out[i][...], v_out[i][...] = _adam_math(
                w_in[i][...], acc[r0:r0 + nr, 0:nc], m_in[i][...], v_in[i][...])

    args = [grads[k] for k in gk] + [d[k] for d in (wts, mom, var) for k in SMALL_ADAM]
    shp = lambda k: SDS((PACK[k][1], PACK[k][2]), F32)
    vm = pl.BlockSpec(memory_space=pltpu.VMEM)
    outs = pl.pallas_call(
        body, name=name, in_specs=[vm] * len(args), out_specs=[vm] * (ng + 3 * na),
        out_shape=[shp(k) for k in gk] + [shp(k) for _ in range(3) for k in SMALL_ADAM],
        scratch_shapes=[pltpu.VMEM((PACK_ROWS, PACK_COLS), F32), pltpu.VMEM((8, PACK_ROWS, PACK_COLS), F32),
                        pltpu.VMEM((PACK_ROWS, PACK_COLS), F32)] + _dma_sems(7, 7),
        compiler_params=COMM_PARAMS)(*args)
    red = dict(zip(gk, outs[:ng]))
    parts = [dict(zip(SMALL_ADAM, outs[ng + i * na: ng + (i + 1) * na])) for i in range(3)]
    return red, parts[0], parts[1], parts[2]


def _adamw_cols(w, gfull, m, v, chip, name):
    _, R, C = w.shape

    def body(w_idx, w_ref, g_ref, m_ref, v_ref, go_ref, d_ref, mo_ref, vo_ref):
        go_ref[...] = g_ref[...]
        d_ref[...], mo_ref[...], vo_ref[...] = _adam_math(w_ref[...], g_ref[...], m_ref[...], v_ref[...])

    blk = pl.BlockSpec((None, R, C), lambda i, w_idx: (0, 0, 0))
    spec = pltpu.PrefetchScalarGridSpec(
        num_scalar_prefetch=1, grid=(1,),
        in_specs=[blk, pl.BlockSpec((R, C), lambda i, w_idx: (0, w_idx[0])), blk, blk], out_specs=[blk] * 4)
    return pl.pallas_call(body, name=name, grid_spec=spec, out_shape=[SDS((1, R, C), F32)] * 4,
                          compiler_params=_cp("arbitrary"))(chip, w, gfull, m, v)


def _adamw(w, g, m, v, name):
    _, R, C = w.shape
    half = R // 2
    tr = _tile(half, max(8, (2 ** 17 // C) // 8 * 8), 8)
    nh = half // tr

    def body(w_ref, g_ref, m_ref, v_ref, go_ref, d_ref, mo_ref, vo_ref):
        go_ref[...] = g_ref[...]
        d_ref[...], mo_ref[...], vo_ref[...] = _adam_math(w_ref[...], g_ref[...], m_ref[...], v_ref[...])

    blk = pl.BlockSpec((None, tr, C), lambda i: (0, i, 0))
    gblk = pl.BlockSpec((None, tr, C), lambda i: (i // nh, i % nh, 0))
    return pl.pallas_call(body, name=name, grid=(R // tr,), in_specs=[blk, gblk, blk, blk], out_specs=[blk] * 4,
                          out_shape=[SDS((1, R, C), F32)] * 4, compiler_params=_cp("parallel"))(w, g, m, v)


WEIGHT_NAMES = ["norm_mix_g", "w_in", "ssd_conv_w", "ssd_conv_b", "ssd_dt_bias", "ssd_A_log", "ssd_D", "ssd_norm_g",
                "cf_conv_w", "cf_conv_b", "cf_ln_g", "cf_ln_b", "w_out", "norm_xattn_g", "norm_mem_g", "w_q", "w_kv",
                "w_o", "norm_ffn_g", "w_gate", "w_up", "w_down", "norm_final_g"]
VEC_REF = [("norm_mix_g", "g_mix"), ("norm_xattn_g", "g_xattn"), ("norm_mem_g", "g_mem"), ("norm_ffn_g", "g_ffn"),
           ("norm_final_g", "g_final"), ("ssd_norm_g", "ssd_norm_g"), ("cf_conv_b", "cf_b"), ("cf_ln_g", "ln_g"),
           ("cf_ln_b", "ln_b"), ("ssd_conv_b", "conv4_b")]
SC_REF = ["ssd_dt_bias", "ssd_A_log", "ssd_D"]


def _small_side(get):
    d = {k: get(ref_name).reshape(1, -1) for ref_name, k in VEC_REF}
    d["sc"] = _stack_sc(*[get(n) for n in SC_REF])
    return d


def kernel(x, mem, norm_mix_g, w_in, ssd_conv_w, ssd_conv_b, ssd_dt_bias, ssd_A_log, ssd_D, ssd_norm_g, cf_conv_w, cf_conv_b, cf_ln_g, cf_ln_b, w_out, norm_xattn_g, norm_mem_g, w_q, w_kv, w_o, norm_ffn_g, w_gate, w_up, w_down, norm_final_g, loss_target, m_norm_mix_g, m_w_in, m_ssd_conv_w, m_ssd_conv_b, m_ssd_dt_bias, m_ssd_A_log, m_ssd_D, m_ssd_norm_g, m_cf_conv_w, m_cf_conv_b, m_cf_ln_g, m_cf_ln_b, m_w_out, m_norm_xattn_g, m_norm_mem_g, m_w_q, m_w_kv, m_w_o, m_norm_ffn_g, m_w_gate, m_w_up, m_w_down, m_norm_final_g, v_norm_mix_g, v_w_in, v_ssd_conv_w, v_ssd_conv_b, v_ssd_dt_bias, v_ssd_A_log, v_ssd_D, v_ssd_norm_g, v_cf_conv_w, v_cf_conv_b, v_cf_ln_g, v_cf_ln_b, v_w_out, v_norm_xattn_g, v_norm_mem_g, v_w_q, v_w_kv, v_w_o, v_norm_ffn_g, v_w_gate, v_w_up, v_w_down, v_norm_final_g):
    env = dict(locals())
    view = lambda n, a: a.transpose(0, 2, 1) if n in TRANSPOSED else a
    wts = {n: view(n, env[n]) for n in WEIGHT_NAMES}
    mom = {n: view(n, env["m_" + n]) for n in WEIGHT_NAMES}
    var = {n: view(n, env["v_" + n]) for n in WEIGHT_NAMES}
    chip = (2 * lax.axis_index("x") + lax.axis_index("y")).astype(jnp.int32).reshape(1)
    core = lax.axis_index("c").astype(jnp.int32).reshape(1)
    where = jnp.concatenate([chip, core])
    big = [n for n, _ in BIG]

    w_in_g, conv4_g, cf_g = _allgather_list([w_in[0].astype(BF16), ssd_conv_w[0], cf_conv_w[0]], "allgather_first")
    W = _pack_in(w_in_g)
    P = _small_side(lambda n: wts[n])
    P["conv4_w"], P["cf_w"] = _cat_cols(conv4_g), _cat_cols(cf_g)
    late = {n: wts[n][0].astype(BF16) for grp in AG_RIDE for n in grp}

    loss, grad_x, GW, GP, pair, got = _local_step(x[0], mem[0], loss_target[0], W, P, core, late)
    joined = _pair_join_list([_chip_sum(pair[n], got[n], where, "rs_chip_sum_" + n) for n in big])
    gshard = dict(zip(big, joined))

    small = dict(GP)
    small["loss"] = loss
    red, sd, sm, sv = _small_allreduce_adamw(small, {k: P[k] for k in SMALL_ADAM}, _small_side(lambda n: mom[n]),
                                             _small_side(lambda n: var[n]))
    grads, delta, new_m, new_v = {}, {}, {}, {}
    for ref_name, k in VEC_REF:
        shp = wts[ref_name].shape
        for dst, src in ((grads, red), (delta, sd), (new_m, sm), (new_v, sv)):
            dst[ref_name] = src[k].reshape(shp)
    for row, ref_name in enumerate(SC_REF):
        for dst, src in ((grads, red), (delta, sd), (new_m, sm), (new_v, sv)):
            dst[ref_name] = src["sc"][row:row + 1, :NH]

    for n, k in (("ssd_conv_w", "conv4_w"), ("cf_conv_w", "cf_w")):
        grads[n], delta[n], new_m[n], new_v[n] = _adamw_cols(wts[n], red[k], mom[n], var[n], chip, "adamw_" + n)
    for n in big:
        outs = _adamw(wts[n], gshard[n], mom[n], var[n], "adamw_" + n)
        grads[n], delta[n], new_m[n], new_v[n] = [view(n, o) for o in outs]

    return (red["loss"][0, 0], grad_x[None], *[grads[n] for n in WEIGHT_NAMES], *[delta[n] for n in WEIGHT_NAMES],
            *[new_m[n] for n in WEIGHT_NAMES], *[new_v[n] for n in WEIGHT_NAMES])
```

```python
import functools
import math

import jax
import jax.numpy as jnp
from jax import lax
from jax.experimental import pallas as pl
from jax.experimental.pallas import tpu as pltpu

F32 = jnp.float32
BF16 = jnp.bfloat16
_MXU = BF16

D = 1024
MEM = 256
NH, HP, NG, NS = 16, 64, 2, 128
GW = NH * HP // NG
CH = 128
XBC = NH * HP + 2 * NG * NS
KS, KC = 4, 31
XH, XD = 4, 256
DFF = 2816
EPS = 1e-6
COL_Z, COL_A, COL_G, COL_XBC, MAINW = 0, 1024, 2048, 3072, 4608
VMEM_LIMIT = 56 * 2 ** 20

ADAM_LR, ADAM_B1, ADAM_B2, ADAM_EPS, ADAM_WD, ADAM_STEP = 0.001, 0.9, 0.999, 1e-08, 0.01, 10

SDS = jax.ShapeDtypeStruct
MESHID = pl.DeviceIdType.MESH


def _cp(*sem):
    return pltpu.CompilerParams(dimension_semantics=sem, vmem_limit_bytes=VMEM_LIMIT)


def _tile(n, cap, unit=128):
    if n <= cap:
        return n
    best = None
    for t in range(unit, cap + 1, unit):
        if n % t == 0:
            best = t
    assert best is not None, (n, cap)
    return best


def _sigmoid(x):
    return 1.0 / (1.0 + jnp.exp(-x))


def _silu(x):
    return x * _sigmoid(x)


def _dsilu(x):
    s = _sigmoid(x)
    return s * (1.0 + x * (1.0 - s))


def _softplus(x):
    return jnp.maximum(x, 0.0) + jnp.log(1.0 + jnp.exp(-jnp.abs(x)))


def _split_bf16(x, passes):
    parts, r = [], x.astype(F32)
    for _ in range(passes):
        p = r.astype(BF16)
        parts.append(p)
        r = r - p.astype(F32)
    return parts


def _dot(a, b, dims=None, exact=None, passes=2):
    dn = {None: (((1,), (0,)), ((), ())), "nt": (((1,), (1,)), ((), ())), "tn": (((0,), (0,)), ((), ()))}[dims]
    if exact is None:
        return lax.dot_general(a.astype(_MXU), b.astype(_MXU), dn, preferred_element_type=F32)
    if exact == "a":
        terms = [(a.astype(BF16), p) for p in _split_bf16(b, passes)]
    else:
        terms = [(p, b.astype(BF16)) for p in _split_bf16(a, passes)]
    out = None
    for lhs, rhs in terms:
        d = lax.dot_general(lhs, rhs, dn, preferred_element_type=F32)
        out = d if out is None else out + d
    return out


def _mm_nn(a, b, name, add=None, out_dtype=F32, tm_cap=1024, tn_cap=1408):
    M, K = a.shape
    _, N = b.shape
    tm, tn = _tile(M, tm_cap, 8), _tile(N, tn_cap)

    def body(a_ref, b_ref, *rest):
        o_ref = rest[-1]
        acc = _dot(a_ref[...], b_ref[...])
        if add is not None:
            acc = acc + rest[0][...]
        o_ref[...] = acc.astype(o_ref.dtype)

    in_specs = [pl.BlockSpec((tm, K), lambda j, i: (i, 0)), pl.BlockSpec((K, tn), lambda j, i: (0, j))]
    args = [a, b]
    if add is not None:
        in_specs.append(pl.BlockSpec((tm, tn), lambda j, i: (i, j)))
        args.append(add)
    return pl.pallas_call(
        body, name=name, grid=(N // tn, M // tm), in_specs=in_specs,
        out_specs=pl.BlockSpec((tm, tn), lambda j, i: (i, j)), out_shape=SDS((M, N), out_dtype),
        compiler_params=_cp("parallel", "parallel"))(*args)


def _mm_nt(a, b, name, add=None, out_dtype=F32, tm_cap=512, tk_cap=1024, comm=()):
    M, N = a.shape
    K = b.shape[0]
    tm, tk = _tile(M, tm_cap, 8), _tile(K, tk_cap)
    nin, nco = 2 + (add is not None), len(comm)
    grid = (K // tk, M // tm)

    def body(*refs):
        a_ref, b_ref, o_ref = refs[0], refs[1], refs[nin + nco]
        exchange = _RidingExchange(refs[nin:nin + nco], refs[nin + nco + 1:nin + 2 * nco + 1],
                                   refs[nin + 2 * nco + 1:], grid)
        exchange.start()
        acc = _dot(a_ref[...], b_ref[...], "nt")
        if add is not None:
            acc = acc + refs[2][...]
        o_ref[...] = acc.astype(o_ref.dtype)
        exchange.finish()

    in_specs = [pl.BlockSpec((tm, N), lambda j, i: (i, 0)), pl.BlockSpec((tk, N), lambda j, i: (j, 0))]
    args = [a, b]
    if add is not None:
        in_specs.append(pl.BlockSpec((tm, tk), lambda j, i: (i, j)))
        args.append(add)
    order = ("arbitrary", "arbitrary") if comm else ("parallel", "parallel")
    outs = pl.pallas_call(
        body, name=name, grid=grid, in_specs=in_specs + [HBM_SPEC] * nco,
        out_specs=[pl.BlockSpec((tm, tk), lambda j, i: (i, j))] + [HBM_SPEC] * nco,
        out_shape=[SDS((M, K), out_dtype)] + [SDS(p.shape, p.dtype) for p in comm],
        scratch_shapes=_RidingExchange.scratch(nco), compiler_params=_cp(*order))(*args, *comm)
    return (outs[0], outs[1:]) if comm else outs[0]


def _mm_tn(a, b, name, tm_cap=1024, tk_cap=512, tn_cap=1408):
    M, K = a.shape
    _, N = b.shape
    tm, tk, tn = _tile(M, tm_cap, 8), _tile(K, tk_cap), _tile(N, tn_cap)

    def body(a_ref, b_ref, o_ref):
        @pl.when(pl.program_id(2) == 0)
        def _():
            o_ref[...] = jnp.zeros_like(o_ref)

        o_ref[...] += _dot(a_ref[...], b_ref[...], "tn")

    return pl.pallas_call(
        body, name=name, grid=(K // tk, N // tn, M // tm),
        in_specs=[pl.BlockSpec((tm, tk), lambda k, n, m: (m, k)), pl.BlockSpec((tm, tn), lambda k, n, m: (m, n))],
        out_specs=pl.BlockSpec((tk, tn), lambda k, n, m: (k, n)), out_shape=SDS((K, N), F32),
        compiler_params=_cp("parallel", "parallel", "arbitrary"))(a, b)


def _rms_fwd(x, g, name, tb_cap=512):
    S, Dm = x.shape
    tb = _tile(S, tb_cap, 8)

    def body(x_ref, g_ref, o_ref):
        xv = x_ref[...]
        r = lax.rsqrt(jnp.mean(xv * xv, axis=-1, keepdims=True) + EPS)
        o_ref[...] = (xv * r * g_ref[...]).astype(o_ref.dtype)

    return pl.pallas_call(
        body, name=name, grid=(S // tb,),
        in_specs=[pl.BlockSpec((tb, Dm), lambda i: (i, 0)), pl.BlockSpec((1, Dm), lambda i: (0, 0))],
        out_specs=pl.BlockSpec((tb, Dm), lambda i: (i, 0)), out_shape=SDS((S, Dm), _MXU),
        compiler_params=_cp("parallel"))(x, g)


def _rms_bwd(x, g, dh, dres, name, tb_cap=512, low=True):
    S, Dm = x.shape
    tb = _tile(S, tb_cap, 8)
    need_dx = dres is not None

    def body(x_ref, g_ref, dh_ref, *rest):
        dg_ref = rest[-1]
        xv = x_ref[...]
        r = lax.rsqrt(jnp.mean(xv * xv, axis=-1, keepdims=True) + EPS)
        xh = xv * r
        dy = dh_ref[...].astype(F32)

        @pl.when(pl.program_id(0) == 0)
        def _():
            dg_ref[...] = jnp.zeros_like(dg_ref)

        dg_ref[...] += jnp.sum(dy * xh, axis=0, keepdims=True)
        if need_dx:
            gdy = dy * g_ref[...]
            dx = r * (gdy - xh * jnp.mean(xh * gdy, axis=-1, keepdims=True))
            tot = rest[0][...] + dx
            rest[1][...] = tot
            if low:
                rest[2][...] = tot.astype(rest[2].dtype)

    row = pl.BlockSpec((tb, Dm), lambda i: (i, 0))
    vec = pl.BlockSpec((1, Dm), lambda i: (0, 0))
    if need_dx:
        outs = [SDS((S, Dm), F32)] + ([SDS((S, Dm), _MXU)] if low else [])
        return pl.pallas_call(
            body, name=name, grid=(S // tb,), in_specs=[row, vec, row, row], out_specs=[row] * len(outs) + [vec],
            out_shape=outs + [SDS((1, Dm), F32)], compiler_params=_cp("arbitrary"))(x, g, dh, dres)
    return pl.pallas_call(
        body, name=name, grid=(S // tb,), in_specs=[row, vec, row], out_specs=vec,
        out_shape=SDS((1, Dm), F32), compiler_params=_cp("arbitrary"))(x, g, dh)


def _final_loss(x, g, tgt, name="final_loss", tb_cap=512):
    S, Dm = x.shape
    tb = _tile(S, tb_cap, 8)

    def body(x_ref, g_ref, t_ref, loss_ref, dx_ref, dxl_ref, dg_ref):
        xv = x_ref[...]
        gv = g_ref[...]
        r = lax.rsqrt(jnp.mean(xv * xv, axis=-1, keepdims=True) + EPS)
        xh = xv * r
        e = xh * gv - t_ref[...]

        @pl.when(pl.program_id(0) == 0)
        def _():
            loss_ref[...] = jnp.zeros_like(loss_ref)
            dg_ref[...] = jnp.zeros_like(dg_ref)

        loss_ref[...] += 0.5 * jnp.sum(jnp.mean(e * e, axis=-1, keepdims=True))
        dy = e * (1.0 / Dm)
        dg_ref[...] += jnp.sum(dy * xh, axis=0, keepdims=True)
        gdy = dy * gv
        dx = r * (gdy - xh * jnp.mean(xh * gdy, axis=-1, keepdims=True))
        dx_ref[...] = dx
        dxl_ref[...] = dx.astype(dxl_ref.dtype)

    row = pl.BlockSpec((tb, Dm), lambda i: (i, 0))
    vec = pl.BlockSpec((1, Dm), lambda i: (0, 0))
    return pl.pallas_call(
        body, name=name, grid=(S // tb,), in_specs=[row, vec, row],
        out_specs=[pl.BlockSpec((1, 128), lambda i: (0, 0)), row, row, vec],
        out_shape=[SDS((1, 128), F32), SDS((S, Dm), F32), SDS((S, Dm), _MXU), SDS((1, Dm), F32)],
        compiler_params=_cp("arbitrary"))(x, g, tgt)


SSD_HALO = 8
CF_HALO = 32

HBM_SPEC = pl.BlockSpec(memory_space=pl.ANY)


def _chip_peers(x, y):
    return [(1 - x, y), (x, 1 - y), (1 - x, 1 - y)]


def _remote(src, dst, send_sem, recv_sem, dev):
    return pltpu.make_async_remote_copy(src_ref=src, dst_ref=dst, send_sem=send_sem, recv_sem=recv_sem,
                                        device_id=dev, device_id_type=MESHID)


def _scatter_copies(srcs, outs, send_sems, recv_sems):
    x, y, c = lax.axis_index("x"), lax.axis_index("y"), lax.axis_index("c")
    me = 2 * x + y
    sends, recvs = [], []
    for i, (s, o) in enumerate(zip(srcs, outs)):
        for k, (px, py) in enumerate(_chip_peers(x, y)):
            j = 3 * i + k
            sends.append(_remote(s.at[2 * px + py], o.at[me], send_sems.at[j], recv_sems.at[j], (px, py, c)))
            recvs.append(_remote(s.at[me], o.at[2 * px + py], send_sems.at[j], recv_sems.at[j], (px, py, c)))
    return sends, recvs


class _RidingExchange:
    def __init__(self, srcs, outs, sems, steps):
        self.srcs, self.outs, self.sems, self.steps = srcs, outs, sems, steps

    @staticmethod
    def scratch(n):
        return [pltpu.SemaphoreType.DMA((3 * n,)), pltpu.SemaphoreType.DMA((3 * n,))] if n else []

    def copies(self):
        return _scatter_copies(self.srcs, self.outs, *self.sems)

    def _at(self, last):
        dims = self.steps if isinstance(self.steps, tuple) else (self.steps,)
        hit = None
        for ax, n in enumerate(dims):
            here = pl.program_id(ax) == (n - 1 if last else 0)
            hit = here if hit is None else jnp.logical_and(hit, here)
        return hit

    def start(self):
        if self.srcs:
            @pl.when(self._at(last=False))
            def _():
                for cp in self.copies()[0]:
                    cp.start()

    def finish(self):
        if self.srcs:
            @pl.when(self._at(last=True))
            def _():
                sends, recvs = self.copies()
                for cp in recvs:
                    cp.wait_recv()
                for cp in sends:
                    cp.wait_send()


def _rows_half(ref, rows, h):
    r = rows // 2
    return ref.at[pl.ds(h * r if isinstance(h, int) else pl.multiple_of(h * r, 8), r)]


def _gather_copies(srcs, outs, rows, send_sems, recv_sems):
    x, y, c = lax.axis_index("x"), lax.axis_index("y"), lax.axis_index("c")
    me = 2 * x + y
    sends, recvs = [], []
    for i, (s, o) in enumerate(zip(srcs, outs)):
        mine = _rows_half(s, rows[i], c)
        for k, (px, py) in enumerate(_chip_peers(x, y)):
            j = 3 * i + k
            sends.append(_remote(mine, o.at[me, c], send_sems.at[j], recv_sems.at[j], (px, py, c)))
            recvs.append(_remote(mine, o.at[2 * px + py, c], send_sems.at[j], recv_sems.at[j], (px, py, c)))
    return sends, recvs


def _gather_shapes(shards):
    return [SDS((4, 2, a.shape[0] // 2, a.shape[1]), a.dtype) for a in shards]


class _RidingGather(_RidingExchange):
    def __init__(self, srcs, outs, sems, steps, rows):
        super().__init__(srcs, outs, sems, steps)
        self.rows = rows

    def copies(self):
        return _gather_copies(self.srcs, self.outs, self.rows, *self.sems)


def _head_consts():
    e = (lax.broadcasted_iota(jnp.int32, (128, NH * HP), 1) // HP == lax.broadcasted_iota(jnp.int32, (128, NH * HP), 0)).astype(F32)
    et = (lax.broadcasted_iota(jnp.int32, (NH * HP, 128), 0) // HP == lax.broadcasted_iota(jnp.int32, (NH * HP, 128), 1)).astype(F32)
    r = lax.broadcasted_iota(jnp.int32, (CH, CH), 0)
    c = lax.broadcasted_iota(jnp.int32, (CH, CH), 1)
    return e, et, (c <= r), (r <= c)


def _ssd_common(xbc_c, dtr, dtb, alog, e, tril, triu):
    xbc = _silu(xbc_c)
    xs = xbc[:, :NH * HP]
    dt = _softplus(dtr + dtb)
    A = -jnp.exp(alog)
    a = dt * A
    cs = _dot(tril, a, exact="a", passes=3)
    csT = _dot(a, triu, "tn", exact="b", passes=3)
    csL = cs[CH - 1:CH, :]
    wdec = jnp.exp(csL - cs) * dt
    dtE = _dot(dt, e, exact="b")
    ecsE = _dot(jnp.exp(cs), e, exact="b")
    wE = _dot(wdec, e, exact="b")
    eL = jnp.exp(csL)
    return xbc, xs, dt, A, cs, csT, csL, wdec, dtE, ecsE, wE, eL


def _ssd_fwd(proj, cw, cb, dtr, sc, norm_g, comm=(), name="ssd_fwd"):
    S = proj.shape[0]
    nc = S // CH
    nco = len(comm)

    def body(*refs):
        z_ref, xp_ref, cw_ref, cb_ref, dtr_ref, sc_ref, ng_ref = refs[:7]
        xc_ref, y_ref, yn_ref, hp_ref = refs[7 + nco:11 + nco]
        hst, cext = refs[11 + 2 * nco:13 + 2 * nco]
        gather = _RidingGather(refs[7:7 + nco], refs[11 + nco:11 + 2 * nco], refs[13 + 2 * nco:], nc,
                               [a.shape[0] for a in comm])
        gather.start()

        @pl.when(pl.program_id(0) == 0)
        def _():
            hst[...] = jnp.zeros_like(hst)
            cext[pl.ds(0, SSD_HALO), :] = jnp.zeros((SSD_HALO, XBC), F32)

        cext[pl.ds(SSD_HALO, CH), :] = xp_ref[...]
        xc = jnp.zeros((CH, XBC), F32) + cb_ref[...]
        for k in range(KS):
            xc = xc + cext[pl.ds(SSD_HALO - (KS - 1) + k, CH), :] * cw_ref[k:k + 1, :]
        xc_ref[...] = xc
        cext[pl.ds(0, SSD_HALO), :] = cext[pl.ds(CH, SSD_HALO), :]

        e, et, tril, triu = _head_consts()
        xbc, xs, dt, A, cs, csT, csL, wdec, dtE, ecsE, wE, eL = _ssd_common(
            xc, dtr_ref[...], sc_ref[0:1, :], sc_ref[1:2, :], e, tril, triu)
        hp_ref[0] = hst[...]
        xd = xs * dtE
        xw = xs * wE
        dE = _dot(jnp.broadcast_to(sc_ref[2:3, :], (8, 128)), e, exact="b", passes=3)[0:1, :]
        eLcol = jnp.sum(et * eL, axis=1, keepdims=True)
        for g in range(NG):
            Bg = xbc[:, NH * HP + g * NS: NH * HP + (g + 1) * NS]
            Cg = xbc[:, NH * HP + NG * NS + g * NS: NH * HP + NG * NS + (g + 1) * NS]
            gs = slice(g * GW, (g + 1) * GW)
            G = _dot(Cg, Bg, "nt")
            hg = hst[gs, :]
            yoff = ecsE[:, gs] * _dot(Cg, hg, "nt")
            hst[gs, :] = eLcol[gs, :] * hg + _dot(xw[:, gs], Bg, "tn")
            for hh in range(NH // NG):
                h = g * (NH // NG) + hh
                hs = slice(h * HP, (h + 1) * HP)
                m = jnp.where(tril, jnp.exp(jnp.where(tril, cs[:, h:h + 1] - csT[h:h + 1, :], 0.0)), 0.0)
                yd = _dot(G * m, xd[:, hs])
                y_ref[:, hs] = yd + yoff[:, hh * HP:(hh + 1) * HP] + dE[:, hs] * xs[:, hs]
        y = y_ref[...]
        yz = y * _silu(z_ref[...])
        for g in range(NG):
            gs = slice(g * GW, (g + 1) * GW)
            yg = yz[:, gs]
            r = lax.rsqrt(jnp.mean(yg * yg, axis=-1, keepdims=True) + EPS)
            yn_ref[:, gs] = (yg * r * ng_ref[:, gs]).astype(yn_ref.dtype)
        gather.finish()

    outs = pl.pallas_call(
        body, name=name, grid=(nc,),
        in_specs=[pl.BlockSpec((CH, D), lambda c: (c, COL_Z // D)),
                  pl.BlockSpec((CH, XBC), lambda c: (c, COL_XBC // XBC)),
                  pl.BlockSpec((KS, XBC), lambda c: (0, 0)),
                  pl.BlockSpec((1, XBC), lambda c: (0, 0)),
                  pl.BlockSpec((CH, 128), lambda c: (c, 0)),
                  pl.BlockSpec((8, 128), lambda c: (0, 0)),
                  pl.BlockSpec((1, D), lambda c: (0, 0))] + [HBM_SPEC] * nco,
        out_specs=[pl.BlockSpec((CH, XBC), lambda c: (c, 0)), pl.BlockSpec((CH, D), lambda c: (c, 0)),
                   pl.BlockSpec((CH, D), lambda c: (c, 0)),
                   pl.BlockSpec((1, NH * HP, NS), lambda c: (c, 0, 0))] + [HBM_SPEC] * nco,
        out_shape=[SDS((S, XBC), F32), SDS((S, D), F32), SDS((S, D), _MXU), SDS((nc, NH * HP, NS), F32)]
        + _gather_shapes(comm),
        scratch_shapes=[pltpu.VMEM((NH * HP, NS), F32), pltpu.VMEM((SSD_HALO + CH, XBC), F32)]
        + _RidingExchange.scratch(nco),
        compiler_params=_cp("arbitrary"))(proj, proj, cw, cb, dtr, sc, norm_g, *comm)
    return outs[:4], outs[4:]


def _ssd_bwd(dmix, y, proj, xbc_c, dtr, hprev, cw, sc, norm_g, comm=(), name="ssd_bwd"):
    S = proj.shape[0]
    nc = S // CH
    nco = len(comm)
    rev = lambda c: nc - 1 - c

    def body(*refs):
        dyn_ref, y_ref, z_ref, x_ref, xp_ref, dtr_ref, hp_ref, cw_ref, sc_ref, ng_ref = refs[:10]
        dz_ref, dx_ref, ddtr_ref, gcw_ref, gcb_ref, gsc_ref, gng_ref = refs[10 + nco:17 + nco]
        dh, dxd, cext = refs[17 + 2 * nco:20 + 2 * nco]
        exchange = _RidingExchange(refs[10:10 + nco], refs[17 + nco:17 + 2 * nco], refs[20 + 2 * nco:], nc)
        exchange.start()

        @pl.when(pl.program_id(0) == 0)
        def _():
            dh[...] = jnp.zeros_like(dh)
            cext[pl.ds(CH, SSD_HALO), :] = jnp.zeros((SSD_HALO, XBC), F32)
            gcw_ref[...] = jnp.zeros_like(gcw_ref)
            gcb_ref[...] = jnp.zeros_like(gcb_ref)
            gsc_ref[...] = jnp.zeros_like(gsc_ref)
            gng_ref[...] = jnp.zeros_like(gng_ref)

        e, et, tril, triu = _head_consts()
        xbc_c = x_ref[...]
        dtr = dtr_ref[...]
        dtb = sc_ref[0:1, :]
        xbc, xs, dt, A, cs, csT, csL, wdec, dtE, ecsE, wE, eL = _ssd_common(
            xbc_c, dtr, dtb, sc_ref[1:2, :], e, tril, triu)
        xd = xs * dtE
        xw = xs * wE
        dE = _dot(jnp.broadcast_to(sc_ref[2:3, :], (8, 128)), e, exact="b", passes=3)[0:1, :]
        eLcol = jnp.sum(et * eL, axis=1, keepdims=True)

        yv = y_ref[...]
        zv = z_ref[...]
        sz = _silu(zv)
        yz = yv * sz
        dyn = dyn_ref[...]
        dyz_parts = []
        for g in range(NG):
            gs = slice(g * GW, (g + 1) * GW)
            yg = yz[:, gs]
            r = lax.rsqrt(jnp.mean(yg * yg, axis=-1, keepdims=True) + EPS)
            yh = yg * r
            dn = dyn[:, gs]
            gng_ref[:, gs] += jnp.sum(dn * yh, axis=0, keepdims=True)
            gdn = dn * ng_ref[:, gs]
            dyz_parts.append(r * (gdn - yh * jnp.mean(yh * gdn, axis=-1, keepdims=True)))
        dyz = jnp.concatenate(dyz_parts, axis=1)
        dy = dyz * sz
        dz_ref[...] = (dyz * yv * _dsilu(zv)).astype(dz_ref.dtype)

        gsc_ref[2:3, :] += jnp.sum(_dot(dy * xs, et, exact="b", passes=3), axis=0, keepdims=True)
        dxs = dE * dy
        dzo = ecsE * dy
        dcs = jnp.zeros((CH, 128), F32)
        dcsL = jnp.zeros((1, 128), F32)
        ddt = jnp.zeros((CH, 128), F32)
        dB_parts, dC_parts, yoff_parts, dxw_parts = [], [], [], []
        for g in range(NG):
            Bg = xbc[:, NH * HP + g * NS: NH * HP + (g + 1) * NS]
            Cg = xbc[:, NH * HP + NG * NS + g * NS: NH * HP + NG * NS + (g + 1) * NS]
            gs = slice(g * GW, (g + 1) * GW)
            hg = hp_ref[0, gs, :]
            dhn = dh[gs, :]
            G = _dot(Cg, Bg, "nt")
            yoff_parts.append(ecsE[:, gs] * _dot(Cg, hg, "nt"))
            dC = _dot(dzo[:, gs], hg)
            dhp = _dot(dzo[:, gs], Cg, "tn") + eLcol[gs, :] * dhn
            t1 = jnp.sum(dhn * hg, axis=1, keepdims=True) * eLcol[gs, :]
            dcsL = dcsL + jnp.sum(et[gs, :] * t1, axis=0, keepdims=True)
            dxw_parts.append(_dot(Bg, dhn, "nt"))
            dB = _dot(xw[:, gs], dhn)
            dgsum = jnp.zeros((CH, CH), F32)
            for hh in range(NH // NG):
                h = g * (NH // NG) + hh
                hs = slice(h * HP, (h + 1) * HP)
                m = jnp.where(tril, jnp.exp(jnp.where(tril, cs[:, h:h + 1] - csT[h:h + 1, :], 0.0)), 0.0)
                sc = G * m
                dyh = dy[:, hs]
                dxd[:, hs] = _dot(sc, dyh, "tn")
                dsc = _dot(dyh, xd[:, hs], "nt")
                q = dsc * sc
                oh = (lax.broadcasted_iota(jnp.int32, (CH, 128), 1) == h).astype(F32)
                dcs = dcs + _dot(q, oh, exact="b") - _dot(q, oh, "tn", exact="b")
                dgsum = dgsum + dsc * m
            dC_parts.append(dC + _dot(dgsum, Bg))
            dB_parts.append(dB + _dot(dgsum, Cg, "tn"))
            dh[gs, :] = dhp
        yoff = jnp.concatenate(yoff_parts, axis=1)
        dxw = jnp.concatenate(dxw_parts, axis=1)
        dxdv = dxd[...]
        dcs = dcs + _dot(dy * yoff, et, exact="b")
        dxs = dxs + wE * dxw + dtE * dxdv
        dw = _dot(dxw * xs, et, exact="b")
        ddt = ddt + dw * jnp.exp(csL - cs) + _dot(dxdv * xs, et, exact="b")
        dcs = dcs - dw * wdec
        dcsL = dcsL + jnp.sum(dw * wdec, axis=0, keepdims=True)
        last = lax.broadcasted_iota(jnp.int32, (CH, 128), 0) == CH - 1
        dcs = dcs + jnp.where(last, dcsL, 0.0)
        da = _dot(triu, dcs, exact="a", passes=3)
        ddt = ddt + da * A
        gsc_ref[1:2, :] += jnp.sum(da * dt, axis=0, keepdims=True) * A
        valid = lax.broadcasted_iota(jnp.int32, (CH, 128), 1) < NH
        ddtr = jnp.where(valid, ddt * _sigmoid(dtr + dtb), 0.0)
        gsc_ref[0:1, :] += jnp.sum(ddtr, axis=0, keepdims=True)
        ddtr_ref[...] = ddtr.astype(ddtr_ref.dtype)
        dxbc = jnp.concatenate([dxs] + dB_parts + dC_parts, axis=1)
        dxc = dxbc * _dsilu(xbc_c)
        cext[pl.ds(0, CH), :] = dxc
        xp = xp_ref[...]
        acc = jnp.zeros((CH, XBC), F32)
        for k in range(KS):
            sh = cext[pl.ds(KS - 1 - k, CH), :]
            acc = acc + sh * cw_ref[k:k + 1, :]
            gcw_ref[k:k + 1, :] += jnp.sum(xp * sh, axis=0, keepdims=True)
        gcb_ref[...] += jnp.sum(dxc, axis=0, keepdims=True)
        dx_ref[...] = acc.astype(dx_ref.dtype)
        cext[pl.ds(CH, SSD_HALO), :] = cext[pl.ds(0, SSD_HALO), :]
        exchange.finish()

    vec = pl.BlockSpec((8, 128), lambda c: (0, 0))
    vecd = pl.BlockSpec((1, D), lambda c: (0, 0))
    cwsp = pl.BlockSpec((KS, XBC), lambda c: (0, 0))
    cbsp = pl.BlockSpec((1, XBC), lambda c: (0, 0))
    row = lambda w, j=0: pl.BlockSpec((CH, w), lambda c: (rev(c), j))
    outs = pl.pallas_call(
        body, name=name, grid=(nc,),
        in_specs=[row(D), row(D), row(D, COL_Z // D), row(XBC), row(XBC, COL_XBC // XBC), row(128),
                  pl.BlockSpec((1, NH * HP, NS), lambda c: (rev(c), 0, 0)), cwsp, vec, vecd] + [HBM_SPEC] * nco,
        out_specs=[row(D), row(XBC), row(128), cwsp, cbsp, vec, vecd] + [HBM_SPEC] * nco,
        out_shape=[SDS((S, D), _MXU), SDS((S, XBC), _MXU), SDS((S, 128), _MXU), SDS((KS, XBC), F32),
                   SDS((1, XBC), F32), SDS((8, 128), F32), SDS((1, D), F32)] + [SDS(p.shape, p.dtype) for p in comm],
        scratch_shapes=[pltpu.VMEM((NH * HP, NS), F32), pltpu.VMEM((CH, NH * HP), F32),
                        pltpu.VMEM((CH + SSD_HALO, XBC), F32)] + _RidingExchange.scratch(nco),
        compiler_params=_cp("arbitrary"))(dmix, y, proj, xbc_c, proj, dtr, hprev, cw, sc, norm_g, *comm)
    return outs[:7], outs[7:]


CONV_RT = 32


def _fill_phases(ext, ph, rows):
    for s in range(1, 8):
        ph[s - 1, pl.ds(0, rows), :] = ext[pl.ds(s, rows), :]


def _window(ext, ph, off, r0, ls):
    s = off % 8
    src = ext if s == 0 else ph.at[s - 1]
    return src[pl.ds(pl.multiple_of(off - s + r0, 8), CONV_RT), ls]


def _cf_fwd(proj, w, b, lg, lb, comm=(), name="cf_fwd", tb_cap=256):
    S = proj.shape[0]
    tb = _tile(S, tb_cap, 8)
    nb = S // tb
    nco = len(comm)

    def body(*refs):
        a_ref, g_ref, w_ref, b_ref, lg_ref, lb_ref = refs[:6]
        u1_ref, u_ref = refs[6 + nco:8 + nco]
        ext, ph = refs[8 + 2 * nco:10 + 2 * nco]
        gather = _RidingGather(refs[6:6 + nco], refs[8 + nco:8 + 2 * nco], refs[10 + 2 * nco:], nb,
                               [a.shape[0] for a in comm])
        gather.start()

        @pl.when(pl.program_id(0) == 0)
        def _():
            ext[pl.ds(0, CF_HALO), :] = jnp.zeros((CF_HALO, D), F32)

        ext[pl.ds(CF_HALO, tb), :] = a_ref[...] * _sigmoid(g_ref[...])
        _fill_phases(ext, ph, tb + CF_HALO - 8)

        def tile(i, carry):
            r0 = pl.multiple_of(i * CONV_RT, CONV_RT)
            for l in range(D // 128):
                ls = pl.ds(l * 128, 128)
                acc = jnp.broadcast_to(b_ref[:, ls], (CONV_RT, 128))
                for k in range(KC):
                    acc = acc + _window(ext, ph, CF_HALO - (KC - 1) + k, r0, ls) * w_ref[k:k + 1, ls]
                u1_ref[pl.ds(r0, CONV_RT), ls] = acc
            return carry

        lax.fori_loop(0, tb // CONV_RT, tile, 0)
        acc = u1_ref[...]
        mu = jnp.mean(acc, axis=-1, keepdims=True)
        xc = acc - mu
        r = lax.rsqrt(jnp.mean(xc * xc, axis=-1, keepdims=True) + EPS)
        u_ref[...] = _silu(xc * r * lg_ref[...] + lb_ref[...]).astype(u_ref.dtype)
        ext[pl.ds(0, CF_HALO), :] = ext[pl.ds(tb, CF_HALO), :]
        gather.finish()

    vec = pl.BlockSpec((1, D), lambda i: (0, 0))
    outs = pl.pallas_call(
        body, name=name, grid=(nb,),
        in_specs=[pl.BlockSpec((tb, D), lambda i: (i, COL_A // D)), pl.BlockSpec((tb, D), lambda i: (i, COL_G // D)),
                  pl.BlockSpec((KC, D), lambda i: (0, 0)), vec, vec, vec] + [HBM_SPEC] * nco,
        out_specs=[pl.BlockSpec((tb, D), lambda i: (i, 0)), pl.BlockSpec((tb, D), lambda i: (i, 0))] + [HBM_SPEC] * nco,
        out_shape=[SDS((S, D), F32), SDS((S, D), _MXU)] + _gather_shapes(comm),
        scratch_shapes=[pltpu.VMEM((CF_HALO + tb, D), F32), pltpu.VMEM((7, tb + CF_HALO - 8, D), F32)]
        + _RidingExchange.scratch(nco),
        compiler_params=_cp("arbitrary"))(proj, proj, w, b, lg, lb, *comm)
    return outs[:2], outs[2:]


def _cf_bwd(dmix, u1, proj, w, lg, lb, comm=(), name="cf_bwd", tb_cap=256):
    S = proj.shape[0]
    tb = _tile(S, tb_cap, 8)
    nb = S // tb
    nco = len(comm)
    rev = lambda i: nb - 1 - i

    def body(*refs):
        du_ref, u1_ref, a_ref, g_ref, w_ref, lg_ref, lb_ref = refs[:7]
        da_ref, dg_ref, dw_ref, db_ref, dlg_ref, dlb_ref = refs[7 + nco:13 + nco]
        ext, ph, u0s = refs[13 + 2 * nco:16 + 2 * nco]
        exchange = _RidingExchange(refs[7:7 + nco], refs[13 + nco:13 + 2 * nco], refs[16 + 2 * nco:], nb)
        exchange.start()

        @pl.when(pl.program_id(0) == 0)
        def _():
            ext[pl.ds(tb, CF_HALO), :] = jnp.zeros((CF_HALO, D), F32)
            dw_ref[...] = jnp.zeros_like(dw_ref)
            db_ref[...] = jnp.zeros_like(db_ref)
            dlg_ref[...] = jnp.zeros_like(dlg_ref)
            dlb_ref[...] = jnp.zeros_like(dlb_ref)

        u1 = u1_ref[...]
        mu = jnp.mean(u1, axis=-1, keepdims=True)
        xc = u1 - mu
        r = lax.rsqrt(jnp.mean(xc * xc, axis=-1, keepdims=True) + EPS)
        xh = xc * r
        lgv = lg_ref[...]
        du2 = du_ref[...] * _dsilu(xh * lgv + lb_ref[...])
        dlg_ref[...] += jnp.sum(du2 * xh, axis=0, keepdims=True)
        dlb_ref[...] += jnp.sum(du2, axis=0, keepdims=True)
        gd = du2 * lgv
        du1 = r * (gd - jnp.mean(gd, axis=-1, keepdims=True) - xh * jnp.mean(gd * xh, axis=-1, keepdims=True))
        db_ref[...] += jnp.sum(du1, axis=0, keepdims=True)
        ext[pl.ds(0, tb), :] = du1
        u0s[...] = a_ref[...] * _sigmoid(g_ref[...])
        _fill_phases(ext, ph, tb + CF_HALO - 8)

        def dx_tile(i, carry):
            r0 = pl.multiple_of(i * CONV_RT, CONV_RT)
            rows = pl.ds(r0, CONV_RT)
            for l in range(D // 128):
                ls = pl.ds(l * 128, 128)
                acc = jnp.zeros((CONV_RT, 128), F32)
                for k in range(KC):
                    acc = acc + _window(ext, ph, KC - 1 - k, r0, ls) * w_ref[k:k + 1, ls]
                sg = _sigmoid(g_ref[rows, ls])
                da_ref[rows, ls] = (acc * sg).astype(da_ref.dtype)
                dg_ref[rows, ls] = (acc * a_ref[rows, ls] * sg * (1.0 - sg)).astype(dg_ref.dtype)
            return carry

        lax.fori_loop(0, tb // CONV_RT, dx_tile, 0)
        for l in range(D // 128):
            ls = pl.ds(l * 128, 128)

            def dw_tile(i, accs, ls=ls):
                r0 = pl.multiple_of(i * CONV_RT, CONV_RT)
                u0t = u0s[pl.ds(r0, CONV_RT), ls]
                out = []
                for k in range(KC):
                    p = u0t * _window(ext, ph, KC - 1 - k, r0, ls)
                    out.append(accs[k] + ((p[0:8] + p[8:16]) + (p[16:24] + p[24:32])))
                return tuple(out)

            accs = lax.fori_loop(0, tb // CONV_RT, dw_tile, tuple(jnp.zeros((8, 128), F32) for _ in range(KC)))
            for k in range(KC):
                dw_ref[k:k + 1, ls] += jnp.sum(accs[k], axis=0, keepdims=True)
        ext[pl.ds(tb, CF_HALO), :] = ext[pl.ds(0, CF_HALO), :]
        exchange.finish()

    vec = pl.BlockSpec((1, D), lambda i: (0, 0))
    wsp = pl.BlockSpec((KC, D), lambda i: (0, 0))
    row = lambda j=0: pl.BlockSpec((tb, D), lambda i: (rev(i), j))
    outs = pl.pallas_call(
        body, name=name, grid=(nb,),
        in_specs=[row(1), row(), row(COL_A // D), row(COL_G // D), wsp, vec, vec] + [HBM_SPEC] * nco,
        out_specs=[row(), row(), wsp, vec, vec, vec] + [HBM_SPEC] * nco,
        out_shape=[SDS((S, D), _MXU), SDS((S, D), _MXU), SDS((KC, D), F32),
                   SDS((1, D), F32), SDS((1, D), F32), SDS((1, D), F32)] + [SDS(p.shape, p.dtype) for p in comm],
        scratch_shapes=[pltpu.VMEM((tb + CF_HALO, D), F32), pltpu.VMEM((7, tb + CF_HALO - 8, D), F32),
                        pltpu.VMEM((tb, D), F32)] + _RidingExchange.scratch(nco),
        compiler_params=_cp("arbitrary"))(dmix, u1, proj, proj, w, lg, lb, *comm)
    return outs[:6], outs[6:]


def _attn_fwd(q, kv, name="attn_fwd", tq_cap=512):
    S = q.shape[0]
    tq = _tile(S, tq_cap, 8)
    scale = XD ** -0.5

    def body(q_ref, kv_ref, o_ref):
        for h in range(XH):
            hs = slice(h * XD, (h + 1) * XD)
            s = _dot(q_ref[:, hs], kv_ref[:, hs], "nt") * scale
            s = s - jnp.max(s, axis=-1, keepdims=True)
            p = jnp.exp(s)
            p = p / jnp.sum(p, axis=-1, keepdims=True)
            o_ref[:, hs] = _dot(p, kv_ref[:, D + h * XD: D + (h + 1) * XD]).astype(o_ref.dtype)

    return pl.pallas_call(
        body, name=name, grid=(S // tq,),
        in_specs=[pl.BlockSpec((tq, D), lambda i: (i, 0)), pl.BlockSpec((MEM, 2 * D), lambda i: (0, 0))],
        out_specs=pl.BlockSpec((tq, D), lambda i: (i, 0)), out_shape=SDS((S, D), _MXU),
        compiler_params=_cp("parallel"))(q, kv)


def _attn_bwd(do, q, kv, name="attn_bwd", tq_cap=512):
    S = q.shape[0]
    tq = _tile(S, tq_cap, 8)
    scale = XD ** -0.5

    def body(do_ref, q_ref, kv_ref, dq_ref, dkv_ref):
        @pl.when(pl.program_id(0) == 0)
        def _():
            dkv_ref[...] = jnp.zeros_like(dkv_ref)

        for h in range(XH):
            hs = slice(h * XD, (h + 1) * XD)
            vs = slice(D + h * XD, D + (h + 1) * XD)
            qh = q_ref[:, hs]
            kh = kv_ref[:, hs]
            s = _dot(qh, kh, "nt") * scale
            s = s - jnp.max(s, axis=-1, keepdims=True)
            p = jnp.exp(s)
            p = p / jnp.sum(p, axis=-1, keepdims=True)
            doh = do_ref[:, hs]
            dp = _dot(doh, kv_ref[:, vs], "nt")
            ds = p * (dp - jnp.sum(dp * p, axis=-1, keepdims=True)) * scale
            dq_ref[:, hs] = _dot(ds, kh).astype(dq_ref.dtype)
            dkv_ref[:, hs] += _dot(ds, qh, "tn")
            dkv_ref[:, vs] += _dot(p, doh, "tn")

    return pl.pallas_call(
        body, name=name, grid=(S // tq,),
        in_specs=[pl.BlockSpec((tq, D), lambda i: (i, 0)), pl.BlockSpec((tq, D), lambda i: (i, 0)),
                  pl.BlockSpec((MEM, 2 * D), lambda i: (0, 0))],
        out_specs=[pl.BlockSpec((tq, D), lambda i: (i, 0)), pl.BlockSpec((MEM, 2 * D), lambda i: (0, 0))],
        out_shape=[SDS((S, D), _MXU), SDS((MEM, 2 * D), F32)],
        compiler_params=_cp("arbitrary"))(do, q, kv)


def _ffn_in(hf, wg_t, wu_t, name="ffn_in", tm_cap=512, tn_cap=1408):
    S, K = hf.shape
    N = wg_t.shape[0]
    tm, tn = _tile(S, tm_cap, 8), _tile(N, tn_cap)

    def body(a_ref, g_ref, u_ref, act_ref, gt_ref, up_ref):
        a = a_ref[...]
        gt = _dot(a, g_ref[...], "nt")
        up = _dot(a, u_ref[...], "nt")
        act_ref[...] = (_silu(gt) * up).astype(act_ref.dtype)
        gt_ref[...] = gt.astype(gt_ref.dtype)
        up_ref[...] = up.astype(up_ref.dtype)

    wsp = pl.BlockSpec((tn, K), lambda j, i: (j, 0))
    osp = pl.BlockSpec((tm, tn), lambda j, i: (i, j))
    return pl.pallas_call(
        body, name=name, grid=(N // tn, S // tm), in_specs=[pl.BlockSpec((tm, K), lambda j, i: (i, 0)), wsp, wsp],
        out_specs=[osp, osp, osp], out_shape=[SDS((S, N), _MXU)] * 3,
        compiler_params=_cp("parallel", "parallel"))(hf, wg_t, wu_t)


def _ffn_out_bwd(dx, w_down, gt, up, name="ffn_out_dx", tm_cap=512, tk_cap=1408):
    S, N = dx.shape
    K = w_down.shape[0]
    tm, tk = _tile(S, tm_cap, 8), _tile(K, tk_cap)

    def body(a_ref, b_ref, g_ref, u_ref, dg_ref, du_ref):
        d = _dot(a_ref[...], b_ref[...], "nt")
        gt = g_ref[...].astype(F32)
        s = _sigmoid(gt)
        dg_ref[...] = (d * u_ref[...].astype(F32) * (s * (1.0 + gt * (1.0 - s)))).astype(dg_ref.dtype)
        du_ref[...] = (d * gt * s).astype(du_ref.dtype)

    osp = pl.BlockSpec((tm, tk), lambda j, i: (i, j))
    return pl.pallas_call(
        body, name=name, grid=(K // tk, S // tm),
        in_specs=[pl.BlockSpec((tm, N), lambda j, i: (i, 0)), pl.BlockSpec((tk, N), lambda j, i: (j, 0)), osp, osp],
        out_specs=[osp, osp], out_shape=[SDS((S, K), _MXU)] * 2,
        compiler_params=_cp("parallel", "parallel"))(dx, w_down, gt, up)


AG_RIDE = (("w_out", "w_q", "w_kv", "w_o"), ("w_gate", "w_up", "w_down"))


def _local_step(x, mem, tgt, W, P, core=None, late=None):
    pair, got = {}, {}
    ride = [[late[n] for n in grp] if late is not None else [] for grp in AG_RIDE]

    def rs_pair(group):
        if core is None:
            return []
        ps = _rs_pair(group, GW, core)
        pair.update(zip(group, ps))
        return ps
    h = _rms_fwd(x, P["g_mix"], "rms_mix")
    proj = _mm_nn(h, W["main"], "in_proj", tn_cap=1152)
    dtr = _mm_nn(h, W["dt"], "in_proj_dt")
    (xbc_c, y, yn, hprev), bufs0 = _ssd_fwd(proj, P["conv4_w"], P["conv4_b"], dtr, P["sc"], P["ssd_norm_g"], comm=ride[0])
    (u1, u), bufs1 = _cf_fwd(proj, P["cf_w"], P["cf_b"], P["ln_g"], P["ln_b"], comm=ride[1])
    if late is not None:
        names = AG_RIDE[0] + AG_RIDE[1]
        W = dict(W, **_pack_late(dict(zip(names, _gather_finish_list(ride[0] + ride[1], list(bufs0) + list(bufs1))))))
    mix = jnp.concatenate([yn, u], axis=1)
    x1 = _mm_nn(mix, W["out"], "out_proj", add=x)
    hq = _rms_fwd(x1, P["g_xattn"], "rms_xattn")
    q = _mm_nn(hq, W["q"], "q_proj")
    mn = _rms_fwd(mem, P["g_mem"], "rms_mem")
    kv = _mm_nn(mn, W["kv"], "kv_proj")
    o = _attn_fwd(q, kv)
    x2 = _mm_nn(o, W["o"], "o_proj", add=x1)
    hf = _rms_fwd(x2, P["g_ffn"], "rms_ffn")
    act, gt, up = _ffn_in(hf, W["gate_t"], W["up_t"])
    x3 = _mm_nn(act, W["down"], "ffn_out", add=x2)
    loss, dx3, dx3b, g_final = _final_loss(x3, P["g_final"], tgt)
    GW, GP = {}, {"g_final": g_final}
    GW["down"] = _mm_tn(act, dx3b, "ffn_out_dw", tk_cap=1408, tn_cap=1024)
    dgt, dup = _ffn_out_bwd(dx3b, W["down"], gt, up)
    dhf = _mm_nn(dgt, W["gate_t"], "ffn_gate_dx", tm_cap=512)
    dhf = _mm_nn(dup, W["up_t"], "ffn_up_dx", add=dhf, tm_cap=512)
    GW["gate_t"] = _mm_tn(dgt, hf, "ffn_gate_dw", tk_cap=1408, tn_cap=1024)
    GW["up_t"] = _mm_tn(dup, hf, "ffn_up_dw", tk_cap=1408, tn_cap=1024)
    ffn_pieces = rs_pair(RS_GROUPS[0])
    dx2, dx2b, GP["g_ffn"] = _rms_bwd(x2, P["g_ffn"], dhf, dx3, "rms_ffn_bwd")
    do = _mm_nt(dx2b, W["o"], "o_proj_dx")
    GW["o"] = _mm_tn(o, dx2b, "o_proj_dw")
    dq, dkv = _attn_bwd(do, q, kv)
    dhq = _mm_nt(dq, W["q"], "q_proj_dx")
    GW["q"] = _mm_tn(hq, dq, "q_proj_dw")
    dkvb = dkv.astype(_MXU)
    GW["kv"] = _mm_tn(mn, dkvb, "kv_proj_dw", tm_cap=256)
    dmn = _mm_nt(dkvb, W["kv"], "kv_proj_dx")
    GP["g_mem"] = _rms_bwd(mem, P["g_mem"], dmn, None, "rms_mem_bwd")
    dx1, dx1b, GP["g_xattn"] = _rms_bwd(x1, P["g_xattn"], dhq, dx2, "rms_xattn_bwd")
    dmix = _mm_nt(dx1b, W["out"], "out_proj_dx")
    GW["out"] = _mm_tn(mix, dx1b, "out_proj_dw", tn_cap=1024)
    attn_pieces = rs_pair(RS_GROUPS[1])
    (da, dg, GP["cf_w"], GP["cf_b"], GP["ln_g"], GP["ln_b"]), came = _cf_bwd(
        dmix, u1, proj, P["cf_w"], P["ln_g"], P["ln_b"], comm=ffn_pieces)
    got.update(zip(RS_GROUPS[0], came))
    (dz, dxbc, ddtr, GP["conv4_w"], GP["conv4_b"], GP["sc"], GP["ssd_norm_g"]), came = _ssd_bwd(
        dmix, y, proj, xbc_c, dtr, hprev, P["conv4_w"], P["sc"], P["ssd_norm_g"], comm=attn_pieces)
    got.update(zip(RS_GROUPS[1], came))
    dproj = jnp.concatenate([dz, da, dg, dxbc], axis=1)
    GW["main"] = _mm_tn(h, dproj, "in_proj_dw", tn_cap=1152)
    GW["dt"] = _mm_tn(h, ddtr, "in_proj_dt_dw")
    in_pieces = rs_pair(RS_GROUPS[2])
    dh = _mm_nt(ddtr, W["dt"], "in_proj_dt_dx")
    if in_pieces:
        dh, came = _mm_nt(dproj, W["main"], "in_proj_dx", add=dh, tk_cap=512, comm=in_pieces)
        got.update(zip(RS_GROUPS[2], came))
    else:
        dh = _mm_nt(dproj, W["main"], "in_proj_dx", add=dh, tk_cap=512)
    grad_x, GP["g_mix"] = _rms_bwd(x, P["g_mix"], dh, dx1, "rms_mix_bwd", low=False)
    if core is None:
        return loss, grad_x, GW, GP
    return loss, grad_x, GW, GP, pair, got


Z_END, XBC_END, DT_END = NH * HP, NH * HP + XBC, NH * HP + XBC + NH


def _pad_to(a, rows=None, cols=None):
    r = 0 if rows is None else rows - a.shape[0]
    c = 0 if cols is None else cols - a.shape[1]
    return jnp.pad(a, ((0, r), (0, c)))


IN_W = DT_END + 2 * D
W_IN_SEGS = [(0, Z_END, "main", COL_Z), (Z_END, XBC_END, "main", COL_XBC), (XBC_END, DT_END, "dt", 0),
             (DT_END, DT_END + D, "main", COL_A), (DT_END + D, IN_W, "main", COL_G)]
BIG = [("w_in", True), ("w_out", False), ("w_q", False), ("w_kv", True), ("w_o", False), ("w_gate", False),
       ("w_up", False), ("w_down", False)]
TRANSPOSED = ("w_gate", "w_up")


def _ref_cols(pieces, a, b):
    cw = IN_W // 4
    out = []
    for j in range(4):
        lo, hi = max(a, j * cw), min(b, (j + 1) * cw)
        if lo < hi:
            out.append(pieces[j][:, lo - j * cw:hi - j * cw])
    return out


def _cat_cols(pieces):
    return jnp.concatenate([pieces[j] for j in range(4)], axis=1)


def _pack_in(w_in):
    main = jnp.concatenate(_ref_cols(w_in, 0, Z_END) + _ref_cols(w_in, DT_END, IN_W) + _ref_cols(w_in, Z_END, XBC_END), axis=1)
    return {"main": main, "dt": _pad_to(jnp.concatenate(_ref_cols(w_in, XBC_END, DT_END), axis=1), cols=128)}


def _pack_late(pc):
    rows = lambda n: pc[n].reshape(-1, pc[n].shape[-1])
    return {"out": rows("w_out"), "q": rows("w_q"), "kv": _cat_cols(pc["w_kv"]), "o": rows("w_o"),
            "gate_t": rows("w_gate"), "up_t": rows("w_up"), "down": rows("w_down")}


GW_KEY = {"w_gate": "gate_t", "w_up": "up_t", "w_kv": "kv", "w_out": "out", "w_q": "q", "w_o": "o", "w_down": "down"}
RS_GROUPS = (("w_down", "w_gate", "w_up"), ("w_out", "w_q", "w_kv", "w_o"), ("w_in",))


def _shard_grad(name, GW):
    if name == "w_in":
        cw = IN_W // 4
        pieces = []
        for j in range(4):
            parts = []
            for a, b, src, col in W_IN_SEGS:
                lo, hi = max(a, j * cw), min(b, (j + 1) * cw)
                if lo < hi:
                    parts.append(GW[src][:, col + lo - a:col + hi - a])
            pieces.append(jnp.concatenate(parts, axis=1))
        return jnp.stack(pieces)
    g = GW[GW_KEY[name]]
    if dict(BIG)[name]:
        cw = g.shape[1] // 4
        return jnp.stack([g[:, j * cw:(j + 1) * cw] for j in range(4)])
    return g.reshape(4, g.shape[0] // 4, g.shape[1])


def _rs_pair(names, GW, core):
    gs = [_shard_grad(n, GW) for n in names]
    halves = [g.reshape(4, 2, g.shape[1] // 2, g.shape[2]) for g in gs]
    theirs = _pair_split_list(halves, "rs_pair_send_" + names[0])
    return [_pair_sum(h, t, core, "rs_pair_sum_" + n) for h, t, n in zip(halves, theirs, names)]


def _stack_sc(dt_bias, a_log, d):
    return _pad_to(jnp.concatenate([dt_bias, a_log, d], axis=0), rows=8, cols=128)


COMM_PARAMS = pltpu.CompilerParams(vmem_limit_bytes=VMEM_LIMIT)


def _dma_sems(*counts):
    return [pltpu.SemaphoreType.DMA((n,)) for n in counts]


def _allgather_list(arrs, name):
    n = len(arrs)
    halved = [a.shape[0] % 16 == 0 for a in arrs]
    oshape = [(4, 2, a.shape[0] // 2, a.shape[1]) if h else (4, 1) + a.shape for a, h in zip(arrs, halved)]

    def body(*refs):
        srcs, outs = refs[:n], refs[n:2 * n]
        ici_send, ici_recv, own_send, own_recv, fwd_send, fwd_recv = refs[2 * n:]
        x, y, c = lax.axis_index("x"), lax.axis_index("y"), lax.axis_index("c")
        me = 2 * x + y
        sib = (x, y, 1 - c)
        peers = _chip_peers(x, y)

        def half(i, h):
            r = arrs[i].shape[0] // 2
            if not halved[i]:
                return srcs[i]
            return srcs[i].at[pl.ds(h * r if isinstance(h, int) else pl.multiple_of(h * r, 8), r)]

        ici, own, fwd = [], [], []
        for i in range(n):
            mine_h = c if halved[i] else 0
            for k, (px, py) in enumerate(peers):
                s = 3 * i + k
                ici.append(_remote(half(i, c), outs[i].at[me, mine_h], ici_send.at[s], ici_recv.at[s], (px, py, c)))
            for h in range(2 if halved[i] else 1):
                s = 2 * i + h
                own.append(_remote(half(i, h), outs[i].at[me, h], own_send.at[s], own_recv.at[s], sib))
        for cp in ici + own:
            cp.start()
        for i in range(n):
            if not halved[i]:
                continue
            for k, (px, py) in enumerate(peers):
                s = 3 * i + k
                got = outs[i].at[2 * px + py, c]
                _remote(half(i, c), got, ici_send.at[s], ici_recv.at[s], (px, py, c)).wait_recv()
                f = _remote(got, got, fwd_send.at[s], fwd_recv.at[s], sib)
                f.start()
                fwd.append(f)
        for i in range(n):
            for k, (px, py) in enumerate(peers):
                s = 3 * i + k
                if halved[i]:
                    _remote(half(i, c), outs[i].at[2 * px + py, 1 - c], fwd_send.at[s], fwd_recv.at[s], sib).wait_recv()
                else:
                    _remote(srcs[i], outs[i].at[2 * px + py, 0], ici_send.at[s], ici_recv.at[s], (px, py, c)).wait_recv()
            for h in range(2 if halved[i] else 1):
                s = 2 * i + h
                _remote(half(i, h), outs[i].at[me, h], own_send.at[s], own_recv.at[s], sib).wait_recv()
        for cp in ici + own + fwd:
            cp.wait_send()

    outs = pl.pallas_call(
        body, name=name, in_specs=[HBM_SPEC] * n, out_specs=[HBM_SPEC] * n,
        out_shape=[SDS(s, a.dtype) for s, a in zip(oshape, arrs)],
        scratch_shapes=_dma_sems(3 * n, 3 * n, 2 * n, 2 * n, 3 * n, 3 * n), compiler_params=COMM_PARAMS)(*arrs)
    return [o.reshape((4,) + a.shape) for o, a in zip(outs, arrs)]


def _pair_split_list(gs, name):
    n = len(gs)

    def body(*refs):
        srcs, outs = refs[:n], refs[n:2 * n]
        send_sems, recv_sems = refs[2 * n:]
        x, y, c = lax.axis_index("x"), lax.axis_index("y"), lax.axis_index("c")
        sib = (x, y, 1 - c)
        sends = [_remote(srcs[i].at[j, 1 - c], outs[i].at[j], send_sems.at[4 * i + j], recv_sems.at[4 * i + j], sib)
                 for i in range(n) for j in range(4)]
        for cp in sends:
            cp.start()
        for cp in sends:
            cp.wait_recv()
        for cp in sends:
            cp.wait_send()

    return pl.pallas_call(
        body, name=name, in_specs=[HBM_SPEC] * n, out_specs=[HBM_SPEC] * n,
        out_shape=[SDS((4,) + g.shape[2:], g.dtype) for g in gs],
        scratch_shapes=_dma_sems(4 * n, 4 * n), compiler_params=COMM_PARAMS)(*gs)


def _gather_finish_list(shards, bufs, name="allgather_finish"):
    n = len(shards)

    def body(*refs):
        srcs, outs = refs[:n], refs[2 * n:3 * n]
        own_send, own_recv, fwd_send, fwd_recv = refs[3 * n:]
        x, y, c = lax.axis_index("x"), lax.axis_index("y"), lax.axis_index("c")
        me = 2 * x + y
        sib = (x, y, 1 - c)
        sends, recvs = [], []
        for i in range(n):
            for h in range(2):
                own = _remote(_rows_half(srcs[i], shards[i].shape[0], h), outs[i].at[me, h],
                              own_send.at[2 * i + h], own_recv.at[2 * i + h], sib)
                sends.append(own)
                recvs.append(own)
            for k, (px, py) in enumerate(_chip_peers(x, y)):
                got, s = outs[i].at[2 * px + py, c], 3 * i + k
                sends.append(_remote(got, got, fwd_send.at[s], fwd_recv.at[s], sib))
                recvs.append(_remote(got, outs[i].at[2 * px + py, 1 - c], fwd_send.at[s], fwd_recv.at[s], sib))
        for cp in sends:
            cp.start()
        for cp in recvs:
            cp.wait_recv()
        for cp in sends:
            cp.wait_send()

    outs = pl.pallas_call(
        body, name=name, in_specs=[HBM_SPEC] * (2 * n), out_specs=[HBM_SPEC] * n,
        out_shape=[SDS(b.shape, b.dtype) for b in bufs], input_output_aliases={n + i: i for i in range(n)},
        scratch_shapes=_dma_sems(2 * n, 2 * n, 3 * n, 3 * n), compiler_params=COMM_PARAMS)(*shards, *bufs)
    return [o.reshape((4,) + a.shape) for o, a in zip(outs, shards)]


JOIN_SPLIT = 4


def _pair_join_list(bufs, name="rs_pair_join"):
    n = len(bufs)

    def body(*refs):
        outs = refs[n:2 * n]
        send_sems, recv_sems = refs[2 * n:]
        x, y, c = lax.axis_index("x"), lax.axis_index("y"), lax.axis_index("c")
        sib = (x, y, 1 - c)
        sends, recvs = [], []
        for i in range(n):
            rc = bufs[i].shape[1] // JOIN_SPLIT
            for q in range(JOIN_SPLIT):
                k = JOIN_SPLIT * i + q
                rows = pl.ds(q * rc, rc)
                sends.append(_remote(outs[i].at[c, rows], outs[i].at[c, rows], send_sems.at[k], recv_sems.at[k], sib))
                recvs.append(_remote(outs[i].at[c, rows], outs[i].at[1 - c, rows], send_sems.at[k], recv_sems.at[k], sib))
        for cp in sends:
            cp.start()
        for cp in recvs:
            cp.wait_recv()
        for cp in sends:
            cp.wait_send()

    return pl.pallas_call(
        body, name=name, in_specs=[HBM_SPEC] * n, out_specs=[HBM_SPEC] * n,
        out_shape=[SDS(b.shape, b.dtype) for b in bufs], input_output_aliases={i: i for i in range(n)},
        scratch_shapes=_dma_sems(JOIN_SPLIT * n, JOIN_SPLIT * n), compiler_params=COMM_PARAMS)(*bufs)


def _pair_sum(g, theirs, core, name):
    _, _, r, c = g.shape

    def body(core_ref, g_ref, t_ref, o_ref):
        o_ref[...] = (g_ref[...] + t_ref[...]).astype(o_ref.dtype)

    spec = pltpu.PrefetchScalarGridSpec(
        num_scalar_prefetch=1, grid=(4,),
        in_specs=[pl.BlockSpec((None, None, r, c), lambda j, core_ref: (j, core_ref[0], 0, 0)),
                  pl.BlockSpec((None, r, c), lambda j, core_ref: (j, 0, 0))],
        out_specs=pl.BlockSpec((None, r, c), lambda j, core_ref: (j, 0, 0)))
    return pl.pallas_call(body, name=name, grid_spec=spec, out_shape=SDS((4, r, c), BF16),
                          compiler_params=_cp("parallel"))(core, g, theirs)


def _chip_sum(own, got, where, name):
    _, r, c = own.shape
    tr = r // 2

    def body(w_ref, a_ref, b1_ref, b2_ref, b3_ref, o_ref):
        o_ref[...] = ((a_ref[...].astype(F32) + b1_ref[...].astype(F32)) + b2_ref[...].astype(F32)) + b3_ref[...].astype(F32)

    piece = lambda k: pl.BlockSpec((None, tr, c), lambda i, w_ref: ((w_ref[0] + k) % 4, i, 0))
    spec = pltpu.PrefetchScalarGridSpec(
        num_scalar_prefetch=1, grid=(r // tr,), in_specs=[piece(0), piece(1), piece(2), piece(3)],
        out_specs=pl.BlockSpec((None, tr, c), lambda i, w_ref: (w_ref[1], i, 0)))
    return pl.pallas_call(body, name=name, grid_spec=spec, out_shape=SDS((2, r, c), F32),
                          compiler_params=_cp("parallel"))(where, own, got, got, got)


def _adam_math(w, g, m, v):
    bc1 = 1.0 - ADAM_B1 ** ADAM_STEP
    bc2 = 1.0 - ADAM_B2 ** ADAM_STEP
    mn = ADAM_B1 * m + (1.0 - ADAM_B1) * g
    vn = ADAM_B2 * v + (1.0 - ADAM_B2) * (g * g)
    return -ADAM_LR * ((mn / bc1) / (jnp.sqrt(vn / bc2) + ADAM_EPS) + ADAM_WD * w), mn, vn


PACK_COLS = XBC
PACK = {"g_mix": (0, 1, D), "g_xattn": (1, 1, D), "g_mem": (2, 1, D), "g_ffn": (3, 1, D), "g_final": (4, 1, D),
        "ssd_norm_g": (5, 1, D), "cf_b": (6, 1, D), "ln_g": (7, 1, D), "ln_b": (8, 1, D), "conv4_b": (9, 1, XBC),
        "conv4_w": (10, KS, XBC), "sc": (16, 8, 128), "cf_w": (24, KC, D), "loss": (55, 1, 128)}
PACK_ROWS = 56
SMALL_ADAM = ["g_mix", "g_xattn", "g_mem", "g_ffn", "g_final", "ssd_norm_g", "cf_b", "ln_g", "ln_b", "conv4_b", "sc"]


def _small_allreduce_adamw(grads, wts, mom, var, name="allreduce_small"):
    gk = list(PACK)
    ng, na = len(gk), len(SMALL_ADAM)

    def body(*refs):
        g_in = refs[:ng]
        w_in, m_in, v_in = (refs[ng + i * na: ng + (i + 1) * na] for i in range(3))
        o = refs[ng + 3 * na:]
        g_out = o[:ng]
        d_out, m_out, v_out = (o[ng + i * na: ng + (i + 1) * na] for i in range(3))
        pack, buf, acc, send_sems, recv_sems = o[ng + 3 * na:]
        x, y, c = lax.axis_index("x"), lax.axis_index("y"), lax.axis_index("c")
        me = 4 * x + 2 * y + c
        pack[...] = jnp.zeros_like(pack)
        for i, k in enumerate(gk):
            r0, nr, nc = PACK[k]
            pack[r0:r0 + nr, 0:nc] = g_in[i][...]
        peers = [(x, y, 1 - c)] + [(px, py, pc) for px, py in _chip_peers(x, y) for pc in (c, 1 - c)]
        sends = [_remote(pack, buf.at[me], send_sems.at[k], recv_sems.at[k], dev) for k, dev in enumerate(peers)]
        for cp in sends:
            cp.start()
        buf[me] = pack[...]
        for k, (px, py, pc) in enumerate(peers):
            _remote(pack, buf.at[4 * px + 2 * py + pc], send_sems.at[k], recv_sems.at[k], (px, py, pc)).wait_recv()
        for cp in sends:
            cp.wait_send()
        tot = buf[0]
        for i in range(1, 8):
            tot = tot + buf[i]
        acc[...] = tot
        for i, k in enumerate(gk):
            r0, nr, nc = PACK[k]
            g_out[i][...] = acc[r0:r0 + nr, 0:nc]
        for i, k in enumerate(SMALL_ADAM):
            r0, nr, nc = PACK[k]
            d_out[i][...], m_out[i][...], v_out[i][...] = _adam_math(
                w_in[i][...], acc[r0:r0 + nr, 0:nc], m_in[i][...], v_in[i][...])

    args = [grads[k] for k in gk] + [d[k] for d in (wts, mom, var) for k in SMALL_ADAM]
    shp = lambda k: SDS((PACK[k][1], PACK[k][2]), F32)
    vm = pl.BlockSpec(memory_space=pltpu.VMEM)
    outs = pl.pallas_call(
        body, name=name, in_specs=[vm] * len(args), out_specs=[vm] * (ng + 3 * na),
        out_shape=[shp(k) for k in gk] + [shp(k) for _ in range(3) for k in SMALL_ADAM],
        scratch_shapes=[pltpu.VMEM((PACK_ROWS, PACK_COLS), F32), pltpu.VMEM((8, PACK_ROWS, PACK_COLS), F32),
                        pltpu.VMEM((PACK_ROWS, PACK_COLS), F32)] + _dma_sems(7, 7),
        compiler_params=COMM_PARAMS)(*args)
    red = dict(zip(gk, outs[:ng]))
    parts = [dict(zip(SMALL_ADAM, outs[ng + i * na: ng + (i + 1) * na])) for i in range(3)]
    return red, parts[0], parts[1], parts[2]


def _adamw_cols(w, gfull, m, v, chip, name):
    _, R, C = w.shape

    def body(w_idx, w_ref, g_ref, m_ref, v_ref, go_ref, d_ref, mo_ref, vo_ref):
        go_ref[...] = g_ref[...]
        d_ref[...], mo_ref[...], vo_ref[...] = _adam_math(w_ref[...], g_ref[...], m_ref[...], v_ref[...])

    blk = pl.BlockSpec((None, R, C), lambda i, w_idx: (0, 0, 0))
    spec = pltpu.PrefetchScalarGridSpec(
        num_scalar_prefetch=1, grid=(1,),
        in_specs=[blk, pl.BlockSpec((R, C), lambda i, w_idx: (0, w_idx[0])), blk, blk], out_specs=[blk] * 4)
    return pl.pallas_call(body, name=name, grid_spec=spec, out_shape=[SDS((1, R, C), F32)] * 4,
                          compiler_params=_cp("arbitrary"))(chip, w, gfull, m, v)


def _adamw(w, g, m, v, name):
    _, R, C = w.shape
    half = R // 2
    tr = _tile(half, max(8, (2 ** 17 // C) // 8 * 8), 8)
    nh = half // tr

    def body(w_ref, g_ref, m_ref, v_ref, go_ref, d_ref, mo_ref, vo_ref):
        go_ref[...] = g_ref[...]
        d_ref[...], mo_ref[...], vo_ref[...] = _adam_math(w_ref[...], g_ref[...], m_ref[...], v_ref[...])

    blk = pl.BlockSpec((None, tr, C), lambda i: (0, i, 0))
    gblk = pl.BlockSpec((None, tr, C), lambda i: (i // nh, i % nh, 0))
    return pl.pallas_call(body, name=name, grid=(R // tr,), in_specs=[blk, gblk, blk, blk], out_specs=[blk] * 4,
                          out_shape=[SDS((1, R, C), F32)] * 4, compiler_params=_cp("parallel"))(w, g, m, v)


WEIGHT_NAMES = ["norm_mix_g", "w_in", "ssd_conv_w", "ssd_conv_b", "ssd_dt_bias", "ssd_A_log", "ssd_D", "ssd_norm_g",
                "cf_conv_w", "cf_conv_b", "cf_ln_g", "cf_ln_b", "w_out", "norm_xattn_g", "norm_mem_g", "w_q", "w_kv",
                "w_o", "norm_ffn_g", "w_gate", "w_up", "w_down", "norm_final_g"]
VEC_REF = [("norm_mix_g", "g_mix"), ("norm_xattn_g", "g_xattn"), ("norm_mem_g", "g_mem"), ("norm_ffn_g", "g_ffn"),
           ("norm_final_g", "g_final"), ("ssd_norm_g", "ssd_norm_g"), ("cf_conv_b", "cf_b"), ("cf_ln_g", "ln_g"),
           ("cf_ln_b", "ln_b"), ("ssd_conv_b", "conv4_b")]
SC_REF = ["ssd_dt_bias", "ssd_A_log", "ssd_D"]


def _small_side(get):
    d = {k: get(ref_name).reshape(1, -1) for ref_name, k in VEC_REF}
    d["sc"] = _stack_sc(*[get(n) for n in SC_REF])
    return d


def kernel(x, mem, norm_mix_g, w_in, ssd_conv_w, ssd_conv_b, ssd_dt_bias, ssd_A_log, ssd_D, ssd_norm_g, cf_conv_w, cf_conv_b, cf_ln_g, cf_ln_b, w_out, norm_xattn_g, norm_mem_g, w_q, w_kv, w_o, norm_ffn_g, w_gate, w_up, w_down, norm_final_g, loss_target, m_norm_mix_g, m_w_in, m_ssd_conv_w, m_ssd_conv_b, m_ssd_dt_bias, m_ssd_A_log, m_ssd_D, m_ssd_norm_g, m_cf_conv_w, m_cf_conv_b, m_cf_ln_g, m_cf_ln_b, m_w_out, m_norm_xattn_g, m_norm_mem_g, m_w_q, m_w_kv, m_w_o, m_norm_ffn_g, m_w_gate, m_w_up, m_w_down, m_norm_final_g, v_norm_mix_g, v_w_in, v_ssd_conv_w, v_ssd_conv_b, v_ssd_dt_bias, v_ssd_A_log, v_ssd_D, v_ssd_norm_g, v_cf_conv_w, v_cf_conv_b, v_cf_ln_g, v_cf_ln_b, v_w_out, v_norm_xattn_g, v_norm_mem_g, v_w_q, v_w_kv, v_w_o, v_norm_ffn_g, v_w_gate, v_w_up, v_w_down, v_norm_final_g):
    env = dict(locals())
    view = lambda n, a: a.transpose(0, 2, 1) if n in TRANSPOSED else a
    wts = {n: view(n, env[n]) for n in WEIGHT_NAMES}
    mom = {n: view(n, env["m_" + n]) for n in WEIGHT_NAMES}
    var = {n: view(n, env["v_" + n]) for n in WEIGHT_NAMES}
    chip = (2 * lax.axis_index("x") + lax.axis_index("y")).astype(jnp.int32).reshape(1)
    core = lax.axis_index("c").astype(jnp.int32).reshape(1)
    where = jnp.concatenate([chip, core])
    big = [n for n, _ in BIG]

    w_in_g, conv4_g, cf_g = _allgather_list([w_in[0].astype(BF16), ssd_conv_w[0], cf_conv_w[0]], "allgather_first")
    W = _pack_in(w_in_g)
    P = _small_side(lambda n: wts[n])
    P["conv4_w"], P["cf_w"] = _cat_cols(conv4_g), _cat_cols(cf_g)
    late = {n: wts[n][0].astype(BF16) for grp in AG_RIDE for n in grp}

    loss, grad_x, GW, GP, pair, got = _local_step(x[0], mem[0], loss_target[0], W, P, core, late)
    joined = _pair_join_list([_chip_sum(pair[n], got[n], where, "rs_chip_sum_" + n) for n in big])
    gshard = dict(zip(big, joined))

    small = dict(GP)
    small["loss"] = loss
    red, sd, sm, sv = _small_allreduce_adamw(small, {k: P[k] for k in SMALL_ADAM}, _small_side(lambda n: mom[n]),
                                             _small_side(lambda n: var[n]))
    grads, delta, new_m, new_v = {}, {}, {}, {}
    for ref_name, k in VEC_REF:
        shp = wts[ref_name].shape
        for dst, src in ((grads, red), (delta, sd), (new_m, sm), (new_v, sv)):
            dst[ref_name] = src[k].reshape(shp)
    for row, ref_name in enumerate(SC_REF):
        for dst, src in ((grads, red), (delta, sd), (new_m, sm), (new_v, sv)):
            dst[ref_name] = src["sc"][row:row + 1, :NH]

    for n, k in (("ssd_conv_w", "conv4_w"), ("cf_conv_w", "cf_w")):
        grads[n], delta[n], new_m[n], new_v[n] = _adamw_cols(wts[n], red[k], mom[n], var[n], chip, "adamw_" + n)
    for n in big:
        outs = _adamw(wts[n], gshard[n], mom[n], var[n], "adamw_" + n)
        grads[n], delta[n], new_m[n], new_v[n] = [view(n, o) for o in outs]

    return (red["loss"][0, 0], grad_x[None], *[grads[n] for n in WEIGHT_NAMES], *[delta[n] for n in WEIGHT_NAMES],
            *[new_m[n] for n in WEIGHT_NAMES], *[new_v[n] for n in WEIGHT_NAMES])
```

```python
import functools
import math

import jax
import jax.numpy as jnp
from jax import lax
from jax.experimental import pallas as pl
from jax.experimental.pallas import tpu as pltpu

F32 = jnp.float32
BF16 = jnp.bfloat16
_MXU = BF16

D = 1024
MEM = 256
NH, HP, NG, NS = 16, 64, 2, 128
GW = NH * HP // NG
CH = 128
XBC = NH * HP + 2 * NG * NS
KS, KC = 4, 31
XH, XD = 4, 256
DFF = 2816
EPS = 1e-6
COL_Z, COL_A, COL_G, COL_XBC, MAINW = 0, 1024, 2048, 3072, 4608
VMEM_LIMIT = 56 * 2 ** 20

ADAM_LR, ADAM_B1, ADAM_B2, ADAM_EPS, ADAM_WD, ADAM_STEP = 0.001, 0.9, 0.999, 1e-08, 0.01, 10

SDS = jax.ShapeDtypeStruct
MESHID = pl.DeviceIdType.MESH


def _cp(*sem):
    return pltpu.CompilerParams(dimension_semantics=sem, vmem_limit_bytes=VMEM_LIMIT)


def _tile(n, cap, unit=128):
    if n <= cap:
        return n
    best = None
    for t in range(unit, cap + 1, unit):
        if n % t == 0:
            best = t
    assert best is not None, (n, cap)
    return best


def _sigmoid(x):
    return 1.0 / (1.0 + jnp.exp(-x))


def _silu(x):
    return x * _sigmoid(x)


def _dsilu(x):
    s = _sigmoid(x)
    return s * (1.0 + x * (1.0 - s))


def _softplus(x):
    return jnp.maximum(x, 0.0) + jnp.log(1.0 + jnp.exp(-jnp.abs(x)))


def _split_bf16(x, passes):
    parts, r = [], x.astype(F32)
    for _ in range(passes):
        p = r.astype(BF16)
        parts.append(p)
        r = r - p.astype(F32)
    return parts


def _dot(a, b, dims=None, exact=None, passes=2):
    dn = {None: (((1,), (0,)), ((), ())), "nt": (((1,), (1,)), ((), ())), "tn": (((0,), (0,)), ((), ()))}[dims]
    if exact is None:
        return lax.dot_general(a.astype(_MXU), b.astype(_MXU), dn, preferred_element_type=F32)
    if exact == "a":
        terms = [(a.astype(BF16), p) for p in _split_bf16(b, passes)]
    else:
        terms = [(p, b.astype(BF16)) for p in _split_bf16(a, passes)]
    out = None
    for lhs, rhs in terms:
        d = lax.dot_general(lhs, rhs, dn, preferred_element_type=F32)
        out = d if out is None else out + d
    return out


def _mm_nn(a, b, name, add=None, out_dtype=F32, tm_cap=1024, tn_cap=1408, gather=()):
    M, K = a.shape
    _, N = b.shape
    tm, tn = _tile(M, tm_cap, 8), _tile(N, tn_cap)
    nin, nco = 2 + (add is not None), len(gather)
    grid = (N // tn, M // tm)

    def body(*refs):
        a_ref, b_ref, o_ref = refs[0], refs[1], refs[nin + nco]
        rider = _RidingGather(refs[nin:nin + nco], refs[nin + nco + 1:nin + 2 * nco + 1], refs[nin + 2 * nco + 1:],
                              grid, [s.shape[0] for s in gather])
        rider.start()
        acc = _dot(a_ref[...], b_ref[...])
        if add is not None:
            acc = acc + refs[2][...]
        o_ref[...] = acc.astype(o_ref.dtype)
        rider.finish()

    in_specs = [pl.BlockSpec((tm, K), lambda j, i: (i, 0)), pl.BlockSpec((K, tn), lambda j, i: (0, j))]
    args = [a, b]
    if add is not None:
        in_specs.append(pl.BlockSpec((tm, tn), lambda j, i: (i, j)))
        args.append(add)
    order = ("arbitrary", "arbitrary") if gather else ("parallel", "parallel")
    outs = pl.pallas_call(
        body, name=name, grid=grid, in_specs=in_specs + [HBM_SPEC] * nco,
        out_specs=[pl.BlockSpec((tm, tn), lambda j, i: (i, j))] + [HBM_SPEC] * nco,
        out_shape=[SDS((M, N), out_dtype)] + _gather_shapes(gather),
        scratch_shapes=_RidingExchange.scratch(nco), compiler_params=_cp(*order))(*args, *gather)
    return (outs[0], outs[1:]) if gather else outs[0]


def _mm_nt(a, b, name, add=None, out_dtype=F32, tm_cap=512, tk_cap=1024, comm=()):
    M, N = a.shape
    K = b.shape[0]
    tm, tk = _tile(M, tm_cap, 8), _tile(K, tk_cap)
    nin, nco = 2 + (add is not None), len(comm)
    grid = (K // tk, M // tm)

    def body(*refs):
        a_ref, b_ref, o_ref = refs[0], refs[1], refs[nin + nco]
        exchange = _RidingExchange(refs[nin:nin + nco], refs[nin + nco + 1:nin + 2 * nco + 1],
                                   refs[nin + 2 * nco + 1:], grid)
        exchange.start()
        acc = _dot(a_ref[...], b_ref[...], "nt")
        if add is not None:
            acc = acc + refs[2][...]
        o_ref[...] = acc.astype(o_ref.dtype)
        exchange.finish()

    in_specs = [pl.BlockSpec((tm, N), lambda j, i: (i, 0)), pl.BlockSpec((tk, N), lambda j, i: (j, 0))]
    args = [a, b]
    if add is not None:
        in_specs.append(pl.BlockSpec((tm, tk), lambda j, i: (i, j)))
        args.append(add)
    order = ("arbitrary", "arbitrary") if comm else ("parallel", "parallel")
    outs = pl.pallas_call(
        body, name=name, grid=grid, in_specs=in_specs + [HBM_SPEC] * nco,
        out_specs=[pl.BlockSpec((tm, tk), lambda j, i: (i, j))] + [HBM_SPEC] * nco,
        out_shape=[SDS((M, K), out_dtype)] + [SDS(p.shape, p.dtype) for p in comm],
        scratch_shapes=_RidingExchange.scratch(nco), compiler_params=_cp(*order))(*args, *comm)
    return (outs[0], outs[1:]) if comm else outs[0]


def _mm_tn(a, b, name, tm_cap=1024, tk_cap=512, tn_cap=1408):
    M, K = a.shape
    _, N = b.shape
    tm, tk, tn = _tile(M, tm_cap, 8), _tile(K, tk_cap), _tile(N, tn_cap)

    def body(a_ref, b_ref, o_ref):
        @pl.when(pl.program_id(2) == 0)
        def _():
            o_ref[...] = jnp.zeros_like(o_ref)

        o_ref[...] += _dot(a_ref[...], b_ref[...], "tn")

    return pl.pallas_call(
        body, name=name, grid=(K // tk, N // tn, M // tm),
        in_specs=[pl.BlockSpec((tm, tk), lambda k, n, m: (m, k)), pl.BlockSpec((tm, tn), lambda k, n, m: (m, n))],
        out_specs=pl.BlockSpec((tk, tn), lambda k, n, m: (k, n)), out_shape=SDS((K, N), F32),
        compiler_params=_cp("parallel", "parallel", "arbitrary"))(a, b)


def _rms_fwd(x, g, name, tb_cap=512):
    S, Dm = x.shape
    tb = _tile(S, tb_cap, 8)

    def body(x_ref, g_ref, o_ref):
        xv = x_ref[...]
        r = lax.rsqrt(jnp.mean(xv * xv, axis=-1, keepdims=True) + EPS)
        o_ref[...] = (xv * r * g_ref[...]).astype(o_ref.dtype)

    return pl.pallas_call(
        body, name=name, grid=(S // tb,),
        in_specs=[pl.BlockSpec((tb, Dm), lambda i: (i, 0)), pl.BlockSpec((1, Dm), lambda i: (0, 0))],
        out_specs=pl.BlockSpec((tb, Dm), lambda i: (i, 0)), out_shape=SDS((S, Dm), _MXU),
        compiler_params=_cp("parallel"))(x, g)


def _rms_bwd(x, g, dh, dres, name, tb_cap=512, low=True):
    S, Dm = x.shape
    tb = _tile(S, tb_cap, 8)
    need_dx = dres is not None

    def body(x_ref, g_ref, dh_ref, *rest):
        dg_ref = rest[-1]
        xv = x_ref[...]
        r = lax.rsqrt(jnp.mean(xv * xv, axis=-1, keepdims=True) + EPS)
        xh = xv * r
        dy = dh_ref[...].astype(F32)

        @pl.when(pl.program_id(0) == 0)
        def _():
            dg_ref[...] = jnp.zeros_like(dg_ref)

        dg_ref[...] += jnp.sum(dy * xh, axis=0, keepdims=True)
        if need_dx:
            gdy = dy * g_ref[...]
            dx = r * (gdy - xh * jnp.mean(xh * gdy, axis=-1, keepdims=True))
            tot = rest[0][...] + dx
            rest[1][...] = tot
            if low:
                rest[2][...] = tot.astype(rest[2].dtype)

    row = pl.BlockSpec((tb, Dm), lambda i: (i, 0))
    vec = pl.BlockSpec((1, Dm), lambda i: (0, 0))
    if need_dx:
        outs = [SDS((S, Dm), F32)] + ([SDS((S, Dm), _MXU)] if low else [])
        return pl.pallas_call(
            body, name=name, grid=(S // tb,), in_specs=[row, vec, row, row], out_specs=[row] * len(outs) + [vec],
            out_shape=outs + [SDS((1, Dm), F32)], compiler_params=_cp("arbitrary"))(x, g, dh, dres)
    return pl.pallas_call(
        body, name=name, grid=(S // tb,), in_specs=[row, vec, row], out_specs=vec,
        out_shape=SDS((1, Dm), F32), compiler_params=_cp("arbitrary"))(x, g, dh)


def _final_loss(x, g, tgt, name="final_loss", tb_cap=512):
    S, Dm = x.shape
    tb = _tile(S, tb_cap, 8)

    def body(x_ref, g_ref, t_ref, loss_ref, dx_ref, dxl_ref, dg_ref):
        xv = x_ref[...]
        gv = g_ref[...]
        r = lax.rsqrt(jnp.mean(xv * xv, axis=-1, keepdims=True) + EPS)
        xh = xv * r
        e = xh * gv - t_ref[...]

        @pl.when(pl.program_id(0) == 0)
        def _():
            loss_ref[...] = jnp.zeros_like(loss_ref)
            dg_ref[...] = jnp.zeros_like(dg_ref)

        loss_ref[...] += 0.5 * jnp.sum(jnp.mean(e * e, axis=-1, keepdims=True))
        dy = e * (1.0 / Dm)
        dg_ref[...] += jnp.sum(dy * xh, axis=0, keepdims=True)
        gdy = dy * gv
        dx = r * (gdy - xh * jnp.mean(xh * gdy, axis=-1, keepdims=True))
        dx_ref[...] = dx
        dxl_ref[...] = dx.astype(dxl_ref.dtype)

    row = pl.BlockSpec((tb, Dm), lambda i: (i, 0))
    vec = pl.BlockSpec((1, Dm), lambda i: (0, 0))
    return pl.pallas_call(
        body, name=name, grid=(S // tb,), in_specs=[row, vec, row],
        out_specs=[pl.BlockSpec((1, 128), lambda i: (0, 0)), row, row, vec],
        out_shape=[SDS((1, 128), F32), SDS((S, Dm), F32), SDS((S, Dm), _MXU), SDS((1, Dm), F32)],
        compiler_params=_cp("arbitrary"))(x, g, tgt)


SSD_HALO = 8
CF_HALO = 32

HBM_SPEC = pl.BlockSpec(memory_space=pl.ANY)


def _chip_peers(x, y):
    return [(1 - x, y), (x, 1 - y), (1 - x, 1 - y)]


def _remote(src, dst, send_sem, recv_sem, dev):
    return pltpu.make_async_remote_copy(src_ref=src, dst_ref=dst, send_sem=send_sem, recv_sem=recv_sem,
                                        device_id=dev, device_id_type=MESHID)


def _scatter_copies(srcs, outs, send_sems, recv_sems):
    x, y, c = lax.axis_index("x"), lax.axis_index("y"), lax.axis_index("c")
    me = 2 * x + y
    sends, recvs = [], []
    for i, (s, o) in enumerate(zip(srcs, outs)):
        for k, (px, py) in enumerate(_chip_peers(x, y)):
            j = 3 * i + k
            sends.append(_remote(s.at[2 * px + py], o.at[me], send_sems.at[j], recv_sems.at[j], (px, py, c)))
            recvs.append(_remote(s.at[me], o.at[2 * px + py], send_sems.at[j], recv_sems.at[j], (px, py, c)))
    return sends, recvs


class _RidingExchange:
    def __init__(self, srcs, outs, sems, steps):
        self.srcs, self.outs, self.sems, self.steps = srcs, outs, sems, steps

    @staticmethod
    def scratch(n):
        return [pltpu.SemaphoreType.DMA((3 * n,)), pltpu.SemaphoreType.DMA((3 * n,))] if n else []

    def copies(self):
        return _scatter_copies(self.srcs, self.outs, *self.sems)

    def _at(self, last):
        dims = self.steps if isinstance(self.steps, tuple) else (self.steps,)
        hit = None
        for ax, n in enumerate(dims):
            here = pl.program_id(ax) == (n - 1 if last else 0)
            hit = here if hit is None else jnp.logical_and(hit, here)
        return hit

    def start(self):
        if self.srcs:
            @pl.when(self._at(last=False))
            def _():
                for cp in self.copies()[0]:
                    cp.start()

    def finish(self):
        if self.srcs:
            @pl.when(self._at(last=True))
            def _():
                sends, recvs = self.copies()
                for cp in recvs:
                    cp.wait_recv()
                for cp in sends:
                    cp.wait_send()


def _rows_half(ref, rows, h):
    r = rows // 2
    return ref.at[pl.ds(h * r if isinstance(h, int) else pl.multiple_of(h * r, 8), r)]


def _gather_copies(srcs, outs, rows, send_sems, recv_sems):
    x, y, c = lax.axis_index("x"), lax.axis_index("y"), lax.axis_index("c")
    me = 2 * x + y
    sends, recvs = [], []
    for i, (s, o) in enumerate(zip(srcs, outs)):
        mine = _rows_half(s, rows[i], c)
        for k, (px, py) in enumerate(_chip_peers(x, y)):
            j = 3 * i + k
            sends.append(_remote(mine, o.at[me, c], send_sems.at[j], recv_sems.at[j], (px, py, c)))
            recvs.append(_remote(mine, o.at[2 * px + py, c], send_sems.at[j], recv_sems.at[j], (px, py, c)))
    return sends, recvs


def _gather_shapes(shards):
    return [SDS((4, 2, a.shape[0] // 2, a.shape[1]), a.dtype) for a in shards]


class _RidingGather(_RidingExchange):
    def __init__(self, srcs, outs, sems, steps, rows):
        super().__init__(srcs, outs, sems, steps)
        self.rows = rows

    def copies(self):
        return _gather_copies(self.srcs, self.outs, self.rows, *self.sems)


def _head_consts():
    e = (lax.broadcasted_iota(jnp.int32, (128, NH * HP), 1) // HP == lax.broadcasted_iota(jnp.int32, (128, NH * HP), 0)).astype(F32)
    et = (lax.broadcasted_iota(jnp.int32, (NH * HP, 128), 0) // HP == lax.broadcasted_iota(jnp.int32, (NH * HP, 128), 1)).astype(F32)
    r = lax.broadcasted_iota(jnp.int32, (CH, CH), 0)
    c = lax.broadcasted_iota(jnp.int32, (CH, CH), 1)
    return e, et, (c <= r), (r <= c)


def _ssd_common(xbc_c, dtr, dtb, alog, e, tril, triu):
    xbc = _silu(xbc_c)
    xs = xbc[:, :NH * HP]
    dt = _softplus(dtr + dtb)
    A = -jnp.exp(alog)
    a = dt * A
    cs = _dot(tril, a, exact="a", passes=3)
    csT = _dot(a, triu, "tn", exact="b", passes=3)
    csL = cs[CH - 1:CH, :]
    wdec = jnp.exp(csL - cs) * dt
    dtE = _dot(dt, e, exact="b")
    ecsE = _dot(jnp.exp(cs), e, exact="b")
    wE = _dot(wdec, e, exact="b")
    eL = jnp.exp(csL)
    return xbc, xs, dt, A, cs, csT, csL, wdec, dtE, ecsE, wE, eL


def _ssd_fwd(proj, cw, cb, dtr, sc, norm_g, comm=(), name="ssd_fwd"):
    S = proj.shape[0]
    nc = S // CH
    nco = len(comm)

    def body(*refs):
        z_ref, xp_ref, cw_ref, cb_ref, dtr_ref, sc_ref, ng_ref = refs[:7]
        xc_ref, y_ref, yn_ref, hp_ref = refs[7 + nco:11 + nco]
        hst, cext = refs[11 + 2 * nco:13 + 2 * nco]
        gather = _RidingGather(refs[7:7 + nco], refs[11 + nco:11 + 2 * nco], refs[13 + 2 * nco:], nc,
                               [a.shape[0] for a in comm])
        gather.start()

        @pl.when(pl.program_id(0) == 0)
        def _():
            hst[...] = jnp.zeros_like(hst)
            cext[pl.ds(0, SSD_HALO), :] = jnp.zeros((SSD_HALO, XBC), F32)

        cext[pl.ds(SSD_HALO, CH), :] = xp_ref[...]
        xc = jnp.zeros((CH, XBC), F32) + cb_ref[...]
        for k in range(KS):
            xc = xc + cext[pl.ds(SSD_HALO - (KS - 1) + k, CH), :] * cw_ref[k:k + 1, :]
        xc_ref[...] = xc
        cext[pl.ds(0, SSD_HALO), :] = cext[pl.ds(CH, SSD_HALO), :]

        e, et, tril, triu = _head_consts()
        xbc, xs, dt, A, cs, csT, csL, wdec, dtE, ecsE, wE, eL = _ssd_common(
            xc, dtr_ref[...], sc_ref[0:1, :], sc_ref[1:2, :], e, tril, triu)
        hp_ref[0] = hst[...]
        xd = xs * dtE
        xw = xs * wE
        dE = _dot(jnp.broadcast_to(sc_ref[2:3, :], (8, 128)), e, exact="b", passes=3)[0:1, :]
        eLcol = jnp.sum(et * eL, axis=1, keepdims=True)
        for g in range(NG):
            Bg = xbc[:, NH * HP + g * NS: NH * HP + (g + 1) * NS]
            Cg = xbc[:, NH * HP + NG * NS + g * NS: NH * HP + NG * NS + (g + 1) * NS]
            gs = slice(g * GW, (g + 1) * GW)
            G = _dot(Cg, Bg, "nt")
            hg = hst[gs, :]
            yoff = ecsE[:, gs] * _dot(Cg, hg, "nt")
            hst[gs, :] = eLcol[gs, :] * hg + _dot(xw[:, gs], Bg, "tn")
            for hh in range(NH // NG):
                h = g * (NH // NG) + hh
                hs = slice(h * HP, (h + 1) * HP)
                m = jnp.where(tril, jnp.exp(jnp.where(tril, cs[:, h:h + 1] - csT[h:h + 1, :], 0.0)), 0.0)
                yd = _dot(G * m, xd[:, hs])
                y_ref[:, hs] = yd + yoff[:, hh * HP:(hh + 1) * HP] + dE[:, hs] * xs[:, hs]
        y = y_ref[...]
        yz = y * _silu(z_ref[...])
        for g in range(NG):
            gs = slice(g * GW, (g + 1) * GW)
            yg = yz[:, gs]
            r = lax.rsqrt(jnp.mean(yg * yg, axis=-1, keepdims=True) + EPS)
            yn_ref[:, gs] = (yg * r * ng_ref[:, gs]).astype(yn_ref.dtype)
        gather.finish()

    outs = pl.pallas_call(
        body, name=name, grid=(nc,),
        in_specs=[pl.BlockSpec((CH, D), lambda c: (c, COL_Z // D)),
                  pl.BlockSpec((CH, XBC), lambda c: (c, COL_XBC // XBC)),
                  pl.BlockSpec((KS, XBC), lambda c: (0, 0)),
                  pl.BlockSpec((1, XBC), lambda c: (0, 0)),
                  pl.BlockSpec((CH, 128), lambda c: (c, 0)),
                  pl.BlockSpec((8, 128), lambda c: (0, 0)),
                  pl.BlockSpec((1, D), lambda c: (0, 0))] + [HBM_SPEC] * nco,
        out_specs=[pl.BlockSpec((CH, XBC), lambda c: (c, 0)), pl.BlockSpec((CH, D), lambda c: (c, 0)),
                   pl.BlockSpec((CH, D), lambda c: (c, 0)),
                   pl.BlockSpec((1, NH * HP, NS), lambda c: (c, 0, 0))] + [HBM_SPEC] * nco,
        out_shape=[SDS((S, XBC), F32), SDS((S, D), F32), SDS((S, D), _MXU), SDS((nc, NH * HP, NS), F32)]
        + _gather_shapes(comm),
        scratch_shapes=[pltpu.VMEM((NH * HP, NS), F32), pltpu.VMEM((SSD_HALO + CH, XBC), F32)]
        + _RidingExchange.scratch(nco),
        compiler_params=_cp("arbitrary"))(proj, proj, cw, cb, dtr, sc, norm_g, *comm)
    return outs[:4], outs[4:]


def _ssd_bwd(dmix, y, proj, xbc_c, dtr, hprev, cw, sc, norm_g, comm=(), name="ssd_bwd"):
    S = proj.shape[0]
    nc = S // CH
    nco = len(comm)
    rev = lambda c: nc - 1 - c

    def body(*refs):
        dyn_ref, y_ref, z_ref, x_ref, xp_ref, dtr_ref, hp_ref, cw_ref, sc_ref, ng_ref = refs[:10]
        dz_ref, dx_ref, ddtr_ref, gcw_ref, gcb_ref, gsc_ref, gng_ref = refs[10 + nco:17 + nco]
        dh, dxd, cext = refs[17 + 2 * nco:20 + 2 * nco]
        exchange = _RidingExchange(refs[10:10 + nco], refs[17 + nco:17 + 2 * nco], refs[20 + 2 * nco:], nc)
        exchange.start()

        @pl.when(pl.program_id(0) == 0)
        def _():
            dh[...] = jnp.zeros_like(dh)
            cext[pl.ds(CH, SSD_HALO), :] = jnp.zeros((SSD_HALO, XBC), F32)
            gcw_ref[...] = jnp.zeros_like(gcw_ref)
            gcb_ref[...] = jnp.zeros_like(gcb_ref)
            gsc_ref[...] = jnp.zeros_like(gsc_ref)
            gng_ref[...] = jnp.zeros_like(gng_ref)

        e, et, tril, triu = _head_consts()
        xbc_c = x_ref[...]
        dtr = dtr_ref[...]
        dtb = sc_ref[0:1, :]
        xbc, xs, dt, A, cs, csT, csL, wdec, dtE, ecsE, wE, eL = _ssd_common(
            xbc_c, dtr, dtb, sc_ref[1:2, :], e, tril, triu)
        xd = xs * dtE
        xw = xs * wE
        dE = _dot(jnp.broadcast_to(sc_ref[2:3, :], (8, 128)), e, exact="b", passes=3)[0:1, :]
        eLcol = jnp.sum(et * eL, axis=1, keepdims=True)

        yv = y_ref[...]
        zv = z_ref[...]
        sz = _silu(zv)
        yz = yv * sz
        dyn = dyn_ref[...]
        dyz_parts = []
        for g in range(NG):
            gs = slice(g * GW, (g + 1) * GW)
            yg = yz[:, gs]
            r = lax.rsqrt(jnp.mean(yg * yg, axis=-1, keepdims=True) + EPS)
            yh = yg * r
            dn = dyn[:, gs]
            gng_ref[:, gs] += jnp.sum(dn * yh, axis=0, keepdims=True)
            gdn = dn * ng_ref[:, gs]
            dyz_parts.append(r * (gdn - yh * jnp.mean(yh * gdn, axis=-1, keepdims=True)))
        dyz = jnp.concatenate(dyz_parts, axis=1)
        dy = dyz * sz
        dz_ref[...] = (dyz * yv * _dsilu(zv)).astype(dz_ref.dtype)

        dxs = dE * dy
        dzo = ecsE * dy
        dcsL = jnp.zeros((1, 128), F32)
        ddt = jnp.zeros((CH, 128), F32)
        qcols = jnp.zeros((CH, 128), F32)
        qrows = jnp.zeros((128, CH), F32)
        lane = lax.broadcasted_iota(jnp.int32, (1, 128), 1)
        sub = lax.broadcasted_iota(jnp.int32, (128, 1), 0)
        dB_parts, dC_parts, yoff_parts, dxw_parts = [], [], [], []
        for g in range(NG):
            Bg = xbc[:, NH * HP + g * NS: NH * HP + (g + 1) * NS]
            Cg = xbc[:, NH * HP + NG * NS + g * NS: NH * HP + NG * NS + (g + 1) * NS]
            gs = slice(g * GW, (g + 1) * GW)
            hg = hp_ref[0, gs, :]
            dhn = dh[gs, :]
            G = _dot(Cg, Bg, "nt")
            yoff_parts.append(ecsE[:, gs] * _dot(Cg, hg, "nt"))
            dC = _dot(dzo[:, gs], hg)
            dhp = _dot(dzo[:, gs], Cg, "tn") + eLcol[gs, :] * dhn
            t1 = jnp.sum(dhn * hg, axis=1, keepdims=True) * eLcol[gs, :]
            dcsL = dcsL + jnp.sum(et[gs, :] * t1, axis=0, keepdims=True)
            dxw_parts.append(_dot(Bg, dhn, "nt"))
            dB = _dot(xw[:, gs], dhn)
            dgsum = jnp.zeros((CH, CH), F32)
            for hh in range(NH // NG):
                h = g * (NH // NG) + hh
                hs = slice(h * HP, (h + 1) * HP)
                m = jnp.where(tril, jnp.exp(jnp.where(tril, cs[:, h:h + 1] - csT[h:h + 1, :], 0.0)), 0.0)
                sc = G * m
                dyh = dy[:, hs]
                dxd[:, hs] = _dot(sc, dyh, "tn")
                dsc = _dot(dyh, xd[:, hs], "nt")
                q = dsc * sc
                qcols = qcols + jnp.where(lane == h, jnp.sum(q, axis=1, keepdims=True), 0.0)
                qrows = qrows + jnp.where(sub == h, jnp.sum(q, axis=0, keepdims=True), 0.0)
                dgsum = dgsum + dsc * m
            dC_parts.append(dC + _dot(dgsum, Bg))
            dB_parts.append(dB + _dot(dgsum, Cg, "tn"))
            dh[gs, :] = dhp
        yoff = jnp.concatenate(yoff_parts, axis=1)
        dxw = jnp.concatenate(dxw_parts, axis=1)
        dxdv = dxd[...]
        per_head = _dot(jnp.concatenate([dy * yoff, dxw * xs, dxdv * xs, dy * xs], axis=0), et, exact="b")
        dcs = qcols - qrows.T + per_head[0:CH]
        dw = per_head[CH:2 * CH]
        gsc_ref[2:3, :] += jnp.sum(per_head[3 * CH:4 * CH], axis=0, keepdims=True)
        dxs = dxs + wE * dxw + dtE * dxdv
        ddt = ddt + dw * jnp.exp(csL - cs) + per_head[2 * CH:3 * CH]
        dcs = dcs - dw * wdec
        dcsL = dcsL + jnp.sum(dw * wdec, axis=0, keepdims=True)
        last = lax.broadcasted_iota(jnp.int32, (CH, 128), 0) == CH - 1
        dcs = dcs + jnp.where(last, dcsL, 0.0)
        da = _dot(triu, dcs, exact="a", passes=3)
        ddt = ddt + da * A
        gsc_ref[1:2, :] += jnp.sum(da * dt, axis=0, keepdims=True) * A
        valid = lax.broadcasted_iota(jnp.int32, (CH, 128), 1) < NH
        ddtr = jnp.where(valid, ddt * _sigmoid(dtr + dtb), 0.0)
        gsc_ref[0:1, :] += jnp.sum(ddtr, axis=0, keepdims=True)
        ddtr_ref[...] = ddtr.astype(ddtr_ref.dtype)
        dxbc = jnp.concatenate([dxs] + dB_parts + dC_parts, axis=1)
        dxc = dxbc * _dsilu(xbc_c)
        cext[pl.ds(0, CH), :] = dxc
        xp = xp_ref[...]
        acc = jnp.zeros((CH, XBC), F32)
        for k in range(KS):
            sh = cext[pl.ds(KS - 1 - k, CH), :]
            acc = acc + sh * cw_ref[k:k + 1, :]
            gcw_ref[k:k + 1, :] += jnp.sum(xp * sh, axis=0, keepdims=True)
        gcb_ref[...] += jnp.sum(dxc, axis=0, keepdims=True)
        dx_ref[...] = acc.astype(dx_ref.dtype)
        cext[pl.ds(CH, SSD_HALO), :] = cext[pl.ds(0, SSD_HALO), :]
        exchange.finish()

    vec = pl.BlockSpec((8, 128), lambda c: (0, 0))
    vecd = pl.BlockSpec((1, D), lambda c: (0, 0))
    cwsp = pl.BlockSpec((KS, XBC), lambda c: (0, 0))
    cbsp = pl.BlockSpec((1, XBC), lambda c: (0, 0))
    row = lambda w, j=0: pl.BlockSpec((CH, w), lambda c: (rev(c), j))
    outs = pl.pallas_call(
        body, name=name, grid=(nc,),
        in_specs=[row(D), row(D), row(D, COL_Z // D), row(XBC), row(XBC, COL_XBC // XBC), row(128),
                  pl.BlockSpec((1, NH * HP, NS), lambda c: (rev(c), 0, 0)), cwsp, vec, vecd] + [HBM_SPEC] * nco,
        out_specs=[row(D), row(XBC), row(128), cwsp, cbsp, vec, vecd] + [HBM_SPEC] * nco,
        out_shape=[SDS((S, D), _MXU), SDS((S, XBC), _MXU), SDS((S, 128), _MXU), SDS((KS, XBC), F32),
                   SDS((1, XBC), F32), SDS((8, 128), F32), SDS((1, D), F32)] + [SDS(p.shape, p.dtype) for p in comm],
        scratch_shapes=[pltpu.VMEM((NH * HP, NS), F32), pltpu.VMEM((CH, NH * HP), F32),
                        pltpu.VMEM((CH + SSD_HALO, XBC), F32)] + _RidingExchange.scratch(nco),
        compiler_params=_cp("arbitrary"))(dmix, y, proj, xbc_c, proj, dtr, hprev, cw, sc, norm_g, *comm)
    return outs[:7], outs[7:]


CONV_RT = 32


def _fill_phases(ext, ph, rows):
    for s in range(1, 8):
        ph[s - 1, pl.ds(0, rows), :] = ext[pl.ds(s, rows), :]


def _window(ext, ph, off, r0, ls):
    s = off % 8
    src = ext if s == 0 else ph.at[s - 1]
    return src[pl.ds(pl.multiple_of(off - s + r0, 8), CONV_RT), ls]


def _cf_fwd(proj, w, b, lg, lb, comm=(), name="cf_fwd", tb_cap=256):
    S = proj.shape[0]
    tb = _tile(S, tb_cap, 8)
    nb = S // tb
    nco = len(comm)

    def body(*refs):
        a_ref, g_ref, w_ref, b_ref, lg_ref, lb_ref = refs[:6]
        u1_ref, u_ref = refs[6 + nco:8 + nco]
        ext, ph = refs[8 + 2 * nco:10 + 2 * nco]
        gather = _RidingGather(refs[6:6 + nco], refs[8 + nco:8 + 2 * nco], refs[10 + 2 * nco:], nb,
                               [a.shape[0] for a in comm])
        gather.start()

        @pl.when(pl.program_id(0) == 0)
        def _():
            ext[pl.ds(0, CF_HALO), :] = jnp.zeros((CF_HALO, D), F32)

        ext[pl.ds(CF_HALO, tb), :] = a_ref[...] * _sigmoid(g_ref[...])
        _fill_phases(ext, ph, tb + CF_HALO - 8)

        def tile(i, carry):
            r0 = pl.multiple_of(i * CONV_RT, CONV_RT)
            for l in range(D // 128):
                ls = pl.ds(l * 128, 128)
                acc = jnp.broadcast_to(b_ref[:, ls], (CONV_RT, 128))
                for k in range(KC):
                    acc = acc + _window(ext, ph, CF_HALO - (KC - 1) + k, r0, ls) * w_ref[k:k + 1, ls]
                u1_ref[pl.ds(r0, CONV_RT), ls] = acc
            return carry

        lax.fori_loop(0, tb // CONV_RT, tile, 0)
        acc = u1_ref[...]
        mu = jnp.mean(acc, axis=-1, keepdims=True)
        xc = acc - mu
        r = lax.rsqrt(jnp.mean(xc * xc, axis=-1, keepdims=True) + EPS)
        u_ref[...] = _silu(xc * r * lg_ref[...] + lb_ref[...]).astype(u_ref.dtype)
        ext[pl.ds(0, CF_HALO), :] = ext[pl.ds(tb, CF_HALO), :]
        gather.finish()

    vec = pl.BlockSpec((1, D), lambda i: (0, 0))
    outs = pl.pallas_call(
        body, name=name, grid=(nb,),
        in_specs=[pl.BlockSpec((tb, D), lambda i: (i, COL_A // D)), pl.BlockSpec((tb, D), lambda i: (i, COL_G // D)),
                  pl.BlockSpec((KC, D), lambda i: (0, 0)), vec, vec, vec] + [HBM_SPEC] * nco,
        out_specs=[pl.BlockSpec((tb, D), lambda i: (i, 0)), pl.BlockSpec((tb, D), lambda i: (i, 0))] + [HBM_SPEC] * nco,
        out_shape=[SDS((S, D), F32), SDS((S, D), _MXU)] + _gather_shapes(comm),
        scratch_shapes=[pltpu.VMEM((CF_HALO + tb, D), F32), pltpu.VMEM((7, tb + CF_HALO - 8, D), F32)]
        + _RidingExchange.scratch(nco),
        compiler_params=_cp("arbitrary"))(proj, proj, w, b, lg, lb, *comm)
    return outs[:2], outs[2:]


def _cf_bwd(dmix, u1, proj, w, lg, lb, comm=(), name="cf_bwd", tb_cap=256):
    S = proj.shape[0]
    tb = _tile(S, tb_cap, 8)
    nb = S // tb
    nco = len(comm)
    rev = lambda i: nb - 1 - i

    def body(*refs):
        du_ref, u1_ref, a_ref, g_ref, w_ref, lg_ref, lb_ref = refs[:7]
        da_ref, dg_ref, dw_ref, db_ref, dlg_ref, dlb_ref = refs[7 + nco:13 + nco]
        ext, ph, u0s = refs[13 + 2 * nco:16 + 2 * nco]
        exchange = _RidingExchange(refs[7:7 + nco], refs[13 + nco:13 + 2 * nco], refs[16 + 2 * nco:], nb)
        exchange.start()

        @pl.when(pl.program_id(0) == 0)
        def _():
            ext[pl.ds(tb, CF_HALO), :] = jnp.zeros((CF_HALO, D), F32)
            dw_ref[...] = jnp.zeros_like(dw_ref)
            db_ref[...] = jnp.zeros_like(db_ref)
            dlg_ref[...] = jnp.zeros_like(dlg_ref)
            dlb_ref[...] = jnp.zeros_like(dlb_ref)

        u1 = u1_ref[...]
        mu = jnp.mean(u1, axis=-1, keepdims=True)
        xc = u1 - mu
        r = lax.rsqrt(jnp.mean(xc * xc, axis=-1, keepdims=True) + EPS)
        xh = xc * r
        lgv = lg_ref[...]
        du2 = du_ref[...] * _dsilu(xh * lgv + lb_ref[...])
        dlg_ref[...] += jnp.sum(du2 * xh, axis=0, keepdims=True)
        dlb_ref[...] += jnp.sum(du2, axis=0, keepdims=True)
        gd = du2 * lgv
        du1 = r * (gd - jnp.mean(gd, axis=-1, keepdims=True) - xh * jnp.mean(gd * xh, axis=-1, keepdims=True))
        db_ref[...] += jnp.sum(du1, axis=0, keepdims=True)
        ext[pl.ds(0, tb), :] = du1
        u0s[...] = a_ref[...] * _sigmoid(g_ref[...])
        _fill_phases(ext, ph, tb + CF_HALO - 8)

        def dx_tile(i, carry):
            r0 = pl.multiple_of(i * CONV_RT, CONV_RT)
            rows = pl.ds(r0, CONV_RT)
            for l in range(D // 128):
                ls = pl.ds(l * 128, 128)
                acc = jnp.zeros((CONV_RT, 128), F32)
                for k in range(KC):
                    acc = acc + _window(ext, ph, KC - 1 - k, r0, ls) * w_ref[k:k + 1, ls]
                sg = _sigmoid(g_ref[rows, ls])
                da_ref[rows, ls] = (acc * sg).astype(da_ref.dtype)
                dg_ref[rows, ls] = (acc * a_ref[rows, ls] * sg * (1.0 - sg)).astype(dg_ref.dtype)
            return carry

        lax.fori_loop(0, tb // CONV_RT, dx_tile, 0)
        for l in range(D // 128):
            ls = pl.ds(l * 128, 128)

            def dw_tile(i, accs, ls=ls):
                r0 = pl.multiple_of(i * CONV_RT, CONV_RT)
                u0t = u0s[pl.ds(r0, CONV_RT), ls]
                out = []
                for k in range(KC):
                    p = u0t * _window(ext, ph, KC - 1 - k, r0, ls)
                    out.append(accs[k] + ((p[0:8] + p[8:16]) + (p[16:24] + p[24:32])))
                return tuple(out)

            accs = lax.fori_loop(0, tb // CONV_RT, dw_tile, tuple(jnp.zeros((8, 128), F32) for _ in range(KC)))
            for k in range(KC):
                dw_ref[k:k + 1, ls] += jnp.sum(accs[k], axis=0, keepdims=True)
        ext[pl.ds(tb, CF_HALO), :] = ext[pl.ds(0, CF_HALO), :]
        exchange.finish()

    vec = pl.BlockSpec((1, D), lambda i: (0, 0))
    wsp = pl.BlockSpec((KC, D), lambda i: (0, 0))
    row = lambda j=0: pl.BlockSpec((tb, D), lambda i: (rev(i), j))
    outs = pl.pallas_call(
        body, name=name, grid=(nb,),
        in_specs=[row(1), row(), row(COL_A // D), row(COL_G // D), wsp, vec, vec] + [HBM_SPEC] * nco,
        out_specs=[row(), row(), wsp, vec, vec, vec] + [HBM_SPEC] * nco,
        out_shape=[SDS((S, D), _MXU), SDS((S, D), _MXU), SDS((KC, D), F32),
                   SDS((1, D), F32), SDS((1, D), F32), SDS((1, D), F32)] + [SDS(p.shape, p.dtype) for p in comm],
        scratch_shapes=[pltpu.VMEM((tb + CF_HALO, D), F32), pltpu.VMEM((7, tb + CF_HALO - 8, D), F32),
                        pltpu.VMEM((tb, D), F32)] + _RidingExchange.scratch(nco),
        compiler_params=_cp("arbitrary"))(dmix, u1, proj, proj, w, lg, lb, *comm)
    return outs[:6], outs[6:]


def _attn_fwd(q, kv, name="attn_fwd", tq_cap=512):
    S = q.shape[0]
    tq = _tile(S, tq_cap, 8)
    scale = XD ** -0.5

    def body(q_ref, kv_ref, o_ref):
        for h in range(XH):
            hs = slice(h * XD, (h + 1) * XD)
            s = _dot(q_ref[:, hs], kv_ref[:, hs], "nt") * scale
            s = s - jnp.max(s, axis=-1, keepdims=True)
            p = jnp.exp(s)
            p = p / jnp.sum(p, axis=-1, keepdims=True)
            o_ref[:, hs] = _dot(p, kv_ref[:, D + h * XD: D + (h + 1) * XD]).astype(o_ref.dtype)

    return pl.pallas_call(
        body, name=name, grid=(S // tq,),
        in_specs=[pl.BlockSpec((tq, D), lambda i: (i, 0)), pl.BlockSpec((MEM, 2 * D), lambda i: (0, 0))],
        out_specs=pl.BlockSpec((tq, D), lambda i: (i, 0)), out_shape=SDS((S, D), _MXU),
        compiler_params=_cp("parallel"))(q, kv)


def _attn_bwd(do, q, kv, name="attn_bwd", tq_cap=512):
    S = q.shape[0]
    tq = _tile(S, tq_cap, 8)
    scale = XD ** -0.5

    def body(do_ref, q_ref, kv_ref, dq_ref, dkv_ref):
        @pl.when(pl.program_id(0) == 0)
        def _():
            dkv_ref[...] = jnp.zeros_like(dkv_ref)

        for h in range(XH):
            hs = slice(h * XD, (h + 1) * XD)
            vs = slice(D + h * XD, D + (h + 1) * XD)
            qh = q_ref[:, hs]
            kh = kv_ref[:, hs]
            s = _dot(qh, kh, "nt") * scale
            s = s - jnp.max(s, axis=-1, keepdims=True)
            p = jnp.exp(s)
            p = p / jnp.sum(p, axis=-1, keepdims=True)
            doh = do_ref[:, hs]
            dp = _dot(doh, kv_ref[:, vs], "nt")
            ds = p * (dp - jnp.sum(dp * p, axis=-1, keepdims=True)) * scale
            dq_ref[:, hs] = _dot(ds, kh).astype(dq_ref.dtype)
            dkv_ref[:, hs] += _dot(ds, qh, "tn")
            dkv_ref[:, vs] += _dot(p, doh, "tn")

    return pl.pallas_call(
        body, name=name, grid=(S // tq,),
        in_specs=[pl.BlockSpec((tq, D), lambda i: (i, 0)), pl.BlockSpec((tq, D), lambda i: (i, 0)),
                  pl.BlockSpec((MEM, 2 * D), lambda i: (0, 0))],
        out_specs=[pl.BlockSpec((tq, D), lambda i: (i, 0)), pl.BlockSpec((MEM, 2 * D), lambda i: (0, 0))],
        out_shape=[SDS((S, D), _MXU), SDS((MEM, 2 * D), F32)],
        compiler_params=_cp("arbitrary"))(do, q, kv)


def _ffn_in(hf, wg_t, wu_t, name="ffn_in", tm_cap=512, tn_cap=1408):
    S, K = hf.shape
    N = wg_t.shape[0]
    tm, tn = _tile(S, tm_cap, 8), _tile(N, tn_cap)

    def body(a_ref, g_ref, u_ref, act_ref, gt_ref, up_ref):
        a = a_ref[...]
        gt = _dot(a, g_ref[...], "nt")
        up = _dot(a, u_ref[...], "nt")
        act_ref[...] = (_silu(gt) * up).astype(act_ref.dtype)
        gt_ref[...] = gt.astype(gt_ref.dtype)
        up_ref[...] = up.astype(up_ref.dtype)

    wsp = pl.BlockSpec((tn, K), lambda j, i: (j, 0))
    osp = pl.BlockSpec((tm, tn), lambda j, i: (i, j))
    return pl.pallas_call(
        body, name=name, grid=(N // tn, S // tm), in_specs=[pl.BlockSpec((tm, K), lambda j, i: (i, 0)), wsp, wsp],
        out_specs=[osp, osp, osp], out_shape=[SDS((S, N), _MXU)] * 3,
        compiler_params=_cp("parallel", "parallel"))(hf, wg_t, wu_t)


def _ffn_out_bwd(dx, w_down, gt, up, name="ffn_out_dx", tm_cap=512, tk_cap=1408):
    S, N = dx.shape
    K = w_down.shape[0]
    tm, tk = _tile(S, tm_cap, 8), _tile(K, tk_cap)

    def body(a_ref, b_ref, g_ref, u_ref, dg_ref, du_ref):
        d = _dot(a_ref[...], b_ref[...], "nt")
        gt = g_ref[...].astype(F32)
        s = _sigmoid(gt)
        dg_ref[...] = (d * u_ref[...].astype(F32) * (s * (1.0 + gt * (1.0 - s)))).astype(dg_ref.dtype)
        du_ref[...] = (d * gt * s).astype(du_ref.dtype)

    osp = pl.BlockSpec((tm, tk), lambda j, i: (i, j))
    return pl.pallas_call(
        body, name=name, grid=(K // tk, S // tm),
        in_specs=[pl.BlockSpec((tm, N), lambda j, i: (i, 0)), pl.BlockSpec((tk, N), lambda j, i: (j, 0)), osp, osp],
        out_specs=[osp, osp], out_shape=[SDS((S, K), _MXU)] * 2,
        compiler_params=_cp("parallel", "parallel"))(dx, w_down, gt, up)


AG_RIDE = (("w_down",), ("w_out", "w_q", "w_kv", "w_o"), ("w_gate", "w_up"))


def _local_step(x, mem, tgt, W, P, core=None, late=None):
    pair, got = {}, {}
    ride = [[late[n] for n in grp] if late is not None else [] for grp in AG_RIDE]

    def rs_pair(group):
        if core is None:
            return []
        ps = _rs_pair(group, GW, core)
        pair.update(zip(group, ps))
        return ps
    h = _rms_fwd(x, P["g_mix"], "rms_mix")
    if ride[0]:
        proj, bufs0 = _mm_nn(h, W["main"], "in_proj", tn_cap=1152, gather=ride[0])
    else:
        proj, bufs0 = _mm_nn(h, W["main"], "in_proj", tn_cap=1152), []
    dtr = _mm_nn(h, W["dt"], "in_proj_dt")
    (xbc_c, y, yn, hprev), bufs1 = _ssd_fwd(proj, P["conv4_w"], P["conv4_b"], dtr, P["sc"], P["ssd_norm_g"], comm=ride[1])
    (u1, u), bufs2 = _cf_fwd(proj, P["cf_w"], P["cf_b"], P["ln_g"], P["ln_b"], comm=ride[2])
    if late is not None:
        names = AG_RIDE[0] + AG_RIDE[1] + AG_RIDE[2]
        bufs = list(bufs0) + list(bufs1) + list(bufs2)
        W = dict(W, **_pack_late(dict(zip(names, _gather_finish_list(ride[0] + ride[1] + ride[2], bufs)))))
    mix = jnp.concatenate([yn, u], axis=1)
    x1 = _mm_nn(mix, W["out"], "out_proj", add=x)
    hq = _rms_fwd(x1, P["g_xattn"], "rms_xattn")
    q = _mm_nn(hq, W["q"], "q_proj")
    mn = _rms_fwd(mem, P["g_mem"], "rms_mem")
    kv = _mm_nn(mn, W["kv"], "kv_proj")
    o = _attn_fwd(q, kv)
    x2 = _mm_nn(o, W["o"], "o_proj", add=x1)
    hf = _rms_fwd(x2, P["g_ffn"], "rms_ffn")
    act, gt, up = _ffn_in(hf, W["gate_t"], W["up_t"])
    x3 = _mm_nn(act, W["down"], "ffn_out", add=x2)
    loss, dx3, dx3b, g_final = _final_loss(x3, P["g_final"], tgt)
    GW, GP = {}, {"g_final": g_final}
    GW["down"] = _mm_tn(act, dx3b, "ffn_out_dw", tk_cap=1408, tn_cap=1024)
    dgt, dup = _ffn_out_bwd(dx3b, W["down"], gt, up)
    dhf = _mm_nn(dgt, W["gate_t"], "ffn_gate_dx", tm_cap=512)
    dhf = _mm_nn(dup, W["up_t"], "ffn_up_dx", add=dhf, tm_cap=512)
    GW["gate_t"] = _mm_tn(dgt, hf, "ffn_gate_dw", tk_cap=1408, tn_cap=1024)
    GW["up_t"] = _mm_tn(dup, hf, "ffn_up_dw", tk_cap=1408, tn_cap=1024)
    ffn_pieces = rs_pair(RS_GROUPS[0])
    dx2, dx2b, GP["g_ffn"] = _rms_bwd(x2, P["g_ffn"], dhf, dx3, "rms_ffn_bwd")
    do = _mm_nt(dx2b, W["o"], "o_proj_dx")
    GW["o"] = _mm_tn(o, dx2b, "o_proj_dw")
    dq, dkv = _attn_bwd(do, q, kv)
    dhq = _mm_nt(dq, W["q"], "q_proj_dx")
    GW["q"] = _mm_tn(hq, dq, "q_proj_dw")
    dkvb = dkv.astype(_MXU)
    GW["kv"] = _mm_tn(mn, dkvb, "kv_proj_dw", tm_cap=256)
    dmn = _mm_nt(dkvb, W["kv"], "kv_proj_dx")
    GP["g_mem"] = _rms_bwd(mem, P["g_mem"], dmn, None, "rms_mem_bwd")
    dx1, dx1b, GP["g_xattn"] = _rms_bwd(x1, P["g_xattn"], dhq, dx2, "rms_xattn_bwd")
    dmix = _mm_nt(dx1b, W["out"], "out_proj_dx")
    GW["out"] = _mm_tn(mix, dx1b, "out_proj_dw", tn_cap=1024)
    attn_pieces = rs_pair(RS_GROUPS[1])
    (da, dg, GP["cf_w"], GP["cf_b"], GP["ln_g"], GP["ln_b"]), came = _cf_bwd(
        dmix, u1, proj, P["cf_w"], P["ln_g"], P["ln_b"], comm=ffn_pieces)
    got.update(zip(RS_GROUPS[0], came))
    (dz, dxbc, ddtr, GP["conv4_w"], GP["conv4_b"], GP["sc"], GP["ssd_norm_g"]), came = _ssd_bwd(
        dmix, y, proj, xbc_c, dtr, hprev, P["conv4_w"], P["sc"], P["ssd_norm_g"], comm=attn_pieces)
    got.update(zip(RS_GROUPS[1], came))
    dproj = jnp.concatenate([dz, da, dg, dxbc], axis=1)
    GW["main"] = _mm_tn(h, dproj, "in_proj_dw", tn_cap=1152)
    GW["dt"] = _mm_tn(h, ddtr, "in_proj_dt_dw")
    in_pieces = rs_pair(RS_GROUPS[2])
    dh = _mm_nt(ddtr, W["dt"], "in_proj_dt_dx")
    if in_pieces:
        dh, came = _mm_nt(dproj, W["main"], "in_proj_dx", add=dh, tk_cap=512, comm=in_pieces)
        got.update(zip(RS_GROUPS[2], came))
    else:
        dh = _mm_nt(dproj, W["main"], "in_proj_dx", add=dh, tk_cap=512)
    grad_x, GP["g_mix"] = _rms_bwd(x, P["g_mix"], dh, dx1, "rms_mix_bwd", low=False)
    if core is None:
        return loss, grad_x, GW, GP
    return loss, grad_x, GW, GP, pair, got


Z_END, XBC_END, DT_END = NH * HP, NH * HP + XBC, NH * HP + XBC + NH


def _pad_to(a, rows=None, cols=None):
    r = 0 if rows is None else rows - a.shape[0]
    c = 0 if cols is None else cols - a.shape[1]
    return jnp.pad(a, ((0, r), (0, c)))


IN_W = DT_END + 2 * D
W_IN_SEGS = [(0, Z_END, "main", COL_Z), (Z_END, XBC_END, "main", COL_XBC), (XBC_END, DT_END, "dt", 0),
             (DT_END, DT_END + D, "main", COL_A), (DT_END + D, IN_W, "main", COL_G)]
BIG = [("w_in", True), ("w_out", False), ("w_q", False), ("w_kv", True), ("w_o", False), ("w_gate", False),
       ("w_up", False), ("w_down", False)]
TRANSPOSED = ("w_gate", "w_up")


def _ref_cols(pieces, a, b):
    cw = IN_W // 4
    out = []
    for j in range(4):
        lo, hi = max(a, j * cw), min(b, (j + 1) * cw)
        if lo < hi:
            out.append(pieces[j][:, lo - j * cw:hi - j * cw])
    return out


def _cat_cols(pieces):
    return jnp.concatenate([pieces[j] for j in range(4)], axis=1)


def _pack_in(w_in):
    main = jnp.concatenate(_ref_cols(w_in, 0, Z_END) + _ref_cols(w_in, DT_END, IN_W) + _ref_cols(w_in, Z_END, XBC_END), axis=1)
    return {"main": main, "dt": _pad_to(jnp.concatenate(_ref_cols(w_in, XBC_END, DT_END), axis=1), cols=128)}


def _pack_late(pc):
    rows = lambda n: pc[n].reshape(-1, pc[n].shape[-1])
    return {"out": rows("w_out"), "q": rows("w_q"), "kv": _cat_cols(pc["w_kv"]), "o": rows("w_o"),
            "gate_t": rows("w_gate"), "up_t": rows("w_up"), "down": rows("w_down")}


GW_KEY = {"w_gate": "gate_t", "w_up": "up_t", "w_kv": "kv", "w_out": "out", "w_q": "q", "w_o": "o", "w_down": "down"}
RS_GROUPS = (("w_down", "w_gate", "w_up"), ("w_out", "w_q", "w_kv", "w_o"), ("w_in",))


def _shard_grad(name, GW):
    if name == "w_in":
        cw = IN_W // 4
        pieces = []
        for j in range(4):
            parts = []
            for a, b, src, col in W_IN_SEGS:
                lo, hi = max(a, j * cw), min(b, (j + 1) * cw)
                if lo < hi:
                    parts.append(GW[src][:, col + lo - a:col + hi - a])
            pieces.append(jnp.concatenate(parts, axis=1))
        return jnp.stack(pieces)
    g = GW[GW_KEY[name]]
    if dict(BIG)[name]:
        cw = g.shape[1] // 4
        return jnp.stack([g[:, j * cw:(j + 1) * cw] for j in range(4)])
    return g.reshape(4, g.shape[0] // 4, g.shape[1])


def _rs_pair(names, GW, core):
    gs = [_shard_grad(n, GW) for n in names]
    halves = [g.reshape(4, 2, g.shape[1] // 2, g.shape[2]) for g in gs]
    theirs = _pair_split_list(halves, "rs_pair_send_" + names[0])
    return [_pair_sum(h, t, core, "rs_pair_sum_" + n) for h, t, n in zip(halves, theirs, names)]


def _stack_sc(dt_bias, a_log, d):
    return _pad_to(jnp.concatenate([dt_bias, a_log, d], axis=0), rows=8, cols=128)


COMM_PARAMS = pltpu.CompilerParams(vmem_limit_bytes=VMEM_LIMIT)


def _dma_sems(*counts):
    return [pltpu.SemaphoreType.DMA((n,)) for n in counts]


def _allgather_list(arrs, name):
    n = len(arrs)
    halved = [a.shape[0] % 16 == 0 for a in arrs]
    oshape = [(4, 2, a.shape[0] // 2, a.shape[1]) if h else (4, 1) + a.shape for a, h in zip(arrs, halved)]

    def body(*refs):
        srcs, outs = refs[:n], refs[n:2 * n]
        ici_send, ici_recv, own_send, own_recv, fwd_send, fwd_recv = refs[2 * n:]
        x, y, c = lax.axis_index("x"), lax.axis_index("y"), lax.axis_index("c")
        me = 2 * x + y
        sib = (x, y, 1 - c)
        peers = _chip_peers(x, y)

        def half(i, h):
            r = arrs[i].shape[0] // 2
            if not halved[i]:
                return srcs[i]
            return srcs[i].at[pl.ds(h * r if isinstance(h, int) else pl.multiple_of(h * r, 8), r)]

        ici, own, fwd = [], [], []
        for i in range(n):
            mine_h = c if halved[i] else 0
            for k, (px, py) in enumerate(peers):
                s = 3 * i + k
                ici.append(_remote(half(i, c), outs[i].at[me, mine_h], ici_send.at[s], ici_recv.at[s], (px, py, c)))
            for h in range(2 if halved[i] else 1):
                s = 2 * i + h
                own.append(_remote(half(i, h), outs[i].at[me, h], own_send.at[s], own_recv.at[s], sib))
        for cp in ici + own:
            cp.start()
        for i in range(n):
            if not halved[i]:
                continue
            for k, (px, py) in enumerate(peers):
                s = 3 * i + k
                got = outs[i].at[2 * px + py, c]
                _remote(half(i, c), got, ici_send.at[s], ici_recv.at[s], (px, py, c)).wait_recv()
                f = _remote(got, got, fwd_send.at[s], fwd_recv.at[s], sib)
                f.start()
                fwd.append(f)
        for i in range(n):
            for k, (px, py) in enumerate(peers):
                s = 3 * i + k
                if halved[i]:
                    _remote(half(i, c), outs[i].at[2 * px + py, 1 - c], fwd_send.at[s], fwd_recv.at[s], sib).wait_recv()
                else:
                    _remote(srcs[i], outs[i].at[2 * px + py, 0], ici_send.at[s], ici_recv.at[s], (px, py, c)).wait_recv()
            for h in range(2 if halved[i] else 1):
                s = 2 * i + h
                _remote(half(i, h), outs[i].at[me, h], own_send.at[s], own_recv.at[s], sib).wait_recv()
        for cp in ici + own + fwd:
            cp.wait_send()

    outs = pl.pallas_call(
        body, name=name, in_specs=[HBM_SPEC] * n, out_specs=[HBM_SPEC] * n,
        out_shape=[SDS(s, a.dtype) for s, a in zip(oshape, arrs)],
        scratch_shapes=_dma_sems(3 * n, 3 * n, 2 * n, 2 * n, 3 * n, 3 * n), compiler_params=COMM_PARAMS)(*arrs)
    return [o.reshape((4,) + a.shape) for o, a in zip(outs, arrs)]


def _pair_split_list(gs, name):
    n = len(gs)

    def body(*refs):
        srcs, outs = refs[:n], refs[n:2 * n]
        send_sems, recv_sems = refs[2 * n:]
        x, y, c = lax.axis_index("x"), lax.axis_index("y"), lax.axis_index("c")
        sib = (x, y, 1 - c)
        sends = [_remote(srcs[i].at[j, 1 - c], outs[i].at[j], send_sems.at[4 * i + j], recv_sems.at[4 * i + j], sib)
                 for i in range(n) for j in range(4)]
        for cp in sends:
            cp.start()
        for cp in sends:
            cp.wait_recv()
        for cp in sends:
            cp.wait_send()

    return pl.pallas_call(
        body, name=name, in_specs=[HBM_SPEC] * n, out_specs=[HBM_SPEC] * n,
        out_shape=[SDS((4,) + g.shape[2:], g.dtype) for g in gs],
        scratch_shapes=_dma_sems(4 * n, 4 * n), compiler_params=COMM_PARAMS)(*gs)


def _gather_finish_list(shards, bufs, name="allgather_finish"):
    n = len(shards)

    def body(*refs):
        srcs, outs = refs[:n], refs[2 * n:3 * n]
        own_send, own_recv, fwd_send, fwd_recv = refs[3 * n:]
        x, y, c = lax.axis_index("x"), lax.axis_index("y"), lax.axis_index("c")
        me = 2 * x + y
        sib = (x, y, 1 - c)
        sends, recvs = [], []
        for i in range(n):
            for h in range(2):
                own = _remote(_rows_half(srcs[i], shards[i].shape[0], h), outs[i].at[me, h],
                              own_send.at[2 * i + h], own_recv.at[2 * i + h], sib)
                sends.append(own)
                recvs.append(own)
            for k, (px, py) in enumerate(_chip_peers(x, y)):
                got, s = outs[i].at[2 * px + py, c], 3 * i + k
                sends.append(_remote(got, got, fwd_send.at[s], fwd_recv.at[s], sib))
                recvs.append(_remote(got, outs[i].at[2 * px + py, 1 - c], fwd_send.at[s], fwd_recv.at[s], sib))
        for cp in sends:
            cp.start()
        for cp in recvs:
            cp.wait_recv()
        for cp in sends:
            cp.wait_send()

    outs = pl.pallas_call(
        body, name=name, in_specs=[HBM_SPEC] * (2 * n), out_specs=[HBM_SPEC] * n,
        out_shape=[SDS(b.shape, b.dtype) for b in bufs], input_output_aliases={n + i: i for i in range(n)},
        scratch_shapes=_dma_sems(2 * n, 2 * n, 3 * n, 3 * n), compiler_params=COMM_PARAMS)(*shards, *bufs)
    return [o.reshape((4,) + a.shape) for o, a in zip(outs, shards)]


JOIN_SPLIT = 4


def _pair_join_list(bufs, name="rs_pair_join"):
    n = len(bufs)

    def body(*refs):
        outs = refs[n:2 * n]
        send_sems, recv_sems = refs[2 * n:]
        x, y, c = lax.axis_index("x"), lax.axis_index("y"), lax.axis_index("c")
        sib = (x, y, 1 - c)
        sends, recvs = [], []
        for i in range(n):
            rc = bufs[i].shape[1] // JOIN_SPLIT
            for q in range(JOIN_SPLIT):
                k = JOIN_SPLIT * i + q
                rows = pl.ds(q * rc, rc)
                sends.append(_remote(outs[i].at[c, rows], outs[i].at[c, rows], send_sems.at[k], recv_sems.at[k], sib))
                recvs.append(_remote(outs[i].at[c, rows], outs[i].at[1 - c, rows], send_sems.at[k], recv_sems.at[k], sib))
        for cp in sends:
            cp.start()
        for cp in recvs:
            cp.wait_recv()
        for cp in sends:
            cp.wait_send()

    return pl.pallas_call(
        body, name=name, in_specs=[HBM_SPEC] * n, out_specs=[HBM_SPEC] * n,
        out_shape=[SDS(b.shape, b.dtype) for b in bufs], input_output_aliases={i: i for i in range(n)},
        scratch_shapes=_dma_sems(JOIN_SPLIT * n, JOIN_SPLIT * n), compiler_params=COMM_PARAMS)(*bufs)


def _pair_sum(g, theirs, core, name):
    _, _, r, c = g.shape

    def body(core_ref, g_ref, t_ref, o_ref):
        o_ref[...] = (g_ref[...] + t_ref[...]).astype(o_ref.dtype)

    spec = pltpu.PrefetchScalarGridSpec(
        num_scalar_prefetch=1, grid=(4,),
        in_specs=[pl.BlockSpec((None, None, r, c), lambda j, core_ref: (j, core_ref[0], 0, 0)),
                  pl.BlockSpec((None, r, c), lambda j, core_ref: (j, 0, 0))],
        out_specs=pl.BlockSpec((None, r, c), lambda j, core_ref: (j, 0, 0)))
    return pl.pallas_call(body, name=name, grid_spec=spec, out_shape=SDS((4, r, c), BF16),
                          compiler_params=_cp("parallel"))(core, g, theirs)


def _chip_sum(own, got, where, name):
    _, r, c = own.shape
    tr = r // 2

    def body(w_ref, a_ref, b1_ref, b2_ref, b3_ref, o_ref):
        o_ref[...] = ((a_ref[...].astype(F32) + b1_ref[...].astype(F32)) + b2_ref[...].astype(F32)) + b3_ref[...].astype(F32)

    piece = lambda k: pl.BlockSpec((None, tr, c), lambda i, w_ref: ((w_ref[0] + k) % 4, i, 0))
    spec = pltpu.PrefetchScalarGridSpec(
        num_scalar_prefetch=1, grid=(r // tr,), in_specs=[piece(0), piece(1), piece(2), piece(3)],
        out_specs=pl.BlockSpec((None, tr, c), lambda i, w_ref: (w_ref[1], i, 0)))
    return pl.pallas_call(body, name=name, grid_spec=spec, out_shape=SDS((2, r, c), F32),
                          compiler_params=_cp("parallel"))(where, own, got, got, got)


def _adam_math(w, g, m, v):
    bc1 = 1.0 - ADAM_B1 ** ADAM_STEP
    bc2 = 1.0 - ADAM_B2 ** ADAM_STEP
    mn = ADAM_B1 * m + (1.0 - ADAM_B1) * g
    vn = ADAM_B2 * v + (1.0 - ADAM_B2) * (g * g)
    return -ADAM_LR * ((mn / bc1) / (jnp.sqrt(vn / bc2) + ADAM_EPS) + ADAM_WD * w), mn, vn


PACK_COLS = XBC
PACK = {"g_mix": (0, 1, D), "g_xattn": (1, 1, D), "g_mem": (2, 1, D), "g_ffn": (3, 1, D), "g_final": (4, 1, D),
        "ssd_norm_g": (5, 1, D), "cf_b": (6, 1, D), "ln_g": (7, 1, D), "ln_b": (8, 1, D), "conv4_b": (9, 1, XBC),
        "conv4_w": (10, KS, XBC), "sc": (16, 8, 128), "cf_w": (24, KC, D), "loss": (55, 1, 128)}
PACK_ROWS = 56
SMALL_ADAM = ["g_mix", "g_xattn", "g_mem", "g_ffn", "g_final", "ssd_norm_g", "cf_b", "ln_g", "ln_b", "conv4_b", "sc"]


def _small_allreduce_adamw(grads, wts, mom, var, name="allreduce_small"):
    gk = list(PACK)
    ng, na = len(gk), len(SMALL_ADAM)

    def body(*refs):
        g_in = refs[:ng]
        w_in, m_in, v_in = (refs[ng + i * na: ng + (i + 1) * na] for i in range(3))
        o = refs[ng + 3 * na:]
        g_out = o[:ng]
        d_out, m_out, v_out = (o[ng + i * na: ng + (i + 1) * na] for i in range(3))
        pack, buf, acc, send_sems, recv_sems = o[ng + 3 * na:]
        x, y, c = lax.axis_index("x"), lax.axis_index("y"), lax.axis_index("c")
        me = 4 * x + 2 * y + c
        pack[...] = jnp.zeros_like(pack)
        for i, k in enumerate(gk):
            r0, nr, nc = PACK[k]
            pack[r0:r0 + nr, 0:nc] = g_in[i][...]
        peers = [(x, y, 1 - c)] + [(px, py, pc) for px, py in _chip_peers(x, y) for pc in (c, 1 - c)]
        sends = [_remote(pack, buf.at[me], send_sems.at[k], recv_sems.at[k], dev) for k, dev in enumerate(peers)]
        for cp in sends:
            cp.start()
        buf[me] = pack[...]
        for k, (px, py, pc) in enumerate(peers):
            _remote(pack, buf.at[4 * px + 2 * py + pc], send_sems.at[k], recv_sems.at[k], (px, py, pc)).wait_recv()
        for cp in sends:
            cp.wait_send()
        tot = buf[0]
        for i in range(1, 8):
            tot = tot + buf[i]
        acc[...] = tot
        for i, k in enumerate(gk):
            r0, nr, nc = PACK[k]
            g_out[i][...] = acc[r0:r0 + nr, 0:nc]
        for i, k in enumerate(SMALL_ADAM):
            r0, nr, nc = PACK[k]
            d_out[i][...], m_out[i][...], v_out[i][...] = _adam_math(
                w_in[i][...], acc[r0:r0 + nr, 0:nc], m_in[i][...], v_in[i][...])

    args = [grads[k] for k in gk] + [d[k] for d in (wts, mom, var) for k in SMALL_ADAM]
    shp = lambda k: SDS((PACK[k][1], PACK[k][2]), F32)
    vm = pl.BlockSpec(memory_space=pltpu.VMEM)
    outs = pl.pallas_call(
        body, name=name, in_specs=[vm] * len(args), out_specs=[vm] * (ng + 3 * na),
        out_shape=[shp(k) for k in gk] + [shp(k) for _ in range(3) for k in SMALL_ADAM],
        scratch_shapes=[pltpu.VMEM((PACK_ROWS, PACK_COLS), F32), pltpu.VMEM((8, PACK_ROWS, PACK_COLS), F32),
                        pltpu.VMEM((PACK_ROWS, PACK_COLS), F32)] + _dma_sems(7, 7),
        compiler_params=COMM_PARAMS)(*args)
    red = dict(zip(gk, outs[:ng]))
    parts = [dict(zip(SMALL_ADAM, outs[ng + i * na: ng + (i + 1) * na])) for i in range(3)]
    return red, parts[0], parts[1], parts[2]


def _adamw_cols(w, gfull, m, v, chip, name):
    _, R, C = w.shape

    def body(w_idx, w_ref, g_ref, m_ref, v_ref, go_ref, d_ref, mo_ref, vo_ref):
        go_ref[...] = g_ref[...]
        d_ref[...], mo_ref[...], vo_ref[...] = _adam_math(w_ref[...], g_ref[...], m_ref[...], v_ref[...])

    blk = pl.BlockSpec((None, R, C), lambda i, w_idx: (0, 0, 0))
    spec = pltpu.PrefetchScalarGridSpec(
        num_scalar_prefetch=1, grid=(1,),
        in_specs=[blk, pl.BlockSpec((R, C), lambda i, w_idx: (0, w_idx[0])), blk, blk], out_specs=[blk] * 4)
    return pl.pallas_call(body, name=name, grid_spec=spec, out_shape=[SDS((1, R, C), F32)] * 4,
                          compiler_params=_cp("arbitrary"))(chip, w, gfull, m, v)


def _adamw(w, g, m, v, name):
    _, R, C = w.shape
    half = R // 2
    tr = _tile(half, max(8, (2 ** 17 // C) // 8 * 8), 8)
    nh = half // tr

    def body(w_ref, g_ref, m_ref, v_ref, go_ref, d_ref, mo_ref, vo_ref):
        go_ref[...] = g_ref[...]
        d_ref[...], mo_ref[...], vo_ref[...] = _adam_math(w_ref[...], g_ref[...], m_ref[...], v_ref[...])

    blk = pl.BlockSpec((None, tr, C), lambda i: (0, i, 0))
    gblk = pl.BlockSpec((None, tr, C), lambda i: (i // nh, i % nh, 0))
    return pl.pallas_call(body, name=name, grid=(R // tr,), in_specs=[blk, gblk, blk, blk], out_specs=[blk] * 4,
                          out_shape=[SDS((1, R, C), F32)] * 4, compiler_params=_cp("parallel"))(w, g, m, v)


WEIGHT_NAMES = ["norm_mix_g", "w_in", "ssd_conv_w", "ssd_conv_b", "ssd_dt_bias", "ssd_A_log", "ssd_D", "ssd_norm_g",
                "cf_conv_w", "cf_conv_b", "cf_ln_g", "cf_ln_b", "w_out", "norm_xattn_g", "norm_mem_g", "w_q", "w_kv",
                "w_o", "norm_ffn_g", "w_gate", "w_up", "w_down", "norm_final_g"]
VEC_REF = [("norm_mix_g", "g_mix"), ("norm_xattn_g", "g_xattn"), ("norm_mem_g", "g_mem"), ("norm_ffn_g", "g_ffn"),
           ("norm_final_g", "g_final"), ("ssd_norm_g", "ssd_norm_g"), ("cf_conv_b", "cf_b"), ("cf_ln_g", "ln_g"),
           ("cf_ln_b", "ln_b"), ("ssd_conv_b", "conv4_b")]
SC_REF = ["ssd_dt_bias", "ssd_A_log", "ssd_D"]


def _small_side(get):
    d = {k: get(ref_name).reshape(1, -1) for ref_name, k in VEC_REF}
    d["sc"] = _stack_sc(*[get(n) for n in SC_REF])
    return d


def kernel(x, mem, norm_mix_g, w_in, ssd_conv_w, ssd_conv_b, ssd_dt_bias, ssd_A_log, ssd_D, ssd_norm_g, cf_conv_w, cf_conv_b, cf_ln_g, cf_ln_b, w_out, norm_xattn_g, norm_mem_g, w_q, w_kv, w_o, norm_ffn_g, w_gate, w_up, w_down, norm_final_g, loss_target, m_norm_mix_g, m_w_in, m_ssd_conv_w, m_ssd_conv_b, m_ssd_dt_bias, m_ssd_A_log, m_ssd_D, m_ssd_norm_g, m_cf_conv_w, m_cf_conv_b, m_cf_ln_g, m_cf_ln_b, m_w_out, m_norm_xattn_g, m_norm_mem_g, m_w_q, m_w_kv, m_w_o, m_norm_ffn_g, m_w_gate, m_w_up, m_w_down, m_norm_final_g, v_norm_mix_g, v_w_in, v_ssd_conv_w, v_ssd_conv_b, v_ssd_dt_bias, v_ssd_A_log, v_ssd_D, v_ssd_norm_g, v_cf_conv_w, v_cf_conv_b, v_cf_ln_g, v_cf_ln_b, v_w_out, v_norm_xattn_g, v_norm_mem_g, v_w_q, v_w_kv, v_w_o, v_norm_ffn_g, v_w_gate, v_w_up, v_w_down, v_norm_final_g):
    env = dict(locals())
    view = lambda n, a: a.transpose(0, 2, 1) if n in TRANSPOSED else a
    wts = {n: view(n, env[n]) for n in WEIGHT_NAMES}
    mom = {n: view(n, env["m_" + n]) for n in WEIGHT_NAMES}
    var = {n: view(n, env["v_" + n]) for n in WEIGHT_NAMES}
    chip = (2 * lax.axis_index("x") + lax.axis_index("y")).astype(jnp.int32).reshape(1)
    core = lax.axis_index("c").astype(jnp.int32).reshape(1)
    where = jnp.concatenate([chip, core])
    big = [n for n, _ in BIG]

    w_in_g, conv4_g, cf_g = _allgather_list([w_in[0].astype(BF16), ssd_conv_w[0], cf_conv_w[0]], "allgather_first")
    W = _pack_in(w_in_g)
    P = _small_side(lambda n: wts[n])
    P["conv4_w"], P["cf_w"] = _cat_cols(conv4_g), _cat_cols(cf_g)
    late = {n: wts[n][0].astype(BF16) for grp in AG_RIDE for n in grp}

    loss, grad_x, GW, GP, pair, got = _local_step(x[0], mem[0], loss_target[0], W, P, core, late)
    joined = _pair_join_list([_chip_sum(pair[n], got[n], where, "rs_chip_sum_" + n) for n in big])
    gshard = dict(zip(big, joined))

    small = dict(GP)
    small["loss"] = loss
    red, sd, sm, sv = _small_allreduce_adamw(small, {k: P[k] for k in SMALL_ADAM}, _small_side(lambda n: mom[n]),
                                             _small_side(lambda n: var[n]))
    grads, delta, new_m, new_v = {}, {}, {}, {}
    for ref_name, k in VEC_REF:
        shp = wts[ref_name].shape
        for dst, src in ((grads, red), (delta, sd), (new_m, sm), (new_v, sv)):
            dst[ref_name] = src[k].reshape(shp)
    for row, ref_name in enumerate(SC_REF):
        for dst, src in ((grads, red), (delta, sd), (new_m, sm), (new_v, sv)):
            dst[ref_name] = src["sc"][row:row + 1, :NH]

    for n, k in (("ssd_conv_w", "conv4_w"), ("cf_conv_w", "cf_w")):
        grads[n], delta[n], new_m[n], new_v[n] = _adamw_cols(wts[n], red[k], mom[n], var[n], chip, "adamw_" + n)
    for n in big:
        outs = _adamw(wts[n], gshard[n], mom[n], var[n], "adamw_" + n)
        grads[n], delta[n], new_m[n], new_v[n] = [view(n, o) for o in outs]

    return (red["loss"][0, 0], grad_x[None], *[grads[n] for n in WEIGHT_NAMES], *[delta[n] for n in WEIGHT_NAMES],
            *[new_m[n] for n in WEIGHT_NAMES], *[new_v[n] for n in WEIGHT_NAMES])
```

```python
import functools
import math

import jax
import jax.numpy as jnp
from jax import lax
from jax.experimental import pallas as pl
from jax.experimental.pallas import tpu as pltpu

F32 = jnp.float32
BF16 = jnp.bfloat16
_MXU = BF16

D = 1024
MEM = 256
NH, HP, NG, NS = 16, 64, 2, 128
GW = NH * HP // NG
CH = 128
XBC = NH * HP + 2 * NG * NS
KS, KC = 4, 31
XH, XD = 4, 256
DFF = 2816
EPS = 1e-6
COL_Z, COL_A, COL_G, COL_XBC, MAINW = 0, 1024, 2048, 3072, 4608
VMEM_LIMIT = 56 * 2 ** 20

ADAM_LR, ADAM_B1, ADAM_B2, ADAM_EPS, ADAM_WD, ADAM_STEP = 0.001, 0.9, 0.999, 1e-08, 0.01, 10

SDS = jax.ShapeDtypeStruct
MESHID = pl.DeviceIdType.MESH


def _cp(*sem):
    return pltpu.CompilerParams(dimension_semantics=sem, vmem_limit_bytes=VMEM_LIMIT)


def _tile(n, cap, unit=128):
    if n <= cap:
        return n
    best = None
    for t in range(unit, cap + 1, unit):
        if n % t == 0:
            best = t
    assert best is not None, (n, cap)
    return best


def _sigmoid(x):
    return 1.0 / (1.0 + jnp.exp(-x))


def _silu(x):
    return x * _sigmoid(x)


def _dsilu(x):
    s = _sigmoid(x)
    return s * (1.0 + x * (1.0 - s))


def _softplus(x):
    return jnp.maximum(x, 0.0) + jnp.log(1.0 + jnp.exp(-jnp.abs(x)))


def _split_bf16(x, passes):
    parts, r = [], x.astype(F32)
    for _ in range(passes):
        p = r.astype(BF16)
        parts.append(p)
        r = r - p.astype(F32)
    return parts


def _dot(a, b, dims=None, exact=None, passes=2):
    dn = {None: (((1,), (0,)), ((), ())), "nt": (((1,), (1,)), ((), ())), "tn": (((0,), (0,)), ((), ()))}[dims]
    if exact is None:
        return lax.dot_general(a.astype(_MXU), b.astype(_MXU), dn, preferred_element_type=F32)
    if exact == "a":
        terms = [(a.astype(BF16), p) for p in _split_bf16(b, passes)]
    else:
        terms = [(p, b.astype(BF16)) for p in _split_bf16(a, passes)]
    out = None
    for lhs, rhs in terms:
        d = lax.dot_general(lhs, rhs, dn, preferred_element_type=F32)
        out = d if out is None else out + d
    return out


def _mm_nn(a, b, name, add=None, out_dtype=F32, tm_cap=1024, tn_cap=1408, riders=None):
    M, K = a.shape
    _, N = b.shape
    tm, tn = _tile(M, tm_cap, 8), _tile(N, tn_cap)
    rd = _Riders(riders or ())
    nin, nco = 2 + (add is not None), len(rd.arrays())
    grid = (N // tn, M // tm)

    def body(*refs):
        a_ref, b_ref, o_ref = refs[0], refs[1], refs[nin + nco]
        rd.bind(refs[nin:nin + nco], refs[nin + nco + 1:nin + 2 * nco + 1], refs[nin + 2 * nco + 1:], grid).start()
        acc = _dot(a_ref[...], b_ref[...])
        if add is not None:
            acc = acc + refs[2][...]
        o_ref[...] = acc.astype(o_ref.dtype)
        rd.finish()

    in_specs = [pl.BlockSpec((tm, K), lambda j, i: (i, 0)), pl.BlockSpec((K, tn), lambda j, i: (0, j))]
    args = [a, b]
    if add is not None:
        in_specs.append(pl.BlockSpec((tm, tn), lambda j, i: (i, j)))
        args.append(add)
    order = ("arbitrary", "arbitrary") if nco else ("parallel", "parallel")
    outs = pl.pallas_call(
        body, name=name, grid=grid, in_specs=in_specs + [HBM_SPEC] * nco,
        out_specs=[pl.BlockSpec((tm, tn), lambda j, i: (i, j))] + [HBM_SPEC] * nco,
        out_shape=[SDS((M, N), out_dtype)] + rd.out_shapes(),
        scratch_shapes=rd.scratch(), compiler_params=_cp(*order))(*args, *rd.arrays())
    return outs[0] if riders is None else (outs[0], rd.split(outs[1:]))


def _mm_nt(a, b, name, add=None, out_dtype=F32, tm_cap=512, tk_cap=1024, riders=None):
    M, N = a.shape
    K = b.shape[0]
    tm, tk = _tile(M, tm_cap, 8), _tile(K, tk_cap)
    rd = _Riders(riders or ())
    nin, nco = 2 + (add is not None), len(rd.arrays())
    grid = (K // tk, M // tm)

    def body(*refs):
        a_ref, b_ref, o_ref = refs[0], refs[1], refs[nin + nco]
        rd.bind(refs[nin:nin + nco], refs[nin + nco + 1:nin + 2 * nco + 1], refs[nin + 2 * nco + 1:], grid).start()
        acc = _dot(a_ref[...], b_ref[...], "nt")
        if add is not None:
            acc = acc + refs[2][...]
        o_ref[...] = acc.astype(o_ref.dtype)
        rd.finish()

    in_specs = [pl.BlockSpec((tm, N), lambda j, i: (i, 0)), pl.BlockSpec((tk, N), lambda j, i: (j, 0))]
    args = [a, b]
    if add is not None:
        in_specs.append(pl.BlockSpec((tm, tk), lambda j, i: (i, j)))
        args.append(add)
    order = ("arbitrary", "arbitrary") if nco else ("parallel", "parallel")
    outs = pl.pallas_call(
        body, name=name, grid=grid, in_specs=in_specs + [HBM_SPEC] * nco,
        out_specs=[pl.BlockSpec((tm, tk), lambda j, i: (i, j))] + [HBM_SPEC] * nco,
        out_shape=[SDS((M, K), out_dtype)] + rd.out_shapes(),
        scratch_shapes=rd.scratch(), compiler_params=_cp(*order))(*args, *rd.arrays())
    return outs[0] if riders is None else (outs[0], rd.split(outs[1:]))


def _mm_tn(a, b, name, tm_cap=1024, tk_cap=512, tn_cap=1408):
    M, K = a.shape
    _, N = b.shape
    tm, tk, tn = _tile(M, tm_cap, 8), _tile(K, tk_cap), _tile(N, tn_cap)

    def body(a_ref, b_ref, o_ref):
        @pl.when(pl.program_id(2) == 0)
        def _():
            o_ref[...] = jnp.zeros_like(o_ref)

        o_ref[...] += _dot(a_ref[...], b_ref[...], "tn")

    return pl.pallas_call(
        body, name=name, grid=(K // tk, N // tn, M // tm),
        in_specs=[pl.BlockSpec((tm, tk), lambda k, n, m: (m, k)), pl.BlockSpec((tm, tn), lambda k, n, m: (m, n))],
        out_specs=pl.BlockSpec((tk, tn), lambda k, n, m: (k, n)), out_shape=SDS((K, N), F32),
        compiler_params=_cp("parallel", "parallel", "arbitrary"))(a, b)


def _rms_fwd(x, g, name, tb_cap=512):
    S, Dm = x.shape
    tb = _tile(S, tb_cap, 8)

    def body(x_ref, g_ref, o_ref):
        xv = x_ref[...]
        r = lax.rsqrt(jnp.mean(xv * xv, axis=-1, keepdims=True) + EPS)
        o_ref[...] = (xv * r * g_ref[...]).astype(o_ref.dtype)

    return pl.pallas_call(
        body, name=name, grid=(S // tb,),
        in_specs=[pl.BlockSpec((tb, Dm), lambda i: (i, 0)), pl.BlockSpec((1, Dm), lambda i: (0, 0))],
        out_specs=pl.BlockSpec((tb, Dm), lambda i: (i, 0)), out_shape=SDS((S, Dm), _MXU),
        compiler_params=_cp("parallel"))(x, g)


def _rms_bwd(x, g, dh, dres, name, tb_cap=512, low=True):
    S, Dm = x.shape
    tb = _tile(S, tb_cap, 8)
    need_dx = dres is not None

    def body(x_ref, g_ref, dh_ref, *rest):
        dg_ref = rest[-1]
        xv = x_ref[...]
        r = lax.rsqrt(jnp.mean(xv * xv, axis=-1, keepdims=True) + EPS)
        xh = xv * r
        dy = dh_ref[...].astype(F32)

        @pl.when(pl.program_id(0) == 0)
        def _():
            dg_ref[...] = jnp.zeros_like(dg_ref)

        dg_ref[...] += jnp.sum(dy * xh, axis=0, keepdims=True)
        if need_dx:
            gdy = dy * g_ref[...]
            dx = r * (gdy - xh * jnp.mean(xh * gdy, axis=-1, keepdims=True))
            tot = rest[0][...] + dx
            rest[1][...] = tot
            if low:
                rest[2][...] = tot.astype(rest[2].dtype)

    row = pl.BlockSpec((tb, Dm), lambda i: (i, 0))
    vec = pl.BlockSpec((1, Dm), lambda i: (0, 0))
    if need_dx:
        outs = [SDS((S, Dm), F32)] + ([SDS((S, Dm), _MXU)] if low else [])
        return pl.pallas_call(
            body, name=name, grid=(S // tb,), in_specs=[row, vec, row, row], out_specs=[row] * len(outs) + [vec],
            out_shape=outs + [SDS((1, Dm), F32)], compiler_params=_cp("arbitrary"))(x, g, dh, dres)
    return pl.pallas_call(
        body, name=name, grid=(S // tb,), in_specs=[row, vec, row], out_specs=vec,
        out_shape=SDS((1, Dm), F32), compiler_params=_cp("arbitrary"))(x, g, dh)


def _final_loss(x, g, tgt, name="final_loss", tb_cap=512):
    S, Dm = x.shape
    tb = _tile(S, tb_cap, 8)

    def body(x_ref, g_ref, t_ref, loss_ref, dx_ref, dxl_ref, dg_ref):
        xv = x_ref[...]
        gv = g_ref[...]
        r = lax.rsqrt(jnp.mean(xv * xv, axis=-1, keepdims=True) + EPS)
        xh = xv * r
        e = xh * gv - t_ref[...]

        @pl.when(pl.program_id(0) == 0)
        def _():
            loss_ref[...] = jnp.zeros_like(loss_ref)
            dg_ref[...] = jnp.zeros_like(dg_ref)

        loss_ref[...] += 0.5 * jnp.sum(jnp.mean(e * e, axis=-1, keepdims=True))
        dy = e * (1.0 / Dm)
        dg_ref[...] += jnp.sum(dy * xh, axis=0, keepdims=True)
        gdy = dy * gv
        dx = r * (gdy - xh * jnp.mean(xh * gdy, axis=-1, keepdims=True))
        dx_ref[...] = dx
        dxl_ref[...] = dx.astype(dxl_ref.dtype)

    row = pl.BlockSpec((tb, Dm), lambda i: (i, 0))
    vec = pl.BlockSpec((1, Dm), lambda i: (0, 0))
    return pl.pallas_call(
        body, name=name, grid=(S // tb,), in_specs=[row, vec, row],
        out_specs=[pl.BlockSpec((1, 128), lambda i: (0, 0)), row, row, vec],
        out_shape=[SDS((1, 128), F32), SDS((S, Dm), F32), SDS((S, Dm), _MXU), SDS((1, Dm), F32)],
        compiler_params=_cp("arbitrary"))(x, g, tgt)


SSD_HALO = 8
CF_HALO = 32

HBM_SPEC = pl.BlockSpec(memory_space=pl.ANY)


def _chip_peers(x, y):
    return [(1 - x, y), (x, 1 - y), (1 - x, 1 - y)]


def _remote(src, dst, send_sem, recv_sem, dev):
    return pltpu.make_async_remote_copy(src_ref=src, dst_ref=dst, send_sem=send_sem, recv_sem=recv_sem,
                                        device_id=dev, device_id_type=MESHID)


def _scatter_copies(srcs, outs, send_sems, recv_sems):
    x, y, c = lax.axis_index("x"), lax.axis_index("y"), lax.axis_index("c")
    me = 2 * x + y
    sends, recvs = [], []
    for i, (s, o) in enumerate(zip(srcs, outs)):
        for k, (px, py) in enumerate(_chip_peers(x, y)):
            j = 3 * i + k
            sends.append(_remote(s.at[2 * px + py], o.at[me], send_sems.at[j], recv_sems.at[j], (px, py, c)))
            recvs.append(_remote(s.at[me], o.at[2 * px + py], send_sems.at[j], recv_sems.at[j], (px, py, c)))
    return sends, recvs


def _pair_copies(srcs, outs, send_sems, recv_sems):
    x, y, c = lax.axis_index("x"), lax.axis_index("y"), lax.axis_index("c")
    sends = [_remote(s.at[j, 1 - c], o.at[j], send_sems.at[4 * i + j], recv_sems.at[4 * i + j], (x, y, 1 - c))
             for i, (s, o) in enumerate(zip(srcs, outs)) for j in range(4)]
    return sends, sends


def _rows_half(ref, rows, h):
    r = rows // 2
    return ref.at[pl.ds(h * r if isinstance(h, int) else pl.multiple_of(h * r, 8), r)]


def _gather_copies(srcs, outs, rows, send_sems, recv_sems):
    x, y, c = lax.axis_index("x"), lax.axis_index("y"), lax.axis_index("c")
    me = 2 * x + y
    sends, recvs = [], []
    for i, (s, o) in enumerate(zip(srcs, outs)):
        mine = _rows_half(s, rows[i], c)
        for k, (px, py) in enumerate(_chip_peers(x, y)):
            j = 3 * i + k
            sends.append(_remote(mine, o.at[me, c], send_sems.at[j], recv_sems.at[j], (px, py, c)))
            recvs.append(_remote(mine, o.at[2 * px + py, c], send_sems.at[j], recv_sems.at[j], (px, py, c)))
    return sends, recvs


def _gather_shapes(shards):
    return [SDS((4, 2, a.shape[0] // 2, a.shape[1]), a.dtype) for a in shards]


class _Rider:
    SEMS_PER_ARRAY = {"exchange": 3, "gather": 3, "pair": 4}

    def __init__(self, kind, arrays):
        self.kind, self.arrays = kind, list(arrays)

    def out_shapes(self):
        if self.kind == "gather":
            return _gather_shapes(self.arrays)
        if self.kind == "pair":
            return [SDS((4,) + a.shape[2:], a.dtype) for a in self.arrays]
        return [SDS(a.shape, a.dtype) for a in self.arrays]

    def scratch(self):
        n = self.SEMS_PER_ARRAY[self.kind] * len(self.arrays)
        return [pltpu.SemaphoreType.DMA((n,)), pltpu.SemaphoreType.DMA((n,))]

    def copies(self, srcs, outs, send_sems, recv_sems):
        if self.kind == "gather":
            return _gather_copies(srcs, outs, [a.shape[0] for a in self.arrays], send_sems, recv_sems)
        if self.kind == "pair":
            return _pair_copies(srcs, outs, send_sems, recv_sems)
        return _scatter_copies(srcs, outs, send_sems, recv_sems)


class _Riders:
    def __init__(self, riders):
        self.given = list(riders)
        self.riders = [r for r in self.given if r.arrays]

    def arrays(self):
        return [a for r in self.riders for a in r.arrays]

    def out_shapes(self):
        return [s for r in self.riders for s in r.out_shapes()]

    def scratch(self):
        return [s for r in self.riders for s in r.scratch()]

    def split(self, outs):
        res, k = [], 0
        for r in self.given:
            res.append(list(outs[k:k + len(r.arrays)]))
            k += len(r.arrays)
        return res

    def bind(self, in_refs, out_refs, sem_refs, steps):
        self.steps = steps if isinstance(steps, tuple) else (steps,)
        self.bound, k = [], 0
        for i, r in enumerate(self.riders):
            n = len(r.arrays)
            self.bound.append((r, in_refs[k:k + n], out_refs[k:k + n], sem_refs[2 * i], sem_refs[2 * i + 1]))
            k += n
        return self

    def _at(self, last):
        hit = None
        for ax, n in enumerate(self.steps):
            here = pl.program_id(ax) == (n - 1 if last else 0)
            hit = here if hit is None else jnp.logical_and(hit, here)
        return hit

    def _copies(self):
        sends, recvs = [], []
        for r, srcs, outs, send_sems, recv_sems in self.bound:
            s, w = r.copies(srcs, outs, send_sems, recv_sems)
            sends += s
            recvs += w
        return sends, recvs

    def start(self):
        if self.riders:
            @pl.when(self._at(last=False))
            def _():
                for cp in self._copies()[0]:
                    cp.start()

    def finish(self):
        if self.riders:
            @pl.when(self._at(last=True))
            def _():
                sends, recvs = self._copies()
                for cp in recvs:
                    cp.wait_recv()
                for cp in sends:
                    cp.wait_send()


def _head_consts():
    e = (lax.broadcasted_iota(jnp.int32, (128, NH * HP), 1) // HP == lax.broadcasted_iota(jnp.int32, (128, NH * HP), 0)).astype(F32)
    et = (lax.broadcasted_iota(jnp.int32, (NH * HP, 128), 0) // HP == lax.broadcasted_iota(jnp.int32, (NH * HP, 128), 1)).astype(F32)
    r = lax.broadcasted_iota(jnp.int32, (CH, CH), 0)
    c = lax.broadcasted_iota(jnp.int32, (CH, CH), 1)
    return e, et, (c <= r), (r <= c)


def _ssd_common(xbc_c, dtr, dtb, alog, e, tril, triu):
    xbc = _silu(xbc_c)
    xs = xbc[:, :NH * HP]
    dt = _softplus(dtr + dtb)
    A = -jnp.exp(alog)
    a = dt * A
    cs = _dot(tril, a, exact="a", passes=3)
    csT = _dot(a, triu, "tn", exact="b", passes=3)
    csL = cs[CH - 1:CH, :]
    wdec = jnp.exp(csL - cs) * dt
    dtE = _dot(dt, e, exact="b")
    ecsE = _dot(jnp.exp(cs), e, exact="b")
    wE = _dot(wdec, e, exact="b")
    eL = jnp.exp(csL)
    return xbc, xs, dt, A, cs, csT, csL, wdec, dtE, ecsE, wE, eL


def _ssd_fwd(proj, cw, cb, dtr, sc, norm_g, riders=(), name="ssd_fwd"):
    S = proj.shape[0]
    nc = S // CH
    rd = _Riders(riders)
    nco = len(rd.arrays())

    def body(*refs):
        z_ref, xp_ref, cw_ref, cb_ref, dtr_ref, sc_ref, ng_ref = refs[:7]
        xc_ref, y_ref, yn_ref, hp_ref = refs[7 + nco:11 + nco]
        hst, cext = refs[11 + 2 * nco:13 + 2 * nco]
        rd.bind(refs[7:7 + nco], refs[11 + nco:11 + 2 * nco], refs[13 + 2 * nco:], nc).start()

        @pl.when(pl.program_id(0) == 0)
        def _():
            hst[...] = jnp.zeros_like(hst)
            cext[pl.ds(0, SSD_HALO), :] = jnp.zeros((SSD_HALO, XBC), F32)

        cext[pl.ds(SSD_HALO, CH), :] = xp_ref[...]
        xc = jnp.zeros((CH, XBC), F32) + cb_ref[...]
        for k in range(KS):
            xc = xc + cext[pl.ds(SSD_HALO - (KS - 1) + k, CH), :] * cw_ref[k:k + 1, :]
        xc_ref[...] = xc
        cext[pl.ds(0, SSD_HALO), :] = cext[pl.ds(CH, SSD_HALO), :]

        e, et, tril, triu = _head_consts()
        xbc, xs, dt, A, cs, csT, csL, wdec, dtE, ecsE, wE, eL = _ssd_common(
            xc, dtr_ref[...], sc_ref[0:1, :], sc_ref[1:2, :], e, tril, triu)
        hp_ref[0] = hst[...]
        xd = xs * dtE
        xw = xs * wE
        dE = _dot(jnp.broadcast_to(sc_ref[2:3, :], (8, 128)), e, exact="b", passes=3)[0:1, :]
        eLcol = jnp.sum(et * eL, axis=1, keepdims=True)
        for g in range(NG):
            Bg = xbc[:, NH * HP + g * NS: NH * HP + (g + 1) * NS]
            Cg = xbc[:, NH * HP + NG * NS + g * NS: NH * HP + NG * NS + (g + 1) * NS]
            gs = slice(g * GW, (g + 1) * GW)
            G = _dot(Cg, Bg, "nt")
            hg = hst[gs, :]
            yoff = ecsE[:, gs] * _dot(Cg, hg, "nt")
            hst[gs, :] = eLcol[gs, :] * hg + _dot(xw[:, gs], Bg, "tn")
            for hh in range(NH // NG):
                h = g * (NH // NG) + hh
                hs = slice(h * HP, (h + 1) * HP)
                m = jnp.where(tril, jnp.exp(jnp.where(tril, cs[:, h:h + 1] - csT[h:h + 1, :], 0.0)), 0.0)
                yd = _dot(G * m, xd[:, hs])
                y_ref[:, hs] = yd + yoff[:, hh * HP:(hh + 1) * HP] + dE[:, hs] * xs[:, hs]
        y = y_ref[...]
        yz = y * _silu(z_ref[...])
        for g in range(NG):
            gs = slice(g * GW, (g + 1) * GW)
            yg = yz[:, gs]
            r = lax.rsqrt(jnp.mean(yg * yg, axis=-1, keepdims=True) + EPS)
            yn_ref[:, gs] = (yg * r * ng_ref[:, gs]).astype(yn_ref.dtype)
        rd.finish()

    outs = pl.pallas_call(
        body, name=name, grid=(nc,),
        in_specs=[pl.BlockSpec((CH, D), lambda c: (c, COL_Z // D)),
                  pl.BlockSpec((CH, XBC), lambda c: (c, COL_XBC // XBC)),
                  pl.BlockSpec((KS, XBC), lambda c: (0, 0)),
                  pl.BlockSpec((1, XBC), lambda c: (0, 0)),
                  pl.BlockSpec((CH, 128), lambda c: (c, 0)),
                  pl.BlockSpec((8, 128), lambda c: (0, 0)),
                  pl.BlockSpec((1, D), lambda c: (0, 0))] + [HBM_SPEC] * nco,
        out_specs=[pl.BlockSpec((CH, XBC), lambda c: (c, 0)), pl.BlockSpec((CH, D), lambda c: (c, 0)),
                   pl.BlockSpec((CH, D), lambda c: (c, 0)),
                   pl.BlockSpec((1, NH * HP, NS), lambda c: (c, 0, 0))] + [HBM_SPEC] * nco,
        out_shape=[SDS((S, XBC), F32), SDS((S, D), F32), SDS((S, D), _MXU), SDS((nc, NH * HP, NS), F32)]
        + rd.out_shapes(),
        scratch_shapes=[pltpu.VMEM((NH * HP, NS), F32), pltpu.VMEM((SSD_HALO + CH, XBC), F32)] + rd.scratch(),
        compiler_params=_cp("arbitrary"))(proj, proj, cw, cb, dtr, sc, norm_g, *rd.arrays())
    return outs[:4], rd.split(outs[4:])


def _ssd_bwd(dmix, y, proj, xbc_c, dtr, hprev, cw, sc, norm_g, riders=(), name="ssd_bwd"):
    S = proj.shape[0]
    nc = S // CH
    rd = _Riders(riders)
    nco = len(rd.arrays())
    rev = lambda c: nc - 1 - c

    def body(*refs):
        dyn_ref, y_ref, z_ref, x_ref, xp_ref, dtr_ref, hp_ref, cw_ref, sc_ref, ng_ref = refs[:10]
        dz_ref, dx_ref, ddtr_ref, gcw_ref, gcb_ref, gsc_ref, gng_ref = refs[10 + nco:17 + nco]
        dh, dxd, cext = refs[17 + 2 * nco:20 + 2 * nco]
        rd.bind(refs[10:10 + nco], refs[17 + nco:17 + 2 * nco], refs[20 + 2 * nco:], nc).start()

        @pl.when(pl.program_id(0) == 0)
        def _():
            dh[...] = jnp.zeros_like(dh)
            cext[pl.ds(CH, SSD_HALO), :] = jnp.zeros((SSD_HALO, XBC), F32)
            gcw_ref[...] = jnp.zeros_like(gcw_ref)
            gcb_ref[...] = jnp.zeros_like(gcb_ref)
            gsc_ref[...] = jnp.zeros_like(gsc_ref)
            gng_ref[...] = jnp.zeros_like(gng_ref)

        e, et, tril, triu = _head_consts()
        xbc_c = x_ref[...]
        dtr = dtr_ref[...]
        dtb = sc_ref[0:1, :]
        xbc, xs, dt, A, cs, csT, csL, wdec, dtE, ecsE, wE, eL = _ssd_common(
            xbc_c, dtr, dtb, sc_ref[1:2, :], e, tril, triu)
        xd = xs * dtE
        xw = xs * wE
        dE = _dot(jnp.broadcast_to(sc_ref[2:3, :], (8, 128)), e, exact="b", passes=3)[0:1, :]
        eLcol = jnp.sum(et * eL, axis=1, keepdims=True)

        yv = y_ref[...]
        zv = z_ref[...]
        sz = _silu(zv)
        yz = yv * sz
        dyn = dyn_ref[...]
        dyz_parts = []
        for g in range(NG):
            gs = slice(g * GW, (g + 1) * GW)
            yg = yz[:, gs]
            r = lax.rsqrt(jnp.mean(yg * yg, axis=-1, keepdims=True) + EPS)
            yh = yg * r
            dn = dyn[:, gs]
            gng_ref[:, gs] += jnp.sum(dn * yh, axis=0, keepdims=True)
            gdn = dn * ng_ref[:, gs]
            dyz_parts.append(r * (gdn - yh * jnp.mean(yh * gdn, axis=-1, keepdims=True)))
        dyz = jnp.concatenate(dyz_parts, axis=1)
        dy = dyz * sz
        dz_ref[...] = (dyz * yv * _dsilu(zv)).astype(dz_ref.dtype)

        dxs = dE * dy
        dzo = ecsE * dy
        dcsL = jnp.zeros((1, 128), F32)
        ddt = jnp.zeros((CH, 128), F32)
        qcols = jnp.zeros((CH, 128), F32)
        qrows = jnp.zeros((128, CH), F32)
        lane = lax.broadcasted_iota(jnp.int32, (1, 128), 1)
        sub = lax.broadcasted_iota(jnp.int32, (128, 1), 0)
        dB_parts, dC_parts, yoff_parts, dxw_parts = [], [], [], []
        for g in range(NG):
            Bg = xbc[:, NH * HP + g * NS: NH * HP + (g + 1) * NS]
            Cg = xbc[:, NH * HP + NG * NS + g * NS: NH * HP + NG * NS + (g + 1) * NS]
            gs = slice(g * GW, (g + 1) * GW)
            hg = hp_ref[0, gs, :]
            dhn = dh[gs, :]
            G = _dot(Cg, Bg, "nt")
            yoff_parts.append(ecsE[:, gs] * _dot(Cg, hg, "nt"))
            dC = _dot(dzo[:, gs], hg)
            dhp = _dot(dzo[:, gs], Cg, "tn") + eLcol[gs, :] * dhn
            t1 = jnp.sum(dhn * hg, axis=1, keepdims=True) * eLcol[gs, :]
            dcsL = dcsL + jnp.sum(et[gs, :] * t1, axis=0, keepdims=True)
            dxw_parts.append(_dot(Bg, dhn, "nt"))
            dB = _dot(xw[:, gs], dhn)
            dgsum = jnp.zeros((CH, CH), F32)
            for hh in range(NH // NG):
                h = g * (NH // NG) + hh
                hs = slice(h * HP, (h + 1) * HP)
                m = jnp.where(tril, jnp.exp(jnp.where(tril, cs[:, h:h + 1] - csT[h:h + 1, :], 0.0)), 0.0)
                sc = G * m
                dyh = dy[:, hs]
                dxd[:, hs] = _dot(sc, dyh, "tn")
                dsc = _dot(dyh, xd[:, hs], "nt")
                q = dsc * sc
                qcols = qcols + jnp.where(lane == h, jnp.sum(q, axis=1, keepdims=True), 0.0)
                qrows = qrows + jnp.where(sub == h, jnp.sum(q, axis=0, keepdims=True), 0.0)
                dgsum = dgsum + dsc * m
            dC_parts.append(dC + _dot(dgsum, Bg))
            dB_parts.append(dB + _dot(dgsum, Cg, "tn"))
            dh[gs, :] = dhp
        yoff = jnp.concatenate(yoff_parts, axis=1)
        dxw = jnp.concatenate(dxw_parts, axis=1)
        dxdv = dxd[...]
        per_head = _dot(jnp.concatenate([dy * yoff, dxw * xs, dxdv * xs, dy * xs], axis=0), et, exact="b")
        dcs = qcols - qrows.T + per_head[0:CH]
        dw = per_head[CH:2 * CH]
        gsc_ref[2:3, :] += jnp.sum(per_head[3 * CH:4 * CH], axis=0, keepdims=True)
        dxs = dxs + wE * dxw + dtE * dxdv
        ddt = ddt + dw * jnp.exp(csL - cs) + per_head[2 * CH:3 * CH]
        dcs = dcs - dw * wdec
        dcsL = dcsL + jnp.sum(dw * wdec, axis=0, keepdims=True)
        last = lax.broadcasted_iota(jnp.int32, (CH, 128), 0) == CH - 1
        dcs = dcs + jnp.where(last, dcsL, 0.0)
        da = _dot(triu, dcs, exact="a", passes=3)
        ddt = ddt + da * A
        gsc_ref[1:2, :] += jnp.sum(da * dt, axis=0, keepdims=True) * A
        valid = lax.broadcasted_iota(jnp.int32, (CH, 128), 1) < NH
        ddtr = jnp.where(valid, ddt * _sigmoid(dtr + dtb), 0.0)
        gsc_ref[0:1, :] += jnp.sum(ddtr, axis=0, keepdims=True)
        ddtr_ref[...] = ddtr.astype(ddtr_ref.dtype)
        dxbc = jnp.concatenate([dxs] + dB_parts + dC_parts, axis=1)
        dxc = dxbc * _dsilu(xbc_c)
        cext[pl.ds(0, CH), :] = dxc
        xp = xp_ref[...]
        acc = jnp.zeros((CH, XBC), F32)
        for k in range(KS):
            sh = cext[pl.ds(KS - 1 - k, CH), :]
            acc = acc + sh * cw_ref[k:k + 1, :]
            gcw_ref[k:k + 1, :] += jnp.sum(xp * sh, axis=0, keepdims=True)
        gcb_ref[...] += jnp.sum(dxc, axis=0, keepdims=True)
        dx_ref[...] = acc.astype(dx_ref.dtype)
        cext[pl.ds(CH, SSD_HALO), :] = cext[pl.ds(0, SSD_HALO), :]
        rd.finish()

    vec = pl.BlockSpec((8, 128), lambda c: (0, 0))
    vecd = pl.BlockSpec((1, D), lambda c: (0, 0))
    cwsp = pl.BlockSpec((KS, XBC), lambda c: (0, 0))
    cbsp = pl.BlockSpec((1, XBC), lambda c: (0, 0))
    row = lambda w, j=0: pl.BlockSpec((CH, w), lambda c: (rev(c), j))
    outs = pl.pallas_call(
        body, name=name, grid=(nc,),
        in_specs=[row(D), row(D), row(D, COL_Z // D), row(XBC), row(XBC, COL_XBC // XBC), row(128),
                  pl.BlockSpec((1, NH * HP, NS), lambda c: (rev(c), 0, 0)), cwsp, vec, vecd] + [HBM_SPEC] * nco,
        out_specs=[row(D), row(XBC), row(128), cwsp, cbsp, vec, vecd] + [HBM_SPEC] * nco,
        out_shape=[SDS((S, D), _MXU), SDS((S, XBC), _MXU), SDS((S, 128), _MXU), SDS((KS, XBC), F32),
                   SDS((1, XBC), F32), SDS((8, 128), F32), SDS((1, D), F32)] + rd.out_shapes(),
        scratch_shapes=[pltpu.VMEM((NH * HP, NS), F32), pltpu.VMEM((CH, NH * HP), F32),
                        pltpu.VMEM((CH + SSD_HALO, XBC), F32)] + rd.scratch(),
        compiler_params=_cp("arbitrary"))(dmix, y, proj, xbc_c, proj, dtr, hprev, cw, sc, norm_g, *rd.arrays())
    return outs[:7], rd.split(outs[7:])


CONV_RT = 32


def _fill_phases(ext, ph, rows):
    for s in range(1, 8):
        ph[s - 1, pl.ds(0, rows), :] = ext[pl.ds(s, rows), :]


def _window(ext, ph, off, r0, ls):
    s = off % 8
    src = ext if s == 0 else ph.at[s - 1]
    return src[pl.ds(pl.multiple_of(off - s + r0, 8), CONV_RT), ls]


def _cf_fwd(proj, w, b, lg, lb, riders=(), name="cf_fwd", tb_cap=256):
    S = proj.shape[0]
    tb = _tile(S, tb_cap, 8)
    nb = S // tb
    rd = _Riders(riders)
    nco = len(rd.arrays())

    def body(*refs):
        a_ref, g_ref, w_ref, b_ref, lg_ref, lb_ref = refs[:6]
        u1_ref, u_ref = refs[6 + nco:8 + nco]
        ext, ph = refs[8 + 2 * nco:10 + 2 * nco]
        rd.bind(refs[6:6 + nco], refs[8 + nco:8 + 2 * nco], refs[10 + 2 * nco:], nb).start()

        @pl.when(pl.program_id(0) == 0)
        def _():
            ext[pl.ds(0, CF_HALO), :] = jnp.zeros((CF_HALO, D), F32)

        ext[pl.ds(CF_HALO, tb), :] = a_ref[...] * _sigmoid(g_ref[...])
        _fill_phases(ext, ph, tb + CF_HALO - 8)

        def tile(i, carry):
            r0 = pl.multiple_of(i * CONV_RT, CONV_RT)
            for l in range(D // 128):
                ls = pl.ds(l * 128, 128)
                acc = jnp.broadcast_to(b_ref[:, ls], (CONV_RT, 128))
                for k in range(KC):
                    acc = acc + _window(ext, ph, CF_HALO - (KC - 1) + k, r0, ls) * w_ref[k:k + 1, ls]
                u1_ref[pl.ds(r0, CONV_RT), ls] = acc
            return carry

        lax.fori_loop(0, tb // CONV_RT, tile, 0)
        acc = u1_ref[...]
        mu = jnp.mean(acc, axis=-1, keepdims=True)
        xc = acc - mu
        r = lax.rsqrt(jnp.mean(xc * xc, axis=-1, keepdims=True) + EPS)
        u_ref[...] = _silu(xc * r * lg_ref[...] + lb_ref[...]).astype(u_ref.dtype)
        ext[pl.ds(0, CF_HALO), :] = ext[pl.ds(tb, CF_HALO), :]
        rd.finish()

    vec = pl.BlockSpec((1, D), lambda i: (0, 0))
    outs = pl.pallas_call(
        body, name=name, grid=(nb,),
        in_specs=[pl.BlockSpec((tb, D), lambda i: (i, COL_A // D)), pl.BlockSpec((tb, D), lambda i: (i, COL_G // D)),
                  pl.BlockSpec((KC, D), lambda i: (0, 0)), vec, vec, vec] + [HBM_SPEC] * nco,
        out_specs=[pl.BlockSpec((tb, D), lambda i: (i, 0)), pl.BlockSpec((tb, D), lambda i: (i, 0))] + [HBM_SPEC] * nco,
        out_shape=[SDS((S, D), F32), SDS((S, D), _MXU)] + rd.out_shapes(),
        scratch_shapes=[pltpu.VMEM((CF_HALO + tb, D), F32), pltpu.VMEM((7, tb + CF_HALO - 8, D), F32)] + rd.scratch(),
        compiler_params=_cp("arbitrary"))(proj, proj, w, b, lg, lb, *rd.arrays())
    return outs[:2], rd.split(outs[2:])


def _cf_bwd(dmix, u1, proj, w, lg, lb, riders=(), name="cf_bwd", tb_cap=256):
    S = proj.shape[0]
    tb = _tile(S, tb_cap, 8)
    nb = S // tb
    rd = _Riders(riders)
    nco = len(rd.arrays())
    rev = lambda i: nb - 1 - i

    def body(*refs):
        du_ref, u1_ref, a_ref, g_ref, w_ref, lg_ref, lb_ref = refs[:7]
        da_ref, dg_ref, dw_ref, db_ref, dlg_ref, dlb_ref = refs[7 + nco:13 + nco]
        ext, ph, u0s = refs[13 + 2 * nco:16 + 2 * nco]
        rd.bind(refs[7:7 + nco], refs[13 + nco:13 + 2 * nco], refs[16 + 2 * nco:], nb).start()

        @pl.when(pl.program_id(0) == 0)
        def _():
            ext[pl.ds(tb, CF_HALO), :] = jnp.zeros((CF_HALO, D), F32)
            dw_ref[...] = jnp.zeros_like(dw_ref)
            db_ref[...] = jnp.zeros_like(db_ref)
            dlg_ref[...] = jnp.zeros_like(dlg_ref)
            dlb_ref[...] = jnp.zeros_like(dlb_ref)

        u1 = u1_ref[...]
        mu = jnp.mean(u1, axis=-1, keepdims=True)
        xc = u1 - mu
        r = lax.rsqrt(jnp.mean(xc * xc, axis=-1, keepdims=True) + EPS)
        xh = xc * r
        lgv = lg_ref[...]
        du2 = du_ref[...] * _dsilu(xh * lgv + lb_ref[...])
        dlg_ref[...] += jnp.sum(du2 * xh, axis=0, keepdims=True)
        dlb_ref[...] += jnp.sum(du2, axis=0, keepdims=True)
        gd = du2 * lgv
        du1 = r * (gd - jnp.mean(gd, axis=-1, keepdims=True) - xh * jnp.mean(gd * xh, axis=-1, keepdims=True))
        db_ref[...] += jnp.sum(du1, axis=0, keepdims=True)
        ext[pl.ds(0, tb), :] = du1
        u0s[...] = a_ref[...] * _sigmoid(g_ref[...])
        _fill_phases(ext, ph, tb + CF_HALO - 8)

        def dx_tile(i, carry):
            r0 = pl.multiple_of(i * CONV_RT, CONV_RT)
            rows = pl.ds(r0, CONV_RT)
            for l in range(D // 128):
                ls = pl.ds(l * 128, 128)
                acc = jnp.zeros((CONV_RT, 128), F32)
                for k in range(KC):
                    acc = acc + _window(ext, ph, KC - 1 - k, r0, ls) * w_ref[k:k + 1, ls]
                sg = _sigmoid(g_ref[rows, ls])
                da_ref[rows, ls] = (acc * sg).astype(da_ref.dtype)
                dg_ref[rows, ls] = (acc * a_ref[rows, ls] * sg * (1.0 - sg)).astype(dg_ref.dtype)
            return carry

        lax.fori_loop(0, tb // CONV_RT, dx_tile, 0)
        for l in range(D // 128):
            ls = pl.ds(l * 128, 128)

            def dw_tile(i, accs, ls=ls):
                r0 = pl.multiple_of(i * CONV_RT, CONV_RT)
                u0t = u0s[pl.ds(r0, CONV_RT), ls]
                out = []
                for k in range(KC):
                    p = u0t * _window(ext, ph, KC - 1 - k, r0, ls)
                    out.append(accs[k] + ((p[0:8] + p[8:16]) + (p[16:24] + p[24:32])))
                return tuple(out)

            accs = lax.fori_loop(0, tb // CONV_RT, dw_tile, tuple(jnp.zeros((8, 128), F32) for _ in range(KC)))
            for k in range(KC):
                dw_ref[k:k + 1, ls] += jnp.sum(accs[k], axis=0, keepdims=True)
        ext[pl.ds(tb, CF_HALO), :] = ext[pl.ds(0, CF_HALO), :]
        rd.finish()

    vec = pl.BlockSpec((1, D), lambda i: (0, 0))
    wsp = pl.BlockSpec((KC, D), lambda i: (0, 0))
    row = lambda j=0: pl.BlockSpec((tb, D), lambda i: (rev(i), j))
    outs = pl.pallas_call(
        body, name=name, grid=(nb,),
        in_specs=[row(1), row(), row(COL_A // D), row(COL_G // D), wsp, vec, vec] + [HBM_SPEC] * nco,
        out_specs=[row(), row(), wsp, vec, vec, vec] + [HBM_SPEC] * nco,
        out_shape=[SDS((S, D), _MXU), SDS((S, D), _MXU), SDS((KC, D), F32),
                   SDS((1, D), F32), SDS((1, D), F32), SDS((1, D), F32)] + rd.out_shapes(),
        scratch_shapes=[pltpu.VMEM((tb + CF_HALO, D), F32), pltpu.VMEM((7, tb + CF_HALO - 8, D), F32),
                        pltpu.VMEM((tb, D), F32)] + rd.scratch(),
        compiler_params=_cp("arbitrary"))(dmix, u1, proj, proj, w, lg, lb, *rd.arrays())
    return outs[:6], rd.split(outs[6:])


def _attn_fwd(q, kv, name="attn_fwd", tq_cap=512):
    S = q.shape[0]
    tq = _tile(S, tq_cap, 8)
    scale = XD ** -0.5

    def body(q_ref, kv_ref, o_ref):
        for h in range(XH):
            hs = slice(h * XD, (h + 1) * XD)
            s = _dot(q_ref[:, hs], kv_ref[:, hs], "nt") * scale
            s = s - jnp.max(s, axis=-1, keepdims=True)
            p = jnp.exp(s)
            p = p / jnp.sum(p, axis=-1, keepdims=True)
            o_ref[:, hs] = _dot(p, kv_ref[:, D + h * XD: D + (h + 1) * XD]).astype(o_ref.dtype)

    return pl.pallas_call(
        body, name=name, grid=(S // tq,),
        in_specs=[pl.BlockSpec((tq, D), lambda i: (i, 0)), pl.BlockSpec((MEM, 2 * D), lambda i: (0, 0))],
        out_specs=pl.BlockSpec((tq, D), lambda i: (i, 0)), out_shape=SDS((S, D), _MXU),
        compiler_params=_cp("parallel"))(q, kv)


def _attn_bwd(do, q, kv, riders=(), name="attn_bwd", tq_cap=512):
    S = q.shape[0]
    tq = _tile(S, tq_cap, 8)
    scale = XD ** -0.5
    rd = _Riders(riders)
    nco = len(rd.arrays())

    def body(*refs):
        do_ref, q_ref, kv_ref = refs[:3]
        dq_ref, dkv_ref = refs[3 + nco:5 + nco]
        rd.bind(refs[3:3 + nco], refs[5 + nco:5 + 2 * nco], refs[5 + 2 * nco:], S // tq).start()

        @pl.when(pl.program_id(0) == 0)
        def _():
            dkv_ref[...] = jnp.zeros_like(dkv_ref)

        for h in range(XH):
            hs = slice(h * XD, (h + 1) * XD)
            vs = slice(D + h * XD, D + (h + 1) * XD)
            qh = q_ref[:, hs]
            kh = kv_ref[:, hs]
            s = _dot(qh, kh, "nt") * scale
            s = s - jnp.max(s, axis=-1, keepdims=True)
            p = jnp.exp(s)
            p = p / jnp.sum(p, axis=-1, keepdims=True)
            doh = do_ref[:, hs]
            dp = _dot(doh, kv_ref[:, vs], "nt")
            ds = p * (dp - jnp.sum(dp * p, axis=-1, keepdims=True)) * scale
            dq_ref[:, hs] = _dot(ds, kh).astype(dq_ref.dtype)
            dkv_ref[:, hs] += _dot(ds, qh, "tn")
            dkv_ref[:, vs] += _dot(p, doh, "tn")
        rd.finish()

    outs = pl.pallas_call(
        body, name=name, grid=(S // tq,),
        in_specs=[pl.BlockSpec((tq, D), lambda i: (i, 0)), pl.BlockSpec((tq, D), lambda i: (i, 0)),
                  pl.BlockSpec((MEM, 2 * D), lambda i: (0, 0))] + [HBM_SPEC] * nco,
        out_specs=[pl.BlockSpec((tq, D), lambda i: (i, 0)), pl.BlockSpec((MEM, 2 * D), lambda i: (0, 0))]
        + [HBM_SPEC] * nco,
        out_shape=[SDS((S, D), _MXU), SDS((MEM, 2 * D), F32)] + rd.out_shapes(), scratch_shapes=rd.scratch(),
        compiler_params=_cp("arbitrary"))(do, q, kv, *rd.arrays())
    return outs[:2], rd.split(outs[2:])


def _ffn_in(hf, wg_t, wu_t, name="ffn_in", tm_cap=512, tn_cap=1408):
    S, K = hf.shape
    N = wg_t.shape[0]
    tm, tn = _tile(S, tm_cap, 8), _tile(N, tn_cap)

    def body(a_ref, g_ref, u_ref, act_ref, gt_ref, up_ref):
        a = a_ref[...]
        gt = _dot(a, g_ref[...], "nt")
        up = _dot(a, u_ref[...], "nt")
        act_ref[...] = (_silu(gt) * up).astype(act_ref.dtype)
        gt_ref[...] = gt.astype(gt_ref.dtype)
        up_ref[...] = up.astype(up_ref.dtype)

    wsp = pl.BlockSpec((tn, K), lambda j, i: (j, 0))
    osp = pl.BlockSpec((tm, tn), lambda j, i: (i, j))
    return pl.pallas_call(
        body, name=name, grid=(N // tn, S // tm), in_specs=[pl.BlockSpec((tm, K), lambda j, i: (i, 0)), wsp, wsp],
        out_specs=[osp, osp, osp], out_shape=[SDS((S, N), _MXU)] * 3,
        compiler_params=_cp("parallel", "parallel"))(hf, wg_t, wu_t)


def _ffn_out_bwd(dx, w_down, gt, up, name="ffn_out_dx", tm_cap=512, tk_cap=1408):
    S, N = dx.shape
    K = w_down.shape[0]
    tm, tk = _tile(S, tm_cap, 8), _tile(K, tk_cap)

    def body(a_ref, b_ref, g_ref, u_ref, dg_ref, du_ref):
        d = _dot(a_ref[...], b_ref[...], "nt")
        gt = g_ref[...].astype(F32)
        s = _sigmoid(gt)
        dg_ref[...] = (d * u_ref[...].astype(F32) * (s * (1.0 + gt * (1.0 - s)))).astype(dg_ref.dtype)
        du_ref[...] = (d * gt * s).astype(du_ref.dtype)

    osp = pl.BlockSpec((tm, tk), lambda j, i: (i, j))
    return pl.pallas_call(
        body, name=name, grid=(K // tk, S // tm),
        in_specs=[pl.BlockSpec((tm, N), lambda j, i: (i, 0)), pl.BlockSpec((tk, N), lambda j, i: (j, 0)), osp, osp],
        out_specs=[osp, osp], out_shape=[SDS((S, K), _MXU)] * 2,
        compiler_params=_cp("parallel", "parallel"))(dx, w_down, gt, up)


AG_RIDE = (("w_down",), ("w_out", "w_q", "w_kv", "w_o"), ("w_gate", "w_up"))


def _local_step(x, mem, tgt, W, P, core=None, late=None):
    pair, got = {}, {}
    ride = [[late[n] for n in grp] if late is not None else [] for grp in AG_RIDE]

    def halves(group):
        if core is None:
            return []
        gs = [_shard_grad(n, GW) for n in group]
        return [g.reshape(4, 2, g.shape[1] // 2, g.shape[2]) for g in gs]

    def pair_sums(group, hs, theirs):
        ps = [_pair_sum(h_, t, core, "rs_pair_sum_" + n) for h_, t, n in zip(hs, theirs, group)]
        pair.update(zip(group, ps))
        return ps

    h = _rms_fwd(x, P["g_mix"], "rms_mix")
    proj, (bufs0,) = _mm_nn(h, W["main"], "in_proj", tn_cap=1152, riders=[_Rider("gather", ride[0])])
    dtr = _mm_nn(h, W["dt"], "in_proj_dt")
    (xbc_c, y, yn, hprev), (bufs1,) = _ssd_fwd(proj, P["conv4_w"], P["conv4_b"], dtr, P["sc"], P["ssd_norm_g"],
                                                riders=[_Rider("gather", ride[1])])
    (u1, u), (bufs2,) = _cf_fwd(proj, P["cf_w"], P["cf_b"], P["ln_g"], P["ln_b"], riders=[_Rider("gather", ride[2])])
    if late is not None:
        names = AG_RIDE[0] + AG_RIDE[1] + AG_RIDE[2]
        full = _gather_finish_list(ride[0] + ride[1] + ride[2], bufs0 + bufs1 + bufs2)
        W = dict(W, **_pack_late(dict(zip(names, full))))
    mix = jnp.concatenate([yn, u], axis=1)
    x1 = _mm_nn(mix, W["out"], "out_proj", add=x)
    hq = _rms_fwd(x1, P["g_xattn"], "rms_xattn")
    q = _mm_nn(hq, W["q"], "q_proj")
    mn = _rms_fwd(mem, P["g_mem"], "rms_mem")
    kv = _mm_nn(mn, W["kv"], "kv_proj")
    o = _attn_fwd(q, kv)
    x2 = _mm_nn(o, W["o"], "o_proj", add=x1)
    hf = _rms_fwd(x2, P["g_ffn"], "rms_ffn")
    act, gt, up = _ffn_in(hf, W["gate_t"], W["up_t"])
    x3 = _mm_nn(act, W["down"], "ffn_out", add=x2)
    loss, dx3, dx3b, g_final = _final_loss(x3, P["g_final"], tgt)
    GW, GP = {}, {"g_final": g_final}
    GW["down"] = _mm_tn(act, dx3b, "ffn_out_dw", tk_cap=1408, tn_cap=1024)
    dgt, dup = _ffn_out_bwd(dx3b, W["down"], gt, up)
    dhf = _mm_nn(dgt, W["gate_t"], "ffn_gate_dx", tm_cap=512)
    dhf = _mm_nn(dup, W["up_t"], "ffn_up_dx", add=dhf, tm_cap=512)
    GW["gate_t"] = _mm_tn(dgt, hf, "ffn_gate_dw", tk_cap=1408, tn_cap=1024)
    GW["up_t"] = _mm_tn(dup, hf, "ffn_up_dw", tk_cap=1408, tn_cap=1024)
    ffn_halves = halves(RS_GROUPS[0])
    dx2, dx2b, GP["g_ffn"] = _rms_bwd(x2, P["g_ffn"], dhf, dx3, "rms_ffn_bwd")
    do = _mm_nt(dx2b, W["o"], "o_proj_dx")
    GW["o"] = _mm_tn(o, dx2b, "o_proj_dw")
    (dq, dkv), (ffn_theirs,) = _attn_bwd(do, q, kv, riders=[_Rider("pair", ffn_halves)])
    ffn_pieces = pair_sums(RS_GROUPS[0], ffn_halves, ffn_theirs)
    dhq = _mm_nt(dq, W["q"], "q_proj_dx")
    GW["q"] = _mm_tn(hq, dq, "q_proj_dw")
    dkvb = dkv.astype(_MXU)
    GW["kv"] = _mm_tn(mn, dkvb, "kv_proj_dw", tm_cap=256)
    dmn = _mm_nt(dkvb, W["kv"], "kv_proj_dx")
    GP["g_mem"] = _rms_bwd(mem, P["g_mem"], dmn, None, "rms_mem_bwd")
    dx1, dx1b, GP["g_xattn"] = _rms_bwd(x1, P["g_xattn"], dhq, dx2, "rms_xattn_bwd")
    dmix = _mm_nt(dx1b, W["out"], "out_proj_dx")
    GW["out"] = _mm_tn(mix, dx1b, "out_proj_dw", tn_cap=1024)
    attn_halves = halves(RS_GROUPS[1])
    (da, dg, GP["cf_w"], GP["cf_b"], GP["ln_g"], GP["ln_b"]), (came, attn_theirs) = _cf_bwd(
        dmix, u1, proj, P["cf_w"], P["ln_g"], P["ln_b"],
        riders=[_Rider("exchange", ffn_pieces), _Rider("pair", attn_halves)])
    got.update(zip(RS_GROUPS[0], came))
    attn_pieces = pair_sums(RS_GROUPS[1], attn_halves, attn_theirs)
    (dz, dxbc, ddtr, GP["conv4_w"], GP["conv4_b"], GP["sc"], GP["ssd_norm_g"]), (came,) = _ssd_bwd(
        dmix, y, proj, xbc_c, dtr, hprev, P["conv4_w"], P["sc"], P["ssd_norm_g"],
        riders=[_Rider("exchange", attn_pieces)])
    got.update(zip(RS_GROUPS[1], came))
    dproj = jnp.concatenate([dz, da, dg, dxbc], axis=1)
    GW["main"] = _mm_tn(h, dproj, "in_proj_dw", tn_cap=1152)
    GW["dt"] = _mm_tn(h, ddtr, "in_proj_dt_dw")
    in_halves = halves(RS_GROUPS[2])
    in_pieces = pair_sums(RS_GROUPS[2], in_halves, _pair_split_list(in_halves, "rs_pair_send_w_in")) if in_halves else []
    dh = _mm_nt(ddtr, W["dt"], "in_proj_dt_dx")
    dh, (came,) = _mm_nt(dproj, W["main"], "in_proj_dx", add=dh, tk_cap=512, riders=[_Rider("exchange", in_pieces)])
    got.update(zip(RS_GROUPS[2], came))
    grad_x, GP["g_mix"] = _rms_bwd(x, P["g_mix"], dh, dx1, "rms_mix_bwd", low=False)
    if core is None:
        return loss, grad_x, GW, GP
    return loss, grad_x, GW, GP, pair, got


Z_END, XBC_END, DT_END = NH * HP, NH * HP + XBC, NH * HP + XBC + NH


def _pad_to(a, rows=None, cols=None):
    r = 0 if rows is None else rows - a.shape[0]
    c = 0 if cols is None else cols - a.shape[1]
    return jnp.pad(a, ((0, r), (0, c)))


IN_W = DT_END + 2 * D
W_IN_SEGS = [(0, Z_END, "main", COL_Z), (Z_END, XBC_END, "main", COL_XBC), (XBC_END, DT_END, "dt", 0),
             (DT_END, DT_END + D, "main", COL_A), (DT_END + D, IN_W, "main", COL_G)]
BIG = [("w_in", True), ("w_out", False), ("w_q", False), ("w_kv", True), ("w_o", False), ("w_gate", False),
       ("w_up", False), ("w_down", False)]
TRANSPOSED = ("w_gate", "w_up")


def _ref_cols(pieces, a, b):
    cw = IN_W // 4
    out = []
    for j in range(4):
        lo, hi = max(a, j * cw), min(b, (j + 1) * cw)
        if lo < hi:
            out.append(pieces[j][:, lo - j * cw:hi - j * cw])
    return out


def _cat_cols(pieces):
    return jnp.concatenate([pieces[j] for j in range(4)], axis=1)


def _pack_in(w_in):
    main = jnp.concatenate(_ref_cols(w_in, 0, Z_END) + _ref_cols(w_in, DT_END, IN_W) + _ref_cols(w_in, Z_END, XBC_END), axis=1)
    return {"main": main, "dt": _pad_to(jnp.concatenate(_ref_cols(w_in, XBC_END, DT_END), axis=1), cols=128)}


def _pack_late(pc):
    rows = lambda n: pc[n].reshape(-1, pc[n].shape[-1])
    return {"out": rows("w_out"), "q": rows("w_q"), "kv": _cat_cols(pc["w_kv"]), "o": rows("w_o"),
            "gate_t": rows("w_gate"), "up_t": rows("w_up"), "down": rows("w_down")}


GW_KEY = {"w_gate": "gate_t", "w_up": "up_t", "w_kv": "kv", "w_out": "out", "w_q": "q", "w_o": "o", "w_down": "down"}
RS_GROUPS = (("w_down", "w_gate", "w_up"), ("w_out", "w_q", "w_kv", "w_o"), ("w_in",))


def _shard_grad(name, GW):
    if name == "w_in":
        cw = IN_W // 4
        pieces = []
        for j in range(4):
            parts = []
            for a, b, src, col in W_IN_SEGS:
                lo, hi = max(a, j * cw), min(b, (j + 1) * cw)
                if lo < hi:
                    parts.append(GW[src][:, col + lo - a:col + hi - a])
            pieces.append(jnp.concatenate(parts, axis=1))
        return jnp.stack(pieces)
    g = GW[GW_KEY[name]]
    if dict(BIG)[name]:
        cw = g.shape[1] // 4
        return jnp.stack([g[:, j * cw:(j + 1) * cw] for j in range(4)])
    return g.reshape(4, g.shape[0] // 4, g.shape[1])


def _stack_sc(dt_bias, a_log, d):
    return _pad_to(jnp.concatenate([dt_bias, a_log, d], axis=0), rows=8, cols=128)


COMM_PARAMS = pltpu.CompilerParams(vmem_limit_bytes=VMEM_LIMIT)


def _dma_sems(*counts):
    return [pltpu.SemaphoreType.DMA((n,)) for n in counts]


def _allgather_list(arrs, name):
    n = len(arrs)
    halved = [a.shape[0] % 16 == 0 for a in arrs]
    oshape = [(4, 2, a.shape[0] // 2, a.shape[1]) if h else (4, 1) + a.shape for a, h in zip(arrs, halved)]

    def body(*refs):
        srcs, outs = refs[:n], refs[n:2 * n]
        ici_send, ici_recv, own_send, own_recv, fwd_send, fwd_recv = refs[2 * n:]
        x, y, c = lax.axis_index("x"), lax.axis_index("y"), lax.axis_index("c")
        me = 2 * x + y
        sib = (x, y, 1 - c)
        peers = _chip_peers(x, y)

        def half(i, h):
            r = arrs[i].shape[0] // 2
            if not halved[i]:
                return srcs[i]
            return srcs[i].at[pl.ds(h * r if isinstance(h, int) else pl.multiple_of(h * r, 8), r)]

        ici, own, fwd = [], [], []
        for i in range(n):
            mine_h = c if halved[i] else 0
            for k, (px, py) in enumerate(peers):
                s = 3 * i + k
                ici.append(_remote(half(i, c), outs[i].at[me, mine_h], ici_send.at[s], ici_recv.at[s], (px, py, c)))
            for h in range(2 if halved[i] else 1):
                s = 2 * i + h
                own.append(_remote(half(i, h), outs[i].at[me, h], own_send.at[s], own_recv.at[s], sib))
        for cp in ici + own:
            cp.start()
        for i in range(n):
            if not halved[i]:
                continue
            for k, (px, py) in enumerate(peers):
                s = 3 * i + k
                got = outs[i].at[2 * px + py, c]
                _remote(half(i, c), got, ici_send.at[s], ici_recv.at[s], (px, py, c)).wait_recv()
                f = _remote(got, got, fwd_send.at[s], fwd_recv.at[s], sib)
                f.start()
                fwd.append(f)
        for i in range(n):
            for k, (px, py) in enumerate(peers):
                s = 3 * i + k
                if halved[i]:
                    _remote(half(i, c), outs[i].at[2 * px + py, 1 - c], fwd_send.at[s], fwd_recv.at[s], sib).wait_recv()
                else:
                    _remote(srcs[i], outs[i].at[2 * px + py, 0], ici_send.at[s], ici_recv.at[s], (px, py, c)).wait_recv()
            for h in range(2 if halved[i] else 1):
                s = 2 * i + h
                _remote(half(i, h), outs[i].at[me, h], own_send.at[s], own_recv.at[s], sib).wait_recv()
        for cp in ici + own + fwd:
            cp.wait_send()

    outs = pl.pallas_call(
        body, name=name, in_specs=[HBM_SPEC] * n, out_specs=[HBM_SPEC] * n,
        out_shape=[SDS(s, a.dtype) for s, a in zip(oshape, arrs)],
        scratch_shapes=_dma_sems(3 * n, 3 * n, 2 * n, 2 * n, 3 * n, 3 * n), compiler_params=COMM_PARAMS)(*arrs)
    return [o.reshape((4,) + a.shape) for o, a in zip(outs, arrs)]


def _pair_split_list(gs, name):
    n = len(gs)

    def body(*refs):
        sends, recvs = _pair_copies(refs[:n], refs[n:2 * n], *refs[2 * n:])
        for cp in sends:
            cp.start()
        for cp in recvs:
            cp.wait_recv()
        for cp in sends:
            cp.wait_send()

    return pl.pallas_call(
        body, name=name, in_specs=[HBM_SPEC] * n, out_specs=[HBM_SPEC] * n,
        out_shape=[SDS((4,) + g.shape[2:], g.dtype) for g in gs],
        scratch_shapes=_dma_sems(4 * n, 4 * n), compiler_params=COMM_PARAMS)(*gs)


def _gather_finish_list(shards, bufs, name="allgather_finish"):
    n = len(shards)

    def body(*refs):
        srcs, outs = refs[:n], refs[2 * n:3 * n]
        own_send, own_recv, fwd_send, fwd_recv = refs[3 * n:]
        x, y, c = lax.axis_index("x"), lax.axis_index("y"), lax.axis_index("c")
        me = 2 * x + y
        sib = (x, y, 1 - c)
        sends, recvs = [], []
        for i in range(n):
            for h in range(2):
                own = _remote(_rows_half(srcs[i], shards[i].shape[0], h), outs[i].at[me, h],
                              own_send.at[2 * i + h], own_recv.at[2 * i + h], sib)
                sends.append(own)
                recvs.append(own)
            for k, (px, py) in enumerate(_chip_peers(x, y)):
                got, s = outs[i].at[2 * px + py, c], 3 * i + k
                sends.append(_remote(got, got, fwd_send.at[s], fwd_recv.at[s], sib))
                recvs.append(_remote(got, outs[i].at[2 * px + py, 1 - c], fwd_send.at[s], fwd_recv.at[s], sib))
        for cp in sends:
            cp.start()
        for cp in recvs:
            cp.wait_recv()
        for cp in sends:
            cp.wait_send()

    outs = pl.pallas_call(
        body, name=name, in_specs=[HBM_SPEC] * (2 * n), out_specs=[HBM_SPEC] * n,
        out_shape=[SDS(b.shape, b.dtype) for b in bufs], input_output_aliases={n + i: i for i in range(n)},
        scratch_shapes=_dma_sems(2 * n, 2 * n, 3 * n, 3 * n), compiler_params=COMM_PARAMS)(*shards, *bufs)
    return [o.reshape((4,) + a.shape) for o, a in zip(outs, shards)]


JOIN_SPLIT = 4


def _pair_join_list(bufs, name="rs_pair_join"):
    n = len(bufs)

    def body(*refs):
        outs = refs[n:2 * n]
        send_sems, recv_sems = refs[2 * n:]
        x, y, c = lax.axis_index("x"), lax.axis_index("y"), lax.axis_index("c")
        sib = (x, y, 1 - c)
        sends, recvs = [], []
        for i in range(n):
            rc = bufs[i].shape[1] // JOIN_SPLIT
            for q in range(JOIN_SPLIT):
                k = JOIN_SPLIT * i + q
                rows = pl.ds(q * rc, rc)
                sends.append(_remote(outs[i].at[c, rows], outs[i].at[c, rows], send_sems.at[k], recv_sems.at[k], sib))
                recvs.append(_remote(outs[i].at[c, rows], outs[i].at[1 - c, rows], send_sems.at[k], recv_sems.at[k], sib))
        for cp in sends:
            cp.start()
        for cp in recvs:
            cp.wait_recv()
        for cp in sends:
            cp.wait_send()

    return pl.pallas_call(
        body, name=name, in_specs=[HBM_SPEC] * n, out_specs=[HBM_SPEC] * n,
        out_shape=[SDS(b.shape, b.dtype) for b in bufs], input_output_aliases={i: i for i in range(n)},
        scratch_shapes=_dma_sems(JOIN_SPLIT * n, JOIN_SPLIT * n), compiler_params=COMM_PARAMS)(*bufs)


def _pair_sum(g, theirs, core, name):
    _, _, r, c = g.shape

    def body(core_ref, g_ref, t_ref, o_ref):
        o_ref[...] = (g_ref[...] + t_ref[...]).astype(o_ref.dtype)

    spec = pltpu.PrefetchScalarGridSpec(
        num_scalar_prefetch=1, grid=(4,),
        in_specs=[pl.BlockSpec((None, None, r, c), lambda j, core_ref: (j, core_ref[0], 0, 0)),
                  pl.BlockSpec((None, r, c), lambda j, core_ref: (j, 0, 0))],
        out_specs=pl.BlockSpec((None, r, c), lambda j, core_ref: (j, 0, 0)))
    return pl.pallas_call(body, name=name, grid_spec=spec, out_shape=SDS((4, r, c), BF16),
                          compiler_params=_cp("parallel"))(core, g, theirs)


def _chip_sum(own, got, where, name):
    _, r, c = own.shape
    tr = r // 2

    def body(w_ref, a_ref, b1_ref, b2_ref, b3_ref, o_ref):
        o_ref[...] = ((a_ref[...].astype(F32) + b1_ref[...].astype(F32)) + b2_ref[...].astype(F32)) + b3_ref[...].astype(F32)

    piece = lambda k: pl.BlockSpec((None, tr, c), lambda i, w_ref: ((w_ref[0] + k) % 4, i, 0))
    spec = pltpu.PrefetchScalarGridSpec(
        num_scalar_prefetch=1, grid=(r // tr,), in_specs=[piece(0), piece(1), piece(2), piece(3)],
        out_specs=pl.BlockSpec((None, tr, c), lambda i, w_ref: (w_ref[1], i, 0)))
    return pl.pallas_call(body, name=name, grid_spec=spec, out_shape=SDS((2, r, c), F32),
                          compiler_params=_cp("parallel"))(where, own, got, got, got)


def _adam_math(w, g, m, v):
    bc1 = 1.0 - ADAM_B1 ** ADAM_STEP
    bc2 = 1.0 - ADAM_B2 ** ADAM_STEP
    mn = ADAM_B1 * m + (1.0 - ADAM_B1) * g
    vn = ADAM_B2 * v + (1.0 - ADAM_B2) * (g * g)
    return -ADAM_LR * ((mn / bc1) / (jnp.sqrt(vn / bc2) + ADAM_EPS) + ADAM_WD * w), mn, vn


PACK_COLS = XBC
PACK = {"g_mix": (0, 1, D), "g_xattn": (1, 1, D), "g_mem": (2, 1, D), "g_ffn": (3, 1, D), "g_final": (4, 1, D),
        "ssd_norm_g": (5, 1, D), "cf_b": (6, 1, D), "ln_g": (7, 1, D), "ln_b": (8, 1, D), "conv4_b": (9, 1, XBC),
        "conv4_w": (10, KS, XBC), "sc": (16, 8, 128), "cf_w": (24, KC, D), "loss": (55, 1, 128)}
PACK_ROWS = 56
SMALL_ADAM = ["g_mix", "g_xattn", "g_mem", "g_ffn", "g_final", "ssd_norm_g", "cf_b", "ln_g", "ln_b", "conv4_b", "sc"]


def _small_allreduce_adamw(grads, wts, mom, var, name="allreduce_small"):
    gk = list(PACK)
    ng, na = len(gk), len(SMALL_ADAM)

    def body(*refs):
        g_in = refs[:ng]
        w_in, m_in, v_in = (refs[ng + i * na: ng + (i + 1) * na] for i in range(3))
        o = refs[ng + 3 * na:]
        g_out = o[:ng]
        d_out, m_out, v_out = (o[ng + i * na: ng + (i + 1) * na] for i in range(3))
        pack, buf, acc, send_sems, recv_sems = o[ng + 3 * na:]
        x, y, c = lax.axis_index("x"), lax.axis_index("y"), lax.axis_index("c")
        me = 4 * x + 2 * y + c
        pack[...] = jnp.zeros_like(pack)
        for i, k in enumerate(gk):
            r0, nr, nc = PACK[k]
            pack[r0:r0 + nr, 0:nc] = g_in[i][...]
        peers = [(x, y, 1 - c)] + [(px, py, pc) for px, py in _chip_peers(x, y) for pc in (c, 1 - c)]
        sends = [_remote(pack, buf.at[me], send_sems.at[k], recv_sems.at[k], dev) for k, dev in enumerate(peers)]
        for cp in sends:
            cp.start()
        buf[me] = pack[...]
        for k, (px, py, pc) in enumerate(peers):
            _remote(pack, buf.at[4 * px + 2 * py + pc], send_sems.at[k], recv_sems.at[k], (px, py, pc)).wait_recv()
        for cp in sends:
            cp.wait_send()
        tot = buf[0]
        for i in range(1, 8):
            tot = tot + buf[i]
        acc[...] = tot
        for i, k in enumerate(gk):
            r0, nr, nc = PACK[k]
            g_out[i][...] = acc[r0:r0 + nr, 0:nc]
        for i, k in enumerate(SMALL_ADAM):
            r0, nr, nc = PACK[k]
            d_out[i][...], m_out[i][...], v_out[i][...] = _adam_math(
                w_in[i][...], acc[r0:r0 + nr, 0:nc], m_in[i][...], v_in[i][...])

    args = [grads[k] for k in gk] + [d[k] for d in (wts, mom, var) for k in SMALL_ADAM]
    shp = lambda k: SDS((PACK[k][1], PACK[k][2]), F32)
    vm = pl.BlockSpec(memory_space=pltpu.VMEM)
    outs = pl.pallas_call(
        body, name=name, in_specs=[vm] * len(args), out_specs=[vm] * (ng + 3 * na),
        out_shape=[shp(k) for k in gk] + [shp(k) for _ in range(3) for k in SMALL_ADAM],
        scratch_shapes=[pltpu.VMEM((PACK_ROWS, PACK_COLS), F32), pltpu.VMEM((8, PACK_ROWS, PACK_COLS), F32),
                        pltpu.VMEM((PACK_ROWS, PACK_COLS), F32)] + _dma_sems(7, 7),
        compiler_params=COMM_PARAMS)(*args)
    red = dict(zip(gk, outs[:ng]))
    parts = [dict(zip(SMALL_ADAM, outs[ng + i * na: ng + (i + 1) * na])) for i in range(3)]
    return red, parts[0], parts[1], parts[2]


def _adamw_cols(w, gfull, m, v, chip, name):
    _, R, C = w.shape

    def body(w_idx, w_ref, g_ref, m_ref, v_ref, go_ref, d_ref, mo_ref, vo_ref):
        go_ref[...] = g_ref[...]
        d_ref[...], mo_ref[...], vo_ref[...] = _adam_math(w_ref[...], g_ref[...], m_ref[...], v_ref[...])

    blk = pl.BlockSpec((None, R, C), lambda i, w_idx: (0, 0, 0))
    spec = pltpu.PrefetchScalarGridSpec(
        num_scalar_prefetch=1, grid=(1,),
        in_specs=[blk, pl.BlockSpec((R, C), lambda i, w_idx: (0, w_idx[0])), blk, blk], out_specs=[blk] * 4)
    return pl.pallas_call(body, name=name, grid_spec=spec, out_shape=[SDS((1, R, C), F32)] * 4,
                          compiler_params=_cp("arbitrary"))(chip, w, gfull, m, v)


def _adamw(w, g, m, v, name):
    _, R, C = w.shape
    half = R // 2
    tr = _tile(half, max(8, (2 ** 17 // C) // 8 * 8), 8)
    nh = half // tr

    def body(w_ref, g_ref, m_ref, v_ref, go_ref, d_ref, mo_ref, vo_ref):
        go_ref[...] = g_ref[...]
        d_ref[...], mo_ref[...], vo_ref[...] = _adam_math(w_ref[...], g_ref[...], m_ref[...], v_ref[...])

    blk = pl.BlockSpec((None, tr, C), lambda i: (0, i, 0))
    gblk = pl.BlockSpec((None, tr, C), lambda i: (i // nh, i % nh, 0))
    return pl.pallas_call(body, name=name, grid=(R // tr,), in_specs=[blk, gblk, blk, blk], out_specs=[blk] * 4,
                          out_shape=[SDS((1, R, C), F32)] * 4, compiler_params=_cp("parallel"))(w, g, m, v)


WEIGHT_NAMES = ["norm_mix_g", "w_in", "ssd_conv_w", "ssd_conv_b", "ssd_dt_bias", "ssd_A_log", "ssd_D", "ssd_norm_g",
                "cf_conv_w", "cf_conv_b", "cf_ln_g", "cf_ln_b", "w_out", "norm_xattn_g", "norm_mem_g", "w_q", "w_kv",
                "w_o", "norm_ffn_g", "w_gate", "w_up", "w_down", "norm_final_g"]
VEC_REF = [("norm_mix_g", "g_mix"), ("norm_xattn_g", "g_xattn"), ("norm_mem_g", "g_mem"), ("norm_ffn_g", "g_ffn"),
           ("norm_final_g", "g_final"), ("ssd_norm_g", "ssd_norm_g"), ("cf_conv_b", "cf_b"), ("cf_ln_g", "ln_g"),
           ("cf_ln_b", "ln_b"), ("ssd_conv_b", "conv4_b")]
SC_REF = ["ssd_dt_bias", "ssd_A_log", "ssd_D"]


def _small_side(get):
    d = {k: get(ref_name).reshape(1, -1) for ref_name, k in VEC_REF}
    d["sc"] = _stack_sc(*[get(n) for n in SC_REF])
    return d


def kernel(x, mem, norm_mix_g, w_in, ssd_conv_w, ssd_conv_b, ssd_dt_bias, ssd_A_log, ssd_D, ssd_norm_g, cf_conv_w, cf_conv_b, cf_ln_g, cf_ln_b, w_out, norm_xattn_g, norm_mem_g, w_q, w_kv, w_o, norm_ffn_g, w_gate, w_up, w_down, norm_final_g, loss_target, m_norm_mix_g, m_w_in, m_ssd_conv_w, m_ssd_conv_b, m_ssd_dt_bias, m_ssd_A_log, m_ssd_D, m_ssd_norm_g, m_cf_conv_w, m_cf_conv_b, m_cf_ln_g, m_cf_ln_b, m_w_out, m_norm_xattn_g, m_norm_mem_g, m_w_q, m_w_kv, m_w_o, m_norm_ffn_g, m_w_gate, m_w_up, m_w_down, m_norm_final_g, v_norm_mix_g, v_w_in, v_ssd_conv_w, v_ssd_conv_b, v_ssd_dt_bias, v_ssd_A_log, v_ssd_D, v_ssd_norm_g, v_cf_conv_w, v_cf_conv_b, v_cf_ln_g, v_cf_ln_b, v_w_out, v_norm_xattn_g, v_norm_mem_g, v_w_q, v_w_kv, v_w_o, v_norm_ffn_g, v_w_gate, v_w_up, v_w_down, v_norm_final_g):
    env = dict(locals())
    view = lambda n, a: a.transpose(0, 2, 1) if n in TRANSPOSED else a
    wts = {n: view(n, env[n]) for n in WEIGHT_NAMES}
    mom = {n: view(n, env["m_" + n]) for n in WEIGHT_NAMES}
    var = {n: view(n, env["v_" + n]) for n in WEIGHT_NAMES}
    chip = (2 * lax.axis_index("x") + lax.axis_index("y")).astype(jnp.int32).reshape(1)
    core = lax.axis_index("c").astype(jnp.int32).reshape(1)
    where = jnp.concatenate([chip, core])
    big = [n for n, _ in BIG]

    w_in_g, conv4_g, cf_g = _allgather_list([w_in[0].astype(BF16), ssd_conv_w[0], cf_conv_w[0]], "allgather_first")
    W = _pack_in(w_in_g)
    P = _small_side(lambda n: wts[n])
    P["conv4_w"], P["cf_w"] = _cat_cols(conv4_g), _cat_cols(cf_g)
    late = {n: wts[n][0].astype(BF16) for grp in AG_RIDE for n in grp}

    loss, grad_x, GW, GP, pair, got = _local_step(x[0], mem[0], loss_target[0], W, P, core, late)
    joined = _pair_join_list([_chip_sum(pair[n], got[n], where, "rs_chip_sum_" + n) for n in big])
    gshard = dict(zip(big, joined))

    small = dict(GP)
    small["loss"] = loss
    red, sd, sm, sv = _small_allreduce_adamw(small, {k: P[k] for k in SMALL_ADAM}, _small_side(lambda n: mom[n]),
                                             _small_side(lambda n: var[n]))
    grads, delta, new_m, new_v = {}, {}, {}, {}
    for ref_name, k in VEC_REF:
        shp = wts[ref_name].shape
        for dst, src in ((grads, red), (delta, sd), (new_m, sm), (new_v, sv)):
            dst[ref_name] = src[k].reshape(shp)
    for row, ref_name in enumerate(SC_REF):
        for dst, src in ((grads, red), (delta, sd), (new_m, sm), (new_v, sv)):
            dst[ref_name] = src["sc"][row:row + 1, :NH]

    for n, k in (("ssd_conv_w", "conv4_w"), ("cf_conv_w", "cf_w")):
        grads[n], delta[n], new_m[n], new_v[n] = _adamw_cols(wts[n], red[k], mom[n], var[n], chip, "adamw_" + n)
    for n in big:
        outs = _adamw(wts[n], gshard[n], mom[n], var[n], "adamw_" + n)
        grads[n], delta[n], new_m[n], new_v[n] = [view(n, o) for o in outs]

    return (red["loss"][0, 0], grad_x[None], *[grads[n] for n in WEIGHT_NAMES], *[delta[n] for n in WEIGHT_NAMES],
            *[new_m[n] for n in WEIGHT_NAMES], *[new_v[n] for n in WEIGHT_NAMES])
```

```python
import functools
import math

import jax
import jax.numpy as jnp
from jax import lax
from jax.experimental import pallas as pl
from jax.experimental.pallas import tpu as pltpu

F32 = jnp.float32
BF16 = jnp.bfloat16
_MXU = BF16

D = 1024
MEM = 256
NH, HP, NG, NS = 16, 64, 2, 128
GW = NH * HP // NG
CH = 128
XBC = NH * HP + 2 * NG * NS
KS, KC = 4, 31
XH, XD = 4, 256
DFF = 2816
EPS = 1e-6
COL_Z, COL_A, COL_G, COL_XBC, COL_DT, MAINW = 0, 1024, 2048, 3072, 4608, 4736
VMEM_LIMIT = 56 * 2 ** 20

ADAM_LR, ADAM_B1, ADAM_B2, ADAM_EPS, ADAM_WD, ADAM_STEP = 0.001, 0.9, 0.999, 1e-08, 0.01, 10

SDS = jax.ShapeDtypeStruct
MESHID = pl.DeviceIdType.MESH


def _cp(*sem):
    return pltpu.CompilerParams(dimension_semantics=sem, vmem_limit_bytes=VMEM_LIMIT)


def _tile(n, cap, unit=128):
    if n <= cap:
        return n
    best = None
    for t in range(unit, cap + 1, unit):
        if n % t == 0:
            best = t
    assert best is not None, (n, cap)
    return best


def _sigmoid(x):
    return 1.0 / (1.0 + jnp.exp(-x))


def _silu(x):
    return x * _sigmoid(x)


def _dsilu(x):
    s = _sigmoid(x)
    return s * (1.0 + x * (1.0 - s))


def _softplus(x):
    return jnp.maximum(x, 0.0) + jnp.log(1.0 + jnp.exp(-jnp.abs(x)))


def _split_bf16(x, passes):
    parts, r = [], x.astype(F32)
    for _ in range(passes):
        p = r.astype(BF16)
        parts.append(p)
        r = r - p.astype(F32)
    return parts


def _dot(a, b, dims=None, exact=None, passes=2):
    dn = {None: (((1,), (0,)), ((), ())), "nt": (((1,), (1,)), ((), ())), "tn": (((0,), (0,)), ((), ()))}[dims]
    if exact is None:
        return lax.dot_general(a.astype(_MXU), b.astype(_MXU), dn, preferred_element_type=F32)
    if exact == "a":
        terms = [(a.astype(BF16), p) for p in _split_bf16(b, passes)]
    else:
        terms = [(p, b.astype(BF16)) for p in _split_bf16(a, passes)]
    out = None
    for lhs, rhs in terms:
        d = lax.dot_general(lhs, rhs, dn, preferred_element_type=F32)
        out = d if out is None else out + d
    return out


def _mm_nn(a, b, name, add=None, out_dtype=F32, tm_cap=1024, tn_cap=1408, riders=None):
    M, K = a.shape
    _, N = b.shape
    tm, tn = _tile(M, tm_cap, 8), _tile(N, tn_cap)
    rd = _Riders(riders or ())
    nin, nco = 2 + (add is not None), len(rd.arrays())
    grid = (N // tn, M // tm)

    def body(*refs):
        a_ref, b_ref, o_ref = refs[0], refs[1], refs[nin + nco]
        rd.bind(refs[nin:nin + nco], refs[nin + nco + 1:nin + 2 * nco + 1], refs[nin + 2 * nco + 1:], grid).start()
        acc = _dot(a_ref[...], b_ref[...])
        if add is not None:
            acc = acc + refs[2][...]
        o_ref[...] = acc.astype(o_ref.dtype)
        rd.finish()

    in_specs = [pl.BlockSpec((tm, K), lambda j, i: (i, 0)), pl.BlockSpec((K, tn), lambda j, i: (0, j))]
    args = [a, b]
    if add is not None:
        in_specs.append(pl.BlockSpec((tm, tn), lambda j, i: (i, j)))
        args.append(add)
    order = ("arbitrary", "arbitrary") if nco else ("parallel", "parallel")
    outs = pl.pallas_call(
        body, name=name, grid=grid, in_specs=in_specs + [HBM_SPEC] * nco,
        out_specs=[pl.BlockSpec((tm, tn), lambda j, i: (i, j))] + [HBM_SPEC] * nco,
        out_shape=[SDS((M, N), out_dtype)] + rd.out_shapes(),
        scratch_shapes=rd.scratch(), compiler_params=_cp(*order))(*args, *rd.arrays())
    return outs[0] if riders is None else (outs[0], rd.split(outs[1:]))


def _mm_nt(a, b, name, add=None, out_dtype=F32, tm_cap=512, tk_cap=1024, riders=None):
    M, N = a.shape
    K = b.shape[0]
    tm, tk = _tile(M, tm_cap, 8), _tile(K, tk_cap)
    rd = _Riders(riders or ())
    nin, nco = 2 + (add is not None), len(rd.arrays())
    grid = (K // tk, M // tm)

    def body(*refs):
        a_ref, b_ref, o_ref = refs[0], refs[1], refs[nin + nco]
        rd.bind(refs[nin:nin + nco], refs[nin + nco + 1:nin + 2 * nco + 1], refs[nin + 2 * nco + 1:], grid).start()
        acc = _dot(a_ref[...], b_ref[...], "nt")
        if add is not None:
            acc = acc + refs[2][...]
        o_ref[...] = acc.astype(o_ref.dtype)
        rd.finish()

    in_specs = [pl.BlockSpec((tm, N), lambda j, i: (i, 0)), pl.BlockSpec((tk, N), lambda j, i: (j, 0))]
    args = [a, b]
    if add is not None:
        in_specs.append(pl.BlockSpec((tm, tk), lambda j, i: (i, j)))
        args.append(add)
    order = ("arbitrary", "arbitrary") if nco else ("parallel", "parallel")
    outs = pl.pallas_call(
        body, name=name, grid=grid, in_specs=in_specs + [HBM_SPEC] * nco,
        out_specs=[pl.BlockSpec((tm, tk), lambda j, i: (i, j))] + [HBM_SPEC] * nco,
        out_shape=[SDS((M, K), out_dtype)] + rd.out_shapes(),
        scratch_shapes=rd.scratch(), compiler_params=_cp(*order))(*args, *rd.arrays())
    return outs[0] if riders is None else (outs[0], rd.split(outs[1:]))


def _mm_tn(a, b, name, tm_cap=1024, tk_cap=512, tn_cap=1408):
    M, K = a.shape
    _, N = b.shape
    tm, tk, tn = _tile(M, tm_cap, 8), _tile(K, tk_cap), _tile(N, tn_cap)

    def body(a_ref, b_ref, o_ref):
        @pl.when(pl.program_id(2) == 0)
        def _():
            o_ref[...] = jnp.zeros_like(o_ref)

        o_ref[...] += _dot(a_ref[...], b_ref[...], "tn")

    return pl.pallas_call(
        body, name=name, grid=(K // tk, N // tn, M // tm),
        in_specs=[pl.BlockSpec((tm, tk), lambda k, n, m: (m, k)), pl.BlockSpec((tm, tn), lambda k, n, m: (m, n))],
        out_specs=pl.BlockSpec((tk, tn), lambda k, n, m: (k, n)), out_shape=SDS((K, N), F32),
        compiler_params=_cp("parallel", "parallel", "arbitrary"))(a, b)


def _rms_fwd(x, g, name, tb_cap=512):
    S, Dm = x.shape
    tb = _tile(S, tb_cap, 8)

    def body(x_ref, g_ref, o_ref):
        xv = x_ref[...]
        r = lax.rsqrt(jnp.mean(xv * xv, axis=-1, keepdims=True) + EPS)
        o_ref[...] = (xv * r * g_ref[...]).astype(o_ref.dtype)

    return pl.pallas_call(
        body, name=name, grid=(S // tb,),
        in_specs=[pl.BlockSpec((tb, Dm), lambda i: (i, 0)), pl.BlockSpec((1, Dm), lambda i: (0, 0))],
        out_specs=pl.BlockSpec((tb, Dm), lambda i: (i, 0)), out_shape=SDS((S, Dm), _MXU),
        compiler_params=_cp("parallel"))(x, g)


def _rms_bwd(x, g, dh, dres, name, tb_cap=512, low=True):
    S, Dm = x.shape
    tb = _tile(S, tb_cap, 8)
    need_dx = dres is not None

    def body(x_ref, g_ref, dh_ref, *rest):
        dg_ref = rest[-1]
        xv = x_ref[...]
        r = lax.rsqrt(jnp.mean(xv * xv, axis=-1, keepdims=True) + EPS)
        xh = xv * r
        dy = dh_ref[...].astype(F32)

        @pl.when(pl.program_id(0) == 0)
        def _():
            dg_ref[...] = jnp.zeros_like(dg_ref)

        dg_ref[...] += jnp.sum(dy * xh, axis=0, keepdims=True)
        if need_dx:
            gdy = dy * g_ref[...]
            dx = r * (gdy - xh * jnp.mean(xh * gdy, axis=-1, keepdims=True))
            tot = rest[0][...] + dx
            rest[1][...] = tot
            if low:
                rest[2][...] = tot.astype(rest[2].dtype)

    row = pl.BlockSpec((tb, Dm), lambda i: (i, 0))
    vec = pl.BlockSpec((1, Dm), lambda i: (0, 0))
    if need_dx:
        outs = [SDS((S, Dm), F32)] + ([SDS((S, Dm), _MXU)] if low else [])
        return pl.pallas_call(
            body, name=name, grid=(S // tb,), in_specs=[row, vec, row, row], out_specs=[row] * len(outs) + [vec],
            out_shape=outs + [SDS((1, Dm), F32)], compiler_params=_cp("arbitrary"))(x, g, dh, dres)
    return pl.pallas_call(
        body, name=name, grid=(S // tb,), in_specs=[row, vec, row], out_specs=vec,
        out_shape=SDS((1, Dm), F32), compiler_params=_cp("arbitrary"))(x, g, dh)


def _final_loss(x, g, tgt, name="final_loss", tb_cap=512):
    S, Dm = x.shape
    tb = _tile(S, tb_cap, 8)

    def body(x_ref, g_ref, t_ref, loss_ref, dx_ref, dxl_ref, dg_ref):
        xv = x_ref[...]
        gv = g_ref[...]
        r = lax.rsqrt(jnp.mean(xv * xv, axis=-1, keepdims=True) + EPS)
        xh = xv * r
        e = xh * gv - t_ref[...]

        @pl.when(pl.program_id(0) == 0)
        def _():
            loss_ref[...] = jnp.zeros_like(loss_ref)
            dg_ref[...] = jnp.zeros_like(dg_ref)

        loss_ref[...] += 0.5 * jnp.sum(jnp.mean(e * e, axis=-1, keepdims=True))
        dy = e * (1.0 / Dm)
        dg_ref[...] += jnp.sum(dy * xh, axis=0, keepdims=True)
        gdy = dy * gv
        dx = r * (gdy - xh * jnp.mean(xh * gdy, axis=-1, keepdims=True))
        dx_ref[...] = dx
        dxl_ref[...] = dx.astype(dxl_ref.dtype)

    row = pl.BlockSpec((tb, Dm), lambda i: (i, 0))
    vec = pl.BlockSpec((1, Dm), lambda i: (0, 0))
    return pl.pallas_call(
        body, name=name, grid=(S // tb,), in_specs=[row, vec, row],
        out_specs=[pl.BlockSpec((1, 128), lambda i: (0, 0)), row, row, vec],
        out_shape=[SDS((1, 128), F32), SDS((S, Dm), F32), SDS((S, Dm), _MXU), SDS((1, Dm), F32)],
        compiler_params=_cp("arbitrary"))(x, g, tgt)


SSD_HALO = 8
CF_HALO = 32

HBM_SPEC = pl.BlockSpec(memory_space=pl.ANY)


def _chip_peers(x, y):
    return [(1 - x, y), (x, 1 - y), (1 - x, 1 - y)]


def _remote(src, dst, send_sem, recv_sem, dev):
    return pltpu.make_async_remote_copy(src_ref=src, dst_ref=dst, send_sem=send_sem, recv_sem=recv_sem,
                                        device_id=dev, device_id_type=MESHID)


def _scatter_copies(srcs, outs, send_sems, recv_sems):
    x, y, c = lax.axis_index("x"), lax.axis_index("y"), lax.axis_index("c")
    me = 2 * x + y
    sends, recvs = [], []
    for i, (s, o) in enumerate(zip(srcs, outs)):
        for k, (px, py) in enumerate(_chip_peers(x, y)):
            j = 3 * i + k
            sends.append(_remote(s.at[2 * px + py], o.at[me], send_sems.at[j], recv_sems.at[j], (px, py, c)))
            recvs.append(_remote(s.at[me], o.at[2 * px + py], send_sems.at[j], recv_sems.at[j], (px, py, c)))
    return sends, recvs


def _pair_copies(srcs, outs, send_sems, recv_sems):
    x, y, c = lax.axis_index("x"), lax.axis_index("y"), lax.axis_index("c")
    sends = [_remote(s.at[j, 1 - c], o.at[j], send_sems.at[4 * i + j], recv_sems.at[4 * i + j], (x, y, 1 - c))
             for i, (s, o) in enumerate(zip(srcs, outs)) for j in range(4)]
    return sends, sends


def _rows_half(ref, rows, h):
    r = rows // 2
    return ref.at[pl.ds(h * r if isinstance(h, int) else pl.multiple_of(h * r, 8), r)]


def _gather_copies(srcs, outs, rows, send_sems, recv_sems):
    x, y, c = lax.axis_index("x"), lax.axis_index("y"), lax.axis_index("c")
    me = 2 * x + y
    sends, recvs = [], []
    for i, (s, o) in enumerate(zip(srcs, outs)):
        mine = _rows_half(s, rows[i], c)
        for k, (px, py) in enumerate(_chip_peers(x, y)):
            j = 3 * i + k
            sends.append(_remote(mine, o.at[me, c], send_sems.at[j], recv_sems.at[j], (px, py, c)))
            recvs.append(_remote(mine, o.at[2 * px + py, c], send_sems.at[j], recv_sems.at[j], (px, py, c)))
    return sends, recvs


def _gather_shapes(shards):
    return [SDS((4, 2, a.shape[0] // 2, a.shape[1]), a.dtype) for a in shards]


class _Rider:
    SEMS_PER_ARRAY = {"exchange": 3, "gather": 3, "pair": 4}

    def __init__(self, kind, arrays):
        self.kind, self.arrays = kind, list(arrays)

    def out_shapes(self):
        if self.kind == "gather":
            return _gather_shapes(self.arrays)
        if self.kind == "pair":
            return [SDS((4,) + a.shape[2:], a.dtype) for a in self.arrays]
        return [SDS(a.shape, a.dtype) for a in self.arrays]

    def scratch(self):
        n = self.SEMS_PER_ARRAY[self.kind] * len(self.arrays)
        return [pltpu.SemaphoreType.DMA((n,)), pltpu.SemaphoreType.DMA((n,))]

    def copies(self, srcs, outs, send_sems, recv_sems):
        if self.kind == "gather":
            return _gather_copies(srcs, outs, [a.shape[0] for a in self.arrays], send_sems, recv_sems)
        if self.kind == "pair":
            return _pair_copies(srcs, outs, send_sems, recv_sems)
        return _scatter_copies(srcs, outs, send_sems, recv_sems)


class _Riders:
    def __init__(self, riders):
        self.given = list(riders)
        self.riders = [r for r in self.given if r.arrays]

    def arrays(self):
        return [a for r in self.riders for a in r.arrays]

    def out_shapes(self):
        return [s for r in self.riders for s in r.out_shapes()]

    def scratch(self):
        return [s for r in self.riders for s in r.scratch()]

    def split(self, outs):
        res, k = [], 0
        for r in self.given:
            res.append(list(outs[k:k + len(r.arrays)]))
            k += len(r.arrays)
        return res

    def bind(self, in_refs, out_refs, sem_refs, steps):
        self.steps = steps if isinstance(steps, tuple) else (steps,)
        self.bound, k = [], 0
        for i, r in enumerate(self.riders):
            n = len(r.arrays)
            self.bound.append((r, in_refs[k:k + n], out_refs[k:k + n], sem_refs[2 * i], sem_refs[2 * i + 1]))
            k += n
        return self

    def _at(self, last):
        hit = None
        for ax, n in enumerate(self.steps):
            here = pl.program_id(ax) == (n - 1 if last else 0)
            hit = here if hit is None else jnp.logical_and(hit, here)
        return hit

    def _copies(self):
        sends, recvs = [], []
        for r, srcs, outs, send_sems, recv_sems in self.bound:
            s, w = r.copies(srcs, outs, send_sems, recv_sems)
            sends += s
            recvs += w
        return sends, recvs

    def start(self):
        if self.riders:
            @pl.when(self._at(last=False))
            def _():
                for cp in self._copies()[0]:
                    cp.start()

    def finish(self):
        if self.riders:
            @pl.when(self._at(last=True))
            def _():
                sends, recvs = self._copies()
                for cp in recvs:
                    cp.wait_recv()
                for cp in sends:
                    cp.wait_send()


def _head_consts():
    e = (lax.broadcasted_iota(jnp.int32, (128, NH * HP), 1) // HP == lax.broadcasted_iota(jnp.int32, (128, NH * HP), 0)).astype(F32)
    et = (lax.broadcasted_iota(jnp.int32, (NH * HP, 128), 0) // HP == lax.broadcasted_iota(jnp.int32, (NH * HP, 128), 1)).astype(F32)
    r = lax.broadcasted_iota(jnp.int32, (CH, CH), 0)
    c = lax.broadcasted_iota(jnp.int32, (CH, CH), 1)
    return e, et, (c <= r), (r <= c)


def _ssd_common(xbc_c, dtr, dtb, alog, e, tril, triu):
    xbc = _silu(xbc_c)
    xs = xbc[:, :NH * HP]
    dt = _softplus(dtr + dtb)
    A = -jnp.exp(alog)
    a = dt * A
    cs = _dot(tril, a, exact="a", passes=3)
    csT = _dot(a, triu, "tn", exact="b", passes=3)
    csL = cs[CH - 1:CH, :]
    wdec = jnp.exp(csL - cs) * dt
    dtE = _dot(dt, e, exact="b")
    ecsE = _dot(jnp.exp(cs), e, exact="b")
    wE = _dot(wdec, e, exact="b")
    eL = jnp.exp(csL)
    return xbc, xs, dt, A, cs, csT, csL, wdec, dtE, ecsE, wE, eL


def _ssd_fwd(proj, cw, cb, sc, norm_g, riders=(), name="ssd_fwd"):
    S = proj.shape[0]
    nc = S // CH
    rd = _Riders(riders)
    nco = len(rd.arrays())

    def body(*refs):
        z_ref, xp_ref, cw_ref, cb_ref, dtr_ref, sc_ref, ng_ref = refs[:7]
        xc_ref, y_ref, yn_ref, hp_ref = refs[7 + nco:11 + nco]
        hst, cext = refs[11 + 2 * nco:13 + 2 * nco]
        rd.bind(refs[7:7 + nco], refs[11 + nco:11 + 2 * nco], refs[13 + 2 * nco:], nc).start()

        @pl.when(pl.program_id(0) == 0)
        def _():
            hst[...] = jnp.zeros_like(hst)
            cext[pl.ds(0, SSD_HALO), :] = jnp.zeros((SSD_HALO, XBC), F32)

        cext[pl.ds(SSD_HALO, CH), :] = xp_ref[...]
        xc = jnp.zeros((CH, XBC), F32) + cb_ref[...]
        for k in range(KS):
            xc = xc + cext[pl.ds(SSD_HALO - (KS - 1) + k, CH), :] * cw_ref[k:k + 1, :]
        xc_ref[...] = xc
        cext[pl.ds(0, SSD_HALO), :] = cext[pl.ds(CH, SSD_HALO), :]

        e, et, tril, triu = _head_consts()
        xbc, xs, dt, A, cs, csT, csL, wdec, dtE, ecsE, wE, eL = _ssd_common(
            xc, dtr_ref[...], sc_ref[0:1, :], sc_ref[1:2, :], e, tril, triu)
        hp_ref[0] = hst[...]
        xd = xs * dtE
        xw = xs * wE
        dE = _dot(jnp.broadcast_to(sc_ref[2:3, :], (8, 128)), e, exact="b", passes=3)[0:1, :]
        eLcol = jnp.sum(et * eL, axis=1, keepdims=True)
        for g in range(NG):
            Bg = xbc[:, NH * HP + g * NS: NH * HP + (g + 1) * NS]
            Cg = xbc[:, NH * HP + NG * NS + g * NS: NH * HP + NG * NS + (g + 1) * NS]
            gs = slice(g * GW, (g + 1) * GW)
            G = _dot(Cg, Bg, "nt")
            hg = hst[gs, :]
            yoff = ecsE[:, gs] * _dot(Cg, hg, "nt")
            hst[gs, :] = eLcol[gs, :] * hg + _dot(xw[:, gs], Bg, "tn")
            for hh in range(NH // NG):
                h = g * (NH // NG) + hh
                hs = slice(h * HP, (h + 1) * HP)
                m = jnp.where(tril, jnp.exp(jnp.where(tril, cs[:, h:h + 1] - csT[h:h + 1, :], 0.0)), 0.0)
                yd = _dot(G * m, xd[:, hs])
                y_ref[:, hs] = yd + yoff[:, hh * HP:(hh + 1) * HP] + dE[:, hs] * xs[:, hs]
        y = y_ref[...]
        yz = y * _silu(z_ref[...])
        for g in range(NG):
            gs = slice(g * GW, (g + 1) * GW)
            yg = yz[:, gs]
            r = lax.rsqrt(jnp.mean(yg * yg, axis=-1, keepdims=True) + EPS)
            yn_ref[:, gs] = (yg * r * ng_ref[:, gs]).astype(yn_ref.dtype)
        rd.finish()

    outs = pl.pallas_call(
        body, name=name, grid=(nc,),
        in_specs=[pl.BlockSpec((CH, D), lambda c: (c, COL_Z // D)),
                  pl.BlockSpec((CH, XBC), lambda c: (c, COL_XBC // XBC)),
                  pl.BlockSpec((KS, XBC), lambda c: (0, 0)),
                  pl.BlockSpec((1, XBC), lambda c: (0, 0)),
                  pl.BlockSpec((CH, 128), lambda c: (c, COL_DT // 128)),
                  pl.BlockSpec((8, 128), lambda c: (0, 0)),
                  pl.BlockSpec((1, D), lambda c: (0, 0))] + [HBM_SPEC] * nco,
        out_specs=[pl.BlockSpec((CH, XBC), lambda c: (c, 0)), pl.BlockSpec((CH, D), lambda c: (c, 0)),
                   pl.BlockSpec((CH, D), lambda c: (c, 0)),
                   pl.BlockSpec((1, NH * HP, NS), lambda c: (c, 0, 0))] + [HBM_SPEC] * nco,
        out_shape=[SDS((S, XBC), F32), SDS((S, D), F32), SDS((S, D), _MXU), SDS((nc, NH * HP, NS), F32)]
        + rd.out_shapes(),
        scratch_shapes=[pltpu.VMEM((NH * HP, NS), F32), pltpu.VMEM((SSD_HALO + CH, XBC), F32)] + rd.scratch(),
        compiler_params=_cp("arbitrary"))(proj, proj, cw, cb, proj, sc, norm_g, *rd.arrays())
    return outs[:4], rd.split(outs[4:])


def _ssd_bwd(dmix, y, proj, xbc_c, hprev, cw, sc, norm_g, riders=(), name="ssd_bwd"):
    S = proj.shape[0]
    nc = S // CH
    rd = _Riders(riders)
    nco = len(rd.arrays())
    rev = lambda c: nc - 1 - c

    def body(*refs):
        dyn_ref, y_ref, z_ref, x_ref, xp_ref, dtr_ref, hp_ref, cw_ref, sc_ref, ng_ref = refs[:10]
        dz_ref, dx_ref, ddtr_ref, gcw_ref, gcb_ref, gsc_ref, gng_ref = refs[10 + nco:17 + nco]
        dh, dxd, cext = refs[17 + 2 * nco:20 + 2 * nco]
        rd.bind(refs[10:10 + nco], refs[17 + nco:17 + 2 * nco], refs[20 + 2 * nco:], nc).start()

        @pl.when(pl.program_id(0) == 0)
        def _():
            dh[...] = jnp.zeros_like(dh)
            cext[pl.ds(CH, SSD_HALO), :] = jnp.zeros((SSD_HALO, XBC), F32)
            gcw_ref[...] = jnp.zeros_like(gcw_ref)
            gcb_ref[...] = jnp.zeros_like(gcb_ref)
            gsc_ref[...] = jnp.zeros_like(gsc_ref)
            gng_ref[...] = jnp.zeros_like(gng_ref)

        e, et, tril, triu = _head_consts()
        xbc_c = x_ref[...]
        dtr = dtr_ref[...]
        dtb = sc_ref[0:1, :]
        xbc, xs, dt, A, cs, csT, csL, wdec, dtE, ecsE, wE, eL = _ssd_common(
            xbc_c, dtr, dtb, sc_ref[1:2, :], e, tril, triu)
        xd = xs * dtE
        xw = xs * wE
        dE = _dot(jnp.broadcast_to(sc_ref[2:3, :], (8, 128)), e, exact="b", passes=3)[0:1, :]
        eLcol = jnp.sum(et * eL, axis=1, keepdims=True)

        yv = y_ref[...]
        zv = z_ref[...]
        sz = _silu(zv)
        yz = yv * sz
        dyn = dyn_ref[...]
        dyz_parts = []
        for g in range(NG):
            gs = slice(g * GW, (g + 1) * GW)
            yg = yz[:, gs]
            r = lax.rsqrt(jnp.mean(yg * yg, axis=-1, keepdims=True) + EPS)
            yh = yg * r
            dn = dyn[:, gs]
            gng_ref[:, gs] += jnp.sum(dn * yh, axis=0, keepdims=True)
            gdn = dn * ng_ref[:, gs]
            dyz_parts.append(r * (gdn - yh * jnp.mean(yh * gdn, axis=-1, keepdims=True)))
        dyz = jnp.concatenate(dyz_parts, axis=1)
        dy = dyz * sz
        dz_ref[...] = (dyz * yv * _dsilu(zv)).astype(dz_ref.dtype)

        dxs = dE * dy
        dzo = ecsE * dy
        dcsL = jnp.zeros((1, 128), F32)
        ddt = jnp.zeros((CH, 128), F32)
        qcols = jnp.zeros((CH, 128), F32)
        qrows = jnp.zeros((128, CH), F32)
        lane = lax.broadcasted_iota(jnp.int32, (1, 128), 1)
        sub = lax.broadcasted_iota(jnp.int32, (128, 1), 0)
        dB_parts, dC_parts, yoff_parts, dxw_parts = [], [], [], []
        for g in range(NG):
            Bg = xbc[:, NH * HP + g * NS: NH * HP + (g + 1) * NS]
            Cg = xbc[:, NH * HP + NG * NS + g * NS: NH * HP + NG * NS + (g + 1) * NS]
            gs = slice(g * GW, (g + 1) * GW)
            hg = hp_ref[0, gs, :]
            dhn = dh[gs, :]
            G = _dot(Cg, Bg, "nt")
            yoff_parts.append(ecsE[:, gs] * _dot(Cg, hg, "nt"))
            dC = _dot(dzo[:, gs], hg)
            dhp = _dot(dzo[:, gs], Cg, "tn") + eLcol[gs, :] * dhn
            t1 = jnp.sum(dhn * hg, axis=1, keepdims=True) * eLcol[gs, :]
            dcsL = dcsL + jnp.sum(et[gs, :] * t1, axis=0, keepdims=True)
            dxw_parts.append(_dot(Bg, dhn, "nt"))
            dB = _dot(xw[:, gs], dhn)
            dgsum = jnp.zeros((CH, CH), F32)
            for hh in range(NH // NG):
                h = g * (NH // NG) + hh
                hs = slice(h * HP, (h + 1) * HP)
                m = jnp.where(tril, jnp.exp(jnp.where(tril, cs[:, h:h + 1] - csT[h:h + 1, :], 0.0)), 0.0)
                sc = G * m
                dyh = dy[:, hs]
                dxd[:, hs] = _dot(sc, dyh, "tn")
                dsc = _dot(dyh, xd[:, hs], "nt")
                q = dsc * sc
                qcols = qcols + jnp.where(lane == h, jnp.sum(q, axis=1, keepdims=True), 0.0)
                qrows = qrows + jnp.where(sub == h, jnp.sum(q, axis=0, keepdims=True), 0.0)
                dgsum = dgsum + dsc * m
            dC_parts.append(dC + _dot(dgsum, Bg))
            dB_parts.append(dB + _dot(dgsum, Cg, "tn"))
            dh[gs, :] = dhp
        yoff = jnp.concatenate(yoff_parts, axis=1)
        dxw = jnp.concatenate(dxw_parts, axis=1)
        dxdv = dxd[...]
        per_head = _dot(jnp.concatenate([dy * yoff, dxw * xs, dxdv * xs, dy * xs], axis=0), et, exact="b")
        dcs = qcols - qrows.T + per_head[0:CH]
        dw = per_head[CH:2 * CH]
        gsc_ref[2:3, :] += jnp.sum(per_head[3 * CH:4 * CH], axis=0, keepdims=True)
        dxs = dxs + wE * dxw + dtE * dxdv
        ddt = ddt + dw * jnp.exp(csL - cs) + per_head[2 * CH:3 * CH]
        dcs = dcs - dw * wdec
        dcsL = dcsL + jnp.sum(dw * wdec, axis=0, keepdims=True)
        last = lax.broadcasted_iota(jnp.int32, (CH, 128), 0) == CH - 1
        dcs = dcs + jnp.where(last, dcsL, 0.0)
        da = _dot(triu, dcs, exact="a", passes=3)
        ddt = ddt + da * A
        gsc_ref[1:2, :] += jnp.sum(da * dt, axis=0, keepdims=True) * A
        valid = lax.broadcasted_iota(jnp.int32, (CH, 128), 1) < NH
        ddtr = jnp.where(valid, ddt * _sigmoid(dtr + dtb), 0.0)
        gsc_ref[0:1, :] += jnp.sum(ddtr, axis=0, keepdims=True)
        ddtr_ref[...] = ddtr.astype(ddtr_ref.dtype)
        dxbc = jnp.concatenate([dxs] + dB_parts + dC_parts, axis=1)
        dxc = dxbc * _dsilu(xbc_c)
        cext[pl.ds(0, CH), :] = dxc
        xp = xp_ref[...]
        acc = jnp.zeros((CH, XBC), F32)
        for k in range(KS):
            sh = cext[pl.ds(KS - 1 - k, CH), :]
            acc = acc + sh * cw_ref[k:k + 1, :]
            gcw_ref[k:k + 1, :] += jnp.sum(xp * sh, axis=0, keepdims=True)
        gcb_ref[...] += jnp.sum(dxc, axis=0, keepdims=True)
        dx_ref[...] = acc.astype(dx_ref.dtype)
        cext[pl.ds(CH, SSD_HALO), :] = cext[pl.ds(0, SSD_HALO), :]
        rd.finish()

    vec = pl.BlockSpec((8, 128), lambda c: (0, 0))
    vecd = pl.BlockSpec((1, D), lambda c: (0, 0))
    cwsp = pl.BlockSpec((KS, XBC), lambda c: (0, 0))
    cbsp = pl.BlockSpec((1, XBC), lambda c: (0, 0))
    row = lambda w, j=0: pl.BlockSpec((CH, w), lambda c: (rev(c), j))
    outs = pl.pallas_call(
        body, name=name, grid=(nc,),
        in_specs=[row(D), row(D), row(D, COL_Z // D), row(XBC), row(XBC, COL_XBC // XBC), row(128, COL_DT // 128),
                  pl.BlockSpec((1, NH * HP, NS), lambda c: (rev(c), 0, 0)), cwsp, vec, vecd] + [HBM_SPEC] * nco,
        out_specs=[row(D), row(XBC), row(128), cwsp, cbsp, vec, vecd] + [HBM_SPEC] * nco,
        out_shape=[SDS((S, D), _MXU), SDS((S, XBC), _MXU), SDS((S, 128), _MXU), SDS((KS, XBC), F32),
                   SDS((1, XBC), F32), SDS((8, 128), F32), SDS((1, D), F32)] + rd.out_shapes(),
        scratch_shapes=[pltpu.VMEM((NH * HP, NS), F32), pltpu.VMEM((CH, NH * HP), F32),
                        pltpu.VMEM((CH + SSD_HALO, XBC), F32)] + rd.scratch(),
        compiler_params=_cp("arbitrary"))(dmix, y, proj, xbc_c, proj, proj, hprev, cw, sc, norm_g, *rd.arrays())
    return outs[:7], rd.split(outs[7:])


CONV_RT = 32


def _fill_phases(ext, ph, rows):
    for s in range(1, 8):
        ph[s - 1, pl.ds(0, rows), :] = ext[pl.ds(s, rows), :]


def _window(ext, ph, off, r0, ls):
    s = off % 8
    src = ext if s == 0 else ph.at[s - 1]
    return src[pl.ds(pl.multiple_of(off - s + r0, 8), CONV_RT), ls]


def _cf_fwd(proj, w, b, lg, lb, riders=(), name="cf_fwd", tb_cap=256):
    S = proj.shape[0]
    tb = _tile(S, tb_cap, 8)
    nb = S // tb
    rd = _Riders(riders)
    nco = len(rd.arrays())

    def body(*refs):
        a_ref, g_ref, w_ref, b_ref, lg_ref, lb_ref = refs[:6]
        u1_ref, u_ref = refs[6 + nco:8 + nco]
        ext, ph = refs[8 + 2 * nco:10 + 2 * nco]
        rd.bind(refs[6:6 + nco], refs[8 + nco:8 + 2 * nco], refs[10 + 2 * nco:], nb).start()

        @pl.when(pl.program_id(0) == 0)
        def _():
            ext[pl.ds(0, CF_HALO), :] = jnp.zeros((CF_HALO, D), F32)

        ext[pl.ds(CF_HALO, tb), :] = a_ref[...] * _sigmoid(g_ref[...])
        _fill_phases(ext, ph, tb + CF_HALO - 8)

        def tile(i, carry):
            r0 = pl.multiple_of(i * CONV_RT, CONV_RT)
            for l in range(D // 128):
                ls = pl.ds(l * 128, 128)
                acc = jnp.broadcast_to(b_ref[:, ls], (CONV_RT, 128))
                for k in range(KC):
                    acc = acc + _window(ext, ph, CF_HALO - (KC - 1) + k, r0, ls) * w_ref[k:k + 1, ls]
                u1_ref[pl.ds(r0, CONV_RT), ls] = acc
            return carry

        lax.fori_loop(0, tb // CONV_RT, tile, 0)
        acc = u1_ref[...]
        mu = jnp.mean(acc, axis=-1, keepdims=True)
        xc = acc - mu
        r = lax.rsqrt(jnp.mean(xc * xc, axis=-1, keepdims=True) + EPS)
        u_ref[...] = _silu(xc * r * lg_ref[...] + lb_ref[...]).astype(u_ref.dtype)
        ext[pl.ds(0, CF_HALO), :] = ext[pl.ds(tb, CF_HALO), :]
        rd.finish()

    vec = pl.BlockSpec((1, D), lambda i: (0, 0))
    outs = pl.pallas_call(
        body, name=name, grid=(nb,),
        in_specs=[pl.BlockSpec((tb, D), lambda i: (i, COL_A // D)), pl.BlockSpec((tb, D), lambda i: (i, COL_G // D)),
                  pl.BlockSpec((KC, D), lambda i: (0, 0)), vec, vec, vec] + [HBM_SPEC] * nco,
        out_specs=[pl.BlockSpec((tb, D), lambda i: (i, 0)), pl.BlockSpec((tb, D), lambda i: (i, 0))] + [HBM_SPEC] * nco,
        out_shape=[SDS((S, D), F32), SDS((S, D), _MXU)] + rd.out_shapes(),
        scratch_shapes=[pltpu.VMEM((CF_HALO + tb, D), F32), pltpu.VMEM((7, tb + CF_HALO - 8, D), F32)] + rd.scratch(),
        compiler_params=_cp("arbitrary"))(proj, proj, w, b, lg, lb, *rd.arrays())
    return outs[:2], rd.split(outs[2:])


def _cf_bwd(dmix, u1, proj, w, lg, lb, riders=(), name="cf_bwd", tb_cap=256):
    S = proj.shape[0]
    tb = _tile(S, tb_cap, 8)
    nb = S // tb
    rd = _Riders(riders)
    nco = len(rd.arrays())
    rev = lambda i: nb - 1 - i

    def body(*refs):
        du_ref, u1_ref, a_ref, g_ref, w_ref, lg_ref, lb_ref = refs[:7]
        da_ref, dg_ref, dw_ref, db_ref, dlg_ref, dlb_ref = refs[7 + nco:13 + nco]
        ext, ph, u0s = refs[13 + 2 * nco:16 + 2 * nco]
        rd.bind(refs[7:7 + nco], refs[13 + nco:13 + 2 * nco], refs[16 + 2 * nco:], nb).start()

        @pl.when(pl.program_id(0) == 0)
        def _():
            ext[pl.ds(tb, CF_HALO), :] = jnp.zeros((CF_HALO, D), F32)
            dw_ref[...] = jnp.zeros_like(dw_ref)
            db_ref[...] = jnp.zeros_like(db_ref)
            dlg_ref[...] = jnp.zeros_like(dlg_ref)
            dlb_ref[...] = jnp.zeros_like(dlb_ref)

        u1 = u1_ref[...]
        mu = jnp.mean(u1, axis=-1, keepdims=True)
        xc = u1 - mu
        r = lax.rsqrt(jnp.mean(xc * xc, axis=-1, keepdims=True) + EPS)
        xh = xc * r
        lgv = lg_ref[...]
        du2 = du_ref[...] * _dsilu(xh * lgv + lb_ref[...])
        dlg_ref[...] += jnp.sum(du2 * xh, axis=0, keepdims=True)
        dlb_ref[...] += jnp.sum(du2, axis=0, keepdims=True)
        gd = du2 * lgv
        du1 = r * (gd - jnp.mean(gd, axis=-1, keepdims=True) - xh * jnp.mean(gd * xh, axis=-1, keepdims=True))
        db_ref[...] += jnp.sum(du1, axis=0, keepdims=True)
        ext[pl.ds(0, tb), :] = du1
        u0s[...] = a_ref[...] * _sigmoid(g_ref[...])
        _fill_phases(ext, ph, tb + CF_HALO - 8)

        for l in range(D // 128):
            ls = pl.ds(l * 128, 128)

            def tile(i, accs, ls=ls):
                r0 = pl.multiple_of(i * CONV_RT, CONV_RT)
                rows = pl.ds(r0, CONV_RT)
                u0t = u0s[rows, ls]
                acc = jnp.zeros((CONV_RT, 128), F32)
                out = []
                for k in range(KC):
                    win = _window(ext, ph, KC - 1 - k, r0, ls)
                    acc = acc + win * w_ref[k:k + 1, ls]
                    p = u0t * win
                    out.append(accs[k] + ((p[0:8] + p[8:16]) + (p[16:24] + p[24:32])))
                sg = _sigmoid(g_ref[rows, ls])
                da_ref[rows, ls] = (acc * sg).astype(da_ref.dtype)
                dg_ref[rows, ls] = (acc * a_ref[rows, ls] * sg * (1.0 - sg)).astype(dg_ref.dtype)
                return tuple(out)

            accs = lax.fori_loop(0, tb // CONV_RT, tile, tuple(jnp.zeros((8, 128), F32) for _ in range(KC)))
            for k in range(KC):
                dw_ref[k:k + 1, ls] += jnp.sum(accs[k], axis=0, keepdims=True)
        ext[pl.ds(tb, CF_HALO), :] = ext[pl.ds(0, CF_HALO), :]
        rd.finish()

    vec = pl.BlockSpec((1, D), lambda i: (0, 0))
    wsp = pl.BlockSpec((KC, D), lambda i: (0, 0))
    row = lambda j=0: pl.BlockSpec((tb, D), lambda i: (rev(i), j))
    outs = pl.pallas_call(
        body, name=name, grid=(nb,),
        in_specs=[row(1), row(), row(COL_A // D), row(COL_G // D), wsp, vec, vec] + [HBM_SPEC] * nco,
        out_specs=[row(), row(), wsp, vec, vec, vec] + [HBM_SPEC] * nco,
        out_shape=[SDS((S, D), _MXU), SDS((S, D), _MXU), SDS((KC, D), F32),
                   SDS((1, D), F32), SDS((1, D), F32), SDS((1, D), F32)] + rd.out_shapes(),
        scratch_shapes=[pltpu.VMEM((tb + CF_HALO, D), F32), pltpu.VMEM((7, tb + CF_HALO - 8, D), F32),
                        pltpu.VMEM((tb, D), F32)] + rd.scratch(),
        compiler_params=_cp("arbitrary"))(dmix, u1, proj, proj, w, lg, lb, *rd.arrays())
    return outs[:6], rd.split(outs[6:])


def _attn_fwd(q, kv, name="attn_fwd", tq_cap=512):
    S = q.shape[0]
    tq = _tile(S, tq_cap, 8)
    scale = XD ** -0.5

    def body(q_ref, kv_ref, o_ref):
        for h in range(XH):
            hs = slice(h * XD, (h + 1) * XD)
            s = _dot(q_ref[:, hs], kv_ref[:, hs], "nt") * scale
            s = s - jnp.max(s, axis=-1, keepdims=True)
            p = jnp.exp(s)
            p = p / jnp.sum(p, axis=-1, keepdims=True)
            o_ref[:, hs] = _dot(p, kv_ref[:, D + h * XD: D + (h + 1) * XD]).astype(o_ref.dtype)

    return pl.pallas_call(
        body, name=name, grid=(S // tq,),
        in_specs=[pl.BlockSpec((tq, D), lambda i: (i, 0)), pl.BlockSpec((MEM, 2 * D), lambda i: (0, 0))],
        out_specs=pl.BlockSpec((tq, D), lambda i: (i, 0)), out_shape=SDS((S, D), _MXU),
        compiler_params=_cp("parallel"))(q, kv)


def _attn_bwd(do, q, kv, riders=(), name="attn_bwd", tq_cap=512):
    S = q.shape[0]
    tq = _tile(S, tq_cap, 8)
    scale = XD ** -0.5
    rd = _Riders(riders)
    nco = len(rd.arrays())

    def body(*refs):
        do_ref, q_ref, kv_ref = refs[:3]
        dq_ref, dkv_ref = refs[3 + nco:5 + nco]
        rd.bind(refs[3:3 + nco], refs[5 + nco:5 + 2 * nco], refs[5 + 2 * nco:], S // tq).start()

        @pl.when(pl.program_id(0) == 0)
        def _():
            dkv_ref[...] = jnp.zeros_like(dkv_ref)

        for h in range(XH):
            hs = slice(h * XD, (h + 1) * XD)
            vs = slice(D + h * XD, D + (h + 1) * XD)
            qh = q_ref[:, hs]
            kh = kv_ref[:, hs]
            s = _dot(qh, kh, "nt") * scale
            s = s - jnp.max(s, axis=-1, keepdims=True)
            p = jnp.exp(s)
            p = p / jnp.sum(p, axis=-1, keepdims=True)
            doh = do_ref[:, hs]
            dp = _dot(doh, kv_ref[:, vs], "nt")
            ds = p * (dp - jnp.sum(dp * p, axis=-1, keepdims=True)) * scale
            dq_ref[:, hs] = _dot(ds, kh).astype(dq_ref.dtype)
            dkv_ref[:, hs] += _dot(ds, qh, "tn")
            dkv_ref[:, vs] += _dot(p, doh, "tn")
        rd.finish()

    outs = pl.pallas_call(
        body, name=name, grid=(S // tq,),
        in_specs=[pl.BlockSpec((tq, D), lambda i: (i, 0)), pl.BlockSpec((tq, D), lambda i: (i, 0)),
                  pl.BlockSpec((MEM, 2 * D), lambda i: (0, 0))] + [HBM_SPEC] * nco,
        out_specs=[pl.BlockSpec((tq, D), lambda i: (i, 0)), pl.BlockSpec((MEM, 2 * D), lambda i: (0, 0))]
        + [HBM_SPEC] * nco,
        out_shape=[SDS((S, D), _MXU), SDS((MEM, 2 * D), F32)] + rd.out_shapes(), scratch_shapes=rd.scratch(),
        compiler_params=_cp("arbitrary"))(do, q, kv, *rd.arrays())
    return outs[:2], rd.split(outs[2:])


def _ffn_in(hf, wg_t, wu_t, name="ffn_in", tm_cap=512, tn_cap=1408):
    S, K = hf.shape
    N = wg_t.shape[0]
    tm, tn = _tile(S, tm_cap, 8), _tile(N, tn_cap)

    def body(a_ref, g_ref, u_ref, act_ref, gt_ref, up_ref):
        a = a_ref[...]
        gt = _dot(a, g_ref[...], "nt")
        up = _dot(a, u_ref[...], "nt")
        act_ref[...] = (_silu(gt) * up).astype(act_ref.dtype)
        gt_ref[...] = gt.astype(gt_ref.dtype)
        up_ref[...] = up.astype(up_ref.dtype)

    wsp = pl.BlockSpec((tn, K), lambda j, i: (j, 0))
    osp = pl.BlockSpec((tm, tn), lambda j, i: (i, j))
    return pl.pallas_call(
        body, name=name, grid=(N // tn, S // tm), in_specs=[pl.BlockSpec((tm, K), lambda j, i: (i, 0)), wsp, wsp],
        out_specs=[osp, osp, osp], out_shape=[SDS((S, N), _MXU)] * 3,
        compiler_params=_cp("parallel", "parallel"))(hf, wg_t, wu_t)


def _ffn_out_bwd(dx, w_down, gt, up, name="ffn_out_dx", tm_cap=512, tk_cap=1408):
    S, N = dx.shape
    K = w_down.shape[0]
    tm, tk = _tile(S, tm_cap, 8), _tile(K, tk_cap)

    def body(a_ref, b_ref, g_ref, u_ref, dg_ref, du_ref):
        d = _dot(a_ref[...], b_ref[...], "nt")
        gt = g_ref[...].astype(F32)
        s = _sigmoid(gt)
        dg_ref[...] = (d * u_ref[...].astype(F32) * (s * (1.0 + gt * (1.0 - s)))).astype(dg_ref.dtype)
        du_ref[...] = (d * gt * s).astype(du_ref.dtype)

    osp = pl.BlockSpec((tm, tk), lambda j, i: (i, j))
    return pl.pallas_call(
        body, name=name, grid=(K // tk, S // tm),
        in_specs=[pl.BlockSpec((tm, N), lambda j, i: (i, 0)), pl.BlockSpec((tk, N), lambda j, i: (j, 0)), osp, osp],
        out_specs=[osp, osp], out_shape=[SDS((S, K), _MXU)] * 2,
        compiler_params=_cp("parallel", "parallel"))(dx, w_down, gt, up)


AG_RIDE = (("w_down",), ("w_out", "w_q", "w_kv", "w_o"), ("w_gate", "w_up"))


def _local_step(x, mem, tgt, W, P, core=None, late=None):
    pair, got = {}, {}
    ride = [[late[n] for n in grp] if late is not None else [] for grp in AG_RIDE]

    def halves(group):
        if core is None:
            return []
        gs = [_shard_grad(n, GW) for n in group]
        return [g.reshape(4, 2, g.shape[1] // 2, g.shape[2]) for g in gs]

    def pair_sums(group, hs, theirs):
        ps = [_pair_sum(h_, t, core, "rs_pair_sum_" + n) for h_, t, n in zip(hs, theirs, group)]
        pair.update(zip(group, ps))
        return ps

    h = _rms_fwd(x, P["g_mix"], "rms_mix")
    proj, (bufs0,) = _mm_nn(h, W["main"], "in_proj", tm_cap=256, tn_cap=MAINW, riders=[_Rider("gather", ride[0])])
    (xbc_c, y, yn, hprev), (bufs1,) = _ssd_fwd(proj, P["conv4_w"], P["conv4_b"], P["sc"], P["ssd_norm_g"],
                                                riders=[_Rider("gather", ride[1])])
    (u1, u), (bufs2,) = _cf_fwd(proj, P["cf_w"], P["cf_b"], P["ln_g"], P["ln_b"], riders=[_Rider("gather", ride[2])])
    if late is not None:
        names = AG_RIDE[0] + AG_RIDE[1] + AG_RIDE[2]
        full = _gather_finish_list(ride[0] + ride[1] + ride[2], bufs0 + bufs1 + bufs2)
        W = dict(W, **_pack_late(dict(zip(names, full))))
    mix = jnp.concatenate([yn, u], axis=1)
    x1 = _mm_nn(mix, W["out"], "out_proj", add=x)
    hq = _rms_fwd(x1, P["g_xattn"], "rms_xattn")
    q = _mm_nn(hq, W["q"], "q_proj")
    mn = _rms_fwd(mem, P["g_mem"], "rms_mem")
    kv = _mm_nn(mn, W["kv"], "kv_proj")
    o = _attn_fwd(q, kv)
    x2 = _mm_nn(o, W["o"], "o_proj", add=x1)
    hf = _rms_fwd(x2, P["g_ffn"], "rms_ffn")
    act, gt, up = _ffn_in(hf, W["gate_t"], W["up_t"])
    x3 = _mm_nn(act, W["down"], "ffn_out", add=x2)
    loss, dx3, dx3b, g_final = _final_loss(x3, P["g_final"], tgt)
    GW, GP = {}, {"g_final": g_final}
    GW["down"] = _mm_tn(act, dx3b, "ffn_out_dw", tk_cap=1408, tn_cap=1024)
    dgt, dup = _ffn_out_bwd(dx3b, W["down"], gt, up)
    dhf = _mm_nn(dgt, W["gate_t"], "ffn_gate_dx", tm_cap=512)
    dhf = _mm_nn(dup, W["up_t"], "ffn_up_dx", add=dhf, tm_cap=512)
    GW["gate_t"] = _mm_tn(dgt, hf, "ffn_gate_dw", tk_cap=1408, tn_cap=1024)
    GW["up_t"] = _mm_tn(dup, hf, "ffn_up_dw", tk_cap=1408, tn_cap=1024)
    ffn_halves = halves(RS_GROUPS[0])
    dx2, dx2b, GP["g_ffn"] = _rms_bwd(x2, P["g_ffn"], dhf, dx3, "rms_ffn_bwd")
    do = _mm_nt(dx2b, W["o"], "o_proj_dx")
    GW["o"] = _mm_tn(o, dx2b, "o_proj_dw")
    (dq, dkv), (ffn_theirs,) = _attn_bwd(do, q, kv, riders=[_Rider("pair", ffn_halves)])
    ffn_pieces = pair_sums(RS_GROUPS[0], ffn_halves, ffn_theirs)
    dhq = _mm_nt(dq, W["q"], "q_proj_dx")
    GW["q"] = _mm_tn(hq, dq, "q_proj_dw")
    dkvb = dkv.astype(_MXU)
    GW["kv"] = _mm_tn(mn, dkvb, "kv_proj_dw", tm_cap=256)
    dmn = _mm_nt(dkvb, W["kv"], "kv_proj_dx")
    GP["g_mem"] = _rms_bwd(mem, P["g_mem"], dmn, None, "rms_mem_bwd")
    dx1, dx1b, GP["g_xattn"] = _rms_bwd(x1, P["g_xattn"], dhq, dx2, "rms_xattn_bwd")
    dmix = _mm_nt(dx1b, W["out"], "out_proj_dx")
    GW["out"] = _mm_tn(mix, dx1b, "out_proj_dw", tn_cap=1024)
    attn_halves = halves(RS_GROUPS[1])
    (da, dg, GP["cf_w"], GP["cf_b"], GP["ln_g"], GP["ln_b"]), (came, attn_theirs) = _cf_bwd(
        dmix, u1, proj, P["cf_w"], P["ln_g"], P["ln_b"],
        riders=[_Rider("exchange", ffn_pieces), _Rider("pair", attn_halves)])
    got.update(zip(RS_GROUPS[0], came))
    attn_pieces = pair_sums(RS_GROUPS[1], attn_halves, attn_theirs)
    (dz, dxbc, ddtr, GP["conv4_w"], GP["conv4_b"], GP["sc"], GP["ssd_norm_g"]), (came,) = _ssd_bwd(
        dmix, y, proj, xbc_c, hprev, P["conv4_w"], P["sc"], P["ssd_norm_g"],
        riders=[_Rider("exchange", attn_pieces)])
    got.update(zip(RS_GROUPS[1], came))
    dproj = jnp.concatenate([dz, da, dg, dxbc, ddtr], axis=1)
    GW["main"] = _mm_tn(h, dproj, "in_proj_dw", tm_cap=512, tk_cap=256, tn_cap=MAINW)
    in_halves = halves(RS_GROUPS[2])
    in_pieces = pair_sums(RS_GROUPS[2], in_halves, _pair_split_list(in_halves, "rs_pair_send_w_in")) if in_halves else []
    dh, (came,) = _mm_nt(dproj, W["main"], "in_proj_dx", tk_cap=512, riders=[_Rider("exchange", in_pieces)])
    got.update(zip(RS_GROUPS[2], came))
    grad_x, GP["g_mix"] = _rms_bwd(x, P["g_mix"], dh, dx1, "rms_mix_bwd", low=False)
    if core is None:
        return loss, grad_x, GW, GP
    return loss, grad_x, GW, GP, pair, got


Z_END, XBC_END, DT_END = NH * HP, NH * HP + XBC, NH * HP + XBC + NH


def _pad_to(a, rows=None, cols=None):
    r = 0 if rows is None else rows - a.shape[0]
    c = 0 if cols is None else cols - a.shape[1]
    return jnp.pad(a, ((0, r), (0, c)))


IN_W = DT_END + 2 * D
W_IN_SEGS = [(0, Z_END, "main", COL_Z), (Z_END, XBC_END, "main", COL_XBC), (XBC_END, DT_END, "main", COL_DT),
             (DT_END, DT_END + D, "main", COL_A), (DT_END + D, IN_W, "main", COL_G)]
BIG = [("w_in", True), ("w_out", False), ("w_q", False), ("w_kv", True), ("w_o", False), ("w_gate", False),
       ("w_up", False), ("w_down", False)]
TRANSPOSED = ("w_gate", "w_up")


def _ref_cols(pieces, a, b):
    cw = IN_W // 4
    out = []
    for j in range(4):
        lo, hi = max(a, j * cw), min(b, (j + 1) * cw)
        if lo < hi:
            out.append(pieces[j][:, lo - j * cw:hi - j * cw])
    return out


def _cat_cols(pieces):
    return jnp.concatenate([pieces[j] for j in range(4)], axis=1)


def _pack_in(w_in):
    dt = _ref_cols(w_in, XBC_END, DT_END)
    pad = jnp.zeros((dt[0].shape[0], MAINW - COL_DT - NH), dt[0].dtype)
    main = jnp.concatenate(_ref_cols(w_in, 0, Z_END) + _ref_cols(w_in, DT_END, IN_W) + _ref_cols(w_in, Z_END, XBC_END)
                           + dt + [pad], axis=1)
    return {"main": main}


def _pack_late(pc):
    rows = lambda n: pc[n].reshape(-1, pc[n].shape[-1])
    return {"out": rows("w_out"), "q": rows("w_q"), "kv": _cat_cols(pc["w_kv"]), "o": rows("w_o"),
            "gate_t": rows("w_gate"), "up_t": rows("w_up"), "down": rows("w_down")}


GW_KEY = {"w_gate": "gate_t", "w_up": "up_t", "w_kv": "kv", "w_out": "out", "w_q": "q", "w_o": "o", "w_down": "down"}
RS_GROUPS = (("w_down", "w_gate", "w_up"), ("w_out", "w_q", "w_kv", "w_o"), ("w_in",))


def _shard_grad(name, GW):
    if name == "w_in":
        cw = IN_W // 4
        pieces = []
        for j in range(4):
            parts = []
            for a, b, src, col in W_IN_SEGS:
                lo, hi = max(a, j * cw), min(b, (j + 1) * cw)
                if lo < hi:
                    parts.append(GW[src][:, col + lo - a:col + hi - a])
            pieces.append(jnp.concatenate(parts, axis=1))
        return jnp.stack(pieces)
    g = GW[GW_KEY[name]]
    if dict(BIG)[name]:
        cw = g.shape[1] // 4
        return jnp.stack([g[:, j * cw:(j + 1) * cw] for j in range(4)])
    return g.reshape(4, g.shape[0] // 4, g.shape[1])


def _stack_sc(dt_bias, a_log, d):
    return _pad_to(jnp.concatenate([dt_bias, a_log, d], axis=0), rows=8, cols=128)


COMM_PARAMS = pltpu.CompilerParams(vmem_limit_bytes=VMEM_LIMIT)


def _dma_sems(*counts):
    return [pltpu.SemaphoreType.DMA((n,)) for n in counts]


def _allgather_list(arrs, name):
    n = len(arrs)
    halved = [a.shape[0] % 16 == 0 for a in arrs]
    oshape = [(4, 2, a.shape[0] // 2, a.shape[1]) if h else (4, 1) + a.shape for a, h in zip(arrs, halved)]

    def body(*refs):
        srcs, outs = refs[:n], refs[n:2 * n]
        ici_send, ici_recv, own_send, own_recv, fwd_send, fwd_recv = refs[2 * n:]
        x, y, c = lax.axis_index("x"), lax.axis_index("y"), lax.axis_index("c")
        me = 2 * x + y
        sib = (x, y, 1 - c)
        peers = _chip_peers(x, y)

        def half(i, h):
            r = arrs[i].shape[0] // 2
            if not halved[i]:
                return srcs[i]
            return srcs[i].at[pl.ds(h * r if isinstance(h, int) else pl.multiple_of(h * r, 8), r)]

        ici, own, fwd = [], [], []
        for i in range(n):
            mine_h = c if halved[i] else 0
            for k, (px, py) in enumerate(peers):
                s = 3 * i + k
                ici.append(_remote(half(i, c), outs[i].at[me, mine_h], ici_send.at[s], ici_recv.at[s], (px, py, c)))
            for h in range(2 if halved[i] else 1):
                s = 2 * i + h
                own.append(_remote(half(i, h), outs[i].at[me, h], own_send.at[s], own_recv.at[s], sib))
        for cp in ici + own:
            cp.start()
        for i in range(n):
            if not halved[i]:
                continue
            for k, (px, py) in enumerate(peers):
                s = 3 * i + k
                got = outs[i].at[2 * px + py, c]
                _remote(half(i, c), got, ici_send.at[s], ici_recv.at[s], (px, py, c)).wait_recv()
                f = _remote(got, got, fwd_send.at[s], fwd_recv.at[s], sib)
                f.start()
                fwd.append(f)
        for i in range(n):
            for k, (px, py) in enumerate(peers):
                s = 3 * i + k
                if halved[i]:
                    _remote(half(i, c), outs[i].at[2 * px + py, 1 - c], fwd_send.at[s], fwd_recv.at[s], sib).wait_recv()
                else:
                    _remote(srcs[i], outs[i].at[2 * px + py, 0], ici_send.at[s], ici_recv.at[s], (px, py, c)).wait_recv()
            for h in range(2 if halved[i] else 1):
                s = 2 * i + h
                _remote(half(i, h), outs[i].at[me, h], own_send.at[s], own_recv.at[s], sib).wait_recv()
        for cp in ici + own + fwd:
            cp.wait_send()

    outs = pl.pallas_call(
        body, name=name, in_specs=[HBM_SPEC] * n, out_specs=[HBM_SPEC] * n,
        out_shape=[SDS(s, a.dtype) for s, a in zip(oshape, arrs)],
        scratch_shapes=_dma_sems(3 * n, 3 * n, 2 * n, 2 * n, 3 * n, 3 * n), compiler_params=COMM_PARAMS)(*arrs)
    return [o.reshape((4,) + a.shape) for o, a in zip(outs, arrs)]


def _pair_split_list(gs, name):
    n = len(gs)

    def body(*refs):
        sends, recvs = _pair_copies(refs[:n], refs[n:2 * n], *refs[2 * n:])
        for cp in sends:
            cp.start()
        for cp in recvs:
            cp.wait_recv()
        for cp in sends:
            cp.wait_send()

    return pl.pallas_call(
        body, name=name, in_specs=[HBM_SPEC] * n, out_specs=[HBM_SPEC] * n,
        out_shape=[SDS((4,) + g.shape[2:], g.dtype) for g in gs],
        scratch_shapes=_dma_sems(4 * n, 4 * n), compiler_params=COMM_PARAMS)(*gs)


def _gather_finish_list(shards, bufs, name="allgather_finish"):
    n = len(shards)

    def body(*refs):
        srcs, outs = refs[:n], refs[2 * n:3 * n]
        own_send, own_recv, fwd_send, fwd_recv = refs[3 * n:]
        x, y, c = lax.axis_index("x"), lax.axis_index("y"), lax.axis_index("c")
        me = 2 * x + y
        sib = (x, y, 1 - c)
        sends, recvs = [], []
        for i in range(n):
            for h in range(2):
                own = _remote(_rows_half(srcs[i], shards[i].shape[0], h), outs[i].at[me, h],
                              own_send.at[2 * i + h], own_recv.at[2 * i + h], sib)
                sends.append(own)
                recvs.append(own)
            for k, (px, py) in enumerate(_chip_peers(x, y)):
                got, s = outs[i].at[2 * px + py, c], 3 * i + k
                sends.append(_remote(got, got, fwd_send.at[s], fwd_recv.at[s], sib))
                recvs.append(_remote(got, outs[i].at[2 * px + py, 1 - c], fwd_send.at[s], fwd_recv.at[s], sib))
        for cp in sends:
            cp.start()
        for cp in recvs:
            cp.wait_recv()
        for cp in sends:
            cp.wait_send()

    outs = pl.pallas_call(
        body, name=name, in_specs=[HBM_SPEC] * (2 * n), out_specs=[HBM_SPEC] * n,
        out_shape=[SDS(b.shape, b.dtype) for b in bufs], input_output_aliases={n + i: i for i in range(n)},
        scratch_shapes=_dma_sems(2 * n, 2 * n, 3 * n, 3 * n), compiler_params=COMM_PARAMS)(*shards, *bufs)
    return [o.reshape((4,) + a.shape) for o, a in zip(outs, shards)]


JOIN_SPLIT = 4


def _pair_join_list(bufs, name="rs_pair_join"):
    n = len(bufs)

    def body(*refs):
        outs = refs[n:2 * n]
        send_sems, recv_sems = refs[2 * n:]
        x, y, c = lax.axis_index("x"), lax.axis_index("y"), lax.axis_index("c")
        sib = (x, y, 1 - c)
        sends, recvs = [], []
        for i in range(n):
            rc = bufs[i].shape[1] // JOIN_SPLIT
            for q in range(JOIN_SPLIT):
                k = JOIN_SPLIT * i + q
                rows = pl.ds(q * rc, rc)
                sends.append(_remote(outs[i].at[c, rows], outs[i].at[c, rows], send_sems.at[k], recv_sems.at[k], sib))
                recvs.append(_remote(outs[i].at[c, rows], outs[i].at[1 - c, rows], send_sems.at[k], recv_sems.at[k], sib))
        for cp in sends:
            cp.start()
        for cp in recvs:
            cp.wait_recv()
        for cp in sends:
            cp.wait_send()

    return pl.pallas_call(
        body, name=name, in_specs=[HBM_SPEC] * n, out_specs=[HBM_SPEC] * n,
        out_shape=[SDS(b.shape, b.dtype) for b in bufs], input_output_aliases={i: i for i in range(n)},
        scratch_shapes=_dma_sems(JOIN_SPLIT * n, JOIN_SPLIT * n), compiler_params=COMM_PARAMS)(*bufs)


def _pair_sum(g, theirs, core, name):
    _, _, r, c = g.shape

    def body(core_ref, g_ref, t_ref, o_ref):
        o_ref[...] = (g_ref[...] + t_ref[...]).astype(o_ref.dtype)

    spec = pltpu.PrefetchScalarGridSpec(
        num_scalar_prefetch=1, grid=(4,),
        in_specs=[pl.BlockSpec((None, None, r, c), lambda j, core_ref: (j, core_ref[0], 0, 0)),
                  pl.BlockSpec((None, r, c), lambda j, core_ref: (j, 0, 0))],
        out_specs=pl.BlockSpec((None, r, c), lambda j, core_ref: (j, 0, 0)))
    return pl.pallas_call(body, name=name, grid_spec=spec, out_shape=SDS((4, r, c), BF16),
                          compiler_params=_cp("parallel"))(core, g, theirs)


def _chip_sum(own, got, where, name):
    _, r, c = own.shape
    tr = r // 2

    def body(w_ref, a_ref, b1_ref, b2_ref, b3_ref, o_ref):
        o_ref[...] = ((a_ref[...].astype(F32) + b1_ref[...].astype(F32)) + b2_ref[...].astype(F32)) + b3_ref[...].astype(F32)

    piece = lambda k: pl.BlockSpec((None, tr, c), lambda i, w_ref: ((w_ref[0] + k) % 4, i, 0))
    spec = pltpu.PrefetchScalarGridSpec(
        num_scalar_prefetch=1, grid=(r // tr,), in_specs=[piece(0), piece(1), piece(2), piece(3)],
        out_specs=pl.BlockSpec((None, tr, c), lambda i, w_ref: (w_ref[1], i, 0)))
    return pl.pallas_call(body, name=name, grid_spec=spec, out_shape=SDS((2, r, c), F32),
                          compiler_params=_cp("parallel"))(where, own, got, got, got)


def _adam_math(w, g, m, v):
    bc1 = 1.0 - ADAM_B1 ** ADAM_STEP
    bc2 = 1.0 - ADAM_B2 ** ADAM_STEP
    mn = ADAM_B1 * m + (1.0 - ADAM_B1) * g
    vn = ADAM_B2 * v + (1.0 - ADAM_B2) * (g * g)
    return -ADAM_LR * ((mn / bc1) / (jnp.sqrt(vn / bc2) + ADAM_EPS) + ADAM_WD * w), mn, vn


PACK_COLS = XBC
PACK = {"g_mix": (0, 1, D), "g_xattn": (1, 1, D), "g_mem": (2, 1, D), "g_ffn": (3, 1, D), "g_final": (4, 1, D),
        "ssd_norm_g": (5, 1, D), "cf_b": (6, 1, D), "ln_g": (7, 1, D), "ln_b": (8, 1, D), "conv4_b": (9, 1, XBC),
        "conv4_w": (10, KS, XBC), "sc": (16, 8, 128), "cf_w": (24, KC, D), "loss": (55, 1, 128)}
PACK_ROWS = 56
SMALL_ADAM = ["g_mix", "g_xattn", "g_mem", "g_ffn", "g_final", "ssd_norm_g", "cf_b", "ln_g", "ln_b", "conv4_b", "sc"]


def _small_allreduce_adamw(grads, wts, mom, var, name="allreduce_small"):
    gk = list(PACK)
    ng, na = len(gk), len(SMALL_ADAM)

    def body(*refs):
        g_in = refs[:ng]
        w_in, m_in, v_in = (refs[ng + i * na: ng + (i + 1) * na] for i in range(3))
        o = refs[ng + 3 * na:]
        g_out = o[:ng]
        d_out, m_out, v_out = (o[ng + i * na: ng + (i + 1) * na] for i in range(3))
        pack, buf, acc, send_sems, recv_sems = o[ng + 3 * na:]
        x, y, c = lax.axis_index("x"), lax.axis_index("y"), lax.axis_index("c")
        me = 4 * x + 2 * y + c
        pack[...] = jnp.zeros_like(pack)
        for i, k in enumerate(gk):
            r0, nr, nc = PACK[k]
            pack[r0:r0 + nr, 0:nc] = g_in[i][...]
        peers = [(x, y, 1 - c)] + [(px, py, pc) for px, py in _chip_peers(x, y) for pc in (c, 1 - c)]
        sends = [_remote(pack, buf.at[me], send_sems.at[k], recv_sems.at[k], dev) for k, dev in enumerate(peers)]
        for cp in sends:
            cp.start()
        buf[me] = pack[...]
        for k, (px, py, pc) in enumerate(peers):
            _remote(pack, buf.at[4 * px + 2 * py + pc], send_sems.at[k], recv_sems.at[k], (px, py, pc)).wait_recv()
        for cp in sends:
            cp.wait_send()
        tot = buf[0]
        for i in range(1, 8):
            tot = tot + buf[i]
        acc[...] = tot
        for i, k in enumerate(gk):
            r0, nr, nc = PACK[k]
            g_out[i][...] = acc[r0:r0 + nr, 0:nc]
        for i, k in enumerate(SMALL_ADAM):
            r0, nr, nc = PACK[k]
            d_out[i][...], m_out[i][...], v_out[i][...] = _adam_math(
                w_in[i][...], acc[r0:r0 + nr, 0:nc], m_in[i][...], v_in[i][...])

    args = [grads[k] for k in gk] + [d[k] for d in (wts, mom, var) for k in SMALL_ADAM]
    shp = lambda k: SDS((PACK[k][1], PACK[k][2]), F32)
    vm = pl.BlockSpec(memory_space=pltpu.VMEM)
    outs = pl.pallas_call(
        body, name=name, in_specs=[vm] * len(args), out_specs=[vm] * (ng + 3 * na),
        out_shape=[shp(k) for k in gk] + [shp(k) for _ in range(3) for k in SMALL_ADAM],
        scratch_shapes=[pltpu.VMEM((PACK_ROWS, PACK_COLS), F32), pltpu.VMEM((8, PACK_ROWS, PACK_COLS), F32),
                        pltpu.VMEM((PACK_ROWS, PACK_COLS), F32)] + _dma_sems(7, 7),
        compiler_params=COMM_PARAMS)(*args)
    red = dict(zip(gk, outs[:ng]))
    parts = [dict(zip(SMALL_ADAM, outs[ng + i * na: ng + (i + 1) * na])) for i in range(3)]
    return red, parts[0], parts[1], parts[2]


def _adamw_cols(w, gfull, m, v, chip, name):
    _, R, C = w.shape

    def body(w_idx, w_ref, g_ref, m_ref, v_ref, go_ref, d_ref, mo_ref, vo_ref):
        go_ref[...] = g_ref[...]
        d_ref[...], mo_ref[...], vo_ref[...] = _adam_math(w_ref[...], g_ref[...], m_ref[...], v_ref[...])

    blk = pl.BlockSpec((None, R, C), lambda i, w_idx: (0, 0, 0))
    spec = pltpu.PrefetchScalarGridSpec(
        num_scalar_prefetch=1, grid=(1,),
        in_specs=[blk, pl.BlockSpec((R, C), lambda i, w_idx: (0, w_idx[0])), blk, blk], out_specs=[blk] * 4)
    return pl.pallas_call(body, name=name, grid_spec=spec, out_shape=[SDS((1, R, C), F32)] * 4,
                          compiler_params=_cp("arbitrary"))(chip, w, gfull, m, v)


def _adamw(w, g, m, v, name):
    _, R, C = w.shape
    half = R // 2
    tr = _tile(half, max(8, (2 ** 17 // C) // 8 * 8), 8)
    nh = half // tr

    def body(w_ref, g_ref, m_ref, v_ref, go_ref, d_ref, mo_ref, vo_ref):
        go_ref[...] = g_ref[...]
        d_ref[...], mo_ref[...], vo_ref[...] = _adam_math(w_ref[...], g_ref[...], m_ref[...], v_ref[...])

    blk = pl.BlockSpec((None, tr, C), lambda i: (0, i, 0))
    gblk = pl.BlockSpec((None, tr, C), lambda i: (i // nh, i % nh, 0))
    return pl.pallas_call(body, name=name, grid=(R // tr,), in_specs=[blk, gblk, blk, blk], out_specs=[blk] * 4,
                          out_shape=[SDS((1, R, C), F32)] * 4, compiler_params=_cp("parallel"))(w, g, m, v)


WEIGHT_NAMES = ["norm_mix_g", "w_in", "ssd_conv_w", "ssd_conv_b", "ssd_dt_bias", "ssd_A_log", "ssd_D", "ssd_norm_g",
                "cf_conv_w", "cf_conv_b", "cf_ln_g", "cf_ln_b", "w_out", "norm_xattn_g", "norm_mem_g", "w_q", "w_kv",
                "w_o", "norm_ffn_g", "w_gate", "w_up", "w_down", "norm_final_g"]
VEC_REF = [("norm_mix_g", "g_mix"), ("norm_xattn_g", "g_xattn"), ("norm_mem_g", "g_mem"), ("norm_ffn_g", "g_ffn"),
           ("norm_final_g", "g_final"), ("ssd_norm_g", "ssd_norm_g"), ("cf_conv_b", "cf_b"), ("cf_ln_g", "ln_g"),
           ("cf_ln_b", "ln_b"), ("ssd_conv_b", "conv4_b")]
SC_REF = ["ssd_dt_bias", "ssd_A_log", "ssd_D"]


def _small_side(get):
    d = {k: get(ref_name).reshape(1, -1) for ref_name, k in VEC_REF}
    d["sc"] = _stack_sc(*[get(n) for n in SC_REF])
    return d


def kernel(x, mem, norm_mix_g, w_in, ssd_conv_w, ssd_conv_b, ssd_dt_bias, ssd_A_log, ssd_D, ssd_norm_g, cf_conv_w, cf_conv_b, cf_ln_g, cf_ln_b, w_out, norm_xattn_g, norm_mem_g, w_q, w_kv, w_o, norm_ffn_g, w_gate, w_up, w_down, norm_final_g, loss_target, m_norm_mix_g, m_w_in, m_ssd_conv_w, m_ssd_conv_b, m_ssd_dt_bias, m_ssd_A_log, m_ssd_D, m_ssd_norm_g, m_cf_conv_w, m_cf_conv_b, m_cf_ln_g, m_cf_ln_b, m_w_out, m_norm_xattn_g, m_norm_mem_g, m_w_q, m_w_kv, m_w_o, m_norm_ffn_g, m_w_gate, m_w_up, m_w_down, m_norm_final_g, v_norm_mix_g, v_w_in, v_ssd_conv_w, v_ssd_conv_b, v_ssd_dt_bias, v_ssd_A_log, v_ssd_D, v_ssd_norm_g, v_cf_conv_w, v_cf_conv_b, v_cf_ln_g, v_cf_ln_b, v_w_out, v_norm_xattn_g, v_norm_mem_g, v_w_q, v_w_kv, v_w_o, v_norm_ffn_g, v_w_gate, v_w_up, v_w_down, v_norm_final_g):
    env = dict(locals())
    view = lambda n, a: a.transpose(0, 2, 1) if n in TRANSPOSED else a
    wts = {n: view(n, env[n]) for n in WEIGHT_NAMES}
    mom = {n: view(n, env["m_" + n]) for n in WEIGHT_NAMES}
    var = {n: view(n, env["v_" + n]) for n in WEIGHT_NAMES}
    chip = (2 * lax.axis_index("x") + lax.axis_index("y")).astype(jnp.int32).reshape(1)
    core = lax.axis_index("c").astype(jnp.int32).reshape(1)
    where = jnp.concatenate([chip, core])
    big = [n for n, _ in BIG]

    w_in_g, conv4_g, cf_g = _allgather_list([w_in[0].astype(BF16), ssd_conv_w[0], cf_conv_w[0]], "allgather_first")
    W = _pack_in(w_in_g)
    P = _small_side(lambda n: wts[n])
    P["conv4_w"], P["cf_w"] = _cat_cols(conv4_g), _cat_cols(cf_g)
    late = {n: wts[n][0].astype(BF16) for grp in AG_RIDE for n in grp}

    loss, grad_x, GW, GP, pair, got = _local_step(x[0], mem[0], loss_target[0], W, P, core, late)
    joined = _pair_join_list([_chip_sum(pair[n], got[n], where, "rs_chip_sum_" + n) for n in big])
    gshard = dict(zip(big, joined))

    small = dict(GP)
    small["loss"] = loss
    red, sd, sm, sv = _small_allreduce_adamw(small, {k: P[k] for k in SMALL_ADAM}, _small_side(lambda n: mom[n]),
                                             _small_side(lambda n: var[n]))
    grads, delta, new_m, new_v = {}, {}, {}, {}
    for ref_name, k in VEC_REF:
        shp = wts[ref_name].shape
        for dst, src in ((grads, red), (delta, sd), (new_m, sm), (new_v, sv)):
            dst[ref_name] = src[k].reshape(shp)
    for row, ref_name in enumerate(SC_REF):
        for dst, src in ((grads, red), (delta, sd), (new_m, sm), (new_v, sv)):
            dst[ref_name] = src["sc"][row:row + 1, :NH]

    for n, k in (("ssd_conv_w", "conv4_w"), ("cf_conv_w", "cf_w")):
        grads[n], delta[n], new_m[n], new_v[n] = _adamw_cols(wts[n], red[k], mom[n], var[n], chip, "adamw_" + n)
    for n in big:
        outs = _adamw(wts[n], gshard[n], mom[n], var[n], "adamw_" + n)
        grads[n], delta[n], new_m[n], new_v[n] = [view(n, o) for o in outs]

    return (red["loss"][0, 0], grad_x[None], *[grads[n] for n in WEIGHT_NAMES], *[delta[n] for n in WEIGHT_NAMES],
            *[new_m[n] for n in WEIGHT_NAMES], *[new_v[n] for n in WEIGHT_NAMES])
```

```python
import functools
import math

import jax
import jax.numpy as jnp
from jax import lax
from jax.experimental import pallas as pl
from jax.experimental.pallas import tpu as pltpu

F32 = jnp.float32
BF16 = jnp.bfloat16
_MXU = BF16

D = 1024
MEM = 256
NH, HP, NG, NS = 16, 64, 2, 128
GW = NH * HP // NG
CH = 128
XBC = NH * HP + 2 * NG * NS
KS, KC = 4, 31
XH, XD = 4, 256
DFF = 2816
EPS = 1e-6
COL_Z, COL_A, COL_G, COL_XBC, COL_DT, MAINW = 0, 1024, 2048, 3072, 4608, 4736
VMEM_LIMIT = 56 * 2 ** 20

ADAM_LR, ADAM_B1, ADAM_B2, ADAM_EPS, ADAM_WD, ADAM_STEP = 0.001, 0.9, 0.999, 1e-08, 0.01, 10

SDS = jax.ShapeDtypeStruct
MESHID = pl.DeviceIdType.MESH


def _cp(*sem):
    return pltpu.CompilerParams(dimension_semantics=sem, vmem_limit_bytes=VMEM_LIMIT)


def _tile(n, cap, unit=128):
    if n <= cap:
        return n
    best = None
    for t in range(unit, cap + 1, unit):
        if n % t == 0:
            best = t
    assert best is not None, (n, cap)
    return best


def _sigmoid(x):
    return 1.0 / (1.0 + jnp.exp(-x))


def _silu(x):
    return x * _sigmoid(x)


def _dsilu(x):
    s = _sigmoid(x)
    return s * (1.0 + x * (1.0 - s))


def _softplus(x):
    return jnp.maximum(x, 0.0) + jnp.log(1.0 + jnp.exp(-jnp.abs(x)))


def _split_bf16(x, passes):
    parts, r = [], x.astype(F32)
    for _ in range(passes):
        p = r.astype(BF16)
        parts.append(p)
        r = r - p.astype(F32)
    return parts


def _dot(a, b, dims=None, exact=None, passes=2):
    dn = {None: (((1,), (0,)), ((), ())), "nt": (((1,), (1,)), ((), ())), "tn": (((0,), (0,)), ((), ()))}[dims]
    if exact is None:
        return lax.dot_general(a.astype(_MXU), b.astype(_MXU), dn, preferred_element_type=F32)
    if exact == "a":
        terms = [(a.astype(BF16), p) for p in _split_bf16(b, passes)]
    else:
        terms = [(p, b.astype(BF16)) for p in _split_bf16(a, passes)]
    out = None
    for lhs, rhs in terms:
        d = lax.dot_general(lhs, rhs, dn, preferred_element_type=F32)
        out = d if out is None else out + d
    return out


def _rms_bwd_tile(xv, gv, dy, dres):
    r = lax.rsqrt(jnp.mean(xv * xv, axis=-1, keepdims=True) + EPS)
    xh = xv * r
    gdy = dy * gv
    dx = r * (gdy - xh * jnp.mean(xh * gdy, axis=-1, keepdims=True))
    return dres + dx, jnp.sum(dy * xh, axis=0, keepdims=True)


def _matmul(kind, a, b, name, add, out_dtype, tm, tw, riders, rms):
    M, K = a.shape
    Wd = b.shape[1] if kind == "nn" else b.shape[0]
    rd = _Riders(riders or ())
    nco = len(rd.arrays())
    nin = 2 + (add is not None) + (3 if rms else 0)
    low = bool(rms and rms[3])
    nout = (2 + low) if rms else 1
    grid = (Wd // tw, M // tm)
    assert not rms or tw == Wd, "the RMSNorm epilogue needs whole rows"

    def body(*refs):
        a_ref, b_ref = refs[0], refs[1]
        outs = refs[nin + nco:nin + nco + nout]
        rd.bind(refs[nin:nin + nco], refs[nin + nco + nout:nin + 2 * nco + nout], refs[nin + 2 * nco + nout:], grid).start()
        acc = _dot(a_ref[...], b_ref[...], None if kind == "nn" else "nt")
        if add is not None:
            acc = acc + refs[2][...]
        if rms:
            x_ref, g_ref, dres_ref = refs[nin - 3:nin]
            tot, dg = _rms_bwd_tile(x_ref[...], g_ref[...], acc, dres_ref[...])

            @pl.when(pl.program_id(1) == 0)
            def _():
                outs[-1][...] = jnp.zeros_like(outs[-1])

            outs[-1][...] += dg
            outs[0][...] = tot
            if low:
                outs[1][...] = tot.astype(outs[1].dtype)
        else:
            outs[0][...] = acc.astype(outs[0].dtype)
        rd.finish()

    tile = pl.BlockSpec((tm, tw), lambda j, i: (i, j))
    bspec = pl.BlockSpec((K, tw), lambda j, i: (0, j)) if kind == "nn" else pl.BlockSpec((tw, K), lambda j, i: (j, 0))
    in_specs, args = [pl.BlockSpec((tm, K), lambda j, i: (i, 0)), bspec], [a, b]
    if add is not None:
        in_specs.append(tile)
        args.append(add)
    if rms:
        vec = pl.BlockSpec((1, tw), lambda j, i: (0, j))
        in_specs += [tile, vec, tile]
        args += [rms[0], rms[1], rms[2]]
        out_specs = [tile] * (1 + low) + [vec]
        out_shape = [SDS((M, Wd), F32)] + ([SDS((M, Wd), _MXU)] if low else []) + [SDS((1, Wd), F32)]
    else:
        out_specs, out_shape = [tile], [SDS((M, Wd), out_dtype)]
    order = ("arbitrary", "arbitrary") if (nco or rms) else ("parallel", "parallel")
    outs = pl.pallas_call(
        body, name=name, grid=grid, in_specs=in_specs + [HBM_SPEC] * nco, out_specs=out_specs + [HBM_SPEC] * nco,
        out_shape=out_shape + rd.out_shapes(), scratch_shapes=rd.scratch(),
        compiler_params=_cp(*order))(*args, *rd.arrays())
    main = tuple(outs[:nout]) if rms else outs[0]
    return main if riders is None else (main, rd.split(outs[nout:]))


def _mm_nn(a, b, name, add=None, out_dtype=F32, tm_cap=1024, tn_cap=1408, riders=None, rms=None):
    tm, tn = _tile(a.shape[0], tm_cap, 8), _tile(b.shape[1], tn_cap)
    return _matmul("nn", a, b, name, add, out_dtype, tm, tn, riders, rms)


def _mm_nt(a, b, name, add=None, out_dtype=F32, tm_cap=512, tk_cap=1024, riders=None, rms=None):
    tm, tk = _tile(a.shape[0], tm_cap, 8), _tile(b.shape[0], tk_cap)
    return _matmul("nt", a, b, name, add, out_dtype, tm, tk, riders, rms)


def _mm_tn(a, b, name, tm_cap=1024, tk_cap=512, tn_cap=1408):
    M, K = a.shape
    _, N = b.shape
    tm, tk, tn = _tile(M, tm_cap, 8), _tile(K, tk_cap), _tile(N, tn_cap)

    def body(a_ref, b_ref, o_ref):
        @pl.when(pl.program_id(2) == 0)
        def _():
            o_ref[...] = jnp.zeros_like(o_ref)

        o_ref[...] += _dot(a_ref[...], b_ref[...], "tn")

    return pl.pallas_call(
        body, name=name, grid=(K // tk, N // tn, M // tm),
        in_specs=[pl.BlockSpec((tm, tk), lambda k, n, m: (m, k)), pl.BlockSpec((tm, tn), lambda k, n, m: (m, n))],
        out_specs=pl.BlockSpec((tk, tn), lambda k, n, m: (k, n)), out_shape=SDS((K, N), F32),
        compiler_params=_cp("parallel", "parallel", "arbitrary"))(a, b)


def _rms_fwd(x, g, name, tb_cap=512):
    S, Dm = x.shape
    tb = _tile(S, tb_cap, 8)

    def body(x_ref, g_ref, o_ref):
        xv = x_ref[...]
        r = lax.rsqrt(jnp.mean(xv * xv, axis=-1, keepdims=True) + EPS)
        o_ref[...] = (xv * r * g_ref[...]).astype(o_ref.dtype)

    return pl.pallas_call(
        body, name=name, grid=(S // tb,),
        in_specs=[pl.BlockSpec((tb, Dm), lambda i: (i, 0)), pl.BlockSpec((1, Dm), lambda i: (0, 0))],
        out_specs=pl.BlockSpec((tb, Dm), lambda i: (i, 0)), out_shape=SDS((S, Dm), _MXU),
        compiler_params=_cp("parallel"))(x, g)


def _rms_bwd(x, g, dh, dres, name, tb_cap=512, low=True):
    S, Dm = x.shape
    tb = _tile(S, tb_cap, 8)
    need_dx = dres is not None

    def body(x_ref, g_ref, dh_ref, *rest):
        dg_ref = rest[-1]
        tot, dg = _rms_bwd_tile(x_ref[...], g_ref[...], dh_ref[...].astype(F32), rest[0][...] if need_dx else 0.0)

        @pl.when(pl.program_id(0) == 0)
        def _():
            dg_ref[...] = jnp.zeros_like(dg_ref)

        dg_ref[...] += dg
        if need_dx:
            rest[1][...] = tot
            if low:
                rest[2][...] = tot.astype(rest[2].dtype)

    row = pl.BlockSpec((tb, Dm), lambda i: (i, 0))
    vec = pl.BlockSpec((1, Dm), lambda i: (0, 0))
    if need_dx:
        outs = [SDS((S, Dm), F32)] + ([SDS((S, Dm), _MXU)] if low else [])
        return pl.pallas_call(
            body, name=name, grid=(S // tb,), in_specs=[row, vec, row, row], out_specs=[row] * len(outs) + [vec],
            out_shape=outs + [SDS((1, Dm), F32)], compiler_params=_cp("arbitrary"))(x, g, dh, dres)
    return pl.pallas_call(
        body, name=name, grid=(S // tb,), in_specs=[row, vec, row], out_specs=vec,
        out_shape=SDS((1, Dm), F32), compiler_params=_cp("arbitrary"))(x, g, dh)


def _final_loss(x, g, tgt, name="final_loss", tb_cap=512):
    S, Dm = x.shape
    tb = _tile(S, tb_cap, 8)

    def body(x_ref, g_ref, t_ref, loss_ref, dx_ref, dxl_ref, dg_ref):
        xv = x_ref[...]
        gv = g_ref[...]
        r = lax.rsqrt(jnp.mean(xv * xv, axis=-1, keepdims=True) + EPS)
        xh = xv * r
        e = xh * gv - t_ref[...]

        @pl.when(pl.program_id(0) == 0)
        def _():
            loss_ref[...] = jnp.zeros_like(loss_ref)
            dg_ref[...] = jnp.zeros_like(dg_ref)

        loss_ref[...] += 0.5 * jnp.sum(jnp.mean(e * e, axis=-1, keepdims=True))
        dy = e * (1.0 / Dm)
        dg_ref[...] += jnp.sum(dy * xh, axis=0, keepdims=True)
        gdy = dy * gv
        dx = r * (gdy - xh * jnp.mean(xh * gdy, axis=-1, keepdims=True))
        dx_ref[...] = dx
        dxl_ref[...] = dx.astype(dxl_ref.dtype)

    row = pl.BlockSpec((tb, Dm), lambda i: (i, 0))
    vec = pl.BlockSpec((1, Dm), lambda i: (0, 0))
    return pl.pallas_call(
        body, name=name, grid=(S // tb,), in_specs=[row, vec, row],
        out_specs=[pl.BlockSpec((1, 128), lambda i: (0, 0)), row, row, vec],
        out_shape=[SDS((1, 128), F32), SDS((S, Dm), F32), SDS((S, Dm), _MXU), SDS((1, Dm), F32)],
        compiler_params=_cp("arbitrary"))(x, g, tgt)


SSD_HALO = 8
CF_HALO = 32

HBM_SPEC = pl.BlockSpec(memory_space=pl.ANY)


def _chip_peers(x, y):
    return [(1 - x, y), (x, 1 - y), (1 - x, 1 - y)]


def _remote(src, dst, send_sem, recv_sem, dev):
    return pltpu.make_async_remote_copy(src_ref=src, dst_ref=dst, send_sem=send_sem, recv_sem=recv_sem,
                                        device_id=dev, device_id_type=MESHID)


def _scatter_copies(srcs, outs, send_sems, recv_sems):
    x, y, c = lax.axis_index("x"), lax.axis_index("y"), lax.axis_index("c")
    me = 2 * x + y
    sends, recvs = [], []
    for i, (s, o) in enumerate(zip(srcs, outs)):
        for k, (px, py) in enumerate(_chip_peers(x, y)):
            j = 3 * i + k
            sends.append(_remote(s.at[2 * px + py], o.at[me], send_sems.at[j], recv_sems.at[j], (px, py, c)))
            recvs.append(_remote(s.at[me], o.at[2 * px + py], send_sems.at[j], recv_sems.at[j], (px, py, c)))
    return sends, recvs


def _pair_copies(srcs, outs, send_sems, recv_sems):
    x, y, c = lax.axis_index("x"), lax.axis_index("y"), lax.axis_index("c")
    sends = [_remote(s.at[j, 1 - c], o.at[j], send_sems.at[4 * i + j], recv_sems.at[4 * i + j], (x, y, 1 - c))
             for i, (s, o) in enumerate(zip(srcs, outs)) for j in range(4)]
    return sends, sends


def _rows_half(ref, rows, h):
    r = rows // 2
    return ref.at[pl.ds(h * r if isinstance(h, int) else pl.multiple_of(h * r, 8), r)]


def _gather_copies(srcs, outs, rows, send_sems, recv_sems):
    x, y, c = lax.axis_index("x"), lax.axis_index("y"), lax.axis_index("c")
    me = 2 * x + y
    sends, recvs = [], []
    for i, (s, o) in enumerate(zip(srcs, outs)):
        mine = _rows_half(s, rows[i], c)
        for k, (px, py) in enumerate(_chip_peers(x, y)):
            j = 3 * i + k
            sends.append(_remote(mine, o.at[me, c], send_sems.at[j], recv_sems.at[j], (px, py, c)))
            recvs.append(_remote(mine, o.at[2 * px + py, c], send_sems.at[j], recv_sems.at[j], (px, py, c)))
    return sends, recvs


def _gather_shapes(shards):
    return [SDS((4, 2, a.shape[0] // 2, a.shape[1]), a.dtype) for a in shards]


class _Rider:
    SEMS_PER_ARRAY = {"exchange": 3, "gather": 3, "pair": 4}

    def __init__(self, kind, arrays):
        self.kind, self.arrays = kind, list(arrays)

    def out_shapes(self):
        if self.kind == "gather":
            return _gather_shapes(self.arrays)
        if self.kind == "pair":
            return [SDS((4,) + a.shape[2:], a.dtype) for a in self.arrays]
        return [SDS(a.shape, a.dtype) for a in self.arrays]

    def scratch(self):
        n = self.SEMS_PER_ARRAY[self.kind] * len(self.arrays)
        return [pltpu.SemaphoreType.DMA((n,)), pltpu.SemaphoreType.DMA((n,))]

    def copies(self, srcs, outs, send_sems, recv_sems):
        if self.kind == "gather":
            return _gather_copies(srcs, outs, [a.shape[0] for a in self.arrays], send_sems, recv_sems)
        if self.kind == "pair":
            return _pair_copies(srcs, outs, send_sems, recv_sems)
        return _scatter_copies(srcs, outs, send_sems, recv_sems)


class _Riders:
    def __init__(self, riders):
        self.given = list(riders)
        self.riders = [r for r in self.given if r.arrays]

    def arrays(self):
        return [a for r in self.riders for a in r.arrays]

    def out_shapes(self):
        return [s for r in self.riders for s in r.out_shapes()]

    def scratch(self):
        return [s for r in self.riders for s in r.scratch()]

    def split(self, outs):
        res, k = [], 0
        for r in self.given:
            res.append(list(outs[k:k + len(r.arrays)]))
            k += len(r.arrays)
        return res

    def bind(self, in_refs, out_refs, sem_refs, steps):
        self.steps = steps if isinstance(steps, tuple) else (steps,)
        self.bound, k = [], 0
        for i, r in enumerate(self.riders):
            n = len(r.arrays)
            self.bound.append((r, in_refs[k:k + n], out_refs[k:k + n], sem_refs[2 * i], sem_refs[2 * i + 1]))
            k += n
        return self

    def _at(self, last):
        hit = None
        for ax, n in enumerate(self.steps):
            here = pl.program_id(ax) == (n - 1 if last else 0)
            hit = here if hit is None else jnp.logical_and(hit, here)
        return hit

    def _copies(self):
        sends, recvs = [], []
        for r, srcs, outs, send_sems, recv_sems in self.bound:
            s, w = r.copies(srcs, outs, send_sems, recv_sems)
            sends += s
            recvs += w
        return sends, recvs

    def start(self):
        if self.riders:
            @pl.when(self._at(last=False))
            def _():
                for cp in self._copies()[0]:
                    cp.start()

    def finish(self):
        if self.riders:
            @pl.when(self._at(last=True))
            def _():
                sends, recvs = self._copies()
                for cp in recvs:
                    cp.wait_recv()
                for cp in sends:
                    cp.wait_send()


def _head_consts():
    e = (lax.broadcasted_iota(jnp.int32, (128, NH * HP), 1) // HP == lax.broadcasted_iota(jnp.int32, (128, NH * HP), 0)).astype(F32)
    et = (lax.broadcasted_iota(jnp.int32, (NH * HP, 128), 0) // HP == lax.broadcasted_iota(jnp.int32, (NH * HP, 128), 1)).astype(F32)
    r = lax.broadcasted_iota(jnp.int32, (CH, CH), 0)
    c = lax.broadcasted_iota(jnp.int32, (CH, CH), 1)
    return e, et, (c <= r), (r <= c)


def _ssd_common(xbc_c, dtr, dtb, alog, e, tril, triu):
    xbc = _silu(xbc_c)
    xs = xbc[:, :NH * HP]
    dt = _softplus(dtr + dtb)
    A = -jnp.exp(alog)
    a = dt * A
    cs = _dot(tril, a, exact="a", passes=3)
    csT = _dot(a, triu, "tn", exact="b", passes=3)
    csL = cs[CH - 1:CH, :]
    wdec = jnp.exp(csL - cs) * dt
    dtE = _dot(dt, e, exact="b")
    ecsE = _dot(jnp.exp(cs), e, exact="b")
    wE = _dot(wdec, e, exact="b")
    eL = jnp.exp(csL)
    return xbc, xs, dt, A, cs, csT, csL, wdec, dtE, ecsE, wE, eL


def _ssd_fwd(proj, cw, cb, sc, norm_g, riders=(), name="ssd_fwd"):
    S = proj.shape[0]
    nc = S // CH
    rd = _Riders(riders)
    nco = len(rd.arrays())

    def body(*refs):
        z_ref, xp_ref, cw_ref, cb_ref, dtr_ref, sc_ref, ng_ref = refs[:7]
        xc_ref, y_ref, yn_ref, hp_ref = refs[7 + nco:11 + nco]
        hst, cext = refs[11 + 2 * nco:13 + 2 * nco]
        rd.bind(refs[7:7 + nco], refs[11 + nco:11 + 2 * nco], refs[13 + 2 * nco:], nc).start()

        @pl.when(pl.program_id(0) == 0)
        def _():
            hst[...] = jnp.zeros_like(hst)
            cext[pl.ds(0, SSD_HALO), :] = jnp.zeros((SSD_HALO, XBC), F32)

        cext[pl.ds(SSD_HALO, CH), :] = xp_ref[...]
        xc = jnp.zeros((CH, XBC), F32) + cb_ref[...]
        for k in range(KS):
            xc = xc + cext[pl.ds(SSD_HALO - (KS - 1) + k, CH), :] * cw_ref[k:k + 1, :]
        xc_ref[...] = xc
        cext[pl.ds(0, SSD_HALO), :] = cext[pl.ds(CH, SSD_HALO), :]

        e, et, tril, triu = _head_consts()
        xbc, xs, dt, A, cs, csT, csL, wdec, dtE, ecsE, wE, eL = _ssd_common(
            xc, dtr_ref[...], sc_ref[0:1, :], sc_ref[1:2, :], e, tril, triu)
        hp_ref[0] = hst[...]
        xd = xs * dtE
        xw = xs * wE
        dE = _dot(jnp.broadcast_to(sc_ref[2:3, :], (8, 128)), e, exact="b", passes=3)[0:1, :]
        eLcol = jnp.sum(et * eL, axis=1, keepdims=True)
        for g in range(NG):
            Bg = xbc[:, NH * HP + g * NS: NH * HP + (g + 1) * NS]
            Cg = xbc[:, NH * HP + NG * NS + g * NS: NH * HP + NG * NS + (g + 1) * NS]
            gs = slice(g * GW, (g + 1) * GW)
            G = _dot(Cg, Bg, "nt")
            hg = hst[gs, :]
            yoff = ecsE[:, gs] * _dot(Cg, hg, "nt")
            hst[gs, :] = eLcol[gs, :] * hg + _dot(xw[:, gs], Bg, "tn")
            for hh in range(NH // NG):
                h = g * (NH // NG) + hh
                hs = slice(h * HP, (h + 1) * HP)
                m = jnp.where(tril, jnp.exp(jnp.where(tril, cs[:, h:h + 1] - csT[h:h + 1, :], 0.0)), 0.0)
                yd = _dot(G * m, xd[:, hs])
                y_ref[:, hs] = yd + yoff[:, hh * HP:(hh + 1) * HP] + dE[:, hs] * xs[:, hs]
        y = y_ref[...]
        yz = y * _silu(z_ref[...])
        for g in range(NG):
            gs = slice(g * GW, (g + 1) * GW)
            yg = yz[:, gs]
            r = lax.rsqrt(jnp.mean(yg * yg, axis=-1, keepdims=True) + EPS)
            yn_ref[:, gs] = (yg * r * ng_ref[:, gs]).astype(yn_ref.dtype)
        rd.finish()

    outs = pl.pallas_call(
        body, name=name, grid=(nc,),
        in_specs=[pl.BlockSpec((CH, D), lambda c: (c, COL_Z // D)),
                  pl.BlockSpec((CH, XBC), lambda c: (c, COL_XBC // XBC)),
                  pl.BlockSpec((KS, XBC), lambda c: (0, 0)),
                  pl.BlockSpec((1, XBC), lambda c: (0, 0)),
                  pl.BlockSpec((CH, 128), lambda c: (c, COL_DT // 128)),
                  pl.BlockSpec((8, 128), lambda c: (0, 0)),
                  pl.BlockSpec((1, D), lambda c: (0, 0))] + [HBM_SPEC] * nco,
        out_specs=[pl.BlockSpec((CH, XBC), lambda c: (c, 0)), pl.BlockSpec((CH, D), lambda c: (c, 0)),
                   pl.BlockSpec((CH, D), lambda c: (c, 0)),
                   pl.BlockSpec((1, NH * HP, NS), lambda c: (c, 0, 0))] + [HBM_SPEC] * nco,
        out_shape=[SDS((S, XBC), F32), SDS((S, D), F32), SDS((S, D), _MXU), SDS((nc, NH * HP, NS), F32)]
        + rd.out_shapes(),
        scratch_shapes=[pltpu.VMEM((NH * HP, NS), F32), pltpu.VMEM((SSD_HALO + CH, XBC), F32)] + rd.scratch(),
        compiler_params=_cp("arbitrary"))(proj, proj, cw, cb, proj, sc, norm_g, *rd.arrays())
    return outs[:4], rd.split(outs[4:])


def _ssd_bwd(dmix, y, proj, xbc_c, hprev, cw, sc, norm_g, riders=(), name="ssd_bwd"):
    S = proj.shape[0]
    nc = S // CH
    rd = _Riders(riders)
    nco = len(rd.arrays())
    rev = lambda c: nc - 1 - c

    def body(*refs):
        dyn_ref, y_ref, z_ref, x_ref, xp_ref, dtr_ref, hp_ref, cw_ref, sc_ref, ng_ref = refs[:10]
        dz_ref, dx_ref, ddtr_ref, gcw_ref, gcb_ref, gsc_ref, gng_ref = refs[10 + nco:17 + nco]
        dh, dxd, cext = refs[17 + 2 * nco:20 + 2 * nco]
        rd.bind(refs[10:10 + nco], refs[17 + nco:17 + 2 * nco], refs[20 + 2 * nco:], nc).start()

        @pl.when(pl.program_id(0) == 0)
        def _():
            dh[...] = jnp.zeros_like(dh)
            cext[pl.ds(CH, SSD_HALO), :] = jnp.zeros((SSD_HALO, XBC), F32)
            gcw_ref[...] = jnp.zeros_like(gcw_ref)
            gcb_ref[...] = jnp.zeros_like(gcb_ref)
            gsc_ref[...] = jnp.zeros_like(gsc_ref)
            gng_ref[...] = jnp.zeros_like(gng_ref)

        e, et, tril, triu = _head_consts()
        xbc_c = x_ref[...]
        dtr = dtr_ref[...]
        dtb = sc_ref[0:1, :]
        xbc, xs, dt, A, cs, csT, csL, wdec, dtE, ecsE, wE, eL = _ssd_common(
            xbc_c, dtr, dtb, sc_ref[1:2, :], e, tril, triu)
        xd = xs * dtE
        xw = xs * wE
        dE = _dot(jnp.broadcast_to(sc_ref[2:3, :], (8, 128)), e, exact="b", passes=3)[0:1, :]
        eLcol = jnp.sum(et * eL, axis=1, keepdims=True)

        yv = y_ref[...]
        zv = z_ref[...]
        sz = _silu(zv)
        yz = yv * sz
        dyn = dyn_ref[...]
        dyz_parts = []
        for g in range(NG):
            gs = slice(g * GW, (g + 1) * GW)
            yg = yz[:, gs]
            r = lax.rsqrt(jnp.mean(yg * yg, axis=-1, keepdims=True) + EPS)
            yh = yg * r
            dn = dyn[:, gs]
            gng_ref[:, gs] += jnp.sum(dn * yh, axis=0, keepdims=True)
            gdn = dn * ng_ref[:, gs]
            dyz_parts.append(r * (gdn - yh * jnp.mean(yh * gdn, axis=-1, keepdims=True)))
        dyz = jnp.concatenate(dyz_parts, axis=1)
        dy = dyz * sz
        dz_ref[...] = (dyz * yv * _dsilu(zv)).astype(dz_ref.dtype)

        dxs = dE * dy
        dzo = ecsE * dy
        dcsL = jnp.zeros((1, 128), F32)
        ddt = jnp.zeros((CH, 128), F32)
        qcols = jnp.zeros((CH, 128), F32)
        qrows = jnp.zeros((128, CH), F32)
        lane = lax.broadcasted_iota(jnp.int32, (1, 128), 1)
        sub = lax.broadcasted_iota(jnp.int32, (128, 1), 0)
        dB_parts, dC_parts, yoff_parts, dxw_parts = [], [], [], []
        for g in range(NG):
            Bg = xbc[:, NH * HP + g * NS: NH * HP + (g + 1) * NS]
            Cg = xbc[:, NH * HP + NG * NS + g * NS: NH * HP + NG * NS + (g + 1) * NS]
            gs = slice(g * GW, (g + 1) * GW)
            hg = hp_ref[0, gs, :]
            dhn = dh[gs, :]
            G = _dot(Cg, Bg, "nt")
            yoff_parts.append(ecsE[:, gs] * _dot(Cg, hg, "nt"))
            dC = _dot(dzo[:, gs], hg)
            dhp = _dot(dzo[:, gs], Cg, "tn") + eLcol[gs, :] * dhn
            t1 = jnp.sum(dhn * hg, axis=1, keepdims=True) * eLcol[gs, :]
            dcsL = dcsL + jnp.sum(et[gs, :] * t1, axis=0, keepdims=True)
            dxw_parts.append(_dot(Bg, dhn, "nt"))
            dB = _dot(xw[:, gs], dhn)
            dgsum = jnp.zeros((CH, CH), F32)
            for hh in range(NH // NG):
                h = g * (NH // NG) + hh
                hs = slice(h * HP, (h + 1) * HP)
                m = jnp.where(tril, jnp.exp(jnp.where(tril, cs[:, h:h + 1] - csT[h:h + 1, :], 0.0)), 0.0)
                sc = G * m
                dyh = dy[:, hs]
                dxd[:, hs] = _dot(sc, dyh, "tn")
                dsc = _dot(dyh, xd[:, hs], "nt")
                q = dsc * sc
                qcols = qcols + jnp.where(lane == h, jnp.sum(q, axis=1, keepdims=True), 0.0)
                qrows = qrows + jnp.where(sub == h, jnp.sum(q, axis=0, keepdims=True), 0.0)
                dgsum = dgsum + dsc * m
            dC_parts.append(dC + _dot(dgsum, Bg))
            dB_parts.append(dB + _dot(dgsum, Cg, "tn"))
            dh[gs, :] = dhp
        yoff = jnp.concatenate(yoff_parts, axis=1)
        dxw = jnp.concatenate(dxw_parts, axis=1)
        dxdv = dxd[...]
        per_head = _dot(jnp.concatenate([dy * yoff, dxw * xs, dxdv * xs, dy * xs], axis=0), et, exact="b")
        dcs = qcols - qrows.T + per_head[0:CH]
        dw = per_head[CH:2 * CH]
        gsc_ref[2:3, :] += jnp.sum(per_head[3 * CH:4 * CH], axis=0, keepdims=True)
        dxs = dxs + wE * dxw + dtE * dxdv
        ddt = ddt + dw * jnp.exp(csL - cs) + per_head[2 * CH:3 * CH]
        dcs = dcs - dw * wdec
        dcsL = dcsL + jnp.sum(dw * wdec, axis=0, keepdims=True)
        last = lax.broadcasted_iota(jnp.int32, (CH, 128), 0) == CH - 1
        dcs = dcs + jnp.where(last, dcsL, 0.0)
        da = _dot(triu, dcs, exact="a", passes=3)
        ddt = ddt + da * A
        gsc_ref[1:2, :] += jnp.sum(da * dt, axis=0, keepdims=True) * A
        valid = lax.broadcasted_iota(jnp.int32, (CH, 128), 1) < NH
        ddtr = jnp.where(valid, ddt * _sigmoid(dtr + dtb), 0.0)
        gsc_ref[0:1, :] += jnp.sum(ddtr, axis=0, keepdims=True)
        ddtr_ref[...] = ddtr.astype(ddtr_ref.dtype)
        dxbc = jnp.concatenate([dxs] + dB_parts + dC_parts, axis=1)
        dxc = dxbc * _dsilu(xbc_c)
        cext[pl.ds(0, CH), :] = dxc
        xp = xp_ref[...]
        acc = jnp.zeros((CH, XBC), F32)
        for k in range(KS):
            sh = cext[pl.ds(KS - 1 - k, CH), :]
            acc = acc + sh * cw_ref[k:k + 1, :]
            gcw_ref[k:k + 1, :] += jnp.sum(xp * sh, axis=0, keepdims=True)
        gcb_ref[...] += jnp.sum(dxc, axis=0, keepdims=True)
        dx_ref[...] = acc.astype(dx_ref.dtype)
        cext[pl.ds(CH, SSD_HALO), :] = cext[pl.ds(0, SSD_HALO), :]
        rd.finish()

    vec = pl.BlockSpec((8, 128), lambda c: (0, 0))
    vecd = pl.BlockSpec((1, D), lambda c: (0, 0))
    cwsp = pl.BlockSpec((KS, XBC), lambda c: (0, 0))
    cbsp = pl.BlockSpec((1, XBC), lambda c: (0, 0))
    row = lambda w, j=0: pl.BlockSpec((CH, w), lambda c: (rev(c), j))
    outs = pl.pallas_call(
        body, name=name, grid=(nc,),
        in_specs=[row(D), row(D), row(D, COL_Z // D), row(XBC), row(XBC, COL_XBC // XBC), row(128, COL_DT // 128),
                  pl.BlockSpec((1, NH * HP, NS), lambda c: (rev(c), 0, 0)), cwsp, vec, vecd] + [HBM_SPEC] * nco,
        out_specs=[row(D), row(XBC), row(128), cwsp, cbsp, vec, vecd] + [HBM_SPEC] * nco,
        out_shape=[SDS((S, D), _MXU), SDS((S, XBC), _MXU), SDS((S, 128), _MXU), SDS((KS, XBC), F32),
                   SDS((1, XBC), F32), SDS((8, 128), F32), SDS((1, D), F32)] + rd.out_shapes(),
        scratch_shapes=[pltpu.VMEM((NH * HP, NS), F32), pltpu.VMEM((CH, NH * HP), F32),
                        pltpu.VMEM((CH + SSD_HALO, XBC), F32)] + rd.scratch(),
        compiler_params=_cp("arbitrary"))(dmix, y, proj, xbc_c, proj, proj, hprev, cw, sc, norm_g, *rd.arrays())
    return outs[:7], rd.split(outs[7:])


CONV_RT = 32


def _fill_phases(ext, ph, rows):
    for s in range(1, 8):
        ph[s - 1, pl.ds(0, rows), :] = ext[pl.ds(s, rows), :]


def _window(ext, ph, off, r0, ls):
    s = off % 8
    src = ext if s == 0 else ph.at[s - 1]
    return src[pl.ds(pl.multiple_of(off - s + r0, 8), CONV_RT), ls]


def _cf_fwd(proj, w, b, lg, lb, riders=(), name="cf_fwd", tb_cap=256):
    S = proj.shape[0]
    tb = _tile(S, tb_cap, 8)
    nb = S // tb
    rd = _Riders(riders)
    nco = len(rd.arrays())

    def body(*refs):
        a_ref, g_ref, w_ref, b_ref, lg_ref, lb_ref = refs[:6]
        u1_ref, u_ref = refs[6 + nco:8 + nco]
        ext, ph = refs[8 + 2 * nco:10 + 2 * nco]
        rd.bind(refs[6:6 + nco], refs[8 + nco:8 + 2 * nco], refs[10 + 2 * nco:], nb).start()

        @pl.when(pl.program_id(0) == 0)
        def _():
            ext[pl.ds(0, CF_HALO), :] = jnp.zeros((CF_HALO, D), F32)

        ext[pl.ds(CF_HALO, tb), :] = a_ref[...] * _sigmoid(g_ref[...])
        _fill_phases(ext, ph, tb + CF_HALO - 8)

        def tile(i, carry):
            r0 = pl.multiple_of(i * CONV_RT, CONV_RT)
            for l in range(D // 128):
                ls = pl.ds(l * 128, 128)
                acc = jnp.broadcast_to(b_ref[:, ls], (CONV_RT, 128))
                for k in range(KC):
                    acc = acc + _window(ext, ph, CF_HALO - (KC - 1) + k, r0, ls) * w_ref[k:k + 1, ls]
                u1_ref[pl.ds(r0, CONV_RT), ls] = acc
            return carry

        lax.fori_loop(0, tb // CONV_RT, tile, 0)
        acc = u1_ref[...]
        mu = jnp.mean(acc, axis=-1, keepdims=True)
        xc = acc - mu
        r = lax.rsqrt(jnp.mean(xc * xc, axis=-1, keepdims=True) + EPS)
        u_ref[...] = _silu(xc * r * lg_ref[...] + lb_ref[...]).astype(u_ref.dtype)
        ext[pl.ds(0, CF_HALO), :] = ext[pl.ds(tb, CF_HALO), :]
        rd.finish()

    vec = pl.BlockSpec((1, D), lambda i: (0, 0))
    outs = pl.pallas_call(
        body, name=name, grid=(nb,),
        in_specs=[pl.BlockSpec((tb, D), lambda i: (i, COL_A // D)), pl.BlockSpec((tb, D), lambda i: (i, COL_G // D)),
                  pl.BlockSpec((KC, D), lambda i: (0, 0)), vec, vec, vec] + [HBM_SPEC] * nco,
        out_specs=[pl.BlockSpec((tb, D), lambda i: (i, 0)), pl.BlockSpec((tb, D), lambda i: (i, 0))] + [HBM_SPEC] * nco,
        out_shape=[SDS((S, D), F32), SDS((S, D), _MXU)] + rd.out_shapes(),
        scratch_shapes=[pltpu.VMEM((CF_HALO + tb, D), F32), pltpu.VMEM((7, tb + CF_HALO - 8, D), F32)] + rd.scratch(),
        compiler_params=_cp("arbitrary"))(proj, proj, w, b, lg, lb, *rd.arrays())
    return outs[:2], rd.split(outs[2:])


def _cf_bwd(dmix, u1, proj, w, lg, lb, riders=(), name="cf_bwd", tb_cap=256):
    S = proj.shape[0]
    tb = _tile(S, tb_cap, 8)
    nb = S // tb
    rd = _Riders(riders)
    nco = len(rd.arrays())
    rev = lambda i: nb - 1 - i

    def body(*refs):
        du_ref, u1_ref, a_ref, g_ref, w_ref, lg_ref, lb_ref = refs[:7]
        da_ref, dg_ref, dw_ref, db_ref, dlg_ref, dlb_ref = refs[7 + nco:13 + nco]
        ext, ph, u0s = refs[13 + 2 * nco:16 + 2 * nco]
        rd.bind(refs[7:7 + nco], refs[13 + nco:13 + 2 * nco], refs[16 + 2 * nco:], nb).start()

        @pl.when(pl.program_id(0) == 0)
        def _():
            ext[pl.ds(tb, CF_HALO), :] = jnp.zeros((CF_HALO, D), F32)
            dw_ref[...] = jnp.zeros_like(dw_ref)
            db_ref[...] = jnp.zeros_like(db_ref)
            dlg_ref[...] = jnp.zeros_like(dlg_ref)
            dlb_ref[...] = jnp.zeros_like(dlb_ref)

        u1 = u1_ref[...]
        mu = jnp.mean(u1, axis=-1, keepdims=True)
        xc = u1 - mu
        r = lax.rsqrt(jnp.mean(xc * xc, axis=-1, keepdims=True) + EPS)
        xh = xc * r
        lgv = lg_ref[...]
        du2 = du_ref[...] * _dsilu(xh * lgv + lb_ref[...])
        dlg_ref[...] += jnp.sum(du2 * xh, axis=0, keepdims=True)
        dlb_ref[...] += jnp.sum(du2, axis=0, keepdims=True)
        gd = du2 * lgv
        du1 = r * (gd - jnp.mean(gd, axis=-1, keepdims=True) - xh * jnp.mean(gd * xh, axis=-1, keepdims=True))
        db_ref[...] += jnp.sum(du1, axis=0, keepdims=True)
        ext[pl.ds(0, tb), :] = du1
        u0s[...] = a_ref[...] * _sigmoid(g_ref[...])
        _fill_phases(ext, ph, tb + CF_HALO - 8)

        for l in range(D // 128):
            ls = pl.ds(l * 128, 128)

            def tile(i, accs, ls=ls):
                r0 = pl.multiple_of(i * CONV_RT, CONV_RT)
                rows = pl.ds(r0, CONV_RT)
                u0t = u0s[rows, ls]
                acc = jnp.zeros((CONV_RT, 128), F32)
                out = []
                for k in range(KC):
                    win = _window(ext, ph, KC - 1 - k, r0, ls)
                    acc = acc + win * w_ref[k:k + 1, ls]
                    p = u0t * win
                    out.append(accs[k] + ((p[0:8] + p[8:16]) + (p[16:24] + p[24:32])))
                sg = _sigmoid(g_ref[rows, ls])
                da_ref[rows, ls] = (acc * sg).astype(da_ref.dtype)
                dg_ref[rows, ls] = (acc * a_ref[rows, ls] * sg * (1.0 - sg)).astype(dg_ref.dtype)
                return tuple(out)

            accs = lax.fori_loop(0, tb // CONV_RT, tile, tuple(jnp.zeros((8, 128), F32) for _ in range(KC)))
            for k in range(KC):
                dw_ref[k:k + 1, ls] += jnp.sum(accs[k], axis=0, keepdims=True)
        ext[pl.ds(tb, CF_HALO), :] = ext[pl.ds(0, CF_HALO), :]
        rd.finish()

    vec = pl.BlockSpec((1, D), lambda i: (0, 0))
    wsp = pl.BlockSpec((KC, D), lambda i: (0, 0))
    row = lambda j=0: pl.BlockSpec((tb, D), lambda i: (rev(i), j))
    outs = pl.pallas_call(
        body, name=name, grid=(nb,),
        in_specs=[row(1), row(), row(COL_A // D), row(COL_G // D), wsp, vec, vec] + [HBM_SPEC] * nco,
        out_specs=[row(), row(), wsp, vec, vec, vec] + [HBM_SPEC] * nco,
        out_shape=[SDS((S, D), _MXU), SDS((S, D), _MXU), SDS((KC, D), F32),
                   SDS((1, D), F32), SDS((1, D), F32), SDS((1, D), F32)] + rd.out_shapes(),
        scratch_shapes=[pltpu.VMEM((tb + CF_HALO, D), F32), pltpu.VMEM((7, tb + CF_HALO - 8, D), F32),
                        pltpu.VMEM((tb, D), F32)] + rd.scratch(),
        compiler_params=_cp("arbitrary"))(dmix, u1, proj, proj, w, lg, lb, *rd.arrays())
    return outs[:6], rd.split(outs[6:])


def _attn_fwd(q, kv, name="attn_fwd", tq_cap=512):
    S = q.shape[0]
    tq = _tile(S, tq_cap, 8)
    scale = XD ** -0.5

    def body(q_ref, kv_ref, o_ref):
        for h in range(XH):
            hs = slice(h * XD, (h + 1) * XD)
            s = _dot(q_ref[:, hs], kv_ref[:, hs], "nt") * scale
            s = s - jnp.max(s, axis=-1, keepdims=True)
            p = jnp.exp(s)
            p = p / jnp.sum(p, axis=-1, keepdims=True)
            o_ref[:, hs] = _dot(p, kv_ref[:, D + h * XD: D + (h + 1) * XD]).astype(o_ref.dtype)

    return pl.pallas_call(
        body, name=name, grid=(S // tq,),
        in_specs=[pl.BlockSpec((tq, D), lambda i: (i, 0)), pl.BlockSpec((MEM, 2 * D), lambda i: (0, 0))],
        out_specs=pl.BlockSpec((tq, D), lambda i: (i, 0)), out_shape=SDS((S, D), _MXU),
        compiler_params=_cp("parallel"))(q, kv)


def _attn_bwd(do, q, kv, riders=(), name="attn_bwd", tq_cap=512):
    S = q.shape[0]
    tq = _tile(S, tq_cap, 8)
    scale = XD ** -0.5
    rd = _Riders(riders)
    nco = len(rd.arrays())

    def body(*refs):
        do_ref, q_ref, kv_ref = refs[:3]
        dq_ref, dkv_ref = refs[3 + nco:5 + nco]
        rd.bind(refs[3:3 + nco], refs[5 + nco:5 + 2 * nco], refs[5 + 2 * nco:], S // tq).start()

        @pl.when(pl.program_id(0) == 0)
        def _():
            dkv_ref[...] = jnp.zeros_like(dkv_ref)

        for h in range(XH):
            hs = slice(h * XD, (h + 1) * XD)
            vs = slice(D + h * XD, D + (h + 1) * XD)
            qh = q_ref[:, hs]
            kh = kv_ref[:, hs]
            s = _dot(qh, kh, "nt") * scale
            s = s - jnp.max(s, axis=-1, keepdims=True)
            p = jnp.exp(s)
            p = p / jnp.sum(p, axis=-1, keepdims=True)
            doh = do_ref[:, hs]
            dp = _dot(doh, kv_ref[:, vs], "nt")
            ds = p * (dp - jnp.sum(dp * p, axis=-1, keepdims=True)) * scale
            dq_ref[:, hs] = _dot(ds, kh).astype(dq_ref.dtype)
            dkv_ref[:, hs] += _dot(ds, qh, "tn")
            dkv_ref[:, vs] += _dot(p, doh, "tn")
        rd.finish()

    outs = pl.pallas_call(
        body, name=name, grid=(S // tq,),
        in_specs=[pl.BlockSpec((tq, D), lambda i: (i, 0)), pl.BlockSpec((tq, D), lambda i: (i, 0)),
                  pl.BlockSpec((MEM, 2 * D), lambda i: (0, 0))] + [HBM_SPEC] * nco,
        out_specs=[pl.BlockSpec((tq, D), lambda i: (i, 0)), pl.BlockSpec((MEM, 2 * D), lambda i: (0, 0))]
        + [HBM_SPEC] * nco,
        out_shape=[SDS((S, D), _MXU), SDS((MEM, 2 * D), F32)] + rd.out_shapes(), scratch_shapes=rd.scratch(),
        compiler_params=_cp("arbitrary"))(do, q, kv, *rd.arrays())
    return outs[:2], rd.split(outs[2:])


def _ffn_in(hf, wg_t, wu_t, name="ffn_in", tm_cap=512, tn_cap=1408):
    S, K = hf.shape
    N = wg_t.shape[0]
    tm, tn = _tile(S, tm_cap, 8), _tile(N, tn_cap)

    def body(a_ref, g_ref, u_ref, act_ref, gt_ref, up_ref):
        a = a_ref[...]
        gt = _dot(a, g_ref[...], "nt")
        up = _dot(a, u_ref[...], "nt")
        act_ref[...] = (_silu(gt) * up).astype(act_ref.dtype)
        gt_ref[...] = gt.astype(gt_ref.dtype)
        up_ref[...] = up.astype(up_ref.dtype)

    wsp = pl.BlockSpec((tn, K), lambda j, i: (j, 0))
    osp = pl.BlockSpec((tm, tn), lambda j, i: (i, j))
    return pl.pallas_call(
        body, name=name, grid=(N // tn, S // tm), in_specs=[pl.BlockSpec((tm, K), lambda j, i: (i, 0)), wsp, wsp],
        out_specs=[osp, osp, osp], out_shape=[SDS((S, N), _MXU)] * 3,
        compiler_params=_cp("parallel", "parallel"))(hf, wg_t, wu_t)


def _ffn_out_bwd(dx, w_down, gt, up, name="ffn_out_dx", tm_cap=512, tk_cap=1408):
    S, N = dx.shape
    K = w_down.shape[0]
    tm, tk = _tile(S, tm_cap, 8), _tile(K, tk_cap)

    def body(a_ref, b_ref, g_ref, u_ref, dg_ref, du_ref):
        d = _dot(a_ref[...], b_ref[...], "nt")
        gt = g_ref[...].astype(F32)
        s = _sigmoid(gt)
        dg_ref[...] = (d * u_ref[...].astype(F32) * (s * (1.0 + gt * (1.0 - s)))).astype(dg_ref.dtype)
        du_ref[...] = (d * gt * s).astype(du_ref.dtype)

    osp = pl.BlockSpec((tm, tk), lambda j, i: (i, j))
    return pl.pallas_call(
        body, name=name, grid=(K // tk, S // tm),
        in_specs=[pl.BlockSpec((tm, N), lambda j, i: (i, 0)), pl.BlockSpec((tk, N), lambda j, i: (j, 0)), osp, osp],
        out_specs=[osp, osp], out_shape=[SDS((S, K), _MXU)] * 2,
        compiler_params=_cp("parallel", "parallel"))(dx, w_down, gt, up)


AG_RIDE = (("w_down",), ("w_out", "w_q", "w_kv", "w_o"), ("w_gate", "w_up"))


def _local_step(x, mem, tgt, W, P, core=None, late=None):
    pair, got = {}, {}
    ride = [[late[n] for n in grp] if late is not None else [] for grp in AG_RIDE]

    def halves(group):
        if core is None:
            return []
        gs = [_shard_grad(n, GW) for n in group]
        return [g.reshape(4, 2, g.shape[1] // 2, g.shape[2]) for g in gs]

    def pair_sums(group, hs, theirs):
        ps = [_pair_sum(h_, t, core, "rs_pair_sum_" + n) for h_, t, n in zip(hs, theirs, group)]
        pair.update(zip(group, ps))
        return ps

    h = _rms_fwd(x, P["g_mix"], "rms_mix")
    proj, (bufs0,) = _mm_nn(h, W["main"], "in_proj", tm_cap=256, tn_cap=MAINW, riders=[_Rider("gather", ride[0])])
    (xbc_c, y, yn, hprev), (bufs1,) = _ssd_fwd(proj, P["conv4_w"], P["conv4_b"], P["sc"], P["ssd_norm_g"],
                                                riders=[_Rider("gather", ride[1])])
    (u1, u), (bufs2,) = _cf_fwd(proj, P["cf_w"], P["cf_b"], P["ln_g"], P["ln_b"], riders=[_Rider("gather", ride[2])])
    if late is not None:
        names = AG_RIDE[0] + AG_RIDE[1] + AG_RIDE[2]
        full = _gather_finish_list(ride[0] + ride[1] + ride[2], bufs0 + bufs1 + bufs2)
        W = dict(W, **_pack_late(dict(zip(names, full))))
    mix = jnp.concatenate([yn, u], axis=1)
    x1 = _mm_nn(mix, W["out"], "out_proj", add=x)
    hq = _rms_fwd(x1, P["g_xattn"], "rms_xattn")
    q = _mm_nn(hq, W["q"], "q_proj")
    mn = _rms_fwd(mem, P["g_mem"], "rms_mem")
    kv = _mm_nn(mn, W["kv"], "kv_proj")
    o = _attn_fwd(q, kv)
    x2 = _mm_nn(o, W["o"], "o_proj", add=x1)
    hf = _rms_fwd(x2, P["g_ffn"], "rms_ffn")
    act, gt, up = _ffn_in(hf, W["gate_t"], W["up_t"])
    x3 = _mm_nn(act, W["down"], "ffn_out", add=x2)
    loss, dx3, dx3b, g_final = _final_loss(x3, P["g_final"], tgt)
    GW, GP = {}, {"g_final": g_final}
    GW["down"] = _mm_tn(act, dx3b, "ffn_out_dw", tk_cap=1408, tn_cap=1024)
    dgt, dup = _ffn_out_bwd(dx3b, W["down"], gt, up)
    dhf = _mm_nn(dgt, W["gate_t"], "ffn_gate_dx", tm_cap=512)
    dx2, dx2b, GP["g_ffn"] = _mm_nn(dup, W["up_t"], "ffn_up_dx", add=dhf, tm_cap=512, tn_cap=D,
                                    rms=(x2, P["g_ffn"], dx3, True))
    GW["gate_t"] = _mm_tn(dgt, hf, "ffn_gate_dw", tk_cap=1408, tn_cap=1024)
    GW["up_t"] = _mm_tn(dup, hf, "ffn_up_dw", tk_cap=1408, tn_cap=1024)
    ffn_halves = halves(RS_GROUPS[0])
    do = _mm_nt(dx2b, W["o"], "o_proj_dx")
    GW["o"] = _mm_tn(o, dx2b, "o_proj_dw")
    (dq, dkv), (ffn_theirs,) = _attn_bwd(do, q, kv, riders=[_Rider("pair", ffn_halves)])
    ffn_pieces = pair_sums(RS_GROUPS[0], ffn_halves, ffn_theirs)
    dx1, dx1b, GP["g_xattn"] = _mm_nt(dq, W["q"], "q_proj_dx", tk_cap=D, rms=(x1, P["g_xattn"], dx2, True))
    GW["q"] = _mm_tn(hq, dq, "q_proj_dw")
    dkvb = dkv.astype(_MXU)
    GW["kv"] = _mm_tn(mn, dkvb, "kv_proj_dw", tm_cap=256)
    dmn = _mm_nt(dkvb, W["kv"], "kv_proj_dx")
    GP["g_mem"] = _rms_bwd(mem, P["g_mem"], dmn, None, "rms_mem_bwd")
    dmix = _mm_nt(dx1b, W["out"], "out_proj_dx")
    GW["out"] = _mm_tn(mix, dx1b, "out_proj_dw", tn_cap=1024)
    attn_halves = halves(RS_GROUPS[1])
    (da, dg, GP["cf_w"], GP["cf_b"], GP["ln_g"], GP["ln_b"]), (came, attn_theirs) = _cf_bwd(
        dmix, u1, proj, P["cf_w"], P["ln_g"], P["ln_b"],
        riders=[_Rider("exchange", ffn_pieces), _Rider("pair", attn_halves)])
    got.update(zip(RS_GROUPS[0], came))
    attn_pieces = pair_sums(RS_GROUPS[1], attn_halves, attn_theirs)
    (dz, dxbc, ddtr, GP["conv4_w"], GP["conv4_b"], GP["sc"], GP["ssd_norm_g"]), (came,) = _ssd_bwd(
        dmix, y, proj, xbc_c, hprev, P["conv4_w"], P["sc"], P["ssd_norm_g"],
        riders=[_Rider("exchange", attn_pieces)])
    got.update(zip(RS_GROUPS[1], came))
    dproj = jnp.concatenate([dz, da, dg, dxbc, ddtr], axis=1)
    GW["main"] = _mm_tn(h, dproj, "in_proj_dw", tm_cap=512, tk_cap=512, tn_cap=MAINW)
    in_halves = halves(RS_GROUPS[2])
    in_pieces = pair_sums(RS_GROUPS[2], in_halves, _pair_split_list(in_halves, "rs_pair_send_w_in")) if in_halves else []
    (grad_x, GP["g_mix"]), (came,) = _mm_nt(dproj, W["main"], "in_proj_dx", tm_cap=256, tk_cap=D,
                                            riders=[_Rider("exchange", in_pieces)], rms=(x, P["g_mix"], dx1, False))
    got.update(zip(RS_GROUPS[2], came))
    if core is None:
        return loss, grad_x, GW, GP
    return loss, grad_x, GW, GP, pair, got


Z_END, XBC_END, DT_END = NH * HP, NH * HP + XBC, NH * HP + XBC + NH


def _pad_to(a, rows=None, cols=None):
    r = 0 if rows is None else rows - a.shape[0]
    c = 0 if cols is None else cols - a.shape[1]
    return jnp.pad(a, ((0, r), (0, c)))


IN_W = DT_END + 2 * D
W_IN_SEGS = [(0, Z_END, "main", COL_Z), (Z_END, XBC_END, "main", COL_XBC), (XBC_END, DT_END, "main", COL_DT),
             (DT_END, DT_END + D, "main", COL_A), (DT_END + D, IN_W, "main", COL_G)]
BIG = [("w_in", True), ("w_out", False), ("w_q", False), ("w_kv", True), ("w_o", False), ("w_gate", False),
       ("w_up", False), ("w_down", False)]
TRANSPOSED = ("w_gate", "w_up")


def _ref_cols(pieces, a, b):
    cw = IN_W // 4
    out = []
    for j in range(4):
        lo, hi = max(a, j * cw), min(b, (j + 1) * cw)
        if lo < hi:
            out.append(pieces[j][:, lo - j * cw:hi - j * cw])
    return out


def _cat_cols(pieces):
    return jnp.concatenate([pieces[j] for j in range(4)], axis=1)


def _pack_in(w_in):
    dt = _ref_cols(w_in, XBC_END, DT_END)
    pad = jnp.zeros((dt[0].shape[0], MAINW - COL_DT - NH), dt[0].dtype)
    main = jnp.concatenate(_ref_cols(w_in, 0, Z_END) + _ref_cols(w_in, DT_END, IN_W) + _ref_cols(w_in, Z_END, XBC_END)
                           + dt + [pad], axis=1)
    return {"main": main}


def _pack_late(pc):
    rows = lambda n: pc[n].reshape(-1, pc[n].shape[-1])
    return {"out": rows("w_out"), "q": rows("w_q"), "kv": _cat_cols(pc["w_kv"]), "o": rows("w_o"),
            "gate_t": rows("w_gate"), "up_t": rows("w_up"), "down": rows("w_down")}


GW_KEY = {"w_gate": "gate_t", "w_up": "up_t", "w_kv": "kv", "w_out": "out", "w_q": "q", "w_o": "o", "w_down": "down"}
RS_GROUPS = (("w_down", "w_gate", "w_up"), ("w_out", "w_q", "w_kv", "w_o"), ("w_in",))


def _shard_grad(name, GW):
    if name == "w_in":
        cw = IN_W // 4
        pieces = []
        for j in range(4):
            parts = []
            for a, b, src, col in W_IN_SEGS:
                lo, hi = max(a, j * cw), min(b, (j + 1) * cw)
                if lo < hi:
                    parts.append(GW[src][:, col + lo - a:col + hi - a])
            pieces.append(jnp.concatenate(parts, axis=1))
        return jnp.stack(pieces)
    g = GW[GW_KEY[name]]
    if dict(BIG)[name]:
        cw = g.shape[1] // 4
        return jnp.stack([g[:, j * cw:(j + 1) * cw] for j in range(4)])
    return g.reshape(4, g.shape[0] // 4, g.shape[1])


def _stack_sc(dt_bias, a_log, d):
    return _pad_to(jnp.concatenate([dt_bias, a_log, d], axis=0), rows=8, cols=128)


COMM_PARAMS = pltpu.CompilerParams(vmem_limit_bytes=VMEM_LIMIT)


def _dma_sems(*counts):
    return [pltpu.SemaphoreType.DMA((n,)) for n in counts]


def _allgather_list(arrs, name):
    n = len(arrs)
    halved = [a.shape[0] % 16 == 0 for a in arrs]
    oshape = [(4, 2, a.shape[0] // 2, a.shape[1]) if h else (4, 1) + a.shape for a, h in zip(arrs, halved)]

    def body(*refs):
        srcs, outs = refs[:n], refs[n:2 * n]
        ici_send, ici_recv, own_send, own_recv, fwd_send, fwd_recv = refs[2 * n:]
        x, y, c = lax.axis_index("x"), lax.axis_index("y"), lax.axis_index("c")
        me = 2 * x + y
        sib = (x, y, 1 - c)
        peers = _chip_peers(x, y)

        def half(i, h):
            r = arrs[i].shape[0] // 2
            if not halved[i]:
                return srcs[i]
            return srcs[i].at[pl.ds(h * r if isinstance(h, int) else pl.multiple_of(h * r, 8), r)]

        ici, own, fwd = [], [], []
        for i in range(n):
            mine_h = c if halved[i] else 0
            for k, (px, py) in enumerate(peers):
                s = 3 * i + k
                ici.append(_remote(half(i, c), outs[i].at[me, mine_h], ici_send.at[s], ici_recv.at[s], (px, py, c)))
            for h in range(2 if halved[i] else 1):
                s = 2 * i + h
                own.append(_remote(half(i, h), outs[i].at[me, h], own_send.at[s], own_recv.at[s], sib))
        for cp in ici + own:
            cp.start()
        for i in range(n):
            if not halved[i]:
                continue
            for k, (px, py) in enumerate(peers):
                s = 3 * i + k
                got = outs[i].at[2 * px + py, c]
                _remote(half(i, c), got, ici_send.at[s], ici_recv.at[s], (px, py, c)).wait_recv()
                f = _remote(got, got, fwd_send.at[s], fwd_recv.at[s], sib)
                f.start()
                fwd.append(f)
        for i in range(n):
            for k, (px, py) in enumerate(peers):
                s = 3 * i + k
                if halved[i]:
                    _remote(half(i, c), outs[i].at[2 * px + py, 1 - c], fwd_send.at[s], fwd_recv.at[s], sib).wait_recv()
                else:
                    _remote(srcs[i], outs[i].at[2 * px + py, 0], ici_send.at[s], ici_recv.at[s], (px, py, c)).wait_recv()
            for h in range(2 if halved[i] else 1):
                s = 2 * i + h
                _remote(half(i, h), outs[i].at[me, h], own_send.at[s], own_recv.at[s], sib).wait_recv()
        for cp in ici + own + fwd:
            cp.wait_send()

    outs = pl.pallas_call(
        body, name=name, in_specs=[HBM_SPEC] * n, out_specs=[HBM_SPEC] * n,
        out_shape=[SDS(s, a.dtype) for s, a in zip(oshape, arrs)],
        scratch_shapes=_dma_sems(3 * n, 3 * n, 2 * n, 2 * n, 3 * n, 3 * n), compiler_params=COMM_PARAMS)(*arrs)
    return [o.reshape((4,) + a.shape) for o, a in zip(outs, arrs)]


def _pair_split_list(gs, name):
    n = len(gs)

    def body(*refs):
        sends, recvs = _pair_copies(refs[:n], refs[n:2 * n], *refs[2 * n:])
        for cp in sends:
            cp.start()
        for cp in recvs:
            cp.wait_recv()
        for cp in sends:
            cp.wait_send()

    return pl.pallas_call(
        body, name=name, in_specs=[HBM_SPEC] * n, out_specs=[HBM_SPEC] * n,
        out_shape=[SDS((4,) + g.shape[2:], g.dtype) for g in gs],
        scratch_shapes=_dma_sems(4 * n, 4 * n), compiler_params=COMM_PARAMS)(*gs)


def _gather_finish_list(shards, bufs, name="allgather_finish"):
    n = len(shards)

    def body(*refs):
        srcs, outs = refs[:n], refs[2 * n:3 * n]
        own_send, own_recv, fwd_send, fwd_recv = refs[3 * n:]
        x, y, c = lax.axis_index("x"), lax.axis_index("y"), lax.axis_index("c")
        me = 2 * x + y
        sib = (x, y, 1 - c)
        sends, recvs = [], []
        for i in range(n):
            for h in range(2):
                own = _remote(_rows_half(srcs[i], shards[i].shape[0], h), outs[i].at[me, h],
                              own_send.at[2 * i + h], own_recv.at[2 * i + h], sib)
                sends.append(own)
                recvs.append(own)
            for k, (px, py) in enumerate(_chip_peers(x, y)):
                got, s = outs[i].at[2 * px + py, c], 3 * i + k
                sends.append(_remote(got, got, fwd_send.at[s], fwd_recv.at[s], sib))
                recvs.append(_remote(got, outs[i].at[2 * px + py, 1 - c], fwd_send.at[s], fwd_recv.at[s], sib))
        for cp in sends:
            cp.start()
        for cp in recvs:
            cp.wait_recv()
        for cp in sends:
            cp.wait_send()

    outs = pl.pallas_call(
        body, name=name, in_specs=[HBM_SPEC] * (2 * n), out_specs=[HBM_SPEC] * n,
        out_shape=[SDS(b.shape, b.dtype) for b in bufs], input_output_aliases={n + i: i for i in range(n)},
        scratch_shapes=_dma_sems(2 * n, 2 * n, 3 * n, 3 * n), compiler_params=COMM_PARAMS)(*shards, *bufs)
    return [o.reshape((4,) + a.shape) for o, a in zip(outs, shards)]


JOIN_SPLIT = 4


def _pair_join_list(bufs, name="rs_pair_join"):
    n = len(bufs)

    def body(*refs):
        outs = refs[n:2 * n]
        send_sems, recv_sems = refs[2 * n:]
        x, y, c = lax.axis_index("x"), lax.axis_index("y"), lax.axis_index("c")
        sib = (x, y, 1 - c)
        sends, recvs = [], []
        for i in range(n):
            rc = bufs[i].shape[1] // JOIN_SPLIT
            for q in range(JOIN_SPLIT):
                k = JOIN_SPLIT * i + q
                rows = pl.ds(q * rc, rc)
                sends.append(_remote(outs[i].at[c, rows], outs[i].at[c, rows], send_sems.at[k], recv_sems.at[k], sib))
                recvs.append(_remote(outs[i].at[c, rows], outs[i].at[1 - c, rows], send_sems.at[k], recv_sems.at[k], sib))
        for cp in sends:
            cp.start()
        for cp in recvs:
            cp.wait_recv()
        for cp in sends:
            cp.wait_send()

    return pl.pallas_call(
        body, name=name, in_specs=[HBM_SPEC] * n, out_specs=[HBM_SPEC] * n,
        out_shape=[SDS(b.shape, b.dtype) for b in bufs], input_output_aliases={i: i for i in range(n)},
        scratch_shapes=_dma_sems(JOIN_SPLIT * n, JOIN_SPLIT * n), compiler_params=COMM_PARAMS)(*bufs)


def _pair_sum(g, theirs, core, name):
    _, _, r, c = g.shape

    def body(core_ref, g_ref, t_ref, o_ref):
        o_ref[...] = (g_ref[...] + t_ref[...]).astype(o_ref.dtype)

    spec = pltpu.PrefetchScalarGridSpec(
        num_scalar_prefetch=1, grid=(4,),
        in_specs=[pl.BlockSpec((None, None, r, c), lambda j, core_ref: (j, core_ref[0], 0, 0)),
                  pl.BlockSpec((None, r, c), lambda j, core_ref: (j, 0, 0))],
        out_specs=pl.BlockSpec((None, r, c), lambda j, core_ref: (j, 0, 0)))
    return pl.pallas_call(body, name=name, grid_spec=spec, out_shape=SDS((4, r, c), BF16),
                          compiler_params=_cp("parallel"))(core, g, theirs)


def _chip_sum(own, got, where, name):
    _, r, c = own.shape
    tr = r // 2

    def body(w_ref, a_ref, b1_ref, b2_ref, b3_ref, o_ref):
        o_ref[...] = ((a_ref[...].astype(F32) + b1_ref[...].astype(F32)) + b2_ref[...].astype(F32)) + b3_ref[...].astype(F32)

    piece = lambda k: pl.BlockSpec((None, tr, c), lambda i, w_ref: ((w_ref[0] + k) % 4, i, 0))
    spec = pltpu.PrefetchScalarGridSpec(
        num_scalar_prefetch=1, grid=(r // tr,), in_specs=[piece(0), piece(1), piece(2), piece(3)],
        out_specs=pl.BlockSpec((None, tr, c), lambda i, w_ref: (w_ref[1], i, 0)))
    return pl.pallas_call(body, name=name, grid_spec=spec, out_shape=SDS((2, r, c), F32),
                          compiler_params=_cp("parallel"))(where, own, got, got, got)


def _adam_math(w, g, m, v):
    bc1 = 1.0 - ADAM_B1 ** ADAM_STEP
    bc2 = 1.0 - ADAM_B2 ** ADAM_STEP
    mn = ADAM_B1 * m + (1.0 - ADAM_B1) * g
    vn = ADAM_B2 * v + (1.0 - ADAM_B2) * (g * g)
    return -ADAM_LR * ((mn / bc1) / (jnp.sqrt(vn / bc2) + ADAM_EPS) + ADAM_WD * w), mn, vn


PACK_COLS = XBC
PACK = {"g_mix": (0, 1, D), "g_xattn": (1, 1, D), "g_mem": (2, 1, D), "g_ffn": (3, 1, D), "g_final": (4, 1, D),
        "ssd_norm_g": (5, 1, D), "cf_b": (6, 1, D), "ln_g": (7, 1, D), "ln_b": (8, 1, D), "conv4_b": (9, 1, XBC),
        "conv4_w": (10, KS, XBC), "sc": (16, 8, 128), "cf_w": (24, KC, D), "loss": (55, 1, 128)}
PACK_ROWS = 56
SMALL_ADAM = ["g_mix", "g_xattn", "g_mem", "g_ffn", "g_final", "ssd_norm_g", "cf_b", "ln_g", "ln_b", "conv4_b", "sc"]


def _small_allreduce_adamw(grads, wts, mom, var, name="allreduce_small"):
    gk = list(PACK)
    ng, na = len(gk), len(SMALL_ADAM)

    def body(*refs):
        g_in = refs[:ng]
        w_in, m_in, v_in = (refs[ng + i * na: ng + (i + 1) * na] for i in range(3))
        o = refs[ng + 3 * na:]
        g_out = o[:ng]
        d_out, m_out, v_out = (o[ng + i * na: ng + (i + 1) * na] for i in range(3))
        pack, pbuf, psum, cbuf, acc, send_sems, recv_sems = o[ng + 3 * na:]
        x, y, c = lax.axis_index("x"), lax.axis_index("y"), lax.axis_index("c")
        me = 2 * x + y
        pack[...] = jnp.zeros_like(pack)
        for i, k in enumerate(gk):
            r0, nr, nc = PACK[k]
            pack[r0:r0 + nr, 0:nc] = g_in[i][...]
        pair = _remote(pack, pbuf.at[c], send_sems.at[0], recv_sems.at[0], (x, y, 1 - c))
        pair.start()
        pbuf[c] = pack[...]
        _remote(pack, pbuf.at[1 - c], send_sems.at[0], recv_sems.at[0], (x, y, 1 - c)).wait_recv()
        pair.wait_send()
        psum[...] = pbuf[0] + pbuf[1]
        peers = _chip_peers(x, y)
        sends = [_remote(psum, cbuf.at[me], send_sems.at[1 + k], recv_sems.at[1 + k], (px, py, c))
                 for k, (px, py) in enumerate(peers)]
        for cp in sends:
            cp.start()
        cbuf[me] = psum[...]
        for k, (px, py) in enumerate(peers):
            _remote(psum, cbuf.at[2 * px + py], send_sems.at[1 + k], recv_sems.at[1 + k], (px, py, c)).wait_recv()
        for cp in sends:
            cp.wait_send()
        acc[...] = (cbuf[0] + cbuf[1]) + (cbuf[2] + cbuf[3])
        for i, k in enumerate(gk):
            r0, nr, nc = PACK[k]
            g_out[i][...] = acc[r0:r0 + nr, 0:nc]
        for i, k in enumerate(SMALL_ADAM):
            r0, nr, nc = PACK[k]
            d_out[i][...], m_out[i][...], v_out[i][...] = _adam_math(
                w_in[i][...], acc[r0:r0 + nr, 0:nc], m_in[i][...], v_in[i][...])

    args = [grads[k] for k in gk] + [d[k] for d in (wts, mom, var) for k in SMALL_ADAM]
    shp = lambda k: SDS((PACK[k][1], PACK[k][2]), F32)
    vm = pl.BlockSpec(memory_space=pltpu.VMEM)
    outs = pl.pallas_call(
        body, name=name, in_specs=[vm] * len(args), out_specs=[vm] * (ng + 3 * na),
        out_shape=[shp(k) for k in gk] + [shp(k) for _ in range(3) for k in SMALL_ADAM],
        scratch_shapes=[pltpu.VMEM((PACK_ROWS, PACK_COLS), F32), pltpu.VMEM((2, PACK_ROWS, PACK_COLS), F32),
                        pltpu.VMEM((PACK_ROWS, PACK_COLS), F32), pltpu.VMEM((4, PACK_ROWS, PACK_COLS), F32),
                        pltpu.VMEM((PACK_ROWS, PACK_COLS), F32)] + _dma_sems(4, 4),
        compiler_params=COMM_PARAMS)(*args)
    red = dict(zip(gk, outs[:ng]))
    parts = [dict(zip(SMALL_ADAM, outs[ng + i * na: ng + (i + 1) * na])) for i in range(3)]
    return red, parts[0], parts[1], parts[2]


def _adamw_cols(w, gfull, m, v, chip, name):
    _, R, C = w.shape

    def body(w_idx, w_ref, g_ref, m_ref, v_ref, go_ref, d_ref, mo_ref, vo_ref):
        go_ref[...] = g_ref[...]
        d_ref[...], mo_ref[...], vo_ref[...] = _adam_math(w_ref[...], g_ref[...], m_ref[...], v_ref[...])

    blk = pl.BlockSpec((None, R, C), lambda i, w_idx: (0, 0, 0))
    spec = pltpu.PrefetchScalarGridSpec(
        num_scalar_prefetch=1, grid=(1,),
        in_specs=[blk, pl.BlockSpec((R, C), lambda i, w_idx: (0, w_idx[0])), blk, blk], out_specs=[blk] * 4)
    return pl.pallas_call(body, name=name, grid_spec=spec, out_shape=[SDS((1, R, C), F32)] * 4,
                          compiler_params=_cp("arbitrary"))(chip, w, gfull, m, v)


def _adamw(w, g, m, v, name):
    _, R, C = w.shape
    half = R // 2
    tr = _tile(half, max(8, (2 ** 17 // C) // 8 * 8), 8)
    nh = half // tr

    def body(w_ref, g_ref, m_ref, v_ref, go_ref, d_ref, mo_ref, vo_ref):
        go_ref[...] = g_ref[...]
        d_ref[...], mo_ref[...], vo_ref[...] = _adam_math(w_ref[...], g_ref[...], m_ref[...], v_ref[...])

    blk = pl.BlockSpec((None, tr, C), lambda i: (0, i, 0))
    gblk = pl.BlockSpec((None, tr, C), lambda i: (i // nh, i % nh, 0))
    return pl.pallas_call(body, name=name, grid=(R // tr,), in_specs=[blk, gblk, blk, blk], out_specs=[blk] * 4,
                          out_shape=[SDS((1, R, C), F32)] * 4, compiler_params=_cp("parallel"))(w, g, m, v)


WEIGHT_NAMES = ["norm_mix_g", "w_in", "ssd_conv_w", "ssd_conv_b", "ssd_dt_bias", "ssd_A_log", "ssd_D", "ssd_norm_g",
                "cf_conv_w", "cf_conv_b", "cf_ln_g", "cf_ln_b", "w_out", "norm_xattn_g", "norm_mem_g", "w_q", "w_kv",
                "w_o", "norm_ffn_g", "w_gate", "w_up", "w_down", "norm_final_g"]
VEC_REF = [("norm_mix_g", "g_mix"), ("norm_xattn_g", "g_xattn"), ("norm_mem_g", "g_mem"), ("norm_ffn_g", "g_ffn"),
           ("norm_final_g", "g_final"), ("ssd_norm_g", "ssd_norm_g"), ("cf_conv_b", "cf_b"), ("cf_ln_g", "ln_g"),
           ("cf_ln_b", "ln_b"), ("ssd_conv_b", "conv4_b")]
SC_REF = ["ssd_dt_bias", "ssd_A_log", "ssd_D"]


def _small_side(get):
    d = {k: get(ref_name).reshape(1, -1) for ref_name, k in VEC_REF}
    d["sc"] = _stack_sc(*[get(n) for n in SC_REF])
    return d


def kernel(x, mem, norm_mix_g, w_in, ssd_conv_w, ssd_conv_b, ssd_dt_bias, ssd_A_log, ssd_D, ssd_norm_g, cf_conv_w, cf_conv_b, cf_ln_g, cf_ln_b, w_out, norm_xattn_g, norm_mem_g, w_q, w_kv, w_o, norm_ffn_g, w_gate, w_up, w_down, norm_final_g, loss_target, m_norm_mix_g, m_w_in, m_ssd_conv_w, m_ssd_conv_b, m_ssd_dt_bias, m_ssd_A_log, m_ssd_D, m_ssd_norm_g, m_cf_conv_w, m_cf_conv_b, m_cf_ln_g, m_cf_ln_b, m_w_out, m_norm_xattn_g, m_norm_mem_g, m_w_q, m_w_kv, m_w_o, m_norm_ffn_g, m_w_gate, m_w_up, m_w_down, m_norm_final_g, v_norm_mix_g, v_w_in, v_ssd_conv_w, v_ssd_conv_b, v_ssd_dt_bias, v_ssd_A_log, v_ssd_D, v_ssd_norm_g, v_cf_conv_w, v_cf_conv_b, v_cf_ln_g, v_cf_ln_b, v_w_out, v_norm_xattn_g, v_norm_mem_g, v_w_q, v_w_kv, v_w_o, v_norm_ffn_g, v_w_gate, v_w_up, v_w_down, v_norm_final_g):
    env = dict(locals())
    view = lambda n, a: a.transpose(0, 2, 1) if n in TRANSPOSED else a
    wts = {n: view(n, env[n]) for n in WEIGHT_NAMES}
    mom = {n: view(n, env["m_" + n]) for n in WEIGHT_NAMES}
    var = {n: view(n, env["v_" + n]) for n in WEIGHT_NAMES}
    chip = (2 * lax.axis_index("x") + lax.axis_index("y")).astype(jnp.int32).reshape(1)
    core = lax.axis_index("c").astype(jnp.int32).reshape(1)
    where = jnp.concatenate([chip, core])
    big = [n for n, _ in BIG]

    w_in_g, conv4_g, cf_g = _allgather_list([w_in[0].astype(BF16), ssd_conv_w[0], cf_conv_w[0]], "allgather_first")
    W = _pack_in(w_in_g)
    P = _small_side(lambda n: wts[n])
    P["conv4_w"], P["cf_w"] = _cat_cols(conv4_g), _cat_cols(cf_g)
    late = {n: wts[n][0].astype(BF16) for grp in AG_RIDE for n in grp}

    loss, grad_x, GW, GP, pair, got = _local_step(x[0], mem[0], loss_target[0], W, P, core, late)
    joined = _pair_join_list([_chip_sum(pair[n], got[n], where, "rs_chip_sum_" + n) for n in big])
    gshard = dict(zip(big, joined))

    small = dict(GP)
    small["loss"] = loss
    red, sd, sm, sv = _small_allreduce_adamw(small, {k: P[k] for k in SMALL_ADAM}, _small_side(lambda n: mom[n]),
                                             _small_side(lambda n: var[n]))
    grads, delta, new_m, new_v = {}, {}, {}, {}
    for ref_name, k in VEC_REF:
        shp = wts[ref_name].shape
        for dst, src in ((grads, red), (delta, sd), (new_m, sm), (new_v, sv)):
            dst[ref_name] = src[k].reshape(shp)
    for row, ref_name in enumerate(SC_REF):
        for dst, src in ((grads, red), (delta, sd), (new_m, sm), (new_v, sv)):
            dst[ref_name] = src["sc"][row:row + 1, :NH]

    for n, k in (("ssd_conv_w", "conv4_w"), ("cf_conv_w", "cf_w")):
        grads[n], delta[n], new_m[n], new_v[n] = _adamw_cols(wts[n], red[k], mom[n], var[n], chip, "adamw_" + n)
    for n in big:
        outs = _adamw(wts[n], gshard[n], mom[n], var[n], "adamw_" + n)
        grads[n], delta[n], new_m[n], new_v[n] = [view(n, o) for o in outs]

    return (red["loss"][0, 0], grad_x[None], *[grads[n] for n in WEIGHT_NAMES], *[delta[n] for n in WEIGHT_NAMES],
            *[new_m[n] for n in WEIGHT_NAMES], *[new_v[n] for n in WEIGHT_NAMES])
```

```python
import functools
import math

import jax
import jax.numpy as jnp
from jax import lax
from jax.experimental import pallas as pl
from jax.experimental.pallas import tpu as pltpu

F32 = jnp.float32
BF16 = jnp.bfloat16
_MXU = BF16

D = 1024
MEM = 256
NH, HP, NG, NS = 16, 64, 2, 128
GW = NH * HP // NG
CH = 128
XBC = NH * HP + 2 * NG * NS
KS, KC = 4, 31
XH, XD = 4, 256
DFF = 2816
EPS = 1e-6
COL_Z, COL_A, COL_G, COL_XBC, COL_DT, MAINW = 0, 1024, 2048, 3072, 4608, 4736
VMEM_LIMIT = 56 * 2 ** 20

ADAM_LR, ADAM_B1, ADAM_B2, ADAM_EPS, ADAM_WD, ADAM_STEP = 0.001, 0.9, 0.999, 1e-08, 0.01, 10

SDS = jax.ShapeDtypeStruct
MESHID = pl.DeviceIdType.MESH


def _cp(*sem):
    return pltpu.CompilerParams(dimension_semantics=sem, vmem_limit_bytes=VMEM_LIMIT)


def _tile(n, cap, unit=128):
    if n <= cap:
        return n
    best = None
    for t in range(unit, cap + 1, unit):
        if n % t == 0:
            best = t
    assert best is not None, (n, cap)
    return best


def _sigmoid(x):
    return 1.0 / (1.0 + jnp.exp(-x))


def _silu(x):
    return x * _sigmoid(x)


def _dsilu(x):
    s = _sigmoid(x)
    return s * (1.0 + x * (1.0 - s))


def _softplus(x):
    return jnp.maximum(x, 0.0) + jnp.log(1.0 + jnp.exp(-jnp.abs(x)))


def _split_bf16(x, passes):
    parts, r = [], x.astype(F32)
    for _ in range(passes):
        p = r.astype(BF16)
        parts.append(p)
        r = r - p.astype(F32)
    return parts


def _dot(a, b, dims=None, exact=None, passes=2):
    dn = {None: (((1,), (0,)), ((), ())), "nt": (((1,), (1,)), ((), ())), "tn": (((0,), (0,)), ((), ()))}[dims]
    if exact is None:
        return lax.dot_general(a.astype(_MXU), b.astype(_MXU), dn, preferred_element_type=F32)
    if exact == "a":
        terms = [(a.astype(BF16), p) for p in _split_bf16(b, passes)]
    else:
        terms = [(p, b.astype(BF16)) for p in _split_bf16(a, passes)]
    out = None
    for lhs, rhs in terms:
        d = lax.dot_general(lhs, rhs, dn, preferred_element_type=F32)
        out = d if out is None else out + d
    return out


def _rms_bwd_tile(xv, gv, dy, dres):
    r = lax.rsqrt(jnp.mean(xv * xv, axis=-1, keepdims=True) + EPS)
    xh = xv * r
    gdy = dy * gv
    dx = r * (gdy - xh * jnp.mean(xh * gdy, axis=-1, keepdims=True))
    return dres + dx, jnp.sum(dy * xh, axis=0, keepdims=True)


def _matmul(kind, a, b, name, add, out_dtype, tm, tw, riders, rms, norm=None, loss=None):
    M, K = a.shape
    Wd = b.shape[1] if kind == "nn" else b.shape[0]
    rd = _Riders(riders or ())
    nco = len(rd.arrays())
    nin = 2 + (add is not None) + (3 if rms else 0) + (norm is not None) + (2 if loss else 0)
    low = bool(rms and rms[3])
    nout = (2 + low) if rms else 2 if norm is not None else 4 if loss else 1
    grid = (Wd // tw, M // tm)
    assert not (rms or loss or norm is not None) or tw == Wd, "the row-wise epilogues need whole rows"

    def body(*refs):
        a_ref, b_ref = refs[0], refs[1]
        outs = refs[nin + nco:nin + nco + nout]
        rd.bind(refs[nin:nin + nco], refs[nin + nco + nout:nin + 2 * nco + nout], refs[nin + 2 * nco + nout:], grid).start()
        acc = _dot(a_ref[...], b_ref[...], None if kind == "nn" else "nt")
        if add is not None:
            acc = acc + refs[2][...]
        if loss:
            lpart, dx, dg = _final_loss_tile(acc, refs[nin - 2][...], refs[nin - 1][...])

            @pl.when(pl.program_id(1) == 0)
            def _():
                outs[0][...] = jnp.zeros_like(outs[0])
                outs[3][...] = jnp.zeros_like(outs[3])

            outs[0][...] += lpart
            outs[1][...] = dx
            outs[2][...] = dx.astype(outs[2].dtype)
            outs[3][...] += dg
        elif norm is not None:
            outs[0][...] = acc.astype(outs[0].dtype)
            r = lax.rsqrt(jnp.mean(acc * acc, axis=-1, keepdims=True) + EPS)
            outs[1][...] = (acc * r * refs[nin - 1][...]).astype(outs[1].dtype)
        elif rms:
            x_ref, g_ref, dres_ref = refs[nin - 3:nin]
            tot, dg = _rms_bwd_tile(x_ref[...], g_ref[...], acc, dres_ref[...])

            @pl.when(pl.program_id(1) == 0)
            def _():
                outs[-1][...] = jnp.zeros_like(outs[-1])

            outs[-1][...] += dg
            outs[0][...] = tot
            if low:
                outs[1][...] = tot.astype(outs[1].dtype)
        else:
            outs[0][...] = acc.astype(outs[0].dtype)
        rd.finish()

    tile = pl.BlockSpec((tm, tw), lambda j, i: (i, j))
    bspec = pl.BlockSpec((K, tw), lambda j, i: (0, j)) if kind == "nn" else pl.BlockSpec((tw, K), lambda j, i: (j, 0))
    in_specs, args = [pl.BlockSpec((tm, K), lambda j, i: (i, 0)), bspec], [a, b]
    if add is not None:
        in_specs.append(tile)
        args.append(add)
    vec = pl.BlockSpec((1, tw), lambda j, i: (0, j))
    if rms:
        in_specs += [tile, vec, tile]
        args += [rms[0], rms[1], rms[2]]
        out_specs = [tile] * (1 + low) + [vec]
        out_shape = [SDS((M, Wd), F32)] + ([SDS((M, Wd), _MXU)] if low else []) + [SDS((1, Wd), F32)]
    elif loss:
        in_specs += [vec, tile]
        args += [loss[0], loss[1]]
        out_specs = [pl.BlockSpec((1, 128), lambda j, i: (0, 0)), tile, tile, vec]
        out_shape = [SDS((1, 128), F32), SDS((M, Wd), F32), SDS((M, Wd), _MXU), SDS((1, Wd), F32)]
    elif norm is not None:
        in_specs.append(vec)
        args.append(norm)
        out_specs, out_shape = [tile, tile], [SDS((M, Wd), out_dtype), SDS((M, Wd), _MXU)]
    else:
        out_specs, out_shape = [tile], [SDS((M, Wd), out_dtype)]
    order = ("arbitrary", "arbitrary") if (nco or rms or loss) else ("parallel", "parallel")
    outs = pl.pallas_call(
        body, name=name, grid=grid, in_specs=in_specs + [HBM_SPEC] * nco, out_specs=out_specs + [HBM_SPEC] * nco,
        out_shape=out_shape + rd.out_shapes(), scratch_shapes=rd.scratch(),
        compiler_params=_cp(*order))(*args, *rd.arrays())
    main = tuple(outs[:nout]) if nout > 1 else outs[0]
    return main if riders is None else (main, rd.split(outs[nout:]))


def _mm_nn(a, b, name, add=None, out_dtype=F32, tm_cap=1024, tn_cap=1408, riders=None, rms=None, norm=None, loss=None):
    tm, tn = _tile(a.shape[0], tm_cap, 8), _tile(b.shape[1], tn_cap)
    return _matmul("nn", a, b, name, add, out_dtype, tm, tn, riders, rms, norm, loss)


def _mm_nt(a, b, name, add=None, out_dtype=F32, tm_cap=512, tk_cap=1024, riders=None, rms=None):
    tm, tk = _tile(a.shape[0], tm_cap, 8), _tile(b.shape[0], tk_cap)
    return _matmul("nt", a, b, name, add, out_dtype, tm, tk, riders, rms)


def _mm_tn(a, b, name, tm_cap=1024, tk_cap=512, tn_cap=1408):
    M, K = a.shape
    _, N = b.shape
    tm, tk, tn = _tile(M, tm_cap, 8), _tile(K, tk_cap), _tile(N, tn_cap)

    def body(a_ref, b_ref, o_ref):
        @pl.when(pl.program_id(2) == 0)
        def _():
            o_ref[...] = jnp.zeros_like(o_ref)

        o_ref[...] += _dot(a_ref[...], b_ref[...], "tn")

    return pl.pallas_call(
        body, name=name, grid=(K // tk, N // tn, M // tm),
        in_specs=[pl.BlockSpec((tm, tk), lambda k, n, m: (m, k)), pl.BlockSpec((tm, tn), lambda k, n, m: (m, n))],
        out_specs=pl.BlockSpec((tk, tn), lambda k, n, m: (k, n)), out_shape=SDS((K, N), F32),
        compiler_params=_cp("parallel", "parallel", "arbitrary"))(a, b)


def _rms_fwd(x, g, name, tb_cap=512):
    S, Dm = x.shape
    tb = _tile(S, tb_cap, 8)

    def body(x_ref, g_ref, o_ref):
        xv = x_ref[...]
        r = lax.rsqrt(jnp.mean(xv * xv, axis=-1, keepdims=True) + EPS)
        o_ref[...] = (xv * r * g_ref[...]).astype(o_ref.dtype)

    return pl.pallas_call(
        body, name=name, grid=(S // tb,),
        in_specs=[pl.BlockSpec((tb, Dm), lambda i: (i, 0)), pl.BlockSpec((1, Dm), lambda i: (0, 0))],
        out_specs=pl.BlockSpec((tb, Dm), lambda i: (i, 0)), out_shape=SDS((S, Dm), _MXU),
        compiler_params=_cp("parallel"))(x, g)


def _rms_bwd(x, g, dh, dres, name, tb_cap=512, low=True):
    S, Dm = x.shape
    tb = _tile(S, tb_cap, 8)
    need_dx = dres is not None

    def body(x_ref, g_ref, dh_ref, *rest):
        dg_ref = rest[-1]
        tot, dg = _rms_bwd_tile(x_ref[...], g_ref[...], dh_ref[...].astype(F32), rest[0][...] if need_dx else 0.0)

        @pl.when(pl.program_id(0) == 0)
        def _():
            dg_ref[...] = jnp.zeros_like(dg_ref)

        dg_ref[...] += dg
        if need_dx:
            rest[1][...] = tot
            if low:
                rest[2][...] = tot.astype(rest[2].dtype)

    row = pl.BlockSpec((tb, Dm), lambda i: (i, 0))
    vec = pl.BlockSpec((1, Dm), lambda i: (0, 0))
    if need_dx:
        outs = [SDS((S, Dm), F32)] + ([SDS((S, Dm), _MXU)] if low else [])
        return pl.pallas_call(
            body, name=name, grid=(S // tb,), in_specs=[row, vec, row, row], out_specs=[row] * len(outs) + [vec],
            out_shape=outs + [SDS((1, Dm), F32)], compiler_params=_cp("arbitrary"))(x, g, dh, dres)
    return pl.pallas_call(
        body, name=name, grid=(S // tb,), in_specs=[row, vec, row], out_specs=vec,
        out_shape=SDS((1, Dm), F32), compiler_params=_cp("arbitrary"))(x, g, dh)


def _final_loss_tile(xv, gv, tv):
    r = lax.rsqrt(jnp.mean(xv * xv, axis=-1, keepdims=True) + EPS)
    xh = xv * r
    e = xh * gv - tv
    dy = e * (1.0 / xv.shape[-1])
    gdy = dy * gv
    dx = r * (gdy - xh * jnp.mean(xh * gdy, axis=-1, keepdims=True))
    return 0.5 * jnp.sum(jnp.mean(e * e, axis=-1, keepdims=True)), dx, jnp.sum(dy * xh, axis=0, keepdims=True)


SSD_HALO = 8
CF_HALO = 32

HBM_SPEC = pl.BlockSpec(memory_space=pl.ANY)


def _chip_peers(x, y):
    return [(1 - x, y), (x, 1 - y), (1 - x, 1 - y)]


def _remote(src, dst, send_sem, recv_sem, dev):
    return pltpu.make_async_remote_copy(src_ref=src, dst_ref=dst, send_sem=send_sem, recv_sem=recv_sem,
                                        device_id=dev, device_id_type=MESHID)


def _scatter_copies(srcs, outs, send_sems, recv_sems):
    x, y, c = lax.axis_index("x"), lax.axis_index("y"), lax.axis_index("c")
    me = 2 * x + y
    sends, recvs = [], []
    for i, (s, o) in enumerate(zip(srcs, outs)):
        for k, (px, py) in enumerate(_chip_peers(x, y)):
            j = 3 * i + k
            sends.append(_remote(s.at[2 * px + py], o.at[me], send_sems.at[j], recv_sems.at[j], (px, py, c)))
            recvs.append(_remote(s.at[me], o.at[2 * px + py], send_sems.at[j], recv_sems.at[j], (px, py, c)))
    return sends, recvs


def _pair_copies(srcs, outs, send_sems, recv_sems):
    x, y, c = lax.axis_index("x"), lax.axis_index("y"), lax.axis_index("c")
    sends = [_remote(s.at[j, 1 - c], o.at[j], send_sems.at[4 * i + j], recv_sems.at[4 * i + j], (x, y, 1 - c))
             for i, (s, o) in enumerate(zip(srcs, outs)) for j in range(4)]
    return sends, sends


def _rows_half(ref, rows, h):
    r = rows // 2
    return ref.at[pl.ds(h * r if isinstance(h, int) else pl.multiple_of(h * r, 8), r)]


def _gather_copies(srcs, outs, rows, send_sems, recv_sems):
    x, y, c = lax.axis_index("x"), lax.axis_index("y"), lax.axis_index("c")
    me = 2 * x + y
    sends, recvs = [], []
    for i, (s, o) in enumerate(zip(srcs, outs)):
        mine = _rows_half(s, rows[i], c)
        for k, (px, py) in enumerate(_chip_peers(x, y)):
            j = 3 * i + k
            sends.append(_remote(mine, o.at[me, c], send_sems.at[j], recv_sems.at[j], (px, py, c)))
            recvs.append(_remote(mine, o.at[2 * px + py, c], send_sems.at[j], recv_sems.at[j], (px, py, c)))
    return sends, recvs


def _gather_shapes(shards):
    return [SDS((4, 2, a.shape[0] // 2, a.shape[1]), a.dtype) for a in shards]


class _Rider:
    SEMS_PER_ARRAY = {"exchange": 3, "gather": 3, "pair": 4}

    def __init__(self, kind, arrays):
        self.kind, self.arrays = kind, list(arrays)

    def out_shapes(self):
        if self.kind == "gather":
            return _gather_shapes(self.arrays)
        if self.kind == "pair":
            return [SDS((4,) + a.shape[2:], a.dtype) for a in self.arrays]
        return [SDS(a.shape, a.dtype) for a in self.arrays]

    def scratch(self):
        n = self.SEMS_PER_ARRAY[self.kind] * len(self.arrays)
        return [pltpu.SemaphoreType.DMA((n,)), pltpu.SemaphoreType.DMA((n,))]

    def copies(self, srcs, outs, send_sems, recv_sems):
        if self.kind == "gather":
            return _gather_copies(srcs, outs, [a.shape[0] for a in self.arrays], send_sems, recv_sems)
        if self.kind == "pair":
            return _pair_copies(srcs, outs, send_sems, recv_sems)
        return _scatter_copies(srcs, outs, send_sems, recv_sems)


class _Riders:
    def __init__(self, riders):
        self.given = list(riders)
        self.riders = [r for r in self.given if r.arrays]

    def arrays(self):
        return [a for r in self.riders for a in r.arrays]

    def out_shapes(self):
        return [s for r in self.riders for s in r.out_shapes()]

    def scratch(self):
        return [s for r in self.riders for s in r.scratch()]

    def split(self, outs):
        res, k = [], 0
        for r in self.given:
            res.append(list(outs[k:k + len(r.arrays)]))
            k += len(r.arrays)
        return res

    def bind(self, in_refs, out_refs, sem_refs, steps):
        self.steps = steps if isinstance(steps, tuple) else (steps,)
        self.bound, k = [], 0
        for i, r in enumerate(self.riders):
            n = len(r.arrays)
            self.bound.append((r, in_refs[k:k + n], out_refs[k:k + n], sem_refs[2 * i], sem_refs[2 * i + 1]))
            k += n
        return self

    def _at(self, last):
        hit = None
        for ax, n in enumerate(self.steps):
            here = pl.program_id(ax) == (n - 1 if last else 0)
            hit = here if hit is None else jnp.logical_and(hit, here)
        return hit

    def _copies(self):
        sends, recvs = [], []
        for r, srcs, outs, send_sems, recv_sems in self.bound:
            s, w = r.copies(srcs, outs, send_sems, recv_sems)
            sends += s
            recvs += w
        return sends, recvs

    def start(self):
        if self.riders:
            @pl.when(self._at(last=False))
            def _():
                for cp in self._copies()[0]:
                    cp.start()

    def finish(self):
        if self.riders:
            @pl.when(self._at(last=True))
            def _():
                sends, recvs = self._copies()
                for cp in recvs:
                    cp.wait_recv()
                for cp in sends:
                    cp.wait_send()


def _head_consts():
    e = (lax.broadcasted_iota(jnp.int32, (128, NH * HP), 1) // HP == lax.broadcasted_iota(jnp.int32, (128, NH * HP), 0)).astype(F32)
    et = (lax.broadcasted_iota(jnp.int32, (NH * HP, 128), 0) // HP == lax.broadcasted_iota(jnp.int32, (NH * HP, 128), 1)).astype(F32)
    r = lax.broadcasted_iota(jnp.int32, (CH, CH), 0)
    c = lax.broadcasted_iota(jnp.int32, (CH, CH), 1)
    return e, et, (c <= r), (r <= c)


def _ssd_common(xbc_c, dtr, dtb, alog, e, tril, triu):
    xbc = _silu(xbc_c)
    xs = xbc[:, :NH * HP]
    dt = _softplus(dtr + dtb)
    A = -jnp.exp(alog)
    a = dt * A
    cs = _dot(tril, a, exact="a", passes=3)
    csT = _dot(a, triu, "tn", exact="b", passes=3)
    csL = cs[CH - 1:CH, :]
    wdec = jnp.exp(csL - cs) * dt
    dtE = _dot(dt, e, exact="b")
    ecsE = _dot(jnp.exp(cs), e, exact="b")
    wE = _dot(wdec, e, exact="b")
    eL = jnp.exp(csL)
    return xbc, xs, dt, A, cs, csT, csL, wdec, dtE, ecsE, wE, eL


def _ssd_fwd(proj, cw, cb, sc, norm_g, riders=(), name="ssd_fwd"):
    S = proj.shape[0]
    nc = S // CH
    rd = _Riders(riders)
    nco = len(rd.arrays())

    def body(*refs):
        z_ref, xp_ref, cw_ref, cb_ref, dtr_ref, sc_ref, ng_ref = refs[:7]
        xc_ref, y_ref, yn_ref, hp_ref = refs[7 + nco:11 + nco]
        hst, cext = refs[11 + 2 * nco:13 + 2 * nco]
        rd.bind(refs[7:7 + nco], refs[11 + nco:11 + 2 * nco], refs[13 + 2 * nco:], nc).start()

        @pl.when(pl.program_id(0) == 0)
        def _():
            hst[...] = jnp.zeros_like(hst)
            cext[pl.ds(0, SSD_HALO), :] = jnp.zeros((SSD_HALO, XBC), F32)

        cext[pl.ds(SSD_HALO, CH), :] = xp_ref[...]
        xc = jnp.zeros((CH, XBC), F32) + cb_ref[...]
        for k in range(KS):
            xc = xc + cext[pl.ds(SSD_HALO - (KS - 1) + k, CH), :] * cw_ref[k:k + 1, :]
        xc_ref[...] = xc
        cext[pl.ds(0, SSD_HALO), :] = cext[pl.ds(CH, SSD_HALO), :]

        e, et, tril, triu = _head_consts()
        xbc, xs, dt, A, cs, csT, csL, wdec, dtE, ecsE, wE, eL = _ssd_common(
            xc, dtr_ref[...], sc_ref[0:1, :], sc_ref[1:2, :], e, tril, triu)
        hp_ref[0] = hst[...]
        xd = xs * dtE
        xw = xs * wE
        dE = _dot(jnp.broadcast_to(sc_ref[2:3, :], (8, 128)), e, exact="b", passes=3)[0:1, :]
        eLcol = jnp.sum(et * eL, axis=1, keepdims=True)
        for g in range(NG):
            Bg = xbc[:, NH * HP + g * NS: NH * HP + (g + 1) * NS]
            Cg = xbc[:, NH * HP + NG * NS + g * NS: NH * HP + NG * NS + (g + 1) * NS]
            gs = slice(g * GW, (g + 1) * GW)
            G = _dot(Cg, Bg, "nt")
            hg = hst[gs, :]
            yoff = ecsE[:, gs] * _dot(Cg, hg, "nt")
            hst[gs, :] = eLcol[gs, :] * hg + _dot(xw[:, gs], Bg, "tn")
            for hh in range(NH // NG):
                h = g * (NH // NG) + hh
                hs = slice(h * HP, (h + 1) * HP)
                m = jnp.where(tril, jnp.exp(jnp.where(tril, cs[:, h:h + 1] - csT[h:h + 1, :], 0.0)), 0.0)
                yd = _dot(G * m, xd[:, hs])
                y_ref[:, hs] = yd + yoff[:, hh * HP:(hh + 1) * HP] + dE[:, hs] * xs[:, hs]
        y = y_ref[...]
        yz = y * _silu(z_ref[...])
        for g in range(NG):
            gs = slice(g * GW, (g + 1) * GW)
            yg = yz[:, gs]
            r = lax.rsqrt(jnp.mean(yg * yg, axis=-1, keepdims=True) + EPS)
            yn_ref[:, gs] = (yg * r * ng_ref[:, gs]).astype(yn_ref.dtype)
        rd.finish()

    outs = pl.pallas_call(
        body, name=name, grid=(nc,),
        in_specs=[pl.BlockSpec((CH, D), lambda c: (c, COL_Z // D)),
                  pl.BlockSpec((CH, XBC), lambda c: (c, COL_XBC // XBC)),
                  pl.BlockSpec((KS, XBC), lambda c: (0, 0)),
                  pl.BlockSpec((1, XBC), lambda c: (0, 0)),
                  pl.BlockSpec((CH, 128), lambda c: (c, COL_DT // 128)),
                  pl.BlockSpec((8, 128), lambda c: (0, 0)),
                  pl.BlockSpec((1, D), lambda c: (0, 0))] + [HBM_SPEC] * nco,
        out_specs=[pl.BlockSpec((CH, XBC), lambda c: (c, 0)), pl.BlockSpec((CH, D), lambda c: (c, 0)),
                   pl.BlockSpec((CH, D), lambda c: (c, 0)),
                   pl.BlockSpec((1, NH * HP, NS), lambda c: (c, 0, 0))] + [HBM_SPEC] * nco,
        out_shape=[SDS((S, XBC), F32), SDS((S, D), F32), SDS((S, D), _MXU), SDS((nc, NH * HP, NS), F32)]
        + rd.out_shapes(),
        scratch_shapes=[pltpu.VMEM((NH * HP, NS), F32), pltpu.VMEM((SSD_HALO + CH, XBC), F32)] + rd.scratch(),
        compiler_params=_cp("arbitrary"))(proj, proj, cw, cb, proj, sc, norm_g, *rd.arrays())
    return outs[:4], rd.split(outs[4:])


def _ssd_bwd(dmix, y, proj, xbc_c, hprev, cw, sc, norm_g, riders=(), name="ssd_bwd"):
    S = proj.shape[0]
    nc = S // CH
    rd = _Riders(riders)
    nco = len(rd.arrays())
    rev = lambda c: nc - 1 - c

    def body(*refs):
        dyn_ref, y_ref, z_ref, x_ref, xp_ref, dtr_ref, hp_ref, cw_ref, sc_ref, ng_ref = refs[:10]
        dz_ref, dx_ref, ddtr_ref, gcw_ref, gcb_ref, gsc_ref, gng_ref = refs[10 + nco:17 + nco]
        dh, dxd, cext = refs[17 + 2 * nco:20 + 2 * nco]
        rd.bind(refs[10:10 + nco], refs[17 + nco:17 + 2 * nco], refs[20 + 2 * nco:], nc).start()

        @pl.when(pl.program_id(0) == 0)
        def _():
            dh[...] = jnp.zeros_like(dh)
            cext[pl.ds(CH, SSD_HALO), :] = jnp.zeros((SSD_HALO, XBC), F32)
            gcw_ref[...] = jnp.zeros_like(gcw_ref)
            gcb_ref[...] = jnp.zeros_like(gcb_ref)
            gsc_ref[...] = jnp.zeros_like(gsc_ref)
            gng_ref[...] = jnp.zeros_like(gng_ref)

        e, et, tril, triu = _head_consts()
        xbc_c = x_ref[...]
        dtr = dtr_ref[...]
        dtb = sc_ref[0:1, :]
        xbc, xs, dt, A, cs, csT, csL, wdec, dtE, ecsE, wE, eL = _ssd_common(
            xbc_c, dtr, dtb, sc_ref[1:2, :], e, tril, triu)
        xd = xs * dtE
        xw = xs * wE
        dE = _dot(jnp.broadcast_to(sc_ref[2:3, :], (8, 128)), e, exact="b", passes=3)[0:1, :]
        eLcol = jnp.sum(et * eL, axis=1, keepdims=True)

        yv = y_ref[...]
        zv = z_ref[...]
        sz = _silu(zv)
        yz = yv * sz
        dyn = dyn_ref[...]
        dyz_parts = []
        for g in range(NG):
            gs = slice(g * GW, (g + 1) * GW)
            yg = yz[:, gs]
            r = lax.rsqrt(jnp.mean(yg * yg, axis=-1, keepdims=True) + EPS)
            yh = yg * r
            dn = dyn[:, gs]
            gng_ref[:, gs] += jnp.sum(dn * yh, axis=0, keepdims=True)
            gdn = dn * ng_ref[:, gs]
            dyz_parts.append(r * (gdn - yh * jnp.mean(yh * gdn, axis=-1, keepdims=True)))
        dyz = jnp.concatenate(dyz_parts, axis=1)
        dy = dyz * sz
        dz_ref[...] = (dyz * yv * _dsilu(zv)).astype(dz_ref.dtype)

        dxs = dE * dy
        dzo = ecsE * dy
        dcsL = jnp.zeros((1, 128), F32)
        ddt = jnp.zeros((CH, 128), F32)
        qcols = jnp.zeros((CH, 128), F32)
        qrows = jnp.zeros((128, CH), F32)
        lane = lax.broadcasted_iota(jnp.int32, (1, 128), 1)
        sub = lax.broadcasted_iota(jnp.int32, (128, 1), 0)
        dB_parts, dC_parts, yoff_parts, dxw_parts = [], [], [], []
        for g in range(NG):
            Bg = xbc[:, NH * HP + g * NS: NH * HP + (g + 1) * NS]
            Cg = xbc[:, NH * HP + NG * NS + g * NS: NH * HP + NG * NS + (g + 1) * NS]
            gs = slice(g * GW, (g + 1) * GW)
            hg = hp_ref[0, gs, :]
            dhn = dh[gs, :]
            G = _dot(Cg, Bg, "nt")
            yoff_parts.append(ecsE[:, gs] * _dot(Cg, hg, "nt"))
            dC = _dot(dzo[:, gs], hg)
            dhp = _dot(dzo[:, gs], Cg, "tn") + eLcol[gs, :] * dhn
            t1 = jnp.sum(dhn * hg, axis=1, keepdims=True) * eLcol[gs, :]
            dcsL = dcsL + jnp.sum(et[gs, :] * t1, axis=0, keepdims=True)
            dxw_parts.append(_dot(Bg, dhn, "nt"))
            dB = _dot(xw[:, gs], dhn)
            dgsum = jnp.zeros((CH, CH), F32)
            for hh in range(NH // NG):
                h = g * (NH // NG) + hh
                hs = slice(h * HP, (h + 1) * HP)
                m = jnp.where(tril, jnp.exp(jnp.where(tril, cs[:, h:h + 1] - csT[h:h + 1, :], 0.0)), 0.0)
                sc = G * m
                dyh = dy[:, hs]
                dxd[:, hs] = _dot(sc, dyh, "tn")
                dsc = _dot(dyh, xd[:, hs], "nt")
                q = dsc * sc
                qcols = qcols + jnp.where(lane == h, jnp.sum(q, axis=1, keepdims=True), 0.0)
                qrows = qrows + jnp.where(sub == h, jnp.sum(q, axis=0, keepdims=True), 0.0)
                dgsum = dgsum + dsc * m
            dC_parts.append(dC + _dot(dgsum, Bg))
            dB_parts.append(dB + _dot(dgsum, Cg, "tn"))
            dh[gs, :] = dhp
        yoff = jnp.concatenate(yoff_parts, axis=1)
        dxw = jnp.concatenate(dxw_parts, axis=1)
        dxdv = dxd[...]
        per_head = _dot(jnp.concatenate([dy * yoff, dxw * xs, dxdv * xs, dy * xs], axis=0), et, exact="b")
        dcs = qcols - qrows.T + per_head[0:CH]
        dw = per_head[CH:2 * CH]
        gsc_ref[2:3, :] += jnp.sum(per_head[3 * CH:4 * CH], axis=0, keepdims=True)
        dxs = dxs + wE * dxw + dtE * dxdv
        ddt = ddt + dw * jnp.exp(csL - cs) + per_head[2 * CH:3 * CH]
        dcs = dcs - dw * wdec
        dcsL = dcsL + jnp.sum(dw * wdec, axis=0, keepdims=True)
        last = lax.broadcasted_iota(jnp.int32, (CH, 128), 0) == CH - 1
        dcs = dcs + jnp.where(last, dcsL, 0.0)
        da = _dot(triu, dcs, exact="a", passes=3)
        ddt = ddt + da * A
        gsc_ref[1:2, :] += jnp.sum(da * dt, axis=0, keepdims=True) * A
        valid = lax.broadcasted_iota(jnp.int32, (CH, 128), 1) < NH
        ddtr = jnp.where(valid, ddt * _sigmoid(dtr + dtb), 0.0)
        gsc_ref[0:1, :] += jnp.sum(ddtr, axis=0, keepdims=True)
        ddtr_ref[...] = ddtr.astype(ddtr_ref.dtype)
        dxbc = jnp.concatenate([dxs] + dB_parts + dC_parts, axis=1)
        dxc = dxbc * _dsilu(xbc_c)
        cext[pl.ds(0, CH), :] = dxc
        xp = xp_ref[...]
        acc = jnp.zeros((CH, XBC), F32)
        for k in range(KS):
            sh = cext[pl.ds(KS - 1 - k, CH), :]
            acc = acc + sh * cw_ref[k:k + 1, :]
            gcw_ref[k:k + 1, :] += jnp.sum(xp * sh, axis=0, keepdims=True)
        gcb_ref[...] += jnp.sum(dxc, axis=0, keepdims=True)
        dx_ref[...] = acc.astype(dx_ref.dtype)
        cext[pl.ds(CH, SSD_HALO), :] = cext[pl.ds(0, SSD_HALO), :]
        rd.finish()

    vec = pl.BlockSpec((8, 128), lambda c: (0, 0))
    vecd = pl.BlockSpec((1, D), lambda c: (0, 0))
    cwsp = pl.BlockSpec((KS, XBC), lambda c: (0, 0))
    cbsp = pl.BlockSpec((1, XBC), lambda c: (0, 0))
    row = lambda w, j=0: pl.BlockSpec((CH, w), lambda c: (rev(c), j))
    outs = pl.pallas_call(
        body, name=name, grid=(nc,),
        in_specs=[row(D), row(D), row(D, COL_Z // D), row(XBC), row(XBC, COL_XBC // XBC), row(128, COL_DT // 128),
                  pl.BlockSpec((1, NH * HP, NS), lambda c: (rev(c), 0, 0)), cwsp, vec, vecd] + [HBM_SPEC] * nco,
        out_specs=[row(D), row(XBC), row(128), cwsp, cbsp, vec, vecd] + [HBM_SPEC] * nco,
        out_shape=[SDS((S, D), _MXU), SDS((S, XBC), _MXU), SDS((S, 128), _MXU), SDS((KS, XBC), F32),
                   SDS((1, XBC), F32), SDS((8, 128), F32), SDS((1, D), F32)] + rd.out_shapes(),
        scratch_shapes=[pltpu.VMEM((NH * HP, NS), F32), pltpu.VMEM((CH, NH * HP), F32),
                        pltpu.VMEM((CH + SSD_HALO, XBC), F32)] + rd.scratch(),
        compiler_params=_cp("arbitrary"))(dmix, y, proj, xbc_c, proj, proj, hprev, cw, sc, norm_g, *rd.arrays())
    return outs[:7], rd.split(outs[7:])


CONV_RT = 32


def _fill_phases(ext, ph, rows):
    for s in range(1, 8):
        ph[s - 1, pl.ds(0, rows), :] = ext[pl.ds(s, rows), :]


def _window(ext, ph, off, r0, ls):
    s = off % 8
    src = ext if s == 0 else ph.at[s - 1]
    return src[pl.ds(pl.multiple_of(off - s + r0, 8), CONV_RT), ls]


def _cf_fwd(proj, w, b, lg, lb, riders=(), name="cf_fwd", tb_cap=256):
    S = proj.shape[0]
    tb = _tile(S, tb_cap, 8)
    nb = S // tb
    rd = _Riders(riders)
    nco = len(rd.arrays())

    def body(*refs):
        a_ref, g_ref, w_ref, b_ref, lg_ref, lb_ref = refs[:6]
        u1_ref, u_ref = refs[6 + nco:8 + nco]
        ext, ph = refs[8 + 2 * nco:10 + 2 * nco]
        rd.bind(refs[6:6 + nco], refs[8 + nco:8 + 2 * nco], refs[10 + 2 * nco:], nb).start()

        @pl.when(pl.program_id(0) == 0)
        def _():
            ext[pl.ds(0, CF_HALO), :] = jnp.zeros((CF_HALO, D), F32)

        ext[pl.ds(CF_HALO, tb), :] = a_ref[...] * _sigmoid(g_ref[...])
        _fill_phases(ext, ph, tb + CF_HALO - 8)

        def tile(i, carry):
            r0 = pl.multiple_of(i * CONV_RT, CONV_RT)
            for l in range(D // 128):
                ls = pl.ds(l * 128, 128)
                acc = jnp.broadcast_to(b_ref[:, ls], (CONV_RT, 128))
                for k in range(KC):
                    acc = acc + _window(ext, ph, CF_HALO - (KC - 1) + k, r0, ls) * w_ref[k:k + 1, ls]
                u1_ref[pl.ds(r0, CONV_RT), ls] = acc
            return carry

        lax.fori_loop(0, tb // CONV_RT, tile, 0)
        acc = u1_ref[...]
        mu = jnp.mean(acc, axis=-1, keepdims=True)
        xc = acc - mu
        r = lax.rsqrt(jnp.mean(xc * xc, axis=-1, keepdims=True) + EPS)
        u_ref[...] = _silu(xc * r * lg_ref[...] + lb_ref[...]).astype(u_ref.dtype)
        ext[pl.ds(0, CF_HALO), :] = ext[pl.ds(tb, CF_HALO), :]
        rd.finish()

    vec = pl.BlockSpec((1, D), lambda i: (0, 0))
    outs = pl.pallas_call(
        body, name=name, grid=(nb,),
        in_specs=[pl.BlockSpec((tb, D), lambda i: (i, COL_A // D)), pl.BlockSpec((tb, D), lambda i: (i, COL_G // D)),
                  pl.BlockSpec((KC, D), lambda i: (0, 0)), vec, vec, vec] + [HBM_SPEC] * nco,
        out_specs=[pl.BlockSpec((tb, D), lambda i: (i, 0)), pl.BlockSpec((tb, D), lambda i: (i, 0))] + [HBM_SPEC] * nco,
        out_shape=[SDS((S, D), F32), SDS((S, D), _MXU)] + rd.out_shapes(),
        scratch_shapes=[pltpu.VMEM((CF_HALO + tb, D), F32), pltpu.VMEM((7, tb + CF_HALO - 8, D), F32)] + rd.scratch(),
        compiler_params=_cp("arbitrary"))(proj, proj, w, b, lg, lb, *rd.arrays())
    return outs[:2], rd.split(outs[2:])


def _cf_bwd(dmix, u1, proj, w, lg, lb, riders=(), name="cf_bwd", tb_cap=256):
    S = proj.shape[0]
    tb = _tile(S, tb_cap, 8)
    nb = S // tb
    rd = _Riders(riders)
    nco = len(rd.arrays())
    rev = lambda i: nb - 1 - i

    def body(*refs):
        du_ref, u1_ref, a_ref, g_ref, w_ref, lg_ref, lb_ref = refs[:7]
        da_ref, dg_ref, dw_ref, db_ref, dlg_ref, dlb_ref = refs[7 + nco:13 + nco]
        ext, ph, u0s = refs[13 + 2 * nco:16 + 2 * nco]
        rd.bind(refs[7:7 + nco], refs[13 + nco:13 + 2 * nco], refs[16 + 2 * nco:], nb).start()

        @pl.when(pl.program_id(0) == 0)
        def _():
            ext[pl.ds(tb, CF_HALO), :] = jnp.zeros((CF_HALO, D), F32)
            dw_ref[...] = jnp.zeros_like(dw_ref)
            db_ref[...] = jnp.zeros_like(db_ref)
            dlg_ref[...] = jnp.zeros_like(dlg_ref)
            dlb_ref[...] = jnp.zeros_like(dlb_ref)

        u1 = u1_ref[...]
        mu = jnp.mean(u1, axis=-1, keepdims=True)
        xc = u1 - mu
        r = lax.rsqrt(jnp.mean(xc * xc, axis=-1, keepdims=True) + EPS)
        xh = xc * r
        lgv = lg_ref[...]
        du2 = du_ref[...] * _dsilu(xh * lgv + lb_ref[...])
        dlg_ref[...] += jnp.sum(du2 * xh, axis=0, keepdims=True)
        dlb_ref[...] += jnp.sum(du2, axis=0, keepdims=True)
        gd = du2 * lgv
        du1 = r * (gd - jnp.mean(gd, axis=-1, keepdims=True) - xh * jnp.mean(gd * xh, axis=-1, keepdims=True))
        db_ref[...] += jnp.sum(du1, axis=0, keepdims=True)
        ext[pl.ds(0, tb), :] = du1
        u0s[...] = a_ref[...] * _sigmoid(g_ref[...])
        _fill_phases(ext, ph, tb + CF_HALO - 8)

        for l in range(D // 128):
            ls = pl.ds(l * 128, 128)

            def tile(i, accs, ls=ls):
                r0 = pl.multiple_of(i * CONV_RT, CONV_RT)
                rows = pl.ds(r0, CONV_RT)
                u0t = u0s[rows, ls]
                acc = jnp.zeros((CONV_RT, 128), F32)
                out = []
                for k in range(KC):
                    win = _window(ext, ph, KC - 1 - k, r0, ls)
                    acc = acc + win * w_ref[k:k + 1, ls]
                    p = u0t * win
                    out.append(accs[k] + ((p[0:8] + p[8:16]) + (p[16:24] + p[24:32])))
                sg = _sigmoid(g_ref[rows, ls])
                da_ref[rows, ls] = (acc * sg).astype(da_ref.dtype)
                dg_ref[rows, ls] = (acc * a_ref[rows, ls] * sg * (1.0 - sg)).astype(dg_ref.dtype)
                return tuple(out)

            accs = lax.fori_loop(0, tb // CONV_RT, tile, tuple(jnp.zeros((8, 128), F32) for _ in range(KC)))
            for k in range(KC):
                dw_ref[k:k + 1, ls] += jnp.sum(accs[k], axis=0, keepdims=True)
        ext[pl.ds(tb, CF_HALO), :] = ext[pl.ds(0, CF_HALO), :]
        rd.finish()

    vec = pl.BlockSpec((1, D), lambda i: (0, 0))
    wsp = pl.BlockSpec((KC, D), lambda i: (0, 0))
    row = lambda j=0: pl.BlockSpec((tb, D), lambda i: (rev(i), j))
    outs = pl.pallas_call(
        body, name=name, grid=(nb,),
        in_specs=[row(1), row(), row(COL_A // D), row(COL_G // D), wsp, vec, vec] + [HBM_SPEC] * nco,
        out_specs=[row(), row(), wsp, vec, vec, vec] + [HBM_SPEC] * nco,
        out_shape=[SDS((S, D), _MXU), SDS((S, D), _MXU), SDS((KC, D), F32),
                   SDS((1, D), F32), SDS((1, D), F32), SDS((1, D), F32)] + rd.out_shapes(),
        scratch_shapes=[pltpu.VMEM((tb + CF_HALO, D), F32), pltpu.VMEM((7, tb + CF_HALO - 8, D), F32),
                        pltpu.VMEM((tb, D), F32)] + rd.scratch(),
        compiler_params=_cp("arbitrary"))(dmix, u1, proj, proj, w, lg, lb, *rd.arrays())
    return outs[:6], rd.split(outs[6:])


def _attn_fwd(q, kv, name="attn_fwd", tq_cap=512):
    S = q.shape[0]
    tq = _tile(S, tq_cap, 8)
    scale = XD ** -0.5

    def body(q_ref, kv_ref, o_ref):
        for h in range(XH):
            hs = slice(h * XD, (h + 1) * XD)
            s = _dot(q_ref[:, hs], kv_ref[:, hs], "nt") * scale
            s = s - jnp.max(s, axis=-1, keepdims=True)
            p = jnp.exp(s)
            p = p / jnp.sum(p, axis=-1, keepdims=True)
            o_ref[:, hs] = _dot(p, kv_ref[:, D + h * XD: D + (h + 1) * XD]).astype(o_ref.dtype)

    return pl.pallas_call(
        body, name=name, grid=(S // tq,),
        in_specs=[pl.BlockSpec((tq, D), lambda i: (i, 0)), pl.BlockSpec((MEM, 2 * D), lambda i: (0, 0))],
        out_specs=pl.BlockSpec((tq, D), lambda i: (i, 0)), out_shape=SDS((S, D), _MXU),
        compiler_params=_cp("parallel"))(q, kv)


def _attn_bwd(do, q, kv, riders=(), name="attn_bwd", tq_cap=512):
    S = q.shape[0]
    tq = _tile(S, tq_cap, 8)
    scale = XD ** -0.5
    rd = _Riders(riders)
    nco = len(rd.arrays())

    def body(*refs):
        do_ref, q_ref, kv_ref = refs[:3]
        dq_ref, dkv_ref = refs[3 + nco:5 + nco]
        rd.bind(refs[3:3 + nco], refs[5 + nco:5 + 2 * nco], refs[5 + 2 * nco:], S // tq).start()

        @pl.when(pl.program_id(0) == 0)
        def _():
            dkv_ref[...] = jnp.zeros_like(dkv_ref)

        for h in range(XH):
            hs = slice(h * XD, (h + 1) * XD)
            vs = slice(D + h * XD, D + (h + 1) * XD)
            qh = q_ref[:, hs]
            kh = kv_ref[:, hs]
            s = _dot(qh, kh, "nt") * scale
            s = s - jnp.max(s, axis=-1, keepdims=True)
            p = jnp.exp(s)
            p = p / jnp.sum(p, axis=-1, keepdims=True)
            doh = do_ref[:, hs]
            dp = _dot(doh, kv_ref[:, vs], "nt")
            ds = p * (dp - jnp.sum(dp * p, axis=-1, keepdims=True)) * scale
            dq_ref[:, hs] = _dot(ds, kh).astype(dq_ref.dtype)
            dkv_ref[:, hs] += _dot(ds, qh, "tn")
            dkv_ref[:, vs] += _dot(p, doh, "tn")
        rd.finish()

    outs = pl.pallas_call(
        body, name=name, grid=(S // tq,),
        in_specs=[pl.BlockSpec((tq, D), lambda i: (i, 0)), pl.BlockSpec((tq, D), lambda i: (i, 0)),
                  pl.BlockSpec((MEM, 2 * D), lambda i: (0, 0))] + [HBM_SPEC] * nco,
        out_specs=[pl.BlockSpec((tq, D), lambda i: (i, 0)), pl.BlockSpec((MEM, 2 * D), lambda i: (0, 0))]
        + [HBM_SPEC] * nco,
        out_shape=[SDS((S, D), _MXU), SDS((MEM, 2 * D), F32)] + rd.out_shapes(), scratch_shapes=rd.scratch(),
        compiler_params=_cp("arbitrary"))(do, q, kv, *rd.arrays())
    return outs[:2], rd.split(outs[2:])


def _ffn_in(hf, wg_t, wu_t, name="ffn_in", tm_cap=512, tn_cap=1408):
    S, K = hf.shape
    N = wg_t.shape[0]
    tm, tn = _tile(S, tm_cap, 8), _tile(N, tn_cap)

    def body(a_ref, g_ref, u_ref, act_ref, gt_ref, up_ref):
        a = a_ref[...]
        gt = _dot(a, g_ref[...], "nt")
        up = _dot(a, u_ref[...], "nt")
        act_ref[...] = (_silu(gt) * up).astype(act_ref.dtype)
        gt_ref[...] = gt.astype(gt_ref.dtype)
        up_ref[...] = up.astype(up_ref.dtype)

    wsp = pl.BlockSpec((tn, K), lambda j, i: (j, 0))
    osp = pl.BlockSpec((tm, tn), lambda j, i: (i, j))
    return pl.pallas_call(
        body, name=name, grid=(N // tn, S // tm), in_specs=[pl.BlockSpec((tm, K), lambda j, i: (i, 0)), wsp, wsp],
        out_specs=[osp, osp, osp], out_shape=[SDS((S, N), _MXU)] * 3,
        compiler_params=_cp("parallel", "parallel"))(hf, wg_t, wu_t)


def _ffn_out_bwd(dx, w_down, gt, up, name="ffn_out_dx", tm_cap=512, tk_cap=1408):
    S, N = dx.shape
    K = w_down.shape[0]
    tm, tk = _tile(S, tm_cap, 8), _tile(K, tk_cap)

    def body(a_ref, b_ref, g_ref, u_ref, dg_ref, du_ref):
        d = _dot(a_ref[...], b_ref[...], "nt")
        gt = g_ref[...].astype(F32)
        s = _sigmoid(gt)
        dg_ref[...] = (d * u_ref[...].astype(F32) * (s * (1.0 + gt * (1.0 - s)))).astype(dg_ref.dtype)
        du_ref[...] = (d * gt * s).astype(du_ref.dtype)

    osp = pl.BlockSpec((tm, tk), lambda j, i: (i, j))
    return pl.pallas_call(
        body, name=name, grid=(K // tk, S // tm),
        in_specs=[pl.BlockSpec((tm, N), lambda j, i: (i, 0)), pl.BlockSpec((tk, N), lambda j, i: (j, 0)), osp, osp],
        out_specs=[osp, osp], out_shape=[SDS((S, K), _MXU)] * 2,
        compiler_params=_cp("parallel", "parallel"))(dx, w_down, gt, up)


AG_RIDE = (("w_down",), ("w_out", "w_q", "w_kv", "w_o"), ("w_gate", "w_up"))


def _local_step(x, mem, tgt, W, P, core=None, late=None):
    pair, got = {}, {}
    ride = [[late[n] for n in grp] if late is not None else [] for grp in AG_RIDE]

    def halves(group):
        if core is None:
            return []
        gs = [_shard_grad(n, GW) for n in group]
        return [g.reshape(4, 2, g.shape[1] // 2, g.shape[2]) for g in gs]

    def pair_sums(group, hs, theirs):
        ps = [_pair_sum(h_, t, core, "rs_pair_sum_" + n) for h_, t, n in zip(hs, theirs, group)]
        pair.update(zip(group, ps))
        return ps

    h = _rms_fwd(x, P["g_mix"], "rms_mix")
    proj, (bufs0,) = _mm_nn(h, W["main"], "in_proj", tm_cap=256, tn_cap=MAINW, riders=[_Rider("gather", ride[0])])
    (xbc_c, y, yn, hprev), (bufs1,) = _ssd_fwd(proj, P["conv4_w"], P["conv4_b"], P["sc"], P["ssd_norm_g"],
                                                riders=[_Rider("gather", ride[1])])
    (u1, u), (bufs2,) = _cf_fwd(proj, P["cf_w"], P["cf_b"], P["ln_g"], P["ln_b"], riders=[_Rider("gather", ride[2])])
    if late is not None:
        names = AG_RIDE[0] + AG_RIDE[1] + AG_RIDE[2]
        full = _gather_finish_list(ride[0] + ride[1] + ride[2], bufs0 + bufs1 + bufs2)
        W = dict(W, **_pack_late(dict(zip(names, full))))
    mix = jnp.concatenate([yn, u], axis=1)
    x1, hq = _mm_nn(mix, W["out"], "out_proj", add=x, tm_cap=512, tn_cap=D, norm=P["g_xattn"])
    q = _mm_nn(hq, W["q"], "q_proj")
    mn = _rms_fwd(mem, P["g_mem"], "rms_mem")
    kv = _mm_nn(mn, W["kv"], "kv_proj")
    o = _attn_fwd(q, kv)
    x2, hf = _mm_nn(o, W["o"], "o_proj", add=x1, tm_cap=512, tn_cap=D, norm=P["g_ffn"])
    act, gt, up = _ffn_in(hf, W["gate_t"], W["up_t"])
    loss, dx3, dx3b, g_final = _mm_nn(act, W["down"], "ffn_out", add=x2, tm_cap=512, tn_cap=D,
                                      loss=(P["g_final"], tgt))
    GW, GP = {}, {"g_final": g_final}
    GW["down"] = _mm_tn(act, dx3b, "ffn_out_dw", tk_cap=1408, tn_cap=1024)
    dgt, dup = _ffn_out_bwd(dx3b, W["down"], gt, up)
    dhf = _mm_nn(dgt, W["gate_t"], "ffn_gate_dx", tm_cap=512)
    dx2, dx2b, GP["g_ffn"] = _mm_nn(dup, W["up_t"], "ffn_up_dx", add=dhf, tm_cap=512, tn_cap=D,
                                    rms=(x2, P["g_ffn"], dx3, True))
    GW["gate_t"] = _mm_tn(dgt, hf, "ffn_gate_dw", tk_cap=1408, tn_cap=1024)
    GW["up_t"] = _mm_tn(dup, hf, "ffn_up_dw", tk_cap=1408, tn_cap=1024)
    ffn_halves = halves(RS_GROUPS[0])
    do = _mm_nt(dx2b, W["o"], "o_proj_dx")
    GW["o"] = _mm_tn(o, dx2b, "o_proj_dw")
    (dq, dkv), (ffn_theirs,) = _attn_bwd(do, q, kv, riders=[_Rider("pair", ffn_halves)])
    ffn_pieces = pair_sums(RS_GROUPS[0], ffn_halves, ffn_theirs)
    dx1, dx1b, GP["g_xattn"] = _mm_nt(dq, W["q"], "q_proj_dx", tk_cap=D, rms=(x1, P["g_xattn"], dx2, True))
    GW["q"] = _mm_tn(hq, dq, "q_proj_dw")
    dkvb = dkv.astype(_MXU)
    GW["kv"] = _mm_tn(mn, dkvb, "kv_proj_dw", tm_cap=256)
    dmn = _mm_nt(dkvb, W["kv"], "kv_proj_dx")
    GP["g_mem"] = _rms_bwd(mem, P["g_mem"], dmn, None, "rms_mem_bwd")
    dmix = _mm_nt(dx1b, W["out"], "out_proj_dx")
    GW["out"] = _mm_tn(mix, dx1b, "out_proj_dw", tn_cap=1024)
    attn_halves = halves(RS_GROUPS[1])
    (da, dg, GP["cf_w"], GP["cf_b"], GP["ln_g"], GP["ln_b"]), (came, attn_theirs) = _cf_bwd(
        dmix, u1, proj, P["cf_w"], P["ln_g"], P["ln_b"],
        riders=[_Rider("exchange", ffn_pieces), _Rider("pair", attn_halves)])
    got.update(zip(RS_GROUPS[0], came))
    attn_pieces = pair_sums(RS_GROUPS[1], attn_halves, attn_theirs)
    (dz, dxbc, ddtr, GP["conv4_w"], GP["conv4_b"], GP["sc"], GP["ssd_norm_g"]), (came,) = _ssd_bwd(
        dmix, y, proj, xbc_c, hprev, P["conv4_w"], P["sc"], P["ssd_norm_g"],
        riders=[_Rider("exchange", attn_pieces)])
    got.update(zip(RS_GROUPS[1], came))
    dproj = jnp.concatenate([dz, da, dg, dxbc, ddtr], axis=1)
    GW["main"] = _mm_tn(h, dproj, "in_proj_dw", tm_cap=512, tk_cap=512, tn_cap=MAINW)
    in_halves = halves(RS_GROUPS[2])
    in_pieces = pair_sums(RS_GROUPS[2], in_halves, _pair_split_list(in_halves, "rs_pair_send_w_in")) if in_halves else []
    (grad_x, GP["g_mix"]), (came,) = _mm_nt(dproj, W["main"], "in_proj_dx", tm_cap=256, tk_cap=D,
                                            riders=[_Rider("exchange", in_pieces)], rms=(x, P["g_mix"], dx1, False))
    got.update(zip(RS_GROUPS[2], came))
    if core is None:
        return loss, grad_x, GW, GP
    return loss, grad_x, GW, GP, pair, got


Z_END, XBC_END, DT_END = NH * HP, NH * HP + XBC, NH * HP + XBC + NH


def _pad_to(a, rows=None, cols=None):
    r = 0 if rows is None else rows - a.shape[0]
    c = 0 if cols is None else cols - a.shape[1]
    return jnp.pad(a, ((0, r), (0, c)))


IN_W = DT_END + 2 * D
W_IN_SEGS = [(0, Z_END, "main", COL_Z), (Z_END, XBC_END, "main", COL_XBC), (XBC_END, DT_END, "main", COL_DT),
             (DT_END, DT_END + D, "main", COL_A), (DT_END + D, IN_W, "main", COL_G)]
BIG = [("w_in", True), ("w_out", False), ("w_q", False), ("w_kv", True), ("w_o", False), ("w_gate", False),
       ("w_up", False), ("w_down", False)]
TRANSPOSED = ("w_gate", "w_up")


def _ref_cols(pieces, a, b):
    cw = IN_W // 4
    out = []
    for j in range(4):
        lo, hi = max(a, j * cw), min(b, (j + 1) * cw)
        if lo < hi:
            out.append(pieces[j][:, lo - j * cw:hi - j * cw])
    return out


def _cat_cols(pieces):
    return jnp.concatenate([pieces[j] for j in range(4)], axis=1)


def _pack_in(w_in):
    dt = _ref_cols(w_in, XBC_END, DT_END)
    pad = jnp.zeros((dt[0].shape[0], MAINW - COL_DT - NH), dt[0].dtype)
    main = jnp.concatenate(_ref_cols(w_in, 0, Z_END) + _ref_cols(w_in, DT_END, IN_W) + _ref_cols(w_in, Z_END, XBC_END)
                           + dt + [pad], axis=1)
    return {"main": main}


def _pack_late(pc):
    rows = lambda n: pc[n].reshape(-1, pc[n].shape[-1])
    return {"out": rows("w_out"), "q": rows("w_q"), "kv": _cat_cols(pc["w_kv"]), "o": rows("w_o"),
            "gate_t": rows("w_gate"), "up_t": rows("w_up"), "down": rows("w_down")}


GW_KEY = {"w_gate": "gate_t", "w_up": "up_t", "w_kv": "kv", "w_out": "out", "w_q": "q", "w_o": "o", "w_down": "down"}
RS_GROUPS = (("w_down", "w_gate", "w_up"), ("w_out", "w_q", "w_kv", "w_o"), ("w_in",))


def _shard_grad(name, GW):
    if name == "w_in":
        cw = IN_W // 4
        pieces = []
        for j in range(4):
            parts = []
            for a, b, src, col in W_IN_SEGS:
                lo, hi = max(a, j * cw), min(b, (j + 1) * cw)
                if lo < hi:
                    parts.append(GW[src][:, col + lo - a:col + hi - a])
            pieces.append(jnp.concatenate(parts, axis=1))
        return jnp.stack(pieces)
    g = GW[GW_KEY[name]]
    if dict(BIG)[name]:
        cw = g.shape[1] // 4
        return jnp.stack([g[:, j * cw:(j + 1) * cw] for j in range(4)])
    return g.reshape(4, g.shape[0] // 4, g.shape[1])


def _stack_sc(dt_bias, a_log, d):
    return _pad_to(jnp.concatenate([dt_bias, a_log, d], axis=0), rows=8, cols=128)


COMM_PARAMS = pltpu.CompilerParams(vmem_limit_bytes=VMEM_LIMIT)


def _dma_sems(*counts):
    return [pltpu.SemaphoreType.DMA((n,)) for n in counts]


def _allgather_list(arrs, name):
    n = len(arrs)
    halved = [a.shape[0] % 16 == 0 for a in arrs]
    oshape = [(4, 2, a.shape[0] // 2, a.shape[1]) if h else (4, 1) + a.shape for a, h in zip(arrs, halved)]

    def body(*refs):
        srcs, outs = refs[:n], refs[n:2 * n]
        ici_send, ici_recv, own_send, own_recv, fwd_send, fwd_recv = refs[2 * n:]
        x, y, c = lax.axis_index("x"), lax.axis_index("y"), lax.axis_index("c")
        me = 2 * x + y
        sib = (x, y, 1 - c)
        peers = _chip_peers(x, y)

        def half(i, h):
            r = arrs[i].shape[0] // 2
            if not halved[i]:
                return srcs[i]
            return srcs[i].at[pl.ds(h * r if isinstance(h, int) else pl.multiple_of(h * r, 8), r)]

        ici, own, fwd = [], [], []
        for i in range(n):
            mine_h = c if halved[i] else 0
            for k, (px, py) in enumerate(peers):
                s = 3 * i + k
                ici.append(_remote(half(i, c), outs[i].at[me, mine_h], ici_send.at[s], ici_recv.at[s], (px, py, c)))
            for h in range(2 if halved[i] else 1):
                s = 2 * i + h
                own.append(_remote(half(i, h), outs[i].at[me, h], own_send.at[s], own_recv.at[s], sib))
        for cp in ici + own:
            cp.start()
        for i in range(n):
            if not halved[i]:
                continue
            for k, (px, py) in enumerate(peers):
                s = 3 * i + k
                got = outs[i].at[2 * px + py, c]
                _remote(half(i, c), got, ici_send.at[s], ici_recv.at[s], (px, py, c)).wait_recv()
                f = _remote(got, got, fwd_send.at[s], fwd_recv.at[s], sib)
                f.start()
                fwd.append(f)
        for i in range(n):
            for k, (px, py) in enumerate(peers):
                s = 3 * i + k
                if halved[i]:
                    _remote(half(i, c), outs[i].at[2 * px + py, 1 - c], fwd_send.at[s], fwd_recv.at[s], sib).wait_recv()
                else:
                    _remote(srcs[i], outs[i].at[2 * px + py, 0], ici_send.at[s], ici_recv.at[s], (px, py, c)).wait_recv()
            for h in range(2 if halved[i] else 1):
                s = 2 * i + h
                _remote(half(i, h), outs[i].at[me, h], own_send.at[s], own_recv.at[s], sib).wait_recv()
        for cp in ici + own + fwd:
            cp.wait_send()

    outs = pl.pallas_call(
        body, name=name, in_specs=[HBM_SPEC] * n, out_specs=[HBM_SPEC] * n,
        out_shape=[SDS(s, a.dtype) for s, a in zip(oshape, arrs)],
        scratch_shapes=_dma_sems(3 * n, 3 * n, 2 * n, 2 * n, 3 * n, 3 * n), compiler_params=COMM_PARAMS)(*arrs)
    return [o.reshape((4,) + a.shape) for o, a in zip(outs, arrs)]


def _pair_split_list(gs, name):
    n = len(gs)

    def body(*refs):
        sends, recvs = _pair_copies(refs[:n], refs[n:2 * n], *refs[2 * n:])
        for cp in sends:
            cp.start()
        for cp in recvs:
            cp.wait_recv()
        for cp in sends:
            cp.wait_send()

    return pl.pallas_call(
        body, name=name, in_specs=[HBM_SPEC] * n, out_specs=[HBM_SPEC] * n,
        out_shape=[SDS((4,) + g.shape[2:], g.dtype) for g in gs],
        scratch_shapes=_dma_sems(4 * n, 4 * n), compiler_params=COMM_PARAMS)(*gs)


def _gather_finish_list(shards, bufs, name="allgather_finish"):
    n = len(shards)

    def body(*refs):
        srcs, outs = refs[:n], refs[2 * n:3 * n]
        own_send, own_recv, fwd_send, fwd_recv = refs[3 * n:]
        x, y, c = lax.axis_index("x"), lax.axis_index("y"), lax.axis_index("c")
        me = 2 * x + y
        sib = (x, y, 1 - c)
        sends, recvs = [], []
        for i in range(n):
            for h in range(2):
                own = _remote(_rows_half(srcs[i], shards[i].shape[0], h), outs[i].at[me, h],
                              own_send.at[2 * i + h], own_recv.at[2 * i + h], sib)
                sends.append(own)
                recvs.append(own)
            for k, (px, py) in enumerate(_chip_peers(x, y)):
                got, s = outs[i].at[2 * px + py, c], 3 * i + k
                sends.append(_remote(got, got, fwd_send.at[s], fwd_recv.at[s], sib))
                recvs.append(_remote(got, outs[i].at[2 * px + py, 1 - c], fwd_send.at[s], fwd_recv.at[s], sib))
        for cp in sends:
            cp.start()
        for cp in recvs:
            cp.wait_recv()
        for cp in sends:
            cp.wait_send()

    outs = pl.pallas_call(
        body, name=name, in_specs=[HBM_SPEC] * (2 * n), out_specs=[HBM_SPEC] * n,
        out_shape=[SDS(b.shape, b.dtype) for b in bufs], input_output_aliases={n + i: i for i in range(n)},
        scratch_shapes=_dma_sems(2 * n, 2 * n, 3 * n, 3 * n), compiler_params=COMM_PARAMS)(*shards, *bufs)
    return [o.reshape((4,) + a.shape) for o, a in zip(outs, shards)]


JOIN_SPLIT = 4


def _pair_join_list(bufs, name="rs_pair_join"):
    n = len(bufs)

    def body(*refs):
        outs = refs[n:2 * n]
        send_sems, recv_sems = refs[2 * n:]
        x, y, c = lax.axis_index("x"), lax.axis_index("y"), lax.axis_index("c")
        sib = (x, y, 1 - c)
        sends, recvs = [], []
        for i in range(n):
            rc = bufs[i].shape[1] // JOIN_SPLIT
            for q in range(JOIN_SPLIT):
                k = JOIN_SPLIT * i + q
                rows = pl.ds(q * rc, rc)
                sends.append(_remote(outs[i].at[c, rows], outs[i].at[c, rows], send_sems.at[k], recv_sems.at[k], sib))
                recvs.append(_remote(outs[i].at[c, rows], outs[i].at[1 - c, rows], send_sems.at[k], recv_sems.at[k], sib))
        for cp in sends:
            cp.start()
        for cp in recvs:
            cp.wait_recv()
        for cp in sends:
            cp.wait_send()

    return pl.pallas_call(
        body, name=name, in_specs=[HBM_SPEC] * n, out_specs=[HBM_SPEC] * n,
        out_shape=[SDS(b.shape, b.dtype) for b in bufs], input_output_aliases={i: i for i in range(n)},
        scratch_shapes=_dma_sems(JOIN_SPLIT * n, JOIN_SPLIT * n), compiler_params=COMM_PARAMS)(*bufs)


def _pair_sum(g, theirs, core, name):
    _, _, r, c = g.shape

    def body(core_ref, g_ref, t_ref, o_ref):
        o_ref[...] = (g_ref[...] + t_ref[...]).astype(o_ref.dtype)

    spec = pltpu.PrefetchScalarGridSpec(
        num_scalar_prefetch=1, grid=(4,),
        in_specs=[pl.BlockSpec((None, None, r, c), lambda j, core_ref: (j, core_ref[0], 0, 0)),
                  pl.BlockSpec((None, r, c), lambda j, core_ref: (j, 0, 0))],
        out_specs=pl.BlockSpec((None, r, c), lambda j, core_ref: (j, 0, 0)))
    return pl.pallas_call(body, name=name, grid_spec=spec, out_shape=SDS((4, r, c), BF16),
                          compiler_params=_cp("parallel"))(core, g, theirs)


def _chip_sum(own, got, where, name):
    _, r, c = own.shape
    tr = r // 2

    def body(w_ref, a_ref, b1_ref, b2_ref, b3_ref, o_ref):
        o_ref[...] = ((a_ref[...].astype(F32) + b1_ref[...].astype(F32)) + b2_ref[...].astype(F32)) + b3_ref[...].astype(F32)

    piece = lambda k: pl.BlockSpec((None, tr, c), lambda i, w_ref: ((w_ref[0] + k) % 4, i, 0))
    spec = pltpu.PrefetchScalarGridSpec(
        num_scalar_prefetch=1, grid=(r // tr,), in_specs=[piece(0), piece(1), piece(2), piece(3)],
        out_specs=pl.BlockSpec((None, tr, c), lambda i, w_ref: (w_ref[1], i, 0)))
    return pl.pallas_call(body, name=name, grid_spec=spec, out_shape=SDS((2, r, c), F32),
                          compiler_params=_cp("parallel"))(where, own, got, got, got)


def _adam_math(w, g, m, v):
    bc1 = 1.0 - ADAM_B1 ** ADAM_STEP
    bc2 = 1.0 - ADAM_B2 ** ADAM_STEP
    mn = ADAM_B1 * m + (1.0 - ADAM_B1) * g
    vn = ADAM_B2 * v + (1.0 - ADAM_B2) * (g * g)
    return -ADAM_LR * ((mn / bc1) / (jnp.sqrt(vn / bc2) + ADAM_EPS) + ADAM_WD * w), mn, vn


PACK_COLS = XBC
PACK = {"g_mix": (0, 1, D), "g_xattn": (1, 1, D), "g_mem": (2, 1, D), "g_ffn": (3, 1, D), "g_final": (4, 1, D),
        "ssd_norm_g": (5, 1, D), "cf_b": (6, 1, D), "ln_g": (7, 1, D), "ln_b": (8, 1, D), "conv4_b": (9, 1, XBC),
        "conv4_w": (10, KS, XBC), "sc": (16, 8, 128), "cf_w": (24, KC, D), "loss": (55, 1, 128)}
PACK_ROWS = 56
SMALL_ADAM = ["g_mix", "g_xattn", "g_mem", "g_ffn", "g_final", "ssd_norm_g", "cf_b", "ln_g", "ln_b", "conv4_b", "sc"]


def _small_allreduce_adamw(grads, wts, mom, var, name="allreduce_small"):
    gk = list(PACK)
    ng, na = len(gk), len(SMALL_ADAM)

    def body(*refs):
        g_in = refs[:ng]
        w_in, m_in, v_in = (refs[ng + i * na: ng + (i + 1) * na] for i in range(3))
        o = refs[ng + 3 * na:]
        g_out = o[:ng]
        d_out, m_out, v_out = (o[ng + i * na: ng + (i + 1) * na] for i in range(3))
        pack, pbuf, psum, cbuf, acc, send_sems, recv_sems = o[ng + 3 * na:]
        x, y, c = lax.axis_index("x"), lax.axis_index("y"), lax.axis_index("c")
        me = 2 * x + y
        pack[...] = jnp.zeros_like(pack)
        for i, k in enumerate(gk):
            r0, nr, nc = PACK[k]
            pack[r0:r0 + nr, 0:nc] = g_in[i][...]
        pair = _remote(pack, pbuf.at[c], send_sems.at[0], recv_sems.at[0], (x, y, 1 - c))
        pair.start()
        pbuf[c] = pack[...]
        _remote(pack, pbuf.at[1 - c], send_sems.at[0], recv_sems.at[0], (x, y, 1 - c)).wait_recv()
        pair.wait_send()
        psum[...] = pbuf[0] + pbuf[1]
        peers = _chip_peers(x, y)
        sends = [_remote(psum, cbuf.at[me], send_sems.at[1 + k], recv_sems.at[1 + k], (px, py, c))
                 for k, (px, py) in enumerate(peers)]
        for cp in sends:
            cp.start()
        cbuf[me] = psum[...]
        for k, (px, py) in enumerate(peers):
            _remote(psum, cbuf.at[2 * px + py], send_sems.at[1 + k], recv_sems.at[1 + k], (px, py, c)).wait_recv()
        for cp in sends:
            cp.wait_send()
        acc[...] = (cbuf[0] + cbuf[1]) + (cbuf[2] + cbuf[3])
        for i, k in enumerate(gk):
            r0, nr, nc = PACK[k]
            g_out[i][...] = acc[r0:r0 + nr, 0:nc]
        for i, k in enumerate(SMALL_ADAM):
            r0, nr, nc = PACK[k]
            d_out[i][...], m_out[i][...], v_out[i][...] = _adam_math(
                w_in[i][...], acc[r0:r0 + nr, 0:nc], m_in[i][...], v_in[i][...])

    args = [grads[k] for k in gk] + [d[k] for d in (wts, mom, var) for k in SMALL_ADAM]
    shp = lambda k: SDS((PACK[k][1], PACK[k][2]), F32)
    vm = pl.BlockSpec(memory_space=pltpu.VMEM)
    outs = pl.pallas_call(
        body, name=name, in_specs=[vm] * len(args), out_specs=[vm] * (ng + 3 * na),
        out_shape=[shp(k) for k in gk] + [shp(k) for _ in range(3) for k in SMALL_ADAM],
        scratch_shapes=[pltpu.VMEM((PACK_ROWS, PACK_COLS), F32), pltpu.VMEM((2, PACK_ROWS, PACK_COLS), F32),
                        pltpu.VMEM((PACK_ROWS, PACK_COLS), F32), pltpu.VMEM((4, PACK_ROWS, PACK_COLS), F32),
                        pltpu.VMEM((PACK_ROWS, PACK_COLS), F32)] + _dma_sems(4, 4),
        compiler_params=COMM_PARAMS)(*args)
    red = dict(zip(gk, outs[:ng]))
    parts = [dict(zip(SMALL_ADAM, outs[ng + i * na: ng + (i + 1) * na])) for i in range(3)]
    return red, parts[0], parts[1], parts[2]


def _adamw_cols(w, gfull, m, v, chip, name):
    _, R, C = w.shape

    def body(w_idx, w_ref, g_ref, m_ref, v_ref, go_ref, d_ref, mo_ref, vo_ref):
        go_ref[...] = g_ref[...]
        d_ref[...], mo_ref[...], vo_ref[...] = _adam_math(w_ref[...], g_ref[...], m_ref[...], v_ref[...])

    blk = pl.BlockSpec((None, R, C), lambda i, w_idx: (0, 0, 0))
    spec = pltpu.PrefetchScalarGridSpec(
        num_scalar_prefetch=1, grid=(1,),
        in_specs=[blk, pl.BlockSpec((R, C), lambda i, w_idx: (0, w_idx[0])), blk, blk], out_specs=[blk] * 4)
    return pl.pallas_call(body, name=name, grid_spec=spec, out_shape=[SDS((1, R, C), F32)] * 4,
                          compiler_params=_cp("arbitrary"))(chip, w, gfull, m, v)


def _adamw(w, g, m, v, name):
    _, R, C = w.shape
    half = R // 2
    tr = _tile(half, max(8, (2 ** 17 // C) // 8 * 8), 8)
    nh = half // tr

    def body(w_ref, g_ref, m_ref, v_ref, go_ref, d_ref, mo_ref, vo_ref):
        go_ref[...] = g_ref[...]
        d_ref[...], mo_ref[...], vo_ref[...] = _adam_math(w_ref[...], g_ref[...], m_ref[...], v_ref[...])

    blk = pl.BlockSpec((None, tr, C), lambda i: (0, i, 0))
    gblk = pl.BlockSpec((None, tr, C), lambda i: (i // nh, i % nh, 0))
    return pl.pallas_call(body, name=name, grid=(R // tr,), in_specs=[blk, gblk, blk, blk], out_specs=[blk] * 4,
                          out_shape=[SDS((1, R, C), F32)] * 4, compiler_params=_cp("parallel"))(w, g, m, v)


WEIGHT_NAMES = ["norm_mix_g", "w_in", "ssd_conv_w", "ssd_conv_b", "ssd_dt_bias", "ssd_A_log", "ssd_D", "ssd_norm_g",
                "cf_conv_w", "cf_conv_b", "cf_ln_g", "cf_ln_b", "w_out", "norm_xattn_g", "norm_mem_g", "w_q", "w_kv",
                "w_o", "norm_ffn_g", "w_gate", "w_up", "w_down", "norm_final_g"]
VEC_REF = [("norm_mix_g", "g_mix"), ("norm_xattn_g", "g_xattn"), ("norm_mem_g", "g_mem"), ("norm_ffn_g", "g_ffn"),
           ("norm_final_g", "g_final"), ("ssd_norm_g", "ssd_norm_g"), ("cf_conv_b", "cf_b"), ("cf_ln_g", "ln_g"),
           ("cf_ln_b", "ln_b"), ("ssd_conv_b", "conv4_b")]
SC_REF = ["ssd_dt_bias", "ssd_A_log", "ssd_D"]


def _small_side(get):
    d = {k: get(ref_name).reshape(1, -1) for ref_name, k in VEC_REF}
    d["sc"] = _stack_sc(*[get(n) for n in SC_REF])
    return d


def kernel(x, mem, norm_mix_g, w_in, ssd_conv_w, ssd_conv_b, ssd_dt_bias, ssd_A_log, ssd_D, ssd_norm_g, cf_conv_w, cf_conv_b, cf_ln_g, cf_ln_b, w_out, norm_xattn_g, norm_mem_g, w_q, w_kv, w_o, norm_ffn_g, w_gate, w_up, w_down, norm_final_g, loss_target, m_norm_mix_g, m_w_in, m_ssd_conv_w, m_ssd_conv_b, m_ssd_dt_bias, m_ssd_A_log, m_ssd_D, m_ssd_norm_g, m_cf_conv_w, m_cf_conv_b, m_cf_ln_g, m_cf_ln_b, m_w_out, m_norm_xattn_g, m_norm_mem_g, m_w_q, m_w_kv, m_w_o, m_norm_ffn_g, m_w_gate, m_w_up, m_w_down, m_norm_final_g, v_norm_mix_g, v_w_in, v_ssd_conv_w, v_ssd_conv_b, v_ssd_dt_bias, v_ssd_A_log, v_ssd_D, v_ssd_norm_g, v_cf_conv_w, v_cf_conv_b, v_cf_ln_g, v_cf_ln_b, v_w_out, v_norm_xattn_g, v_norm_mem_g, v_w_q, v_w_kv, v_w_o, v_norm_ffn_g, v_w_gate, v_w_up, v_w_down, v_norm_final_g):
    env = dict(locals())
    view = lambda n, a: a.transpose(0, 2, 1) if n in TRANSPOSED else a
    wts = {n: view(n, env[n]) for n in WEIGHT_NAMES}
    mom = {n: view(n, env["m_" + n]) for n in WEIGHT_NAMES}
    var = {n: view(n, env["v_" + n]) for n in WEIGHT_NAMES}
    chip = (2 * lax.axis_index("x") + lax.axis_index("y")).astype(jnp.int32).reshape(1)
    core = lax.axis_index("c").astype(jnp.int32).reshape(1)
    where = jnp.concatenate([chip, core])
    big = [n for n, _ in BIG]

    w_in_g, conv4_g, cf_g = _allgather_list([w_in[0].astype(BF16), ssd_conv_w[0], cf_conv_w[0]], "allgather_first")
    W = _pack_in(w_in_g)
    P = _small_side(lambda n: wts[n])
    P["conv4_w"], P["cf_w"] = _cat_cols(conv4_g), _cat_cols(cf_g)
    late = {n: wts[n][0].astype(BF16) for grp in AG_RIDE for n in grp}

    loss, grad_x, GW, GP, pair, got = _local_step(x[0], mem[0], loss_target[0], W, P, core, late)
    joined = _pair_join_list([_chip_sum(pair[n], got[n], where, "rs_chip_sum_" + n) for n in big])
    gshard = dict(zip(big, joined))

    small = dict(GP)
    small["loss"] = loss
    red, sd, sm, sv = _small_allreduce_adamw(small, {k: P[k] for k in SMALL_ADAM}, _small_side(lambda n: mom[n]),
                                             _small_side(lambda n: var[n]))
    grads, delta, new_m, new_v = {}, {}, {}, {}
    for ref_name, k in VEC_REF:
        shp = wts[ref_name].shape
        for dst, src in ((grads, red), (delta, sd), (new_m, sm), (new_v, sv)):
            dst[ref_name] = src[k].reshape(shp)
    for row, ref_name in enumerate(SC_REF):
        for dst, src in ((grads, red), (delta, sd), (new_m, sm), (new_v, sv)):
            dst[ref_name] = src["sc"][row:row + 1, :NH]

    for n, k in (("ssd_conv_w", "conv4_w"), ("cf_conv_w", "cf_w")):
        grads[n], delta[n], new_m[n], new_v[n] = _adamw_cols(wts[n], red[k], mom[n], var[n], chip, "adamw_" + n)
    for n in big:
        outs = _adamw(wts[n], gshard[n], mom[n], var[n], "adamw_" + n)
        grads[n], delta[n], new_m[n], new_v[n] = [view(n, o) for o in outs]

    return (red["loss"][0, 0], grad_x[None], *[grads[n] for n in WEIGHT_NAMES], *[delta[n] for n in WEIGHT_NAMES],
            *[new_m[n] for n in WEIGHT_NAMES], *[new_v[n] for n in WEIGHT_NAMES])
```

```python
import functools
import math

import jax
import jax.numpy as jnp
from jax import lax
from jax.experimental import pallas as pl
from jax.experimental.pallas import tpu as pltpu

F32 = jnp.float32
BF16 = jnp.bfloat16
_MXU = BF16

D = 1024
MEM = 256
NH, HP, NG, NS = 16, 64, 2, 128
GW = NH * HP // NG
CH = 128
XBC = NH * HP + 2 * NG * NS
KS, KC = 4, 31
XH, XD = 4, 256
DFF = 2816
EPS = 1e-6
COL_Z, COL_A, COL_G, COL_XBC, COL_DT, MAINW = 0, 1024, 2048, 3072, 4608, 4736
VMEM_LIMIT = 56 * 2 ** 20

ADAM_LR, ADAM_B1, ADAM_B2, ADAM_EPS, ADAM_WD, ADAM_STEP = 0.001, 0.9, 0.999, 1e-08, 0.01, 10

SDS = jax.ShapeDtypeStruct
MESHID = pl.DeviceIdType.MESH


def _cp(*sem):
    return pltpu.CompilerParams(dimension_semantics=sem, vmem_limit_bytes=VMEM_LIMIT)


def _tile(n, cap, unit=128):
    if n <= cap:
        return n
    best = None
    for t in range(unit, cap + 1, unit):
        if n % t == 0:
            best = t
    assert best is not None, (n, cap)
    return best


def _sigmoid(x):
    return 1.0 / (1.0 + jnp.exp(-x))


def _silu(x):
    return x * _sigmoid(x)


def _dsilu(x):
    s = _sigmoid(x)
    return s * (1.0 + x * (1.0 - s))


def _softplus(x):
    return jnp.maximum(x, 0.0) + jnp.log(1.0 + jnp.exp(-jnp.abs(x)))


def _split_bf16(x, passes):
    parts, r = [], x.astype(F32)
    for _ in range(passes):
        p = r.astype(BF16)
        parts.append(p)
        r = r - p.astype(F32)
    return parts


def _dot(a, b, dims=None, exact=None, passes=2):
    dn = {None: (((1,), (0,)), ((), ())), "nt": (((1,), (1,)), ((), ())), "tn": (((0,), (0,)), ((), ()))}[dims]
    if exact is None:
        return lax.dot_general(a.astype(_MXU), b.astype(_MXU), dn, preferred_element_type=F32)
    if exact == "a":
        terms = [(a.astype(BF16), p) for p in _split_bf16(b, passes)]
    else:
        terms = [(p, b.astype(BF16)) for p in _split_bf16(a, passes)]
    out = None
    for lhs, rhs in terms:
        d = lax.dot_general(lhs, rhs, dn, preferred_element_type=F32)
        out = d if out is None else out + d
    return out


def _rms_bwd_tile(xv, gv, dy, dres):
    r = lax.rsqrt(jnp.mean(xv * xv, axis=-1, keepdims=True) + EPS)
    xh = xv * r
    gdy = dy * gv
    dx = r * (gdy - xh * jnp.mean(xh * gdy, axis=-1, keepdims=True))
    return dres + dx, jnp.sum(dy * xh, axis=0, keepdims=True)


def _matmul(kind, a, b, name, add, out_dtype, tm, tw, riders, rms, norm=None, loss=None):
    M, K = a.shape
    Wd = b.shape[1] if kind == "nn" else b.shape[0]
    rd = _Riders(riders or ())
    nco = len(rd.arrays())
    nin = 2 + (add is not None) + (3 if rms else 0) + (norm is not None) + (2 if loss else 0)
    low = bool(rms and rms[3])
    nout = (2 + low) if rms else 2 if norm is not None else 4 if loss else 1
    grid = (Wd // tw, M // tm)
    assert not (rms or loss or norm is not None) or tw == Wd, "the row-wise epilogues need whole rows"

    def body(*refs):
        a_ref, b_ref = refs[0], refs[1]
        outs = refs[nin + nco:nin + nco + nout]
        rd.bind(refs[nin:nin + nco], refs[nin + nco + nout:nin + 2 * nco + nout], refs[nin + 2 * nco + nout:], grid).start()
        acc = _dot(a_ref[...], b_ref[...], None if kind == "nn" else "nt")
        if add is not None:
            acc = acc + refs[2][...]
        if loss:
            lpart, dx, dg = _final_loss_tile(acc, refs[nin - 2][...], refs[nin - 1][...])

            @pl.when(pl.program_id(1) == 0)
            def _():
                outs[0][...] = jnp.zeros_like(outs[0])
                outs[3][...] = jnp.zeros_like(outs[3])

            outs[0][...] += lpart
            outs[1][...] = dx
            outs[2][...] = dx.astype(outs[2].dtype)
            outs[3][...] += dg
        elif norm is not None:
            outs[0][...] = acc.astype(outs[0].dtype)
            r = lax.rsqrt(jnp.mean(acc * acc, axis=-1, keepdims=True) + EPS)
            outs[1][...] = (acc * r * refs[nin - 1][...]).astype(outs[1].dtype)
        elif rms:
            x_ref, g_ref, dres_ref = refs[nin - 3:nin]
            tot, dg = _rms_bwd_tile(x_ref[...], g_ref[...], acc, dres_ref[...])

            @pl.when(pl.program_id(1) == 0)
            def _():
                outs[-1][...] = jnp.zeros_like(outs[-1])

            outs[-1][...] += dg
            outs[0][...] = tot
            if low:
                outs[1][...] = tot.astype(outs[1].dtype)
        else:
            outs[0][...] = acc.astype(outs[0].dtype)
        rd.finish()

    tile = pl.BlockSpec((tm, tw), lambda j, i: (i, j))
    bspec = pl.BlockSpec((K, tw), lambda j, i: (0, j)) if kind == "nn" else pl.BlockSpec((tw, K), lambda j, i: (j, 0))
    in_specs, args = [pl.BlockSpec((tm, K), lambda j, i: (i, 0)), bspec], [a, b]
    if add is not None:
        in_specs.append(tile)
        args.append(add)
    vec = pl.BlockSpec((1, tw), lambda j, i: (0, j))
    if rms:
        in_specs += [tile, vec, tile]
        args += [rms[0], rms[1], rms[2]]
        out_specs = [tile] * (1 + low) + [vec]
        out_shape = [SDS((M, Wd), F32)] + ([SDS((M, Wd), _MXU)] if low else []) + [SDS((1, Wd), F32)]
    elif loss:
        in_specs += [vec, tile]
        args += [loss[0], loss[1]]
        out_specs = [pl.BlockSpec((1, 128), lambda j, i: (0, 0)), tile, tile, vec]
        out_shape = [SDS((1, 128), F32), SDS((M, Wd), F32), SDS((M, Wd), _MXU), SDS((1, Wd), F32)]
    elif norm is not None:
        in_specs.append(vec)
        args.append(norm)
        out_specs, out_shape = [tile, tile], [SDS((M, Wd), out_dtype), SDS((M, Wd), _MXU)]
    else:
        out_specs, out_shape = [tile], [SDS((M, Wd), out_dtype)]
    order = ("arbitrary", "arbitrary") if (nco or rms or loss) else ("parallel", "parallel")
    outs = pl.pallas_call(
        body, name=name, grid=grid, in_specs=in_specs + [HBM_SPEC] * nco, out_specs=out_specs + [HBM_SPEC] * nco,
        out_shape=out_shape + rd.out_shapes(), scratch_shapes=rd.scratch(),
        compiler_params=_cp(*order))(*args, *rd.arrays())
    main = tuple(outs[:nout]) if nout > 1 else outs[0]
    return main if riders is None else (main, rd.split(outs[nout:]))


def _mm_nn(a, b, name, add=None, out_dtype=F32, tm_cap=1024, tn_cap=1408, riders=None, rms=None, norm=None, loss=None):
    tm, tn = _tile(a.shape[0], tm_cap, 8), _tile(b.shape[1], tn_cap)
    return _matmul("nn", a, b, name, add, out_dtype, tm, tn, riders, rms, norm, loss)


def _mm_nt(a, b, name, add=None, out_dtype=F32, tm_cap=512, tk_cap=1024, riders=None, rms=None):
    tm, tk = _tile(a.shape[0], tm_cap, 8), _tile(b.shape[0], tk_cap)
    return _matmul("nt", a, b, name, add, out_dtype, tm, tk, riders, rms)


def _mm_tn(a, b, name, tm_cap=1024, tk_cap=512, tn_cap=1408):
    M, K = a.shape
    _, N = b.shape
    tm, tk, tn = _tile(M, tm_cap, 8), _tile(K, tk_cap), _tile(N, tn_cap)

    def body(a_ref, b_ref, o_ref):
        @pl.when(pl.program_id(2) == 0)
        def _():
            o_ref[...] = jnp.zeros_like(o_ref)

        o_ref[...] += _dot(a_ref[...], b_ref[...], "tn")

    return pl.pallas_call(
        body, name=name, grid=(K // tk, N // tn, M // tm),
        in_specs=[pl.BlockSpec((tm, tk), lambda k, n, m: (m, k)), pl.BlockSpec((tm, tn), lambda k, n, m: (m, n))],
        out_specs=pl.BlockSpec((tk, tn), lambda k, n, m: (k, n)), out_shape=SDS((K, N), F32),
        compiler_params=_cp("parallel", "parallel", "arbitrary"))(a, b)


def _rms_fwd(x, g, name, tb_cap=512):
    S, Dm = x.shape
    tb = _tile(S, tb_cap, 8)

    def body(x_ref, g_ref, o_ref):
        xv = x_ref[...]
        r = lax.rsqrt(jnp.mean(xv * xv, axis=-1, keepdims=True) + EPS)
        o_ref[...] = (xv * r * g_ref[...]).astype(o_ref.dtype)

    return pl.pallas_call(
        body, name=name, grid=(S // tb,),
        in_specs=[pl.BlockSpec((tb, Dm), lambda i: (i, 0)), pl.BlockSpec((1, Dm), lambda i: (0, 0))],
        out_specs=pl.BlockSpec((tb, Dm), lambda i: (i, 0)), out_shape=SDS((S, Dm), _MXU),
        compiler_params=_cp("parallel"))(x, g)


def _rms_bwd(x, g, dh, dres, name, tb_cap=512, low=True):
    S, Dm = x.shape
    tb = _tile(S, tb_cap, 8)
    need_dx = dres is not None

    def body(x_ref, g_ref, dh_ref, *rest):
        dg_ref = rest[-1]
        tot, dg = _rms_bwd_tile(x_ref[...], g_ref[...], dh_ref[...].astype(F32), rest[0][...] if need_dx else 0.0)

        @pl.when(pl.program_id(0) == 0)
        def _():
            dg_ref[...] = jnp.zeros_like(dg_ref)

        dg_ref[...] += dg
        if need_dx:
            rest[1][...] = tot
            if low:
                rest[2][...] = tot.astype(rest[2].dtype)

    row = pl.BlockSpec((tb, Dm), lambda i: (i, 0))
    vec = pl.BlockSpec((1, Dm), lambda i: (0, 0))
    if need_dx:
        outs = [SDS((S, Dm), F32)] + ([SDS((S, Dm), _MXU)] if low else [])
        return pl.pallas_call(
            body, name=name, grid=(S // tb,), in_specs=[row, vec, row, row], out_specs=[row] * len(outs) + [vec],
            out_shape=outs + [SDS((1, Dm), F32)], compiler_params=_cp("arbitrary"))(x, g, dh, dres)
    return pl.pallas_call(
        body, name=name, grid=(S // tb,), in_specs=[row, vec, row], out_specs=vec,
        out_shape=SDS((1, Dm), F32), compiler_params=_cp("arbitrary"))(x, g, dh)


def _final_loss_tile(xv, gv, tv):
    r = lax.rsqrt(jnp.mean(xv * xv, axis=-1, keepdims=True) + EPS)
    xh = xv * r
    e = xh * gv - tv
    dy = e * (1.0 / xv.shape[-1])
    gdy = dy * gv
    dx = r * (gdy - xh * jnp.mean(xh * gdy, axis=-1, keepdims=True))
    return 0.5 * jnp.sum(jnp.mean(e * e, axis=-1, keepdims=True)), dx, jnp.sum(dy * xh, axis=0, keepdims=True)


SSD_HALO = 8
CF_HALO = 32

HBM_SPEC = pl.BlockSpec(memory_space=pl.ANY)


def _chip_peers(x, y):
    return [(1 - x, y), (x, 1 - y), (1 - x, 1 - y)]


def _remote(src, dst, send_sem, recv_sem, dev):
    return pltpu.make_async_remote_copy(src_ref=src, dst_ref=dst, send_sem=send_sem, recv_sem=recv_sem,
                                        device_id=dev, device_id_type=MESHID)


def _scatter_copies(srcs, outs, send_sems, recv_sems):
    x, y, c = lax.axis_index("x"), lax.axis_index("y"), lax.axis_index("c")
    me = 2 * x + y
    sends, recvs = [], []
    for i, (s, o) in enumerate(zip(srcs, outs)):
        for k, (px, py) in enumerate(_chip_peers(x, y)):
            j = 3 * i + k
            sends.append(_remote(s.at[2 * px + py], o.at[me], send_sems.at[j], recv_sems.at[j], (px, py, c)))
            recvs.append(_remote(s.at[me], o.at[2 * px + py], send_sems.at[j], recv_sems.at[j], (px, py, c)))
    return sends, recvs


def _pair_copies(srcs, outs, send_sems, recv_sems):
    x, y, c = lax.axis_index("x"), lax.axis_index("y"), lax.axis_index("c")
    sends = [_remote(s.at[j, 1 - c], o.at[j], send_sems.at[4 * i + j], recv_sems.at[4 * i + j], (x, y, 1 - c))
             for i, (s, o) in enumerate(zip(srcs, outs)) for j in range(4)]
    return sends, sends


def _rows_half(ref, rows, h):
    r = rows // 2
    return ref.at[pl.ds(h * r if isinstance(h, int) else pl.multiple_of(h * r, 8), r)]


def _gather_copies(srcs, outs, rows, send_sems, recv_sems):
    x, y, c = lax.axis_index("x"), lax.axis_index("y"), lax.axis_index("c")
    me = 2 * x + y
    sends, recvs = [], []
    for i, (s, o) in enumerate(zip(srcs, outs)):
        mine = _rows_half(s, rows[i], c)
        for k, (px, py) in enumerate(_chip_peers(x, y)):
            j = 3 * i + k
            sends.append(_remote(mine, o.at[me, c], send_sems.at[j], recv_sems.at[j], (px, py, c)))
            recvs.append(_remote(mine, o.at[2 * px + py, c], send_sems.at[j], recv_sems.at[j], (px, py, c)))
    return sends, recvs


def _gather_shapes(shards):
    return [SDS((4, 2, a.shape[0] // 2, a.shape[1]), a.dtype) for a in shards]


class _Rider:
    SEMS_PER_ARRAY = {"exchange": 3, "gather": 3, "pair": 4}

    def __init__(self, kind, arrays):
        self.kind, self.arrays = kind, list(arrays)

    def out_shapes(self):
        if self.kind == "gather":
            return _gather_shapes(self.arrays)
        if self.kind == "pair":
            return [SDS((4,) + a.shape[2:], a.dtype) for a in self.arrays]
        return [SDS(a.shape, a.dtype) for a in self.arrays]

    def scratch(self):
        n = self.SEMS_PER_ARRAY[self.kind] * len(self.arrays)
        return [pltpu.SemaphoreType.DMA((n,)), pltpu.SemaphoreType.DMA((n,))]

    def copies(self, srcs, outs, send_sems, recv_sems):
        if self.kind == "gather":
            return _gather_copies(srcs, outs, [a.shape[0] for a in self.arrays], send_sems, recv_sems)
        if self.kind == "pair":
            return _pair_copies(srcs, outs, send_sems, recv_sems)
        return _scatter_copies(srcs, outs, send_sems, recv_sems)


class _Riders:
    def __init__(self, riders):
        self.given = list(riders)
        self.riders = [r for r in self.given if r.arrays]

    def arrays(self):
        return [a for r in self.riders for a in r.arrays]

    def out_shapes(self):
        return [s for r in self.riders for s in r.out_shapes()]

    def scratch(self):
        return [s for r in self.riders for s in r.scratch()]

    def split(self, outs):
        res, k = [], 0
        for r in self.given:
            res.append(list(outs[k:k + len(r.arrays)]))
            k += len(r.arrays)
        return res

    def bind(self, in_refs, out_refs, sem_refs, steps):
        self.steps = steps if isinstance(steps, tuple) else (steps,)
        self.bound, k = [], 0
        for i, r in enumerate(self.riders):
            n = len(r.arrays)
            self.bound.append((r, in_refs[k:k + n], out_refs[k:k + n], sem_refs[2 * i], sem_refs[2 * i + 1]))
            k += n
        return self

    def _at(self, last):
        hit = None
        for ax, n in enumerate(self.steps):
            here = pl.program_id(ax) == (n - 1 if last else 0)
            hit = here if hit is None else jnp.logical_and(hit, here)
        return hit

    def _copies(self):
        sends, recvs = [], []
        for r, srcs, outs, send_sems, recv_sems in self.bound:
            s, w = r.copies(srcs, outs, send_sems, recv_sems)
            sends += s
            recvs += w
        return sends, recvs

    def start(self):
        if self.riders:
            @pl.when(self._at(last=False))
            def _():
                for cp in self._copies()[0]:
                    cp.start()

    def finish(self):
        if self.riders:
            @pl.when(self._at(last=True))
            def _():
                sends, recvs = self._copies()
                for cp in recvs:
                    cp.wait_recv()
                for cp in sends:
                    cp.wait_send()


def _head_consts():
    e = (lax.broadcasted_iota(jnp.int32, (128, NH * HP), 1) // HP == lax.broadcasted_iota(jnp.int32, (128, NH * HP), 0)).astype(F32)
    et = (lax.broadcasted_iota(jnp.int32, (NH * HP, 128), 0) // HP == lax.broadcasted_iota(jnp.int32, (NH * HP, 128), 1)).astype(F32)
    r = lax.broadcasted_iota(jnp.int32, (CH, CH), 0)
    c = lax.broadcasted_iota(jnp.int32, (CH, CH), 1)
    return e, et, (c <= r), (r <= c)


def _ssd_common(xbc_c, dtr, dtb, alog, e, tril, triu):
    xbc = _silu(xbc_c)
    xs = xbc[:, :NH * HP]
    dt = _softplus(dtr + dtb)
    A = -jnp.exp(alog)
    a = dt * A
    cs = _dot(tril, a, exact="a", passes=3)
    csT = _dot(a, triu, "tn", exact="b", passes=3)
    csL = cs[CH - 1:CH, :]
    wdec = jnp.exp(csL - cs) * dt
    dtE = _dot(dt, e, exact="b")
    ecsE = _dot(jnp.exp(cs), e, exact="b")
    wE = _dot(wdec, e, exact="b")
    eL = jnp.exp(csL)
    return xbc, xs, dt, A, cs, csT, csL, wdec, dtE, ecsE, wE, eL


def _ssd_fwd(proj, cw, cb, sc, norm_g, riders=(), name="ssd_fwd"):
    S = proj.shape[0]
    nc = S // CH
    rd = _Riders(riders)
    nco = len(rd.arrays())

    def body(*refs):
        z_ref, xp_ref, cw_ref, cb_ref, dtr_ref, sc_ref, ng_ref = refs[:7]
        xc_ref, y_ref, yn_ref, hp_ref = refs[7 + nco:11 + nco]
        hst, cext = refs[11 + 2 * nco:13 + 2 * nco]
        rd.bind(refs[7:7 + nco], refs[11 + nco:11 + 2 * nco], refs[13 + 2 * nco:], nc).start()

        @pl.when(pl.program_id(0) == 0)
        def _():
            hst[...] = jnp.zeros_like(hst)
            cext[pl.ds(0, SSD_HALO), :] = jnp.zeros((SSD_HALO, XBC), F32)

        cext[pl.ds(SSD_HALO, CH), :] = xp_ref[...]
        xc = jnp.zeros((CH, XBC), F32) + cb_ref[...]
        for k in range(KS):
            xc = xc + cext[pl.ds(SSD_HALO - (KS - 1) + k, CH), :] * cw_ref[k:k + 1, :]
        xc_ref[...] = xc
        cext[pl.ds(0, SSD_HALO), :] = cext[pl.ds(CH, SSD_HALO), :]

        e, et, tril, triu = _head_consts()
        xbc, xs, dt, A, cs, csT, csL, wdec, dtE, ecsE, wE, eL = _ssd_common(
            xc, dtr_ref[...], sc_ref[0:1, :], sc_ref[1:2, :], e, tril, triu)
        hp_ref[0] = hst[...]
        xd = xs * dtE
        xw = xs * wE
        dE = _dot(jnp.broadcast_to(sc_ref[2:3, :], (8, 128)), e, exact="b", passes=3)[0:1, :]
        eLcol = jnp.sum(et * eL, axis=1, keepdims=True)
        for g in range(NG):
            Bg = xbc[:, NH * HP + g * NS: NH * HP + (g + 1) * NS]
            Cg = xbc[:, NH * HP + NG * NS + g * NS: NH * HP + NG * NS + (g + 1) * NS]
            gs = slice(g * GW, (g + 1) * GW)
            G = _dot(Cg, Bg, "nt")
            hg = hst[gs, :]
            yoff = ecsE[:, gs] * _dot(Cg, hg, "nt")
            hst[gs, :] = eLcol[gs, :] * hg + _dot(xw[:, gs], Bg, "tn")
            for hh in range(NH // NG):
                h = g * (NH // NG) + hh
                hs = slice(h * HP, (h + 1) * HP)
                m = jnp.where(tril, jnp.exp(jnp.where(tril, cs[:, h:h + 1] - csT[h:h + 1, :], 0.0)), 0.0)
                yd = _dot(G * m, xd[:, hs])
                y_ref[:, hs] = yd + yoff[:, hh * HP:(hh + 1) * HP] + dE[:, hs] * xs[:, hs]
        y = y_ref[...]
        yz = y * _silu(z_ref[...])
        for g in range(NG):
            gs = slice(g * GW, (g + 1) * GW)
            yg = yz[:, gs]
            r = lax.rsqrt(jnp.mean(yg * yg, axis=-1, keepdims=True) + EPS)
            yn_ref[:, gs] = (yg * r * ng_ref[:, gs]).astype(yn_ref.dtype)
        rd.finish()

    outs = pl.pallas_call(
        body, name=name, grid=(nc,),
        in_specs=[pl.BlockSpec((CH, D), lambda c: (c, COL_Z // D)),
                  pl.BlockSpec((CH, XBC), lambda c: (c, COL_XBC // XBC)),
                  pl.BlockSpec((KS, XBC), lambda c: (0, 0)),
                  pl.BlockSpec((1, XBC), lambda c: (0, 0)),
                  pl.BlockSpec((CH, 128), lambda c: (c, COL_DT // 128)),
                  pl.BlockSpec((8, 128), lambda c: (0, 0)),
                  pl.BlockSpec((1, D), lambda c: (0, 0))] + [HBM_SPEC] * nco,
        out_specs=[pl.BlockSpec((CH, XBC), lambda c: (c, 0)), pl.BlockSpec((CH, D), lambda c: (c, 0)),
                   pl.BlockSpec((CH, D), lambda c: (c, 0)),
                   pl.BlockSpec((1, NH * HP, NS), lambda c: (c, 0, 0))] + [HBM_SPEC] * nco,
        out_shape=[SDS((S, XBC), F32), SDS((S, D), F32), SDS((S, D), _MXU), SDS((nc, NH * HP, NS), F32)]
        + rd.out_shapes(),
        scratch_shapes=[pltpu.VMEM((NH * HP, NS), F32), pltpu.VMEM((SSD_HALO + CH, XBC), F32)] + rd.scratch(),
        compiler_params=_cp("arbitrary"))(proj, proj, cw, cb, proj, sc, norm_g, *rd.arrays())
    return outs[:4], rd.split(outs[4:])


def _ssd_bwd(dmix, y, proj, xbc_c, hprev, cw, sc, norm_g, riders=(), name="ssd_bwd"):
    S = proj.shape[0]
    nc = S // CH
    rd = _Riders(riders)
    nco = len(rd.arrays())
    rev = lambda c: nc - 1 - c

    def body(*refs):
        dyn_ref, y_ref, z_ref, x_ref, xp_ref, dtr_ref, hp_ref, cw_ref, sc_ref, ng_ref = refs[:10]
        dz_ref, dx_ref, ddtr_ref, gcw_ref, gcb_ref, gsc_ref, gng_ref = refs[10 + nco:17 + nco]
        dh, dxd, cext = refs[17 + 2 * nco:20 + 2 * nco]
        rd.bind(refs[10:10 + nco], refs[17 + nco:17 + 2 * nco], refs[20 + 2 * nco:], nc).start()

        @pl.when(pl.program_id(0) == 0)
        def _():
            dh[...] = jnp.zeros_like(dh)
            cext[pl.ds(CH, SSD_HALO), :] = jnp.zeros((SSD_HALO, XBC), F32)
            gcw_ref[...] = jnp.zeros_like(gcw_ref)
            gcb_ref[...] = jnp.zeros_like(gcb_ref)
            gsc_ref[...] = jnp.zeros_like(gsc_ref)
            gng_ref[...] = jnp.zeros_like(gng_ref)

        e, et, tril, triu = _head_consts()
        xbc_c = x_ref[...]
        dtr = dtr_ref[...]
        dtb = sc_ref[0:1, :]
        xbc, xs, dt, A, cs, csT, csL, wdec, dtE, ecsE, wE, eL = _ssd_common(
            xbc_c, dtr, dtb, sc_ref[1:2, :], e, tril, triu)
        xd = xs * dtE
        xw = xs * wE
        dE = _dot(jnp.broadcast_to(sc_ref[2:3, :], (8, 128)), e, exact="b", passes=3)[0:1, :]
        eLcol = jnp.sum(et * eL, axis=1, keepdims=True)

        yv = y_ref[...]
        zv = z_ref[...]
        sz = _silu(zv)
        yz = yv * sz
        dyn = dyn_ref[...]
        dyz_parts = []
        for g in range(NG):
            gs = slice(g * GW, (g + 1) * GW)
            yg = yz[:, gs]
            r = lax.rsqrt(jnp.mean(yg * yg, axis=-1, keepdims=True) + EPS)
            yh = yg * r
            dn = dyn[:, gs]
            gng_ref[:, gs] += jnp.sum(dn * yh, axis=0, keepdims=True)
            gdn = dn * ng_ref[:, gs]
            dyz_parts.append(r * (gdn - yh * jnp.mean(yh * gdn, axis=-1, keepdims=True)))
        dyz = jnp.concatenate(dyz_parts, axis=1)
        dy = dyz * sz
        dz_ref[...] = (dyz * yv * _dsilu(zv)).astype(dz_ref.dtype)

        dxs = dE * dy
        dzo = ecsE * dy
        dcsL = jnp.zeros((1, 128), F32)
        ddt = jnp.zeros((CH, 128), F32)
        qcols = jnp.zeros((CH, 128), F32)
        qrows = jnp.zeros((128, CH), F32)
        lane = lax.broadcasted_iota(jnp.int32, (1, 128), 1)
        sub = lax.broadcasted_iota(jnp.int32, (128, 1), 0)
        dB_parts, dC_parts, yoff_parts, dxw_parts = [], [], [], []
        for g in range(NG):
            Bg = xbc[:, NH * HP + g * NS: NH * HP + (g + 1) * NS]
            Cg = xbc[:, NH * HP + NG * NS + g * NS: NH * HP + NG * NS + (g + 1) * NS]
            gs = slice(g * GW, (g + 1) * GW)
            hg = hp_ref[0, gs, :]
            dhn = dh[gs, :]
            G = _dot(Cg, Bg, "nt")
            yoff_parts.append(ecsE[:, gs] * _dot(Cg, hg, "nt"))
            dC = _dot(dzo[:, gs], hg)
            dhp = _dot(dzo[:, gs], Cg, "tn") + eLcol[gs, :] * dhn
            t1 = jnp.sum(dhn * hg, axis=1, keepdims=True) * eLcol[gs, :]
            dcsL = dcsL + jnp.sum(et[gs, :] * t1, axis=0, keepdims=True)
            dxw_parts.append(_dot(Bg, dhn, "nt"))
            dB = _dot(xw[:, gs], dhn)
            dgsum = jnp.zeros((CH, CH), F32)
            for hh in range(NH // NG):
                h = g * (NH // NG) + hh
                hs = slice(h * HP, (h + 1) * HP)
                m = jnp.where(tril, jnp.exp(jnp.where(tril, cs[:, h:h + 1] - csT[h:h + 1, :], 0.0)), 0.0)
                sc = G * m
                dyh = dy[:, hs]
                dxd[:, hs] = _dot(sc, dyh, "tn")
                dsc = _dot(dyh, xd[:, hs], "nt")
                q = dsc * sc
                qcols = qcols + jnp.where(lane == h, jnp.sum(q, axis=1, keepdims=True), 0.0)
                qrows = qrows + jnp.where(sub == h, jnp.sum(q, axis=0, keepdims=True), 0.0)
                dgsum = dgsum + dsc * m
            dC_parts.append(dC + _dot(dgsum, Bg))
            dB_parts.append(dB + _dot(dgsum, Cg, "tn"))
            dh[gs, :] = dhp
        yoff = jnp.concatenate(yoff_parts, axis=1)
        dxw = jnp.concatenate(dxw_parts, axis=1)
        dxdv = dxd[...]
        per_head = _dot(jnp.concatenate([dy * yoff, dxw * xs, dxdv * xs, dy * xs], axis=0), et, exact="b")
        dcs = qcols - qrows.T + per_head[0:CH]
        dw = per_head[CH:2 * CH]
        gsc_ref[2:3, :] += jnp.sum(per_head[3 * CH:4 * CH], axis=0, keepdims=True)
        dxs = dxs + wE * dxw + dtE * dxdv
        ddt = ddt + dw * jnp.exp(csL - cs) + per_head[2 * CH:3 * CH]
        dcs = dcs - dw * wdec
        dcsL = dcsL + jnp.sum(dw * wdec, axis=0, keepdims=True)
        last = lax.broadcasted_iota(jnp.int32, (CH, 128), 0) == CH - 1
        dcs = dcs + jnp.where(last, dcsL, 0.0)
        da = _dot(triu, dcs, exact="a", passes=3)
        ddt = ddt + da * A
        gsc_ref[1:2, :] += jnp.sum(da * dt, axis=0, keepdims=True) * A
        valid = lax.broadcasted_iota(jnp.int32, (CH, 128), 1) < NH
        ddtr = jnp.where(valid, ddt * _sigmoid(dtr + dtb), 0.0)
        gsc_ref[0:1, :] += jnp.sum(ddtr, axis=0, keepdims=True)
        ddtr_ref[...] = ddtr.astype(ddtr_ref.dtype)
        dxbc = jnp.concatenate([dxs] + dB_parts + dC_parts, axis=1)
        dxc = dxbc * _dsilu(xbc_c)
        cext[pl.ds(0, CH), :] = dxc
        xp = xp_ref[...]
        acc = jnp.zeros((CH, XBC), F32)
        for k in range(KS):
            sh = cext[pl.ds(KS - 1 - k, CH), :]
            acc = acc + sh * cw_ref[k:k + 1, :]
            gcw_ref[k:k + 1, :] += jnp.sum(xp * sh, axis=0, keepdims=True)
        gcb_ref[...] += jnp.sum(dxc, axis=0, keepdims=True)
        dx_ref[...] = acc.astype(dx_ref.dtype)
        cext[pl.ds(CH, SSD_HALO), :] = cext[pl.ds(0, SSD_HALO), :]
        rd.finish()

    vec = pl.BlockSpec((8, 128), lambda c: (0, 0))
    vecd = pl.BlockSpec((1, D), lambda c: (0, 0))
    cwsp = pl.BlockSpec((KS, XBC), lambda c: (0, 0))
    cbsp = pl.BlockSpec((1, XBC), lambda c: (0, 0))
    row = lambda w, j=0: pl.BlockSpec((CH, w), lambda c: (rev(c), j))
    outs = pl.pallas_call(
        body, name=name, grid=(nc,),
        in_specs=[row(D), row(D), row(D, COL_Z // D), row(XBC), row(XBC, COL_XBC // XBC), row(128, COL_DT // 128),
                  pl.BlockSpec((1, NH * HP, NS), lambda c: (rev(c), 0, 0)), cwsp, vec, vecd] + [HBM_SPEC] * nco,
        out_specs=[row(D), row(XBC), row(128), cwsp, cbsp, vec, vecd] + [HBM_SPEC] * nco,
        out_shape=[SDS((S, D), _MXU), SDS((S, XBC), _MXU), SDS((S, 128), _MXU), SDS((KS, XBC), F32),
                   SDS((1, XBC), F32), SDS((8, 128), F32), SDS((1, D), F32)] + rd.out_shapes(),
        scratch_shapes=[pltpu.VMEM((NH * HP, NS), F32), pltpu.VMEM((CH, NH * HP), F32),
                        pltpu.VMEM((CH + SSD_HALO, XBC), F32)] + rd.scratch(),
        compiler_params=_cp("arbitrary"))(dmix, y, proj, xbc_c, proj, proj, hprev, cw, sc, norm_g, *rd.arrays())
    return outs[:7], rd.split(outs[7:])


CONV_RT = 32


def _fill_phases(ext, ph, rows):
    for s in range(1, 8):
        ph[s - 1, pl.ds(0, rows), :] = ext[pl.ds(s, rows), :]


def _window(ext, ph, off, r0, ls):
    s = off % 8
    src = ext if s == 0 else ph.at[s - 1]
    return src[pl.ds(pl.multiple_of(off - s + r0, 8), CONV_RT), ls]


def _cf_fwd(proj, w, b, lg, lb, riders=(), name="cf_fwd", tb_cap=256):
    S = proj.shape[0]
    tb = _tile(S, tb_cap, 8)
    nb = S // tb
    rd = _Riders(riders)
    nco = len(rd.arrays())

    def body(*refs):
        a_ref, g_ref, w_ref, b_ref, lg_ref, lb_ref = refs[:6]
        u1_ref, u_ref = refs[6 + nco:8 + nco]
        ext, ph = refs[8 + 2 * nco:10 + 2 * nco]
        rd.bind(refs[6:6 + nco], refs[8 + nco:8 + 2 * nco], refs[10 + 2 * nco:], nb).start()

        @pl.when(pl.program_id(0) == 0)
        def _():
            ext[pl.ds(0, CF_HALO), :] = jnp.zeros((CF_HALO, D), F32)

        ext[pl.ds(CF_HALO, tb), :] = a_ref[...] * _sigmoid(g_ref[...])
        _fill_phases(ext, ph, tb + CF_HALO - 8)

        def tile(i, carry):
            r0 = pl.multiple_of(i * CONV_RT, CONV_RT)
            for l in range(D // 128):
                ls = pl.ds(l * 128, 128)
                acc = jnp.broadcast_to(b_ref[:, ls], (CONV_RT, 128))
                for k in range(KC):
                    acc = acc + _window(ext, ph, CF_HALO - (KC - 1) + k, r0, ls) * w_ref[k:k + 1, ls]
                u1_ref[pl.ds(r0, CONV_RT), ls] = acc
            return carry

        lax.fori_loop(0, tb // CONV_RT, tile, 0)
        acc = u1_ref[...]
        mu = jnp.mean(acc, axis=-1, keepdims=True)
        xc = acc - mu
        r = lax.rsqrt(jnp.mean(xc * xc, axis=-1, keepdims=True) + EPS)
        u_ref[...] = _silu(xc * r * lg_ref[...] + lb_ref[...]).astype(u_ref.dtype)
        ext[pl.ds(0, CF_HALO), :] = ext[pl.ds(tb, CF_HALO), :]
        rd.finish()

    vec = pl.BlockSpec((1, D), lambda i: (0, 0))
    outs = pl.pallas_call(
        body, name=name, grid=(nb,),
        in_specs=[pl.BlockSpec((tb, D), lambda i: (i, COL_A // D)), pl.BlockSpec((tb, D), lambda i: (i, COL_G // D)),
                  pl.BlockSpec((KC, D), lambda i: (0, 0)), vec, vec, vec] + [HBM_SPEC] * nco,
        out_specs=[pl.BlockSpec((tb, D), lambda i: (i, 0)), pl.BlockSpec((tb, D), lambda i: (i, 0))] + [HBM_SPEC] * nco,
        out_shape=[SDS((S, D), F32), SDS((S, D), _MXU)] + rd.out_shapes(),
        scratch_shapes=[pltpu.VMEM((CF_HALO + tb, D), F32), pltpu.VMEM((7, tb + CF_HALO - 8, D), F32)] + rd.scratch(),
        compiler_params=_cp("arbitrary"))(proj, proj, w, b, lg, lb, *rd.arrays())
    return outs[:2], rd.split(outs[2:])


def _cf_bwd(dmix, u1, proj, w, lg, lb, riders=(), name="cf_bwd", tb_cap=256):
    S = proj.shape[0]
    tb = _tile(S, tb_cap, 8)
    nb = S // tb
    rd = _Riders(riders)
    nco = len(rd.arrays())
    rev = lambda i: nb - 1 - i

    def body(*refs):
        du_ref, u1_ref, a_ref, g_ref, w_ref, lg_ref, lb_ref = refs[:7]
        da_ref, dg_ref, dw_ref, db_ref, dlg_ref, dlb_ref = refs[7 + nco:13 + nco]
        ext, ph, u0s = refs[13 + 2 * nco:16 + 2 * nco]
        rd.bind(refs[7:7 + nco], refs[13 + nco:13 + 2 * nco], refs[16 + 2 * nco:], nb).start()

        @pl.when(pl.program_id(0) == 0)
        def _():
            ext[pl.ds(tb, CF_HALO), :] = jnp.zeros((CF_HALO, D), F32)
            dw_ref[...] = jnp.zeros_like(dw_ref)
            db_ref[...] = jnp.zeros_like(db_ref)
            dlg_ref[...] = jnp.zeros_like(dlg_ref)
            dlb_ref[...] = jnp.zeros_like(dlb_ref)

        u1 = u1_ref[...]
        mu = jnp.mean(u1, axis=-1, keepdims=True)
        xc = u1 - mu
        r = lax.rsqrt(jnp.mean(xc * xc, axis=-1, keepdims=True) + EPS)
        xh = xc * r
        lgv = lg_ref[...]
        du2 = du_ref[...] * _dsilu(xh * lgv + lb_ref[...])
        dlg_ref[...] += jnp.sum(du2 * xh, axis=0, keepdims=True)
        dlb_ref[...] += jnp.sum(du2, axis=0, keepdims=True)
        gd = du2 * lgv
        du1 = r * (gd - jnp.mean(gd, axis=-1, keepdims=True) - xh * jnp.mean(gd * xh, axis=-1, keepdims=True))
        db_ref[...] += jnp.sum(du1, axis=0, keepdims=True)
        ext[pl.ds(0, tb), :] = du1
        u0s[...] = a_ref[...] * _sigmoid(g_ref[...])
        _fill_phases(ext, ph, tb + CF_HALO - 8)

        for l in range(D // 128):
            ls = pl.ds(l * 128, 128)

            def tile(i, accs, ls=ls):
                r0 = pl.multiple_of(i * CONV_RT, CONV_RT)
                rows = pl.ds(r0, CONV_RT)
                u0t = u0s[rows, ls]
                acc = jnp.zeros((CONV_RT, 128), F32)
                out = []
                for k in range(KC):
                    win = _window(ext, ph, KC - 1 - k, r0, ls)
                    acc = acc + win * w_ref[k:k + 1, ls]
                    p = u0t * win
                    out.append(accs[k] + ((p[0:8] + p[8:16]) + (p[16:24] + p[24:32])))
                sg = _sigmoid(g_ref[rows, ls])
                da_ref[rows, ls] = (acc * sg).astype(da_ref.dtype)
                dg_ref[rows, ls] = (acc * a_ref[rows, ls] * sg * (1.0 - sg)).astype(dg_ref.dtype)
                return tuple(out)

            accs = lax.fori_loop(0, tb // CONV_RT, tile, tuple(jnp.zeros((8, 128), F32) for _ in range(KC)))
            for k in range(KC):
                dw_ref[k:k + 1, ls] += jnp.sum(accs[k], axis=0, keepdims=True)
        ext[pl.ds(tb, CF_HALO), :] = ext[pl.ds(0, CF_HALO), :]
        rd.finish()

    vec = pl.BlockSpec((1, D), lambda i: (0, 0))
    wsp = pl.BlockSpec((KC, D), lambda i: (0, 0))
    row = lambda j=0: pl.BlockSpec((tb, D), lambda i: (rev(i), j))
    outs = pl.pallas_call(
        body, name=name, grid=(nb,),
        in_specs=[row(1), row(), row(COL_A // D), row(COL_G // D), wsp, vec, vec] + [HBM_SPEC] * nco,
        out_specs=[row(), row(), wsp, vec, vec, vec] + [HBM_SPEC] * nco,
        out_shape=[SDS((S, D), _MXU), SDS((S, D), _MXU), SDS((KC, D), F32),
                   SDS((1, D), F32), SDS((1, D), F32), SDS((1, D), F32)] + rd.out_shapes(),
        scratch_shapes=[pltpu.VMEM((tb + CF_HALO, D), F32), pltpu.VMEM((7, tb + CF_HALO - 8, D), F32),
                        pltpu.VMEM((tb, D), F32)] + rd.scratch(),
        compiler_params=_cp("arbitrary"))(dmix, u1, proj, proj, w, lg, lb, *rd.arrays())
    return outs[:6], rd.split(outs[6:])


def _attn_fwd(q, kv, name="attn_fwd", tq_cap=512):
    S = q.shape[0]
    tq = _tile(S, tq_cap, 8)
    scale = XD ** -0.5

    def body(q_ref, kv_ref, o_ref):
        for h in range(XH):
            hs = slice(h * XD, (h + 1) * XD)
            s = _dot(q_ref[:, hs], kv_ref[:, hs], "nt") * scale
            s = s - jnp.max(s, axis=-1, keepdims=True)
            p = jnp.exp(s)
            p = p / jnp.sum(p, axis=-1, keepdims=True)
            o_ref[:, hs] = _dot(p, kv_ref[:, D + h * XD: D + (h + 1) * XD]).astype(o_ref.dtype)

    return pl.pallas_call(
        body, name=name, grid=(S // tq,),
        in_specs=[pl.BlockSpec((tq, D), lambda i: (i, 0)), pl.BlockSpec((MEM, 2 * D), lambda i: (0, 0))],
        out_specs=pl.BlockSpec((tq, D), lambda i: (i, 0)), out_shape=SDS((S, D), _MXU),
        compiler_params=_cp("parallel"))(q, kv)


def _attn_bwd(do, q, kv, riders=(), name="attn_bwd", tq_cap=512):
    S = q.shape[0]
    tq = _tile(S, tq_cap, 8)
    scale = XD ** -0.5
    rd = _Riders(riders)
    nco = len(rd.arrays())

    def body(*refs):
        do_ref, q_ref, kv_ref = refs[:3]
        dq_ref, dkv_ref = refs[3 + nco:5 + nco]
        rd.bind(refs[3:3 + nco], refs[5 + nco:5 + 2 * nco], refs[5 + 2 * nco:], S // tq).start()

        @pl.when(pl.program_id(0) == 0)
        def _():
            dkv_ref[...] = jnp.zeros_like(dkv_ref)

        for h in range(XH):
            hs = slice(h * XD, (h + 1) * XD)
            vs = slice(D + h * XD, D + (h + 1) * XD)
            qh = q_ref[:, hs]
            kh = kv_ref[:, hs]
            s = _dot(qh, kh, "nt") * scale
            s = s - jnp.max(s, axis=-1, keepdims=True)
            p = jnp.exp(s)
            p = p / jnp.sum(p, axis=-1, keepdims=True)
            doh = do_ref[:, hs]
            dp = _dot(doh, kv_ref[:, vs], "nt")
            ds = p * (dp - jnp.sum(dp * p, axis=-1, keepdims=True)) * scale
            dq_ref[:, hs] = _dot(ds, kh).astype(dq_ref.dtype)
            dkv_ref[:, hs] += _dot(ds, qh, "tn")
            dkv_ref[:, vs] += _dot(p, doh, "tn")
        rd.finish()

    outs = pl.pallas_call(
        body, name=name, grid=(S // tq,),
        in_specs=[pl.BlockSpec((tq, D), lambda i: (i, 0)), pl.BlockSpec((tq, D), lambda i: (i, 0)),
                  pl.BlockSpec((MEM, 2 * D), lambda i: (0, 0))] + [HBM_SPEC] * nco,
        out_specs=[pl.BlockSpec((tq, D), lambda i: (i, 0)), pl.BlockSpec((MEM, 2 * D), lambda i: (0, 0))]
        + [HBM_SPEC] * nco,
        out_shape=[SDS((S, D), _MXU), SDS((MEM, 2 * D), F32)] + rd.out_shapes(), scratch_shapes=rd.scratch(),
        compiler_params=_cp("arbitrary"))(do, q, kv, *rd.arrays())
    return outs[:2], rd.split(outs[2:])


def _ffn_in(hf, wg_t, wu_t, name="ffn_in", tm_cap=512, tn_cap=1408):
    S, K = hf.shape
    N = wg_t.shape[0]
    tm, tn = _tile(S, tm_cap, 8), _tile(N, tn_cap)

    def body(a_ref, g_ref, u_ref, act_ref, gt_ref, up_ref):
        a = a_ref[...]
        gt = _dot(a, g_ref[...], "nt")
        up = _dot(a, u_ref[...], "nt")
        act_ref[...] = (_silu(gt) * up).astype(act_ref.dtype)
        gt_ref[...] = gt.astype(gt_ref.dtype)
        up_ref[...] = up.astype(up_ref.dtype)

    wsp = pl.BlockSpec((tn, K), lambda j, i: (j, 0))
    osp = pl.BlockSpec((tm, tn), lambda j, i: (i, j))
    return pl.pallas_call(
        body, name=name, grid=(N // tn, S // tm), in_specs=[pl.BlockSpec((tm, K), lambda j, i: (i, 0)), wsp, wsp],
        out_specs=[osp, osp, osp], out_shape=[SDS((S, N), _MXU)] * 3,
        compiler_params=_cp("parallel", "parallel"))(hf, wg_t, wu_t)


def _ffn_out_bwd(dx, w_down, gt, up, name="ffn_out_dx", tm_cap=512, tk_cap=1408):
    S, N = dx.shape
    K = w_down.shape[0]
    tm, tk = _tile(S, tm_cap, 8), _tile(K, tk_cap)

    def body(a_ref, b_ref, g_ref, u_ref, dg_ref, du_ref):
        d = _dot(a_ref[...], b_ref[...], "nt")
        gt = g_ref[...].astype(F32)
        s = _sigmoid(gt)
        dg_ref[...] = (d * u_ref[...].astype(F32) * (s * (1.0 + gt * (1.0 - s)))).astype(dg_ref.dtype)
        du_ref[...] = (d * gt * s).astype(du_ref.dtype)

    osp = pl.BlockSpec((tm, tk), lambda j, i: (i, j))
    return pl.pallas_call(
        body, name=name, grid=(K // tk, S // tm),
        in_specs=[pl.BlockSpec((tm, N), lambda j, i: (i, 0)), pl.BlockSpec((tk, N), lambda j, i: (j, 0)), osp, osp],
        out_specs=[osp, osp], out_shape=[SDS((S, K), _MXU)] * 2,
        compiler_params=_cp("parallel", "parallel"))(dx, w_down, gt, up)


AG_RIDE = (("w_down",), ("w_out", "w_q", "w_kv", "w_o"), ("w_gate", "w_up"))


def _local_step(x, mem, tgt, W, P, core=None, late=None):
    pair, got = {}, {}
    ride = [[late[n] for n in grp] if late is not None else [] for grp in AG_RIDE]

    def halves(group):
        if core is None:
            return []
        gs = [_shard_grad(n, GW) for n in group]
        return [g.reshape(4, 2, g.shape[1] // 2, g.shape[2]) for g in gs]

    def pair_sums(group, hs, theirs):
        ps = [_pair_sum(h_, t, core, "rs_pair_sum_" + n) for h_, t, n in zip(hs, theirs, group)]
        pair.update(zip(group, ps))
        return ps

    h = _rms_fwd(x, P["g_mix"], "rms_mix")
    proj, (bufs0,) = _mm_nn(h, W["main"], "in_proj", tm_cap=256, tn_cap=MAINW, riders=[_Rider("gather", ride[0])])
    (xbc_c, y, yn, hprev), (bufs1,) = _ssd_fwd(proj, P["conv4_w"], P["conv4_b"], P["sc"], P["ssd_norm_g"],
                                                riders=[_Rider("gather", ride[1])])
    (u1, u), (bufs2,) = _cf_fwd(proj, P["cf_w"], P["cf_b"], P["ln_g"], P["ln_b"], riders=[_Rider("gather", ride[2])])
    if late is not None:
        names = AG_RIDE[0] + AG_RIDE[1] + AG_RIDE[2]
        full = _gather_finish_list(ride[0] + ride[1] + ride[2], bufs0 + bufs1 + bufs2)
        W = dict(W, **_pack_late(dict(zip(names, full))))
    mix = jnp.concatenate([yn, u], axis=1)
    x1, hq = _mm_nn(mix, W["out"], "out_proj", add=x, tm_cap=512, tn_cap=D, norm=P["g_xattn"])
    q = _mm_nn(hq, W["q"], "q_proj")
    mn = _rms_fwd(mem, P["g_mem"], "rms_mem")
    kv = _mm_nn(mn, W["kv"], "kv_proj")
    o = _attn_fwd(q, kv)
    x2, hf = _mm_nn(o, W["o"], "o_proj", add=x1, tm_cap=512, tn_cap=D, norm=P["g_ffn"])
    act, gt, up = _ffn_in(hf, W["gate_t"], W["up_t"])
    loss, dx3, dx3b, g_final = _mm_nn(act, W["down"], "ffn_out", add=x2, tm_cap=512, tn_cap=D,
                                      loss=(P["g_final"], tgt))
    GW, GP = {}, {"g_final": g_final}
    GW["down"] = _mm_tn(act, dx3b, "ffn_out_dw", tk_cap=1408, tn_cap=1024)
    dgt, dup = _ffn_out_bwd(dx3b, W["down"], gt, up)
    dhf = _mm_nn(dgt, W["gate_t"], "ffn_gate_dx", tm_cap=512)
    dx2, dx2b, GP["g_ffn"] = _mm_nn(dup, W["up_t"], "ffn_up_dx", add=dhf, tm_cap=512, tn_cap=D,
                                    rms=(x2, P["g_ffn"], dx3, True))
    GW["gate_t"] = _mm_tn(dgt, hf, "ffn_gate_dw", tk_cap=1408, tn_cap=1024)
    GW["up_t"] = _mm_tn(dup, hf, "ffn_up_dw", tk_cap=1408, tn_cap=1024)
    ffn_halves = halves(RS_GROUPS[0])
    do = _mm_nt(dx2b, W["o"], "o_proj_dx")
    GW["o"] = _mm_tn(o, dx2b, "o_proj_dw")
    (dq, dkv), (ffn_theirs,) = _attn_bwd(do, q, kv, riders=[_Rider("pair", ffn_halves)])
    ffn_pieces = pair_sums(RS_GROUPS[0], ffn_halves, ffn_theirs)
    dx1, dx1b, GP["g_xattn"] = _mm_nt(dq, W["q"], "q_proj_dx", tk_cap=D, rms=(x1, P["g_xattn"], dx2, True))
    GW["q"] = _mm_tn(hq, dq, "q_proj_dw")
    dkvb = dkv.astype(_MXU)
    GW["kv"] = _mm_tn(mn, dkvb, "kv_proj_dw", tm_cap=256)
    dmn = _mm_nt(dkvb, W["kv"], "kv_proj_dx")
    GP["g_mem"] = _rms_bwd(mem, P["g_mem"], dmn, None, "rms_mem_bwd")
    dmix = _mm_nt(dx1b, W["out"], "out_proj_dx")
    GW["out"] = _mm_tn(mix, dx1b, "out_proj_dw", tn_cap=1024)
    attn_halves = halves(RS_GROUPS[1])
    (da, dg, GP["cf_w"], GP["cf_b"], GP["ln_g"], GP["ln_b"]), (came, attn_theirs) = _cf_bwd(
        dmix, u1, proj, P["cf_w"], P["ln_g"], P["ln_b"],
        riders=[_Rider("exchange", ffn_pieces), _Rider("pair", attn_halves)])
    got.update(zip(RS_GROUPS[0], came))
    attn_pieces = pair_sums(RS_GROUPS[1], attn_halves, attn_theirs)
    (dz, dxbc, ddtr, GP["conv4_w"], GP["conv4_b"], GP["sc"], GP["ssd_norm_g"]), (came,) = _ssd_bwd(
        dmix, y, proj, xbc_c, hprev, P["conv4_w"], P["sc"], P["ssd_norm_g"],
        riders=[_Rider("exchange", attn_pieces)])
    got.update(zip(RS_GROUPS[1], came))
    dproj = jnp.concatenate([dz, da, dg, dxbc, ddtr], axis=1)
    GW["main"] = _mm_tn(h, dproj, "in_proj_dw", tm_cap=512, tk_cap=512, tn_cap=MAINW)
    in_halves = halves(RS_GROUPS[2])
    in_pieces = pair_sums(RS_GROUPS[2], in_halves, _pair_split_list(in_halves, "rs_pair_send_w_in")) if in_halves else []
    (grad_x, GP["g_mix"]), (came,) = _mm_nt(dproj, W["main"], "in_proj_dx", tm_cap=256, tk_cap=D,
                                            riders=[_Rider("exchange", in_pieces)], rms=(x, P["g_mix"], dx1, False))
    got.update(zip(RS_GROUPS[2], came))
    if core is None:
        return loss, grad_x, GW, GP
    return loss, grad_x, GW, GP, pair, got


Z_END, XBC_END, DT_END = NH * HP, NH * HP + XBC, NH * HP + XBC + NH


def _pad_to(a, rows=None, cols=None):
    r = 0 if rows is None else rows - a.shape[0]
    c = 0 if cols is None else cols - a.shape[1]
    return jnp.pad(a, ((0, r), (0, c)))


IN_W = DT_END + 2 * D
W_IN_SEGS = [(0, Z_END, "main", COL_Z), (Z_END, XBC_END, "main", COL_XBC), (XBC_END, DT_END, "main", COL_DT),
             (DT_END, DT_END + D, "main", COL_A), (DT_END + D, IN_W, "main", COL_G)]
BIG = [("w_in", True), ("w_out", False), ("w_q", False), ("w_kv", True), ("w_o", False), ("w_gate", False),
       ("w_up", False), ("w_down", False)]
TRANSPOSED = ("w_gate", "w_up")


def _ref_cols(pieces, a, b):
    cw = IN_W // 4
    out = []
    for j in range(4):
        lo, hi = max(a, j * cw), min(b, (j + 1) * cw)
        if lo < hi:
            out.append(pieces[j][:, lo - j * cw:hi - j * cw])
    return out


def _cat_cols(pieces):
    return jnp.concatenate([pieces[j] for j in range(4)], axis=1)


def _pack_in(w_in):
    dt = _ref_cols(w_in, XBC_END, DT_END)
    pad = jnp.zeros((dt[0].shape[0], MAINW - COL_DT - NH), dt[0].dtype)
    main = jnp.concatenate(_ref_cols(w_in, 0, Z_END) + _ref_cols(w_in, DT_END, IN_W) + _ref_cols(w_in, Z_END, XBC_END)
                           + dt + [pad], axis=1)
    return {"main": main}


def _pack_late(pc):
    rows = lambda n: pc[n].reshape(-1, pc[n].shape[-1])
    return {"out": rows("w_out"), "q": rows("w_q"), "kv": _cat_cols(pc["w_kv"]), "o": rows("w_o"),
            "gate_t": rows("w_gate"), "up_t": rows("w_up"), "down": rows("w_down")}


GW_KEY = {"w_gate": "gate_t", "w_up": "up_t", "w_kv": "kv", "w_out": "out", "w_q": "q", "w_o": "o", "w_down": "down"}
RS_GROUPS = (("w_down", "w_gate", "w_up"), ("w_out", "w_q", "w_kv", "w_o"), ("w_in",))


def _shard_grad(name, GW):
    if name == "w_in":
        cw = IN_W // 4
        pieces = []
        for j in range(4):
            parts = []
            for a, b, src, col in W_IN_SEGS:
                lo, hi = max(a, j * cw), min(b, (j + 1) * cw)
                if lo < hi:
                    parts.append(GW[src][:, col + lo - a:col + hi - a])
            pieces.append(jnp.concatenate(parts, axis=1))
        return jnp.stack(pieces)
    g = GW[GW_KEY[name]]
    if dict(BIG)[name]:
        cw = g.shape[1] // 4
        return jnp.stack([g[:, j * cw:(j + 1) * cw] for j in range(4)])
    return g.reshape(4, g.shape[0] // 4, g.shape[1])


def _stack_sc(dt_bias, a_log, d):
    return _pad_to(jnp.concatenate([dt_bias, a_log, d], axis=0), rows=8, cols=128)


COMM_PARAMS = pltpu.CompilerParams(vmem_limit_bytes=VMEM_LIMIT)


def _dma_sems(*counts):
    return [pltpu.SemaphoreType.DMA((n,)) for n in counts]


def _allgather_list(arrs, name):
    n = len(arrs)
    halved = [a.shape[0] % 16 == 0 for a in arrs]
    oshape = [(4, 2, a.shape[0] // 2, a.shape[1]) if h else (4, 1) + a.shape for a, h in zip(arrs, halved)]

    def body(*refs):
        srcs, outs = refs[:n], refs[n:2 * n]
        ici_send, ici_recv, own_send, own_recv, fwd_send, fwd_recv = refs[2 * n:]
        x, y, c = lax.axis_index("x"), lax.axis_index("y"), lax.axis_index("c")
        me = 2 * x + y
        sib = (x, y, 1 - c)
        peers = _chip_peers(x, y)

        def half(i, h):
            r = arrs[i].shape[0] // 2
            if not halved[i]:
                return srcs[i]
            return srcs[i].at[pl.ds(h * r if isinstance(h, int) else pl.multiple_of(h * r, 8), r)]

        ici, own, fwd = [], [], []
        for i in range(n):
            mine_h = c if halved[i] else 0
            for k, (px, py) in enumerate(peers):
                s = 3 * i + k
                ici.append(_remote(half(i, c), outs[i].at[me, mine_h], ici_send.at[s], ici_recv.at[s], (px, py, c)))
            for h in range(2 if halved[i] else 1):
                s = 2 * i + h
                own.append(_remote(half(i, h), outs[i].at[me, h], own_send.at[s], own_recv.at[s], sib))
        for cp in ici + own:
            cp.start()
        for i in range(n):
            if not halved[i]:
                continue
            for k, (px, py) in enumerate(peers):
                s = 3 * i + k
                got = outs[i].at[2 * px + py, c]
                _remote(half(i, c), got, ici_send.at[s], ici_recv.at[s], (px, py, c)).wait_recv()
                f = _remote(got, got, fwd_send.at[s], fwd_recv.at[s], sib)
                f.start()
                fwd.append(f)
        for i in range(n):
            for k, (px, py) in enumerate(peers):
                s = 3 * i + k
                if halved[i]:
                    _remote(half(i, c), outs[i].at[2 * px + py, 1 - c], fwd_send.at[s], fwd_recv.at[s], sib).wait_recv()
                else:
                    _remote(srcs[i], outs[i].at[2 * px + py, 0], ici_send.at[s], ici_recv.at[s], (px, py, c)).wait_recv()
            for h in range(2 if halved[i] else 1):
                s = 2 * i + h
                _remote(half(i, h), outs[i].at[me, h], own_send.at[s], own_recv.at[s], sib).wait_recv()
        for cp in ici + own + fwd:
            cp.wait_send()

    outs = pl.pallas_call(
        body, name=name, in_specs=[HBM_SPEC] * n, out_specs=[HBM_SPEC] * n,
        out_shape=[SDS(s, a.dtype) for s, a in zip(oshape, arrs)],
        scratch_shapes=_dma_sems(3 * n, 3 * n, 2 * n, 2 * n, 3 * n, 3 * n), compiler_params=COMM_PARAMS)(*arrs)
    return [o.reshape((4,) + a.shape) for o, a in zip(outs, arrs)]


def _pair_split_list(gs, name):
    n = len(gs)

    def body(*refs):
        sends, recvs = _pair_copies(refs[:n], refs[n:2 * n], *refs[2 * n:])
        for cp in sends:
            cp.start()
        for cp in recvs:
            cp.wait_recv()
        for cp in sends:
            cp.wait_send()

    return pl.pallas_call(
        body, name=name, in_specs=[HBM_SPEC] * n, out_specs=[HBM_SPEC] * n,
        out_shape=[SDS((4,) + g.shape[2:], g.dtype) for g in gs],
        scratch_shapes=_dma_sems(4 * n, 4 * n), compiler_params=COMM_PARAMS)(*gs)


def _gather_finish_list(shards, bufs, name="allgather_finish"):
    n = len(shards)

    def body(*refs):
        srcs, outs = refs[:n], refs[2 * n:3 * n]
        own_send, own_recv, fwd_send, fwd_recv = refs[3 * n:]
        x, y, c = lax.axis_index("x"), lax.axis_index("y"), lax.axis_index("c")
        me = 2 * x + y
        sib = (x, y, 1 - c)
        sends, recvs = [], []
        for i in range(n):
            for h in range(2):
                own = _remote(_rows_half(srcs[i], shards[i].shape[0], h), outs[i].at[me, h],
                              own_send.at[2 * i + h], own_recv.at[2 * i + h], sib)
                sends.append(own)
                recvs.append(own)
            for k, (px, py) in enumerate(_chip_peers(x, y)):
                got, s = outs[i].at[2 * px + py, c], 3 * i + k
                sends.append(_remote(got, got, fwd_send.at[s], fwd_recv.at[s], sib))
                recvs.append(_remote(got, outs[i].at[2 * px + py, 1 - c], fwd_send.at[s], fwd_recv.at[s], sib))
        for cp in sends:
            cp.start()
        for cp in recvs:
            cp.wait_recv()
        for cp in sends:
            cp.wait_send()

    outs = pl.pallas_call(
        body, name=name, in_specs=[HBM_SPEC] * (2 * n), out_specs=[HBM_SPEC] * n,
        out_shape=[SDS(b.shape, b.dtype) for b in bufs], input_output_aliases={n + i: i for i in range(n)},
        scratch_shapes=_dma_sems(2 * n, 2 * n, 3 * n, 3 * n), compiler_params=COMM_PARAMS)(*shards, *bufs)
    return [o.reshape((4,) + a.shape) for o, a in zip(outs, shards)]


JOIN_SPLIT = 4


def _pair_join_list(bufs, name="rs_pair_join"):
    n = len(bufs)

    def body(*refs):
        outs = refs[n:2 * n]
        send_sems, recv_sems = refs[2 * n:]
        x, y, c = lax.axis_index("x"), lax.axis_index("y"), lax.axis_index("c")
        sib = (x, y, 1 - c)
        sends, recvs = [], []
        for i in range(n):
            rc = bufs[i].shape[1] // JOIN_SPLIT
            for q in range(JOIN_SPLIT):
                k = JOIN_SPLIT * i + q
                rows = pl.ds(q * rc, rc)
                sends.append(_remote(outs[i].at[c, rows], outs[i].at[c, rows], send_sems.at[k], recv_sems.at[k], sib))
                recvs.append(_remote(outs[i].at[c, rows], outs[i].at[1 - c, rows], send_sems.at[k], recv_sems.at[k], sib))
        for cp in sends:
            cp.start()
        for cp in recvs:
            cp.wait_recv()
        for cp in sends:
            cp.wait_send()

    return pl.pallas_call(
        body, name=name, in_specs=[HBM_SPEC] * n, out_specs=[HBM_SPEC] * n,
        out_shape=[SDS(b.shape, b.dtype) for b in bufs], input_output_aliases={i: i for i in range(n)},
        scratch_shapes=_dma_sems(JOIN_SPLIT * n, JOIN_SPLIT * n), compiler_params=COMM_PARAMS)(*bufs)


def _pair_sum(g, theirs, core, name):
    _, _, r, c = g.shape

    def body(core_ref, g_ref, t_ref, o_ref):
        o_ref[...] = (g_ref[...] + t_ref[...]).astype(o_ref.dtype)

    spec = pltpu.PrefetchScalarGridSpec(
        num_scalar_prefetch=1, grid=(4,),
        in_specs=[pl.BlockSpec((None, None, r, c), lambda j, core_ref: (j, core_ref[0], 0, 0)),
                  pl.BlockSpec((None, r, c), lambda j, core_ref: (j, 0, 0))],
        out_specs=pl.BlockSpec((None, r, c), lambda j, core_ref: (j, 0, 0)))
    return pl.pallas_call(body, name=name, grid_spec=spec, out_shape=SDS((4, r, c), BF16),
                          compiler_params=_cp("parallel"))(core, g, theirs)


def _chip_sum(own, got, where, name):
    _, r, c = own.shape
    tr = r // 2

    def body(w_ref, a_ref, b1_ref, b2_ref, b3_ref, o_ref):
        o_ref[...] = ((a_ref[...].astype(F32) + b1_ref[...].astype(F32)) + b2_ref[...].astype(F32)) + b3_ref[...].astype(F32)

    piece = lambda k: pl.BlockSpec((None, tr, c), lambda i, w_ref: ((w_ref[0] + k) % 4, i, 0))
    spec = pltpu.PrefetchScalarGridSpec(
        num_scalar_prefetch=1, grid=(r // tr,), in_specs=[piece(0), piece(1), piece(2), piece(3)],
        out_specs=pl.BlockSpec((None, tr, c), lambda i, w_ref: (w_ref[1], i, 0)))
    return pl.pallas_call(body, name=name, grid_spec=spec, out_shape=SDS((2, r, c), F32),
                          compiler_params=_cp("parallel"))(where, own, got, got, got)


ADAM_BLOCK = 2 ** 19


def _adam_math(w, g, m, v):
    bc1 = 1.0 - ADAM_B1 ** ADAM_STEP
    bc2 = 1.0 - ADAM_B2 ** ADAM_STEP
    mn = ADAM_B1 * m + (1.0 - ADAM_B1) * g
    vn = ADAM_B2 * v + (1.0 - ADAM_B2) * (g * g)
    return -ADAM_LR * ((mn / bc1) / (jnp.sqrt(vn / bc2) + ADAM_EPS) + ADAM_WD * w), mn, vn


PACK_COLS = XBC
PACK = {"g_mix": (0, 1, D), "g_xattn": (1, 1, D), "g_mem": (2, 1, D), "g_ffn": (3, 1, D), "g_final": (4, 1, D),
        "ssd_norm_g": (5, 1, D), "cf_b": (6, 1, D), "ln_g": (7, 1, D), "ln_b": (8, 1, D), "conv4_b": (9, 1, XBC),
        "conv4_w": (10, KS, XBC), "sc": (16, 8, 128), "cf_w": (24, KC, D), "loss": (55, 1, 128)}
PACK_ROWS = 56
SMALL_ADAM = ["g_mix", "g_xattn", "g_mem", "g_ffn", "g_final", "ssd_norm_g", "cf_b", "ln_g", "ln_b", "conv4_b", "sc"]


def _small_allreduce_adamw(grads, wts, mom, var, name="allreduce_small"):
    gk = list(PACK)
    ng, na = len(gk), len(SMALL_ADAM)

    def body(*refs):
        g_in = refs[:ng]
        w_in, m_in, v_in = (refs[ng + i * na: ng + (i + 1) * na] for i in range(3))
        o = refs[ng + 3 * na:]
        g_out = o[:ng]
        d_out, m_out, v_out = (o[ng + i * na: ng + (i + 1) * na] for i in range(3))
        pack, pbuf, psum, cbuf, acc, send_sems, recv_sems = o[ng + 3 * na:]
        x, y, c = lax.axis_index("x"), lax.axis_index("y"), lax.axis_index("c")
        me = 2 * x + y
        pack[...] = jnp.zeros_like(pack)
        for i, k in enumerate(gk):
            r0, nr, nc = PACK[k]
            pack[r0:r0 + nr, 0:nc] = g_in[i][...]
        pair = _remote(pack, pbuf.at[c], send_sems.at[0], recv_sems.at[0], (x, y, 1 - c))
        pair.start()
        pbuf[c] = pack[...]
        _remote(pack, pbuf.at[1 - c], send_sems.at[0], recv_sems.at[0], (x, y, 1 - c)).wait_recv()
        pair.wait_send()
        psum[...] = pbuf[0] + pbuf[1]
        peers = _chip_peers(x, y)
        sends = [_remote(psum, cbuf.at[me], send_sems.at[1 + k], recv_sems.at[1 + k], (px, py, c))
                 for k, (px, py) in enumerate(peers)]
        for cp in sends:
            cp.start()
        cbuf[me] = psum[...]
        for k, (px, py) in enumerate(peers):
            _remote(psum, cbuf.at[2 * px + py], send_sems.at[1 + k], recv_sems.at[1 + k], (px, py, c)).wait_recv()
        for cp in sends:
            cp.wait_send()
        acc[...] = (cbuf[0] + cbuf[1]) + (cbuf[2] + cbuf[3])
        for i, k in enumerate(gk):
            r0, nr, nc = PACK[k]
            g_out[i][...] = acc[r0:r0 + nr, 0:nc]
        for i, k in enumerate(SMALL_ADAM):
            r0, nr, nc = PACK[k]
            d_out[i][...], m_out[i][...], v_out[i][...] = _adam_math(
                w_in[i][...], acc[r0:r0 + nr, 0:nc], m_in[i][...], v_in[i][...])

    args = [grads[k] for k in gk] + [d[k] for d in (wts, mom, var) for k in SMALL_ADAM]
    shp = lambda k: SDS((PACK[k][1], PACK[k][2]), F32)
    vm = pl.BlockSpec(memory_space=pltpu.VMEM)
    outs = pl.pallas_call(
        body, name=name, in_specs=[vm] * len(args), out_specs=[vm] * (ng + 3 * na),
        out_shape=[shp(k) for k in gk] + [shp(k) for _ in range(3) for k in SMALL_ADAM],
        scratch_shapes=[pltpu.VMEM((PACK_ROWS, PACK_COLS), F32), pltpu.VMEM((2, PACK_ROWS, PACK_COLS), F32),
                        pltpu.VMEM((PACK_ROWS, PACK_COLS), F32), pltpu.VMEM((4, PACK_ROWS, PACK_COLS), F32),
                        pltpu.VMEM((PACK_ROWS, PACK_COLS), F32)] + _dma_sems(4, 4),
        compiler_params=COMM_PARAMS)(*args)
    red = dict(zip(gk, outs[:ng]))
    parts = [dict(zip(SMALL_ADAM, outs[ng + i * na: ng + (i + 1) * na])) for i in range(3)]
    return red, parts[0], parts[1], parts[2]


def _adamw_cols(w, gfull, m, v, chip, name):
    _, R, C = w.shape

    def body(w_idx, w_ref, g_ref, m_ref, v_ref, go_ref, d_ref, mo_ref, vo_ref):
        go_ref[...] = g_ref[...]
        d_ref[...], mo_ref[...], vo_ref[...] = _adam_math(w_ref[...], g_ref[...], m_ref[...], v_ref[...])

    blk = pl.BlockSpec((None, R, C), lambda i, w_idx: (0, 0, 0))
    spec = pltpu.PrefetchScalarGridSpec(
        num_scalar_prefetch=1, grid=(1,),
        in_specs=[blk, pl.BlockSpec((R, C), lambda i, w_idx: (0, w_idx[0])), blk, blk], out_specs=[blk] * 4)
    return pl.pallas_call(body, name=name, grid_spec=spec, out_shape=[SDS((1, R, C), F32)] * 4,
                          compiler_params=_cp("arbitrary"))(chip, w, gfull, m, v)


def _adamw(w, g, m, v, name):
    _, R, C = w.shape
    half = R // 2
    tr = _tile(half, max(8, (ADAM_BLOCK // C) // 8 * 8), 8)
    nh = half // tr

    def body(w_ref, g_ref, m_ref, v_ref, go_ref, d_ref, mo_ref, vo_ref):
        go_ref[...] = g_ref[...]
        d_ref[...], mo_ref[...], vo_ref[...] = _adam_math(w_ref[...], g_ref[...], m_ref[...], v_ref[...])

    blk = pl.BlockSpec((None, tr, C), lambda i: (0, i, 0))
    gblk = pl.BlockSpec((None, tr, C), lambda i: (i // nh, i % nh, 0))
    return pl.pallas_call(body, name=name, grid=(R // tr,), in_specs=[blk, gblk, blk, blk], out_specs=[blk] * 4,
                          out_shape=[SDS((1, R, C), F32)] * 4, compiler_params=_cp("parallel"))(w, g, m, v)


WEIGHT_NAMES = ["norm_mix_g", "w_in", "ssd_conv_w", "ssd_conv_b", "ssd_dt_bias", "ssd_A_log", "ssd_D", "ssd_norm_g",
                "cf_conv_w", "cf_conv_b", "cf_ln_g", "cf_ln_b", "w_out", "norm_xattn_g", "norm_mem_g", "w_q", "w_kv",
                "w_o", "norm_ffn_g", "w_gate", "w_up", "w_down", "norm_final_g"]
VEC_REF = [("norm_mix_g", "g_mix"), ("norm_xattn_g", "g_xattn"), ("norm_mem_g", "g_mem"), ("norm_ffn_g", "g_ffn"),
           ("norm_final_g", "g_final"), ("ssd_norm_g", "ssd_norm_g"), ("cf_conv_b", "cf_b"), ("cf_ln_g", "ln_g"),
           ("cf_ln_b", "ln_b"), ("ssd_conv_b", "conv4_b")]
SC_REF = ["ssd_dt_bias", "ssd_A_log", "ssd_D"]


def _small_side(get):
    d = {k: get(ref_name).reshape(1, -1) for ref_name, k in VEC_REF}
    d["sc"] = _stack_sc(*[get(n) for n in SC_REF])
    return d


def kernel(x, mem, norm_mix_g, w_in, ssd_conv_w, ssd_conv_b, ssd_dt_bias, ssd_A_log, ssd_D, ssd_norm_g, cf_conv_w, cf_conv_b, cf_ln_g, cf_ln_b, w_out, norm_xattn_g, norm_mem_g, w_q, w_kv, w_o, norm_ffn_g, w_gate, w_up, w_down, norm_final_g, loss_target, m_norm_mix_g, m_w_in, m_ssd_conv_w, m_ssd_conv_b, m_ssd_dt_bias, m_ssd_A_log, m_ssd_D, m_ssd_norm_g, m_cf_conv_w, m_cf_conv_b, m_cf_ln_g, m_cf_ln_b, m_w_out, m_norm_xattn_g, m_norm_mem_g, m_w_q, m_w_kv, m_w_o, m_norm_ffn_g, m_w_gate, m_w_up, m_w_down, m_norm_final_g, v_norm_mix_g, v_w_in, v_ssd_conv_w, v_ssd_conv_b, v_ssd_dt_bias, v_ssd_A_log, v_ssd_D, v_ssd_norm_g, v_cf_conv_w, v_cf_conv_b, v_cf_ln_g, v_cf_ln_b, v_w_out, v_norm_xattn_g, v_norm_mem_g, v_w_q, v_w_kv, v_w_o, v_norm_ffn_g, v_w_gate, v_w_up, v_w_down, v_norm_final_g):
    env = dict(locals())
    view = lambda n, a: a.transpose(0, 2, 1) if n in TRANSPOSED else a
    wts = {n: view(n, env[n]) for n in WEIGHT_NAMES}
    mom = {n: view(n, env["m_" + n]) for n in WEIGHT_NAMES}
    var = {n: view(n, env["v_" + n]) for n in WEIGHT_NAMES}
    chip = (2 * lax.axis_index("x") + lax.axis_index("y")).astype(jnp.int32).reshape(1)
    core = lax.axis_index("c").astype(jnp.int32).reshape(1)
    where = jnp.concatenate([chip, core])
    big = [n for n, _ in BIG]

    w_in_g, conv4_g, cf_g = _allgather_list([w_in[0].astype(BF16), ssd_conv_w[0], cf_conv_w[0]], "allgather_first")
    W = _pack_in(w_in_g)
    P = _small_side(lambda n: wts[n])
    P["conv4_w"], P["cf_w"] = _cat_cols(conv4_g), _cat_cols(cf_g)
    late = {n: wts[n][0].astype(BF16) for grp in AG_RIDE for n in grp}

    loss, grad_x, GW, GP, pair, got = _local_step(x[0], mem[0], loss_target[0], W, P, core, late)
    joined = _pair_join_list([_chip_sum(pair[n], got[n], where, "rs_chip_sum_" + n) for n in big])
    gshard = dict(zip(big, joined))

    small = dict(GP)
    small["loss"] = loss
    red, sd, sm, sv = _small_allreduce_adamw(small, {k: P[k] for k in SMALL_ADAM}, _small_side(lambda n: mom[n]),
                                             _small_side(lambda n: var[n]))
    grads, delta, new_m, new_v = {}, {}, {}, {}
    for ref_name, k in VEC_REF:
        shp = wts[ref_name].shape
        for dst, src in ((grads, red), (delta, sd), (new_m, sm), (new_v, sv)):
            dst[ref_name] = src[k].reshape(shp)
    for row, ref_name in enumerate(SC_REF):
        for dst, src in ((grads, red), (delta, sd), (new_m, sm), (new_v, sv)):
            dst[ref_name] = src["sc"][row:row + 1, :NH]

    for n, k in (("ssd_conv_w", "conv4_w"), ("cf_conv_w", "cf_w")):
        grads[n], delta[n], new_m[n], new_v[n] = _adamw_cols(wts[n], red[k], mom[n], var[n], chip, "adamw_" + n)
    for n in big:
        outs = _adamw(wts[n], gshard[n], mom[n], var[n], "adamw_" + n)
        grads[n], delta[n], new_m[n], new_v[n] = [view(n, o) for o in outs]

    return (red["loss"][0, 0], grad_x[None], *[grads[n] for n in WEIGHT_NAMES], *[delta[n] for n in WEIGHT_NAMES],
            *[new_m[n] for n in WEIGHT_NAMES], *[new_v[n] for n in WEIGHT_NAMES])
```

```python
import functools
import math

import jax
import jax.numpy as jnp
from jax import lax
from jax.experimental import pallas as pl
from jax.experimental.pallas import tpu as pltpu

F32 = jnp.float32
BF16 = jnp.bfloat16
_MXU = BF16

D = 1024
MEM = 256
NH, HP, NG, NS = 16, 64, 2, 128
GW = NH * HP // NG
CH = 128
XBC = NH * HP + 2 * NG * NS
KS, KC = 4, 31
XH, XD = 4, 256
DFF = 2816
EPS = 1e-6
COL_Z, COL_A, COL_G, COL_XBC, COL_DT, MAINW = 0, 1024, 2048, 3072, 4608, 4736
VMEM_LIMIT = 56 * 2 ** 20

ADAM_LR, ADAM_B1, ADAM_B2, ADAM_EPS, ADAM_WD, ADAM_STEP = 0.001, 0.9, 0.999, 1e-08, 0.01, 10

SDS = jax.ShapeDtypeStruct
MESHID = pl.DeviceIdType.MESH


def _cp(*sem):
    return pltpu.CompilerParams(dimension_semantics=sem, vmem_limit_bytes=VMEM_LIMIT)


def _tile(n, cap, unit=128):
    if n <= cap:
        return n
    best = None
    for t in range(unit, cap + 1, unit):
        if n % t == 0:
            best = t
    assert best is not None, (n, cap)
    return best


def _sigmoid(x):
    return 1.0 / (1.0 + jnp.exp(-x))


def _silu(x):
    return x * _sigmoid(x)


def _dsilu(x):
    s = _sigmoid(x)
    return s * (1.0 + x * (1.0 - s))


def _softplus(x):
    return jnp.maximum(x, 0.0) + jnp.log(1.0 + jnp.exp(-jnp.abs(x)))


def _split_bf16(x, passes):
    parts, r = [], x.astype(F32)
    for _ in range(passes):
        p = r.astype(BF16)
        parts.append(p)
        r = r - p.astype(F32)
    return parts


def _dot(a, b, dims=None, exact=None, passes=2):
    dn = {None: (((1,), (0,)), ((), ())), "nt": (((1,), (1,)), ((), ())), "tn": (((0,), (0,)), ((), ()))}[dims]
    if exact is None:
        return lax.dot_general(a.astype(_MXU), b.astype(_MXU), dn, preferred_element_type=F32)
    if exact == "a":
        terms = [(a.astype(BF16), p) for p in _split_bf16(b, passes)]
    else:
        terms = [(p, b.astype(BF16)) for p in _split_bf16(a, passes)]
    out = None
    for lhs, rhs in terms:
        d = lax.dot_general(lhs, rhs, dn, preferred_element_type=F32)
        out = d if out is None else out + d
    return out


def _rms_bwd_tile(xv, gv, dy, dres):
    r = lax.rsqrt(jnp.mean(xv * xv, axis=-1, keepdims=True) + EPS)
    xh = xv * r
    gdy = dy * gv
    dx = r * (gdy - xh * jnp.mean(xh * gdy, axis=-1, keepdims=True))
    return dres + dx, jnp.sum(dy * xh, axis=0, keepdims=True)


def _matmul(kind, a, b, name, add, out_dtype, tm, tw, riders, rms, norm=None, loss=None):
    a_parts = list(a) if isinstance(a, (list, tuple)) else [a]
    na = len(a_parts)
    M, K = a_parts[0].shape[0], sum(p.shape[1] for p in a_parts)
    Wd = b.shape[1] if kind == "nn" else b.shape[0]
    rd = _Riders(riders or ())
    nco = len(rd.arrays())
    nin = na + 1 + (add is not None) + (3 if rms else 0) + (norm is not None) + (2 if loss else 0)
    low = bool(rms and rms[3])
    nout = (2 + low) if rms else 2 if norm is not None else 4 if loss else 1
    grid = (Wd // tw, M // tm)
    assert not (rms or loss or norm is not None) or tw == Wd, "the row-wise epilogues need whole rows"

    def body(*refs):
        b_ref = refs[na]
        av = refs[0][...] if na == 1 else jnp.concatenate([r[...] for r in refs[:na]], axis=1)
        outs = refs[nin + nco:nin + nco + nout]
        rd.bind(refs[nin:nin + nco], refs[nin + nco + nout:nin + 2 * nco + nout], refs[nin + 2 * nco + nout:], grid).start()
        acc = _dot(av, b_ref[...], None if kind == "nn" else "nt")
        if add is not None:
            acc = acc + refs[na + 1][...]
        if loss:
            lpart, dx, dg = _final_loss_tile(acc, refs[nin - 2][...], refs[nin - 1][...])

            @pl.when(pl.program_id(1) == 0)
            def _():
                outs[0][...] = jnp.zeros_like(outs[0])
                outs[3][...] = jnp.zeros_like(outs[3])

            outs[0][...] += lpart
            outs[1][...] = dx
            outs[2][...] = dx.astype(outs[2].dtype)
            outs[3][...] += dg
        elif norm is not None:
            outs[0][...] = acc.astype(outs[0].dtype)
            r = lax.rsqrt(jnp.mean(acc * acc, axis=-1, keepdims=True) + EPS)
            outs[1][...] = (acc * r * refs[nin - 1][...]).astype(outs[1].dtype)
        elif rms:
            x_ref, g_ref, dres_ref = refs[nin - 3:nin]
            tot, dg = _rms_bwd_tile(x_ref[...], g_ref[...], acc, dres_ref[...])

            @pl.when(pl.program_id(1) == 0)
            def _():
                outs[-1][...] = jnp.zeros_like(outs[-1])

            outs[-1][...] += dg
            outs[0][...] = tot
            if low:
                outs[1][...] = tot.astype(outs[1].dtype)
        else:
            outs[0][...] = acc.astype(outs[0].dtype)
        rd.finish()

    tile = pl.BlockSpec((tm, tw), lambda j, i: (i, j))
    bspec = pl.BlockSpec((K, tw), lambda j, i: (0, j)) if kind == "nn" else pl.BlockSpec((tw, K), lambda j, i: (j, 0))
    in_specs = [pl.BlockSpec((tm, p.shape[1]), lambda j, i: (i, 0)) for p in a_parts] + [bspec]
    args = a_parts + [b]
    if add is not None:
        in_specs.append(tile)
        args.append(add)
    vec = pl.BlockSpec((1, tw), lambda j, i: (0, j))
    if rms:
        in_specs += [tile, vec, tile]
        args += [rms[0], rms[1], rms[2]]
        out_specs = [tile] * (1 + low) + [vec]
        out_shape = [SDS((M, Wd), F32)] + ([SDS((M, Wd), _MXU)] if low else []) + [SDS((1, Wd), F32)]
    elif loss:
        in_specs += [vec, tile]
        args += [loss[0], loss[1]]
        out_specs = [pl.BlockSpec((1, 128), lambda j, i: (0, 0)), tile, tile, vec]
        out_shape = [SDS((1, 128), F32), SDS((M, Wd), F32), SDS((M, Wd), _MXU), SDS((1, Wd), F32)]
    elif norm is not None:
        in_specs.append(vec)
        args.append(norm)
        out_specs, out_shape = [tile, tile], [SDS((M, Wd), out_dtype), SDS((M, Wd), _MXU)]
    else:
        out_specs, out_shape = [tile], [SDS((M, Wd), out_dtype)]
    order = ("arbitrary", "arbitrary") if (nco or rms or loss) else ("parallel", "parallel")
    outs = pl.pallas_call(
        body, name=name, grid=grid, in_specs=in_specs + [HBM_SPEC] * nco, out_specs=out_specs + [HBM_SPEC] * nco,
        out_shape=out_shape + rd.out_shapes(), scratch_shapes=rd.scratch(),
        compiler_params=_cp(*order))(*args, *rd.arrays())
    main = tuple(outs[:nout]) if nout > 1 else outs[0]
    return main if riders is None else (main, rd.split(outs[nout:]))


def _mm_nn(a, b, name, add=None, out_dtype=F32, tm_cap=1024, tn_cap=1408, riders=None, rms=None, norm=None, loss=None):
    tm, tn = _tile(a.shape[0], tm_cap, 8), _tile(b.shape[1], tn_cap)
    return _matmul("nn", a, b, name, add, out_dtype, tm, tn, riders, rms, norm, loss)


def _mm_nt(a, b, name, add=None, out_dtype=F32, tm_cap=512, tk_cap=1024, riders=None, rms=None):
    rows = (a[0] if isinstance(a, (list, tuple)) else a).shape[0]
    tm, tk = _tile(rows, tm_cap, 8), _tile(b.shape[0], tk_cap)
    return _matmul("nt", a, b, name, add, out_dtype, tm, tk, riders, rms)


def _mm_tn(a, b, name, tm_cap=1024, tk_cap=512, tn_cap=1408):
    b_parts = list(b) if isinstance(b, (list, tuple)) else [b]
    nb = len(b_parts)
    M, K = a.shape
    N = sum(p.shape[1] for p in b_parts)
    tm, tk, tn = _tile(M, tm_cap, 8), _tile(K, tk_cap), _tile(N, tn_cap)
    assert nb == 1 or tn == N

    def body(a_ref, *rest):
        o_ref = rest[nb]
        bv = rest[0][...] if nb == 1 else jnp.concatenate([r[...] for r in rest[:nb]], axis=1)

        @pl.when(pl.program_id(2) == 0)
        def _():
            o_ref[...] = jnp.zeros_like(o_ref)

        o_ref[...] += _dot(a_ref[...], bv, "tn")

    b_specs = ([pl.BlockSpec((tm, tn), lambda k, n, m: (m, n))] if nb == 1 else
               [pl.BlockSpec((tm, p.shape[1]), lambda k, n, m: (m, 0)) for p in b_parts])
    return pl.pallas_call(
        body, name=name, grid=(K // tk, N // tn, M // tm),
        in_specs=[pl.BlockSpec((tm, tk), lambda k, n, m: (m, k))] + b_specs,
        out_specs=pl.BlockSpec((tk, tn), lambda k, n, m: (k, n)), out_shape=SDS((K, N), F32),
        compiler_params=_cp("parallel", "parallel", "arbitrary"))(a, *b_parts)


def _rms_fwd(x, g, name, tb_cap=512):
    S, Dm = x.shape
    tb = _tile(S, tb_cap, 8)

    def body(x_ref, g_ref, o_ref):
        xv = x_ref[...]
        r = lax.rsqrt(jnp.mean(xv * xv, axis=-1, keepdims=True) + EPS)
        o_ref[...] = (xv * r * g_ref[...]).astype(o_ref.dtype)

    return pl.pallas_call(
        body, name=name, grid=(S // tb,),
        in_specs=[pl.BlockSpec((tb, Dm), lambda i: (i, 0)), pl.BlockSpec((1, Dm), lambda i: (0, 0))],
        out_specs=pl.BlockSpec((tb, Dm), lambda i: (i, 0)), out_shape=SDS((S, Dm), _MXU),
        compiler_params=_cp("parallel"))(x, g)


def _rms_bwd(x, g, dh, dres, name, tb_cap=512, low=True):
    S, Dm = x.shape
    tb = _tile(S, tb_cap, 8)
    need_dx = dres is not None

    def body(x_ref, g_ref, dh_ref, *rest):
        dg_ref = rest[-1]
        tot, dg = _rms_bwd_tile(x_ref[...], g_ref[...], dh_ref[...].astype(F32), rest[0][...] if need_dx else 0.0)

        @pl.when(pl.program_id(0) == 0)
        def _():
            dg_ref[...] = jnp.zeros_like(dg_ref)

        dg_ref[...] += dg
        if need_dx:
            rest[1][...] = tot
            if low:
                rest[2][...] = tot.astype(rest[2].dtype)

    row = pl.BlockSpec((tb, Dm), lambda i: (i, 0))
    vec = pl.BlockSpec((1, Dm), lambda i: (0, 0))
    if need_dx:
        outs = [SDS((S, Dm), F32)] + ([SDS((S, Dm), _MXU)] if low else [])
        return pl.pallas_call(
            body, name=name, grid=(S // tb,), in_specs=[row, vec, row, row], out_specs=[row] * len(outs) + [vec],
            out_shape=outs + [SDS((1, Dm), F32)], compiler_params=_cp("arbitrary"))(x, g, dh, dres)
    return pl.pallas_call(
        body, name=name, grid=(S // tb,), in_specs=[row, vec, row], out_specs=vec,
        out_shape=SDS((1, Dm), F32), compiler_params=_cp("arbitrary"))(x, g, dh)


def _final_loss_tile(xv, gv, tv):
    r = lax.rsqrt(jnp.mean(xv * xv, axis=-1, keepdims=True) + EPS)
    xh = xv * r
    e = xh * gv - tv
    dy = e * (1.0 / xv.shape[-1])
    gdy = dy * gv
    dx = r * (gdy - xh * jnp.mean(xh * gdy, axis=-1, keepdims=True))
    return 0.5 * jnp.sum(jnp.mean(e * e, axis=-1, keepdims=True)), dx, jnp.sum(dy * xh, axis=0, keepdims=True)


SSD_HALO = 8
CF_HALO = 32

HBM_SPEC = pl.BlockSpec(memory_space=pl.ANY)


def _chip_peers(x, y):
    return [(1 - x, y), (x, 1 - y), (1 - x, 1 - y)]


def _remote(src, dst, send_sem, recv_sem, dev):
    return pltpu.make_async_remote_copy(src_ref=src, dst_ref=dst, send_sem=send_sem, recv_sem=recv_sem,
                                        device_id=dev, device_id_type=MESHID)


def _scatter_copies(srcs, outs, send_sems, recv_sems):
    x, y, c = lax.axis_index("x"), lax.axis_index("y"), lax.axis_index("c")
    me = 2 * x + y
    sends, recvs = [], []
    for i, (s, o) in enumerate(zip(srcs, outs)):
        for k, (px, py) in enumerate(_chip_peers(x, y)):
            j = 3 * i + k
            sends.append(_remote(s.at[2 * px + py], o.at[me], send_sems.at[j], recv_sems.at[j], (px, py, c)))
            recvs.append(_remote(s.at[me], o.at[2 * px + py], send_sems.at[j], recv_sems.at[j], (px, py, c)))
    return sends, recvs


def _pair_copies(srcs, outs, send_sems, recv_sems):
    x, y, c = lax.axis_index("x"), lax.axis_index("y"), lax.axis_index("c")
    sends = [_remote(s.at[j, 1 - c], o.at[j], send_sems.at[4 * i + j], recv_sems.at[4 * i + j], (x, y, 1 - c))
             for i, (s, o) in enumerate(zip(srcs, outs)) for j in range(4)]
    return sends, sends


def _rows_half(ref, rows, h):
    r = rows // 2
    return ref.at[pl.ds(h * r if isinstance(h, int) else pl.multiple_of(h * r, 8), r)]


def _gather_copies(srcs, outs, rows, send_sems, recv_sems):
    x, y, c = lax.axis_index("x"), lax.axis_index("y"), lax.axis_index("c")
    me = 2 * x + y
    sends, recvs = [], []
    for i, (s, o) in enumerate(zip(srcs, outs)):
        mine = _rows_half(s, rows[i], c)
        for k, (px, py) in enumerate(_chip_peers(x, y)):
            j = 3 * i + k
            sends.append(_remote(mine, o.at[me, c], send_sems.at[j], recv_sems.at[j], (px, py, c)))
            recvs.append(_remote(mine, o.at[2 * px + py, c], send_sems.at[j], recv_sems.at[j], (px, py, c)))
    return sends, recvs


def _gather_shapes(shards):
    return [SDS((4, 2, a.shape[0] // 2, a.shape[1]), a.dtype) for a in shards]


class _Rider:
    SEMS_PER_ARRAY = {"exchange": 3, "gather": 3, "pair": 4}

    def __init__(self, kind, arrays):
        self.kind, self.arrays = kind, list(arrays)

    def out_shapes(self):
        if self.kind == "gather":
            return _gather_shapes(self.arrays)
        if self.kind == "pair":
            return [SDS((4,) + a.shape[2:], a.dtype) for a in self.arrays]
        return [SDS(a.shape, a.dtype) for a in self.arrays]

    def scratch(self):
        n = self.SEMS_PER_ARRAY[self.kind] * len(self.arrays)
        return [pltpu.SemaphoreType.DMA((n,)), pltpu.SemaphoreType.DMA((n,))]

    def copies(self, srcs, outs, send_sems, recv_sems):
        if self.kind == "gather":
            return _gather_copies(srcs, outs, [a.shape[0] for a in self.arrays], send_sems, recv_sems)
        if self.kind == "pair":
            return _pair_copies(srcs, outs, send_sems, recv_sems)
        return _scatter_copies(srcs, outs, send_sems, recv_sems)


class _Riders:
    def __init__(self, riders):
        self.given = list(riders)
        self.riders = [r for r in self.given if r.arrays]

    def arrays(self):
        return [a for r in self.riders for a in r.arrays]

    def out_shapes(self):
        return [s for r in self.riders for s in r.out_shapes()]

    def scratch(self):
        return [s for r in self.riders for s in r.scratch()]

    def split(self, outs):
        res, k = [], 0
        for r in self.given:
            res.append(list(outs[k:k + len(r.arrays)]))
            k += len(r.arrays)
        return res

    def bind(self, in_refs, out_refs, sem_refs, steps):
        self.steps = steps if isinstance(steps, tuple) else (steps,)
        self.bound, k = [], 0
        for i, r in enumerate(self.riders):
            n = len(r.arrays)
            self.bound.append((r, in_refs[k:k + n], out_refs[k:k + n], sem_refs[2 * i], sem_refs[2 * i + 1]))
            k += n
        return self

    def _at(self, last):
        hit = None
        for ax, n in enumerate(self.steps):
            here = pl.program_id(ax) == (n - 1 if last else 0)
            hit = here if hit is None else jnp.logical_and(hit, here)
        return hit

    def _copies(self):
        sends, recvs = [], []
        for r, srcs, outs, send_sems, recv_sems in self.bound:
            s, w = r.copies(srcs, outs, send_sems, recv_sems)
            sends += s
            recvs += w
        return sends, recvs

    def start(self):
        if self.riders:
            @pl.when(self._at(last=False))
            def _():
                for cp in self._copies()[0]:
                    cp.start()

    def finish(self):
        if self.riders:
            @pl.when(self._at(last=True))
            def _():
                sends, recvs = self._copies()
                for cp in recvs:
                    cp.wait_recv()
                for cp in sends:
                    cp.wait_send()


def _head_consts():
    e = (lax.broadcasted_iota(jnp.int32, (128, NH * HP), 1) // HP == lax.broadcasted_iota(jnp.int32, (128, NH * HP), 0)).astype(F32)
    et = (lax.broadcasted_iota(jnp.int32, (NH * HP, 128), 0) // HP == lax.broadcasted_iota(jnp.int32, (NH * HP, 128), 1)).astype(F32)
    r = lax.broadcasted_iota(jnp.int32, (CH, CH), 0)
    c = lax.broadcasted_iota(jnp.int32, (CH, CH), 1)
    return e, et, (c <= r), (r <= c)


def _ssd_common(xbc_c, dtr, dtb, alog, e, tril, triu):
    xbc = _silu(xbc_c)
    xs = xbc[:, :NH * HP]
    dt = _softplus(dtr + dtb)
    A = -jnp.exp(alog)
    a = dt * A
    cs = _dot(tril, a, exact="a", passes=3)
    csT = _dot(a, triu, "tn", exact="b", passes=3)
    csL = cs[CH - 1:CH, :]
    wdec = jnp.exp(csL - cs) * dt
    dtE = _dot(dt, e, exact="b")
    ecsE = _dot(jnp.exp(cs), e, exact="b")
    wE = _dot(wdec, e, exact="b")
    eL = jnp.exp(csL)
    return xbc, xs, dt, A, cs, csT, csL, wdec, dtE, ecsE, wE, eL


def _ssd_fwd(proj, cw, cb, sc, norm_g, riders=(), name="ssd_fwd"):
    S = proj.shape[0]
    nc = S // CH
    rd = _Riders(riders)
    nco = len(rd.arrays())

    def body(*refs):
        z_ref, xp_ref, cw_ref, cb_ref, dtr_ref, sc_ref, ng_ref = refs[:7]
        xc_ref, y_ref, yn_ref, hp_ref = refs[7 + nco:11 + nco]
        hst, cext = refs[11 + 2 * nco:13 + 2 * nco]
        rd.bind(refs[7:7 + nco], refs[11 + nco:11 + 2 * nco], refs[13 + 2 * nco:], nc).start()

        @pl.when(pl.program_id(0) == 0)
        def _():
            hst[...] = jnp.zeros_like(hst)
            cext[pl.ds(0, SSD_HALO), :] = jnp.zeros((SSD_HALO, XBC), F32)

        cext[pl.ds(SSD_HALO, CH), :] = xp_ref[...]
        xc = jnp.zeros((CH, XBC), F32) + cb_ref[...]
        for k in range(KS):
            xc = xc + cext[pl.ds(SSD_HALO - (KS - 1) + k, CH), :] * cw_ref[k:k + 1, :]
        xc_ref[...] = xc
        cext[pl.ds(0, SSD_HALO), :] = cext[pl.ds(CH, SSD_HALO), :]

        e, et, tril, triu = _head_consts()
        xbc, xs, dt, A, cs, csT, csL, wdec, dtE, ecsE, wE, eL = _ssd_common(
            xc, dtr_ref[...], sc_ref[0:1, :], sc_ref[1:2, :], e, tril, triu)
        hp_ref[0] = hst[...]
        xd = xs * dtE
        xw = xs * wE
        dE = _dot(jnp.broadcast_to(sc_ref[2:3, :], (8, 128)), e, exact="b", passes=3)[0:1, :]
        eLcol = jnp.sum(et * eL, axis=1, keepdims=True)
        for g in range(NG):
            Bg = xbc[:, NH * HP + g * NS: NH * HP + (g + 1) * NS]
            Cg = xbc[:, NH * HP + NG * NS + g * NS: NH * HP + NG * NS + (g + 1) * NS]
            gs = slice(g * GW, (g + 1) * GW)
            G = _dot(Cg, Bg, "nt")
            hg = hst[gs, :]
            yoff = ecsE[:, gs] * _dot(Cg, hg, "nt")
            hst[gs, :] = eLcol[gs, :] * hg + _dot(xw[:, gs], Bg, "tn")
            for hh in range(NH // NG):
                h = g * (NH // NG) + hh
                hs = slice(h * HP, (h + 1) * HP)
                m = jnp.where(tril, jnp.exp(jnp.where(tril, cs[:, h:h + 1] - csT[h:h + 1, :], 0.0)), 0.0)
                yd = _dot(G * m, xd[:, hs])
                y_ref[:, hs] = yd + yoff[:, hh * HP:(hh + 1) * HP] + dE[:, hs] * xs[:, hs]
        y = y_ref[...]
        yz = y * _silu(z_ref[...])
        for g in range(NG):
            gs = slice(g * GW, (g + 1) * GW)
            yg = yz[:, gs]
            r = lax.rsqrt(jnp.mean(yg * yg, axis=-1, keepdims=True) + EPS)
            yn_ref[:, gs] = (yg * r * ng_ref[:, gs]).astype(yn_ref.dtype)
        rd.finish()

    outs = pl.pallas_call(
        body, name=name, grid=(nc,),
        in_specs=[pl.BlockSpec((CH, D), lambda c: (c, COL_Z // D)),
                  pl.BlockSpec((CH, XBC), lambda c: (c, COL_XBC // XBC)),
                  pl.BlockSpec((KS, XBC), lambda c: (0, 0)),
                  pl.BlockSpec((1, XBC), lambda c: (0, 0)),
                  pl.BlockSpec((CH, 128), lambda c: (c, COL_DT // 128)),
                  pl.BlockSpec((8, 128), lambda c: (0, 0)),
                  pl.BlockSpec((1, D), lambda c: (0, 0))] + [HBM_SPEC] * nco,
        out_specs=[pl.BlockSpec((CH, XBC), lambda c: (c, 0)), pl.BlockSpec((CH, D), lambda c: (c, 0)),
                   pl.BlockSpec((CH, D), lambda c: (c, 0)),
                   pl.BlockSpec((1, NH * HP, NS), lambda c: (c, 0, 0))] + [HBM_SPEC] * nco,
        out_shape=[SDS((S, XBC), F32), SDS((S, D), F32), SDS((S, D), _MXU), SDS((nc, NH * HP, NS), F32)]
        + rd.out_shapes(),
        scratch_shapes=[pltpu.VMEM((NH * HP, NS), F32), pltpu.VMEM((SSD_HALO + CH, XBC), F32)] + rd.scratch(),
        compiler_params=_cp("arbitrary"))(proj, proj, cw, cb, proj, sc, norm_g, *rd.arrays())
    return outs[:4], rd.split(outs[4:])


def _ssd_bwd(dmix, y, proj, xbc_c, hprev, cw, sc, norm_g, riders=(), name="ssd_bwd"):
    S = proj.shape[0]
    nc = S // CH
    rd = _Riders(riders)
    nco = len(rd.arrays())
    rev = lambda c: nc - 1 - c

    def body(*refs):
        dyn_ref, y_ref, z_ref, x_ref, xp_ref, dtr_ref, hp_ref, cw_ref, sc_ref, ng_ref = refs[:10]
        dz_ref, dx_ref, ddtr_ref, gcw_ref, gcb_ref, gsc_ref, gng_ref = refs[10 + nco:17 + nco]
        dh, dxd, cext = refs[17 + 2 * nco:20 + 2 * nco]
        rd.bind(refs[10:10 + nco], refs[17 + nco:17 + 2 * nco], refs[20 + 2 * nco:], nc).start()

        @pl.when(pl.program_id(0) == 0)
        def _():
            dh[...] = jnp.zeros_like(dh)
            cext[pl.ds(CH, SSD_HALO), :] = jnp.zeros((SSD_HALO, XBC), F32)
            gcw_ref[...] = jnp.zeros_like(gcw_ref)
            gcb_ref[...] = jnp.zeros_like(gcb_ref)
            gsc_ref[...] = jnp.zeros_like(gsc_ref)
            gng_ref[...] = jnp.zeros_like(gng_ref)

        e, et, tril, triu = _head_consts()
        xbc_c = x_ref[...]
        dtr = dtr_ref[...]
        dtb = sc_ref[0:1, :]
        xbc, xs, dt, A, cs, csT, csL, wdec, dtE, ecsE, wE, eL = _ssd_common(
            xbc_c, dtr, dtb, sc_ref[1:2, :], e, tril, triu)
        xd = xs * dtE
        xw = xs * wE
        dE = _dot(jnp.broadcast_to(sc_ref[2:3, :], (8, 128)), e, exact="b", passes=3)[0:1, :]
        eLcol = jnp.sum(et * eL, axis=1, keepdims=True)

        yv = y_ref[...]
        zv = z_ref[...]
        sz = _silu(zv)
        yz = yv * sz
        dyn = dyn_ref[...]
        dyz_parts = []
        for g in range(NG):
            gs = slice(g * GW, (g + 1) * GW)
            yg = yz[:, gs]
            r = lax.rsqrt(jnp.mean(yg * yg, axis=-1, keepdims=True) + EPS)
            yh = yg * r
            dn = dyn[:, gs]
            gng_ref[:, gs] += jnp.sum(dn * yh, axis=0, keepdims=True)
            gdn = dn * ng_ref[:, gs]
            dyz_parts.append(r * (gdn - yh * jnp.mean(yh * gdn, axis=-1, keepdims=True)))
        dyz = jnp.concatenate(dyz_parts, axis=1)
        dy = dyz * sz
        dz_ref[...] = (dyz * yv * _dsilu(zv)).astype(dz_ref.dtype)

        dxs = dE * dy
        dzo = ecsE * dy
        dcsL = jnp.zeros((1, 128), F32)
        ddt = jnp.zeros((CH, 128), F32)
        qcols = jnp.zeros((CH, 128), F32)
        qrows = jnp.zeros((128, CH), F32)
        lane = lax.broadcasted_iota(jnp.int32, (1, 128), 1)
        sub = lax.broadcasted_iota(jnp.int32, (128, 1), 0)
        dB_parts, dC_parts, yoff_parts, dxw_parts = [], [], [], []
        for g in range(NG):
            Bg = xbc[:, NH * HP + g * NS: NH * HP + (g + 1) * NS]
            Cg = xbc[:, NH * HP + NG * NS + g * NS: NH * HP + NG * NS + (g + 1) * NS]
            gs = slice(g * GW, (g + 1) * GW)
            hg = hp_ref[0, gs, :]
            dhn = dh[gs, :]
            G = _dot(Cg, Bg, "nt")
            yoff_parts.append(ecsE[:, gs] * _dot(Cg, hg, "nt"))
            dC = _dot(dzo[:, gs], hg)
            dhp = _dot(dzo[:, gs], Cg, "tn") + eLcol[gs, :] * dhn
            t1 = jnp.sum(dhn * hg, axis=1, keepdims=True) * eLcol[gs, :]
            dcsL = dcsL + jnp.sum(et[gs, :] * t1, axis=0, keepdims=True)
            dxw_parts.append(_dot(Bg, dhn, "nt"))
            dB = _dot(xw[:, gs], dhn)
            dgsum = jnp.zeros((CH, CH), F32)
            for hh in range(NH // NG):
                h = g * (NH // NG) + hh
                hs = slice(h * HP, (h + 1) * HP)
                m = jnp.where(tril, jnp.exp(jnp.where(tril, cs[:, h:h + 1] - csT[h:h + 1, :], 0.0)), 0.0)
                sc = G * m
                dyh = dy[:, hs]
                dxd[:, hs] = _dot(sc, dyh, "tn")
                dsc = _dot(dyh, xd[:, hs], "nt")
                q = dsc * sc
                qcols = qcols + jnp.where(lane == h, jnp.sum(q, axis=1, keepdims=True), 0.0)
                qrows = qrows + jnp.where(sub == h, jnp.sum(q, axis=0, keepdims=True), 0.0)
                dgsum = dgsum + dsc * m
            dC_parts.append(dC + _dot(dgsum, Bg))
            dB_parts.append(dB + _dot(dgsum, Cg, "tn"))
            dh[gs, :] = dhp
        yoff = jnp.concatenate(yoff_parts, axis=1)
        dxw = jnp.concatenate(dxw_parts, axis=1)
        dxdv = dxd[...]
        per_head = _dot(jnp.concatenate([dy * yoff, dxw * xs, dxdv * xs, dy * xs], axis=0), et, exact="b")
        dcs = qcols - qrows.T + per_head[0:CH]
        dw = per_head[CH:2 * CH]
        gsc_ref[2:3, :] += jnp.sum(per_head[3 * CH:4 * CH], axis=0, keepdims=True)
        dxs = dxs + wE * dxw + dtE * dxdv
        ddt = ddt + dw * jnp.exp(csL - cs) + per_head[2 * CH:3 * CH]
        dcs = dcs - dw * wdec
        dcsL = dcsL + jnp.sum(dw * wdec, axis=0, keepdims=True)
        last = lax.broadcasted_iota(jnp.int32, (CH, 128), 0) == CH - 1
        dcs = dcs + jnp.where(last, dcsL, 0.0)
        da = _dot(triu, dcs, exact="a", passes=3)
        ddt = ddt + da * A
        gsc_ref[1:2, :] += jnp.sum(da * dt, axis=0, keepdims=True) * A
        valid = lax.broadcasted_iota(jnp.int32, (CH, 128), 1) < NH
        ddtr = jnp.where(valid, ddt * _sigmoid(dtr + dtb), 0.0)
        gsc_ref[0:1, :] += jnp.sum(ddtr, axis=0, keepdims=True)
        ddtr_ref[...] = ddtr.astype(ddtr_ref.dtype)
        dxbc = jnp.concatenate([dxs] + dB_parts + dC_parts, axis=1)
        dxc = dxbc * _dsilu(xbc_c)
        cext[pl.ds(0, CH), :] = dxc
        xp = xp_ref[...]
        acc = jnp.zeros((CH, XBC), F32)
        for k in range(KS):
            sh = cext[pl.ds(KS - 1 - k, CH), :]
            acc = acc + sh * cw_ref[k:k + 1, :]
            gcw_ref[k:k + 1, :] += jnp.sum(xp * sh, axis=0, keepdims=True)
        gcb_ref[...] += jnp.sum(dxc, axis=0, keepdims=True)
        dx_ref[...] = acc.astype(dx_ref.dtype)
        cext[pl.ds(CH, SSD_HALO), :] = cext[pl.ds(0, SSD_HALO), :]
        rd.finish()

    vec = pl.BlockSpec((8, 128), lambda c: (0, 0))
    vecd = pl.BlockSpec((1, D), lambda c: (0, 0))
    cwsp = pl.BlockSpec((KS, XBC), lambda c: (0, 0))
    cbsp = pl.BlockSpec((1, XBC), lambda c: (0, 0))
    row = lambda w, j=0: pl.BlockSpec((CH, w), lambda c: (rev(c), j))
    outs = pl.pallas_call(
        body, name=name, grid=(nc,),
        in_specs=[row(D), row(D), row(D, COL_Z // D), row(XBC), row(XBC, COL_XBC // XBC), row(128, COL_DT // 128),
                  pl.BlockSpec((1, NH * HP, NS), lambda c: (rev(c), 0, 0)), cwsp, vec, vecd] + [HBM_SPEC] * nco,
        out_specs=[row(D), row(XBC), row(128), cwsp, cbsp, vec, vecd] + [HBM_SPEC] * nco,
        out_shape=[SDS((S, D), _MXU), SDS((S, XBC), _MXU), SDS((S, 128), _MXU), SDS((KS, XBC), F32),
                   SDS((1, XBC), F32), SDS((8, 128), F32), SDS((1, D), F32)] + rd.out_shapes(),
        scratch_shapes=[pltpu.VMEM((NH * HP, NS), F32), pltpu.VMEM((CH, NH * HP), F32),
                        pltpu.VMEM((CH + SSD_HALO, XBC), F32)] + rd.scratch(),
        compiler_params=_cp("arbitrary"))(dmix, y, proj, xbc_c, proj, proj, hprev, cw, sc, norm_g, *rd.arrays())
    return outs[:7], rd.split(outs[7:])


CONV_RT = 32


def _fill_phases(ext, ph, rows):
    for s in range(1, 8):
        ph[s - 1, pl.ds(0, rows), :] = ext[pl.ds(s, rows), :]


def _window(ext, ph, off, r0, ls):
    s = off % 8
    src = ext if s == 0 else ph.at[s - 1]
    return src[pl.ds(pl.multiple_of(off - s + r0, 8), CONV_RT), ls]


def _cf_fwd(proj, w, b, lg, lb, riders=(), name="cf_fwd", tb_cap=256):
    S = proj.shape[0]
    tb = _tile(S, tb_cap, 8)
    nb = S // tb
    rd = _Riders(riders)
    nco = len(rd.arrays())

    def body(*refs):
        a_ref, g_ref, w_ref, b_ref, lg_ref, lb_ref = refs[:6]
        u1_ref, u_ref = refs[6 + nco:8 + nco]
        ext, ph = refs[8 + 2 * nco:10 + 2 * nco]
        rd.bind(refs[6:6 + nco], refs[8 + nco:8 + 2 * nco], refs[10 + 2 * nco:], nb).start()

        @pl.when(pl.program_id(0) == 0)
        def _():
            ext[pl.ds(0, CF_HALO), :] = jnp.zeros((CF_HALO, D), F32)

        ext[pl.ds(CF_HALO, tb), :] = a_ref[...] * _sigmoid(g_ref[...])
        _fill_phases(ext, ph, tb + CF_HALO - 8)

        def tile(i, carry):
            r0 = pl.multiple_of(i * CONV_RT, CONV_RT)
            for l in range(D // 128):
                ls = pl.ds(l * 128, 128)
                acc = jnp.broadcast_to(b_ref[:, ls], (CONV_RT, 128))
                for k in range(KC):
                    acc = acc + _window(ext, ph, CF_HALO - (KC - 1) + k, r0, ls) * w_ref[k:k + 1, ls]
                u1_ref[pl.ds(r0, CONV_RT), ls] = acc
            return carry

        lax.fori_loop(0, tb // CONV_RT, tile, 0)
        acc = u1_ref[...]
        mu = jnp.mean(acc, axis=-1, keepdims=True)
        xc = acc - mu
        r = lax.rsqrt(jnp.mean(xc * xc, axis=-1, keepdims=True) + EPS)
        u_ref[...] = _silu(xc * r * lg_ref[...] + lb_ref[...]).astype(u_ref.dtype)
        ext[pl.ds(0, CF_HALO), :] = ext[pl.ds(tb, CF_HALO), :]
        rd.finish()

    vec = pl.BlockSpec((1, D), lambda i: (0, 0))
    outs = pl.pallas_call(
        body, name=name, grid=(nb,),
        in_specs=[pl.BlockSpec((tb, D), lambda i: (i, COL_A // D)), pl.BlockSpec((tb, D), lambda i: (i, COL_G // D)),
                  pl.BlockSpec((KC, D), lambda i: (0, 0)), vec, vec, vec] + [HBM_SPEC] * nco,
        out_specs=[pl.BlockSpec((tb, D), lambda i: (i, 0)), pl.BlockSpec((tb, D), lambda i: (i, 0))] + [HBM_SPEC] * nco,
        out_shape=[SDS((S, D), F32), SDS((S, D), _MXU)] + rd.out_shapes(),
        scratch_shapes=[pltpu.VMEM((CF_HALO + tb, D), F32), pltpu.VMEM((7, tb + CF_HALO - 8, D), F32)] + rd.scratch(),
        compiler_params=_cp("arbitrary"))(proj, proj, w, b, lg, lb, *rd.arrays())
    return outs[:2], rd.split(outs[2:])


def _cf_bwd(dmix, u1, proj, w, lg, lb, riders=(), name="cf_bwd", tb_cap=256):
    S = proj.shape[0]
    tb = _tile(S, tb_cap, 8)
    nb = S // tb
    rd = _Riders(riders)
    nco = len(rd.arrays())
    rev = lambda i: nb - 1 - i

    def body(*refs):
        du_ref, u1_ref, a_ref, g_ref, w_ref, lg_ref, lb_ref = refs[:7]
        da_ref, dg_ref, dw_ref, db_ref, dlg_ref, dlb_ref = refs[7 + nco:13 + nco]
        ext, ph, u0s = refs[13 + 2 * nco:16 + 2 * nco]
        rd.bind(refs[7:7 + nco], refs[13 + nco:13 + 2 * nco], refs[16 + 2 * nco:], nb).start()

        @pl.when(pl.program_id(0) == 0)
        def _():
            ext[pl.ds(tb, CF_HALO), :] = jnp.zeros((CF_HALO, D), F32)
            dw_ref[...] = jnp.zeros_like(dw_ref)
            db_ref[...] = jnp.zeros_like(db_ref)
            dlg_ref[...] = jnp.zeros_like(dlg_ref)
            dlb_ref[...] = jnp.zeros_like(dlb_ref)

        u1 = u1_ref[...]
        mu = jnp.mean(u1, axis=-1, keepdims=True)
        xc = u1 - mu
        r = lax.rsqrt(jnp.mean(xc * xc, axis=-1, keepdims=True) + EPS)
        xh = xc * r
        lgv = lg_ref[...]
        du2 = du_ref[...] * _dsilu(xh * lgv + lb_ref[...])
        dlg_ref[...] += jnp.sum(du2 * xh, axis=0, keepdims=True)
        dlb_ref[...] += jnp.sum(du2, axis=0, keepdims=True)
        gd = du2 * lgv
        du1 = r * (gd - jnp.mean(gd, axis=-1, keepdims=True) - xh * jnp.mean(gd * xh, axis=-1, keepdims=True))
        db_ref[...] += jnp.sum(du1, axis=0, keepdims=True)
        ext[pl.ds(0, tb), :] = du1
        u0s[...] = a_ref[...] * _sigmoid(g_ref[...])
        _fill_phases(ext, ph, tb + CF_HALO - 8)

        for l in range(D // 128):
            ls = pl.ds(l * 128, 128)

            def tile(i, accs, ls=ls):
                r0 = pl.multiple_of(i * CONV_RT, CONV_RT)
                rows = pl.ds(r0, CONV_RT)
                u0t = u0s[rows, ls]
                acc = jnp.zeros((CONV_RT, 128), F32)
                out = []
                for k in range(KC):
                    win = _window(ext, ph, KC - 1 - k, r0, ls)
                    acc = acc + win * w_ref[k:k + 1, ls]
                    p = u0t * win
                    out.append(accs[k] + ((p[0:8] + p[8:16]) + (p[16:24] + p[24:32])))
                sg = _sigmoid(g_ref[rows, ls])
                da_ref[rows, ls] = (acc * sg).astype(da_ref.dtype)
                dg_ref[rows, ls] = (acc * a_ref[rows, ls] * sg * (1.0 - sg)).astype(dg_ref.dtype)
                return tuple(out)

            accs = lax.fori_loop(0, tb // CONV_RT, tile, tuple(jnp.zeros((8, 128), F32) for _ in range(KC)))
            for k in range(KC):
                dw_ref[k:k + 1, ls] += jnp.sum(accs[k], axis=0, keepdims=True)
        ext[pl.ds(tb, CF_HALO), :] = ext[pl.ds(0, CF_HALO), :]
        rd.finish()

    vec = pl.BlockSpec((1, D), lambda i: (0, 0))
    wsp = pl.BlockSpec((KC, D), lambda i: (0, 0))
    row = lambda j=0: pl.BlockSpec((tb, D), lambda i: (rev(i), j))
    outs = pl.pallas_call(
        body, name=name, grid=(nb,),
        in_specs=[row(1), row(), row(COL_A // D), row(COL_G // D), wsp, vec, vec] + [HBM_SPEC] * nco,
        out_specs=[row(), row(), wsp, vec, vec, vec] + [HBM_SPEC] * nco,
        out_shape=[SDS((S, D), _MXU), SDS((S, D), _MXU), SDS((KC, D), F32),
                   SDS((1, D), F32), SDS((1, D), F32), SDS((1, D), F32)] + rd.out_shapes(),
        scratch_shapes=[pltpu.VMEM((tb + CF_HALO, D), F32), pltpu.VMEM((7, tb + CF_HALO - 8, D), F32),
                        pltpu.VMEM((tb, D), F32)] + rd.scratch(),
        compiler_params=_cp("arbitrary"))(dmix, u1, proj, proj, w, lg, lb, *rd.arrays())
    return outs[:6], rd.split(outs[6:])


def _attn_fwd(q, kv, name="attn_fwd", tq_cap=512):
    S = q.shape[0]
    tq = _tile(S, tq_cap, 8)
    scale = XD ** -0.5

    def body(q_ref, kv_ref, o_ref):
        for h in range(XH):
            hs = slice(h * XD, (h + 1) * XD)
            s = _dot(q_ref[:, hs], kv_ref[:, hs], "nt") * scale
            s = s - jnp.max(s, axis=-1, keepdims=True)
            p = jnp.exp(s)
            p = p / jnp.sum(p, axis=-1, keepdims=True)
            o_ref[:, hs] = _dot(p, kv_ref[:, D + h * XD: D + (h + 1) * XD]).astype(o_ref.dtype)

    return pl.pallas_call(
        body, name=name, grid=(S // tq,),
        in_specs=[pl.BlockSpec((tq, D), lambda i: (i, 0)), pl.BlockSpec((MEM, 2 * D), lambda i: (0, 0))],
        out_specs=pl.BlockSpec((tq, D), lambda i: (i, 0)), out_shape=SDS((S, D), _MXU),
        compiler_params=_cp("parallel"))(q, kv)


def _attn_bwd(do, q, kv, riders=(), name="attn_bwd", tq_cap=512):
    S = q.shape[0]
    tq = _tile(S, tq_cap, 8)
    scale = XD ** -0.5
    rd = _Riders(riders)
    nco = len(rd.arrays())

    def body(*refs):
        do_ref, q_ref, kv_ref = refs[:3]
        dq_ref, dkv_ref = refs[3 + nco:5 + nco]
        rd.bind(refs[3:3 + nco], refs[5 + nco:5 + 2 * nco], refs[5 + 2 * nco:], S // tq).start()

        @pl.when(pl.program_id(0) == 0)
        def _():
            dkv_ref[...] = jnp.zeros_like(dkv_ref)

        for h in range(XH):
            hs = slice(h * XD, (h + 1) * XD)
            vs = slice(D + h * XD, D + (h + 1) * XD)
            qh = q_ref[:, hs]
            kh = kv_ref[:, hs]
            s = _dot(qh, kh, "nt") * scale
            s = s - jnp.max(s, axis=-1, keepdims=True)
            p = jnp.exp(s)
            p = p / jnp.sum(p, axis=-1, keepdims=True)
            doh = do_ref[:, hs]
            dp = _dot(doh, kv_ref[:, vs], "nt")
            ds = p * (dp - jnp.sum(dp * p, axis=-1, keepdims=True)) * scale
            dq_ref[:, hs] = _dot(ds, kh).astype(dq_ref.dtype)
            dkv_ref[:, hs] += _dot(ds, qh, "tn")
            dkv_ref[:, vs] += _dot(p, doh, "tn")
        rd.finish()

    outs = pl.pallas_call(
        body, name=name, grid=(S // tq,),
        in_specs=[pl.BlockSpec((tq, D), lambda i: (i, 0)), pl.BlockSpec((tq, D), lambda i: (i, 0)),
                  pl.BlockSpec((MEM, 2 * D), lambda i: (0, 0))] + [HBM_SPEC] * nco,
        out_specs=[pl.BlockSpec((tq, D), lambda i: (i, 0)), pl.BlockSpec((MEM, 2 * D), lambda i: (0, 0))]
        + [HBM_SPEC] * nco,
        out_shape=[SDS((S, D), _MXU), SDS((MEM, 2 * D), F32)] + rd.out_shapes(), scratch_shapes=rd.scratch(),
        compiler_params=_cp("arbitrary"))(do, q, kv, *rd.arrays())
    return outs[:2], rd.split(outs[2:])


def _ffn_in(hf, wg_t, wu_t, name="ffn_in", tm_cap=512, tn_cap=1408):
    S, K = hf.shape
    N = wg_t.shape[0]
    tm, tn = _tile(S, tm_cap, 8), _tile(N, tn_cap)

    def body(a_ref, g_ref, u_ref, act_ref, gt_ref, up_ref):
        a = a_ref[...]
        gt = _dot(a, g_ref[...], "nt")
        up = _dot(a, u_ref[...], "nt")
        act_ref[...] = (_silu(gt) * up).astype(act_ref.dtype)
        gt_ref[...] = gt.astype(gt_ref.dtype)
        up_ref[...] = up.astype(up_ref.dtype)

    wsp = pl.BlockSpec((tn, K), lambda j, i: (j, 0))
    osp = pl.BlockSpec((tm, tn), lambda j, i: (i, j))
    return pl.pallas_call(
        body, name=name, grid=(N // tn, S // tm), in_specs=[pl.BlockSpec((tm, K), lambda j, i: (i, 0)), wsp, wsp],
        out_specs=[osp, osp, osp], out_shape=[SDS((S, N), _MXU)] * 3,
        compiler_params=_cp("parallel", "parallel"))(hf, wg_t, wu_t)


def _ffn_out_bwd(dx, w_down, gt, up, name="ffn_out_dx", tm_cap=512, tk_cap=1408):
    S, N = dx.shape
    K = w_down.shape[0]
    tm, tk = _tile(S, tm_cap, 8), _tile(K, tk_cap)

    def body(a_ref, b_ref, g_ref, u_ref, dg_ref, du_ref):
        d = _dot(a_ref[...], b_ref[...], "nt")
        gt = g_ref[...].astype(F32)
        s = _sigmoid(gt)
        dg_ref[...] = (d * u_ref[...].astype(F32) * (s * (1.0 + gt * (1.0 - s)))).astype(dg_ref.dtype)
        du_ref[...] = (d * gt * s).astype(du_ref.dtype)

    osp = pl.BlockSpec((tm, tk), lambda j, i: (i, j))
    return pl.pallas_call(
        body, name=name, grid=(K // tk, S // tm),
        in_specs=[pl.BlockSpec((tm, N), lambda j, i: (i, 0)), pl.BlockSpec((tk, N), lambda j, i: (j, 0)), osp, osp],
        out_specs=[osp, osp], out_shape=[SDS((S, K), _MXU)] * 2,
        compiler_params=_cp("parallel", "parallel"))(dx, w_down, gt, up)


AG_RIDE = (("w_down",), ("w_out", "w_q", "w_kv", "w_o"), ("w_gate", "w_up"))


def _local_step(x, mem, tgt, W, P, core=None, late=None):
    pair, got = {}, {}
    ride = [[late[n] for n in grp] if late is not None else [] for grp in AG_RIDE]

    def halves(group):
        if core is None:
            return []
        gs = [_shard_grad(n, GW) for n in group]
        return [g.reshape(4, 2, g.shape[1] // 2, g.shape[2]) for g in gs]

    def pair_sums(group, hs, theirs):
        ps = [_pair_sum(h_, t, core, "rs_pair_sum_" + n) for h_, t, n in zip(hs, theirs, group)]
        pair.update(zip(group, ps))
        return ps

    h = _rms_fwd(x, P["g_mix"], "rms_mix")
    proj, (bufs0,) = _mm_nn(h, W["main"], "in_proj", tm_cap=256, tn_cap=MAINW, riders=[_Rider("gather", ride[0])])
    (xbc_c, y, yn, hprev), (bufs1,) = _ssd_fwd(proj, P["conv4_w"], P["conv4_b"], P["sc"], P["ssd_norm_g"],
                                                riders=[_Rider("gather", ride[1])])
    (u1, u), (bufs2,) = _cf_fwd(proj, P["cf_w"], P["cf_b"], P["ln_g"], P["ln_b"], riders=[_Rider("gather", ride[2])])
    if late is not None:
        names = AG_RIDE[0] + AG_RIDE[1] + AG_RIDE[2]
        full = _gather_finish_list(ride[0] + ride[1] + ride[2], bufs0 + bufs1 + bufs2)
        W = dict(W, **_pack_late(dict(zip(names, full))))
    mix = jnp.concatenate([yn, u], axis=1)
    x1, hq = _mm_nn(mix, W["out"], "out_proj", add=x, tm_cap=512, tn_cap=D, norm=P["g_xattn"])
    q = _mm_nn(hq, W["q"], "q_proj")
    mn = _rms_fwd(mem, P["g_mem"], "rms_mem")
    kv = _mm_nn(mn, W["kv"], "kv_proj")
    o = _attn_fwd(q, kv)
    x2, hf = _mm_nn(o, W["o"], "o_proj", add=x1, tm_cap=512, tn_cap=D, norm=P["g_ffn"])
    act, gt, up = _ffn_in(hf, W["gate_t"], W["up_t"])
    loss, dx3, dx3b, g_final = _mm_nn(act, W["down"], "ffn_out", add=x2, tm_cap=512, tn_cap=D,
                                      loss=(P["g_final"], tgt))
    GW, GP = {}, {"g_final": g_final}
    GW["down"] = _mm_tn(act, dx3b, "ffn_out_dw", tk_cap=1408, tn_cap=1024)
    dgt, dup = _ffn_out_bwd(dx3b, W["down"], gt, up)
    dhf = _mm_nn(dgt, W["gate_t"], "ffn_gate_dx", tm_cap=512)
    dx2, dx2b, GP["g_ffn"] = _mm_nn(dup, W["up_t"], "ffn_up_dx", add=dhf, tm_cap=512, tn_cap=D,
                                    rms=(x2, P["g_ffn"], dx3, True))
    GW["gate_t"] = _mm_tn(dgt, hf, "ffn_gate_dw", tk_cap=1408, tn_cap=1024)
    GW["up_t"] = _mm_tn(dup, hf, "ffn_up_dw", tk_cap=1408, tn_cap=1024)
    ffn_halves = halves(RS_GROUPS[0])
    do = _mm_nt(dx2b, W["o"], "o_proj_dx")
    GW["o"] = _mm_tn(o, dx2b, "o_proj_dw")
    (dq, dkv), (ffn_theirs,) = _attn_bwd(do, q, kv, riders=[_Rider("pair", ffn_halves)])
    ffn_pieces = pair_sums(RS_GROUPS[0], ffn_halves, ffn_theirs)
    dx1, dx1b, GP["g_xattn"] = _mm_nt(dq, W["q"], "q_proj_dx", tk_cap=D, rms=(x1, P["g_xattn"], dx2, True))
    GW["q"] = _mm_tn(hq, dq, "q_proj_dw")
    dkvb = dkv.astype(_MXU)
    GW["kv"] = _mm_tn(mn, dkvb, "kv_proj_dw", tm_cap=256)
    dmn = _mm_nt(dkvb, W["kv"], "kv_proj_dx")
    GP["g_mem"] = _rms_bwd(mem, P["g_mem"], dmn, None, "rms_mem_bwd")
    dmix = _mm_nt(dx1b, W["out"], "out_proj_dx")
    GW["out"] = _mm_tn(mix, dx1b, "out_proj_dw", tn_cap=1024)
    attn_halves = halves(RS_GROUPS[1])
    (da, dg, GP["cf_w"], GP["cf_b"], GP["ln_g"], GP["ln_b"]), (came, attn_theirs) = _cf_bwd(
        dmix, u1, proj, P["cf_w"], P["ln_g"], P["ln_b"],
        riders=[_Rider("exchange", ffn_pieces), _Rider("pair", attn_halves)])
    got.update(zip(RS_GROUPS[0], came))
    attn_pieces = pair_sums(RS_GROUPS[1], attn_halves, attn_theirs)
    (dz, dxbc, ddtr, GP["conv4_w"], GP["conv4_b"], GP["sc"], GP["ssd_norm_g"]), (came,) = _ssd_bwd(
        dmix, y, proj, xbc_c, hprev, P["conv4_w"], P["sc"], P["ssd_norm_g"],
        riders=[_Rider("exchange", attn_pieces)])
    got.update(zip(RS_GROUPS[1], came))
    dproj = [dz, da, dg, dxbc, ddtr]
    GW["main"] = _mm_tn(h, dproj, "in_proj_dw", tm_cap=512, tk_cap=512, tn_cap=MAINW)
    in_halves = halves(RS_GROUPS[2])
    in_pieces = pair_sums(RS_GROUPS[2], in_halves, _pair_split_list(in_halves, "rs_pair_send_w_in")) if in_halves else []
    (grad_x, GP["g_mix"]), (came,) = _mm_nt(dproj, W["main"], "in_proj_dx", tm_cap=256, tk_cap=D,
                                            riders=[_Rider("exchange", in_pieces)], rms=(x, P["g_mix"], dx1, False))
    got.update(zip(RS_GROUPS[2], came))
    if core is None:
        return loss, grad_x, GW, GP
    return loss, grad_x, GW, GP, pair, got


Z_END, XBC_END, DT_END = NH * HP, NH * HP + XBC, NH * HP + XBC + NH


def _pad_to(a, rows=None, cols=None):
    r = 0 if rows is None else rows - a.shape[0]
    c = 0 if cols is None else cols - a.shape[1]
    return jnp.pad(a, ((0, r), (0, c)))


IN_W = DT_END + 2 * D
W_IN_SEGS = [(0, Z_END, "main", COL_Z), (Z_END, XBC_END, "main", COL_XBC), (XBC_END, DT_END, "main", COL_DT),
             (DT_END, DT_END + D, "main", COL_A), (DT_END + D, IN_W, "main", COL_G)]
BIG = [("w_in", True), ("w_out", False), ("w_q", False), ("w_kv", True), ("w_o", False), ("w_gate", False),
       ("w_up", False), ("w_down", False)]
TRANSPOSED = ("w_gate", "w_up")


def _ref_cols(pieces, a, b):
    cw = IN_W // 4
    out = []
    for j in range(4):
        lo, hi = max(a, j * cw), min(b, (j + 1) * cw)
        if lo < hi:
            out.append(pieces[j][:, lo - j * cw:hi - j * cw])
    return out


def _cat_cols(pieces):
    return jnp.concatenate([pieces[j] for j in range(4)], axis=1)


def _pack_in(w_in):
    dt = _ref_cols(w_in, XBC_END, DT_END)
    pad = jnp.zeros((dt[0].shape[0], MAINW - COL_DT - NH), dt[0].dtype)
    main = jnp.concatenate(_ref_cols(w_in, 0, Z_END) + _ref_cols(w_in, DT_END, IN_W) + _ref_cols(w_in, Z_END, XBC_END)
                           + dt + [pad], axis=1)
    return {"main": main}


def _pack_late(pc):
    rows = lambda n: pc[n].reshape(-1, pc[n].shape[-1])
    return {"out": rows("w_out"), "q": rows("w_q"), "kv": _cat_cols(pc["w_kv"]), "o": rows("w_o"),
            "gate_t": rows("w_gate"), "up_t": rows("w_up"), "down": rows("w_down")}


GW_KEY = {"w_gate": "gate_t", "w_up": "up_t", "w_kv": "kv", "w_out": "out", "w_q": "q", "w_o": "o", "w_down": "down"}
RS_GROUPS = (("w_down", "w_gate", "w_up"), ("w_out", "w_q", "w_kv", "w_o"), ("w_in",))


def _shard_grad(name, GW):
    if name == "w_in":
        cw = IN_W // 4
        pieces = []
        for j in range(4):
            parts = []
            for a, b, src, col in W_IN_SEGS:
                lo, hi = max(a, j * cw), min(b, (j + 1) * cw)
                if lo < hi:
                    parts.append(GW[src][:, col + lo - a:col + hi - a])
            pieces.append(jnp.concatenate(parts, axis=1))
        return jnp.stack(pieces)
    g = GW[GW_KEY[name]]
    if dict(BIG)[name]:
        cw = g.shape[1] // 4
        return jnp.stack([g[:, j * cw:(j + 1) * cw] for j in range(4)])
    return g.reshape(4, g.shape[0] // 4, g.shape[1])


def _stack_sc(dt_bias, a_log, d):
    return _pad_to(jnp.concatenate([dt_bias, a_log, d], axis=0), rows=8, cols=128)


COMM_PARAMS = pltpu.CompilerParams(vmem_limit_bytes=VMEM_LIMIT)


def _dma_sems(*counts):
    return [pltpu.SemaphoreType.DMA((n,)) for n in counts]


def _allgather_list(arrs, name):
    n = len(arrs)
    halved = [a.shape[0] % 16 == 0 for a in arrs]
    oshape = [(4, 2, a.shape[0] // 2, a.shape[1]) if h else (4, 1) + a.shape for a, h in zip(arrs, halved)]

    def body(*refs):
        srcs, outs = refs[:n], refs[n:2 * n]
        ici_send, ici_recv, own_send, own_recv, fwd_send, fwd_recv = refs[2 * n:]
        x, y, c = lax.axis_index("x"), lax.axis_index("y"), lax.axis_index("c")
        me = 2 * x + y
        sib = (x, y, 1 - c)
        peers = _chip_peers(x, y)

        def half(i, h):
            r = arrs[i].shape[0] // 2
            if not halved[i]:
                return srcs[i]
            return srcs[i].at[pl.ds(h * r if isinstance(h, int) else pl.multiple_of(h * r, 8), r)]

        ici, own, fwd = [], [], []
        for i in range(n):
            mine_h = c if halved[i] else 0
            for k, (px, py) in enumerate(peers):
                s = 3 * i + k
                ici.append(_remote(half(i, c), outs[i].at[me, mine_h], ici_send.at[s], ici_recv.at[s], (px, py, c)))
            for h in range(2 if halved[i] else 1):
                s = 2 * i + h
                own.append(_remote(half(i, h), outs[i].at[me, h], own_send.at[s], own_recv.at[s], sib))
        for cp in ici + own:
            cp.start()
        for i in range(n):
            if not halved[i]:
                continue
            for k, (px, py) in enumerate(peers):
                s = 3 * i + k
                got = outs[i].at[2 * px + py, c]
                _remote(half(i, c), got, ici_send.at[s], ici_recv.at[s], (px, py, c)).wait_recv()
                f = _remote(got, got, fwd_send.at[s], fwd_recv.at[s], sib)
                f.start()
                fwd.append(f)
        for i in range(n):
            for k, (px, py) in enumerate(peers):
                s = 3 * i + k
                if halved[i]:
                    _remote(half(i, c), outs[i].at[2 * px + py, 1 - c], fwd_send.at[s], fwd_recv.at[s], sib).wait_recv()
                else:
                    _remote(srcs[i], outs[i].at[2 * px + py, 0], ici_send.at[s], ici_recv.at[s], (px, py, c)).wait_recv()
            for h in range(2 if halved[i] else 1):
                s = 2 * i + h
                _remote(half(i, h), outs[i].at[me, h], own_send.at[s], own_recv.at[s], sib).wait_recv()
        for cp in ici + own + fwd:
            cp.wait_send()

    outs = pl.pallas_call(
        body, name=name, in_specs=[HBM_SPEC] * n, out_specs=[HBM_SPEC] * n,
        out_shape=[SDS(s, a.dtype) for s, a in zip(oshape, arrs)],
        scratch_shapes=_dma_sems(3 * n, 3 * n, 2 * n, 2 * n, 3 * n, 3 * n), compiler_params=COMM_PARAMS)(*arrs)
    return [o.reshape((4,) + a.shape) for o, a in zip(outs, arrs)]


def _pair_split_list(gs, name):
    n = len(gs)

    def body(*refs):
        sends, recvs = _pair_copies(refs[:n], refs[n:2 * n], *refs[2 * n:])
        for cp in sends:
            cp.start()
        for cp in recvs:
            cp.wait_recv()
        for cp in sends:
            cp.wait_send()

    return pl.pallas_call(
        body, name=name, in_specs=[HBM_SPEC] * n, out_specs=[HBM_SPEC] * n,
        out_shape=[SDS((4,) + g.shape[2:], g.dtype) for g in gs],
        scratch_shapes=_dma_sems(4 * n, 4 * n), compiler_params=COMM_PARAMS)(*gs)


def _gather_finish_list(shards, bufs, name="allgather_finish"):
    n = len(shards)

    def body(*refs):
        srcs, outs = refs[:n], refs[2 * n:3 * n]
        own_send, own_recv, fwd_send, fwd_recv = refs[3 * n:]
        x, y, c = lax.axis_index("x"), lax.axis_index("y"), lax.axis_index("c")
        me = 2 * x + y
        sib = (x, y, 1 - c)
        sends, recvs = [], []
        for i in range(n):
            for h in range(2):
                own = _remote(_rows_half(srcs[i], shards[i].shape[0], h), outs[i].at[me, h],
                              own_send.at[2 * i + h], own_recv.at[2 * i + h], sib)
                sends.append(own)
                recvs.append(own)
            for k, (px, py) in enumerate(_chip_peers(x, y)):
                got, s = outs[i].at[2 * px + py, c], 3 * i + k
                sends.append(_remote(got, got, fwd_send.at[s], fwd_recv.at[s], sib))
                recvs.append(_remote(got, outs[i].at[2 * px + py, 1 - c], fwd_send.at[s], fwd_recv.at[s], sib))
        for cp in sends:
            cp.start()
        for cp in recvs:
            cp.wait_recv()
        for cp in sends:
            cp.wait_send()

    outs = pl.pallas_call(
        body, name=name, in_specs=[HBM_SPEC] * (2 * n), out_specs=[HBM_SPEC] * n,
        out_shape=[SDS(b.shape, b.dtype) for b in bufs], input_output_aliases={n + i: i for i in range(n)},
        scratch_shapes=_dma_sems(2 * n, 2 * n, 3 * n, 3 * n), compiler_params=COMM_PARAMS)(*shards, *bufs)
    return [o.reshape((4,) + a.shape) for o, a in zip(outs, shards)]


JOIN_SPLIT = 4


def _pair_join_list(bufs, name="rs_pair_join"):
    n = len(bufs)

    def body(*refs):
        outs = refs[n:2 * n]
        send_sems, recv_sems = refs[2 * n:]
        x, y, c = lax.axis_index("x"), lax.axis_index("y"), lax.axis_index("c")
        sib = (x, y, 1 - c)
        sends, recvs = [], []
        for i in range(n):
            rc = bufs[i].shape[1] // JOIN_SPLIT
            for q in range(JOIN_SPLIT):
                k = JOIN_SPLIT * i + q
                rows = pl.ds(q * rc, rc)
                sends.append(_remote(outs[i].at[c, rows], outs[i].at[c, rows], send_sems.at[k], recv_sems.at[k], sib))
                recvs.append(_remote(outs[i].at[c, rows], outs[i].at[1 - c, rows], send_sems.at[k], recv_sems.at[k], sib))
        for cp in sends:
            cp.start()
        for cp in recvs:
            cp.wait_recv()
        for cp in sends:
            cp.wait_send()

    return pl.pallas_call(
        body, name=name, in_specs=[HBM_SPEC] * n, out_specs=[HBM_SPEC] * n,
        out_shape=[SDS(b.shape, b.dtype) for b in bufs], input_output_aliases={i: i for i in range(n)},
        scratch_shapes=_dma_sems(JOIN_SPLIT * n, JOIN_SPLIT * n), compiler_params=COMM_PARAMS)(*bufs)


def _pair_sum(g, theirs, core, name):
    _, _, r, c = g.shape

    def body(core_ref, g_ref, t_ref, o_ref):
        o_ref[...] = (g_ref[...] + t_ref[...]).astype(o_ref.dtype)

    spec = pltpu.PrefetchScalarGridSpec(
        num_scalar_prefetch=1, grid=(4,),
        in_specs=[pl.BlockSpec((None, None, r, c), lambda j, core_ref: (j, core_ref[0], 0, 0)),
                  pl.BlockSpec((None, r, c), lambda j, core_ref: (j, 0, 0))],
        out_specs=pl.BlockSpec((None, r, c), lambda j, core_ref: (j, 0, 0)))
    return pl.pallas_call(body, name=name, grid_spec=spec, out_shape=SDS((4, r, c), BF16),
                          compiler_params=_cp("parallel"))(core, g, theirs)


def _chip_sum(own, got, where, name):
    _, r, c = own.shape
    tr = r // 2

    def body(w_ref, a_ref, b1_ref, b2_ref, b3_ref, o_ref):
        o_ref[...] = ((a_ref[...].astype(F32) + b1_ref[...].astype(F32)) + b2_ref[...].astype(F32)) + b3_ref[...].astype(F32)

    piece = lambda k: pl.BlockSpec((None, tr, c), lambda i, w_ref: ((w_ref[0] + k) % 4, i, 0))
    spec = pltpu.PrefetchScalarGridSpec(
        num_scalar_prefetch=1, grid=(r // tr,), in_specs=[piece(0), piece(1), piece(2), piece(3)],
        out_specs=pl.BlockSpec((None, tr, c), lambda i, w_ref: (w_ref[1], i, 0)))
    return pl.pallas_call(body, name=name, grid_spec=spec, out_shape=SDS((2, r, c), F32),
                          compiler_params=_cp("parallel"))(where, own, got, got, got)


ADAM_BLOCK = 2 ** 19


def _adam_math(w, g, m, v):
    bc1 = 1.0 - ADAM_B1 ** ADAM_STEP
    bc2 = 1.0 - ADAM_B2 ** ADAM_STEP
    mn = ADAM_B1 * m + (1.0 - ADAM_B1) * g
    vn = ADAM_B2 * v + (1.0 - ADAM_B2) * (g * g)
    return -ADAM_LR * ((mn / bc1) / (jnp.sqrt(vn / bc2) + ADAM_EPS) + ADAM_WD * w), mn, vn


PACK_COLS = XBC
PACK = {"g_mix": (0, 1, D), "g_xattn": (1, 1, D), "g_mem": (2, 1, D), "g_ffn": (3, 1, D), "g_final": (4, 1, D),
        "ssd_norm_g": (5, 1, D), "cf_b": (6, 1, D), "ln_g": (7, 1, D), "ln_b": (8, 1, D), "conv4_b": (9, 1, XBC),
        "conv4_w": (10, KS, XBC), "sc": (16, 8, 128), "cf_w": (24, KC, D), "loss": (55, 1, 128)}
PACK_ROWS = 56
SMALL_ADAM = ["g_mix", "g_xattn", "g_mem", "g_ffn", "g_final", "ssd_norm_g", "cf_b", "ln_g", "ln_b", "conv4_b", "sc"]


def _small_allreduce_adamw(grads, wts, mom, var, name="allreduce_small"):
    gk = list(PACK)
    ng, na = len(gk), len(SMALL_ADAM)

    def body(*refs):
        g_in = refs[:ng]
        w_in, m_in, v_in = (refs[ng + i * na: ng + (i + 1) * na] for i in range(3))
        o = refs[ng + 3 * na:]
        g_out = o[:ng]
        d_out, m_out, v_out = (o[ng + i * na: ng + (i + 1) * na] for i in range(3))
        pack, pbuf, psum, cbuf, acc, send_sems, recv_sems = o[ng + 3 * na:]
        x, y, c = lax.axis_index("x"), lax.axis_index("y"), lax.axis_index("c")
        me = 2 * x + y
        pack[...] = jnp.zeros_like(pack)
        for i, k in enumerate(gk):
            r0, nr, nc = PACK[k]
            pack[r0:r0 + nr, 0:nc] = g_in[i][...]
        pair = _remote(pack, pbuf.at[c], send_sems.at[0], recv_sems.at[0], (x, y, 1 - c))
        pair.start()
        pbuf[c] = pack[...]
        _remote(pack, pbuf.at[1 - c], send_sems.at[0], recv_sems.at[0], (x, y, 1 - c)).wait_recv()
        pair.wait_send()
        psum[...] = pbuf[0] + pbuf[1]
        peers = _chip_peers(x, y)
        sends = [_remote(psum, cbuf.at[me], send_sems.at[1 + k], recv_sems.at[1 + k], (px, py, c))
                 for k, (px, py) in enumerate(peers)]
        for cp in sends:
            cp.start()
        cbuf[me] = psum[...]
        for k, (px, py) in enumerate(peers):
            _remote(psum, cbuf.at[2 * px + py], send_sems.at[1 + k], recv_sems.at[1 + k], (px, py, c)).wait_recv()
        for cp in sends:
            cp.wait_send()
        acc[...] = (cbuf[0] + cbuf[1]) + (cbuf[2] + cbuf[3])
        for i, k in enumerate(gk):
            r0, nr, nc = PACK[k]
            g_out[i][...] = acc[r0:r0 + nr, 0:nc]
        for i, k in enumerate(SMALL_ADAM):
            r0, nr, nc = PACK[k]
            d_out[i][...], m_out[i][...], v_out[i][...] = _adam_math(
                w_in[i][...], acc[r0:r0 + nr, 0:nc], m_in[i][...], v_in[i][...])

    args = [grads[k] for k in gk] + [d[k] for d in (wts, mom, var) for k in SMALL_ADAM]
    shp = lambda k: SDS((PACK[k][1], PACK[k][2]), F32)
    vm = pl.BlockSpec(memory_space=pltpu.VMEM)
    outs = pl.pallas_call(
        body, name=name, in_specs=[vm] * len(args), out_specs=[vm] * (ng + 3 * na),
        out_shape=[shp(k) for k in gk] + [shp(k) for _ in range(3) for k in SMALL_ADAM],
        scratch_shapes=[pltpu.VMEM((PACK_ROWS, PACK_COLS), F32), pltpu.VMEM((2, PACK_ROWS, PACK_COLS), F32),
                        pltpu.VMEM((PACK_ROWS, PACK_COLS), F32), pltpu.VMEM((4, PACK_ROWS, PACK_COLS), F32),
                        pltpu.VMEM((PACK_ROWS, PACK_COLS), F32)] + _dma_sems(4, 4),
        compiler_params=COMM_PARAMS)(*args)
    red = dict(zip(gk, outs[:ng]))
    parts = [dict(zip(SMALL_ADAM, outs[ng + i * na: ng + (i + 1) * na])) for i in range(3)]
    return red, parts[0], parts[1], parts[2]


def _adamw_cols(w, gfull, m, v, chip, name):
    _, R, C = w.shape

    def body(w_idx, w_ref, g_ref, m_ref, v_ref, go_ref, d_ref, mo_ref, vo_ref):
        go_ref[...] = g_ref[...]
        d_ref[...], mo_ref[...], vo_ref[...] = _adam_math(w_ref[...], g_ref[...], m_ref[...], v_ref[...])

    blk = pl.BlockSpec((None, R, C), lambda i, w_idx: (0, 0, 0))
    spec = pltpu.PrefetchScalarGridSpec(
        num_scalar_prefetch=1, grid=(1,),
        in_specs=[blk, pl.BlockSpec((R, C), lambda i, w_idx: (0, w_idx[0])), blk, blk], out_specs=[blk] * 4)
    return pl.pallas_call(body, name=name, grid_spec=spec, out_shape=[SDS((1, R, C), F32)] * 4,
                          compiler_params=_cp("arbitrary"))(chip, w, gfull, m, v)


def _adamw(w, g, m, v, name):
    _, R, C = w.shape
    half = R // 2
    tr = _tile(half, max(8, (ADAM_BLOCK // C) // 8 * 8), 8)
    nh = half // tr

    def body(w_ref, g_ref, m_ref, v_ref, go_ref, d_ref, mo_ref, vo_ref):
        go_ref[...] = g_ref[...]
        d_ref[...], mo_ref[...], vo_ref[...] = _adam_math(w_ref[...], g_ref[...], m_ref[...], v_ref[...])

    blk = pl.BlockSpec((None, tr, C), lambda i: (0, i, 0))
    gblk = pl.BlockSpec((None, tr, C), lambda i: (i // nh, i % nh, 0))
    return pl.pallas_call(body, name=name, grid=(R // tr,), in_specs=[blk, gblk, blk, blk], out_specs=[blk] * 4,
                          out_shape=[SDS((1, R, C), F32)] * 4, compiler_params=_cp("parallel"))(w, g, m, v)


WEIGHT_NAMES = ["norm_mix_g", "w_in", "ssd_conv_w", "ssd_conv_b", "ssd_dt_bias", "ssd_A_log", "ssd_D", "ssd_norm_g",
                "cf_conv_w", "cf_conv_b", "cf_ln_g", "cf_ln_b", "w_out", "norm_xattn_g", "norm_mem_g", "w_q", "w_kv",
                "w_o", "norm_ffn_g", "w_gate", "w_up", "w_down", "norm_final_g"]
VEC_REF = [("norm_mix_g", "g_mix"), ("norm_xattn_g", "g_xattn"), ("norm_mem_g", "g_mem"), ("norm_ffn_g", "g_ffn"),
           ("norm_final_g", "g_final"), ("ssd_norm_g", "ssd_norm_g"), ("cf_conv_b", "cf_b"), ("cf_ln_g", "ln_g"),
           ("cf_ln_b", "ln_b"), ("ssd_conv_b", "conv4_b")]
SC_REF = ["ssd_dt_bias", "ssd_A_log", "ssd_D"]


def _small_side(get):
    d = {k: get(ref_name).reshape(1, -1) for ref_name, k in VEC_REF}
    d["sc"] = _stack_sc(*[get(n) for n in SC_REF])
    return d


def kernel(x, mem, norm_mix_g, w_in, ssd_conv_w, ssd_conv_b, ssd_dt_bias, ssd_A_log, ssd_D, ssd_norm_g, cf_conv_w, cf_conv_b, cf_ln_g, cf_ln_b, w_out, norm_xattn_g, norm_mem_g, w_q, w_kv, w_o, norm_ffn_g, w_gate, w_up, w_down, norm_final_g, loss_target, m_norm_mix_g, m_w_in, m_ssd_conv_w, m_ssd_conv_b, m_ssd_dt_bias, m_ssd_A_log, m_ssd_D, m_ssd_norm_g, m_cf_conv_w, m_cf_conv_b, m_cf_ln_g, m_cf_ln_b, m_w_out, m_norm_xattn_g, m_norm_mem_g, m_w_q, m_w_kv, m_w_o, m_norm_ffn_g, m_w_gate, m_w_up, m_w_down, m_norm_final_g, v_norm_mix_g, v_w_in, v_ssd_conv_w, v_ssd_conv_b, v_ssd_dt_bias, v_ssd_A_log, v_ssd_D, v_ssd_norm_g, v_cf_conv_w, v_cf_conv_b, v_cf_ln_g, v_cf_ln_b, v_w_out, v_norm_xattn_g, v_norm_mem_g, v_w_q, v_w_kv, v_w_o, v_norm_ffn_g, v_w_gate, v_w_up, v_w_down, v_norm_final_g):
    env = dict(locals())
    view = lambda n, a: a.transpose(0, 2, 1) if n in TRANSPOSED else a
    wts = {n: view(n, env[n]) for n in WEIGHT_NAMES}
    mom = {n: view(n, env["m_" + n]) for n in WEIGHT_NAMES}
    var = {n: view(n, env["v_" + n]) for n in WEIGHT_NAMES}
    chip = (2 * lax.axis_index("x") + lax.axis_index("y")).astype(jnp.int32).reshape(1)
    core = lax.axis_index("c").astype(jnp.int32).reshape(1)
    where = jnp.concatenate([chip, core])
    big = [n for n, _ in BIG]

    w_in_g, conv4_g, cf_g = _allgather_list([w_in[0].astype(BF16), ssd_conv_w[0], cf_conv_w[0]], "allgather_first")
    W = _pack_in(w_in_g)
    P = _small_side(lambda n: wts[n])
    P["conv4_w"], P["cf_w"] = _cat_cols(conv4_g), _cat_cols(cf_g)
    late = {n: wts[n][0].astype(BF16) for grp in AG_RIDE for n in grp}

    loss, grad_x, GW, GP, pair, got = _local_step(x[0], mem[0], loss_target[0], W, P, core, late)
    joined = _pair_join_list([_chip_sum(pair[n], got[n], where, "rs_chip_sum_" + n) for n in big])
    gshard = dict(zip(big, joined))

    small = dict(GP)
    small["loss"] = loss
    red, sd, sm, sv = _small_allreduce_adamw(small, {k: P[k] for k in SMALL_ADAM}, _small_side(lambda n: mom[n]),
                                             _small_side(lambda n: var[n]))
    grads, delta, new_m, new_v = {}, {}, {}, {}
    for ref_name, k in VEC_REF:
        shp = wts[ref_name].shape
        for dst, src in ((grads, red), (delta, sd), (new_m, sm), (new_v, sv)):
            dst[ref_name] = src[k].reshape(shp)
    for row, ref_name in enumerate(SC_REF):
        for dst, src in ((grads, red), (delta, sd), (new_m, sm), (new_v, sv)):
            dst[ref_name] = src["sc"][row:row + 1, :NH]

    for n, k in (("ssd_conv_w", "conv4_w"), ("cf_conv_w", "cf_w")):
        grads[n], delta[n], new_m[n], new_v[n] = _adamw_cols(wts[n], red[k], mom[n], var[n], chip, "adamw_" + n)
    for n in big:
        outs = _adamw(wts[n], gshard[n], mom[n], var[n], "adamw_" + n)
        grads[n], delta[n], new_m[n], new_v[n] = [view(n, o) for o in outs]

    return (red["loss"][0, 0], grad_x[None], *[grads[n] for n in WEIGHT_NAMES], *[delta[n] for n in WEIGHT_NAMES],
            *[new_m[n] for n in WEIGHT_NAMES], *[new_v[n] for n in WEIGHT_NAMES])
```

```python
import functools
import math

import jax
import jax.numpy as jnp
from jax import lax
from jax.experimental import pallas as pl
from jax.experimental.pallas import tpu as pltpu

F32 = jnp.float32
BF16 = jnp.bfloat16
_MXU = BF16

D = 1024
MEM = 256
NH, HP, NG, NS = 16, 64, 2, 128
GW = NH * HP // NG
CH = 128
XBC = NH * HP + 2 * NG * NS
KS, KC = 4, 31
XH, XD = 4, 256
DFF = 2816
EPS = 1e-6
COL_Z, COL_A, COL_G, COL_XBC, COL_DT, MAINW = 0, 1024, 2048, 3072, 4608, 4736
VMEM_LIMIT = 56 * 2 ** 20

ADAM_LR, ADAM_B1, ADAM_B2, ADAM_EPS, ADAM_WD, ADAM_STEP = 0.001, 0.9, 0.999, 1e-08, 0.01, 10

SDS = jax.ShapeDtypeStruct
MESHID = pl.DeviceIdType.MESH


def _cp(*sem):
    return pltpu.CompilerParams(dimension_semantics=sem, vmem_limit_bytes=VMEM_LIMIT)


def _tile(n, cap, unit=128):
    if n <= cap:
        return n
    best = None
    for t in range(unit, cap + 1, unit):
        if n % t == 0:
            best = t
    assert best is not None, (n, cap)
    return best


def _sigmoid(x):
    return 1.0 / (1.0 + jnp.exp(-x))


def _silu(x):
    return x * _sigmoid(x)


def _dsilu(x):
    s = _sigmoid(x)
    return s * (1.0 + x * (1.0 - s))


def _softplus(x):
    return jnp.maximum(x, 0.0) + jnp.log(1.0 + jnp.exp(-jnp.abs(x)))


def _split_bf16(x, passes):
    parts, r = [], x.astype(F32)
    for _ in range(passes):
        p = r.astype(BF16)
        parts.append(p)
        r = r - p.astype(F32)
    return parts


def _dot(a, b, dims=None, exact=None, passes=2):
    dn = {None: (((1,), (0,)), ((), ())), "nt": (((1,), (1,)), ((), ())), "tn": (((0,), (0,)), ((), ()))}[dims]
    if exact is None:
        return lax.dot_general(a.astype(_MXU), b.astype(_MXU), dn, preferred_element_type=F32)
    if exact == "a":
        terms = [(a.astype(BF16), p) for p in _split_bf16(b, passes)]
    else:
        terms = [(p, b.astype(BF16)) for p in _split_bf16(a, passes)]
    out = None
    for lhs, rhs in terms:
        d = lax.dot_general(lhs, rhs, dn, preferred_element_type=F32)
        out = d if out is None else out + d
    return out


def _rms_bwd_tile(xv, gv, dy, dres):
    r = lax.rsqrt(jnp.mean(xv * xv, axis=-1, keepdims=True) + EPS)
    xh = xv * r
    gdy = dy * gv
    dx = r * (gdy - xh * jnp.mean(xh * gdy, axis=-1, keepdims=True))
    return dres + dx, jnp.sum(dy * xh, axis=0, keepdims=True)


def _matmul(kind, a, b, name, add, out_dtype, tm, tw, riders, rms, norm=None, loss=None):
    a_parts = list(a) if isinstance(a, (list, tuple)) else [a]
    na = len(a_parts)
    M, K = a_parts[0].shape[0], sum(p.shape[1] for p in a_parts)
    Wd = b.shape[1] if kind == "nn" else b.shape[0]
    rd = _Riders(riders or ())
    nco = len(rd.arrays())
    nin = na + 1 + (add is not None) + (3 if rms else 0) + (norm is not None) + (2 if loss else 0)
    low = bool(rms and rms[3])
    nout = (2 + low) if rms else 2 if norm is not None else 4 if loss else 1
    grid = (Wd // tw, M // tm)
    assert not (rms or loss or norm is not None) or tw == Wd, "the row-wise epilogues need whole rows"

    def body(*refs):
        b_ref = refs[na]
        av = refs[0][...] if na == 1 else jnp.concatenate([r[...] for r in refs[:na]], axis=1)
        outs = refs[nin + nco:nin + nco + nout]
        rd.bind(refs[nin:nin + nco], refs[nin + nco + nout:nin + 2 * nco + nout], refs[nin + 2 * nco + nout:], grid).start()
        acc = _dot(av, b_ref[...], None if kind == "nn" else "nt")
        if add is not None:
            acc = acc + refs[na + 1][...]
        if loss:
            lpart, dx, dg = _final_loss_tile(acc, refs[nin - 2][...], refs[nin - 1][...])

            @pl.when(pl.program_id(1) == 0)
            def _():
                outs[0][...] = jnp.zeros_like(outs[0])
                outs[3][...] = jnp.zeros_like(outs[3])

            outs[0][...] += lpart
            outs[1][...] = dx
            outs[2][...] = dx.astype(outs[2].dtype)
            outs[3][...] += dg
        elif norm is not None:
            outs[0][...] = acc.astype(outs[0].dtype)
            r = lax.rsqrt(jnp.mean(acc * acc, axis=-1, keepdims=True) + EPS)
            outs[1][...] = (acc * r * refs[nin - 1][...]).astype(outs[1].dtype)
        elif rms:
            x_ref, g_ref, dres_ref = refs[nin - 3:nin]
            tot, dg = _rms_bwd_tile(x_ref[...], g_ref[...], acc, dres_ref[...])

            @pl.when(pl.program_id(1) == 0)
            def _():
                outs[-1][...] = jnp.zeros_like(outs[-1])

            outs[-1][...] += dg
            outs[0][...] = tot
            if low:
                outs[1][...] = tot.astype(outs[1].dtype)
        else:
            outs[0][...] = acc.astype(outs[0].dtype)
        rd.finish()

    tile = pl.BlockSpec((tm, tw), lambda j, i: (i, j))
    bspec = pl.BlockSpec((K, tw), lambda j, i: (0, j)) if kind == "nn" else pl.BlockSpec((tw, K), lambda j, i: (j, 0))
    in_specs = [pl.BlockSpec((tm, p.shape[1]), lambda j, i: (i, 0)) for p in a_parts] + [bspec]
    args = a_parts + [b]
    if add is not None:
        in_specs.append(tile)
        args.append(add)
    vec = pl.BlockSpec((1, tw), lambda j, i: (0, j))
    if rms:
        in_specs += [tile, vec, tile]
        args += [rms[0], rms[1], rms[2]]
        out_specs = [tile] * (1 + low) + [vec]
        out_shape = [SDS((M, Wd), F32)] + ([SDS((M, Wd), _MXU)] if low else []) + [SDS((1, Wd), F32)]
    elif loss:
        in_specs += [vec, tile]
        args += [loss[0], loss[1]]
        out_specs = [pl.BlockSpec((1, 128), lambda j, i: (0, 0)), tile, tile, vec]
        out_shape = [SDS((1, 128), F32), SDS((M, Wd), F32), SDS((M, Wd), _MXU), SDS((1, Wd), F32)]
    elif norm is not None:
        in_specs.append(vec)
        args.append(norm)
        out_specs, out_shape = [tile, tile], [SDS((M, Wd), out_dtype), SDS((M, Wd), _MXU)]
    else:
        out_specs, out_shape = [tile], [SDS((M, Wd), out_dtype)]
    order = ("arbitrary", "arbitrary") if (nco or rms or loss) else ("parallel", "parallel")
    outs = pl.pallas_call(
        body, name=name, grid=grid, in_specs=in_specs + [HBM_SPEC] * nco, out_specs=out_specs + [HBM_SPEC] * nco,
        out_shape=out_shape + rd.out_shapes(), scratch_shapes=rd.scratch(),
        compiler_params=_cp(*order))(*args, *rd.arrays())
    main = tuple(outs[:nout]) if nout > 1 else outs[0]
    return main if riders is None else (main, rd.split(outs[nout:]))


def _mm_nn(a, b, name, add=None, out_dtype=F32, tm_cap=1024, tn_cap=1408, riders=None, rms=None, norm=None, loss=None):
    tm, tn = _tile(a.shape[0], tm_cap, 8), _tile(b.shape[1], tn_cap)
    return _matmul("nn", a, b, name, add, out_dtype, tm, tn, riders, rms, norm, loss)


def _mm_nt(a, b, name, add=None, out_dtype=F32, tm_cap=512, tk_cap=1024, riders=None, rms=None):
    rows = (a[0] if isinstance(a, (list, tuple)) else a).shape[0]
    tm, tk = _tile(rows, tm_cap, 8), _tile(b.shape[0], tk_cap)
    return _matmul("nt", a, b, name, add, out_dtype, tm, tk, riders, rms)


def _mm_tn(a, b, name, tm_cap=1024, tk_cap=512, tn_cap=1408):
    b_parts = list(b) if isinstance(b, (list, tuple)) else [b]
    nb = len(b_parts)
    M, K = a.shape
    N = sum(p.shape[1] for p in b_parts)
    tm, tk, tn = _tile(M, tm_cap, 8), _tile(K, tk_cap), _tile(N, tn_cap)
    assert nb == 1 or tn == N

    def body(a_ref, *rest):
        o_ref = rest[nb]
        bv = rest[0][...] if nb == 1 else jnp.concatenate([r[...] for r in rest[:nb]], axis=1)

        @pl.when(pl.program_id(2) == 0)
        def _():
            o_ref[...] = jnp.zeros_like(o_ref)

        o_ref[...] += _dot(a_ref[...], bv, "tn")

    b_specs = ([pl.BlockSpec((tm, tn), lambda k, n, m: (m, n))] if nb == 1 else
               [pl.BlockSpec((tm, p.shape[1]), lambda k, n, m: (m, 0)) for p in b_parts])
    return pl.pallas_call(
        body, name=name, grid=(K // tk, N // tn, M // tm),
        in_specs=[pl.BlockSpec((tm, tk), lambda k, n, m: (m, k))] + b_specs,
        out_specs=pl.BlockSpec((tk, tn), lambda k, n, m: (k, n)), out_shape=SDS((K, N), F32),
        compiler_params=_cp("parallel", "parallel", "arbitrary"))(a, *b_parts)


def _rms_fwd(x, g, name, tb_cap=512):
    S, Dm = x.shape
    tb = _tile(S, tb_cap, 8)

    def body(x_ref, g_ref, o_ref):
        xv = x_ref[...]
        r = lax.rsqrt(jnp.mean(xv * xv, axis=-1, keepdims=True) + EPS)
        o_ref[...] = (xv * r * g_ref[...]).astype(o_ref.dtype)

    return pl.pallas_call(
        body, name=name, grid=(S // tb,),
        in_specs=[pl.BlockSpec((tb, Dm), lambda i: (i, 0)), pl.BlockSpec((1, Dm), lambda i: (0, 0))],
        out_specs=pl.BlockSpec((tb, Dm), lambda i: (i, 0)), out_shape=SDS((S, Dm), _MXU),
        compiler_params=_cp("parallel"))(x, g)


def _rms_bwd(x, g, dh, dres, name, tb_cap=512, low=True):
    S, Dm = x.shape
    tb = _tile(S, tb_cap, 8)
    need_dx = dres is not None

    def body(x_ref, g_ref, dh_ref, *rest):
        dg_ref = rest[-1]
        tot, dg = _rms_bwd_tile(x_ref[...], g_ref[...], dh_ref[...].astype(F32), rest[0][...] if need_dx else 0.0)

        @pl.when(pl.program_id(0) == 0)
        def _():
            dg_ref[...] = jnp.zeros_like(dg_ref)

        dg_ref[...] += dg
        if need_dx:
            rest[1][...] = tot
            if low:
                rest[2][...] = tot.astype(rest[2].dtype)

    row = pl.BlockSpec((tb, Dm), lambda i: (i, 0))
    vec = pl.BlockSpec((1, Dm), lambda i: (0, 0))
    if need_dx:
        outs = [SDS((S, Dm), F32)] + ([SDS((S, Dm), _MXU)] if low else [])
        return pl.pallas_call(
            body, name=name, grid=(S // tb,), in_specs=[row, vec, row, row], out_specs=[row] * len(outs) + [vec],
            out_shape=outs + [SDS((1, Dm), F32)], compiler_params=_cp("arbitrary"))(x, g, dh, dres)
    return pl.pallas_call(
        body, name=name, grid=(S // tb,), in_specs=[row, vec, row], out_specs=vec,
        out_shape=SDS((1, Dm), F32), compiler_params=_cp("arbitrary"))(x, g, dh)


def _final_loss_tile(xv, gv, tv):
    r = lax.rsqrt(jnp.mean(xv * xv, axis=-1, keepdims=True) + EPS)
    xh = xv * r
    e = xh * gv - tv
    dy = e * (1.0 / xv.shape[-1])
    gdy = dy * gv
    dx = r * (gdy - xh * jnp.mean(xh * gdy, axis=-1, keepdims=True))
    return 0.5 * jnp.sum(jnp.mean(e * e, axis=-1, keepdims=True)), dx, jnp.sum(dy * xh, axis=0, keepdims=True)


SSD_HALO = 8
CF_HALO = 32

HBM_SPEC = pl.BlockSpec(memory_space=pl.ANY)


def _chip_peers(x, y):
    return [(1 - x, y), (x, 1 - y), (1 - x, 1 - y)]


def _remote(src, dst, send_sem, recv_sem, dev):
    return pltpu.make_async_remote_copy(src_ref=src, dst_ref=dst, send_sem=send_sem, recv_sem=recv_sem,
                                        device_id=dev, device_id_type=MESHID)


def _scatter_copies(srcs, outs, send_sems, recv_sems):
    x, y, c = lax.axis_index("x"), lax.axis_index("y"), lax.axis_index("c")
    me = 2 * x + y
    sends, recvs = [], []
    for i, (s, o) in enumerate(zip(srcs, outs)):
        for k, (px, py) in enumerate(_chip_peers(x, y)):
            j = 3 * i + k
            sends.append(_remote(s.at[2 * px + py], o.at[me], send_sems.at[j], recv_sems.at[j], (px, py, c)))
            recvs.append(_remote(s.at[me], o.at[2 * px + py], send_sems.at[j], recv_sems.at[j], (px, py, c)))
    return sends, recvs


def _pair_copies(srcs, outs, send_sems, recv_sems):
    x, y, c = lax.axis_index("x"), lax.axis_index("y"), lax.axis_index("c")
    sends = [_remote(s.at[j, 1 - c], o.at[j], send_sems.at[4 * i + j], recv_sems.at[4 * i + j], (x, y, 1 - c))
             for i, (s, o) in enumerate(zip(srcs, outs)) for j in range(4)]
    return sends, sends


def _rows_half(ref, rows, h):
    r = rows // 2
    return ref.at[pl.ds(h * r if isinstance(h, int) else pl.multiple_of(h * r, 8), r)]


def _gather_copies(srcs, outs, rows, send_sems, recv_sems):
    x, y, c = lax.axis_index("x"), lax.axis_index("y"), lax.axis_index("c")
    me = 2 * x + y
    sends, recvs = [], []
    for i, (s, o) in enumerate(zip(srcs, outs)):
        mine = _rows_half(s, rows[i], c)
        for k, (px, py) in enumerate(_chip_peers(x, y)):
            j = 3 * i + k
            sends.append(_remote(mine, o.at[me, c], send_sems.at[j], recv_sems.at[j], (px, py, c)))
            recvs.append(_remote(mine, o.at[2 * px + py, c], send_sems.at[j], recv_sems.at[j], (px, py, c)))
    return sends, recvs


def _gather_shapes(shards):
    return [SDS((4, 2, a.shape[0] // 2, a.shape[1]), a.dtype) for a in shards]


class _Rider:
    SEMS_PER_ARRAY = {"exchange": 3, "gather": 3, "pair": 4}

    def __init__(self, kind, arrays):
        self.kind, self.arrays = kind, list(arrays)

    def out_shapes(self):
        if self.kind == "gather":
            return _gather_shapes(self.arrays)
        if self.kind == "pair":
            return [SDS((4,) + a.shape[2:], a.dtype) for a in self.arrays]
        return [SDS(a.shape, a.dtype) for a in self.arrays]

    def scratch(self):
        n = self.SEMS_PER_ARRAY[self.kind] * len(self.arrays)
        return [pltpu.SemaphoreType.DMA((n,)), pltpu.SemaphoreType.DMA((n,))]

    def copies(self, srcs, outs, send_sems, recv_sems):
        if self.kind == "gather":
            return _gather_copies(srcs, outs, [a.shape[0] for a in self.arrays], send_sems, recv_sems)
        if self.kind == "pair":
            return _pair_copies(srcs, outs, send_sems, recv_sems)
        return _scatter_copies(srcs, outs, send_sems, recv_sems)


class _Riders:
    def __init__(self, riders):
        self.given = list(riders)
        self.riders = [r for r in self.given if r.arrays]

    def arrays(self):
        return [a for r in self.riders for a in r.arrays]

    def out_shapes(self):
        return [s for r in self.riders for s in r.out_shapes()]

    def scratch(self):
        return [s for r in self.riders for s in r.scratch()]

    def split(self, outs):
        res, k = [], 0
        for r in self.given:
            res.append(list(outs[k:k + len(r.arrays)]))
            k += len(r.arrays)
        return res

    def bind(self, in_refs, out_refs, sem_refs, steps):
        self.steps = steps if isinstance(steps, tuple) else (steps,)
        self.bound, k = [], 0
        for i, r in enumerate(self.riders):
            n = len(r.arrays)
            self.bound.append((r, in_refs[k:k + n], out_refs[k:k + n], sem_refs[2 * i], sem_refs[2 * i + 1]))
            k += n
        return self

    def _at(self, last):
        hit = None
        for ax, n in enumerate(self.steps):
            here = pl.program_id(ax) == (n - 1 if last else 0)
            hit = here if hit is None else jnp.logical_and(hit, here)
        return hit

    def _copies(self):
        sends, recvs = [], []
        for r, srcs, outs, send_sems, recv_sems in self.bound:
            s, w = r.copies(srcs, outs, send_sems, recv_sems)
            sends += s
            recvs += w
        return sends, recvs

    def start(self):
        if self.riders:
            @pl.when(self._at(last=False))
            def _():
                for cp in self._copies()[0]:
                    cp.start()

    def finish(self):
        if self.riders:
            @pl.when(self._at(last=True))
            def _():
                sends, recvs = self._copies()
                for cp in recvs:
                    cp.wait_recv()
                for cp in sends:
                    cp.wait_send()


def _head_consts():
    e = (lax.broadcasted_iota(jnp.int32, (128, NH * HP), 1) // HP == lax.broadcasted_iota(jnp.int32, (128, NH * HP), 0)).astype(F32)
    et = (lax.broadcasted_iota(jnp.int32, (NH * HP, 128), 0) // HP == lax.broadcasted_iota(jnp.int32, (NH * HP, 128), 1)).astype(F32)
    r = lax.broadcasted_iota(jnp.int32, (CH, CH), 0)
    c = lax.broadcasted_iota(jnp.int32, (CH, CH), 1)
    return e, et, (c <= r), (r <= c)


def _ssd_common(xbc_c, dtr, dtb, alog, e, tril, triu):
    xbc = _silu(xbc_c)
    xs = xbc[:, :NH * HP]
    dt = _softplus(dtr + dtb)
    A = -jnp.exp(alog)
    a = dt * A
    cs = _dot(tril, a, exact="a", passes=3)
    csT = _dot(a, triu, "tn", exact="b", passes=3)
    csL = cs[CH - 1:CH, :]
    wdec = jnp.exp(csL - cs) * dt
    dtE = _dot(dt, e, exact="b")
    ecsE = _dot(jnp.exp(cs), e, exact="b")
    wE = _dot(wdec, e, exact="b")
    eL = jnp.exp(csL)
    return xbc, xs, dt, A, cs, csT, csL, wdec, dtE, ecsE, wE, eL


def _ssd_fwd(proj, cw, cb, sc, norm_g, riders=(), name="ssd_fwd"):
    S = proj.shape[0]
    nc = S // CH
    rd = _Riders(riders)
    nco = len(rd.arrays())

    def body(*refs):
        z_ref, xp_ref, cw_ref, cb_ref, dtr_ref, sc_ref, ng_ref = refs[:7]
        xc_ref, y_ref, yn_ref, hp_ref = refs[7 + nco:11 + nco]
        hst, cext = refs[11 + 2 * nco:13 + 2 * nco]
        rd.bind(refs[7:7 + nco], refs[11 + nco:11 + 2 * nco], refs[13 + 2 * nco:], nc).start()

        @pl.when(pl.program_id(0) == 0)
        def _():
            hst[...] = jnp.zeros_like(hst)
            cext[pl.ds(0, SSD_HALO), :] = jnp.zeros((SSD_HALO, XBC), F32)

        cext[pl.ds(SSD_HALO, CH), :] = xp_ref[...]
        xc = jnp.zeros((CH, XBC), F32) + cb_ref[...]
        for k in range(KS):
            xc = xc + cext[pl.ds(SSD_HALO - (KS - 1) + k, CH), :] * cw_ref[k:k + 1, :]
        xc_ref[...] = xc
        cext[pl.ds(0, SSD_HALO), :] = cext[pl.ds(CH, SSD_HALO), :]

        e, et, tril, triu = _head_consts()
        xbc, xs, dt, A, cs, csT, csL, wdec, dtE, ecsE, wE, eL = _ssd_common(
            xc, dtr_ref[...], sc_ref[0:1, :], sc_ref[1:2, :], e, tril, triu)
        hp_ref[0] = hst[...]
        xd = xs * dtE
        xw = xs * wE
        dE = _dot(jnp.broadcast_to(sc_ref[2:3, :], (8, 128)), e, exact="b", passes=3)[0:1, :]
        eLcol = jnp.sum(et * eL, axis=1, keepdims=True)
        for g in range(NG):
            Bg = xbc[:, NH * HP + g * NS: NH * HP + (g + 1) * NS]
            Cg = xbc[:, NH * HP + NG * NS + g * NS: NH * HP + NG * NS + (g + 1) * NS]
            gs = slice(g * GW, (g + 1) * GW)
            G = _dot(Cg, Bg, "nt")
            hg = hst[gs, :]
            yoff = ecsE[:, gs] * _dot(Cg, hg, "nt")
            hst[gs, :] = eLcol[gs, :] * hg + _dot(xw[:, gs], Bg, "tn")
            for hh in range(NH // NG):
                h = g * (NH // NG) + hh
                hs = slice(h * HP, (h + 1) * HP)
                m = jnp.where(tril, jnp.exp(jnp.where(tril, cs[:, h:h + 1] - csT[h:h + 1, :], 0.0)), 0.0)
                yd = _dot(G * m, xd[:, hs])
                y_ref[:, hs] = yd + yoff[:, hh * HP:(hh + 1) * HP] + dE[:, hs] * xs[:, hs]
        y = y_ref[...]
        yz = y * _silu(z_ref[...])
        for g in range(NG):
            gs = slice(g * GW, (g + 1) * GW)
            yg = yz[:, gs]
            r = lax.rsqrt(jnp.mean(yg * yg, axis=-1, keepdims=True) + EPS)
            yn_ref[:, gs] = (yg * r * ng_ref[:, gs]).astype(yn_ref.dtype)
        rd.finish()

    outs = pl.pallas_call(
        body, name=name, grid=(nc,),
        in_specs=[pl.BlockSpec((CH, D), lambda c: (c, COL_Z // D)),
                  pl.BlockSpec((CH, XBC), lambda c: (c, COL_XBC // XBC)),
                  pl.BlockSpec((KS, XBC), lambda c: (0, 0)),
                  pl.BlockSpec((1, XBC), lambda c: (0, 0)),
                  pl.BlockSpec((CH, 128), lambda c: (c, COL_DT // 128)),
                  pl.BlockSpec((8, 128), lambda c: (0, 0)),
                  pl.BlockSpec((1, D), lambda c: (0, 0))] + [HBM_SPEC] * nco,
        out_specs=[pl.BlockSpec((CH, XBC), lambda c: (c, 0)), pl.BlockSpec((CH, D), lambda c: (c, 0)),
                   pl.BlockSpec((CH, D), lambda c: (c, 0)),
                   pl.BlockSpec((1, NH * HP, NS), lambda c: (c, 0, 0))] + [HBM_SPEC] * nco,
        out_shape=[SDS((S, XBC), F32), SDS((S, D), F32), SDS((S, D), _MXU), SDS((nc, NH * HP, NS), F32)]
        + rd.out_shapes(),
        scratch_shapes=[pltpu.VMEM((NH * HP, NS), F32), pltpu.VMEM((SSD_HALO + CH, XBC), F32)] + rd.scratch(),
        compiler_params=_cp("arbitrary"))(proj, proj, cw, cb, proj, sc, norm_g, *rd.arrays())
    return outs[:4], rd.split(outs[4:])


def _ssd_bwd(dmix, y, proj, xbc_c, hprev, cw, sc, norm_g, riders=(), name="ssd_bwd"):
    S = proj.shape[0]
    nc = S // CH
    rd = _Riders(riders)
    nco = len(rd.arrays())
    rev = lambda c: nc - 1 - c

    def body(*refs):
        dyn_ref, y_ref, z_ref, x_ref, xp_ref, dtr_ref, hp_ref, cw_ref, sc_ref, ng_ref = refs[:10]
        dz_ref, dx_ref, ddtr_ref, gcw_ref, gcb_ref, gsc_ref, gng_ref = refs[10 + nco:17 + nco]
        dh, dxd, cext = refs[17 + 2 * nco:20 + 2 * nco]
        rd.bind(refs[10:10 + nco], refs[17 + nco:17 + 2 * nco], refs[20 + 2 * nco:], nc).start()

        @pl.when(pl.program_id(0) == 0)
        def _():
            dh[...] = jnp.zeros_like(dh)
            cext[pl.ds(CH, SSD_HALO), :] = jnp.zeros((SSD_HALO, XBC), F32)
            gcw_ref[...] = jnp.zeros_like(gcw_ref)
            gcb_ref[...] = jnp.zeros_like(gcb_ref)
            gsc_ref[...] = jnp.zeros_like(gsc_ref)
            gng_ref[...] = jnp.zeros_like(gng_ref)

        e, et, tril, triu = _head_consts()
        xbc_c = x_ref[...]
        dtr = dtr_ref[...]
        dtb = sc_ref[0:1, :]
        xbc, xs, dt, A, cs, csT, csL, wdec, dtE, ecsE, wE, eL = _ssd_common(
            xbc_c, dtr, dtb, sc_ref[1:2, :], e, tril, triu)
        xd = xs * dtE
        xw = xs * wE
        dE = _dot(jnp.broadcast_to(sc_ref[2:3, :], (8, 128)), e, exact="b", passes=3)[0:1, :]
        eLcol = jnp.sum(et * eL, axis=1, keepdims=True)

        yv = y_ref[...]
        zv = z_ref[...]
        sz = _silu(zv)
        yz = yv * sz
        dyn = dyn_ref[...]
        dyz_parts = []
        for g in range(NG):
            gs = slice(g * GW, (g + 1) * GW)
            yg = yz[:, gs]
            r = lax.rsqrt(jnp.mean(yg * yg, axis=-1, keepdims=True) + EPS)
            yh = yg * r
            dn = dyn[:, gs]
            gng_ref[:, gs] += jnp.sum(dn * yh, axis=0, keepdims=True)
            gdn = dn * ng_ref[:, gs]
            dyz_parts.append(r * (gdn - yh * jnp.mean(yh * gdn, axis=-1, keepdims=True)))
        dyz = jnp.concatenate(dyz_parts, axis=1)
        dy = dyz * sz
        dz_ref[...] = (dyz * yv * _dsilu(zv)).astype(dz_ref.dtype)

        dxs = dE * dy
        dzo = ecsE * dy
        dcsL = jnp.zeros((1, 128), F32)
        ddt = jnp.zeros((CH, 128), F32)
        qcols = jnp.zeros((CH, 128), F32)
        qrows = jnp.zeros((128, CH), F32)
        lane = lax.broadcasted_iota(jnp.int32, (1, 128), 1)
        sub = lax.broadcasted_iota(jnp.int32, (128, 1), 0)
        dB_parts, dC_parts, yoff_parts, dxw_parts = [], [], [], []
        for g in range(NG):
            Bg = xbc[:, NH * HP + g * NS: NH * HP + (g + 1) * NS]
            Cg = xbc[:, NH * HP + NG * NS + g * NS: NH * HP + NG * NS + (g + 1) * NS]
            gs = slice(g * GW, (g + 1) * GW)
            hg = hp_ref[0, gs, :]
            dhn = dh[gs, :]
            G = _dot(Cg, Bg, "nt")
            yoff_parts.append(ecsE[:, gs] * _dot(Cg, hg, "nt"))
            dC = _dot(dzo[:, gs], hg)
            dhp = _dot(dzo[:, gs], Cg, "tn") + eLcol[gs, :] * dhn
            t1 = jnp.sum(dhn * hg, axis=1, keepdims=True) * eLcol[gs, :]
            dcsL = dcsL + jnp.sum(et[gs, :] * t1, axis=0, keepdims=True)
            dxw_parts.append(_dot(Bg, dhn, "nt"))
            dB = _dot(xw[:, gs], dhn)
            dgsum = jnp.zeros((CH, CH), F32)
            for hh in range(NH // NG):
                h = g * (NH // NG) + hh
                hs = slice(h * HP, (h + 1) * HP)
                m = jnp.where(tril, jnp.exp(jnp.where(tril, cs[:, h:h + 1] - csT[h:h + 1, :], 0.0)), 0.0)
                sc = G * m
                dyh = dy[:, hs]
                dxd[:, hs] = _dot(sc, dyh, "tn")
                dsc = _dot(dyh, xd[:, hs], "nt")
                q = dsc * sc
                qcols = qcols + jnp.where(lane == h, jnp.sum(q, axis=1, keepdims=True), 0.0)
                qrows = qrows + jnp.where(sub == h, jnp.sum(q, axis=0, keepdims=True), 0.0)
                dgsum = dgsum + dsc * m
            dC_parts.append(dC + _dot(dgsum, Bg))
            dB_parts.append(dB + _dot(dgsum, Cg, "tn"))
            dh[gs, :] = dhp
        yoff = jnp.concatenate(yoff_parts, axis=1)
        dxw = jnp.concatenate(dxw_parts, axis=1)
        dxdv = dxd[...]
        per_head = _dot(jnp.concatenate([dy * yoff, dxw * xs, dxdv * xs, dy * xs], axis=0), et, exact="b")
        dcs = qcols - qrows.T + per_head[0:CH]
        dw = per_head[CH:2 * CH]
        gsc_ref[2:3, :] += jnp.sum(per_head[3 * CH:4 * CH], axis=0, keepdims=True)
        dxs = dxs + wE * dxw + dtE * dxdv
        ddt = ddt + dw * jnp.exp(csL - cs) + per_head[2 * CH:3 * CH]
        dcs = dcs - dw * wdec
        dcsL = dcsL + jnp.sum(dw * wdec, axis=0, keepdims=True)
        last = lax.broadcasted_iota(jnp.int32, (CH, 128), 0) == CH - 1
        dcs = dcs + jnp.where(last, dcsL, 0.0)
        da = _dot(triu, dcs, exact="a", passes=3)
        ddt = ddt + da * A
        gsc_ref[1:2, :] += jnp.sum(da * dt, axis=0, keepdims=True) * A
        valid = lax.broadcasted_iota(jnp.int32, (CH, 128), 1) < NH
        ddtr = jnp.where(valid, ddt * _sigmoid(dtr + dtb), 0.0)
        gsc_ref[0:1, :] += jnp.sum(ddtr, axis=0, keepdims=True)
        ddtr_ref[...] = ddtr.astype(ddtr_ref.dtype)
        dxbc = jnp.concatenate([dxs] + dB_parts + dC_parts, axis=1)
        dxc = dxbc * _dsilu(xbc_c)
        cext[pl.ds(0, CH), :] = dxc
        xp = xp_ref[...]
        acc = jnp.zeros((CH, XBC), F32)
        for k in range(KS):
            sh = cext[pl.ds(KS - 1 - k, CH), :]
            acc = acc + sh * cw_ref[k:k + 1, :]
            gcw_ref[k:k + 1, :] += jnp.sum(xp * sh, axis=0, keepdims=True)
        gcb_ref[...] += jnp.sum(dxc, axis=0, keepdims=True)
        dx_ref[...] = acc.astype(dx_ref.dtype)
        cext[pl.ds(CH, SSD_HALO), :] = cext[pl.ds(0, SSD_HALO), :]
        rd.finish()

    vec = pl.BlockSpec((8, 128), lambda c: (0, 0))
    vecd = pl.BlockSpec((1, D), lambda c: (0, 0))
    cwsp = pl.BlockSpec((KS, XBC), lambda c: (0, 0))
    cbsp = pl.BlockSpec((1, XBC), lambda c: (0, 0))
    row = lambda w, j=0: pl.BlockSpec((CH, w), lambda c: (rev(c), j))
    outs = pl.pallas_call(
        body, name=name, grid=(nc,),
        in_specs=[row(D), row(D), row(D, COL_Z // D), row(XBC), row(XBC, COL_XBC // XBC), row(128, COL_DT // 128),
                  pl.BlockSpec((1, NH * HP, NS), lambda c: (rev(c), 0, 0)), cwsp, vec, vecd] + [HBM_SPEC] * nco,
        out_specs=[row(D), row(XBC), row(128), cwsp, cbsp, vec, vecd] + [HBM_SPEC] * nco,
        out_shape=[SDS((S, D), _MXU), SDS((S, XBC), _MXU), SDS((S, 128), _MXU), SDS((KS, XBC), F32),
                   SDS((1, XBC), F32), SDS((8, 128), F32), SDS((1, D), F32)] + rd.out_shapes(),
        scratch_shapes=[pltpu.VMEM((NH * HP, NS), F32), pltpu.VMEM((CH, NH * HP), F32),
                        pltpu.VMEM((CH + SSD_HALO, XBC), F32)] + rd.scratch(),
        compiler_params=_cp("arbitrary"))(dmix, y, proj, xbc_c, proj, proj, hprev, cw, sc, norm_g, *rd.arrays())
    return outs[:7], rd.split(outs[7:])


CONV_RT = 32


def _fill_phases(ext, ph, rows):
    for s in range(1, 8):
        ph[s - 1, pl.ds(0, rows), :] = ext[pl.ds(s, rows), :]


def _window(ext, ph, off, r0, ls):
    s = off % 8
    src = ext if s == 0 else ph.at[s - 1]
    return src[pl.ds(pl.multiple_of(off - s + r0, 8), CONV_RT), ls]


def _cf_fwd(proj, w, b, lg, lb, riders=(), name="cf_fwd", tb_cap=256):
    S = proj.shape[0]
    tb = _tile(S, tb_cap, 8)
    nb = S // tb
    rd = _Riders(riders)
    nco = len(rd.arrays())

    def body(*refs):
        a_ref, g_ref, w_ref, b_ref, lg_ref, lb_ref = refs[:6]
        u1_ref, u_ref = refs[6 + nco:8 + nco]
        ext, ph = refs[8 + 2 * nco:10 + 2 * nco]
        rd.bind(refs[6:6 + nco], refs[8 + nco:8 + 2 * nco], refs[10 + 2 * nco:], nb).start()

        @pl.when(pl.program_id(0) == 0)
        def _():
            ext[pl.ds(0, CF_HALO), :] = jnp.zeros((CF_HALO, D), F32)

        ext[pl.ds(CF_HALO, tb), :] = a_ref[...] * _sigmoid(g_ref[...])
        _fill_phases(ext, ph, tb + CF_HALO - 8)

        def tile(i, carry):
            r0 = pl.multiple_of(i * CONV_RT, CONV_RT)
            for l in range(D // 128):
                ls = pl.ds(l * 128, 128)
                acc = jnp.broadcast_to(b_ref[:, ls], (CONV_RT, 128))
                for k in range(KC):
                    acc = acc + _window(ext, ph, CF_HALO - (KC - 1) + k, r0, ls) * w_ref[k:k + 1, ls]
                u1_ref[pl.ds(r0, CONV_RT), ls] = acc
            return carry

        lax.fori_loop(0, tb // CONV_RT, tile, 0)
        acc = u1_ref[...]
        mu = jnp.mean(acc, axis=-1, keepdims=True)
        xc = acc - mu
        r = lax.rsqrt(jnp.mean(xc * xc, axis=-1, keepdims=True) + EPS)
        u_ref[...] = _silu(xc * r * lg_ref[...] + lb_ref[...]).astype(u_ref.dtype)
        ext[pl.ds(0, CF_HALO), :] = ext[pl.ds(tb, CF_HALO), :]
        rd.finish()

    vec = pl.BlockSpec((1, D), lambda i: (0, 0))
    outs = pl.pallas_call(
        body, name=name, grid=(nb,),
        in_specs=[pl.BlockSpec((tb, D), lambda i: (i, COL_A // D)), pl.BlockSpec((tb, D), lambda i: (i, COL_G // D)),
                  pl.BlockSpec((KC, D), lambda i: (0, 0)), vec, vec, vec] + [HBM_SPEC] * nco,
        out_specs=[pl.BlockSpec((tb, D), lambda i: (i, 0)), pl.BlockSpec((tb, D), lambda i: (i, 0))] + [HBM_SPEC] * nco,
        out_shape=[SDS((S, D), F32), SDS((S, D), _MXU)] + rd.out_shapes(),
        scratch_shapes=[pltpu.VMEM((CF_HALO + tb, D), F32), pltpu.VMEM((7, tb + CF_HALO - 8, D), F32)] + rd.scratch(),
        compiler_params=_cp("arbitrary"))(proj, proj, w, b, lg, lb, *rd.arrays())
    return outs[:2], rd.split(outs[2:])


def _cf_bwd(dmix, u1, proj, w, lg, lb, riders=(), name="cf_bwd", tb_cap=256):
    S = proj.shape[0]
    tb = _tile(S, tb_cap, 8)
    nb = S // tb
    rd = _Riders(riders)
    nco = len(rd.arrays())
    rev = lambda i: nb - 1 - i

    def body(*refs):
        du_ref, u1_ref, a_ref, g_ref, w_ref, lg_ref, lb_ref = refs[:7]
        da_ref, dg_ref, dw_ref, db_ref, dlg_ref, dlb_ref = refs[7 + nco:13 + nco]
        ext, ph, u0s = refs[13 + 2 * nco:16 + 2 * nco]
        rd.bind(refs[7:7 + nco], refs[13 + nco:13 + 2 * nco], refs[16 + 2 * nco:], nb).start()

        @pl.when(pl.program_id(0) == 0)
        def _():
            ext[pl.ds(tb, CF_HALO), :] = jnp.zeros((CF_HALO, D), F32)
            dw_ref[...] = jnp.zeros_like(dw_ref)
            db_ref[...] = jnp.zeros_like(db_ref)
            dlg_ref[...] = jnp.zeros_like(dlg_ref)
            dlb_ref[...] = jnp.zeros_like(dlb_ref)

        u1 = u1_ref[...]
        mu = jnp.mean(u1, axis=-1, keepdims=True)
        xc = u1 - mu
        r = lax.rsqrt(jnp.mean(xc * xc, axis=-1, keepdims=True) + EPS)
        xh = xc * r
        lgv = lg_ref[...]
        du2 = du_ref[...] * _dsilu(xh * lgv + lb_ref[...])
        dlg_ref[...] += jnp.sum(du2 * xh, axis=0, keepdims=True)
        dlb_ref[...] += jnp.sum(du2, axis=0, keepdims=True)
        gd = du2 * lgv
        du1 = r * (gd - jnp.mean(gd, axis=-1, keepdims=True) - xh * jnp.mean(gd * xh, axis=-1, keepdims=True))
        db_ref[...] += jnp.sum(du1, axis=0, keepdims=True)
        ext[pl.ds(0, tb), :] = du1
        u0s[...] = a_ref[...] * _sigmoid(g_ref[...])
        _fill_phases(ext, ph, tb + CF_HALO - 8)

        for l in range(D // 128):
            ls = pl.ds(l * 128, 128)

            def tile(i, accs, ls=ls):
                r0 = pl.multiple_of(i * CONV_RT, CONV_RT)
                rows = pl.ds(r0, CONV_RT)
                u0t = u0s[rows, ls]
                acc = jnp.zeros((CONV_RT, 128), F32)
                out = []
                for k in range(KC):
                    win = _window(ext, ph, KC - 1 - k, r0, ls)
                    acc = acc + win * w_ref[k:k + 1, ls]
                    p = u0t * win
                    out.append(accs[k] + ((p[0:8] + p[8:16]) + (p[16:24] + p[24:32])))
                sg = _sigmoid(g_ref[rows, ls])
                da_ref[rows, ls] = (acc * sg).astype(da_ref.dtype)
                dg_ref[rows, ls] = (acc * a_ref[rows, ls] * sg * (1.0 - sg)).astype(dg_ref.dtype)
                return tuple(out)

            accs = lax.fori_loop(0, tb // CONV_RT, tile, tuple(jnp.zeros((8, 128), F32) for _ in range(KC)))
            for k in range(KC):
                dw_ref[k:k + 1, ls] += jnp.sum(accs[k], axis=0, keepdims=True)
        ext[pl.ds(tb, CF_HALO), :] = ext[pl.ds(0, CF_HALO), :]
        rd.finish()

    vec = pl.BlockSpec((1, D), lambda i: (0, 0))
    wsp = pl.BlockSpec((KC, D), lambda i: (0, 0))
    row = lambda j=0: pl.BlockSpec((tb, D), lambda i: (rev(i), j))
    outs = pl.pallas_call(
        body, name=name, grid=(nb,),
        in_specs=[row(1), row(), row(COL_A // D), row(COL_G // D), wsp, vec, vec] + [HBM_SPEC] * nco,
        out_specs=[row(), row(), wsp, vec, vec, vec] + [HBM_SPEC] * nco,
        out_shape=[SDS((S, D), _MXU), SDS((S, D), _MXU), SDS((KC, D), F32),
                   SDS((1, D), F32), SDS((1, D), F32), SDS((1, D), F32)] + rd.out_shapes(),
        scratch_shapes=[pltpu.VMEM((tb + CF_HALO, D), F32), pltpu.VMEM((7, tb + CF_HALO - 8, D), F32),
                        pltpu.VMEM((tb, D), F32)] + rd.scratch(),
        compiler_params=_cp("arbitrary"))(dmix, u1, proj, proj, w, lg, lb, *rd.arrays())
    return outs[:6], rd.split(outs[6:])


def _attn_fwd(q, kv, name="attn_fwd", tq_cap=512):
    S = q.shape[0]
    tq = _tile(S, tq_cap, 8)
    scale = XD ** -0.5

    def body(q_ref, kv_ref, o_ref):
        for h in range(XH):
            hs = slice(h * XD, (h + 1) * XD)
            s = _dot(q_ref[:, hs], kv_ref[:, hs], "nt") * scale
            s = s - jnp.max(s, axis=-1, keepdims=True)
            p = jnp.exp(s)
            p = p / jnp.sum(p, axis=-1, keepdims=True)
            o_ref[:, hs] = _dot(p, kv_ref[:, D + h * XD: D + (h + 1) * XD]).astype(o_ref.dtype)

    return pl.pallas_call(
        body, name=name, grid=(S // tq,),
        in_specs=[pl.BlockSpec((tq, D), lambda i: (i, 0)), pl.BlockSpec((MEM, 2 * D), lambda i: (0, 0))],
        out_specs=pl.BlockSpec((tq, D), lambda i: (i, 0)), out_shape=SDS((S, D), _MXU),
        compiler_params=_cp("parallel"))(q, kv)


def _attn_bwd(do, q, kv, riders=(), name="attn_bwd", tq_cap=512):
    S = q.shape[0]
    tq = _tile(S, tq_cap, 8)
    scale = XD ** -0.5
    rd = _Riders(riders)
    nco = len(rd.arrays())

    def body(*refs):
        do_ref, q_ref, kv_ref = refs[:3]
        dq_ref, dkv_ref = refs[3 + nco:5 + nco]
        rd.bind(refs[3:3 + nco], refs[5 + nco:5 + 2 * nco], refs[5 + 2 * nco:], S // tq).start()

        @pl.when(pl.program_id(0) == 0)
        def _():
            dkv_ref[...] = jnp.zeros_like(dkv_ref)

        for h in range(XH):
            hs = slice(h * XD, (h + 1) * XD)
            vs = slice(D + h * XD, D + (h + 1) * XD)
            qh = q_ref[:, hs]
            kh = kv_ref[:, hs]
            s = _dot(qh, kh, "nt") * scale
            s = s - jnp.max(s, axis=-1, keepdims=True)
            p = jnp.exp(s)
            p = p / jnp.sum(p, axis=-1, keepdims=True)
            doh = do_ref[:, hs]
            dp = _dot(doh, kv_ref[:, vs], "nt")
            ds = p * (dp - jnp.sum(dp * p, axis=-1, keepdims=True)) * scale
            dq_ref[:, hs] = _dot(ds, kh).astype(dq_ref.dtype)
            dkv_ref[:, hs] += _dot(ds, qh, "tn")
            dkv_ref[:, vs] += _dot(p, doh, "tn")
        rd.finish()

    outs = pl.pallas_call(
        body, name=name, grid=(S // tq,),
        in_specs=[pl.BlockSpec((tq, D), lambda i: (i, 0)), pl.BlockSpec((tq, D), lambda i: (i, 0)),
                  pl.BlockSpec((MEM, 2 * D), lambda i: (0, 0))] + [HBM_SPEC] * nco,
        out_specs=[pl.BlockSpec((tq, D), lambda i: (i, 0)), pl.BlockSpec((MEM, 2 * D), lambda i: (0, 0))]
        + [HBM_SPEC] * nco,
        out_shape=[SDS((S, D), _MXU), SDS((MEM, 2 * D), F32)] + rd.out_shapes(), scratch_shapes=rd.scratch(),
        compiler_params=_cp("arbitrary"))(do, q, kv, *rd.arrays())
    return outs[:2], rd.split(outs[2:])


def _ffn_in(hf, wg_t, wu_t, name="ffn_in", tm_cap=512, tn_cap=1408):
    S, K = hf.shape
    N = wg_t.shape[0]
    tm, tn = _tile(S, tm_cap, 8), _tile(N, tn_cap)

    def body(a_ref, g_ref, u_ref, act_ref, gt_ref, up_ref):
        a = a_ref[...]
        gt = _dot(a, g_ref[...], "nt")
        up = _dot(a, u_ref[...], "nt")
        act_ref[...] = (_silu(gt) * up).astype(act_ref.dtype)
        gt_ref[...] = gt.astype(gt_ref.dtype)
        up_ref[...] = up.astype(up_ref.dtype)

    wsp = pl.BlockSpec((tn, K), lambda j, i: (j, 0))
    osp = pl.BlockSpec((tm, tn), lambda j, i: (i, j))
    return pl.pallas_call(
        body, name=name, grid=(N // tn, S // tm), in_specs=[pl.BlockSpec((tm, K), lambda j, i: (i, 0)), wsp, wsp],
        out_specs=[osp, osp, osp], out_shape=[SDS((S, N), _MXU)] * 3,
        compiler_params=_cp("parallel", "parallel"))(hf, wg_t, wu_t)


def _ffn_out_bwd(dx, w_down, gt, up, name="ffn_out_dx", tm_cap=512, tk_cap=1408):
    S, N = dx.shape
    K = w_down.shape[0]
    tm, tk = _tile(S, tm_cap, 8), _tile(K, tk_cap)

    def body(a_ref, b_ref, g_ref, u_ref, dg_ref, du_ref):
        d = _dot(a_ref[...], b_ref[...], "nt")
        gt = g_ref[...].astype(F32)
        s = _sigmoid(gt)
        dg_ref[...] = (d * u_ref[...].astype(F32) * (s * (1.0 + gt * (1.0 - s)))).astype(dg_ref.dtype)
        du_ref[...] = (d * gt * s).astype(du_ref.dtype)

    osp = pl.BlockSpec((tm, tk), lambda j, i: (i, j))
    return pl.pallas_call(
        body, name=name, grid=(K // tk, S // tm),
        in_specs=[pl.BlockSpec((tm, N), lambda j, i: (i, 0)), pl.BlockSpec((tk, N), lambda j, i: (j, 0)), osp, osp],
        out_specs=[osp, osp], out_shape=[SDS((S, K), _MXU)] * 2,
        compiler_params=_cp("parallel", "parallel"))(dx, w_down, gt, up)


AG_RIDE = (("w_down", "w_o"), ("w_out", "w_q", "w_kv"), ("w_gate", "w_up"))


def _local_step(x, mem, tgt, W, P, core=None, late=None):
    pair, got = {}, {}
    ride = [[late[n] for n in grp] if late is not None else [] for grp in AG_RIDE]

    def halves(group):
        if core is None:
            return []
        gs = [_shard_grad(n, GW) for n in group]
        return [g.reshape(4, 2, g.shape[1] // 2, g.shape[2]) for g in gs]

    def pair_sums(group, hs, theirs):
        ps = [_pair_sum(h_, t, core, "rs_pair_sum_" + n) for h_, t, n in zip(hs, theirs, group)]
        pair.update(zip(group, ps))
        return ps

    h = _rms_fwd(x, P["g_mix"], "rms_mix")
    proj, (bufs0,) = _mm_nn(h, W["main"], "in_proj", tm_cap=256, tn_cap=MAINW, riders=[_Rider("gather", ride[0])])
    (xbc_c, y, yn, hprev), (bufs1,) = _ssd_fwd(proj, P["conv4_w"], P["conv4_b"], P["sc"], P["ssd_norm_g"],
                                                riders=[_Rider("gather", ride[1])])
    (u1, u), (bufs2,) = _cf_fwd(proj, P["cf_w"], P["cf_b"], P["ln_g"], P["ln_b"], riders=[_Rider("gather", ride[2])])
    if late is not None:
        names = AG_RIDE[0] + AG_RIDE[1] + AG_RIDE[2]
        full = _gather_finish_list(ride[0] + ride[1] + ride[2], bufs0 + bufs1 + bufs2)
        W = dict(W, **_pack_late(dict(zip(names, full))))
    mix = jnp.concatenate([yn, u], axis=1)
    x1, hq = _mm_nn(mix, W["out"], "out_proj", add=x, tm_cap=512, tn_cap=D, norm=P["g_xattn"])
    q = _mm_nn(hq, W["q"], "q_proj")
    mn = _rms_fwd(mem, P["g_mem"], "rms_mem")
    kv = _mm_nn(mn, W["kv"], "kv_proj")
    o = _attn_fwd(q, kv)
    x2, hf = _mm_nn(o, W["o"], "o_proj", add=x1, tm_cap=512, tn_cap=D, norm=P["g_ffn"])
    act, gt, up = _ffn_in(hf, W["gate_t"], W["up_t"])
    loss, dx3, dx3b, g_final = _mm_nn(act, W["down"], "ffn_out", add=x2, tm_cap=512, tn_cap=D,
                                      loss=(P["g_final"], tgt))
    GW, GP = {}, {"g_final": g_final}
    GW["down"] = _mm_tn(act, dx3b, "ffn_out_dw", tk_cap=1408, tn_cap=1024)
    dgt, dup = _ffn_out_bwd(dx3b, W["down"], gt, up)
    dhf = _mm_nn(dgt, W["gate_t"], "ffn_gate_dx", tm_cap=512)
    dx2, dx2b, GP["g_ffn"] = _mm_nn(dup, W["up_t"], "ffn_up_dx", add=dhf, tm_cap=512, tn_cap=D,
                                    rms=(x2, P["g_ffn"], dx3, True))
    GW["gate_t"] = _mm_tn(dgt, hf, "ffn_gate_dw", tk_cap=1408, tn_cap=1024)
    GW["up_t"] = _mm_tn(dup, hf, "ffn_up_dw", tk_cap=1408, tn_cap=1024)
    ffn_halves = halves(RS_GROUPS[0])
    do = _mm_nt(dx2b, W["o"], "o_proj_dx")
    GW["o"] = _mm_tn(o, dx2b, "o_proj_dw")
    (dq, dkv), (ffn_theirs,) = _attn_bwd(do, q, kv, riders=[_Rider("pair", ffn_halves)])
    ffn_pieces = pair_sums(RS_GROUPS[0], ffn_halves, ffn_theirs)
    dx1, dx1b, GP["g_xattn"] = _mm_nt(dq, W["q"], "q_proj_dx", tk_cap=D, rms=(x1, P["g_xattn"], dx2, True))
    GW["q"] = _mm_tn(hq, dq, "q_proj_dw")
    dkvb = dkv.astype(_MXU)
    GW["kv"] = _mm_tn(mn, dkvb, "kv_proj_dw", tm_cap=256)
    dmn = _mm_nt(dkvb, W["kv"], "kv_proj_dx")
    GP["g_mem"] = _rms_bwd(mem, P["g_mem"], dmn, None, "rms_mem_bwd")
    dmix = _mm_nt(dx1b, W["out"], "out_proj_dx")
    GW["out"] = _mm_tn(mix, dx1b, "out_proj_dw", tn_cap=1024)
    attn_halves = halves(RS_GROUPS[1])
    (da, dg, GP["cf_w"], GP["cf_b"], GP["ln_g"], GP["ln_b"]), (came, attn_theirs) = _cf_bwd(
        dmix, u1, proj, P["cf_w"], P["ln_g"], P["ln_b"],
        riders=[_Rider("exchange", ffn_pieces), _Rider("pair", attn_halves)])
    got.update(zip(RS_GROUPS[0], came))
    attn_pieces = pair_sums(RS_GROUPS[1], attn_halves, attn_theirs)
    (dz, dxbc, ddtr, GP["conv4_w"], GP["conv4_b"], GP["sc"], GP["ssd_norm_g"]), (came,) = _ssd_bwd(
        dmix, y, proj, xbc_c, hprev, P["conv4_w"], P["sc"], P["ssd_norm_g"],
        riders=[_Rider("exchange", attn_pieces)])
    got.update(zip(RS_GROUPS[1], came))
    dproj = [dz, da, dg, dxbc, ddtr]
    GW["main"] = _mm_tn(h, dproj, "in_proj_dw", tm_cap=512, tk_cap=512, tn_cap=MAINW)
    in_halves = halves(RS_GROUPS[2])
    in_pieces = pair_sums(RS_GROUPS[2], in_halves, _pair_split_list(in_halves, "rs_pair_send_w_in")) if in_halves else []
    (grad_x, GP["g_mix"]), (came,) = _mm_nt(dproj, W["main"], "in_proj_dx", tm_cap=256, tk_cap=D,
                                            riders=[_Rider("exchange", in_pieces)], rms=(x, P["g_mix"], dx1, False))
    got.update(zip(RS_GROUPS[2], came))
    if core is None:
        return loss, grad_x, GW, GP
    return loss, grad_x, GW, GP, pair, got


Z_END, XBC_END, DT_END = NH * HP, NH * HP + XBC, NH * HP + XBC + NH


def _pad_to(a, rows=None, cols=None):
    r = 0 if rows is None else rows - a.shape[0]
    c = 0 if cols is None else cols - a.shape[1]
    return jnp.pad(a, ((0, r), (0, c)))


IN_W = DT_END + 2 * D
W_IN_SEGS = [(0, Z_END, "main", COL_Z), (Z_END, XBC_END, "main", COL_XBC), (XBC_END, DT_END, "main", COL_DT),
             (DT_END, DT_END + D, "main", COL_A), (DT_END + D, IN_W, "main", COL_G)]
BIG = [("w_in", True), ("w_out", False), ("w_q", False), ("w_kv", True), ("w_o", False), ("w_gate", False),
       ("w_up", False), ("w_down", False)]
TRANSPOSED = ("w_gate", "w_up")


def _ref_cols(pieces, a, b):
    cw = IN_W // 4
    out = []
    for j in range(4):
        lo, hi = max(a, j * cw), min(b, (j + 1) * cw)
        if lo < hi:
            out.append(pieces[j][:, lo - j * cw:hi - j * cw])
    return out


def _cat_cols(pieces):
    return jnp.concatenate([pieces[j] for j in range(4)], axis=1)


def _pack_in(w_in):
    dt = _ref_cols(w_in, XBC_END, DT_END)
    pad = jnp.zeros((dt[0].shape[0], MAINW - COL_DT - NH), dt[0].dtype)
    main = jnp.concatenate(_ref_cols(w_in, 0, Z_END) + _ref_cols(w_in, DT_END, IN_W) + _ref_cols(w_in, Z_END, XBC_END)
                           + dt + [pad], axis=1)
    return {"main": main}


def _pack_late(pc):
    rows = lambda n: pc[n].reshape(-1, pc[n].shape[-1])
    return {"out": rows("w_out"), "q": rows("w_q"), "kv": _cat_cols(pc["w_kv"]), "o": rows("w_o"),
            "gate_t": rows("w_gate"), "up_t": rows("w_up"), "down": rows("w_down")}


GW_KEY = {"w_gate": "gate_t", "w_up": "up_t", "w_kv": "kv", "w_out": "out", "w_q": "q", "w_o": "o", "w_down": "down"}
RS_GROUPS = (("w_down", "w_gate", "w_up"), ("w_out", "w_q", "w_kv", "w_o"), ("w_in",))


def _shard_grad(name, GW):
    if name == "w_in":
        cw = IN_W // 4
        pieces = []
        for j in range(4):
            parts = []
            for a, b, src, col in W_IN_SEGS:
                lo, hi = max(a, j * cw), min(b, (j + 1) * cw)
                if lo < hi:
                    parts.append(GW[src][:, col + lo - a:col + hi - a])
            pieces.append(jnp.concatenate(parts, axis=1))
        return jnp.stack(pieces)
    g = GW[GW_KEY[name]]
    if dict(BIG)[name]:
        cw = g.shape[1] // 4
        return jnp.stack([g[:, j * cw:(j + 1) * cw] for j in range(4)])
    return g.reshape(4, g.shape[0] // 4, g.shape[1])


def _stack_sc(dt_bias, a_log, d):
    return _pad_to(jnp.concatenate([dt_bias, a_log, d], axis=0), rows=8, cols=128)


COMM_PARAMS = pltpu.CompilerParams(vmem_limit_bytes=VMEM_LIMIT)


def _dma_sems(*counts):
    return [pltpu.SemaphoreType.DMA((n,)) for n in counts]


def _allgather_list(arrs, name):
    n = len(arrs)
    halved = [a.shape[0] % 16 == 0 for a in arrs]
    oshape = [(4, 2, a.shape[0] // 2, a.shape[1]) if h else (4, 1) + a.shape for a, h in zip(arrs, halved)]

    def body(*refs):
        srcs, outs = refs[:n], refs[n:2 * n]
        ici_send, ici_recv, own_send, own_recv, fwd_send, fwd_recv = refs[2 * n:]
        x, y, c = lax.axis_index("x"), lax.axis_index("y"), lax.axis_index("c")
        me = 2 * x + y
        sib = (x, y, 1 - c)
        peers = _chip_peers(x, y)

        def half(i, h):
            r = arrs[i].shape[0] // 2
            if not halved[i]:
                return srcs[i]
            return srcs[i].at[pl.ds(h * r if isinstance(h, int) else pl.multiple_of(h * r, 8), r)]

        ici, own, fwd = [], [], []
        for i in range(n):
            mine_h = c if halved[i] else 0
            for k, (px, py) in enumerate(peers):
                s = 3 * i + k
                ici.append(_remote(half(i, c), outs[i].at[me, mine_h], ici_send.at[s], ici_recv.at[s], (px, py, c)))
            for h in range(2 if halved[i] else 1):
                s = 2 * i + h
                own.append(_remote(half(i, h), outs[i].at[me, h], own_send.at[s], own_recv.at[s], sib))
        for cp in ici + own:
            cp.start()
        for i in range(n):
            if not halved[i]:
                continue
            for k, (px, py) in enumerate(peers):
                s = 3 * i + k
                got = outs[i].at[2 * px + py, c]
                _remote(half(i, c), got, ici_send.at[s], ici_recv.at[s], (px, py, c)).wait_recv()
                f = _remote(got, got, fwd_send.at[s], fwd_recv.at[s], sib)
                f.start()
                fwd.append(f)
        for i in range(n):
            for k, (px, py) in enumerate(peers):
                s = 3 * i + k
                if halved[i]:
                    _remote(half(i, c), outs[i].at[2 * px + py, 1 - c], fwd_send.at[s], fwd_recv.at[s], sib).wait_recv()
                else:
                    _remote(srcs[i], outs[i].at[2 * px + py, 0], ici_send.at[s], ici_recv.at[s], (px, py, c)).wait_recv()
            for h in range(2 if halved[i] else 1):
                s = 2 * i + h
                _remote(half(i, h), outs[i].at[me, h], own_send.at[s], own_recv.at[s], sib).wait_recv()
        for cp in ici + own + fwd:
            cp.wait_send()

    outs = pl.pallas_call(
        body, name=name, in_specs=[HBM_SPEC] * n, out_specs=[HBM_SPEC] * n,
        out_shape=[SDS(s, a.dtype) for s, a in zip(oshape, arrs)],
        scratch_shapes=_dma_sems(3 * n, 3 * n, 2 * n, 2 * n, 3 * n, 3 * n), compiler_params=COMM_PARAMS)(*arrs)
    return [o.reshape((4,) + a.shape) for o, a in zip(outs, arrs)]


def _pair_split_list(gs, name):
    n = len(gs)

    def body(*refs):
        sends, recvs = _pair_copies(refs[:n], refs[n:2 * n], *refs[2 * n:])
        for cp in sends:
            cp.start()
        for cp in recvs:
            cp.wait_recv()
        for cp in sends:
            cp.wait_send()

    return pl.pallas_call(
        body, name=name, in_specs=[HBM_SPEC] * n, out_specs=[HBM_SPEC] * n,
        out_shape=[SDS((4,) + g.shape[2:], g.dtype) for g in gs],
        scratch_shapes=_dma_sems(4 * n, 4 * n), compiler_params=COMM_PARAMS)(*gs)


def _gather_finish_list(shards, bufs, name="allgather_finish"):
    n = len(shards)

    def body(*refs):
        srcs, outs = refs[:n], refs[2 * n:3 * n]
        own_send, own_recv, fwd_send, fwd_recv = refs[3 * n:]
        x, y, c = lax.axis_index("x"), lax.axis_index("y"), lax.axis_index("c")
        me = 2 * x + y
        sib = (x, y, 1 - c)
        sends, recvs = [], []
        for i in range(n):
            for h in range(2):
                own = _remote(_rows_half(srcs[i], shards[i].shape[0], h), outs[i].at[me, h],
                              own_send.at[2 * i + h], own_recv.at[2 * i + h], sib)
                sends.append(own)
                recvs.append(own)
            for k, (px, py) in enumerate(_chip_peers(x, y)):
                got, s = outs[i].at[2 * px + py, c], 3 * i + k
                sends.append(_remote(got, got, fwd_send.at[s], fwd_recv.at[s], sib))
                recvs.append(_remote(got, outs[i].at[2 * px + py, 1 - c], fwd_send.at[s], fwd_recv.at[s], sib))
        for cp in sends:
            cp.start()
        for cp in recvs:
            cp.wait_recv()
        for cp in sends:
            cp.wait_send()

    outs = pl.pallas_call(
        body, name=name, in_specs=[HBM_SPEC] * (2 * n), out_specs=[HBM_SPEC] * n,
        out_shape=[SDS(b.shape, b.dtype) for b in bufs], input_output_aliases={n + i: i for i in range(n)},
        scratch_shapes=_dma_sems(2 * n, 2 * n, 3 * n, 3 * n), compiler_params=COMM_PARAMS)(*shards, *bufs)
    return [o.reshape((4,) + a.shape) for o, a in zip(outs, shards)]


JOIN_SPLIT = 4


def _pair_join_list(bufs, name="rs_pair_join"):
    n = len(bufs)

    def body(*refs):
        outs = refs[n:2 * n]
        send_sems, recv_sems = refs[2 * n:]
        x, y, c = lax.axis_index("x"), lax.axis_index("y"), lax.axis_index("c")
        sib = (x, y, 1 - c)
        sends, recvs = [], []
        for i in range(n):
            rc = bufs[i].shape[1] // JOIN_SPLIT
            for q in range(JOIN_SPLIT):
                k = JOIN_SPLIT * i + q
                rows = pl.ds(q * rc, rc)
                sends.append(_remote(outs[i].at[c, rows], outs[i].at[c, rows], send_sems.at[k], recv_sems.at[k], sib))
                recvs.append(_remote(outs[i].at[c, rows], outs[i].at[1 - c, rows], send_sems.at[k], recv_sems.at[k], sib))
        for cp in sends:
            cp.start()
        for cp in recvs:
            cp.wait_recv()
        for cp in sends:
            cp.wait_send()

    return pl.pallas_call(
        body, name=name, in_specs=[HBM_SPEC] * n, out_specs=[HBM_SPEC] * n,
        out_shape=[SDS(b.shape, b.dtype) for b in bufs], input_output_aliases={i: i for i in range(n)},
        scratch_shapes=_dma_sems(JOIN_SPLIT * n, JOIN_SPLIT * n), compiler_params=COMM_PARAMS)(*bufs)


def _pair_sum(g, theirs, core, name):
    _, _, r, c = g.shape

    def body(core_ref, g_ref, t_ref, o_ref):
        o_ref[...] = (g_ref[...] + t_ref[...]).astype(o_ref.dtype)

    spec = pltpu.PrefetchScalarGridSpec(
        num_scalar_prefetch=1, grid=(4,),
        in_specs=[pl.BlockSpec((None, None, r, c), lambda j, core_ref: (j, core_ref[0], 0, 0)),
                  pl.BlockSpec((None, r, c), lambda j, core_ref: (j, 0, 0))],
        out_specs=pl.BlockSpec((None, r, c), lambda j, core_ref: (j, 0, 0)))
    return pl.pallas_call(body, name=name, grid_spec=spec, out_shape=SDS((4, r, c), BF16),
                          compiler_params=_cp("parallel"))(core, g, theirs)


def _chip_sum(own, got, where, name):
    _, r, c = own.shape
    tr = r // 2

    def body(w_ref, a_ref, b1_ref, b2_ref, b3_ref, o_ref):
        o_ref[...] = ((a_ref[...].astype(F32) + b1_ref[...].astype(F32)) + b2_ref[...].astype(F32)) + b3_ref[...].astype(F32)

    piece = lambda k: pl.BlockSpec((None, tr, c), lambda i, w_ref: ((w_ref[0] + k) % 4, i, 0))
    spec = pltpu.PrefetchScalarGridSpec(
        num_scalar_prefetch=1, grid=(r // tr,), in_specs=[piece(0), piece(1), piece(2), piece(3)],
        out_specs=pl.BlockSpec((None, tr, c), lambda i, w_ref: (w_ref[1], i, 0)))
    return pl.pallas_call(body, name=name, grid_spec=spec, out_shape=SDS((2, r, c), F32),
                          compiler_params=_cp("parallel"))(where, own, got, got, got)


ADAM_BLOCK = 2 ** 19


def _adam_math(w, g, m, v):
    bc1 = 1.0 - ADAM_B1 ** ADAM_STEP
    bc2 = 1.0 - ADAM_B2 ** ADAM_STEP
    mn = ADAM_B1 * m + (1.0 - ADAM_B1) * g
    vn = ADAM_B2 * v + (1.0 - ADAM_B2) * (g * g)
    return -ADAM_LR * ((mn / bc1) / (jnp.sqrt(vn / bc2) + ADAM_EPS) + ADAM_WD * w), mn, vn


PACK_COLS = XBC
PACK = {"g_mix": (0, 1, D), "g_xattn": (1, 1, D), "g_mem": (2, 1, D), "g_ffn": (3, 1, D), "g_final": (4, 1, D),
        "ssd_norm_g": (5, 1, D), "cf_b": (6, 1, D), "ln_g": (7, 1, D), "ln_b": (8, 1, D), "conv4_b": (9, 1, XBC),
        "conv4_w": (10, KS, XBC), "sc": (16, 8, 128), "cf_w": (24, KC, D), "loss": (55, 1, 128)}
PACK_ROWS = 56
SMALL_ADAM = ["g_mix", "g_xattn", "g_mem", "g_ffn", "g_final", "ssd_norm_g", "cf_b", "ln_g", "ln_b", "conv4_b", "sc"]


def _small_allreduce_adamw(grads, wts, mom, var, name="allreduce_small"):
    gk = list(PACK)
    ng, na = len(gk), len(SMALL_ADAM)

    def body(*refs):
        g_in = refs[:ng]
        w_in, m_in, v_in = (refs[ng + i * na: ng + (i + 1) * na] for i in range(3))
        o = refs[ng + 3 * na:]
        g_out = o[:ng]
        d_out, m_out, v_out = (o[ng + i * na: ng + (i + 1) * na] for i in range(3))
        pack, pbuf, psum, cbuf, acc, send_sems, recv_sems = o[ng + 3 * na:]
        x, y, c = lax.axis_index("x"), lax.axis_index("y"), lax.axis_index("c")
        me = 2 * x + y
        pack[...] = jnp.zeros_like(pack)
        for i, k in enumerate(gk):
            r0, nr, nc = PACK[k]
            pack[r0:r0 + nr, 0:nc] = g_in[i][...]
        pair = _remote(pack, pbuf.at[c], send_sems.at[0], recv_sems.at[0], (x, y, 1 - c))
        pair.start()
        pbuf[c] = pack[...]
        _remote(pack, pbuf.at[1 - c], send_sems.at[0], recv_sems.at[0], (x, y, 1 - c)).wait_recv()
        pair.wait_send()
        psum[...] = pbuf[0] + pbuf[1]
        peers = _chip_peers(x, y)
        sends = [_remote(psum, cbuf.at[me], send_sems.at[1 + k], recv_sems.at[1 + k], (px, py, c))
                 for k, (px, py) in enumerate(peers)]
        for cp in sends:
            cp.start()
        cbuf[me] = psum[...]
        for k, (px, py) in enumerate(peers):
            _remote(psum, cbuf.at[2 * px + py], send_sems.at[1 + k], recv_sems.at[1 + k], (px, py, c)).wait_recv()
        for cp in sends:
            cp.wait_send()
        acc[...] = (cbuf[0] + cbuf[1]) + (cbuf[2] + cbuf[3])
        for i, k in enumerate(gk):
            r0, nr, nc = PACK[k]
            g_out[i][...] = acc[r0:r0 + nr, 0:nc]
        for i, k in enumerate(SMALL_ADAM):
            r0, nr, nc = PACK[k]
            d_out[i][...], m_out[i][...], v_out[i][...] = _adam_math(
                w_in[i][...], acc[r0:r0 + nr, 0:nc], m_in[i][...], v_in[i][...])

    args = [grads[k] for k in gk] + [d[k] for d in (wts, mom, var) for k in SMALL_ADAM]
    shp = lambda k: SDS((PACK[k][1], PACK[k][2]), F32)
    vm = pl.BlockSpec(memory_space=pltpu.VMEM)
    outs = pl.pallas_call(
        body, name=name, in_specs=[vm] * len(args), out_specs=[vm] * (ng + 3 * na),
        out_shape=[shp(k) for k in gk] + [shp(k) for _ in range(3) for k in SMALL_ADAM],
        scratch_shapes=[pltpu.VMEM((PACK_ROWS, PACK_COLS), F32), pltpu.VMEM((2, PACK_ROWS, PACK_COLS), F32),
                        pltpu.VMEM((PACK_ROWS, PACK_COLS), F32), pltpu.VMEM((4, PACK_ROWS, PACK_COLS), F32),
                        pltpu.VMEM((PACK_ROWS, PACK_COLS), F32)] + _dma_sems(4, 4),
        compiler_params=COMM_PARAMS)(*args)
    red = dict(zip(gk, outs[:ng]))
    parts = [dict(zip(SMALL_ADAM, outs[ng + i * na: ng + (i + 1) * na])) for i in range(3)]
    return red, parts[0], parts[1], parts[2]


def _adamw_cols(w, gfull, m, v, chip, name):
    _, R, C = w.shape

    def body(w_idx, w_ref, g_ref, m_ref, v_ref, go_ref, d_ref, mo_ref, vo_ref):
        go_ref[...] = g_ref[...]
        d_ref[...], mo_ref[...], vo_ref[...] = _adam_math(w_ref[...], g_ref[...], m_ref[...], v_ref[...])

    blk = pl.BlockSpec((None, R, C), lambda i, w_idx: (0, 0, 0))
    spec = pltpu.PrefetchScalarGridSpec(
        num_scalar_prefetch=1, grid=(1,),
        in_specs=[blk, pl.BlockSpec((R, C), lambda i, w_idx: (0, w_idx[0])), blk, blk], out_specs=[blk] * 4)
    return pl.pallas_call(body, name=name, grid_spec=spec, out_shape=[SDS((1, R, C), F32)] * 4,
                          compiler_params=_cp("arbitrary"))(chip, w, gfull, m, v)


def _adamw(w, g, m, v, name):
    _, R, C = w.shape
    half = R // 2
    tr = _tile(half, max(8, (ADAM_BLOCK // C) // 8 * 8), 8)
    nh = half // tr

    def body(w_ref, g_ref, m_ref, v_ref, go_ref, d_ref, mo_ref, vo_ref):
        go_ref[...] = g_ref[...]
        d_ref[...], mo_ref[...], vo_ref[...] = _adam_math(w_ref[...], g_ref[...], m_ref[...], v_ref[...])

    blk = pl.BlockSpec((None, tr, C), lambda i: (0, i, 0))
    gblk = pl.BlockSpec((None, tr, C), lambda i: (i // nh, i % nh, 0))
    return pl.pallas_call(body, name=name, grid=(R // tr,), in_specs=[blk, gblk, blk, blk], out_specs=[blk] * 4,
                          out_shape=[SDS((1, R, C), F32)] * 4, compiler_params=_cp("parallel"))(w, g, m, v)


WEIGHT_NAMES = ["norm_mix_g", "w_in", "ssd_conv_w", "ssd_conv_b", "ssd_dt_bias", "ssd_A_log", "ssd_D", "ssd_norm_g",
                "cf_conv_w", "cf_conv_b", "cf_ln_g", "cf_ln_b", "w_out", "norm_xattn_g", "norm_mem_g", "w_q", "w_kv",
                "w_o", "norm_ffn_g", "w_gate", "w_up", "w_down", "norm_final_g"]
VEC_REF = [("norm_mix_g", "g_mix"), ("norm_xattn_g", "g_xattn"), ("norm_mem_g", "g_mem"), ("norm_ffn_g", "g_ffn"),
           ("norm_final_g", "g_final"), ("ssd_norm_g", "ssd_norm_g"), ("cf_conv_b", "cf_b"), ("cf_ln_g", "ln_g"),
           ("cf_ln_b", "ln_b"), ("ssd_conv_b", "conv4_b")]
SC_REF = ["ssd_dt_bias", "ssd_A_log", "ssd_D"]


def _small_side(get):
    d = {k: get(ref_name).reshape(1, -1) for ref_name, k in VEC_REF}
    d["sc"] = _stack_sc(*[get(n) for n in SC_REF])
    return d


def kernel(x, mem, norm_mix_g, w_in, ssd_conv_w, ssd_conv_b, ssd_dt_bias, ssd_A_log, ssd_D, ssd_norm_g, cf_conv_w, cf_conv_b, cf_ln_g, cf_ln_b, w_out, norm_xattn_g, norm_mem_g, w_q, w_kv, w_o, norm_ffn_g, w_gate, w_up, w_down, norm_final_g, loss_target, m_norm_mix_g, m_w_in, m_ssd_conv_w, m_ssd_conv_b, m_ssd_dt_bias, m_ssd_A_log, m_ssd_D, m_ssd_norm_g, m_cf_conv_w, m_cf_conv_b, m_cf_ln_g, m_cf_ln_b, m_w_out, m_norm_xattn_g, m_norm_mem_g, m_w_q, m_w_kv, m_w_o, m_norm_ffn_g, m_w_gate, m_w_up, m_w_down, m_norm_final_g, v_norm_mix_g, v_w_in, v_ssd_conv_w, v_ssd_conv_b, v_ssd_dt_bias, v_ssd_A_log, v_ssd_D, v_ssd_norm_g, v_cf_conv_w, v_cf_conv_b, v_cf_ln_g, v_cf_ln_b, v_w_out, v_norm_xattn_g, v_norm_mem_g, v_w_q, v_w_kv, v_w_o, v_norm_ffn_g, v_w_gate, v_w_up, v_w_down, v_norm_final_g):
    env = dict(locals())
    view = lambda n, a: a.transpose(0, 2, 1) if n in TRANSPOSED else a
    wts = {n: view(n, env[n]) for n in WEIGHT_NAMES}
    mom = {n: view(n, env["m_" + n]) for n in WEIGHT_NAMES}
    var = {n: view(n, env["v_" + n]) for n in WEIGHT_NAMES}
    chip = (2 * lax.axis_index("x") + lax.axis_index("y")).astype(jnp.int32).reshape(1)
    core = lax.axis_index("c").astype(jnp.int32).reshape(1)
    where = jnp.concatenate([chip, core])
    big = [n for n, _ in BIG]

    w_in_g, conv4_g, cf_g = _allgather_list([w_in[0].astype(BF16), ssd_conv_w[0], cf_conv_w[0]], "allgather_first")
    W = _pack_in(w_in_g)
    P = _small_side(lambda n: wts[n])
    P["conv4_w"], P["cf_w"] = _cat_cols(conv4_g), _cat_cols(cf_g)
    late = {n: wts[n][0].astype(BF16) for grp in AG_RIDE for n in grp}

    loss, grad_x, GW, GP, pair, got = _local_step(x[0], mem[0], loss_target[0], W, P, core, late)
    joined = _pair_join_list([_chip_sum(pair[n], got[n], where, "rs_chip_sum_" + n) for n in big])
    gshard = dict(zip(big, joined))

    small = dict(GP)
    small["loss"] = loss
    red, sd, sm, sv = _small_allreduce_adamw(small, {k: P[k] for k in SMALL_ADAM}, _small_side(lambda n: mom[n]),
                                             _small_side(lambda n: var[n]))
    grads, delta, new_m, new_v = {}, {}, {}, {}
    for ref_name, k in VEC_REF:
        shp = wts[ref_name].shape
        for dst, src in ((grads, red), (delta, sd), (new_m, sm), (new_v, sv)):
            dst[ref_name] = src[k].reshape(shp)
    for row, ref_name in enumerate(SC_REF):
        for dst, src in ((grads, red), (delta, sd), (new_m, sm), (new_v, sv)):
            dst[ref_name] = src["sc"][row:row + 1, :NH]

    for n, k in (("ssd_conv_w", "conv4_w"), ("cf_conv_w", "cf_w")):
        grads[n], delta[n], new_m[n], new_v[n] = _adamw_cols(wts[n], red[k], mom[n], var[n], chip, "adamw_" + n)
    for n in big:
        outs = _adamw(wts[n], gshard[n], mom[n], var[n], "adamw_" + n)
        grads[n], delta[n], new_m[n], new_v[n] = [view(n, o) for o in outs]

    return (red["loss"][0, 0], grad_x[None], *[grads[n] for n in WEIGHT_NAMES], *[delta[n] for n in WEIGHT_NAMES],
            *[new_m[n] for n in WEIGHT_NAMES], *[new_v[n] for n in WEIGHT_NAMES])
```

```python
import functools
import math

import jax
import jax.numpy as jnp
from jax import lax
from jax.experimental import pallas as pl
from jax.experimental.pallas import tpu as pltpu

F32 = jnp.float32
BF16 = jnp.bfloat16
_MXU = BF16

D = 1024
MEM = 256
NH, HP, NG, NS = 16, 64, 2, 128
GW = NH * HP // NG
CH = 128
XBC = NH * HP + 2 * NG * NS
KS, KC = 4, 31
XH, XD = 4, 256
DFF = 2816
EPS = 1e-6
COL_Z, COL_A, COL_G, COL_XBC, COL_DT, MAINW = 0, 1024, 2048, 3072, 4608, 4736
VMEM_LIMIT = 56 * 2 ** 20

ADAM_LR, ADAM_B1, ADAM_B2, ADAM_EPS, ADAM_WD, ADAM_STEP = 0.001, 0.9, 0.999, 1e-08, 0.01, 10

SDS = jax.ShapeDtypeStruct
MESHID = pl.DeviceIdType.MESH


def _cp(*sem):
    return pltpu.CompilerParams(dimension_semantics=sem, vmem_limit_bytes=VMEM_LIMIT)


def _tile(n, cap, unit=128):
    if n <= cap:
        return n
    best = None
    for t in range(unit, cap + 1, unit):
        if n % t == 0:
            best = t
    assert best is not None, (n, cap)
    return best


def _sigmoid(x):
    return 1.0 / (1.0 + jnp.exp(-x))


def _silu(x):
    return x * _sigmoid(x)


def _dsilu(x):
    s = _sigmoid(x)
    return s * (1.0 + x * (1.0 - s))


def _softplus(x):
    return jnp.maximum(x, 0.0) + jnp.log(1.0 + jnp.exp(-jnp.abs(x)))


def _split_bf16(x, passes):
    parts, r = [], x.astype(F32)
    for _ in range(passes):
        p = r.astype(BF16)
        parts.append(p)
        r = r - p.astype(F32)
    return parts


def _dot(a, b, dims=None, exact=None, passes=2):
    dn = {None: (((1,), (0,)), ((), ())), "nt": (((1,), (1,)), ((), ())), "tn": (((0,), (0,)), ((), ()))}[dims]
    if exact is None:
        return lax.dot_general(a.astype(_MXU), b.astype(_MXU), dn, preferred_element_type=F32)
    if exact == "a":
        terms = [(a.astype(BF16), p) for p in _split_bf16(b, passes)]
    else:
        terms = [(p, b.astype(BF16)) for p in _split_bf16(a, passes)]
    out = None
    for lhs, rhs in terms:
        d = lax.dot_general(lhs, rhs, dn, preferred_element_type=F32)
        out = d if out is None else out + d
    return out


def _rms_bwd_tile(xv, gv, dy, dres):
    r = lax.rsqrt(jnp.mean(xv * xv, axis=-1, keepdims=True) + EPS)
    xh = xv * r
    gdy = dy * gv
    dx = r * (gdy - xh * jnp.mean(xh * gdy, axis=-1, keepdims=True))
    return dres + dx, jnp.sum(dy * xh, axis=0, keepdims=True)


def _matmul(kind, a, b, name, add, out_dtype, tm, tw, riders, rms, norm=None, loss=None):
    a_parts = list(a) if isinstance(a, (list, tuple)) else [a]
    na = len(a_parts)
    M, K = a_parts[0].shape[0], sum(p.shape[1] for p in a_parts)
    Wd = b.shape[1] if kind == "nn" else b.shape[0]
    rd = _Riders(riders or ())
    nco = len(rd.arrays())
    nin = na + 1 + (add is not None) + (3 if rms else 0) + (norm is not None) + (2 if loss else 0)
    low = bool(rms and rms[3])
    nout = (2 + low) if rms else 2 if norm is not None else 4 if loss else 1
    grid = (Wd // tw, M // tm)
    assert not (rms or loss or norm is not None) or tw == Wd, "the row-wise epilogues need whole rows"

    def body(*refs):
        b_ref = refs[na]
        av = refs[0][...] if na == 1 else jnp.concatenate([r[...] for r in refs[:na]], axis=1)
        outs = refs[nin + nco:nin + nco + nout]
        rd.bind(refs[nin:nin + nco], refs[nin + nco + nout:nin + 2 * nco + nout], refs[nin + 2 * nco + nout:], grid).start()
        acc = _dot(av, b_ref[...], None if kind == "nn" else "nt")
        if add is not None:
            acc = acc + refs[na + 1][...]
        if loss:
            lpart, dx, dg = _final_loss_tile(acc, refs[nin - 2][...], refs[nin - 1][...])

            @pl.when(pl.program_id(1) == 0)
            def _():
                outs[0][...] = jnp.zeros_like(outs[0])
                outs[3][...] = jnp.zeros_like(outs[3])

            outs[0][...] += lpart
            outs[1][...] = dx
            outs[2][...] = dx.astype(outs[2].dtype)
            outs[3][...] += dg
        elif norm is not None:
            outs[0][...] = acc.astype(outs[0].dtype)
            r = lax.rsqrt(jnp.mean(acc * acc, axis=-1, keepdims=True) + EPS)
            outs[1][...] = (acc * r * refs[nin - 1][...]).astype(outs[1].dtype)
        elif rms:
            x_ref, g_ref, dres_ref = refs[nin - 3:nin]
            tot, dg = _rms_bwd_tile(x_ref[...], g_ref[...], acc, dres_ref[...])

            @pl.when(pl.program_id(1) == 0)
            def _():
                outs[-1][...] = jnp.zeros_like(outs[-1])

            outs[-1][...] += dg
            outs[0][...] = tot
            if low:
                outs[1][...] = tot.astype(outs[1].dtype)
        else:
            outs[0][...] = acc.astype(outs[0].dtype)
        rd.finish()

    tile = pl.BlockSpec((tm, tw), lambda j, i: (i, j))
    bspec = pl.BlockSpec((K, tw), lambda j, i: (0, j)) if kind == "nn" else pl.BlockSpec((tw, K), lambda j, i: (j, 0))
    in_specs = [pl.BlockSpec((tm, p.shape[1]), lambda j, i: (i, 0)) for p in a_parts] + [bspec]
    args = a_parts + [b]
    if add is not None:
        in_specs.append(tile)
        args.append(add)
    vec = pl.BlockSpec((1, tw), lambda j, i: (0, j))
    if rms:
        in_specs += [tile, vec, tile]
        args += [rms[0], rms[1], rms[2]]
        out_specs = [tile] * (1 + low) + [vec]
        out_shape = [SDS((M, Wd), F32)] + ([SDS((M, Wd), _MXU)] if low else []) + [SDS((1, Wd), F32)]
    elif loss:
        in_specs += [vec, tile]
        args += [loss[0], loss[1]]
        out_specs = [pl.BlockSpec((1, 128), lambda j, i: (0, 0)), tile, tile, vec]
        out_shape = [SDS((1, 128), F32), SDS((M, Wd), F32), SDS((M, Wd), _MXU), SDS((1, Wd), F32)]
    elif norm is not None:
        in_specs.append(vec)
        args.append(norm)
        out_specs, out_shape = [tile, tile], [SDS((M, Wd), out_dtype), SDS((M, Wd), _MXU)]
    else:
        out_specs, out_shape = [tile], [SDS((M, Wd), out_dtype)]
    order = ("arbitrary", "arbitrary") if (nco or rms or loss) else ("parallel", "parallel")
    outs = pl.pallas_call(
        body, name=name, grid=grid, in_specs=in_specs + [HBM_SPEC] * nco, out_specs=out_specs + [HBM_SPEC] * nco,
        out_shape=out_shape + rd.out_shapes(), scratch_shapes=rd.scratch(),
        compiler_params=_cp(*order))(*args, *rd.arrays())
    main = tuple(outs[:nout]) if nout > 1 else outs[0]
    return main if riders is None else (main, rd.split(outs[nout:]))


def _mm_nn(a, b, name, add=None, out_dtype=F32, tm_cap=1024, tn_cap=1408, riders=None, rms=None, norm=None, loss=None):
    tm, tn = _tile(a.shape[0], tm_cap, 8), _tile(b.shape[1], tn_cap)
    return _matmul("nn", a, b, name, add, out_dtype, tm, tn, riders, rms, norm, loss)


def _mm_nt(a, b, name, add=None, out_dtype=F32, tm_cap=512, tk_cap=1024, riders=None, rms=None):
    rows = (a[0] if isinstance(a, (list, tuple)) else a).shape[0]
    tm, tk = _tile(rows, tm_cap, 8), _tile(b.shape[0], tk_cap)
    return _matmul("nt", a, b, name, add, out_dtype, tm, tk, riders, rms)


def _mm_tn(a, b, name, tm_cap=1024, tk_cap=512, tn_cap=1408):
    b_parts = list(b) if isinstance(b, (list, tuple)) else [b]
    nb = len(b_parts)
    M, K = a.shape
    N = sum(p.shape[1] for p in b_parts)
    tm, tk, tn = _tile(M, tm_cap, 8), _tile(K, tk_cap), _tile(N, tn_cap)
    assert nb == 1 or tn == N

    def body(a_ref, *rest):
        o_ref = rest[nb]
        bv = rest[0][...] if nb == 1 else jnp.concatenate([r[...] for r in rest[:nb]], axis=1)

        @pl.when(pl.program_id(2) == 0)
        def _():
            o_ref[...] = jnp.zeros_like(o_ref)

        o_ref[...] += _dot(a_ref[...], bv, "tn")

    b_specs = ([pl.BlockSpec((tm, tn), lambda k, n, m: (m, n))] if nb == 1 else
               [pl.BlockSpec((tm, p.shape[1]), lambda k, n, m: (m, 0)) for p in b_parts])
    return pl.pallas_call(
        body, name=name, grid=(K // tk, N // tn, M // tm),
        in_specs=[pl.BlockSpec((tm, tk), lambda k, n, m: (m, k))] + b_specs,
        out_specs=pl.BlockSpec((tk, tn), lambda k, n, m: (k, n)), out_shape=SDS((K, N), F32),
        compiler_params=_cp("parallel", "parallel", "arbitrary"))(a, *b_parts)


def _rms_fwd(x, g, name, tb_cap=512):
    S, Dm = x.shape
    tb = _tile(S, tb_cap, 8)

    def body(x_ref, g_ref, o_ref):
        xv = x_ref[...]
        r = lax.rsqrt(jnp.mean(xv * xv, axis=-1, keepdims=True) + EPS)
        o_ref[...] = (xv * r * g_ref[...]).astype(o_ref.dtype)

    return pl.pallas_call(
        body, name=name, grid=(S // tb,),
        in_specs=[pl.BlockSpec((tb, Dm), lambda i: (i, 0)), pl.BlockSpec((1, Dm), lambda i: (0, 0))],
        out_specs=pl.BlockSpec((tb, Dm), lambda i: (i, 0)), out_shape=SDS((S, Dm), _MXU),
        compiler_params=_cp("parallel"))(x, g)


def _rms_bwd(x, g, dh, dres, name, tb_cap=512, low=True):
    S, Dm = x.shape
    tb = _tile(S, tb_cap, 8)
    need_dx = dres is not None

    def body(x_ref, g_ref, dh_ref, *rest):
        dg_ref = rest[-1]
        tot, dg = _rms_bwd_tile(x_ref[...], g_ref[...], dh_ref[...].astype(F32), rest[0][...] if need_dx else 0.0)

        @pl.when(pl.program_id(0) == 0)
        def _():
            dg_ref[...] = jnp.zeros_like(dg_ref)

        dg_ref[...] += dg
        if need_dx:
            rest[1][...] = tot
            if low:
                rest[2][...] = tot.astype(rest[2].dtype)

    row = pl.BlockSpec((tb, Dm), lambda i: (i, 0))
    vec = pl.BlockSpec((1, Dm), lambda i: (0, 0))
    if need_dx:
        outs = [SDS((S, Dm), F32)] + ([SDS((S, Dm), _MXU)] if low else [])
        return pl.pallas_call(
            body, name=name, grid=(S // tb,), in_specs=[row, vec, row, row], out_specs=[row] * len(outs) + [vec],
            out_shape=outs + [SDS((1, Dm), F32)], compiler_params=_cp("arbitrary"))(x, g, dh, dres)
    return pl.pallas_call(
        body, name=name, grid=(S // tb,), in_specs=[row, vec, row], out_specs=vec,
        out_shape=SDS((1, Dm), F32), compiler_params=_cp("arbitrary"))(x, g, dh)


def _final_loss_tile(xv, gv, tv):
    r = lax.rsqrt(jnp.mean(xv * xv, axis=-1, keepdims=True) + EPS)
    xh = xv * r
    e = xh * gv - tv
    dy = e * (1.0 / xv.shape[-1])
    gdy = dy * gv
    dx = r * (gdy - xh * jnp.mean(xh * gdy, axis=-1, keepdims=True))
    return 0.5 * jnp.sum(jnp.mean(e * e, axis=-1, keepdims=True)), dx, jnp.sum(dy * xh, axis=0, keepdims=True)


SSD_HALO = 8
CF_HALO = 32

HBM_SPEC = pl.BlockSpec(memory_space=pl.ANY)


def _chip_peers(x, y):
    return [(1 - x, y), (x, 1 - y), (1 - x, 1 - y)]


def _remote(src, dst, send_sem, recv_sem, dev):
    return pltpu.make_async_remote_copy(src_ref=src, dst_ref=dst, send_sem=send_sem, recv_sem=recv_sem,
                                        device_id=dev, device_id_type=MESHID)


def _scatter_copies(srcs, outs, send_sems, recv_sems):
    x, y, c = lax.axis_index("x"), lax.axis_index("y"), lax.axis_index("c")
    me = 2 * x + y
    sends, recvs = [], []
    for i, (s, o) in enumerate(zip(srcs, outs)):
        for k, (px, py) in enumerate(_chip_peers(x, y)):
            j = 3 * i + k
            sends.append(_remote(s.at[2 * px + py], o.at[me], send_sems.at[j], recv_sems.at[j], (px, py, c)))
            recvs.append(_remote(s.at[me], o.at[2 * px + py], send_sems.at[j], recv_sems.at[j], (px, py, c)))
    return sends, recvs


def _pair_copies(srcs, outs, send_sems, recv_sems):
    x, y, c = lax.axis_index("x"), lax.axis_index("y"), lax.axis_index("c")
    sends = [_remote(s.at[j, 1 - c], o.at[j], send_sems.at[4 * i + j], recv_sems.at[4 * i + j], (x, y, 1 - c))
             for i, (s, o) in enumerate(zip(srcs, outs)) for j in range(4)]
    return sends, sends


def _rows_half(ref, rows, h):
    r = rows // 2
    return ref.at[pl.ds(h * r if isinstance(h, int) else pl.multiple_of(h * r, 8), r)]


def _gather_copies(srcs, outs, rows, send_sems, recv_sems):
    x, y, c = lax.axis_index("x"), lax.axis_index("y"), lax.axis_index("c")
    me = 2 * x + y
    sends, recvs = [], []
    for i, (s, o) in enumerate(zip(srcs, outs)):
        mine = _rows_half(s, rows[i], c)
        for k, (px, py) in enumerate(_chip_peers(x, y)):
            j = 3 * i + k
            sends.append(_remote(mine, o.at[me, c], send_sems.at[j], recv_sems.at[j], (px, py, c)))
            recvs.append(_remote(mine, o.at[2 * px + py, c], send_sems.at[j], recv_sems.at[j], (px, py, c)))
    return sends, recvs


def _gather_shapes(shards):
    return [SDS((4, 2, a.shape[0] // 2, a.shape[1]), a.dtype) for a in shards]


class _Rider:
    SEMS_PER_ARRAY = {"exchange": 3, "gather": 3, "pair": 4}

    def __init__(self, kind, arrays):
        self.kind, self.arrays = kind, list(arrays)

    def out_shapes(self):
        if self.kind == "gather":
            return _gather_shapes(self.arrays)
        if self.kind == "pair":
            return [SDS((4,) + a.shape[2:], a.dtype) for a in self.arrays]
        return [SDS(a.shape, a.dtype) for a in self.arrays]

    def scratch(self):
        n = self.SEMS_PER_ARRAY[self.kind] * len(self.arrays)
        return [pltpu.SemaphoreType.DMA((n,)), pltpu.SemaphoreType.DMA((n,))]

    def copies(self, srcs, outs, send_sems, recv_sems):
        if self.kind == "gather":
            return _gather_copies(srcs, outs, [a.shape[0] for a in self.arrays], send_sems, recv_sems)
        if self.kind == "pair":
            return _pair_copies(srcs, outs, send_sems, recv_sems)
        return _scatter_copies(srcs, outs, send_sems, recv_sems)


class _Riders:
    def __init__(self, riders):
        self.given = list(riders)
        self.riders = [r for r in self.given if r.arrays]

    def arrays(self):
        return [a for r in self.riders for a in r.arrays]

    def out_shapes(self):
        return [s for r in self.riders for s in r.out_shapes()]

    def scratch(self):
        return [s for r in self.riders for s in r.scratch()]

    def split(self, outs):
        res, k = [], 0
        for r in self.given:
            res.append(list(outs[k:k + len(r.arrays)]))
            k += len(r.arrays)
        return res

    def bind(self, in_refs, out_refs, sem_refs, steps):
        self.steps = steps if isinstance(steps, tuple) else (steps,)
        self.bound, k = [], 0
        for i, r in enumerate(self.riders):
            n = len(r.arrays)
            self.bound.append((r, in_refs[k:k + n], out_refs[k:k + n], sem_refs[2 * i], sem_refs[2 * i + 1]))
            k += n
        return self

    def _at(self, last):
        hit = None
        for ax, n in enumerate(self.steps):
            here = pl.program_id(ax) == (n - 1 if last else 0)
            hit = here if hit is None else jnp.logical_and(hit, here)
        return hit

    def _copies(self):
        sends, recvs = [], []
        for r, srcs, outs, send_sems, recv_sems in self.bound:
            s, w = r.copies(srcs, outs, send_sems, recv_sems)
            sends += s
            recvs += w
        return sends, recvs

    def start(self):
        if self.riders:
            @pl.when(self._at(last=False))
            def _():
                for cp in self._copies()[0]:
                    cp.start()

    def finish(self):
        if self.riders:
            @pl.when(self._at(last=True))
            def _():
                sends, recvs = self._copies()
                for cp in recvs:
                    cp.wait_recv()
                for cp in sends:
                    cp.wait_send()


def _head_mats():
    e = lax.broadcasted_iota(jnp.int32, (128, NH * HP), 1) // HP == lax.broadcasted_iota(jnp.int32, (128, NH * HP), 0)
    return e.astype(BF16), e.T.astype(F32)


def _tri_masks():
    r = lax.broadcasted_iota(jnp.int32, (CH, CH), 0)
    c = lax.broadcasted_iota(jnp.int32, (CH, CH), 1)
    return (c <= r), (r <= c)


def _ssd_common(xbc_c, dtr, dtb, alog, e, tril, triu):
    xbc = _silu(xbc_c)
    xs = xbc[:, :NH * HP]
    dt = _softplus(dtr + dtb)
    A = -jnp.exp(alog)
    a = dt * A
    cs = _dot(tril, a, exact="a", passes=3)
    csT = _dot(a, triu, "tn", exact="b", passes=3)
    csL = cs[CH - 1:CH, :]
    wdec = jnp.exp(csL - cs) * dt
    dtE = _dot(dt, e, exact="b")
    ecsE = _dot(jnp.exp(cs), e, exact="b")
    wE = _dot(wdec, e, exact="b")
    eL = jnp.exp(csL)
    return xbc, xs, dt, A, cs, csT, csL, wdec, dtE, ecsE, wE, eL


def _ssd_fwd(proj, cw, cb, sc, norm_g, riders=(), name="ssd_fwd"):
    S = proj.shape[0]
    nc = S // CH
    rd = _Riders(riders)
    nco = len(rd.arrays())

    def body(*refs):
        z_ref, xp_ref, cw_ref, cb_ref, dtr_ref, sc_ref, ng_ref, e_ref, et_ref = refs[:9]
        xc_ref, y_ref, yn_ref, hp_ref = refs[9 + nco:13 + nco]
        hst, cext = refs[13 + 2 * nco:15 + 2 * nco]
        rd.bind(refs[9:9 + nco], refs[13 + nco:13 + 2 * nco], refs[15 + 2 * nco:], nc).start()

        @pl.when(pl.program_id(0) == 0)
        def _():
            hst[...] = jnp.zeros_like(hst)
            cext[pl.ds(0, SSD_HALO), :] = jnp.zeros((SSD_HALO, XBC), F32)

        cext[pl.ds(SSD_HALO, CH), :] = xp_ref[...]
        xc = jnp.zeros((CH, XBC), F32) + cb_ref[...]
        for k in range(KS):
            xc = xc + cext[pl.ds(SSD_HALO - (KS - 1) + k, CH), :] * cw_ref[k:k + 1, :]
        xc_ref[...] = xc
        cext[pl.ds(0, SSD_HALO), :] = cext[pl.ds(CH, SSD_HALO), :]

        e, et = e_ref[...], et_ref[...]
        tril, triu = _tri_masks()
        xbc, xs, dt, A, cs, csT, csL, wdec, dtE, ecsE, wE, eL = _ssd_common(
            xc, dtr_ref[...], sc_ref[0:1, :], sc_ref[1:2, :], e, tril, triu)
        hp_ref[0] = hst[...]
        xd = xs * dtE
        xw = xs * wE
        dE = _dot(jnp.broadcast_to(sc_ref[2:3, :], (8, 128)), e, exact="b", passes=3)[0:1, :]
        eLcol = jnp.sum(et * eL, axis=1, keepdims=True)
        for g in range(NG):
            Bg = xbc[:, NH * HP + g * NS: NH * HP + (g + 1) * NS]
            Cg = xbc[:, NH * HP + NG * NS + g * NS: NH * HP + NG * NS + (g + 1) * NS]
            gs = slice(g * GW, (g + 1) * GW)
            G = _dot(Cg, Bg, "nt")
            hg = hst[gs, :]
            yoff = ecsE[:, gs] * _dot(Cg, hg, "nt")
            hst[gs, :] = eLcol[gs, :] * hg + _dot(xw[:, gs], Bg, "tn")
            for hh in range(NH // NG):
                h = g * (NH // NG) + hh
                hs = slice(h * HP, (h + 1) * HP)
                m = jnp.where(tril, jnp.exp(jnp.where(tril, cs[:, h:h + 1] - csT[h:h + 1, :], 0.0)), 0.0)
                yd = _dot(G * m, xd[:, hs])
                y_ref[:, hs] = yd + yoff[:, hh * HP:(hh + 1) * HP] + dE[:, hs] * xs[:, hs]
        y = y_ref[...]
        yz = y * _silu(z_ref[...])
        for g in range(NG):
            gs = slice(g * GW, (g + 1) * GW)
            yg = yz[:, gs]
            r = lax.rsqrt(jnp.mean(yg * yg, axis=-1, keepdims=True) + EPS)
            yn_ref[:, gs] = (yg * r * ng_ref[:, gs]).astype(yn_ref.dtype)
        rd.finish()

    outs = pl.pallas_call(
        body, name=name, grid=(nc,),
        in_specs=[pl.BlockSpec((CH, D), lambda c: (c, COL_Z // D)),
                  pl.BlockSpec((CH, XBC), lambda c: (c, COL_XBC // XBC)),
                  pl.BlockSpec((KS, XBC), lambda c: (0, 0)),
                  pl.BlockSpec((1, XBC), lambda c: (0, 0)),
                  pl.BlockSpec((CH, 128), lambda c: (c, COL_DT // 128)),
                  pl.BlockSpec((8, 128), lambda c: (0, 0)),
                  pl.BlockSpec((1, D), lambda c: (0, 0)),
                  pl.BlockSpec((128, NH * HP), lambda c: (0, 0)),
                  pl.BlockSpec((NH * HP, 128), lambda c: (0, 0))] + [HBM_SPEC] * nco,
        out_specs=[pl.BlockSpec((CH, XBC), lambda c: (c, 0)), pl.BlockSpec((CH, D), lambda c: (c, 0)),
                   pl.BlockSpec((CH, D), lambda c: (c, 0)),
                   pl.BlockSpec((1, NH * HP, NS), lambda c: (c, 0, 0))] + [HBM_SPEC] * nco,
        out_shape=[SDS((S, XBC), F32), SDS((S, D), F32), SDS((S, D), _MXU), SDS((nc, NH * HP, NS), F32)]
        + rd.out_shapes(),
        scratch_shapes=[pltpu.VMEM((NH * HP, NS), F32), pltpu.VMEM((SSD_HALO + CH, XBC), F32)] + rd.scratch(),
        compiler_params=_cp("arbitrary"))(proj, proj, cw, cb, proj, sc, norm_g, *_head_mats(), *rd.arrays())
    return outs[:4], rd.split(outs[4:])


def _ssd_bwd(dmix, y, proj, xbc_c, hprev, cw, sc, norm_g, riders=(), name="ssd_bwd"):
    S = proj.shape[0]
    nc = S // CH
    rd = _Riders(riders)
    nco = len(rd.arrays())
    rev = lambda c: nc - 1 - c

    def body(*refs):
        dyn_ref, y_ref, z_ref, x_ref, xp_ref, dtr_ref, hp_ref, cw_ref, sc_ref, ng_ref, e_ref, et_ref = refs[:12]
        dz_ref, dx_ref, ddtr_ref, gcw_ref, gcb_ref, gsc_ref, gng_ref = refs[12 + nco:19 + nco]
        dh, dxd, cext = refs[19 + 2 * nco:22 + 2 * nco]
        rd.bind(refs[12:12 + nco], refs[19 + nco:19 + 2 * nco], refs[22 + 2 * nco:], nc).start()

        @pl.when(pl.program_id(0) == 0)
        def _():
            dh[...] = jnp.zeros_like(dh)
            cext[pl.ds(CH, SSD_HALO), :] = jnp.zeros((SSD_HALO, XBC), F32)
            gcw_ref[...] = jnp.zeros_like(gcw_ref)
            gcb_ref[...] = jnp.zeros_like(gcb_ref)
            gsc_ref[...] = jnp.zeros_like(gsc_ref)
            gng_ref[...] = jnp.zeros_like(gng_ref)

        e, et = e_ref[...], et_ref[...]
        tril, triu = _tri_masks()
        xbc_c = x_ref[...]
        dtr = dtr_ref[...]
        dtb = sc_ref[0:1, :]
        xbc, xs, dt, A, cs, csT, csL, wdec, dtE, ecsE, wE, eL = _ssd_common(
            xbc_c, dtr, dtb, sc_ref[1:2, :], e, tril, triu)
        xd = xs * dtE
        xw = xs * wE
        dE = _dot(jnp.broadcast_to(sc_ref[2:3, :], (8, 128)), e, exact="b", passes=3)[0:1, :]
        eLcol = jnp.sum(et * eL, axis=1, keepdims=True)

        yv = y_ref[...]
        zv = z_ref[...]
        sz = _silu(zv)
        yz = yv * sz
        dyn = dyn_ref[...]
        dyz_parts = []
        for g in range(NG):
            gs = slice(g * GW, (g + 1) * GW)
            yg = yz[:, gs]
            r = lax.rsqrt(jnp.mean(yg * yg, axis=-1, keepdims=True) + EPS)
            yh = yg * r
            dn = dyn[:, gs]
            gng_ref[:, gs] += jnp.sum(dn * yh, axis=0, keepdims=True)
            gdn = dn * ng_ref[:, gs]
            dyz_parts.append(r * (gdn - yh * jnp.mean(yh * gdn, axis=-1, keepdims=True)))
        dyz = jnp.concatenate(dyz_parts, axis=1)
        dy = dyz * sz
        dz_ref[...] = (dyz * yv * _dsilu(zv)).astype(dz_ref.dtype)

        dxs = dE * dy
        dzo = ecsE * dy
        dcsL = jnp.zeros((1, 128), F32)
        ddt = jnp.zeros((CH, 128), F32)
        qcols = jnp.zeros((CH, 128), F32)
        qrows = jnp.zeros((128, CH), F32)
        lane = lax.broadcasted_iota(jnp.int32, (1, 128), 1)
        sub = lax.broadcasted_iota(jnp.int32, (128, 1), 0)
        dB_parts, dC_parts, yoff_parts, dxw_parts = [], [], [], []
        for g in range(NG):
            Bg = xbc[:, NH * HP + g * NS: NH * HP + (g + 1) * NS]
            Cg = xbc[:, NH * HP + NG * NS + g * NS: NH * HP + NG * NS + (g + 1) * NS]
            gs = slice(g * GW, (g + 1) * GW)
            hg = hp_ref[0, gs, :]
            dhn = dh[gs, :]
            G = _dot(Cg, Bg, "nt")
            yoff_parts.append(ecsE[:, gs] * _dot(Cg, hg, "nt"))
            dC = _dot(dzo[:, gs], hg)
            dhp = _dot(dzo[:, gs], Cg, "tn") + eLcol[gs, :] * dhn
            t1 = jnp.sum(dhn * hg, axis=1, keepdims=True) * eLcol[gs, :]
            dcsL = dcsL + jnp.sum(et[gs, :] * t1, axis=0, keepdims=True)
            dxw_parts.append(_dot(Bg, dhn, "nt"))
            dB = _dot(xw[:, gs], dhn)
            dgsum = jnp.zeros((CH, CH), F32)
            for hh in range(NH // NG):
                h = g * (NH // NG) + hh
                hs = slice(h * HP, (h + 1) * HP)
                m = jnp.where(tril, jnp.exp(jnp.where(tril, cs[:, h:h + 1] - csT[h:h + 1, :], 0.0)), 0.0)
                sc = G * m
                dyh = dy[:, hs]
                dxd[:, hs] = _dot(sc, dyh, "tn")
                dsc = _dot(dyh, xd[:, hs], "nt")
                q = dsc * sc
                qcols = qcols + jnp.where(lane == h, jnp.sum(q, axis=1, keepdims=True), 0.0)
                qrows = qrows + jnp.where(sub == h, jnp.sum(q, axis=0, keepdims=True), 0.0)
                dgsum = dgsum + dsc * m
            dC_parts.append(dC + _dot(dgsum, Bg))
            dB_parts.append(dB + _dot(dgsum, Cg, "tn"))
            dh[gs, :] = dhp
        yoff = jnp.concatenate(yoff_parts, axis=1)
        dxw = jnp.concatenate(dxw_parts, axis=1)
        dxdv = dxd[...]
        per_head = _dot(jnp.concatenate([dy * yoff, dxw * xs, dxdv * xs, dy * xs], axis=0), et, exact="b")
        dcs = qcols - qrows.T + per_head[0:CH]
        dw = per_head[CH:2 * CH]
        gsc_ref[2:3, :] += jnp.sum(per_head[3 * CH:4 * CH], axis=0, keepdims=True)
        dxs = dxs + wE * dxw + dtE * dxdv
        ddt = ddt + dw * jnp.exp(csL - cs) + per_head[2 * CH:3 * CH]
        dcs = dcs - dw * wdec
        dcsL = dcsL + jnp.sum(dw * wdec, axis=0, keepdims=True)
        last = lax.broadcasted_iota(jnp.int32, (CH, 128), 0) == CH - 1
        dcs = dcs + jnp.where(last, dcsL, 0.0)
        da = _dot(triu, dcs, exact="a", passes=3)
        ddt = ddt + da * A
        gsc_ref[1:2, :] += jnp.sum(da * dt, axis=0, keepdims=True) * A
        valid = lax.broadcasted_iota(jnp.int32, (CH, 128), 1) < NH
        ddtr = jnp.where(valid, ddt * _sigmoid(dtr + dtb), 0.0)
        gsc_ref[0:1, :] += jnp.sum(ddtr, axis=0, keepdims=True)
        ddtr_ref[...] = ddtr.astype(ddtr_ref.dtype)
        dxbc = jnp.concatenate([dxs] + dB_parts + dC_parts, axis=1)
        dxc = dxbc * _dsilu(xbc_c)
        cext[pl.ds(0, CH), :] = dxc
        xp = xp_ref[...]
        acc = jnp.zeros((CH, XBC), F32)
        for k in range(KS):
            sh = cext[pl.ds(KS - 1 - k, CH), :]
            acc = acc + sh * cw_ref[k:k + 1, :]
            gcw_ref[k:k + 1, :] += jnp.sum(xp * sh, axis=0, keepdims=True)
        gcb_ref[...] += jnp.sum(dxc, axis=0, keepdims=True)
        dx_ref[...] = acc.astype(dx_ref.dtype)
        cext[pl.ds(CH, SSD_HALO), :] = cext[pl.ds(0, SSD_HALO), :]
        rd.finish()

    vec = pl.BlockSpec((8, 128), lambda c: (0, 0))
    vecd = pl.BlockSpec((1, D), lambda c: (0, 0))
    cwsp = pl.BlockSpec((KS, XBC), lambda c: (0, 0))
    cbsp = pl.BlockSpec((1, XBC), lambda c: (0, 0))
    row = lambda w, j=0: pl.BlockSpec((CH, w), lambda c: (rev(c), j))
    outs = pl.pallas_call(
        body, name=name, grid=(nc,),
        in_specs=[row(D), row(D), row(D, COL_Z // D), row(XBC), row(XBC, COL_XBC // XBC), row(128, COL_DT // 128),
                  pl.BlockSpec((1, NH * HP, NS), lambda c: (rev(c), 0, 0)), cwsp, vec, vecd,
                  pl.BlockSpec((128, NH * HP), lambda c: (0, 0)),
                  pl.BlockSpec((NH * HP, 128), lambda c: (0, 0))] + [HBM_SPEC] * nco,
        out_specs=[row(D), row(XBC), row(128), cwsp, cbsp, vec, vecd] + [HBM_SPEC] * nco,
        out_shape=[SDS((S, D), _MXU), SDS((S, XBC), _MXU), SDS((S, 128), _MXU), SDS((KS, XBC), F32),
                   SDS((1, XBC), F32), SDS((8, 128), F32), SDS((1, D), F32)] + rd.out_shapes(),
        scratch_shapes=[pltpu.VMEM((NH * HP, NS), F32), pltpu.VMEM((CH, NH * HP), F32),
                        pltpu.VMEM((CH + SSD_HALO, XBC), F32)] + rd.scratch(),
        compiler_params=_cp("arbitrary"))(dmix, y, proj, xbc_c, proj, proj, hprev, cw, sc, norm_g, *_head_mats(),
                                          *rd.arrays())
    return outs[:7], rd.split(outs[7:])


CONV_RT = 32


def _fill_phases(ext, ph, rows):
    for s in range(1, 8):
        ph[s - 1, pl.ds(0, rows), :] = ext[pl.ds(s, rows), :]


def _window(ext, ph, off, r0, ls):
    s = off % 8
    src = ext if s == 0 else ph.at[s - 1]
    return src[pl.ds(pl.multiple_of(off - s + r0, 8), CONV_RT), ls]


def _cf_fwd(proj, w, b, lg, lb, riders=(), name="cf_fwd", tb_cap=256):
    S = proj.shape[0]
    tb = _tile(S, tb_cap, 8)
    nb = S // tb
    rd = _Riders(riders)
    nco = len(rd.arrays())

    def body(*refs):
        a_ref, g_ref, w_ref, b_ref, lg_ref, lb_ref = refs[:6]
        u1_ref, u_ref = refs[6 + nco:8 + nco]
        ext, ph = refs[8 + 2 * nco:10 + 2 * nco]
        rd.bind(refs[6:6 + nco], refs[8 + nco:8 + 2 * nco], refs[10 + 2 * nco:], nb).start()

        @pl.when(pl.program_id(0) == 0)
        def _():
            ext[pl.ds(0, CF_HALO), :] = jnp.zeros((CF_HALO, D), F32)

        ext[pl.ds(CF_HALO, tb), :] = a_ref[...] * _sigmoid(g_ref[...])
        _fill_phases(ext, ph, tb + CF_HALO - 8)

        def tile(i, carry):
            r0 = pl.multiple_of(i * CONV_RT, CONV_RT)
            for l in range(D // 128):
                ls = pl.ds(l * 128, 128)
                acc = jnp.broadcast_to(b_ref[:, ls], (CONV_RT, 128))
                for k in range(KC):
                    acc = acc + _window(ext, ph, CF_HALO - (KC - 1) + k, r0, ls) * w_ref[k:k + 1, ls]
                u1_ref[pl.ds(r0, CONV_RT), ls] = acc
            return carry

        lax.fori_loop(0, tb // CONV_RT, tile, 0)
        acc = u1_ref[...]
        mu = jnp.mean(acc, axis=-1, keepdims=True)
        xc = acc - mu
        r = lax.rsqrt(jnp.mean(xc * xc, axis=-1, keepdims=True) + EPS)
        u_ref[...] = _silu(xc * r * lg_ref[...] + lb_ref[...]).astype(u_ref.dtype)
        ext[pl.ds(0, CF_HALO), :] = ext[pl.ds(tb, CF_HALO), :]
        rd.finish()

    vec = pl.BlockSpec((1, D), lambda i: (0, 0))
    outs = pl.pallas_call(
        body, name=name, grid=(nb,),
        in_specs=[pl.BlockSpec((tb, D), lambda i: (i, COL_A // D)), pl.BlockSpec((tb, D), lambda i: (i, COL_G // D)),
                  pl.BlockSpec((KC, D), lambda i: (0, 0)), vec, vec, vec] + [HBM_SPEC] * nco,
        out_specs=[pl.BlockSpec((tb, D), lambda i: (i, 0)), pl.BlockSpec((tb, D), lambda i: (i, 0))] + [HBM_SPEC] * nco,
        out_shape=[SDS((S, D), F32), SDS((S, D), _MXU)] + rd.out_shapes(),
        scratch_shapes=[pltpu.VMEM((CF_HALO + tb, D), F32), pltpu.VMEM((7, tb + CF_HALO - 8, D), F32)] + rd.scratch(),
        compiler_params=_cp("arbitrary"))(proj, proj, w, b, lg, lb, *rd.arrays())
    return outs[:2], rd.split(outs[2:])


def _cf_bwd(dmix, u1, proj, w, lg, lb, riders=(), name="cf_bwd", tb_cap=256):
    S = proj.shape[0]
    tb = _tile(S, tb_cap, 8)
    nb = S // tb
    rd = _Riders(riders)
    nco = len(rd.arrays())
    rev = lambda i: nb - 1 - i

    def body(*refs):
        du_ref, u1_ref, a_ref, g_ref, w_ref, lg_ref, lb_ref = refs[:7]
        da_ref, dg_ref, dw_ref, db_ref, dlg_ref, dlb_ref = refs[7 + nco:13 + nco]
        ext, ph, u0s = refs[13 + 2 * nco:16 + 2 * nco]
        rd.bind(refs[7:7 + nco], refs[13 + nco:13 + 2 * nco], refs[16 + 2 * nco:], nb).start()

        @pl.when(pl.program_id(0) == 0)
        def _():
            ext[pl.ds(tb, CF_HALO), :] = jnp.zeros((CF_HALO, D), F32)
            dw_ref[...] = jnp.zeros_like(dw_ref)
            db_ref[...] = jnp.zeros_like(db_ref)
            dlg_ref[...] = jnp.zeros_like(dlg_ref)
            dlb_ref[...] = jnp.zeros_like(dlb_ref)

        u1 = u1_ref[...]
        mu = jnp.mean(u1, axis=-1, keepdims=True)
        xc = u1 - mu
        r = lax.rsqrt(jnp.mean(xc * xc, axis=-1, keepdims=True) + EPS)
        xh = xc * r
        lgv = lg_ref[...]
        du2 = du_ref[...] * _dsilu(xh * lgv + lb_ref[...])
        dlg_ref[...] += jnp.sum(du2 * xh, axis=0, keepdims=True)
        dlb_ref[...] += jnp.sum(du2, axis=0, keepdims=True)
        gd = du2 * lgv
        du1 = r * (gd - jnp.mean(gd, axis=-1, keepdims=True) - xh * jnp.mean(gd * xh, axis=-1, keepdims=True))
        db_ref[...] += jnp.sum(du1, axis=0, keepdims=True)
        ext[pl.ds(0, tb), :] = du1
        u0s[...] = a_ref[...] * _sigmoid(g_ref[...])
        _fill_phases(ext, ph, tb + CF_HALO - 8)

        for l in range(D // 128):
            ls = pl.ds(l * 128, 128)

            def tile(i, accs, ls=ls):
                r0 = pl.multiple_of(i * CONV_RT, CONV_RT)
                rows = pl.ds(r0, CONV_RT)
                u0t = u0s[rows, ls]
                acc = jnp.zeros((CONV_RT, 128), F32)
                out = []
                for k in range(KC):
                    win = _window(ext, ph, KC - 1 - k, r0, ls)
                    acc = acc + win * w_ref[k:k + 1, ls]
                    p = u0t * win
                    out.append(accs[k] + ((p[0:8] + p[8:16]) + (p[16:24] + p[24:32])))
                sg = _sigmoid(g_ref[rows, ls])
                da_ref[rows, ls] = (acc * sg).astype(da_ref.dtype)
                dg_ref[rows, ls] = (acc * a_ref[rows, ls] * sg * (1.0 - sg)).astype(dg_ref.dtype)
                return tuple(out)

            accs = lax.fori_loop(0, tb // CONV_RT, tile, tuple(jnp.zeros((8, 128), F32) for _ in range(KC)))
            for k in range(KC):
                dw_ref[k:k + 1, ls] += jnp.sum(accs[k], axis=0, keepdims=True)
        ext[pl.ds(tb, CF_HALO), :] = ext[pl.ds(0, CF_HALO), :]
        rd.finish()

    vec = pl.BlockSpec((1, D), lambda i: (0, 0))
    wsp = pl.BlockSpec((KC, D), lambda i: (0, 0))
    row = lambda j=0: pl.BlockSpec((tb, D), lambda i: (rev(i), j))
    outs = pl.pallas_call(
        body, name=name, grid=(nb,),
        in_specs=[row(1), row(), row(COL_A // D), row(COL_G // D), wsp, vec, vec] + [HBM_SPEC] * nco,
        out_specs=[row(), row(), wsp, vec, vec, vec] + [HBM_SPEC] * nco,
        out_shape=[SDS((S, D), _MXU), SDS((S, D), _MXU), SDS((KC, D), F32),
                   SDS((1, D), F32), SDS((1, D), F32), SDS((1, D), F32)] + rd.out_shapes(),
        scratch_shapes=[pltpu.VMEM((tb + CF_HALO, D), F32), pltpu.VMEM((7, tb + CF_HALO - 8, D), F32),
                        pltpu.VMEM((tb, D), F32)] + rd.scratch(),
        compiler_params=_cp("arbitrary"))(dmix, u1, proj, proj, w, lg, lb, *rd.arrays())
    return outs[:6], rd.split(outs[6:])


def _attn_fwd(q, kv, name="attn_fwd", tq_cap=512):
    S = q.shape[0]
    tq = _tile(S, tq_cap, 8)
    scale = XD ** -0.5

    def body(q_ref, kv_ref, o_ref):
        for h in range(XH):
            hs = slice(h * XD, (h + 1) * XD)
            s = _dot(q_ref[:, hs], kv_ref[:, hs], "nt") * scale
            s = s - jnp.max(s, axis=-1, keepdims=True)
            p = jnp.exp(s)
            p = p / jnp.sum(p, axis=-1, keepdims=True)
            o_ref[:, hs] = _dot(p, kv_ref[:, D + h * XD: D + (h + 1) * XD]).astype(o_ref.dtype)

    return pl.pallas_call(
        body, name=name, grid=(S // tq,),
        in_specs=[pl.BlockSpec((tq, D), lambda i: (i, 0)), pl.BlockSpec((MEM, 2 * D), lambda i: (0, 0))],
        out_specs=pl.BlockSpec((tq, D), lambda i: (i, 0)), out_shape=SDS((S, D), _MXU),
        compiler_params=_cp("parallel"))(q, kv)


def _attn_bwd(do, q, kv, riders=(), name="attn_bwd", tq_cap=512):
    S = q.shape[0]
    tq = _tile(S, tq_cap, 8)
    scale = XD ** -0.5
    rd = _Riders(riders)
    nco = len(rd.arrays())

    def body(*refs):
        do_ref, q_ref, kv_ref = refs[:3]
        dq_ref, dkv_ref = refs[3 + nco:5 + nco]
        rd.bind(refs[3:3 + nco], refs[5 + nco:5 + 2 * nco], refs[5 + 2 * nco:], S // tq).start()

        @pl.when(pl.program_id(0) == 0)
        def _():
            dkv_ref[...] = jnp.zeros_like(dkv_ref)

        for h in range(XH):
            hs = slice(h * XD, (h + 1) * XD)
            vs = slice(D + h * XD, D + (h + 1) * XD)
            qh = q_ref[:, hs]
            kh = kv_ref[:, hs]
            s = _dot(qh, kh, "nt") * scale
            s = s - jnp.max(s, axis=-1, keepdims=True)
            p = jnp.exp(s)
            p = p / jnp.sum(p, axis=-1, keepdims=True)
            doh = do_ref[:, hs]
            dp = _dot(doh, kv_ref[:, vs], "nt")
            ds = p * (dp - jnp.sum(dp * p, axis=-1, keepdims=True)) * scale
            dq_ref[:, hs] = _dot(ds, kh).astype(dq_ref.dtype)
            dkv_ref[:, hs] += _dot(ds, qh, "tn")
            dkv_ref[:, vs] += _dot(p, doh, "tn")
        rd.finish()

    outs = pl.pallas_call(
        body, name=name, grid=(S // tq,),
        in_specs=[pl.BlockSpec((tq, D), lambda i: (i, 0)), pl.BlockSpec((tq, D), lambda i: (i, 0)),
                  pl.BlockSpec((MEM, 2 * D), lambda i: (0, 0))] + [HBM_SPEC] * nco,
        out_specs=[pl.BlockSpec((tq, D), lambda i: (i, 0)), pl.BlockSpec((MEM, 2 * D), lambda i: (0, 0))]
        + [HBM_SPEC] * nco,
        out_shape=[SDS((S, D), _MXU), SDS((MEM, 2 * D), F32)] + rd.out_shapes(), scratch_shapes=rd.scratch(),
        compiler_params=_cp("arbitrary"))(do, q, kv, *rd.arrays())
    return outs[:2], rd.split(outs[2:])


def _ffn_in(hf, wg_t, wu_t, name="ffn_in", tm_cap=512, tn_cap=1408):
    S, K = hf.shape
    N = wg_t.shape[0]
    tm, tn = _tile(S, tm_cap, 8), _tile(N, tn_cap)

    def body(a_ref, g_ref, u_ref, act_ref, gt_ref, up_ref):
        a = a_ref[...]
        gt = _dot(a, g_ref[...], "nt")
        up = _dot(a, u_ref[...], "nt")
        act_ref[...] = (_silu(gt) * up).astype(act_ref.dtype)
        gt_ref[...] = gt.astype(gt_ref.dtype)
        up_ref[...] = up.astype(up_ref.dtype)

    wsp = pl.BlockSpec((tn, K), lambda j, i: (j, 0))
    osp = pl.BlockSpec((tm, tn), lambda j, i: (i, j))
    return pl.pallas_call(
        body, name=name, grid=(N // tn, S // tm), in_specs=[pl.BlockSpec((tm, K), lambda j, i: (i, 0)), wsp, wsp],
        out_specs=[osp, osp, osp], out_shape=[SDS((S, N), _MXU)] * 3,
        compiler_params=_cp("parallel", "parallel"))(hf, wg_t, wu_t)


def _ffn_out_bwd(dx, w_down, gt, up, name="ffn_out_dx", tm_cap=512, tk_cap=1408):
    S, N = dx.shape
    K = w_down.shape[0]
    tm, tk = _tile(S, tm_cap, 8), _tile(K, tk_cap)

    def body(a_ref, b_ref, g_ref, u_ref, dg_ref, du_ref):
        d = _dot(a_ref[...], b_ref[...], "nt")
        gt = g_ref[...].astype(F32)
        s = _sigmoid(gt)
        dg_ref[...] = (d * u_ref[...].astype(F32) * (s * (1.0 + gt * (1.0 - s)))).astype(dg_ref.dtype)
        du_ref[...] = (d * gt * s).astype(du_ref.dtype)

    osp = pl.BlockSpec((tm, tk), lambda j, i: (i, j))
    return pl.pallas_call(
        body, name=name, grid=(K // tk, S // tm),
        in_specs=[pl.BlockSpec((tm, N), lambda j, i: (i, 0)), pl.BlockSpec((tk, N), lambda j, i: (j, 0)), osp, osp],
        out_specs=[osp, osp], out_shape=[SDS((S, K), _MXU)] * 2,
        compiler_params=_cp("parallel", "parallel"))(dx, w_down, gt, up)


AG_RIDE = (("w_down", "w_o"), ("w_out", "w_q", "w_kv"), ("w_gate", "w_up"))


def _local_step(x, mem, tgt, W, P, core=None, late=None):
    pair, got = {}, {}
    ride = [[late[n] for n in grp] if late is not None else [] for grp in AG_RIDE]

    def halves(group):
        if core is None:
            return []
        gs = [_shard_grad(n, GW) for n in group]
        return [g.reshape(4, 2, g.shape[1] // 2, g.shape[2]) for g in gs]

    def pair_sums(group, hs, theirs):
        ps = [_pair_sum(h_, t, core, "rs_pair_sum_" + n) for h_, t, n in zip(hs, theirs, group)]
        pair.update(zip(group, ps))
        return ps

    h = _rms_fwd(x, P["g_mix"], "rms_mix")
    proj, (bufs0,) = _mm_nn(h, W["main"], "in_proj", tm_cap=256, tn_cap=MAINW, riders=[_Rider("gather", ride[0])])
    (xbc_c, y, yn, hprev), (bufs1,) = _ssd_fwd(proj, P["conv4_w"], P["conv4_b"], P["sc"], P["ssd_norm_g"],
                                                riders=[_Rider("gather", ride[1])])
    (u1, u), (bufs2,) = _cf_fwd(proj, P["cf_w"], P["cf_b"], P["ln_g"], P["ln_b"], riders=[_Rider("gather", ride[2])])
    if late is not None:
        names = AG_RIDE[0] + AG_RIDE[1] + AG_RIDE[2]
        full = _gather_finish_list(ride[0] + ride[1] + ride[2], bufs0 + bufs1 + bufs2)
        W = dict(W, **_pack_late(dict(zip(names, full))))
    mix = jnp.concatenate([yn, u], axis=1)
    x1, hq = _mm_nn(mix, W["out"], "out_proj", add=x, tm_cap=512, tn_cap=D, norm=P["g_xattn"])
    q = _mm_nn(hq, W["q"], "q_proj")
    mn = _rms_fwd(mem, P["g_mem"], "rms_mem")
    kv = _mm_nn(mn, W["kv"], "kv_proj")
    o = _attn_fwd(q, kv)
    x2, hf = _mm_nn(o, W["o"], "o_proj", add=x1, tm_cap=512, tn_cap=D, norm=P["g_ffn"])
    act, gt, up = _ffn_in(hf, W["gate_t"], W["up_t"])
    loss, dx3, dx3b, g_final = _mm_nn(act, W["down"], "ffn_out", add=x2, tm_cap=512, tn_cap=D,
                                      loss=(P["g_final"], tgt))
    GW, GP = {}, {"g_final": g_final}
    GW["down"] = _mm_tn(act, dx3b, "ffn_out_dw", tk_cap=1408, tn_cap=1024)
    dgt, dup = _ffn_out_bwd(dx3b, W["down"], gt, up)
    dhf = _mm_nn(dgt, W["gate_t"], "ffn_gate_dx", tm_cap=512)
    dx2, dx2b, GP["g_ffn"] = _mm_nn(dup, W["up_t"], "ffn_up_dx", add=dhf, tm_cap=512, tn_cap=D,
                                    rms=(x2, P["g_ffn"], dx3, True))
    GW["gate_t"] = _mm_tn(dgt, hf, "ffn_gate_dw", tk_cap=1408, tn_cap=1024)
    GW["up_t"] = _mm_tn(dup, hf, "ffn_up_dw", tk_cap=1408, tn_cap=1024)
    ffn_halves = halves(RS_GROUPS[0])
    do = _mm_nt(dx2b, W["o"], "o_proj_dx")
    GW["o"] = _mm_tn(o, dx2b, "o_proj_dw")
    (dq, dkv), (ffn_theirs,) = _attn_bwd(do, q, kv, riders=[_Rider("pair", ffn_halves)])
    ffn_pieces = pair_sums(RS_GROUPS[0], ffn_halves, ffn_theirs)
    dx1, dx1b, GP["g_xattn"] = _mm_nt(dq, W["q"], "q_proj_dx", tk_cap=D, rms=(x1, P["g_xattn"], dx2, True))
    GW["q"] = _mm_tn(hq, dq, "q_proj_dw")
    dkvb = dkv.astype(_MXU)
    GW["kv"] = _mm_tn(mn, dkvb, "kv_proj_dw", tm_cap=256)
    dmn = _mm_nt(dkvb, W["kv"], "kv_proj_dx")
    GP["g_mem"] = _rms_bwd(mem, P["g_mem"], dmn, None, "rms_mem_bwd")
    dmix = _mm_nt(dx1b, W["out"], "out_proj_dx")
    GW["out"] = _mm_tn(mix, dx1b, "out_proj_dw", tn_cap=1024)
    attn_halves = halves(RS_GROUPS[1])
    (da, dg, GP["cf_w"], GP["cf_b"], GP["ln_g"], GP["ln_b"]), (came, attn_theirs) = _cf_bwd(
        dmix, u1, proj, P["cf_w"], P["ln_g"], P["ln_b"],
        riders=[_Rider("exchange", ffn_pieces), _Rider("pair", attn_halves)])
    got.update(zip(RS_GROUPS[0], came))
    attn_pieces = pair_sums(RS_GROUPS[1], attn_halves, attn_theirs)
    (dz, dxbc, ddtr, GP["conv4_w"], GP["conv4_b"], GP["sc"], GP["ssd_norm_g"]), (came,) = _ssd_bwd(
        dmix, y, proj, xbc_c, hprev, P["conv4_w"], P["sc"], P["ssd_norm_g"],
        riders=[_Rider("exchange", attn_pieces)])
    got.update(zip(RS_GROUPS[1], came))
    dproj = [dz, da, dg, dxbc, ddtr]
    GW["main"] = _mm_tn(h, dproj, "in_proj_dw", tm_cap=512, tk_cap=512, tn_cap=MAINW)
    in_halves = halves(RS_GROUPS[2])
    in_pieces = pair_sums(RS_GROUPS[2], in_halves, _pair_split_list(in_halves, "rs_pair_send_w_in")) if in_halves else []
    (grad_x, GP["g_mix"]), (came,) = _mm_nt(dproj, W["main"], "in_proj_dx", tm_cap=256, tk_cap=D,
                                            riders=[_Rider("exchange", in_pieces)], rms=(x, P["g_mix"], dx1, False))
    got.update(zip(RS_GROUPS[2], came))
    if core is None:
        return loss, grad_x, GW, GP
    return loss, grad_x, GW, GP, pair, got


Z_END, XBC_END, DT_END = NH * HP, NH * HP + XBC, NH * HP + XBC + NH


def _pad_to(a, rows=None, cols=None):
    r = 0 if rows is None else rows - a.shape[0]
    c = 0 if cols is None else cols - a.shape[1]
    return jnp.pad(a, ((0, r), (0, c)))


IN_W = DT_END + 2 * D
W_IN_SEGS = [(0, Z_END, "main", COL_Z), (Z_END, XBC_END, "main", COL_XBC), (XBC_END, DT_END, "main", COL_DT),
             (DT_END, DT_END + D, "main", COL_A), (DT_END + D, IN_W, "main", COL_G)]
BIG = [("w_in", True), ("w_out", False), ("w_q", False), ("w_kv", True), ("w_o", False), ("w_gate", False),
       ("w_up", False), ("w_down", False)]
TRANSPOSED = ("w_gate", "w_up")


def _ref_cols(pieces, a, b):
    cw = IN_W // 4
    out = []
    for j in range(4):
        lo, hi = max(a, j * cw), min(b, (j + 1) * cw)
        if lo < hi:
            out.append(pieces[j][:, lo - j * cw:hi - j * cw])
    return out


def _cat_cols(pieces):
    return jnp.concatenate([pieces[j] for j in range(4)], axis=1)


def _pack_in(w_in):
    dt = _ref_cols(w_in, XBC_END, DT_END)
    pad = jnp.zeros((dt[0].shape[0], MAINW - COL_DT - NH), dt[0].dtype)
    main = jnp.concatenate(_ref_cols(w_in, 0, Z_END) + _ref_cols(w_in, DT_END, IN_W) + _ref_cols(w_in, Z_END, XBC_END)
                           + dt + [pad], axis=1)
    return {"main": main}


def _pack_late(pc):
    rows = lambda n: pc[n].reshape(-1, pc[n].shape[-1])
    return {"out": rows("w_out"), "q": rows("w_q"), "kv": _cat_cols(pc["w_kv"]), "o": rows("w_o"),
            "gate_t": rows("w_gate"), "up_t": rows("w_up"), "down": rows("w_down")}


GW_KEY = {"w_gate": "gate_t", "w_up": "up_t", "w_kv": "kv", "w_out": "out", "w_q": "q", "w_o": "o", "w_down": "down"}
RS_GROUPS = (("w_down", "w_gate", "w_up"), ("w_out", "w_q", "w_kv", "w_o"), ("w_in",))


def _shard_grad(name, GW):
    if name == "w_in":
        cw = IN_W // 4
        pieces = []
        for j in range(4):
            parts = []
            for a, b, src, col in W_IN_SEGS:
                lo, hi = max(a, j * cw), min(b, (j + 1) * cw)
                if lo < hi:
                    parts.append(GW[src][:, col + lo - a:col + hi - a])
            pieces.append(jnp.concatenate(parts, axis=1))
        return jnp.stack(pieces)
    g = GW[GW_KEY[name]]
    if dict(BIG)[name]:
        cw = g.shape[1] // 4
        return jnp.stack([g[:, j * cw:(j + 1) * cw] for j in range(4)])
    return g.reshape(4, g.shape[0] // 4, g.shape[1])


def _stack_sc(dt_bias, a_log, d):
    return _pad_to(jnp.concatenate([dt_bias, a_log, d], axis=0), rows=8, cols=128)


COMM_PARAMS = pltpu.CompilerParams(vmem_limit_bytes=VMEM_LIMIT)


def _dma_sems(*counts):
    return [pltpu.SemaphoreType.DMA((n,)) for n in counts]


def _allgather_list(arrs, name):
    n = len(arrs)
    halved = [a.shape[0] % 16 == 0 for a in arrs]
    oshape = [(4, 2, a.shape[0] // 2, a.shape[1]) if h else (4, 1) + a.shape for a, h in zip(arrs, halved)]

    def body(*refs):
        srcs, outs = refs[:n], refs[n:2 * n]
        ici_send, ici_recv, own_send, own_recv, fwd_send, fwd_recv = refs[2 * n:]
        x, y, c = lax.axis_index("x"), lax.axis_index("y"), lax.axis_index("c")
        me = 2 * x + y
        sib = (x, y, 1 - c)
        peers = _chip_peers(x, y)

        def half(i, h):
            r = arrs[i].shape[0] // 2
            if not halved[i]:
                return srcs[i]
            return srcs[i].at[pl.ds(h * r if isinstance(h, int) else pl.multiple_of(h * r, 8), r)]

        ici, own, fwd = [], [], []
        for i in range(n):
            mine_h = c if halved[i] else 0
            for k, (px, py) in enumerate(peers):
                s = 3 * i + k
                ici.append(_remote(half(i, c), outs[i].at[me, mine_h], ici_send.at[s], ici_recv.at[s], (px, py, c)))
            for h in range(2 if halved[i] else 1):
                s = 2 * i + h
                own.append(_remote(half(i, h), outs[i].at[me, h], own_send.at[s], own_recv.at[s], sib))
        for cp in ici + own:
            cp.start()
        for i in range(n):
            if not halved[i]:
                continue
            for k, (px, py) in enumerate(peers):
                s = 3 * i + k
                got = outs[i].at[2 * px + py, c]
                _remote(half(i, c), got, ici_send.at[s], ici_recv.at[s], (px, py, c)).wait_recv()
                f = _remote(got, got, fwd_send.at[s], fwd_recv.at[s], sib)
                f.start()
                fwd.append(f)
        for i in range(n):
            for k, (px, py) in enumerate(peers):
                s = 3 * i + k
                if halved[i]:
                    _remote(half(i, c), outs[i].at[2 * px + py, 1 - c], fwd_send.at[s], fwd_recv.at[s], sib).wait_recv()
                else:
                    _remote(srcs[i], outs[i].at[2 * px + py, 0], ici_send.at[s], ici_recv.at[s], (px, py, c)).wait_recv()
            for h in range(2 if halved[i] else 1):
                s = 2 * i + h
                _remote(half(i, h), outs[i].at[me, h], own_send.at[s], own_recv.at[s], sib).wait_recv()
        for cp in ici + own + fwd:
            cp.wait_send()

    outs = pl.pallas_call(
        body, name=name, in_specs=[HBM_SPEC] * n, out_specs=[HBM_SPEC] * n,
        out_shape=[SDS(s, a.dtype) for s, a in zip(oshape, arrs)],
        scratch_shapes=_dma_sems(3 * n, 3 * n, 2 * n, 2 * n, 3 * n, 3 * n), compiler_params=COMM_PARAMS)(*arrs)
    return [o.reshape((4,) + a.shape) for o, a in zip(outs, arrs)]


def _pair_split_list(gs, name):
    n = len(gs)

    def body(*refs):
        sends, recvs = _pair_copies(refs[:n], refs[n:2 * n], *refs[2 * n:])
        for cp in sends:
            cp.start()
        for cp in recvs:
            cp.wait_recv()
        for cp in sends:
            cp.wait_send()

    return pl.pallas_call(
        body, name=name, in_specs=[HBM_SPEC] * n, out_specs=[HBM_SPEC] * n,
        out_shape=[SDS((4,) + g.shape[2:], g.dtype) for g in gs],
        scratch_shapes=_dma_sems(4 * n, 4 * n), compiler_params=COMM_PARAMS)(*gs)


def _gather_finish_list(shards, bufs, name="allgather_finish"):
    n = len(shards)

    def body(*refs):
        srcs, outs = refs[:n], refs[2 * n:3 * n]
        own_send, own_recv, fwd_send, fwd_recv = refs[3 * n:]
        x, y, c = lax.axis_index("x"), lax.axis_index("y"), lax.axis_index("c")
        me = 2 * x + y
        sib = (x, y, 1 - c)
        sends, recvs = [], []
        for i in range(n):
            for h in range(2):
                own = _remote(_rows_half(srcs[i], shards[i].shape[0], h), outs[i].at[me, h],
                              own_send.at[2 * i + h], own_recv.at[2 * i + h], sib)
                sends.append(own)
                recvs.append(own)
            for k, (px, py) in enumerate(_chip_peers(x, y)):
                got, s = outs[i].at[2 * px + py, c], 3 * i + k
                sends.append(_remote(got, got, fwd_send.at[s], fwd_recv.at[s], sib))
                recvs.append(_remote(got, outs[i].at[2 * px + py, 1 - c], fwd_send.at[s], fwd_recv.at[s], sib))
        for cp in sends:
            cp.start()
        for cp in recvs:
            cp.wait_recv()
        for cp in sends:
            cp.wait_send()

    outs = pl.pallas_call(
        body, name=name, in_specs=[HBM_SPEC] * (2 * n), out_specs=[HBM_SPEC] * n,
        out_shape=[SDS(b.shape, b.dtype) for b in bufs], input_output_aliases={n + i: i for i in range(n)},
        scratch_shapes=_dma_sems(2 * n, 2 * n, 3 * n, 3 * n), compiler_params=COMM_PARAMS)(*shards, *bufs)
    return [o.reshape((4,) + a.shape) for o, a in zip(outs, shards)]


JOIN_SPLIT = 4


def _pair_join_list(bufs, name="rs_pair_join"):
    n = len(bufs)

    def body(*refs):
        outs = refs[n:2 * n]
        send_sems, recv_sems = refs[2 * n:]
        x, y, c = lax.axis_index("x"), lax.axis_index("y"), lax.axis_index("c")
        sib = (x, y, 1 - c)
        sends, recvs = [], []
        for i in range(n):
            rc = bufs[i].shape[1] // JOIN_SPLIT
            for q in range(JOIN_SPLIT):
                k = JOIN_SPLIT * i + q
                rows = pl.ds(q * rc, rc)
                sends.append(_remote(outs[i].at[c, rows], outs[i].at[c, rows], send_sems.at[k], recv_sems.at[k], sib))
                recvs.append(_remote(outs[i].at[c, rows], outs[i].at[1 - c, rows], send_sems.at[k], recv_sems.at[k], sib))
        for cp in sends:
            cp.start()
        for cp in recvs:
            cp.wait_recv()
        for cp in sends:
            cp.wait_send()

    return pl.pallas_call(
        body, name=name, in_specs=[HBM_SPEC] * n, out_specs=[HBM_SPEC] * n,
        out_shape=[SDS(b.shape, b.dtype) for b in bufs], input_output_aliases={i: i for i in range(n)},
        scratch_shapes=_dma_sems(JOIN_SPLIT * n, JOIN_SPLIT * n), compiler_params=COMM_PARAMS)(*bufs)


def _pair_sum(g, theirs, core, name):
    _, _, r, c = g.shape

    def body(core_ref, g_ref, t_ref, o_ref):
        o_ref[...] = (g_ref[...] + t_ref[...]).astype(o_ref.dtype)

    spec = pltpu.PrefetchScalarGridSpec(
        num_scalar_prefetch=1, grid=(4,),
        in_specs=[pl.BlockSpec((None, None, r, c), lambda j, core_ref: (j, core_ref[0], 0, 0)),
                  pl.BlockSpec((None, r, c), lambda j, core_ref: (j, 0, 0))],
        out_specs=pl.BlockSpec((None, r, c), lambda j, core_ref: (j, 0, 0)))
    return pl.pallas_call(body, name=name, grid_spec=spec, out_shape=SDS((4, r, c), BF16),
                          compiler_params=_cp("parallel"))(core, g, theirs)


def _chip_sum(own, got, where, name):
    _, r, c = own.shape
    tr = r // 2

    def body(w_ref, a_ref, b1_ref, b2_ref, b3_ref, o_ref):
        o_ref[...] = ((a_ref[...].astype(F32) + b1_ref[...].astype(F32)) + b2_ref[...].astype(F32)) + b3_ref[...].astype(F32)

    piece = lambda k: pl.BlockSpec((None, tr, c), lambda i, w_ref: ((w_ref[0] + k) % 4, i, 0))
    spec = pltpu.PrefetchScalarGridSpec(
        num_scalar_prefetch=1, grid=(r // tr,), in_specs=[piece(0), piece(1), piece(2), piece(3)],
        out_specs=pl.BlockSpec((None, tr, c), lambda i, w_ref: (w_ref[1], i, 0)))
    return pl.pallas_call(body, name=name, grid_spec=spec, out_shape=SDS((2, r, c), F32),
                          compiler_params=_cp("parallel"))(where, own, got, got, got)


ADAM_BLOCK = 2 ** 19


def _adam_math(w, g, m, v):
    bc1 = 1.0 - ADAM_B1 ** ADAM_STEP
    bc2 = 1.0 - ADAM_B2 ** ADAM_STEP
    mn = ADAM_B1 * m + (1.0 - ADAM_B1) * g
    vn = ADAM_B2 * v + (1.0 - ADAM_B2) * (g * g)
    return -ADAM_LR * ((mn / bc1) / (jnp.sqrt(vn / bc2) + ADAM_EPS) + ADAM_WD * w), mn, vn


PACK_COLS = XBC
PACK = {"g_mix": (0, 1, D), "g_xattn": (1, 1, D), "g_mem": (2, 1, D), "g_ffn": (3, 1, D), "g_final": (4, 1, D),
        "ssd_norm_g": (5, 1, D), "cf_b": (6, 1, D), "ln_g": (7, 1, D), "ln_b": (8, 1, D), "conv4_b": (9, 1, XBC),
        "conv4_w": (10, KS, XBC), "sc": (16, 8, 128), "cf_w": (24, KC, D), "loss": (55, 1, 128)}
PACK_ROWS = 56
SMALL_ADAM = ["g_mix", "g_xattn", "g_mem", "g_ffn", "g_final", "ssd_norm_g", "cf_b", "ln_g", "ln_b", "conv4_b", "sc"]


def _small_allreduce_adamw(grads, wts, mom, var, name="allreduce_small"):
    gk = list(PACK)
    ng, na = len(gk), len(SMALL_ADAM)

    def body(*refs):
        g_in = refs[:ng]
        w_in, m_in, v_in = (refs[ng + i * na: ng + (i + 1) * na] for i in range(3))
        o = refs[ng + 3 * na:]
        g_out = o[:ng]
        d_out, m_out, v_out = (o[ng + i * na: ng + (i + 1) * na] for i in range(3))
        pack, pbuf, psum, cbuf, acc, send_sems, recv_sems = o[ng + 3 * na:]
        x, y, c = lax.axis_index("x"), lax.axis_index("y"), lax.axis_index("c")
        me = 2 * x + y
        pack[...] = jnp.zeros_like(pack)
        for i, k in enumerate(gk):
            r0, nr, nc = PACK[k]
            pack[r0:r0 + nr, 0:nc] = g_in[i][...]
        pair = _remote(pack, pbuf.at[c], send_sems.at[0], recv_sems.at[0], (x, y, 1 - c))
        pair.start()
        pbuf[c] = pack[...]
        _remote(pack, pbuf.at[1 - c], send_sems.at[0], recv_sems.at[0], (x, y, 1 - c)).wait_recv()
        pair.wait_send()
        psum[...] = pbuf[0] + pbuf[1]
        peers = _chip_peers(x, y)
        sends = [_remote(psum, cbuf.at[me], send_sems.at[1 + k], recv_sems.at[1 + k], (px, py, c))
                 for k, (px, py) in enumerate(peers)]
        for cp in sends:
            cp.start()
        cbuf[me] = psum[...]
        for k, (px, py) in enumerate(peers):
            _remote(psum, cbuf.at[2 * px + py], send_sems.at[1 + k], recv_sems.at[1 + k], (px, py, c)).wait_recv()
        for cp in sends:
            cp.wait_send()
        acc[...] = (cbuf[0] + cbuf[1]) + (cbuf[2] + cbuf[3])
        for i, k in enumerate(gk):
            r0, nr, nc = PACK[k]
            g_out[i][...] = acc[r0:r0 + nr, 0:nc]
        for i, k in enumerate(SMALL_ADAM):
            r0, nr, nc = PACK[k]
            d_out[i][...], m_out[i][...], v_out[i][...] = _adam_math(
                w_in[i][...], acc[r0:r0 + nr, 0:nc], m_in[i][...], v_in[i][...])

    args = [grads[k] for k in gk] + [d[k] for d in (wts, mom, var) for k in SMALL_ADAM]
    shp = lambda k: SDS((PACK[k][1], PACK[k][2]), F32)
    vm = pl.BlockSpec(memory_space=pltpu.VMEM)
    outs = pl.pallas_call(
        body, name=name, in_specs=[vm] * len(args), out_specs=[vm] * (ng + 3 * na),
        out_shape=[shp(k) for k in gk] + [shp(k) for _ in range(3) for k in SMALL_ADAM],
        scratch_shapes=[pltpu.VMEM((PACK_ROWS, PACK_COLS), F32), pltpu.VMEM((2, PACK_ROWS, PACK_COLS), F32),
                        pltpu.VMEM((PACK_ROWS, PACK_COLS), F32), pltpu.VMEM((4, PACK_ROWS, PACK_COLS), F32),
                        pltpu.VMEM((PACK_ROWS, PACK_COLS), F32)] + _dma_sems(4, 4),
        compiler_params=COMM_PARAMS)(*args)
    red = dict(zip(gk, outs[:ng]))
    parts = [dict(zip(SMALL_ADAM, outs[ng + i * na: ng + (i + 1) * na])) for i in range(3)]
    return red, parts[0], parts[1], parts[2]


def _adamw_cols(w, gfull, m, v, chip, name):
    _, R, C = w.shape

    def body(w_idx, w_ref, g_ref, m_ref, v_ref, go_ref, d_ref, mo_ref, vo_ref):
        go_ref[...] = g_ref[...]
        d_ref[...], mo_ref[...], vo_ref[...] = _adam_math(w_ref[...], g_ref[...], m_ref[...], v_ref[...])

    blk = pl.BlockSpec((None, R, C), lambda i, w_idx: (0, 0, 0))
    spec = pltpu.PrefetchScalarGridSpec(
        num_scalar_prefetch=1, grid=(1,),
        in_specs=[blk, pl.BlockSpec((R, C), lambda i, w_idx: (0, w_idx[0])), blk, blk], out_specs=[blk] * 4)
    return pl.pallas_call(body, name=name, grid_spec=spec, out_shape=[SDS((1, R, C), F32)] * 4,
                          compiler_params=_cp("arbitrary"))(chip, w, gfull, m, v)


def _adamw(w, g, m, v, name):
    _, R, C = w.shape
    half = R // 2
    tr = _tile(half, max(8, (ADAM_BLOCK // C) // 8 * 8), 8)
    nh = half // tr

    def body(w_ref, g_ref, m_ref, v_ref, go_ref, d_ref, mo_ref, vo_ref):
        go_ref[...] = g_ref[...]
        d_ref[...], mo_ref[...], vo_ref[...] = _adam_math(w_ref[...], g_ref[...], m_ref[...], v_ref[...])

    blk = pl.BlockSpec((None, tr, C), lambda i: (0, i, 0))
    gblk = pl.BlockSpec((None, tr, C), lambda i: (i // nh, i % nh, 0))
    return pl.pallas_call(body, name=name, grid=(R // tr,), in_specs=[blk, gblk, blk, blk], out_specs=[blk] * 4,
                          out_shape=[SDS((1, R, C), F32)] * 4, compiler_params=_cp("parallel"))(w, g, m, v)


WEIGHT_NAMES = ["norm_mix_g", "w_in", "ssd_conv_w", "ssd_conv_b", "ssd_dt_bias", "ssd_A_log", "ssd_D", "ssd_norm_g",
                "cf_conv_w", "cf_conv_b", "cf_ln_g", "cf_ln_b", "w_out", "norm_xattn_g", "norm_mem_g", "w_q", "w_kv",
                "w_o", "norm_ffn_g", "w_gate", "w_up", "w_down", "norm_final_g"]
VEC_REF = [("norm_mix_g", "g_mix"), ("norm_xattn_g", "g_xattn"), ("norm_mem_g", "g_mem"), ("norm_ffn_g", "g_ffn"),
           ("norm_final_g", "g_final"), ("ssd_norm_g", "ssd_norm_g"), ("cf_conv_b", "cf_b"), ("cf_ln_g", "ln_g"),
           ("cf_ln_b", "ln_b"), ("ssd_conv_b", "conv4_b")]
SC_REF = ["ssd_dt_bias", "ssd_A_log", "ssd_D"]


def _small_side(get):
    d = {k: get(ref_name).reshape(1, -1) for ref_name, k in VEC_REF}
    d["sc"] = _stack_sc(*[get(n) for n in SC_REF])
    return d


def kernel(x, mem, norm_mix_g, w_in, ssd_conv_w, ssd_conv_b, ssd_dt_bias, ssd_A_log, ssd_D, ssd_norm_g, cf_conv_w, cf_conv_b, cf_ln_g, cf_ln_b, w_out, norm_xattn_g, norm_mem_g, w_q, w_kv, w_o, norm_ffn_g, w_gate, w_up, w_down, norm_final_g, loss_target, m_norm_mix_g, m_w_in, m_ssd_conv_w, m_ssd_conv_b, m_ssd_dt_bias, m_ssd_A_log, m_ssd_D, m_ssd_norm_g, m_cf_conv_w, m_cf_conv_b, m_cf_ln_g, m_cf_ln_b, m_w_out, m_norm_xattn_g, m_norm_mem_g, m_w_q, m_w_kv, m_w_o, m_norm_ffn_g, m_w_gate, m_w_up, m_w_down, m_norm_final_g, v_norm_mix_g, v_w_in, v_ssd_conv_w, v_ssd_conv_b, v_ssd_dt_bias, v_ssd_A_log, v_ssd_D, v_ssd_norm_g, v_cf_conv_w, v_cf_conv_b, v_cf_ln_g, v_cf_ln_b, v_w_out, v_norm_xattn_g, v_norm_mem_g, v_w_q, v_w_kv, v_w_o, v_norm_ffn_g, v_w_gate, v_w_up, v_w_down, v_norm_final_g):
    env = dict(locals())
    view = lambda n, a: a.transpose(0, 2, 1) if n in TRANSPOSED else a
    wts = {n: view(n, env[n]) for n in WEIGHT_NAMES}
    mom = {n: view(n, env["m_" + n]) for n in WEIGHT_NAMES}
    var = {n: view(n, env["v_" + n]) for n in WEIGHT_NAMES}
    chip = (2 * lax.axis_index("x") + lax.axis_index("y")).astype(jnp.int32).reshape(1)
    core = lax.axis_index("c").astype(jnp.int32).reshape(1)
    where = jnp.concatenate([chip, core])
    big = [n for n, _ in BIG]

    w_in_g, conv4_g, cf_g = _allgather_list([w_in[0].astype(BF16), ssd_conv_w[0], cf_conv_w[0]], "allgather_first")
    W = _pack_in(w_in_g)
    P = _small_side(lambda n: wts[n])
    P["conv4_w"], P["cf_w"] = _cat_cols(conv4_g), _cat_cols(cf_g)
    late = {n: wts[n][0].astype(BF16) for grp in AG_RIDE for n in grp}

    loss, grad_x, GW, GP, pair, got = _local_step(x[0], mem[0], loss_target[0], W, P, core, late)
    joined = _pair_join_list([_chip_sum(pair[n], got[n], where, "rs_chip_sum_" + n) for n in big])
    gshard = dict(zip(big, joined))

    small = dict(GP)
    small["loss"] = loss
    red, sd, sm, sv = _small_allreduce_adamw(small, {k: P[k] for k in SMALL_ADAM}, _small_side(lambda n: mom[n]),
                                             _small_side(lambda n: var[n]))
    grads, delta, new_m, new_v = {}, {}, {}, {}
    for ref_name, k in VEC_REF:
        shp = wts[ref_name].shape
        for dst, src in ((grads, red), (delta, sd), (new_m, sm), (new_v, sv)):
            dst[ref_name] = src[k].reshape(shp)
    for row, ref_name in enumerate(SC_REF):
        for dst, src in ((grads, red), (delta, sd), (new_m, sm), (new_v, sv)):
            dst[ref_name] = src["sc"][row:row + 1, :NH]

    for n, k in (("ssd_conv_w", "conv4_w"), ("cf_conv_w", "cf_w")):
        grads[n], delta[n], new_m[n], new_v[n] = _adamw_cols(wts[n], red[k], mom[n], var[n], chip, "adamw_" + n)
    for n in big:
        outs = _adamw(wts[n], gshard[n], mom[n], var[n], "adamw_" + n)
        grads[n], delta[n], new_m[n], new_v[n] = [view(n, o) for o in outs]

    return (red["loss"][0, 0], grad_x[None], *[grads[n] for n in WEIGHT_NAMES], *[delta[n] for n in WEIGHT_NAMES],
            *[new_m[n] for n in WEIGHT_NAMES], *[new_v[n] for n in WEIGHT_NAMES])
```

```python
import functools
import math

import jax
import jax.numpy as jnp
from jax import lax
from jax.experimental import pallas as pl
from jax.experimental.pallas import tpu as pltpu

F32 = jnp.float32
BF16 = jnp.bfloat16
_MXU = BF16

D = 1024
MEM = 256
NH, HP, NG, NS = 16, 64, 2, 128
GW = NH * HP // NG
CH = 128
XBC = NH * HP + 2 * NG * NS
KS, KC = 4, 31
XH, XD = 4, 256
DFF = 2816
EPS = 1e-6
COL_Z, COL_A, COL_G, COL_XBC, COL_DT, MAINW = 0, 1024, 2048, 3072, 4608, 4736
VMEM_LIMIT = 56 * 2 ** 20

ADAM_LR, ADAM_B1, ADAM_B2, ADAM_EPS, ADAM_WD, ADAM_STEP = 0.001, 0.9, 0.999, 1e-08, 0.01, 10

SDS = jax.ShapeDtypeStruct
MESHID = pl.DeviceIdType.MESH


def _cp(*sem):
    return pltpu.CompilerParams(dimension_semantics=sem, vmem_limit_bytes=VMEM_LIMIT)


def _tile(n, cap, unit=128):
    if n <= cap:
        return n
    best = None
    for t in range(unit, cap + 1, unit):
        if n % t == 0:
            best = t
    assert best is not None, (n, cap)
    return best


def _sigmoid(x):
    return 1.0 / (1.0 + jnp.exp(-x))


def _silu(x):
    return x * _sigmoid(x)


def _dsilu(x):
    s = _sigmoid(x)
    return s * (1.0 + x * (1.0 - s))


def _softplus(x):
    return jnp.maximum(x, 0.0) + jnp.log(1.0 + jnp.exp(-jnp.abs(x)))


def _split_bf16(x, passes):
    parts, r = [], x.astype(F32)
    for _ in range(passes):
        p = r.astype(BF16)
        parts.append(p)
        r = r - p.astype(F32)
    return parts


def _dot(a, b, dims=None, exact=None, passes=2):
    dn = {None: (((1,), (0,)), ((), ())), "nt": (((1,), (1,)), ((), ())), "tn": (((0,), (0,)), ((), ()))}[dims]
    if exact is None:
        return lax.dot_general(a.astype(_MXU), b.astype(_MXU), dn, preferred_element_type=F32)
    if exact == "a":
        terms = [(a.astype(BF16), p) for p in _split_bf16(b, passes)]
    else:
        terms = [(p, b.astype(BF16)) for p in _split_bf16(a, passes)]
    out = None
    for lhs, rhs in terms:
        d = lax.dot_general(lhs, rhs, dn, preferred_element_type=F32)
        out = d if out is None else out + d
    return out


def _rms_bwd_tile(xv, gv, dy, dres):
    r = lax.rsqrt(jnp.mean(xv * xv, axis=-1, keepdims=True) + EPS)
    xh = xv * r
    gdy = dy * gv
    dx = r * (gdy - xh * jnp.mean(xh * gdy, axis=-1, keepdims=True))
    return dres + dx, jnp.sum(dy * xh, axis=0, keepdims=True)


def _matmul(kind, a, b, name, add, out_dtype, tm, tw, riders, rms, norm=None, loss=None):
    a_parts = list(a) if isinstance(a, (list, tuple)) else [a]
    na = len(a_parts)
    M, K = a_parts[0].shape[0], sum(p.shape[1] for p in a_parts)
    Wd = b.shape[1] if kind == "nn" else b.shape[0]
    rd = _Riders(riders or ())
    nco = len(rd.arrays())
    nin = na + 1 + (add is not None) + (3 if rms else 0) + (norm is not None) + (2 if loss else 0)
    low = bool(rms and rms[3])
    nout = (2 + low) if rms else 2 if norm is not None else 4 if loss else 1
    grid = (Wd // tw, M // tm)
    assert not (rms or loss or norm is not None) or tw == Wd, "the row-wise epilogues need whole rows"

    def body(*refs):
        b_ref = refs[na]
        av = refs[0][...] if na == 1 else jnp.concatenate([r[...] for r in refs[:na]], axis=1)
        outs = refs[nin + nco:nin + nco + nout]
        rd.bind(refs[nin:nin + nco], refs[nin + nco + nout:nin + 2 * nco + nout], refs[nin + 2 * nco + nout:], grid).start()
        acc = _dot(av, b_ref[...], None if kind == "nn" else "nt")
        if add is not None:
            acc = acc + refs[na + 1][...]
        if loss:
            lpart, dx, dg = _final_loss_tile(acc, refs[nin - 2][...], refs[nin - 1][...])

            @pl.when(pl.program_id(1) == 0)
            def _():
                outs[0][...] = jnp.zeros_like(outs[0])
                outs[3][...] = jnp.zeros_like(outs[3])

            outs[0][...] += lpart
            outs[1][...] = dx
            outs[2][...] = dx.astype(outs[2].dtype)
            outs[3][...] += dg
        elif norm is not None:
            outs[0][...] = acc.astype(outs[0].dtype)
            r = lax.rsqrt(jnp.mean(acc * acc, axis=-1, keepdims=True) + EPS)
            outs[1][...] = (acc * r * refs[nin - 1][...]).astype(outs[1].dtype)
        elif rms:
            x_ref, g_ref, dres_ref = refs[nin - 3:nin]
            tot, dg = _rms_bwd_tile(x_ref[...], g_ref[...], acc, dres_ref[...])

            @pl.when(pl.program_id(1) == 0)
            def _():
                outs[-1][...] = jnp.zeros_like(outs[-1])

            outs[-1][...] += dg
            outs[0][...] = tot
            if low:
                outs[1][...] = tot.astype(outs[1].dtype)
        else:
            outs[0][...] = acc.astype(outs[0].dtype)
        rd.finish()

    tile = pl.BlockSpec((tm, tw), lambda j, i: (i, j))
    bspec = pl.BlockSpec((K, tw), lambda j, i: (0, j)) if kind == "nn" else pl.BlockSpec((tw, K), lambda j, i: (j, 0))
    in_specs = [pl.BlockSpec((tm, p.shape[1]), lambda j, i: (i, 0)) for p in a_parts] + [bspec]
    args = a_parts + [b]
    if add is not None:
        in_specs.append(tile)
        args.append(add)
    vec = pl.BlockSpec((1, tw), lambda j, i: (0, j))
    if rms:
        in_specs += [tile, vec, tile]
        args += [rms[0], rms[1], rms[2]]
        out_specs = [tile] * (1 + low) + [vec]
        out_shape = [SDS((M, Wd), F32)] + ([SDS((M, Wd), _MXU)] if low else []) + [SDS((1, Wd), F32)]
    elif loss:
        in_specs += [vec, tile]
        args += [loss[0], loss[1]]
        out_specs = [pl.BlockSpec((1, 128), lambda j, i: (0, 0)), tile, tile, vec]
        out_shape = [SDS((1, 128), F32), SDS((M, Wd), F32), SDS((M, Wd), _MXU), SDS((1, Wd), F32)]
    elif norm is not None:
        in_specs.append(vec)
        args.append(norm)
        out_specs, out_shape = [tile, tile], [SDS((M, Wd), out_dtype), SDS((M, Wd), _MXU)]
    else:
        out_specs, out_shape = [tile], [SDS((M, Wd), out_dtype)]
    order = ("arbitrary", "arbitrary") if (nco or rms or loss) else ("parallel", "parallel")
    outs = pl.pallas_call(
        body, name=name, grid=grid, in_specs=in_specs + [HBM_SPEC] * nco, out_specs=out_specs + [HBM_SPEC] * nco,
        out_shape=out_shape + rd.out_shapes(), scratch_shapes=rd.scratch(),
        compiler_params=_cp(*order))(*args, *rd.arrays())
    main = tuple(outs[:nout]) if nout > 1 else outs[0]
    return main if riders is None else (main, rd.split(outs[nout:]))


def _mm_nn(a, b, name, add=None, out_dtype=F32, tm_cap=1024, tn_cap=1408, riders=None, rms=None, norm=None, loss=None):
    tm, tn = _tile(a.shape[0], tm_cap, 8), _tile(b.shape[1], tn_cap)
    return _matmul("nn", a, b, name, add, out_dtype, tm, tn, riders, rms, norm, loss)


def _mm_nt(a, b, name, add=None, out_dtype=F32, tm_cap=512, tk_cap=1024, riders=None, rms=None):
    rows = (a[0] if isinstance(a, (list, tuple)) else a).shape[0]
    tm, tk = _tile(rows, tm_cap, 8), _tile(b.shape[0], tk_cap)
    return _matmul("nt", a, b, name, add, out_dtype, tm, tk, riders, rms)


def _mm_tn(a, b, name, tm_cap=1024, tk_cap=512, tn_cap=1408):
    b_parts = list(b) if isinstance(b, (list, tuple)) else [b]
    nb = len(b_parts)
    M, K = a.shape
    N = sum(p.shape[1] for p in b_parts)
    tm, tk, tn = _tile(M, tm_cap, 8), _tile(K, tk_cap), _tile(N, tn_cap)
    assert nb == 1 or tn == N

    def body(a_ref, *rest):
        o_ref = rest[nb]
        bv = rest[0][...] if nb == 1 else jnp.concatenate([r[...] for r in rest[:nb]], axis=1)

        @pl.when(pl.program_id(2) == 0)
        def _():
            o_ref[...] = jnp.zeros_like(o_ref)

        o_ref[...] += _dot(a_ref[...], bv, "tn")

    b_specs = ([pl.BlockSpec((tm, tn), lambda k, n, m: (m, n))] if nb == 1 else
               [pl.BlockSpec((tm, p.shape[1]), lambda k, n, m: (m, 0)) for p in b_parts])
    return pl.pallas_call(
        body, name=name, grid=(K // tk, N // tn, M // tm),
        in_specs=[pl.BlockSpec((tm, tk), lambda k, n, m: (m, k))] + b_specs,
        out_specs=pl.BlockSpec((tk, tn), lambda k, n, m: (k, n)), out_shape=SDS((K, N), F32),
        compiler_params=_cp("parallel", "parallel", "arbitrary"))(a, *b_parts)


def _rms_fwd(x, g, name, tb_cap=512):
    S, Dm = x.shape
    tb = _tile(S, tb_cap, 8)

    def body(x_ref, g_ref, o_ref):
        xv = x_ref[...]
        r = lax.rsqrt(jnp.mean(xv * xv, axis=-1, keepdims=True) + EPS)
        o_ref[...] = (xv * r * g_ref[...]).astype(o_ref.dtype)

    return pl.pallas_call(
        body, name=name, grid=(S // tb,),
        in_specs=[pl.BlockSpec((tb, Dm), lambda i: (i, 0)), pl.BlockSpec((1, Dm), lambda i: (0, 0))],
        out_specs=pl.BlockSpec((tb, Dm), lambda i: (i, 0)), out_shape=SDS((S, Dm), _MXU),
        compiler_params=_cp("parallel"))(x, g)


def _rms_bwd(x, g, dh, dres, name, tb_cap=512, low=True):
    S, Dm = x.shape
    tb = _tile(S, tb_cap, 8)
    need_dx = dres is not None

    def body(x_ref, g_ref, dh_ref, *rest):
        dg_ref = rest[-1]
        tot, dg = _rms_bwd_tile(x_ref[...], g_ref[...], dh_ref[...].astype(F32), rest[0][...] if need_dx else 0.0)

        @pl.when(pl.program_id(0) == 0)
        def _():
            dg_ref[...] = jnp.zeros_like(dg_ref)

        dg_ref[...] += dg
        if need_dx:
            rest[1][...] = tot
            if low:
                rest[2][...] = tot.astype(rest[2].dtype)

    row = pl.BlockSpec((tb, Dm), lambda i: (i, 0))
    vec = pl.BlockSpec((1, Dm), lambda i: (0, 0))
    if need_dx:
        outs = [SDS((S, Dm), F32)] + ([SDS((S, Dm), _MXU)] if low else [])
        return pl.pallas_call(
            body, name=name, grid=(S // tb,), in_specs=[row, vec, row, row], out_specs=[row] * len(outs) + [vec],
            out_shape=outs + [SDS((1, Dm), F32)], compiler_params=_cp("arbitrary"))(x, g, dh, dres)
    return pl.pallas_call(
        body, name=name, grid=(S // tb,), in_specs=[row, vec, row], out_specs=vec,
        out_shape=SDS((1, Dm), F32), compiler_params=_cp("arbitrary"))(x, g, dh)


def _final_loss_tile(xv, gv, tv):
    r = lax.rsqrt(jnp.mean(xv * xv, axis=-1, keepdims=True) + EPS)
    xh = xv * r
    e = xh * gv - tv
    dy = e * (1.0 / xv.shape[-1])
    gdy = dy * gv
    dx = r * (gdy - xh * jnp.mean(xh * gdy, axis=-1, keepdims=True))
    return 0.5 * jnp.sum(jnp.mean(e * e, axis=-1, keepdims=True)), dx, jnp.sum(dy * xh, axis=0, keepdims=True)


SSD_HALO = 8
CF_HALO = 32

HBM_SPEC = pl.BlockSpec(memory_space=pl.ANY)


def _chip_peers(x, y):
    return [(1 - x, y), (x, 1 - y), (1 - x, 1 - y)]


def _remote(src, dst, send_sem, recv_sem, dev):
    return pltpu.make_async_remote_copy(src_ref=src, dst_ref=dst, send_sem=send_sem, recv_sem=recv_sem,
                                        device_id=dev, device_id_type=MESHID)


def _scatter_copies(srcs, outs, send_sems, recv_sems):
    x, y, c = lax.axis_index("x"), lax.axis_index("y"), lax.axis_index("c")
    me = 2 * x + y
    sends, recvs = [], []
    for i, (s, o) in enumerate(zip(srcs, outs)):
        for k, (px, py) in enumerate(_chip_peers(x, y)):
            j = 3 * i + k
            sends.append(_remote(s.at[2 * px + py], o.at[me], send_sems.at[j], recv_sems.at[j], (px, py, c)))
            recvs.append(_remote(s.at[me], o.at[2 * px + py], send_sems.at[j], recv_sems.at[j], (px, py, c)))
    return sends, recvs


def _pair_copies(srcs, outs, send_sems, recv_sems):
    x, y, c = lax.axis_index("x"), lax.axis_index("y"), lax.axis_index("c")
    sends = [_remote(s.at[j, 1 - c], o.at[j], send_sems.at[4 * i + j], recv_sems.at[4 * i + j], (x, y, 1 - c))
             for i, (s, o) in enumerate(zip(srcs, outs)) for j in range(4)]
    return sends, sends


def _rows_half(ref, rows, h):
    r = rows // 2
    return ref.at[pl.ds(h * r if isinstance(h, int) else pl.multiple_of(h * r, 8), r)]


def _gather_copies(srcs, outs, rows, send_sems, recv_sems):
    x, y, c = lax.axis_index("x"), lax.axis_index("y"), lax.axis_index("c")
    me = 2 * x + y
    sends, recvs = [], []
    for i, (s, o) in enumerate(zip(srcs, outs)):
        mine = _rows_half(s, rows[i], c)
        for k, (px, py) in enumerate(_chip_peers(x, y)):
            j = 3 * i + k
            sends.append(_remote(mine, o.at[me, c], send_sems.at[j], recv_sems.at[j], (px, py, c)))
            recvs.append(_remote(mine, o.at[2 * px + py, c], send_sems.at[j], recv_sems.at[j], (px, py, c)))
    return sends, recvs


def _gather_shapes(shards):
    return [SDS((4, 2, a.shape[0] // 2, a.shape[1]), a.dtype) for a in shards]


class _Rider:
    SEMS_PER_ARRAY = {"exchange": 3, "gather": 3, "pair": 4}

    def __init__(self, kind, arrays):
        self.kind, self.arrays = kind, list(arrays)

    def out_shapes(self):
        if self.kind == "gather":
            return _gather_shapes(self.arrays)
        if self.kind == "pair":
            return [SDS((4,) + a.shape[2:], a.dtype) for a in self.arrays]
        return [SDS(a.shape, a.dtype) for a in self.arrays]

    def scratch(self):
        n = self.SEMS_PER_ARRAY[self.kind] * len(self.arrays)
        return [pltpu.SemaphoreType.DMA((n,)), pltpu.SemaphoreType.DMA((n,))]

    def copies(self, srcs, outs, send_sems, recv_sems):
        if self.kind == "gather":
            return _gather_copies(srcs, outs, [a.shape[0] for a in self.arrays], send_sems, recv_sems)
        if self.kind == "pair":
            return _pair_copies(srcs, outs, send_sems, recv_sems)
        return _scatter_copies(srcs, outs, send_sems, recv_sems)


class _Riders:
    def __init__(self, riders):
        self.given = list(riders)
        self.riders = [r for r in self.given if r.arrays]

    def arrays(self):
        return [a for r in self.riders for a in r.arrays]

    def out_shapes(self):
        return [s for r in self.riders for s in r.out_shapes()]

    def scratch(self):
        return [s for r in self.riders for s in r.scratch()]

    def split(self, outs):
        res, k = [], 0
        for r in self.given:
            res.append(list(outs[k:k + len(r.arrays)]))
            k += len(r.arrays)
        return res

    def bind(self, in_refs, out_refs, sem_refs, steps):
        self.steps = steps if isinstance(steps, tuple) else (steps,)
        self.bound, k = [], 0
        for i, r in enumerate(self.riders):
            n = len(r.arrays)
            self.bound.append((r, in_refs[k:k + n], out_refs[k:k + n], sem_refs[2 * i], sem_refs[2 * i + 1]))
            k += n
        return self

    def _at(self, last):
        hit = None
        for ax, n in enumerate(self.steps):
            here = pl.program_id(ax) == (n - 1 if last else 0)
            hit = here if hit is None else jnp.logical_and(hit, here)
        return hit

    def _copies(self):
        sends, recvs = [], []
        for r, srcs, outs, send_sems, recv_sems in self.bound:
            s, w = r.copies(srcs, outs, send_sems, recv_sems)
            sends += s
            recvs += w
        return sends, recvs

    def start(self):
        if self.riders:
            @pl.when(self._at(last=False))
            def _():
                for cp in self._copies()[0]:
                    cp.start()

    def finish(self):
        if self.riders:
            @pl.when(self._at(last=True))
            def _():
                sends, recvs = self._copies()
                for cp in recvs:
                    cp.wait_recv()
                for cp in sends:
                    cp.wait_send()


def _head_mats():
    e = lax.broadcasted_iota(jnp.int32, (128, NH * HP), 1) // HP == lax.broadcasted_iota(jnp.int32, (128, NH * HP), 0)
    return e.astype(BF16), e.T.astype(F32)


def _tri_masks():
    r = lax.broadcasted_iota(jnp.int32, (CH, CH), 0)
    c = lax.broadcasted_iota(jnp.int32, (CH, CH), 1)
    return (c <= r), (r <= c)


def _ssd_common(xbc_c, dtr, dtb, alog, e, tril, triu):
    xbc = _silu(xbc_c)
    xs = xbc[:, :NH * HP]
    dt = _softplus(dtr + dtb)
    A = -jnp.exp(alog)
    a = dt * A
    cs = _dot(tril, a, exact="a", passes=3)
    csT = _dot(a, triu, "tn", exact="b", passes=3)
    csL = cs[CH - 1:CH, :]
    wdec = jnp.exp(csL - cs) * dt
    dtE = _dot(dt, e, exact="b")
    ecsE = _dot(jnp.exp(cs), e, exact="b")
    wE = _dot(wdec, e, exact="b")
    eL = jnp.exp(csL)
    return xbc, xs, dt, A, cs, csT, csL, wdec, dtE, ecsE, wE, eL


def _ssd_fwd(proj, cw, cb, sc, norm_g, riders=(), name="ssd_fwd"):
    S = proj.shape[0]
    nc = S // CH
    rd = _Riders(riders)
    nco = len(rd.arrays())

    def body(*refs):
        z_ref, xp_ref, cw_ref, cb_ref, dtr_ref, sc_ref, ng_ref, e_ref, et_ref = refs[:9]
        xc_ref, y_ref, yn_ref, hp_ref = refs[9 + nco:13 + nco]
        hst, cext = refs[13 + 2 * nco:15 + 2 * nco]
        rd.bind(refs[9:9 + nco], refs[13 + nco:13 + 2 * nco], refs[15 + 2 * nco:], nc).start()

        @pl.when(pl.program_id(0) == 0)
        def _():
            hst[...] = jnp.zeros_like(hst)
            cext[pl.ds(0, SSD_HALO), :] = jnp.zeros((SSD_HALO, XBC), F32)

        cext[pl.ds(SSD_HALO, CH), :] = xp_ref[...]
        xc = jnp.zeros((CH, XBC), F32) + cb_ref[...]
        for k in range(KS):
            xc = xc + cext[pl.ds(SSD_HALO - (KS - 1) + k, CH), :] * cw_ref[k:k + 1, :]
        xc_ref[...] = xc
        cext[pl.ds(0, SSD_HALO), :] = cext[pl.ds(CH, SSD_HALO), :]

        e, et = e_ref[...], et_ref[...]
        tril, triu = _tri_masks()
        xbc, xs, dt, A, cs, csT, csL, wdec, dtE, ecsE, wE, eL = _ssd_common(
            xc, dtr_ref[...], sc_ref[0:1, :], sc_ref[1:2, :], e, tril, triu)
        hp_ref[0] = hst[...]
        xd = xs * dtE
        xw = xs * wE
        dE = _dot(jnp.broadcast_to(sc_ref[2:3, :], (8, 128)), e, exact="b", passes=3)[0:1, :]
        eLcol = jnp.sum(et * eL, axis=1, keepdims=True)
        for g in range(NG):
            Bg = xbc[:, NH * HP + g * NS: NH * HP + (g + 1) * NS]
            Cg = xbc[:, NH * HP + NG * NS + g * NS: NH * HP + NG * NS + (g + 1) * NS]
            gs = slice(g * GW, (g + 1) * GW)
            G = _dot(Cg, Bg, "nt")
            hg = hst[gs, :]
            yoff = ecsE[:, gs] * _dot(Cg, hg, "nt")
            hst[gs, :] = eLcol[gs, :] * hg + _dot(xw[:, gs], Bg, "tn")
            for hh in range(NH // NG):
                h = g * (NH // NG) + hh
                hs = slice(h * HP, (h + 1) * HP)
                m = jnp.where(tril, jnp.exp(jnp.where(tril, cs[:, h:h + 1] - csT[h:h + 1, :], 0.0)), 0.0)
                yd = _dot(G * m, xd[:, hs])
                y_ref[:, hs] = yd + yoff[:, hh * HP:(hh + 1) * HP] + dE[:, hs] * xs[:, hs]
        y = y_ref[...]
        yz = y * _silu(z_ref[...])
        for g in range(NG):
            gs = slice(g * GW, (g + 1) * GW)
            yg = yz[:, gs]
            r = lax.rsqrt(jnp.mean(yg * yg, axis=-1, keepdims=True) + EPS)
            yn_ref[:, gs] = (yg * r * ng_ref[:, gs]).astype(yn_ref.dtype)
        rd.finish()

    outs = pl.pallas_call(
        body, name=name, grid=(nc,),
        in_specs=[pl.BlockSpec((CH, D), lambda c: (c, COL_Z // D)),
                  pl.BlockSpec((CH, XBC), lambda c: (c, COL_XBC // XBC)),
                  pl.BlockSpec((KS, XBC), lambda c: (0, 0)),
                  pl.BlockSpec((1, XBC), lambda c: (0, 0)),
                  pl.BlockSpec((CH, 128), lambda c: (c, COL_DT // 128)),
                  pl.BlockSpec((8, 128), lambda c: (0, 0)),
                  pl.BlockSpec((1, D), lambda c: (0, 0)),
                  pl.BlockSpec((128, NH * HP), lambda c: (0, 0)),
                  pl.BlockSpec((NH * HP, 128), lambda c: (0, 0))] + [HBM_SPEC] * nco,
        out_specs=[pl.BlockSpec((CH, XBC), lambda c: (c, 0)), pl.BlockSpec((CH, D), lambda c: (c, 0)),
                   pl.BlockSpec((CH, D), lambda c: (c, 0)),
                   pl.BlockSpec((1, NH * HP, NS), lambda c: (c, 0, 0))] + [HBM_SPEC] * nco,
        out_shape=[SDS((S, XBC), F32), SDS((S, D), F32), SDS((S, D), _MXU), SDS((nc, NH * HP, NS), F32)]
        + rd.out_shapes(),
        scratch_shapes=[pltpu.VMEM((NH * HP, NS), F32), pltpu.VMEM((SSD_HALO + CH, XBC), F32)] + rd.scratch(),
        compiler_params=_cp("arbitrary"))(proj, proj, cw, cb, proj, sc, norm_g, *_head_mats(), *rd.arrays())
    return outs[:4], rd.split(outs[4:])


def _ssd_bwd(dmix, y, proj, xbc_c, hprev, cw, sc, norm_g, riders=(), name="ssd_bwd"):
    S = proj.shape[0]
    nc = S // CH
    rd = _Riders(riders)
    nco = len(rd.arrays())
    rev = lambda c: nc - 1 - c

    def body(*refs):
        dyn_ref, y_ref, z_ref, x_ref, xp_ref, dtr_ref, hp_ref, cw_ref, sc_ref, ng_ref, e_ref, et_ref = refs[:12]
        dz_ref, dx_ref, ddtr_ref, gcw_ref, gcb_ref, gsc_ref, gng_ref = refs[12 + nco:19 + nco]
        dh, dxd, cext = refs[19 + 2 * nco:22 + 2 * nco]
        rd.bind(refs[12:12 + nco], refs[19 + nco:19 + 2 * nco], refs[22 + 2 * nco:], nc).start()

        @pl.when(pl.program_id(0) == 0)
        def _():
            dh[...] = jnp.zeros_like(dh)
            cext[pl.ds(CH, SSD_HALO), :] = jnp.zeros((SSD_HALO, XBC), F32)
            gcw_ref[...] = jnp.zeros_like(gcw_ref)
            gcb_ref[...] = jnp.zeros_like(gcb_ref)
            gsc_ref[...] = jnp.zeros_like(gsc_ref)
            gng_ref[...] = jnp.zeros_like(gng_ref)

        e, et = e_ref[...], et_ref[...]
        tril, triu = _tri_masks()
        xbc_c = x_ref[...]
        dtr = dtr_ref[...]
        dtb = sc_ref[0:1, :]
        xbc, xs, dt, A, cs, csT, csL, wdec, dtE, ecsE, wE, eL = _ssd_common(
            xbc_c, dtr, dtb, sc_ref[1:2, :], e, tril, triu)
        xd = xs * dtE
        xw = xs * wE
        dE = _dot(jnp.broadcast_to(sc_ref[2:3, :], (8, 128)), e, exact="b", passes=3)[0:1, :]
        eLcol = jnp.sum(et * eL, axis=1, keepdims=True)

        yv = y_ref[...]
        zv = z_ref[...]
        sz = _silu(zv)
        yz = yv * sz
        dyn = dyn_ref[...]
        dyz_parts = []
        for g in range(NG):
            gs = slice(g * GW, (g + 1) * GW)
            yg = yz[:, gs]
            r = lax.rsqrt(jnp.mean(yg * yg, axis=-1, keepdims=True) + EPS)
            yh = yg * r
            dn = dyn[:, gs]
            gng_ref[:, gs] += jnp.sum(dn * yh, axis=0, keepdims=True)
            gdn = dn * ng_ref[:, gs]
            dyz_parts.append(r * (gdn - yh * jnp.mean(yh * gdn, axis=-1, keepdims=True)))
        dyz = jnp.concatenate(dyz_parts, axis=1)
        dy = dyz * sz
        dz_ref[...] = (dyz * yv * _dsilu(zv)).astype(dz_ref.dtype)

        dxs = dE * dy
        dzo = ecsE * dy
        dcsL = jnp.zeros((1, 128), F32)
        ddt = jnp.zeros((CH, 128), F32)
        qcols = jnp.zeros((CH, 128), F32)
        qrows = jnp.zeros((128, CH), F32)
        lane = lax.broadcasted_iota(jnp.int32, (1, 128), 1)
        sub = lax.broadcasted_iota(jnp.int32, (128, 1), 0)
        dB_parts, dC_parts, yoff_parts, dxw_parts = [], [], [], []
        for g in range(NG):
            Bg = xbc[:, NH * HP + g * NS: NH * HP + (g + 1) * NS]
            Cg = xbc[:, NH * HP + NG * NS + g * NS: NH * HP + NG * NS + (g + 1) * NS]
            gs = slice(g * GW, (g + 1) * GW)
            hg = hp_ref[0, gs, :]
            dhn = dh[gs, :]
            G = _dot(Cg, Bg, "nt")
            yoff_parts.append(ecsE[:, gs] * _dot(Cg, hg, "nt"))
            dC = _dot(dzo[:, gs], hg)
            dhp = _dot(dzo[:, gs], Cg, "tn") + eLcol[gs, :] * dhn
            t1 = jnp.sum(dhn * hg, axis=1, keepdims=True) * eLcol[gs, :]
            dcsL = dcsL + jnp.sum(et[gs, :] * t1, axis=0, keepdims=True)
            dxw_parts.append(_dot(Bg, dhn, "nt"))
            dB = _dot(xw[:, gs], dhn)
            dgsum = jnp.zeros((CH, CH), F32)
            for hh in range(NH // NG):
                h = g * (NH // NG) + hh
                hs = slice(h * HP, (h + 1) * HP)
                m = jnp.where(tril, jnp.exp(jnp.where(tril, cs[:, h:h + 1] - csT[h:h + 1, :], 0.0)), 0.0)
                sc = G * m
                dyh = dy[:, hs]
                dxd[:, hs] = _dot(sc, dyh, "tn")
                dsc = _dot(dyh, xd[:, hs], "nt")
                q = dsc * sc
                qcols = qcols + jnp.where(lane == h, jnp.sum(q, axis=1, keepdims=True), 0.0)
                qrows = qrows + jnp.where(sub == h, jnp.sum(q, axis=0, keepdims=True), 0.0)
                dgsum = dgsum + dsc * m
            dC_parts.append(dC + _dot(dgsum, Bg))
            dB_parts.append(dB + _dot(dgsum, Cg, "tn"))
            dh[gs, :] = dhp
        yoff = jnp.concatenate(yoff_parts, axis=1)
        dxw = jnp.concatenate(dxw_parts, axis=1)
        dxdv = dxd[...]
        per_head = _dot(jnp.concatenate([dy * yoff, dxw * xs, dxdv * xs, dy * xs], axis=0), et, exact="b")
        dcs = qcols - qrows.T + per_head[0:CH]
        dw = per_head[CH:2 * CH]
        gsc_ref[2:3, :] += jnp.sum(per_head[3 * CH:4 * CH], axis=0, keepdims=True)
        dxs = dxs + wE * dxw + dtE * dxdv
        ddt = ddt + dw * jnp.exp(csL - cs) + per_head[2 * CH:3 * CH]
        dcs = dcs - dw * wdec
        dcsL = dcsL + jnp.sum(dw * wdec, axis=0, keepdims=True)
        last = lax.broadcasted_iota(jnp.int32, (CH, 128), 0) == CH - 1
        dcs = dcs + jnp.where(last, dcsL, 0.0)
        da = _dot(triu, dcs, exact="a", passes=3)
        ddt = ddt + da * A
        gsc_ref[1:2, :] += jnp.sum(da * dt, axis=0, keepdims=True) * A
        valid = lax.broadcasted_iota(jnp.int32, (CH, 128), 1) < NH
        ddtr = jnp.where(valid, ddt * _sigmoid(dtr + dtb), 0.0)
        gsc_ref[0:1, :] += jnp.sum(ddtr, axis=0, keepdims=True)
        ddtr_ref[...] = ddtr.astype(ddtr_ref.dtype)
        dxbc = jnp.concatenate([dxs] + dB_parts + dC_parts, axis=1)
        dxc = dxbc * _dsilu(xbc_c)
        cext[pl.ds(0, CH), :] = dxc
        xp = xp_ref[...]
        acc = jnp.zeros((CH, XBC), F32)
        for k in range(KS):
            sh = cext[pl.ds(KS - 1 - k, CH), :]
            acc = acc + sh * cw_ref[k:k + 1, :]
            gcw_ref[k:k + 1, :] += jnp.sum(xp * sh, axis=0, keepdims=True)
        gcb_ref[...] += jnp.sum(dxc, axis=0, keepdims=True)
        dx_ref[...] = acc.astype(dx_ref.dtype)
        cext[pl.ds(CH, SSD_HALO), :] = cext[pl.ds(0, SSD_HALO), :]
        rd.finish()

    vec = pl.BlockSpec((8, 128), lambda c: (0, 0))
    vecd = pl.BlockSpec((1, D), lambda c: (0, 0))
    cwsp = pl.BlockSpec((KS, XBC), lambda c: (0, 0))
    cbsp = pl.BlockSpec((1, XBC), lambda c: (0, 0))
    row = lambda w, j=0: pl.BlockSpec((CH, w), lambda c: (rev(c), j))
    outs = pl.pallas_call(
        body, name=name, grid=(nc,),
        in_specs=[row(D), row(D), row(D, COL_Z // D), row(XBC), row(XBC, COL_XBC // XBC), row(128, COL_DT // 128),
                  pl.BlockSpec((1, NH * HP, NS), lambda c: (rev(c), 0, 0)), cwsp, vec, vecd,
                  pl.BlockSpec((128, NH * HP), lambda c: (0, 0)),
                  pl.BlockSpec((NH * HP, 128), lambda c: (0, 0))] + [HBM_SPEC] * nco,
        out_specs=[row(D), row(XBC), row(128), cwsp, cbsp, vec, vecd] + [HBM_SPEC] * nco,
        out_shape=[SDS((S, D), _MXU), SDS((S, XBC), _MXU), SDS((S, 128), _MXU), SDS((KS, XBC), F32),
                   SDS((1, XBC), F32), SDS((8, 128), F32), SDS((1, D), F32)] + rd.out_shapes(),
        scratch_shapes=[pltpu.VMEM((NH * HP, NS), F32), pltpu.VMEM((CH, NH * HP), F32),
                        pltpu.VMEM((CH + SSD_HALO, XBC), F32)] + rd.scratch(),
        compiler_params=_cp("arbitrary"))(dmix, y, proj, xbc_c, proj, proj, hprev, cw, sc, norm_g, *_head_mats(),
                                          *rd.arrays())
    return outs[:7], rd.split(outs[7:])


CONV_RT = 32


def _fill_phases(ext, ph, rows):
    for s in range(1, 8):
        ph[s - 1, pl.ds(0, rows), :] = ext[pl.ds(s, rows), :]


def _window(ext, ph, off, r0, ls):
    s = off % 8
    src = ext if s == 0 else ph.at[s - 1]
    return src[pl.ds(pl.multiple_of(off - s + r0, 8), CONV_RT), ls]


def _cf_fwd(proj, w, b, lg, lb, riders=(), name="cf_fwd", tb_cap=512):
    S = proj.shape[0]
    tb = _tile(S, tb_cap, 8)
    nb = S // tb
    rd = _Riders(riders)
    nco = len(rd.arrays())

    def body(*refs):
        a_ref, g_ref, w_ref, b_ref, lg_ref, lb_ref = refs[:6]
        u1_ref, u_ref = refs[6 + nco:8 + nco]
        ext, ph = refs[8 + 2 * nco:10 + 2 * nco]
        rd.bind(refs[6:6 + nco], refs[8 + nco:8 + 2 * nco], refs[10 + 2 * nco:], nb).start()

        @pl.when(pl.program_id(0) == 0)
        def _():
            ext[pl.ds(0, CF_HALO), :] = jnp.zeros((CF_HALO, D), F32)

        ext[pl.ds(CF_HALO, tb), :] = a_ref[...] * _sigmoid(g_ref[...])
        _fill_phases(ext, ph, tb + CF_HALO - 8)

        def tile(i, carry):
            r0 = pl.multiple_of(i * CONV_RT, CONV_RT)
            for l in range(D // 128):
                ls = pl.ds(l * 128, 128)
                acc = jnp.broadcast_to(b_ref[:, ls], (CONV_RT, 128))
                for k in range(KC):
                    acc = acc + _window(ext, ph, CF_HALO - (KC - 1) + k, r0, ls) * w_ref[k:k + 1, ls]
                u1_ref[pl.ds(r0, CONV_RT), ls] = acc
            return carry

        lax.fori_loop(0, tb // CONV_RT, tile, 0)
        acc = u1_ref[...]
        mu = jnp.mean(acc, axis=-1, keepdims=True)
        xc = acc - mu
        r = lax.rsqrt(jnp.mean(xc * xc, axis=-1, keepdims=True) + EPS)
        u_ref[...] = _silu(xc * r * lg_ref[...] + lb_ref[...]).astype(u_ref.dtype)
        ext[pl.ds(0, CF_HALO), :] = ext[pl.ds(tb, CF_HALO), :]
        rd.finish()

    vec = pl.BlockSpec((1, D), lambda i: (0, 0))
    outs = pl.pallas_call(
        body, name=name, grid=(nb,),
        in_specs=[pl.BlockSpec((tb, D), lambda i: (i, COL_A // D)), pl.BlockSpec((tb, D), lambda i: (i, COL_G // D)),
                  pl.BlockSpec((KC, D), lambda i: (0, 0)), vec, vec, vec] + [HBM_SPEC] * nco,
        out_specs=[pl.BlockSpec((tb, D), lambda i: (i, 0)), pl.BlockSpec((tb, D), lambda i: (i, 0))] + [HBM_SPEC] * nco,
        out_shape=[SDS((S, D), F32), SDS((S, D), _MXU)] + rd.out_shapes(),
        scratch_shapes=[pltpu.VMEM((CF_HALO + tb, D), F32), pltpu.VMEM((7, tb + CF_HALO - 8, D), F32)] + rd.scratch(),
        compiler_params=_cp("arbitrary"))(proj, proj, w, b, lg, lb, *rd.arrays())
    return outs[:2], rd.split(outs[2:])


def _cf_bwd(dmix, u1, proj, w, lg, lb, riders=(), name="cf_bwd", tb_cap=256):
    S = proj.shape[0]
    tb = _tile(S, tb_cap, 8)
    nb = S // tb
    rd = _Riders(riders)
    nco = len(rd.arrays())
    rev = lambda i: nb - 1 - i

    def body(*refs):
        du_ref, u1_ref, a_ref, g_ref, w_ref, lg_ref, lb_ref = refs[:7]
        da_ref, dg_ref, dw_ref, db_ref, dlg_ref, dlb_ref = refs[7 + nco:13 + nco]
        ext, ph, u0s = refs[13 + 2 * nco:16 + 2 * nco]
        rd.bind(refs[7:7 + nco], refs[13 + nco:13 + 2 * nco], refs[16 + 2 * nco:], nb).start()

        @pl.when(pl.program_id(0) == 0)
        def _():
            ext[pl.ds(tb, CF_HALO), :] = jnp.zeros((CF_HALO, D), F32)
            dw_ref[...] = jnp.zeros_like(dw_ref)
            db_ref[...] = jnp.zeros_like(db_ref)
            dlg_ref[...] = jnp.zeros_like(dlg_ref)
            dlb_ref[...] = jnp.zeros_like(dlb_ref)

        u1 = u1_ref[...]
        mu = jnp.mean(u1, axis=-1, keepdims=True)
        xc = u1 - mu
        r = lax.rsqrt(jnp.mean(xc * xc, axis=-1, keepdims=True) + EPS)
        xh = xc * r
        lgv = lg_ref[...]
        du2 = du_ref[...] * _dsilu(xh * lgv + lb_ref[...])
        dlg_ref[...] += jnp.sum(du2 * xh, axis=0, keepdims=True)
        dlb_ref[...] += jnp.sum(du2, axis=0, keepdims=True)
        gd = du2 * lgv
        du1 = r * (gd - jnp.mean(gd, axis=-1, keepdims=True) - xh * jnp.mean(gd * xh, axis=-1, keepdims=True))
        db_ref[...] += jnp.sum(du1, axis=0, keepdims=True)
        ext[pl.ds(0, tb), :] = du1
        u0s[...] = a_ref[...] * _sigmoid(g_ref[...])
        _fill_phases(ext, ph, tb + CF_HALO - 8)

        for l in range(D // 128):
            ls = pl.ds(l * 128, 128)

            def tile(i, accs, ls=ls):
                r0 = pl.multiple_of(i * CONV_RT, CONV_RT)
                rows = pl.ds(r0, CONV_RT)
                u0t = u0s[rows, ls]
                acc = jnp.zeros((CONV_RT, 128), F32)
                out = []
                for k in range(KC):
                    win = _window(ext, ph, KC - 1 - k, r0, ls)
                    acc = acc + win * w_ref[k:k + 1, ls]
                    p = u0t * win
                    out.append(accs[k] + ((p[0:8] + p[8:16]) + (p[16:24] + p[24:32])))
                sg = _sigmoid(g_ref[rows, ls])
                da_ref[rows, ls] = (acc * sg).astype(da_ref.dtype)
                dg_ref[rows, ls] = (acc * a_ref[rows, ls] * sg * (1.0 - sg)).astype(dg_ref.dtype)
                return tuple(out)

            accs = lax.fori_loop(0, tb // CONV_RT, tile, tuple(jnp.zeros((8, 128), F32) for _ in range(KC)))
            for k in range(KC):
                dw_ref[k:k + 1, ls] += jnp.sum(accs[k], axis=0, keepdims=True)
        ext[pl.ds(tb, CF_HALO), :] = ext[pl.ds(0, CF_HALO), :]
        rd.finish()

    vec = pl.BlockSpec((1, D), lambda i: (0, 0))
    wsp = pl.BlockSpec((KC, D), lambda i: (0, 0))
    row = lambda j=0: pl.BlockSpec((tb, D), lambda i: (rev(i), j))
    outs = pl.pallas_call(
        body, name=name, grid=(nb,),
        in_specs=[row(1), row(), row(COL_A // D), row(COL_G // D), wsp, vec, vec] + [HBM_SPEC] * nco,
        out_specs=[row(), row(), wsp, vec, vec, vec] + [HBM_SPEC] * nco,
        out_shape=[SDS((S, D), _MXU), SDS((S, D), _MXU), SDS((KC, D), F32),
                   SDS((1, D), F32), SDS((1, D), F32), SDS((1, D), F32)] + rd.out_shapes(),
        scratch_shapes=[pltpu.VMEM((tb + CF_HALO, D), F32), pltpu.VMEM((7, tb + CF_HALO - 8, D), F32),
                        pltpu.VMEM((tb, D), F32)] + rd.scratch(),
        compiler_params=_cp("arbitrary"))(dmix, u1, proj, proj, w, lg, lb, *rd.arrays())
    return outs[:6], rd.split(outs[6:])


def _attn_fwd(q, kv, name="attn_fwd", tq_cap=512):
    S = q.shape[0]
    tq = _tile(S, tq_cap, 8)
    scale = XD ** -0.5

    def body(q_ref, kv_ref, o_ref):
        for h in range(XH):
            hs = slice(h * XD, (h + 1) * XD)
            s = _dot(q_ref[:, hs], kv_ref[:, hs], "nt") * scale
            s = s - jnp.max(s, axis=-1, keepdims=True)
            p = jnp.exp(s)
            p = p / jnp.sum(p, axis=-1, keepdims=True)
            o_ref[:, hs] = _dot(p, kv_ref[:, D + h * XD: D + (h + 1) * XD]).astype(o_ref.dtype)

    return pl.pallas_call(
        body, name=name, grid=(S // tq,),
        in_specs=[pl.BlockSpec((tq, D), lambda i: (i, 0)), pl.BlockSpec((MEM, 2 * D), lambda i: (0, 0))],
        out_specs=pl.BlockSpec((tq, D), lambda i: (i, 0)), out_shape=SDS((S, D), _MXU),
        compiler_params=_cp("parallel"))(q, kv)


def _attn_bwd(do, q, kv, riders=(), name="attn_bwd", tq_cap=512):
    S = q.shape[0]
    tq = _tile(S, tq_cap, 8)
    scale = XD ** -0.5
    rd = _Riders(riders)
    nco = len(rd.arrays())

    def body(*refs):
        do_ref, q_ref, kv_ref = refs[:3]
        dq_ref, dkv_ref = refs[3 + nco:5 + nco]
        rd.bind(refs[3:3 + nco], refs[5 + nco:5 + 2 * nco], refs[5 + 2 * nco:], S // tq).start()

        @pl.when(pl.program_id(0) == 0)
        def _():
            dkv_ref[...] = jnp.zeros_like(dkv_ref)

        for h in range(XH):
            hs = slice(h * XD, (h + 1) * XD)
            vs = slice(D + h * XD, D + (h + 1) * XD)
            qh = q_ref[:, hs]
            kh = kv_ref[:, hs]
            s = _dot(qh, kh, "nt") * scale
            s = s - jnp.max(s, axis=-1, keepdims=True)
            p = jnp.exp(s)
            p = p / jnp.sum(p, axis=-1, keepdims=True)
            doh = do_ref[:, hs]
            dp = _dot(doh, kv_ref[:, vs], "nt")
            ds = p * (dp - jnp.sum(dp * p, axis=-1, keepdims=True)) * scale
            dq_ref[:, hs] = _dot(ds, kh).astype(dq_ref.dtype)
            dkv_ref[:, hs] += _dot(ds, qh, "tn")
            dkv_ref[:, vs] += _dot(p, doh, "tn")
        rd.finish()

    outs = pl.pallas_call(
        body, name=name, grid=(S // tq,),
        in_specs=[pl.BlockSpec((tq, D), lambda i: (i, 0)), pl.BlockSpec((tq, D), lambda i: (i, 0)),
                  pl.BlockSpec((MEM, 2 * D), lambda i: (0, 0))] + [HBM_SPEC] * nco,
        out_specs=[pl.BlockSpec((tq, D), lambda i: (i, 0)), pl.BlockSpec((MEM, 2 * D), lambda i: (0, 0))]
        + [HBM_SPEC] * nco,
        out_shape=[SDS((S, D), _MXU), SDS((MEM, 2 * D), F32)] + rd.out_shapes(), scratch_shapes=rd.scratch(),
        compiler_params=_cp("arbitrary"))(do, q, kv, *rd.arrays())
    return outs[:2], rd.split(outs[2:])


def _ffn_in(hf, wg_t, wu_t, name="ffn_in", tm_cap=512, tn_cap=1408):
    S, K = hf.shape
    N = wg_t.shape[0]
    tm, tn = _tile(S, tm_cap, 8), _tile(N, tn_cap)

    def body(a_ref, g_ref, u_ref, act_ref, gt_ref, up_ref):
        a = a_ref[...]
        gt = _dot(a, g_ref[...], "nt")
        up = _dot(a, u_ref[...], "nt")
        act_ref[...] = (_silu(gt) * up).astype(act_ref.dtype)
        gt_ref[...] = gt.astype(gt_ref.dtype)
        up_ref[...] = up.astype(up_ref.dtype)

    wsp = pl.BlockSpec((tn, K), lambda j, i: (j, 0))
    osp = pl.BlockSpec((tm, tn), lambda j, i: (i, j))
    return pl.pallas_call(
        body, name=name, grid=(N // tn, S // tm), in_specs=[pl.BlockSpec((tm, K), lambda j, i: (i, 0)), wsp, wsp],
        out_specs=[osp, osp, osp], out_shape=[SDS((S, N), _MXU)] * 3,
        compiler_params=_cp("parallel", "parallel"))(hf, wg_t, wu_t)


def _ffn_out_bwd(dx, w_down, gt, up, name="ffn_out_dx", tm_cap=512, tk_cap=1408):
    S, N = dx.shape
    K = w_down.shape[0]
    tm, tk = _tile(S, tm_cap, 8), _tile(K, tk_cap)

    def body(a_ref, b_ref, g_ref, u_ref, dg_ref, du_ref):
        d = _dot(a_ref[...], b_ref[...], "nt")
        gt = g_ref[...].astype(F32)
        s = _sigmoid(gt)
        dg_ref[...] = (d * u_ref[...].astype(F32) * (s * (1.0 + gt * (1.0 - s)))).astype(dg_ref.dtype)
        du_ref[...] = (d * gt * s).astype(du_ref.dtype)

    osp = pl.BlockSpec((tm, tk), lambda j, i: (i, j))
    return pl.pallas_call(
        body, name=name, grid=(K // tk, S // tm),
        in_specs=[pl.BlockSpec((tm, N), lambda j, i: (i, 0)), pl.BlockSpec((tk, N), lambda j, i: (j, 0)), osp, osp],
        out_specs=[osp, osp], out_shape=[SDS((S, K), _MXU)] * 2,
        compiler_params=_cp("parallel", "parallel"))(dx, w_down, gt, up)


AG_RIDE = (("w_down", "w_o"), ("w_out", "w_q", "w_kv"), ("w_gate", "w_up"))


def _local_step(x, mem, tgt, W, P, core=None, late=None):
    pair, got = {}, {}
    ride = [[late[n] for n in grp] if late is not None else [] for grp in AG_RIDE]

    def halves(group):
        if core is None:
            return []
        gs = [_shard_grad(n, GW) for n in group]
        return [g.reshape(4, 2, g.shape[1] // 2, g.shape[2]) for g in gs]

    def pair_sums(group, hs, theirs):
        ps = [_pair_sum(h_, t, core, "rs_pair_sum_" + n) for h_, t, n in zip(hs, theirs, group)]
        pair.update(zip(group, ps))
        return ps

    h = _rms_fwd(x, P["g_mix"], "rms_mix")
    proj, (bufs0,) = _mm_nn(h, W["main"], "in_proj", tm_cap=256, tn_cap=MAINW, riders=[_Rider("gather", ride[0])])
    (xbc_c, y, yn, hprev), (bufs1,) = _ssd_fwd(proj, P["conv4_w"], P["conv4_b"], P["sc"], P["ssd_norm_g"],
                                                riders=[_Rider("gather", ride[1])])
    (u1, u), (bufs2,) = _cf_fwd(proj, P["cf_w"], P["cf_b"], P["ln_g"], P["ln_b"], riders=[_Rider("gather", ride[2])])
    if late is not None:
        names = AG_RIDE[0] + AG_RIDE[1] + AG_RIDE[2]
        full = _gather_finish_list(ride[0] + ride[1] + ride[2], bufs0 + bufs1 + bufs2)
        W = dict(W, **_pack_late(dict(zip(names, full))))
    mix = jnp.concatenate([yn, u], axis=1)
    x1, hq = _mm_nn(mix, W["out"], "out_proj", add=x, tm_cap=512, tn_cap=D, norm=P["g_xattn"])
    q = _mm_nn(hq, W["q"], "q_proj")
    mn = _rms_fwd(mem, P["g_mem"], "rms_mem")
    kv = _mm_nn(mn, W["kv"], "kv_proj")
    o = _attn_fwd(q, kv)
    x2, hf = _mm_nn(o, W["o"], "o_proj", add=x1, tm_cap=512, tn_cap=D, norm=P["g_ffn"])
    act, gt, up = _ffn_in(hf, W["gate_t"], W["up_t"])
    loss, dx3, dx3b, g_final = _mm_nn(act, W["down"], "ffn_out", add=x2, tm_cap=512, tn_cap=D,
                                      loss=(P["g_final"], tgt))
    GW, GP = {}, {"g_final": g_final}
    GW["down"] = _mm_tn(act, dx3b, "ffn_out_dw", tk_cap=1408, tn_cap=1024)
    dgt, dup = _ffn_out_bwd(dx3b, W["down"], gt, up)
    dhf = _mm_nn(dgt, W["gate_t"], "ffn_gate_dx", tm_cap=512)
    dx2, dx2b, GP["g_ffn"] = _mm_nn(dup, W["up_t"], "ffn_up_dx", add=dhf, tm_cap=512, tn_cap=D,
                                    rms=(x2, P["g_ffn"], dx3, True))
    GW["gate_t"] = _mm_tn(dgt, hf, "ffn_gate_dw", tk_cap=1408, tn_cap=1024)
    GW["up_t"] = _mm_tn(dup, hf, "ffn_up_dw", tk_cap=1408, tn_cap=1024)
    ffn_halves = halves(RS_GROUPS[0])
    do = _mm_nt(dx2b, W["o"], "o_proj_dx")
    GW["o"] = _mm_tn(o, dx2b, "o_proj_dw")
    (dq, dkv), (ffn_theirs,) = _attn_bwd(do, q, kv, riders=[_Rider("pair", ffn_halves)])
    ffn_pieces = pair_sums(RS_GROUPS[0], ffn_halves, ffn_theirs)
    dx1, dx1b, GP["g_xattn"] = _mm_nt(dq, W["q"], "q_proj_dx", tk_cap=D, rms=(x1, P["g_xattn"], dx2, True))
    GW["q"] = _mm_tn(hq, dq, "q_proj_dw")
    dkvb = dkv.astype(_MXU)
    GW["kv"] = _mm_tn(mn, dkvb, "kv_proj_dw", tm_cap=256)
    dmn = _mm_nt(dkvb, W["kv"], "kv_proj_dx")
    GP["g_mem"] = _rms_bwd(mem, P["g_mem"], dmn, None, "rms_mem_bwd")
    dmix = _mm_nt(dx1b, W["out"], "out_proj_dx")
    GW["out"] = _mm_tn(mix, dx1b, "out_proj_dw", tn_cap=1024)
    attn_halves = halves(RS_GROUPS[1])
    (da, dg, GP["cf_w"], GP["cf_b"], GP["ln_g"], GP["ln_b"]), (came, attn_theirs) = _cf_bwd(
        dmix, u1, proj, P["cf_w"], P["ln_g"], P["ln_b"],
        riders=[_Rider("exchange", ffn_pieces), _Rider("pair", attn_halves)])
    got.update(zip(RS_GROUPS[0], came))
    attn_pieces = pair_sums(RS_GROUPS[1], attn_halves, attn_theirs)
    (dz, dxbc, ddtr, GP["conv4_w"], GP["conv4_b"], GP["sc"], GP["ssd_norm_g"]), (came,) = _ssd_bwd(
        dmix, y, proj, xbc_c, hprev, P["conv4_w"], P["sc"], P["ssd_norm_g"],
        riders=[_Rider("exchange", attn_pieces)])
    got.update(zip(RS_GROUPS[1], came))
    dproj = [dz, da, dg, dxbc, ddtr]
    GW["main"] = _mm_tn(h, dproj, "in_proj_dw", tm_cap=512, tk_cap=512, tn_cap=MAINW)
    in_halves = halves(RS_GROUPS[2])
    in_pieces = pair_sums(RS_GROUPS[2], in_halves, _pair_split_list(in_halves, "rs_pair_send_w_in")) if in_halves else []
    (grad_x, GP["g_mix"]), (came,) = _mm_nt(dproj, W["main"], "in_proj_dx", tm_cap=256, tk_cap=D,
                                            riders=[_Rider("exchange", in_pieces)], rms=(x, P["g_mix"], dx1, False))
    got.update(zip(RS_GROUPS[2], came))
    if core is None:
        return loss, grad_x, GW, GP
    return loss, grad_x, GW, GP, pair, got


Z_END, XBC_END, DT_END = NH * HP, NH * HP + XBC, NH * HP + XBC + NH


def _pad_to(a, rows=None, cols=None):
    r = 0 if rows is None else rows - a.shape[0]
    c = 0 if cols is None else cols - a.shape[1]
    return jnp.pad(a, ((0, r), (0, c)))


IN_W = DT_END + 2 * D
W_IN_SEGS = [(0, Z_END, "main", COL_Z), (Z_END, XBC_END, "main", COL_XBC), (XBC_END, DT_END, "main", COL_DT),
             (DT_END, DT_END + D, "main", COL_A), (DT_END + D, IN_W, "main", COL_G)]
BIG = [("w_in", True), ("w_out", False), ("w_q", False), ("w_kv", True), ("w_o", False), ("w_gate", False),
       ("w_up", False), ("w_down", False)]
TRANSPOSED = ("w_gate", "w_up")


def _ref_cols(pieces, a, b):
    cw = IN_W // 4
    out = []
    for j in range(4):
        lo, hi = max(a, j * cw), min(b, (j + 1) * cw)
        if lo < hi:
            out.append(pieces[j][:, lo - j * cw:hi - j * cw])
    return out


def _cat_cols(pieces):
    return jnp.concatenate([pieces[j] for j in range(4)], axis=1)


def _pack_in(w_in):
    dt = _ref_cols(w_in, XBC_END, DT_END)
    pad = jnp.zeros((dt[0].shape[0], MAINW - COL_DT - NH), dt[0].dtype)
    main = jnp.concatenate(_ref_cols(w_in, 0, Z_END) + _ref_cols(w_in, DT_END, IN_W) + _ref_cols(w_in, Z_END, XBC_END)
                           + dt + [pad], axis=1)
    return {"main": main}


def _pack_late(pc):
    rows = lambda n: pc[n].reshape(-1, pc[n].shape[-1])
    return {"out": rows("w_out"), "q": rows("w_q"), "kv": _cat_cols(pc["w_kv"]), "o": rows("w_o"),
            "gate_t": rows("w_gate"), "up_t": rows("w_up"), "down": rows("w_down")}


GW_KEY = {"w_gate": "gate_t", "w_up": "up_t", "w_kv": "kv", "w_out": "out", "w_q": "q", "w_o": "o", "w_down": "down"}
RS_GROUPS = (("w_down", "w_gate", "w_up"), ("w_out", "w_q", "w_kv", "w_o"), ("w_in",))


def _shard_grad(name, GW):
    if name == "w_in":
        cw = IN_W // 4
        pieces = []
        for j in range(4):
            parts = []
            for a, b, src, col in W_IN_SEGS:
                lo, hi = max(a, j * cw), min(b, (j + 1) * cw)
                if lo < hi:
                    parts.append(GW[src][:, col + lo - a:col + hi - a])
            pieces.append(jnp.concatenate(parts, axis=1))
        return jnp.stack(pieces)
    g = GW[GW_KEY[name]]
    if dict(BIG)[name]:
        cw = g.shape[1] // 4
        return jnp.stack([g[:, j * cw:(j + 1) * cw] for j in range(4)])
    return g.reshape(4, g.shape[0] // 4, g.shape[1])


def _stack_sc(dt_bias, a_log, d):
    return _pad_to(jnp.concatenate([dt_bias, a_log, d], axis=0), rows=8, cols=128)


COMM_PARAMS = pltpu.CompilerParams(vmem_limit_bytes=VMEM_LIMIT)


def _dma_sems(*counts):
    return [pltpu.SemaphoreType.DMA((n,)) for n in counts]


def _allgather_list(arrs, name):
    n = len(arrs)
    halved = [a.shape[0] % 16 == 0 for a in arrs]
    oshape = [(4, 2, a.shape[0] // 2, a.shape[1]) if h else (4, 1) + a.shape for a, h in zip(arrs, halved)]

    def body(*refs):
        srcs, outs = refs[:n], refs[n:2 * n]
        ici_send, ici_recv, own_send, own_recv, fwd_send, fwd_recv = refs[2 * n:]
        x, y, c = lax.axis_index("x"), lax.axis_index("y"), lax.axis_index("c")
        me = 2 * x + y
        sib = (x, y, 1 - c)
        peers = _chip_peers(x, y)

        def half(i, h):
            r = arrs[i].shape[0] // 2
            if not halved[i]:
                return srcs[i]
            return srcs[i].at[pl.ds(h * r if isinstance(h, int) else pl.multiple_of(h * r, 8), r)]

        ici, own, fwd = [], [], []
        for i in range(n):
            mine_h = c if halved[i] else 0
            for k, (px, py) in enumerate(peers):
                s = 3 * i + k
                ici.append(_remote(half(i, c), outs[i].at[me, mine_h], ici_send.at[s], ici_recv.at[s], (px, py, c)))
            for h in range(2 if halved[i] else 1):
                s = 2 * i + h
                own.append(_remote(half(i, h), outs[i].at[me, h], own_send.at[s], own_recv.at[s], sib))
        for cp in ici + own:
            cp.start()
        for i in range(n):
            if not halved[i]:
                continue
            for k, (px, py) in enumerate(peers):
                s = 3 * i + k
                got = outs[i].at[2 * px + py, c]
                _remote(half(i, c), got, ici_send.at[s], ici_recv.at[s], (px, py, c)).wait_recv()
                f = _remote(got, got, fwd_send.at[s], fwd_recv.at[s], sib)
                f.start()
                fwd.append(f)
        for i in range(n):
            for k, (px, py) in enumerate(peers):
                s = 3 * i + k
                if halved[i]:
                    _remote(half(i, c), outs[i].at[2 * px + py, 1 - c], fwd_send.at[s], fwd_recv.at[s], sib).wait_recv()
                else:
                    _remote(srcs[i], outs[i].at[2 * px + py, 0], ici_send.at[s], ici_recv.at[s], (px, py, c)).wait_recv()
            for h in range(2 if halved[i] else 1):
                s = 2 * i + h
                _remote(half(i, h), outs[i].at[me, h], own_send.at[s], own_recv.at[s], sib).wait_recv()
        for cp in ici + own + fwd:
            cp.wait_send()

    outs = pl.pallas_call(
        body, name=name, in_specs=[HBM_SPEC] * n, out_specs=[HBM_SPEC] * n,
        out_shape=[SDS(s, a.dtype) for s, a in zip(oshape, arrs)],
        scratch_shapes=_dma_sems(3 * n, 3 * n, 2 * n, 2 * n, 3 * n, 3 * n), compiler_params=COMM_PARAMS)(*arrs)
    return [o.reshape((4,) + a.shape) for o, a in zip(outs, arrs)]


def _pair_split_list(gs, name):
    n = len(gs)

    def body(*refs):
        sends, recvs = _pair_copies(refs[:n], refs[n:2 * n], *refs[2 * n:])
        for cp in sends:
            cp.start()
        for cp in recvs:
            cp.wait_recv()
        for cp in sends:
            cp.wait_send()

    return pl.pallas_call(
        body, name=name, in_specs=[HBM_SPEC] * n, out_specs=[HBM_SPEC] * n,
        out_shape=[SDS((4,) + g.shape[2:], g.dtype) for g in gs],
        scratch_shapes=_dma_sems(4 * n, 4 * n), compiler_params=COMM_PARAMS)(*gs)


def _gather_finish_list(shards, bufs, name="allgather_finish"):
    n = len(shards)

    def body(*refs):
        srcs, outs = refs[:n], refs[2 * n:3 * n]
        own_send, own_recv, fwd_send, fwd_recv = refs[3 * n:]
        x, y, c = lax.axis_index("x"), lax.axis_index("y"), lax.axis_index("c")
        me = 2 * x + y
        sib = (x, y, 1 - c)
        sends, recvs = [], []
        for i in range(n):
            for h in range(2):
                own = _remote(_rows_half(srcs[i], shards[i].shape[0], h), outs[i].at[me, h],
                              own_send.at[2 * i + h], own_recv.at[2 * i + h], sib)
                sends.append(own)
                recvs.append(own)
            for k, (px, py) in enumerate(_chip_peers(x, y)):
                got, s = outs[i].at[2 * px + py, c], 3 * i + k
                sends.append(_remote(got, got, fwd_send.at[s], fwd_recv.at[s], sib))
                recvs.append(_remote(got, outs[i].at[2 * px + py, 1 - c], fwd_send.at[s], fwd_recv.at[s], sib))
        for cp in sends:
            cp.start()
        for cp in recvs:
            cp.wait_recv()
        for cp in sends:
            cp.wait_send()

    outs = pl.pallas_call(
        body, name=name, in_specs=[HBM_SPEC] * (2 * n), out_specs=[HBM_SPEC] * n,
        out_shape=[SDS(b.shape, b.dtype) for b in bufs], input_output_aliases={n + i: i for i in range(n)},
        scratch_shapes=_dma_sems(2 * n, 2 * n, 3 * n, 3 * n), compiler_params=COMM_PARAMS)(*shards, *bufs)
    return [o.reshape((4,) + a.shape) for o, a in zip(outs, shards)]


JOIN_SPLIT = 4


def _pair_join_list(bufs, name="rs_pair_join"):
    n = len(bufs)

    def body(*refs):
        outs = refs[n:2 * n]
        send_sems, recv_sems = refs[2 * n:]
        x, y, c = lax.axis_index("x"), lax.axis_index("y"), lax.axis_index("c")
        sib = (x, y, 1 - c)
        sends, recvs = [], []
        for i in range(n):
            rc = bufs[i].shape[1] // JOIN_SPLIT
            for q in range(JOIN_SPLIT):
                k = JOIN_SPLIT * i + q
                rows = pl.ds(q * rc, rc)
                sends.append(_remote(outs[i].at[c, rows], outs[i].at[c, rows], send_sems.at[k], recv_sems.at[k], sib))
                recvs.append(_remote(outs[i].at[c, rows], outs[i].at[1 - c, rows], send_sems.at[k], recv_sems.at[k], sib))
        for cp in sends:
            cp.start()
        for cp in recvs:
            cp.wait_recv()
        for cp in sends:
            cp.wait_send()

    return pl.pallas_call(
        body, name=name, in_specs=[HBM_SPEC] * n, out_specs=[HBM_SPEC] * n,
        out_shape=[SDS(b.shape, b.dtype) for b in bufs], input_output_aliases={i: i for i in range(n)},
        scratch_shapes=_dma_sems(JOIN_SPLIT * n, JOIN_SPLIT * n), compiler_params=COMM_PARAMS)(*bufs)


def _pair_sum(g, theirs, core, name):
    _, _, r, c = g.shape

    def body(core_ref, g_ref, t_ref, o_ref):
        o_ref[...] = (g_ref[...] + t_ref[...]).astype(o_ref.dtype)

    spec = pltpu.PrefetchScalarGridSpec(
        num_scalar_prefetch=1, grid=(4,),
        in_specs=[pl.BlockSpec((None, None, r, c), lambda j, core_ref: (j, core_ref[0], 0, 0)),
                  pl.BlockSpec((None, r, c), lambda j, core_ref: (j, 0, 0))],
        out_specs=pl.BlockSpec((None, r, c), lambda j, core_ref: (j, 0, 0)))
    return pl.pallas_call(body, name=name, grid_spec=spec, out_shape=SDS((4, r, c), BF16),
                          compiler_params=_cp("parallel"))(core, g, theirs)


def _chip_sum(own, got, where, name):
    _, r, c = own.shape
    tr = r // 2

    def body(w_ref, a_ref, b1_ref, b2_ref, b3_ref, o_ref):
        o_ref[...] = ((a_ref[...].astype(F32) + b1_ref[...].astype(F32)) + b2_ref[...].astype(F32)) + b3_ref[...].astype(F32)

    piece = lambda k: pl.BlockSpec((None, tr, c), lambda i, w_ref: ((w_ref[0] + k) % 4, i, 0))
    spec = pltpu.PrefetchScalarGridSpec(
        num_scalar_prefetch=1, grid=(r // tr,), in_specs=[piece(0), piece(1), piece(2), piece(3)],
        out_specs=pl.BlockSpec((None, tr, c), lambda i, w_ref: (w_ref[1], i, 0)))
    return pl.pallas_call(body, name=name, grid_spec=spec, out_shape=SDS((2, r, c), F32),
                          compiler_params=_cp("parallel"))(where, own, got, got, got)


ADAM_BLOCK = 2 ** 19


def _adam_math(w, g, m, v):
    bc1 = 1.0 - ADAM_B1 ** ADAM_STEP
    bc2 = 1.0 - ADAM_B2 ** ADAM_STEP
    mn = ADAM_B1 * m + (1.0 - ADAM_B1) * g
    vn = ADAM_B2 * v + (1.0 - ADAM_B2) * (g * g)
    return -ADAM_LR * ((mn / bc1) / (jnp.sqrt(vn / bc2) + ADAM_EPS) + ADAM_WD * w), mn, vn


PACK_COLS = XBC
PACK = {"g_mix": (0, 1, D), "g_xattn": (1, 1, D), "g_mem": (2, 1, D), "g_ffn": (3, 1, D), "g_final": (4, 1, D),
        "ssd_norm_g": (5, 1, D), "cf_b": (6, 1, D), "ln_g": (7, 1, D), "ln_b": (8, 1, D), "conv4_b": (9, 1, XBC),
        "conv4_w": (10, KS, XBC), "sc": (16, 8, 128), "cf_w": (24, KC, D), "loss": (55, 1, 128)}
PACK_ROWS = 56
SMALL_ADAM = ["g_mix", "g_xattn", "g_mem", "g_ffn", "g_final", "ssd_norm_g", "cf_b", "ln_g", "ln_b", "conv4_b", "sc"]


def _small_allreduce_adamw(grads, wts, mom, var, name="allreduce_small"):
    gk = list(PACK)
    ng, na = len(gk), len(SMALL_ADAM)

    def body(*refs):
        g_in = refs[:ng]
        w_in, m_in, v_in = (refs[ng + i * na: ng + (i + 1) * na] for i in range(3))
        o = refs[ng + 3 * na:]
        g_out = o[:ng]
        d_out, m_out, v_out = (o[ng + i * na: ng + (i + 1) * na] for i in range(3))
        pack, pbuf, psum, cbuf, acc, send_sems, recv_sems = o[ng + 3 * na:]
        x, y, c = lax.axis_index("x"), lax.axis_index("y"), lax.axis_index("c")
        me = 2 * x + y
        pack[...] = jnp.zeros_like(pack)
        for i, k in enumerate(gk):
            r0, nr, nc = PACK[k]
            pack[r0:r0 + nr, 0:nc] = g_in[i][...]
        pair = _remote(pack, pbuf.at[c], send_sems.at[0], recv_sems.at[0], (x, y, 1 - c))
        pair.start()
        pbuf[c] = pack[...]
        _remote(pack, pbuf.at[1 - c], send_sems.at[0], recv_sems.at[0], (x, y, 1 - c)).wait_recv()
        pair.wait_send()
        psum[...] = pbuf[0] + pbuf[1]
        peers = _chip_peers(x, y)
        sends = [_remote(psum, cbuf.at[me], send_sems.at[1 + k], recv_sems.at[1 + k], (px, py, c))
                 for k, (px, py) in enumerate(peers)]
        for cp in sends:
            cp.start()
        cbuf[me] = psum[...]
        for k, (px, py) in enumerate(peers):
            _remote(psum, cbuf.at[2 * px + py], send_sems.at[1 + k], recv_sems.at[1 + k], (px, py, c)).wait_recv()
        for cp in sends:
            cp.wait_send()
        acc[...] = (cbuf[0] + cbuf[1]) + (cbuf[2] + cbuf[3])
        for i, k in enumerate(gk):
            r0, nr, nc = PACK[k]
            g_out[i][...] = acc[r0:r0 + nr, 0:nc]
        for i, k in enumerate(SMALL_ADAM):
            r0, nr, nc = PACK[k]
            d_out[i][...], m_out[i][...], v_out[i][...] = _adam_math(
                w_in[i][...], acc[r0:r0 + nr, 0:nc], m_in[i][...], v_in[i][...])

    args = [grads[k] for k in gk] + [d[k] for d in (wts, mom, var) for k in SMALL_ADAM]
    shp = lambda k: SDS((PACK[k][1], PACK[k][2]), F32)
    vm = pl.BlockSpec(memory_space=pltpu.VMEM)
    outs = pl.pallas_call(
        body, name=name, in_specs=[vm] * len(args), out_specs=[vm] * (ng + 3 * na),
        out_shape=[shp(k) for k in gk] + [shp(k) for _ in range(3) for k in SMALL_ADAM],
        scratch_shapes=[pltpu.VMEM((PACK_ROWS, PACK_COLS), F32), pltpu.VMEM((2, PACK_ROWS, PACK_COLS), F32),
                        pltpu.VMEM((PACK_ROWS, PACK_COLS), F32), pltpu.VMEM((4, PACK_ROWS, PACK_COLS), F32),
                        pltpu.VMEM((PACK_ROWS, PACK_COLS), F32)] + _dma_sems(4, 4),
        compiler_params=COMM_PARAMS)(*args)
    red = dict(zip(gk, outs[:ng]))
    parts = [dict(zip(SMALL_ADAM, outs[ng + i * na: ng + (i + 1) * na])) for i in range(3)]
    return red, parts[0], parts[1], parts[2]


def _adamw_cols(w, gfull, m, v, chip, name):
    _, R, C = w.shape

    def body(w_idx, w_ref, g_ref, m_ref, v_ref, go_ref, d_ref, mo_ref, vo_ref):
        go_ref[...] = g_ref[...]
        d_ref[...], mo_ref[...], vo_ref[...] = _adam_math(w_ref[...], g_ref[...], m_ref[...], v_ref[...])

    blk = pl.BlockSpec((None, R, C), lambda i, w_idx: (0, 0, 0))
    spec = pltpu.PrefetchScalarGridSpec(
        num_scalar_prefetch=1, grid=(1,),
        in_specs=[blk, pl.BlockSpec((R, C), lambda i, w_idx: (0, w_idx[0])), blk, blk], out_specs=[blk] * 4)
    return pl.pallas_call(body, name=name, grid_spec=spec, out_shape=[SDS((1, R, C), F32)] * 4,
                          compiler_params=_cp("arbitrary"))(chip, w, gfull, m, v)


def _adamw(w, g, m, v, name):
    _, R, C = w.shape
    half = R // 2
    tr = _tile(half, max(8, (ADAM_BLOCK // C) // 8 * 8), 8)
    nh = half // tr

    def body(w_ref, g_ref, m_ref, v_ref, go_ref, d_ref, mo_ref, vo_ref):
        go_ref[...] = g_ref[...]
        d_ref[...], mo_ref[...], vo_ref[...] = _adam_math(w_ref[...], g_ref[...], m_ref[...], v_ref[...])

    blk = pl.BlockSpec((None, tr, C), lambda i: (0, i, 0))
    gblk = pl.BlockSpec((None, tr, C), lambda i: (i // nh, i % nh, 0))
    return pl.pallas_call(body, name=name, grid=(R // tr,), in_specs=[blk, gblk, blk, blk], out_specs=[blk] * 4,
                          out_shape=[SDS((1, R, C), F32)] * 4, compiler_params=_cp("parallel"))(w, g, m, v)


WEIGHT_NAMES = ["norm_mix_g", "w_in", "ssd_conv_w", "ssd_conv_b", "ssd_dt_bias", "ssd_A_log", "ssd_D", "ssd_norm_g",
                "cf_conv_w", "cf_conv_b", "cf_ln_g", "cf_ln_b", "w_out", "norm_xattn_g", "norm_mem_g", "w_q", "w_kv",
                "w_o", "norm_ffn_g", "w_gate", "w_up", "w_down", "norm_final_g"]
VEC_REF = [("norm_mix_g", "g_mix"), ("norm_xattn_g", "g_xattn"), ("norm_mem_g", "g_mem"), ("norm_ffn_g", "g_ffn"),
           ("norm_final_g", "g_final"), ("ssd_norm_g", "ssd_norm_g"), ("cf_conv_b", "cf_b"), ("cf_ln_g", "ln_g"),
           ("cf_ln_b", "ln_b"), ("ssd_conv_b", "conv4_b")]
SC_REF = ["ssd_dt_bias", "ssd_A_log", "ssd_D"]


def _small_side(get):
    d = {k: get(ref_name).reshape(1, -1) for ref_name, k in VEC_REF}
    d["sc"] = _stack_sc(*[get(n) for n in SC_REF])
    return d


def kernel(x, mem, norm_mix_g, w_in, ssd_conv_w, ssd_conv_b, ssd_dt_bias, ssd_A_log, ssd_D, ssd_norm_g, cf_conv_w, cf_conv_b, cf_ln_g, cf_ln_b, w_out, norm_xattn_g, norm_mem_g, w_q, w_kv, w_o, norm_ffn_g, w_gate, w_up, w_down, norm_final_g, loss_target, m_norm_mix_g, m_w_in, m_ssd_conv_w, m_ssd_conv_b, m_ssd_dt_bias, m_ssd_A_log, m_ssd_D, m_ssd_norm_g, m_cf_conv_w, m_cf_conv_b, m_cf_ln_g, m_cf_ln_b, m_w_out, m_norm_xattn_g, m_norm_mem_g, m_w_q, m_w_kv, m_w_o, m_norm_ffn_g, m_w_gate, m_w_up, m_w_down, m_norm_final_g, v_norm_mix_g, v_w_in, v_ssd_conv_w, v_ssd_conv_b, v_ssd_dt_bias, v_ssd_A_log, v_ssd_D, v_ssd_norm_g, v_cf_conv_w, v_cf_conv_b, v_cf_ln_g, v_cf_ln_b, v_w_out, v_norm_xattn_g, v_norm_mem_g, v_w_q, v_w_kv, v_w_o, v_norm_ffn_g, v_w_gate, v_w_up, v_w_down, v_norm_final_g):
    env = dict(locals())
    view = lambda n, a: a.transpose(0, 2, 1) if n in TRANSPOSED else a
    wts = {n: view(n, env[n]) for n in WEIGHT_NAMES}
    mom = {n: view(n, env["m_" + n]) for n in WEIGHT_NAMES}
    var = {n: view(n, env["v_" + n]) for n in WEIGHT_NAMES}
    chip = (2 * lax.axis_index("x") + lax.axis_index("y")).astype(jnp.int32).reshape(1)
    core = lax.axis_index("c").astype(jnp.int32).reshape(1)
    where = jnp.concatenate([chip, core])
    big = [n for n, _ in BIG]

    w_in_g, conv4_g, cf_g = _allgather_list([w_in[0].astype(BF16), ssd_conv_w[0], cf_conv_w[0]], "allgather_first")
    W = _pack_in(w_in_g)
    P = _small_side(lambda n: wts[n])
    P["conv4_w"], P["cf_w"] = _cat_cols(conv4_g), _cat_cols(cf_g)
    late = {n: wts[n][0].astype(BF16) for grp in AG_RIDE for n in grp}

    loss, grad_x, GW, GP, pair, got = _local_step(x[0], mem[0], loss_target[0], W, P, core, late)
    joined = _pair_join_list([_chip_sum(pair[n], got[n], where, "rs_chip_sum_" + n) for n in big])
    gshard = dict(zip(big, joined))

    small = dict(GP)
    small["loss"] = loss
    red, sd, sm, sv = _small_allreduce_adamw(small, {k: P[k] for k in SMALL_ADAM}, _small_side(lambda n: mom[n]),
                                             _small_side(lambda n: var[n]))
    grads, delta, new_m, new_v = {}, {}, {}, {}
    for ref_name, k in VEC_REF:
        shp = wts[ref_name].shape
        for dst, src in ((grads, red), (delta, sd), (new_m, sm), (new_v, sv)):
            dst[ref_name] = src[k].reshape(shp)
    for row, ref_name in enumerate(SC_REF):
        for dst, src in ((grads, red), (delta, sd), (new_m, sm), (new_v, sv)):
            dst[ref_name] = src["sc"][row:row + 1, :NH]

    for n, k in (("ssd_conv_w", "conv4_w"), ("cf_conv_w", "cf_w")):
        grads[n], delta[n], new_m[n], new_v[n] = _adamw_cols(wts[n], red[k], mom[n], var[n], chip, "adamw_" + n)
    for n in big:
        outs = _adamw(wts[n], gshard[n], mom[n], var[n], "adamw_" + n)
        grads[n], delta[n], new_m[n], new_v[n] = [view(n, o) for o in outs]

    return (red["loss"][0, 0], grad_x[None], *[grads[n] for n in WEIGHT_NAMES], *[delta[n] for n in WEIGHT_NAMES],
            *[new_m[n] for n in WEIGHT_NAMES], *[new_v[n] for n in WEIGHT_NAMES])
```
